```python
import jax, jax.numpy as jnp
from jax import lax
import numpy as np

D_MODEL = 2048
BATCH = 4
SEQ = 2048
DEPTH = 1

D_MIX = D_MODEL
ATT_WIDTH = D_MIX // 2
ATT_HEAD_DIM = 64
ATT_HEADS = ATT_WIDTH // ATT_HEAD_DIM
DILATED_PAIRS = ((128, 1), (512, 4), (2048, 16))
ATT_BLOCK = 128
ROT_DIM = ATT_HEAD_DIM // 4
ROPE_THETA = 500000.0
HG_WIDTH = D_MIX - ATT_WIDTH
HG_EXPAND = 128
HG_HEADS = HG_WIDTH // HG_EXPAND
HG_HEAD_K = HG_EXPAND
HG_HEAD_V = HG_WIDTH // HG_HEADS
HG_CHUNK = 64
IN_COLS = 3 * ATT_WIDTH + 4 * HG_WIDTH
N_EXPERTS = 32
TOP_K = 4
D_FF_EXPERT = D_MODEL
SWIGLU_ALPHA = 1.702
SWIGLU_LIMIT = 7.0
MOE_BLOCK = 128
ALPHA_DEEPNORM = (2.0 * DEPTH) ** 0.25
BETA_DEEPNORM = (8.0 * DEPTH) ** -0.25
LN_EPS = 1e-5
RMS_EPS = 1e-6
NEG_INF = -1e30

kernel_name = "hymba_dilated_hgrn2_moe_deepnorm_adaln"


def layer_norm(t, g, b):
    t32 = t.astype(jnp.float32)
    mu = jnp.mean(t32, axis=-1, keepdims=True)
    var = jnp.mean(jnp.square(t32 - mu), axis=-1, keepdims=True)
    return ((t32 - mu) * lax.rsqrt(var + LN_EPS) * g + b).astype(t.dtype)


def partial_rope(t, positions):
    half = ROT_DIM // 2
    inv = ROPE_THETA ** (-(jnp.arange(0, ROT_DIM, 2, dtype=jnp.float32) / ROT_DIM))
    ang = positions.astype(jnp.float32)[..., None] * inv
    cos = jnp.cos(ang)[:, :, None, :]
    sin = jnp.sin(ang)[:, :, None, :]
    x1, x2, rest = t[..., :half], t[..., half:ROT_DIM], t[..., ROT_DIM:]
    return jnp.concatenate([x1 * cos - x2 * sin, x1 * sin + x2 * cos, rest], axis=-1)


def dilated_branch(q, k, v, window, dilation):
    B_, S_, H_, Dh = q.shape
    w_steps = window // dilation
    Q = ATT_BLOCK
    nb = -(-S_ // (dilation * Q))
    S_pad = nb * Q * dilation
    pad = ((0, 0), (0, S_pad - S_), (0, 0), (0, 0))

    def to_streams(t):
        return jnp.pad(t, pad).reshape(B_, nb, Q, dilation, H_, Dh)

    def with_prev(t):
        prev = jnp.pad(t, ((0, 0), (1, 0), (0, 0), (0, 0), (0, 0), (0, 0)))[:, :-1]
        return jnp.concatenate([prev, t], axis=2)

    qs = to_streams(q)
    kb = with_prev(to_streams(k))
    vb = with_prev(to_streams(v))
    s = jnp.einsum('bnqrhd,bnkrhd->bnrhqk', qs, kb) * (Dh ** -0.5)
    qi = jnp.arange(Q)[:, None]
    kj = jnp.arange(2 * Q)[None, :]
    dist = qi + Q - kj
    blk = jnp.arange(nb)[:, None, None]
    valid = (dist >= 0) & (dist <= w_steps) & ((blk > 0) | (kj >= Q))
    s = jnp.where(valid[None, :, None, None], s, NEG_INF)
    m = jnp.max(s, axis=-1, keepdims=True)
    p = jnp.exp(s - m)
    den = jnp.sum(p, axis=-1)
    o = jnp.einsum('bnrhqk,bnkrhd->bnqrhd', p, vb)
    o = o / jnp.moveaxis(den, -1, 2)[..., None]
    lse = jnp.moveaxis(m[..., 0] + jnp.log(den), -1, 2)
    o = o.reshape(B_, S_pad, H_, Dh)[:, :S_]
    lse = lse.reshape(B_, S_pad, H_)[:, :S_]
    return o, lse


def dilated_attention(q, k, v, positions):
    B_, S_, _ = q.shape
    shp = (B_, S_, ATT_HEADS, ATT_HEAD_DIM)
    qh = partial_rope(q.reshape(shp).astype(jnp.float32), positions)
    kh = partial_rope(k.reshape(shp).astype(jnp.float32), positions)
    vh = v.reshape(shp).astype(jnp.float32)
    outs, lses = [], []
    for window, dilation in DILATED_PAIRS:
        o, lse = dilated_branch(qh, kh, vh, window, dilation)
        outs.append(o)
        lses.append(lse)
    wts = jax.nn.softmax(jnp.stack(lses, axis=0), axis=0)
    o = jnp.sum(wts[..., None] * jnp.stack(outs, axis=0), axis=0)
    return o.reshape(B_, S_, ATT_WIDTH)


def hgrn2(q, f_logit, i, gate, lb, gnorm_w):
    B_, S_, _ = q.shape
    nc = S_ // HG_CHUNK
    f = lb + (1.0 - lb) * jax.nn.sigmoid(f_logit.astype(jnp.float32))
    log_f = jnp.log(f)
    k = 1.0 - f

    def chunks(t):
        return t.astype(jnp.float32).reshape(B_, nc, HG_CHUNK, HG_HEADS, -1).transpose(1, 0, 3, 2, 4)

    qc, kc, vc, gc = chunks(q), chunks(k), chunks(i), chunks(log_f)
    causal = jnp.tril(jnp.ones((HG_CHUNK, HG_CHUNK), dtype=bool))[:, :, None]

    def step(state, inp):
        qb, kb, vb, gb = inp
        b = jnp.cumsum(gb, axis=2)
        o_inter = jnp.einsum('bhtk,bhkv->bhtv', qb * jnp.exp(b), state)
        diff = b[:, :, :, None, :] - b[:, :, None, :, :]
        decay = jnp.exp(jnp.where(causal, diff, NEG_INF))
        a = jnp.einsum('bhtk,bhsk,bhtsk->bhts', qb, kb, decay)
        o = o_inter + jnp.einsum('bhts,bhsv->bhtv', a, vb)
        b_last = b[:, :, -1:, :]
        new_state = jnp.exp(b_last[:, :, 0, :])[..., None] * state + jnp.einsum(
            'bhsk,bhsv->bhkv', kb * jnp.exp(b_last - b), vb)
        return new_state, o

    s0 = jnp.zeros((B_, HG_HEADS, HG_HEAD_K, HG_HEAD_V), jnp.float32)
    _, o = lax.scan(step, s0, (qc, kc, vc, gc))
    o = o.transpose(1, 0, 3, 2, 4).reshape(B_, S_, HG_HEADS, HG_HEAD_V)
    o = o * lax.rsqrt(jnp.mean(jnp.square(o), axis=-1, keepdims=True) + RMS_EPS)
    o = o * gnorm_w.astype(jnp.float32).reshape(HG_HEADS, HG_HEAD_V)
    o = o * jax.nn.silu(gate.astype(jnp.float32)).reshape(B_, S_, HG_HEADS, HG_HEAD_V)
    return o.reshape(B_, S_, HG_WIDTH)


def hybrid_mixer(h, positions, w_in, lb, gnorm_w, w_o):
    proj = h @ w_in
    A, G = ATT_WIDTH, HG_WIDTH
    q_a, k_a, v_a, q_r, f_r, i_r, g_r = jnp.split(
        proj, [A, 2 * A, 3 * A, 3 * A + G, 3 * A + 2 * G, 3 * A + 3 * G], axis=-1)
    att = dilated_attention(q_a, k_a, v_a, positions)
    rec = hgrn2(q_r, f_r, i_r, g_r, lb, gnorm_w)
    mixed = jnp.concatenate([att, rec], axis=-1).astype(h.dtype)
    return mixed @ w_o


def moe_ffn(h, router_w, router_b, w1, b1, w2, b2):
    B_, S_, D_ = h.shape
    T = B_ * S_
    hf = h.reshape(T, D_)
    logits = hf.astype(jnp.float32) @ router_w.astype(jnp.float32) + router_b.astype(jnp.float32)
    top_val, top_idx = lax.top_k(logits, TOP_K)
    gates = jax.nn.softmax(top_val, axis=-1)
    n_assign = T * TOP_K
    flat_e = top_idx.reshape(n_assign).astype(jnp.int32)
    flat_tok = jnp.arange(n_assign, dtype=jnp.int32) // TOP_K
    order = jnp.argsort(flat_e)
    sorted_e = flat_e[order]
    sorted_tok = flat_tok[order]
    sorted_gate = gates.reshape(n_assign)[order]
    counts = jnp.zeros((N_EXPERTS,), jnp.int32).at[flat_e].add(1)
    starts = jnp.cumsum(counts) - counts
    padded = (counts + MOE_BLOCK - 1) // MOE_BLOCK * MOE_BLOCK
    pends = jnp.cumsum(padded)
    pstarts = pends - padded
    dest = pstarts[sorted_e] + jnp.arange(n_assign, dtype=jnp.int32) - starts[sorted_e]
    n_rows = -(-n_assign // MOE_BLOCK) * MOE_BLOCK + N_EXPERTS * MOE_BLOCK
    n_rblk = n_rows // MOE_BLOCK
    row_tok = jnp.zeros((n_rows,), jnp.int32).at[dest].set(sorted_tok)
    blk_start = jnp.arange(n_rblk, dtype=jnp.int32) * MOE_BLOCK
    blk_e = jnp.minimum(jnp.searchsorted(pends, blk_start, side='right'), N_EXPERTS - 1)
    xin = hf[row_tok].reshape(n_rblk, MOE_BLOCK, D_)

    def expert_block(args):
        xb, e = args
        hb = (xb @ w1[e] + b1[e]).astype(jnp.float32)
        x_glu = jnp.minimum(hb[:, 0::2], SWIGLU_LIMIT)
        x_lin = jnp.clip(hb[:, 1::2], -SWIGLU_LIMIT, SWIGLU_LIMIT)
        act = x_glu * jax.nn.sigmoid(SWIGLU_ALPHA * x_glu) * (x_lin + 1.0)
        return act.astype(xb.dtype) @ w2[e] + b2[e]

    y = lax.map(expert_block, (xin, blk_e)).reshape(n_rows, D_)
    contrib = y[dest].astype(jnp.float32) * sorted_gate[:, None]
    out = jax.ops.segment_sum(contrib, sorted_tok, num_segments=T)
    return out.reshape(B_, S_, D_).astype(h.dtype)


def setup_inputs(seed: int = 0) -> dict:
    key = jax.random.key(seed)
    ks = jax.random.split(key, 20)
    nrm = jax.random.normal
    d_in = D_MODEL ** -0.5
    return {
        "x": nrm(ks[0], (BATCH, SEQ, D_MODEL)),
        "c": nrm(ks[1], (BATCH, D_MODEL)),
        "positions": jnp.broadcast_to(jnp.arange(SEQ, dtype=jnp.int32), (BATCH, SEQ)),
        "w_ada": nrm(ks[2], (DEPTH, D_MODEL, 6 * D_MODEL)) * (0.2 * d_in),
        "b_ada": nrm(ks[3], (DEPTH, 6 * D_MODEL)) * 0.02,
        "w_in": nrm(ks[4], (DEPTH, D_MODEL, IN_COLS)) * d_in,
        "hgrn_lb": 1.0 + 0.1 * nrm(ks[5], (DEPTH + 1, HG_WIDTH)),
        "gnorm_w": 1.0 + 0.1 * nrm(ks[6], (DEPTH, HG_WIDTH)),
        "w_o": nrm(ks[7], (DEPTH, D_MIX, D_MODEL)) * (D_MIX ** -0.5 * BETA_DEEPNORM),
        "ln1_g": 1.0 + 0.1 * nrm(ks[8], (DEPTH, D_MODEL)),
        "ln1_b": 0.02 * nrm(ks[9], (DEPTH, D_MODEL)),
        "router_w": nrm(ks[10], (DEPTH, D_MODEL, N_EXPERTS)) * d_in,
        "router_b": 0.01 * nrm(ks[11], (DEPTH, N_EXPERTS)),
        "w1": nrm(ks[12], (DEPTH, N_EXPERTS, D_MODEL, 2 * D_FF_EXPERT)) * d_in,
        "b1": 0.02 * nrm(ks[13], (DEPTH, N_EXPERTS, 2 * D_FF_EXPERT)),
        "w2": nrm(ks[14], (DEPTH, N_EXPERTS, D_FF_EXPERT, D_MODEL)) * (D_FF_EXPERT ** -0.5 * BETA_DEEPNORM),
        "b2": 0.02 * nrm(ks[15], (DEPTH, N_EXPERTS, D_MODEL)),
        "ln2_g": 1.0 + 0.1 * nrm(ks[16], (DEPTH, D_MODEL)),
        "ln2_b": 0.02 * nrm(ks[17], (DEPTH, D_MODEL)),
    }


def reference(x, c, positions, w_ada, b_ada, w_in, hgrn_lb, gnorm_w, w_o, ln1_g, ln1_b,
              router_w, router_b, w1, b1, w2, b2, ln2_g, ln2_b):
    lb_all = jnp.cumsum(jax.nn.softmax(hgrn_lb.astype(jnp.float32), axis=0), axis=0)
    for l in range(DEPTH):
        mod = jax.nn.silu(c) @ w_ada[l] + b_ada[l]
        sh_a, sc_a, gt_a, sh_f, sc_f, gt_f = [m[:, None, :] for m in jnp.split(mod, 6, axis=-1)]
        h = x * (1.0 + sc_a) + sh_a
        mix = hybrid_mixer(h, positions, w_in[l], lb_all[l], gnorm_w[l], w_o[l])
        x = layer_norm(ALPHA_DEEPNORM * x + (1.0 + gt_a) * mix, ln1_g[l], ln1_b[l])
        h = x * (1.0 + sc_f) + sh_f
        ff = moe_ffn(h, router_w[l], router_b[l], w1[l], b1[l], w2[l], b2[l])
        x = layer_norm(ALPHA_DEEPNORM * x + (1.0 + gt_f) * ff, ln2_g[l], ln2_b[l])
    return x
```

```python
import functools

import numpy as np
import jax
import jax.numpy as jnp
from jax import lax
from jax.experimental import pallas as pl
from jax.experimental.pallas import tpu as pltpu

F32 = jnp.float32
BF16 = jnp.bfloat16

V7X_LANES = 128
V7X_MXU_DIM = 256
V7X_VMEM_LIMIT = 56 * 1024 * 1024

ATT_HEAD_DIM = 64
DILATED_PAIRS = ((128, 1), (512, 4), (2048, 16))
ATT_BLOCK = 128
ROT_DIM = ATT_HEAD_DIM // 4
ROPE_THETA = 500000.0
HG_EXPAND = 128
HG_CHUNK = 64
HG_SUB = 16
TOP_K = 4
SWIGLU_ALPHA = 1.702
SWIGLU_LIMIT = 7.0
LN_EPS = 1e-5
RMS_EPS = 1e-6
NEG_INF = -1e30

NT_DIMS = (((1,), (1,)), ((), ()))


def _pick(n, candidates):
    for c in candidates:
        if n % c == 0:
            return c
    raise ValueError(f"no tile in {candidates} divides {n}")


def _cparams(sem, vmem=V7X_VMEM_LIMIT):
    return pltpu.CompilerParams(dimension_semantics=sem, vmem_limit_bytes=vmem)


def _adaln_kernel(c_ref, w_ref, b_ref, o_ref):
    c = c_ref[...]
    s = c * jax.nn.sigmoid(c)
    o_ref[...] = jnp.dot(s.astype(BF16), w_ref[...].astype(BF16),
                         preferred_element_type=F32) + b_ref[...]


def _adaln(c, w, b):
    bsz, d = c.shape
    n = w.shape[1]
    rows = 8
    cp = jnp.zeros((rows, d), F32).at[:bsz].set(c)
    tn = _pick(n, (1024, 512, 256, 128))
    out = pl.pallas_call(
        _adaln_kernel,
        grid=(n // tn,),
        in_specs=[pl.BlockSpec((rows, d), lambda j: (0, 0)),
                  pl.BlockSpec((d, tn), lambda j: (0, j)),
                  pl.BlockSpec((1, tn), lambda j: (0, j))],
        out_specs=pl.BlockSpec((rows, tn), lambda j: (0, j)),
        out_shape=jax.ShapeDtypeStruct((rows, n), F32),
        compiler_params=_cparams(("arbitrary",)),
        name="adaln",
    )(cp, w, b.reshape(1, n))
    return out[:bsz]


def _in_proj_kernel(x_ref, sc_ref, sh_ref, w_ref, o_ref, wbf_ref):
    @pl.when(pl.program_id(1) == 0)
    def _():
        wbf_ref[...] = w_ref[...].astype(BF16)

    h = x_ref[...] * (1.0 + sc_ref[0]) + sh_ref[0]
    o_ref[...] = jnp.dot(h.astype(BF16), wbf_ref[...], preferred_element_type=F32)


def _in_proj(x2d, sc, sh, w, seq):
    t, d = x2d.shape
    nc = w.shape[1]
    tm = _pick(seq, (512, 256, 128))
    tn = _pick(nc, (1024, 512, 256, 128))
    per_b = seq // tm
    vec = pl.BlockSpec((1, 1, d), lambda j, i: (i // per_b, 0, 0))
    return pl.pallas_call(
        _in_proj_kernel,
        grid=(nc // tn, t // tm),
        in_specs=[pl.BlockSpec((tm, d), lambda j, i: (i, 0)), vec, vec,
                  pl.BlockSpec((d, tn), lambda j, i: (0, j))],
        out_specs=pl.BlockSpec((tm, tn), lambda j, i: (i, j)),
        out_shape=jax.ShapeDtypeStruct((t, nc), F32),
        scratch_shapes=[pltpu.VMEM((d, tn), BF16)],
        compiler_params=_cparams(("arbitrary", "arbitrary")),
        name="in_proj",
    )(x2d, sc[:, None, :], sh[:, None, :], w)


def _rope_kernel(q_ref, k_ref, pos_ref, inv_ref, qo_ref, ko_ref, cos_ref, sin_ref):
    lane = lax.broadcasted_iota(jnp.int32, cos_ref.shape, 1)
    lh = lane % ATT_HEAD_DIM
    half = ROT_DIM // 2

    @pl.when(pl.program_id(1) == 0)
    def _():
        ang = pos_ref[...] * inv_ref[...]
        cos_ref[...] = jnp.where(lh < ROT_DIM, jnp.cos(ang), 1.0)
        sn = jnp.sin(ang)
        sin_ref[...] = jnp.where(lh < half, -sn, jnp.where(lh < ROT_DIM, sn, 0.0))

    cs = cos_ref[...]
    sn = sin_ref[...]

    def rope(t):
        swapped = jnp.where(lh < half,
                            pltpu.roll(t, V7X_LANES - half, axis=1),
                            pltpu.roll(t, half, axis=1))
        return t * cs + swapped * sn

    qo_ref[0, 0] = rope(q_ref[...]) * (ATT_HEAD_DIM ** -0.5)
    ko_ref[0, 0] = rope(k_ref[...])


def _qk_rope(proj, pos_b, inv_lane, bsz, seq, att_w):
    t = proj.shape[0]
    hp = att_w // V7X_LANES
    tm = _pick(seq, (512, 256, 128))
    per_b = seq // tm
    out_spec = pl.BlockSpec((1, 1, tm, V7X_LANES),
                            lambda i, h: (i // per_b, h, i % per_b, 0))
    shp = jax.ShapeDtypeStruct((bsz, hp, seq, V7X_LANES), F32)
    return pl.pallas_call(
        _rope_kernel,
        grid=(t // tm, hp),
        in_specs=[pl.BlockSpec((tm, V7X_LANES), lambda i, h: (i, h)),
                  pl.BlockSpec((tm, V7X_LANES), lambda i, h: (i, hp + h)),
                  pl.BlockSpec((tm, V7X_LANES), lambda i, h: (i, 0)),
                  pl.BlockSpec((1, V7X_LANES), lambda i, h: (0, 0))],
        out_specs=[out_spec, out_spec],
        out_shape=[shp, shp],
        scratch_shapes=[pltpu.VMEM((tm, V7X_LANES), F32), pltpu.VMEM((tm, V7X_LANES), F32)],
        compiler_params=_cparams(("arbitrary", "arbitrary")),
        name="qk_rope",
    )(proj, proj, pos_b, inv_lane)


def _attn_block(q_blk, k_cat, v_cat, first):
    qn = q_blk.shape[0]
    nk = k_cat.shape[0]
    qi = lax.broadcasted_iota(jnp.int32, (qn, nk), 0)
    kj = lax.broadcasted_iota(jnp.int32, (qn, nk), 1)
    dist = qi - kj if first else qi + qn - kj
    w_steps = qn
    valid = (dist >= 0) & (dist <= w_steps)
    lane = lax.broadcasted_iota(jnp.int32, (nk, V7X_LANES), 1)
    qb = q_blk.astype(BF16)
    vb = v_cat.astype(BF16)
    outs, lses = [], []
    for head in range(V7X_LANES // ATT_HEAD_DIM):
        in_head = (lane // ATT_HEAD_DIM) == head
        kh = jnp.where(in_head, k_cat, 0.0).astype(BF16)
        s = lax.dot_general(qb, kh, NT_DIMS, preferred_element_type=F32)
        s = jnp.where(valid, s, NEG_INF)
        m = jnp.max(s, axis=-1, keepdims=True)
        p = jnp.exp(s - m)
        den = jnp.sum(p, axis=-1, keepdims=True)
        o = jnp.dot(p.astype(BF16), vb, preferred_element_type=F32) / den
        outs.append(o)
        lses.append(m + jnp.log(den))
    olane = lax.broadcasted_iota(jnp.int32, (qn, V7X_LANES), 1)
    o = outs[-1]
    lse = jnp.broadcast_to(lses[-1], (qn, V7X_LANES))
    for head in range(len(outs) - 2, -1, -1):
        sel = (olane // ATT_HEAD_DIM) == head
        o = jnp.where(sel, outs[head], o)
        lse = jnp.where(sel, lses[head], lse)
    return o, lse


def _attn_kernel(q_ref, k_ref, v_ref, out_ref, o_scr, l_scr, *, seq):
    qn = ATT_BLOCK

    def rows(start, n, d):
        return pl.ds(start, n) if d == 1 else pl.ds(start, n, stride=d)

    for bi, (window, d) in enumerate(DILATED_PAIRS):
        nb = seq // (d * qn)

        def first_block(r, carry, d=d, bi=bi):
            sl = rows(r, qn, d)
            o, lse = _attn_block(q_ref[0, 0, sl, :], k_ref[0, 0, sl, :], v_ref[0, sl, :], True)
            o_scr[bi, sl, :] = o
            l_scr[bi, sl, :] = lse
            return carry

        def later_block(idx, carry, d=d, bi=bi, nb=nb):
            r = idx // (nb - 1)
            n = idx % (nb - 1) + 1
            qsl = rows(r + n * qn * d, qn, d)
            ksl = rows(r + (n - 1) * qn * d, 2 * qn, d)
            o, lse = _attn_block(q_ref[0, 0, qsl, :], k_ref[0, 0, ksl, :], v_ref[0, ksl, :], False)
            o_scr[bi, qsl, :] = o
            l_scr[bi, qsl, :] = lse
            return carry

        lax.fori_loop(0, d, first_block, 0)
        if nb > 1:
            lax.fori_loop(0, d * (nb - 1), later_block, 0)

    def combine(i, carry):
        sl = pl.ds(pl.multiple_of(i * qn, qn), qn)
        ls = [l_scr[bi, sl, :] for bi in range(len(DILATED_PAIRS))]
        mx = functools.reduce(jnp.maximum, ls)
        ws = [jnp.exp(l - mx) for l in ls]
        tot = functools.reduce(lambda a, b: a + b, ws)
        acc = functools.reduce(lambda a, b: a + b,
                               [w * o_scr[bi, sl, :] for bi, w in enumerate(ws)])
        out_ref[0, sl, :] = (acc / tot).astype(out_ref.dtype)
        return carry

    lax.fori_loop(0, seq // qn, combine, 0)


def _attention(q_hp, k_hp, proj3, att_w):
    bsz, hp, seq, _ = q_hp.shape
    for window, d in DILATED_PAIRS:
        assert window // d == ATT_BLOCK and seq % (d * ATT_BLOCK) == 0
    v_blk0 = 2 * att_w // V7X_LANES
    nbr = len(DILATED_PAIRS)
    qk_spec = pl.BlockSpec((1, 1, seq, V7X_LANES), lambda b, h: (b, h, 0, 0))
    return pl.pallas_call(
        functools.partial(_attn_kernel, seq=seq),
        grid=(bsz, hp),
        in_specs=[qk_spec, qk_spec,
                  pl.BlockSpec((1, seq, V7X_LANES), lambda b, h: (b, 0, v_blk0 + h))],
        out_specs=pl.BlockSpec((1, seq, V7X_LANES), lambda b, h: (b, 0, h)),
        out_shape=jax.ShapeDtypeStruct((bsz, seq, att_w), BF16),
        scratch_shapes=[pltpu.VMEM((nbr, seq, V7X_LANES), F32),
                        pltpu.VMEM((nbr, seq, V7X_LANES), F32)],
        compiler_params=_cparams(("arbitrary", "arbitrary")),
        name="dilated_attn",
    )(q_hp, k_hp, proj3)


def _hgrn_kernel(q_ref, f_ref, i_ref, g_ref, lb_ref, gw_ref, tri_ref, o_ref, st_ref, *, heads, ts):
    @pl.when(pl.program_id(2) == 0)
    def _():
        st_ref[...] = jnp.zeros_like(st_ref)

    c_len = HG_CHUNK
    tri = tri_ref[...]

    def chunk(ci, carry):
        r0 = pl.multiple_of(ci * c_len, c_len)
        rsl = pl.ds(r0, c_len)
        for h in range(heads):
            lsl = slice(h * HG_EXPAND, (h + 1) * HG_EXPAND)
            lb = lb_ref[:, lsl]
            f = lb + (1.0 - lb) * jax.nn.sigmoid(f_ref[0, rsl, lsl])
            logf = jnp.log(f)
            kk = 1.0 - f
            q = q_ref[0, rsl, lsl]
            v = i_ref[0, rsl, lsl]
            b = jnp.dot(tri, logf, precision=lax.Precision.HIGHEST, preferred_element_type=F32)
            b_last = b[c_len - 1:c_len, :]
            st = st_ref[h]
            o_inter = lax.dot_general((q * jnp.exp(b)).astype(BF16), st.astype(BF16), NT_DIMS,
                                      preferred_element_type=F32)
            vb = v.astype(BF16)
            o_rows = []
            for i in range(c_len // HG_SUB):
                lo, hi = i * HG_SUB, (i + 1) * HG_SUB
                mid = lo + HG_SUB // 2
                anchor = b[mid:mid + 1, :]
                qt = q[lo:hi] * jnp.exp(b[lo:hi] - anchor)
                kt = kk[:hi] * jnp.exp(anchor - b[:hi])
                a = lax.dot_general(qt.astype(BF16), kt.astype(BF16), NT_DIMS,
                                    preferred_element_type=F32)
                ti = lax.broadcasted_iota(jnp.int32, (HG_SUB, hi), 0) + lo
                si = lax.broadcasted_iota(jnp.int32, (HG_SUB, hi), 1)
                a = jnp.where(si <= ti, a, 0.0)
                o_rows.append(jnp.dot(a.astype(BF16), vb[:hi], preferred_element_type=F32))
            o = o_inter + jnp.concatenate(o_rows, axis=0)
            kl = kk * jnp.exp(b_last - b)
            upd = jnp.dot(v.T.astype(BF16), kl.astype(BF16), preferred_element_type=F32)
            st_ref[h] = st * jnp.exp(b_last) + upd
            o = o * lax.rsqrt(jnp.mean(o * o, axis=-1, keepdims=True) + RMS_EPS)
            g = g_ref[0, rsl, lsl]
            o = o * gw_ref[:, lsl] * (g * jax.nn.sigmoid(g))
            o_ref[0, rsl, lsl] = o.astype(o_ref.dtype)
        return carry

    lax.fori_loop(0, ts // c_len, chunk, 0)


def _hgrn2(proj3, lb, gw, att_w, hg_w):
    bsz, seq, _ = proj3.shape
    n_heads = hg_w // HG_EXPAND
    heads = _pick(n_heads, (4, 2, 1))
    lanes = heads * HG_EXPAND
    ts = _pick(seq, (256, 128, 64))
    base = 3 * att_w
    assert base % lanes == 0 and hg_w % lanes == 0

    def seg(k):
        off = (base + k * hg_w) // lanes
        return pl.BlockSpec((1, ts, lanes), lambda b, h, s: (b, s, off + h))

    vec = pl.BlockSpec((1, lanes), lambda b, h, s: (0, h))
    tri = jnp.asarray(np.tril(np.ones((HG_CHUNK, HG_CHUNK), np.float32)))
    return pl.pallas_call(
        functools.partial(_hgrn_kernel, heads=heads, ts=ts),
        grid=(bsz, hg_w // lanes, seq // ts),
        in_specs=[seg(0), seg(1), seg(2), seg(3), vec, vec,
                  pl.BlockSpec((HG_CHUNK, HG_CHUNK), lambda b, h, s: (0, 0))],
        out_specs=pl.BlockSpec((1, ts, lanes), lambda b, h, s: (b, s, h)),
        out_shape=jax.ShapeDtypeStruct((bsz, seq, hg_w), BF16),
        scratch_shapes=[pltpu.VMEM((heads, HG_EXPAND, HG_EXPAND), F32)],
        compiler_params=_cparams(("arbitrary", "arbitrary", "arbitrary")),
        name="hgrn2",
    )(proj3, proj3, proj3, proj3, lb.reshape(1, hg_w), gw.reshape(1, hg_w), tri)


def _layer_norm(y, g, b):
    mu = jnp.mean(y, axis=-1, keepdims=True)
    yc = y - mu
    var = jnp.mean(yc * yc, axis=-1, keepdims=True)
    return yc * lax.rsqrt(var + LN_EPS) * g + b


def _bf16_split(x):
    hi = x.astype(BF16)
    lo = (x - hi.astype(F32)).astype(BF16)
    return hi, lo


def _out_proj_kernel(att_ref, rec_ref, x_ref, wo_ref, gt_ref, sc_ref, sh_ref, g_ref, b_ref,
                     rw_ref, rb_ref, x1_ref, h2_ref, lg_ref, wobf_ref, rwhi_ref, rwlo_ref,
                     *, alpha, att_w):
    @pl.when(pl.program_id(0) == 0)
    def _():
        wobf_ref[...] = wo_ref[...].astype(BF16)
        hi, lo = _bf16_split(rw_ref[...])
        rwhi_ref[...] = hi
        rwlo_ref[...] = lo

    mix = (jnp.dot(att_ref[...], wobf_ref[:att_w, :], preferred_element_type=F32)
           + jnp.dot(rec_ref[...], wobf_ref[att_w:, :], preferred_element_type=F32))
    y = alpha * x_ref[...] + (1.0 + gt_ref[0]) * mix
    x1 = _layer_norm(y, g_ref[...], b_ref[...])
    x1_ref[...] = x1
    h2 = x1 * (1.0 + sc_ref[0]) + sh_ref[0]
    hi, lo = _bf16_split(h2)
    lg = (lax.dot_general(rwhi_ref[...], hi, NT_DIMS, preferred_element_type=F32)
          + lax.dot_general(rwhi_ref[...], lo, NT_DIMS, preferred_element_type=F32)
          + lax.dot_general(rwlo_ref[...], hi, NT_DIMS, preferred_element_type=F32))
    lg_ref[...] = lg + rb_ref[...]
    half = h2.shape[1] // 2
    bits = lax.bitcast_convert_type(hi.astype(F32), jnp.uint32)
    h2_ref[...] = (bits[:, :half] >> 16) | (bits[:, half:] & jnp.uint32(0xFFFF0000))


def _out_proj(att, rec, x2d, wo, gt, sc, sh, g, b, rw_t, rb, seq, alpha):
    t, d = x2d.shape
    att_w = att.shape[1]
    n_e = rw_t.shape[0]
    tm = _pick(seq, (256, 128))
    per_b = seq // tm
    vec3 = pl.BlockSpec((1, 1, d), lambda i: (i // per_b, 0, 0))
    full = lambda shape: pl.BlockSpec(shape, lambda i: (0,) * len(shape))
    once = lambda shape: pl.BlockSpec(shape, lambda i: (0,) * len(shape),
                                      pipeline_mode=pl.Buffered(1))
    return pl.pallas_call(
        functools.partial(_out_proj_kernel, alpha=alpha, att_w=att_w),
        grid=(t // tm,),
        in_specs=[pl.BlockSpec((tm, att_w), lambda i: (i, 0)),
                  pl.BlockSpec((tm, d - att_w), lambda i: (i, 0)),
                  pl.BlockSpec((tm, d), lambda i: (i, 0)),
                  once((d, d)), vec3, vec3, vec3, full((1, d)), full((1, d)),
                  full((n_e, d)), full((n_e, 1))],
        out_specs=[pl.BlockSpec((tm, d), lambda i: (i, 0)),
                   pl.BlockSpec((tm, d // 2), lambda i: (i, 0)),
                   pl.BlockSpec((n_e, tm), lambda i: (0, i))],
        out_shape=[jax.ShapeDtypeStruct((t, d), F32),
                   jax.ShapeDtypeStruct((t, d // 2), jnp.uint32),
                   jax.ShapeDtypeStruct((n_e, t), F32)],
        scratch_shapes=[pltpu.VMEM((d, d), BF16), pltpu.VMEM((n_e, d), BF16),
                        pltpu.VMEM((n_e, d), BF16)],
        compiler_params=_cparams(("arbitrary",)),
        name="out_proj_ln1",
    )(att, rec, x2d, wo, gt[:, None, :], sc[:, None, :], sh[:, None, :],
      g.reshape(1, d), b.reshape(1, d), rw_t, rb.reshape(n_e, 1))


def _routing_kernel(lg_ref, tri_ref, idx_ref, gate_ref, rank_ref, cnt_ref, sel_scr, rk_scr, *, blk):
    n_e, t = lg_ref.shape
    eidx = lax.broadcasted_iota(jnp.int32, (n_e, t), 0)
    cur = lg_ref[...]
    vals, idxs = [], []
    for _ in range(TOP_K):
        m = jnp.max(cur, axis=0, keepdims=True)
        ik = jnp.min(jnp.where(cur == m, eidx, n_e), axis=0, keepdims=True)
        cur = jnp.where(eidx == ik, -jnp.inf, cur)
        vals.append(m)
        idxs.append(ik)
    es = [jnp.exp(v - vals[0]) for v in vals]
    den = functools.reduce(lambda a, b: a + b, es)
    for k in range(TOP_K):
        idx_ref[k:k + 1, :] = idxs[k]
        gate_ref[k:k + 1, :] = es[k] / den
    sel = functools.reduce(lambda a, b: a | b, [eidx == ik for ik in idxs])
    sel_scr[...] = jnp.where(sel, 1.0, 0.0)

    tri = tri_ref[...]
    carry = jnp.zeros((n_e, 1), F32)
    for j in range(t // blk):
        sb = sel_scr[:, j * blk:(j + 1) * blk]
        pre = jnp.dot(sb.astype(BF16), tri, preferred_element_type=F32)
        rk_scr[:, j * blk:(j + 1) * blk] = pre + carry
        carry = carry + jnp.sum(sb, axis=1, keepdims=True)
    cnt_ref[...] = jnp.broadcast_to(carry, cnt_ref.shape).astype(jnp.int32)
    rk = rk_scr[...]
    for k in range(TOP_K):
        rank_ref[k:k + 1, :] = jnp.sum(jnp.where(eidx == idxs[k], rk, 0.0), axis=0,
                                       keepdims=True).astype(jnp.int32)


def _routing(logits_t):
    n_e, t = logits_t.shape
    blk = _pick(t, (256, 128))
    tri = jnp.asarray(np.triu(np.ones((blk, blk), np.float32), 1), BF16)
    full = lambda shape: pl.BlockSpec(shape, lambda: (0,) * len(shape))
    return pl.pallas_call(
        functools.partial(_routing_kernel, blk=blk),
        in_specs=[full((n_e, t)), full((blk, blk))],
        out_specs=[full((TOP_K, t)), full((TOP_K, t)), full((TOP_K, t)), full((n_e, V7X_LANES))],
        out_shape=[jax.ShapeDtypeStruct((TOP_K, t), jnp.int32),
                   jax.ShapeDtypeStruct((TOP_K, t), F32),
                   jax.ShapeDtypeStruct((TOP_K, t), jnp.int32),
                   jax.ShapeDtypeStruct((n_e, V7X_LANES), jnp.int32)],
        scratch_shapes=[pltpu.VMEM((n_e, t), F32), pltpu.VMEM((n_e, t), F32)],
        compiler_params=pltpu.CompilerParams(vmem_limit_bytes=V7X_VMEM_LIMIT),
        name="routing",
    )(logits_t, tri)


DISPATCH_CHUNK = 256


def _dispatch_kernel(tok_ref, nrow_ref, h_ref, o_ref, zero_ref, sem):
    base = pl.program_id(0) * DISPATCH_CHUNK

    @pl.when(base >= nrow_ref[0])
    def _():
        zero_ref[...] = jnp.zeros_like(zero_ref)
        fill = pltpu.make_async_copy(zero_ref, o_ref.at[pl.ds(base, DISPATCH_CHUNK)], sem)
        fill.start()
        fill.wait()

    def copy(r):
        return pltpu.make_async_copy(h_ref.at[pl.ds(tok_ref[r], 1)], o_ref.at[pl.ds(r, 1)], sem)

    @pl.when(base < nrow_ref[0])
    def _():
        def start(i, c):
            copy(base + i).start()
            return c

        def wait(i, c):
            copy(base + i).wait()
            return c

        lax.fori_loop(0, DISPATCH_CHUNK, start, 0)
        lax.fori_loop(0, DISPATCH_CHUNK, wait, 0)


def _dispatch(h2p, row_tok, n_used_rows):
    n_rows = row_tok.shape[0]
    width = h2p.shape[1]
    assert n_rows % DISPATCH_CHUNK == 0
    return pl.pallas_call(
        _dispatch_kernel,
        grid_spec=pltpu.PrefetchScalarGridSpec(
            num_scalar_prefetch=2, grid=(n_rows // DISPATCH_CHUNK,),
            in_specs=[pl.BlockSpec(memory_space=pl.ANY)],
            out_specs=pl.BlockSpec(memory_space=pl.ANY),
            scratch_shapes=[pltpu.VMEM((DISPATCH_CHUNK, width), h2p.dtype),
                            pltpu.SemaphoreType.DMA(())]),
        out_shape=jax.ShapeDtypeStruct((n_rows, width), h2p.dtype),
        compiler_params=pltpu.CompilerParams(dimension_semantics=("arbitrary",),
                                             has_side_effects=True),
        name="moe_dispatch",
    )(row_tok, n_used_rows, h2p)


def _unpack_bf16_pair(words):
    lo = lax.bitcast_convert_type(words << 16, F32).astype(BF16)
    hi = lax.bitcast_convert_type(words & jnp.uint32(0xFFFF0000), F32).astype(BF16)
    return lo, hi


def _gemm1_kernel(te_ref, nt_ref, x_ref, w_ref, b_ref, cmp_ref, o_ref, wbf_ref):
    i = pl.program_id(1)
    prev = te_ref[jnp.maximum(i - 1, 0)]

    @pl.when((i < nt_ref[0]) & ((i == 0) | (te_ref[i] != prev)))
    def _():
        wbf_ref[...] = w_ref[0].astype(BF16)

    @pl.when(i < nt_ref[0])
    def _():
        x_lo, x_hi = _unpack_bf16_pair(x_ref[...])
        half = x_lo.shape[1]
        hb = (jnp.dot(x_lo, wbf_ref[:half, :], preferred_element_type=F32)
              + jnp.dot(x_hi, wbf_ref[half:, :], preferred_element_type=F32) + b_ref[0])
        tn = hb.shape[1]
        nxt = pltpu.roll(hb, tn - 1, axis=1)
        glu = jnp.minimum(hb, SWIGLU_LIMIT)
        lin = jnp.clip(nxt, -SWIGLU_LIMIT, SWIGLU_LIMIT)
        act = glu * jax.nn.sigmoid(SWIGLU_ALPHA * glu) * (lin + 1.0)
        lane = lax.broadcasted_iota(jnp.int32, hb.shape, 1)
        act = jnp.where(lane % 2 == 0, act, 0.0).astype(BF16)
        cw = cmp_ref.shape[0]
        for c in range(tn // cw):
            o_ref[:, c * (cw // 2):(c + 1) * (cw // 2)] = jnp.dot(
                act[:, c * cw:(c + 1) * cw], cmp_ref[...],
                preferred_element_type=F32).astype(o_ref.dtype)

    @pl.when(i >= nt_ref[0])
    def _():
        o_ref[...] = jnp.zeros_like(o_ref)


def _gemm1(xin, w1, b1, tile_e, n_tiles_used, tm):
    n_rows, half_d = xin.shape
    n_e, d, f2 = w1.shape
    tn = _pick(f2, (1024, 512, 256))
    n_tiles = n_rows // tm
    cw = V7X_MXU_DIM
    cmp_np = np.zeros((cw, cw // 2), np.float32)
    cmp_np[np.arange(0, cw, 2), np.arange(cw // 2)] = 1.0

    def used(i, nt):
        return jnp.minimum(i, nt[0] - 1)

    return pl.pallas_call(
        _gemm1_kernel,
        grid_spec=pltpu.PrefetchScalarGridSpec(
            num_scalar_prefetch=2, grid=(f2 // tn, n_tiles),
            in_specs=[pl.BlockSpec((tm, half_d), lambda j, i, te, nt: (used(i, nt), 0)),
                      pl.BlockSpec((1, d, tn), lambda j, i, te, nt: (te[used(i, nt)], 0, j)),
                      pl.BlockSpec((1, 1, tn), lambda j, i, te, nt: (te[used(i, nt)], 0, j)),
                      pl.BlockSpec((cw, cw // 2), lambda j, i, te, nt: (0, 0))],
            out_specs=pl.BlockSpec((tm, tn // 2), lambda j, i, te, nt: (i, j)),
            scratch_shapes=[pltpu.VMEM((d, tn), BF16)]),
        out_shape=jax.ShapeDtypeStruct((n_rows, f2 // 2), BF16),
        compiler_params=_cparams(("arbitrary", "arbitrary")),
        name="moe_gemm1",
    )(tile_e, n_tiles_used, xin, w1, b1.reshape(n_e, 1, f2), jnp.asarray(cmp_np, BF16))


def _gemm2_kernel(te_ref, nt_ref, a_ref, w_ref, b_ref, o_ref, wbf_ref):
    i = pl.program_id(1)
    prev = te_ref[jnp.maximum(i - 1, 0)]

    @pl.when((i < nt_ref[0]) & ((i == 0) | (te_ref[i] != prev)))
    def _():
        wbf_ref[...] = w_ref[0].astype(BF16)

    @pl.when(i < nt_ref[0])
    def _():
        o_ref[...] = jnp.dot(a_ref[...], wbf_ref[...], preferred_element_type=F32) + b_ref[0]

    @pl.when(i >= nt_ref[0])
    def _():
        o_ref[...] = jnp.zeros_like(o_ref)


def _gemm2(act, w2, b2, tile_e, n_tiles_used, tm):
    n_rows, f = act.shape
    n_e, _, d = w2.shape
    tn = _pick(d, (1024, 512, 256))
    n_tiles = n_rows // tm

    def used(i, nt):
        return jnp.minimum(i, nt[0] - 1)

    return pl.pallas_call(
        _gemm2_kernel,
        grid_spec=pltpu.PrefetchScalarGridSpec(
            num_scalar_prefetch=2, grid=(d // tn, n_tiles),
            in_specs=[pl.BlockSpec((tm, f), lambda j, i, te, nt: (used(i, nt), 0)),
                      pl.BlockSpec((1, f, tn), lambda j, i, te, nt: (te[used(i, nt)], 0, j)),
                      pl.BlockSpec((1, 1, tn), lambda j, i, te, nt: (te[used(i, nt)], 0, j))],
            out_specs=pl.BlockSpec((tm, tn), lambda j, i, te, nt: (i, j)),
            scratch_shapes=[pltpu.VMEM((f, tn), BF16)]),
        out_shape=jax.ShapeDtypeStruct((n_rows, d), F32),
        compiler_params=_cparams(("arbitrary", "arbitrary")),
        name="moe_gemm2",
    )(tile_e, n_tiles_used, act, w2, b2.reshape(n_e, 1, d))


def _combine_kernel(dest_ref, x1_ref, gate_ref, gt_ref, g_ref, b_ref, y_ref, o_ref, buf, sem,
                    *, alpha, tc, n_tok):
    i = pl.program_id(0)
    n_steps = pl.num_programs(0)

    def copy(step, slot, k, r):
        src = y_ref.at[pl.ds(dest_ref[k * n_tok + step * tc + r], 1)]
        return pltpu.make_async_copy(src, buf.at[slot, k, pl.ds(r, 1)], sem.at[slot])

    def issue(step, slot):
        def body(r, c):
            for k in range(TOP_K):
                copy(step, slot, k, r).start()
            return c
        lax.fori_loop(0, tc, body, 0)

    def drain(step, slot):
        def body(r, c):
            for k in range(TOP_K):
                copy(step, slot, k, r).wait()
            return c
        lax.fori_loop(0, tc, body, 0)

    slot = i % 2

    @pl.when(i == 0)
    def _():
        issue(0, 0)

    @pl.when(i + 1 < n_steps)
    def _():
        issue(i + 1, 1 - slot)

    drain(i, slot)
    gates = gate_ref[...]
    ff = gates[:, 0:1] * buf[slot, 0]
    for k in range(1, TOP_K):
        ff = ff + gates[:, k:k + 1] * buf[slot, k]
    y = alpha * x1_ref[...] + (1.0 + gt_ref[0]) * ff
    o_ref[...] = _layer_norm(y, g_ref[...], b_ref[...])


def _combine(dest_flat, x1, gates_tk, gt, g, b, y, seq, alpha):
    t, d = x1.shape
    tc = _pick(seq, (128,))
    per_b = seq // tc
    return pl.pallas_call(
        functools.partial(_combine_kernel, alpha=alpha, tc=tc, n_tok=t),
        grid_spec=pltpu.PrefetchScalarGridSpec(
            num_scalar_prefetch=1, grid=(t // tc,),
            in_specs=[pl.BlockSpec((tc, d), lambda i, ds: (i, 0)),
                      pl.BlockSpec((tc, TOP_K), lambda i, ds: (i, 0)),
                      pl.BlockSpec((1, 1, d), lambda i, ds: (i // per_b, 0, 0)),
                      pl.BlockSpec((1, d), lambda i, ds: (0, 0)),
                      pl.BlockSpec((1, d), lambda i, ds: (0, 0)),
                      pl.BlockSpec(memory_space=pl.ANY)],
            out_specs=pl.BlockSpec((tc, d), lambda i, ds: (i, 0)),
            scratch_shapes=[pltpu.VMEM((2, TOP_K, tc, d), F32),
                            pltpu.SemaphoreType.DMA((2,))]),
        out_shape=jax.ShapeDtypeStruct((t, d), F32),
        compiler_params=_cparams(("arbitrary",)),
        name="moe_combine_ln2",
    )(dest_flat, x1, gates_tk, gt[:, None, :], g.reshape(1, d), b.reshape(1, d), y)


MOE_ROW_TILE = 256


def _moe_plan(idx_t, rank_t, counts, tm):
    n_e = counts.shape[0]
    k, t = idx_t.shape
    padded = (counts + tm - 1) // tm * tm
    pends = jnp.cumsum(padded)
    pstarts = pends - padded
    dest = pstarts[idx_t] + rank_t
    n_rows = -(-(k * t) // tm) * tm + n_e * tm
    tok = jnp.broadcast_to(jnp.arange(t, dtype=jnp.int32)[None, :], (k, t))
    row_tok = jnp.zeros((n_rows,), jnp.int32).at[dest.reshape(-1)].set(tok.reshape(-1))
    tile_start = jnp.arange(n_rows // tm, dtype=jnp.int32) * tm
    tile_e = jnp.minimum(jnp.searchsorted(pends, tile_start, side="right"), n_e - 1).astype(jnp.int32)
    n_used_rows = pends[-1].astype(jnp.int32).reshape(1)
    return dest.reshape(-1).astype(jnp.int32), row_tok, tile_e, n_used_rows


def kernel(x, c, positions, w_ada, b_ada, w_in, hgrn_lb, gnorm_w, w_o, ln1_g, ln1_b,
           router_w, router_b, w1, b1, w2, b2, ln2_g, ln2_b):
    bsz, seq, d = x.shape
    depth = w_ada.shape[0]
    t = bsz * seq
    att_w = d // 2
    hg_w = d - att_w
    alpha = (2.0 * depth) ** 0.25
    tm = MOE_ROW_TILE

    lb_all = jnp.cumsum(jax.nn.softmax(hgrn_lb.astype(F32), axis=0), axis=0)
    inv = ROPE_THETA ** (-(jnp.arange(0, ROT_DIM, 2, dtype=F32) / ROT_DIM))
    lane = np.arange(V7X_LANES)
    inv_lane = inv[(lane % ATT_HEAD_DIM) % (ROT_DIM // 2)].reshape(1, V7X_LANES)
    pos_b = jnp.broadcast_to(positions.astype(F32).reshape(t, 1), (t, V7X_LANES))

    x2d = x.reshape(t, d)
    for l in range(depth):
        mod = _adaln(c, w_ada[l], b_ada[l])
        sh_a, sc_a, gt_a, sh_f, sc_f, gt_f = jnp.split(mod, 6, axis=-1)

        proj = _in_proj(x2d, sc_a, sh_a, w_in[l], seq)
        proj3 = proj.reshape(bsz, seq, proj.shape[1])
        q_hp, k_hp = _qk_rope(proj, pos_b, inv_lane, bsz, seq, att_w)
        att = _attention(q_hp, k_hp, proj3, att_w).reshape(t, att_w)
        rec = _hgrn2(proj3, lb_all[l], gnorm_w[l], att_w, hg_w).reshape(t, hg_w)

        x1, h2p, logits_t = _out_proj(att, rec, x2d, w_o[l], gt_a, sc_f, sh_f, ln1_g[l], ln1_b[l],
                                      router_w[l].T, router_b[l], seq, alpha)

        idx_t, gate_t, rank_t, counts = _routing(logits_t)
        dest, row_tok, tile_e, n_used_rows = _moe_plan(idx_t, rank_t, counts[:, 0], tm)
        n_used_tiles = n_used_rows // tm
        xin = _dispatch(h2p, row_tok, n_used_rows)
        act = _gemm1(xin, w1[l], b1[l], tile_e, n_used_tiles, tm)
        y = _gemm2(act, w2[l], b2[l], tile_e, n_used_tiles, tm)
        x2d = _combine(dest, x1, gate_t.T, gt_f, ln2_g[l], ln2_b[l], y, seq, alpha)
    return x2d.reshape(bsz, seq, d)
```

```python
import functools

import numpy as np
import jax
import jax.numpy as jnp
from jax import lax
from jax.experimental import pallas as pl
from jax.experimental.pallas import tpu as pltpu

F32 = jnp.float32
BF16 = jnp.bfloat16

V7X_LANES = 128
V7X_MXU_DIM = 256
V7X_VMEM_LIMIT = 56 * 1024 * 1024

ATT_HEAD_DIM = 64
DILATED_PAIRS = ((128, 1), (512, 4), (2048, 16))
ATT_BLOCK = 128
ROT_DIM = ATT_HEAD_DIM // 4
ROPE_THETA = 500000.0
HG_EXPAND = 128
HG_CHUNK = 64
HG_SUB = 16
TOP_K = 4
SWIGLU_ALPHA = 1.702
SWIGLU_LIMIT = 7.0
LN_EPS = 1e-5
RMS_EPS = 1e-6
NEG_INF = -1e30

NT_DIMS = (((1,), (1,)), ((), ()))


def _pick(n, candidates):
    for c in candidates:
        if n % c == 0:
            return c
    raise ValueError(f"no tile in {candidates} divides {n}")


def _cparams(sem, vmem=V7X_VMEM_LIMIT):
    return pltpu.CompilerParams(dimension_semantics=sem, vmem_limit_bytes=vmem)


def _adaln_kernel(c_ref, w_ref, b_ref, o_ref):
    c = c_ref[...]
    s = c * jax.nn.sigmoid(c)
    o_ref[...] = jnp.dot(s.astype(BF16), w_ref[...].astype(BF16),
                         preferred_element_type=F32) + b_ref[...]


def _adaln(c, w, b):
    bsz, d = c.shape
    n = w.shape[1]
    rows = 8
    cp = jnp.zeros((rows, d), F32).at[:bsz].set(c)
    tn = _pick(n, (1024, 512, 256, 128))
    out = pl.pallas_call(
        _adaln_kernel,
        grid=(n // tn,),
        in_specs=[pl.BlockSpec((rows, d), lambda j: (0, 0)),
                  pl.BlockSpec((d, tn), lambda j: (0, j)),
                  pl.BlockSpec((1, tn), lambda j: (0, j))],
        out_specs=pl.BlockSpec((rows, tn), lambda j: (0, j)),
        out_shape=jax.ShapeDtypeStruct((rows, n), F32),
        compiler_params=_cparams(("arbitrary",)),
        name="adaln",
    )(cp, w, b.reshape(1, n))
    return out[:bsz]


def _in_proj_kernel(x_ref, sc_ref, sh_ref, w_ref, o_ref, wbf_ref):
    @pl.when(pl.program_id(1) == 0)
    def _():
        wbf_ref[...] = w_ref[...].astype(BF16)

    h = x_ref[...] * (1.0 + sc_ref[0]) + sh_ref[0]
    o_ref[...] = jnp.dot(h.astype(BF16), wbf_ref[...], preferred_element_type=F32)


def _in_proj(x2d, sc, sh, w, seq):
    t, d = x2d.shape
    nc = w.shape[1]
    tm = _pick(seq, (512, 256, 128))
    tn = _pick(nc, (1024, 512, 256, 128))
    per_b = seq // tm
    vec = pl.BlockSpec((1, 1, d), lambda j, i: (i // per_b, 0, 0))
    return pl.pallas_call(
        _in_proj_kernel,
        grid=(nc // tn, t // tm),
        in_specs=[pl.BlockSpec((tm, d), lambda j, i: (i, 0)), vec, vec,
                  pl.BlockSpec((d, tn), lambda j, i: (0, j))],
        out_specs=pl.BlockSpec((tm, tn), lambda j, i: (i, j)),
        out_shape=jax.ShapeDtypeStruct((t, nc), F32),
        scratch_shapes=[pltpu.VMEM((d, tn), BF16)],
        compiler_params=_cparams(("arbitrary", "arbitrary")),
        name="in_proj",
    )(x2d, sc[:, None, :], sh[:, None, :], w)


def _rope_kernel(q_ref, k_ref, pos_ref, inv_ref, qo_ref, ko_ref, cos_ref, sin_ref):
    lane = lax.broadcasted_iota(jnp.int32, cos_ref.shape, 1)
    lh = lane % ATT_HEAD_DIM
    half = ROT_DIM // 2

    @pl.when(pl.program_id(1) == 0)
    def _():
        ang = pos_ref[...] * inv_ref[...]
        cos_ref[...] = jnp.where(lh < ROT_DIM, jnp.cos(ang), 1.0)
        sn = jnp.sin(ang)
        sin_ref[...] = jnp.where(lh < half, -sn, jnp.where(lh < ROT_DIM, sn, 0.0))

    cs = cos_ref[...]
    sn = sin_ref[...]

    def rope(t):
        swapped = jnp.where(lh < half,
                            pltpu.roll(t, V7X_LANES - half, axis=1),
                            pltpu.roll(t, half, axis=1))
        return t * cs + swapped * sn

    qo_ref[0, 0] = rope(q_ref[...]) * (ATT_HEAD_DIM ** -0.5)
    ko_ref[0, 0] = rope(k_ref[...])


def _qk_rope(proj, pos_b, inv_lane, bsz, seq, att_w):
    t = proj.shape[0]
    hp = att_w // V7X_LANES
    tm = _pick(seq, (512, 256, 128))
    per_b = seq // tm
    out_spec = pl.BlockSpec((1, 1, tm, V7X_LANES),
                            lambda i, h: (i // per_b, h, i % per_b, 0))
    shp = jax.ShapeDtypeStruct((bsz, hp, seq, V7X_LANES), F32)
    return pl.pallas_call(
        _rope_kernel,
        grid=(t // tm, hp),
        in_specs=[pl.BlockSpec((tm, V7X_LANES), lambda i, h: (i, h)),
                  pl.BlockSpec((tm, V7X_LANES), lambda i, h: (i, hp + h)),
                  pl.BlockSpec((tm, V7X_LANES), lambda i, h: (i, 0)),
                  pl.BlockSpec((1, V7X_LANES), lambda i, h: (0, 0))],
        out_specs=[out_spec, out_spec],
        out_shape=[shp, shp],
        scratch_shapes=[pltpu.VMEM((tm, V7X_LANES), F32), pltpu.VMEM((tm, V7X_LANES), F32)],
        compiler_params=_cparams(("arbitrary", "arbitrary")),
        name="qk_rope",
    )(proj, proj, pos_b, inv_lane)


def _attn_block(q_blk, k_cat, v_cat, first):
    qn = q_blk.shape[0]
    nk = k_cat.shape[0]
    qi = lax.broadcasted_iota(jnp.int32, (qn, nk), 0)
    kj = lax.broadcasted_iota(jnp.int32, (qn, nk), 1)
    dist = qi - kj if first else qi + qn - kj
    w_steps = qn
    valid = (dist >= 0) & (dist <= w_steps)
    lane = lax.broadcasted_iota(jnp.int32, (nk, V7X_LANES), 1)
    qb = q_blk.astype(BF16)
    vb = v_cat.astype(BF16)
    outs, lses = [], []
    for head in range(V7X_LANES // ATT_HEAD_DIM):
        in_head = (lane // ATT_HEAD_DIM) == head
        kh = jnp.where(in_head, k_cat, 0.0).astype(BF16)
        s = lax.dot_general(qb, kh, NT_DIMS, preferred_element_type=F32)
        s = jnp.where(valid, s, NEG_INF)
        m = jnp.max(s, axis=-1, keepdims=True)
        p = jnp.exp(s - m)
        den = jnp.sum(p, axis=-1, keepdims=True)
        o = jnp.dot(p.astype(BF16), vb, preferred_element_type=F32) / den
        outs.append(o)
        lses.append(m + jnp.log(den))
    olane = lax.broadcasted_iota(jnp.int32, (qn, V7X_LANES), 1)
    o = outs[-1]
    lse = jnp.broadcast_to(lses[-1], (qn, V7X_LANES))
    for head in range(len(outs) - 2, -1, -1):
        sel = (olane // ATT_HEAD_DIM) == head
        o = jnp.where(sel, outs[head], o)
        lse = jnp.where(sel, lses[head], lse)
    return o, lse


ATT_UNROLL = 4


def _attn_kernel(q_ref, k_ref, v_ref, out_ref, o_scr, l_scr, *, seq):
    qn = ATT_BLOCK

    def rows(start, n, d):
        return pl.ds(start, n) if d == 1 else pl.ds(start, n, stride=d)

    for bi, (window, d) in enumerate(DILATED_PAIRS):
        nb = seq // (d * qn)

        def first_block(r, carry, d=d, bi=bi):
            sl = rows(r, qn, d)
            o, lse = _attn_block(q_ref[0, 0, sl, :], k_ref[0, 0, sl, :], v_ref[0, sl, :], True)
            o_scr[bi, sl, :] = o
            l_scr[bi, sl, :] = lse
            return carry

        def later_block(idx, carry, d=d, bi=bi, nb=nb):
            r = idx // (nb - 1)
            n = idx % (nb - 1) + 1
            qsl = rows(r + n * qn * d, qn, d)
            ksl = rows(r + (n - 1) * qn * d, 2 * qn, d)
            o, lse = _attn_block(q_ref[0, 0, qsl, :], k_ref[0, 0, ksl, :], v_ref[0, ksl, :], False)
            o_scr[bi, qsl, :] = o
            l_scr[bi, qsl, :] = lse
            return carry

        lax.fori_loop(0, d, first_block, 0, unroll=min(d, ATT_UNROLL))
        if nb > 1:
            trips = d * (nb - 1)
            lax.fori_loop(0, trips, later_block, 0,
                          unroll=max(u for u in range(1, ATT_UNROLL + 1) if trips % u == 0))

    def combine(i, carry):
        sl = pl.ds(pl.multiple_of(i * qn, qn), qn)
        ls = [l_scr[bi, sl, :] for bi in range(len(DILATED_PAIRS))]
        mx = functools.reduce(jnp.maximum, ls)
        ws = [jnp.exp(l - mx) for l in ls]
        tot = functools.reduce(lambda a, b: a + b, ws)
        acc = functools.reduce(lambda a, b: a + b,
                               [w * o_scr[bi, sl, :] for bi, w in enumerate(ws)])
        out_ref[0, sl, :] = (acc / tot).astype(out_ref.dtype)
        return carry

    lax.fori_loop(0, seq // qn, combine, 0)


def _attention(q_hp, k_hp, proj3, att_w):
    bsz, hp, seq, _ = q_hp.shape
    for window, d in DILATED_PAIRS:
        assert window // d == ATT_BLOCK and seq % (d * ATT_BLOCK) == 0
    v_blk0 = 2 * att_w // V7X_LANES
    nbr = len(DILATED_PAIRS)
    qk_spec = pl.BlockSpec((1, 1, seq, V7X_LANES), lambda b, h: (b, h, 0, 0))
    return pl.pallas_call(
        functools.partial(_attn_kernel, seq=seq),
        grid=(bsz, hp),
        in_specs=[qk_spec, qk_spec,
                  pl.BlockSpec((1, seq, V7X_LANES), lambda b, h: (b, 0, v_blk0 + h))],
        out_specs=pl.BlockSpec((1, seq, V7X_LANES), lambda b, h: (b, 0, h)),
        out_shape=jax.ShapeDtypeStruct((bsz, seq, att_w), BF16),
        scratch_shapes=[pltpu.VMEM((nbr, seq, V7X_LANES), F32),
                        pltpu.VMEM((nbr, seq, V7X_LANES), F32)],
        compiler_params=_cparams(("arbitrary", "arbitrary")),
        name="dilated_attn",
    )(q_hp, k_hp, proj3)


def _hgrn_kernel(q_ref, f_ref, i_ref, g_ref, lb_ref, gw_ref, tri_ref, o_ref, st_ref, *, heads, ts):
    @pl.when(pl.program_id(2) == 0)
    def _():
        st_ref[...] = jnp.zeros_like(st_ref)

    c_len = HG_CHUNK
    tri = tri_ref[...]

    def chunk(ci, carry):
        r0 = pl.multiple_of(ci * c_len, c_len)
        rsl = pl.ds(r0, c_len)
        for h in range(heads):
            lsl = slice(h * HG_EXPAND, (h + 1) * HG_EXPAND)
            lb = lb_ref[:, lsl]
            f = lb + (1.0 - lb) * jax.nn.sigmoid(f_ref[0, rsl, lsl])
            logf = jnp.log(f)
            kk = 1.0 - f
            q = q_ref[0, rsl, lsl]
            v = i_ref[0, rsl, lsl]
            b = jnp.dot(tri, logf, precision=lax.Precision.HIGHEST, preferred_element_type=F32)
            b_last = b[c_len - 1:c_len, :]
            st = st_ref[h]
            o_inter = lax.dot_general((q * jnp.exp(b)).astype(BF16), st.astype(BF16), NT_DIMS,
                                      preferred_element_type=F32)
            vb = v.astype(BF16)
            o_rows = []
            for i in range(c_len // HG_SUB):
                lo, hi = i * HG_SUB, (i + 1) * HG_SUB
                mid = lo + HG_SUB // 2
                anchor = b[mid:mid + 1, :]
                qt = q[lo:hi] * jnp.exp(b[lo:hi] - anchor)
                kt = kk[:hi] * jnp.exp(anchor - b[:hi])
                a = lax.dot_general(qt.astype(BF16), kt.astype(BF16), NT_DIMS,
                                    preferred_element_type=F32)
                ti = lax.broadcasted_iota(jnp.int32, (HG_SUB, hi), 0) + lo
                si = lax.broadcasted_iota(jnp.int32, (HG_SUB, hi), 1)
                a = jnp.where(si <= ti, a, 0.0)
                o_rows.append(jnp.dot(a.astype(BF16), vb[:hi], preferred_element_type=F32))
            o = o_inter + jnp.concatenate(o_rows, axis=0)
            kl = kk * jnp.exp(b_last - b)
            upd = jnp.dot(v.T.astype(BF16), kl.astype(BF16), preferred_element_type=F32)
            st_ref[h] = st * jnp.exp(b_last) + upd
            o = o * lax.rsqrt(jnp.mean(o * o, axis=-1, keepdims=True) + RMS_EPS)
            g = g_ref[0, rsl, lsl]
            o = o * gw_ref[:, lsl] * (g * jax.nn.sigmoid(g))
            o_ref[0, rsl, lsl] = o.astype(o_ref.dtype)
        return carry

    lax.fori_loop(0, ts // c_len, chunk, 0, unroll=True)


def _hgrn2(proj3, lb, gw, att_w, hg_w):
    bsz, seq, _ = proj3.shape
    n_heads = hg_w // HG_EXPAND
    heads = _pick(n_heads, (4, 2, 1))
    lanes = heads * HG_EXPAND
    ts = _pick(seq, (256, 128, 64))
    base = 3 * att_w
    assert base % lanes == 0 and hg_w % lanes == 0

    def seg(k):
        off = (base + k * hg_w) // lanes
        return pl.BlockSpec((1, ts, lanes), lambda b, h, s: (b, s, off + h))

    vec = pl.BlockSpec((1, lanes), lambda b, h, s: (0, h))
    tri = jnp.asarray(np.tril(np.ones((HG_CHUNK, HG_CHUNK), np.float32)))
    return pl.pallas_call(
        functools.partial(_hgrn_kernel, heads=heads, ts=ts),
        grid=(bsz, hg_w // lanes, seq // ts),
        in_specs=[seg(0), seg(1), seg(2), seg(3), vec, vec,
                  pl.BlockSpec((HG_CHUNK, HG_CHUNK), lambda b, h, s: (0, 0))],
        out_specs=pl.BlockSpec((1, ts, lanes), lambda b, h, s: (b, s, h)),
        out_shape=jax.ShapeDtypeStruct((bsz, seq, hg_w), BF16),
        scratch_shapes=[pltpu.VMEM((heads, HG_EXPAND, HG_EXPAND), F32)],
        compiler_params=_cparams(("arbitrary", "arbitrary", "arbitrary")),
        name="hgrn2",
    )(proj3, proj3, proj3, proj3, lb.reshape(1, hg_w), gw.reshape(1, hg_w), tri)


def _layer_norm(y, g, b):
    mu = jnp.mean(y, axis=-1, keepdims=True)
    yc = y - mu
    var = jnp.mean(yc * yc, axis=-1, keepdims=True)
    return yc * lax.rsqrt(var + LN_EPS) * g + b


def _bf16_split(x):
    hi = x.astype(BF16)
    lo = (x - hi.astype(F32)).astype(BF16)
    return hi, lo


def _out_proj_kernel(att_ref, rec_ref, x_ref, wo_ref, gt_ref, sc_ref, sh_ref, g_ref, b_ref,
                     rw_ref, rb_ref, x1_ref, h2_ref, lg_ref, wobf_ref, rwhi_ref, rwlo_ref,
                     *, alpha, att_w):
    @pl.when(pl.program_id(0) == 0)
    def _():
        wobf_ref[...] = wo_ref[...].astype(BF16)
        hi, lo = _bf16_split(rw_ref[...])
        rwhi_ref[...] = hi
        rwlo_ref[...] = lo

    mix = (jnp.dot(att_ref[...], wobf_ref[:att_w, :], preferred_element_type=F32)
           + jnp.dot(rec_ref[...], wobf_ref[att_w:, :], preferred_element_type=F32))
    y = alpha * x_ref[...] + (1.0 + gt_ref[0]) * mix
    x1 = _layer_norm(y, g_ref[...], b_ref[...])
    x1_ref[...] = x1
    h2 = x1 * (1.0 + sc_ref[0]) + sh_ref[0]
    hi, lo = _bf16_split(h2)
    lg = (lax.dot_general(rwhi_ref[...], hi, NT_DIMS, preferred_element_type=F32)
          + lax.dot_general(rwhi_ref[...], lo, NT_DIMS, preferred_element_type=F32)
          + lax.dot_general(rwlo_ref[...], hi, NT_DIMS, preferred_element_type=F32))
    lg_ref[...] = lg + rb_ref[...]
    half = h2.shape[1] // 2
    bits = lax.bitcast_convert_type(hi.astype(F32), jnp.uint32)
    h2_ref[...] = (bits[:, :half] >> 16) | (bits[:, half:] & jnp.uint32(0xFFFF0000))


def _out_proj(att, rec, x2d, wo, gt, sc, sh, g, b, rw_t, rb, seq, alpha):
    t, d = x2d.shape
    att_w = att.shape[1]
    n_e = rw_t.shape[0]
    tm = _pick(seq, (256, 128))
    per_b = seq // tm
    vec3 = pl.BlockSpec((1, 1, d), lambda i: (i // per_b, 0, 0))
    full = lambda shape: pl.BlockSpec(shape, lambda i: (0,) * len(shape))
    once = lambda shape: pl.BlockSpec(shape, lambda i: (0,) * len(shape),
                                      pipeline_mode=pl.Buffered(1))
    return pl.pallas_call(
        functools.partial(_out_proj_kernel, alpha=alpha, att_w=att_w),
        grid=(t // tm,),
        in_specs=[pl.BlockSpec((tm, att_w), lambda i: (i, 0)),
                  pl.BlockSpec((tm, d - att_w), lambda i: (i, 0)),
                  pl.BlockSpec((tm, d), lambda i: (i, 0)),
                  once((d, d)), vec3, vec3, vec3, full((1, d)), full((1, d)),
                  full((n_e, d)), full((n_e, 1))],
        out_specs=[pl.BlockSpec((tm, d), lambda i: (i, 0)),
                   pl.BlockSpec((tm, d // 2), lambda i: (i, 0)),
                   pl.BlockSpec((n_e, tm), lambda i: (0, i))],
        out_shape=[jax.ShapeDtypeStruct((t, d), F32),
                   jax.ShapeDtypeStruct((t, d // 2), jnp.uint32),
                   jax.ShapeDtypeStruct((n_e, t), F32)],
        scratch_shapes=[pltpu.VMEM((d, d), BF16), pltpu.VMEM((n_e, d), BF16),
                        pltpu.VMEM((n_e, d), BF16)],
        compiler_params=_cparams(("arbitrary",)),
        name="out_proj_ln1",
    )(att, rec, x2d, wo, gt[:, None, :], sc[:, None, :], sh[:, None, :],
      g.reshape(1, d), b.reshape(1, d), rw_t, rb.reshape(n_e, 1))


def _routing_kernel(lg_ref, tri_ref, low_ref, gate_ref, dest_ref, tile_ref, pad_ref, sel_scr, rk_scr,
                    *, blk, tm):
    n_e, t = lg_ref.shape
    eidx = lax.broadcasted_iota(jnp.int32, (n_e, t), 0)
    cur = lg_ref[...]
    vals, idxs = [], []
    for _ in range(TOP_K):
        m = jnp.max(cur, axis=0, keepdims=True)
        ik = jnp.min(jnp.where(cur == m, eidx, n_e), axis=0, keepdims=True)
        cur = jnp.where(eidx == ik, -jnp.inf, cur)
        vals.append(m)
        idxs.append(ik)
    es = [jnp.exp(v - vals[0]) for v in vals]
    den = functools.reduce(lambda a, b: a + b, es)
    for k in range(TOP_K):
        gate_ref[k:k + 1, :] = es[k] / den
    sel = functools.reduce(lambda a, b: a | b, [eidx == ik for ik in idxs])
    sel_scr[...] = jnp.where(sel, 1.0, 0.0)

    tri = tri_ref[...]
    carry = jnp.zeros((n_e, 1), F32)
    for j in range(t // blk):
        sb = sel_scr[:, j * blk:(j + 1) * blk]
        pre = jnp.dot(sb.astype(BF16), tri, preferred_element_type=F32)
        rk_scr[:, j * blk:(j + 1) * blk] = pre + carry
        carry = carry + jnp.sum(sb, axis=1, keepdims=True)

    counts = jnp.broadcast_to(carry, (n_e, V7X_LANES))
    padded = jnp.floor((counts + (tm - 1)) * (1.0 / tm)) * tm
    pends = jnp.dot(low_ref[...], padded, precision=lax.Precision.HIGHEST,
                    preferred_element_type=F32)
    pstarts = pends - padded
    row0 = rk_scr[...] + pstarts[:, 0:1]
    for k in range(TOP_K):
        dest_ref[k:k + 1, :] = jnp.sum(jnp.where(eidx == idxs[k], row0, 0.0), axis=0,
                                       keepdims=True).astype(jnp.int32)
    starts = (lax.broadcasted_iota(jnp.int32, (n_e, tile_ref.shape[1]), 1) * tm).astype(F32)
    owner = jnp.sum(jnp.where(pends[:, 0:1] <= starts, 1.0, 0.0), axis=0, keepdims=True)
    tile_ref[0:1, :] = jnp.minimum(owner, n_e - 1.0).astype(jnp.int32)
    tile_ref[1:2, :] = jnp.broadcast_to(jnp.max(pends[:, 0:1], axis=0, keepdims=True),
                                        (1, tile_ref.shape[1])).astype(jnp.int32)
    pad_ref[0] = (pstarts + counts).astype(jnp.int32)
    pad_ref[1] = (padded - counts).astype(jnp.int32)


def _routing(logits_t, tm, n_tiles):
    n_e, t = logits_t.shape
    assert tm & (tm - 1) == 0
    blk = _pick(t, (256, 128))
    tri = jnp.asarray(np.triu(np.ones((blk, blk), np.float32), 1), BF16)
    low = jnp.asarray(np.tril(np.ones((n_e, n_e), np.float32)))
    ntp = -(-n_tiles // V7X_LANES) * V7X_LANES
    full = lambda shape: pl.BlockSpec(shape, lambda: (0,) * len(shape))
    return pl.pallas_call(
        functools.partial(_routing_kernel, blk=blk, tm=tm),
        in_specs=[full((n_e, t)), full((blk, blk)), full((n_e, n_e))],
        out_specs=[full((TOP_K, t)), full((TOP_K, t)), full((2, ntp)), full((2, n_e, V7X_LANES))],
        out_shape=[jax.ShapeDtypeStruct((TOP_K, t), F32),
                   jax.ShapeDtypeStruct((TOP_K, t), jnp.int32),
                   jax.ShapeDtypeStruct((2, ntp), jnp.int32),
                   jax.ShapeDtypeStruct((2, n_e, V7X_LANES), jnp.int32)],
        scratch_shapes=[pltpu.VMEM((n_e, t), F32), pltpu.VMEM((n_e, t), F32)],
        compiler_params=pltpu.CompilerParams(vmem_limit_bytes=V7X_VMEM_LIMIT),
        name="routing",
    )(logits_t, tri, low)


def _dispatch_kernel(dest_ref, pad_off_ref, pad_n_ref, nrow_ref, h_ref, o_ref, zero_ref, sem, zsem,
                     *, n_tok, tq, tm, n_e):
    step = pl.program_id(0)
    base = step * tq

    def fill(off, n):
        return pltpu.make_async_copy(zero_ref.at[pl.ds(0, n)], o_ref.at[pl.ds(off, n)], zsem)

    sub = 8
    pieces = [1 << s for s in range(tm.bit_length() - 2, sub.bit_length() - 2, -1)]

    def pad_rows(wait):
        def per_expert(e, c):
            off = pad_off_ref[e]
            n = pad_n_ref[e]
            head = jnp.minimum(n, (-off) & (sub - 1))
            for s in range(sub - 1):
                @pl.when(s < head)
                def _(s=s):
                    cp = fill(off + s, 1)
                    cp.wait() if wait else cp.start()
            off = off + head
            n = n - head
            for p in pieces:
                hit = (n & p) != 0

                @pl.when(hit)
                def _(off=off, p=p):
                    cp = fill(pl.multiple_of(off, sub), p)
                    cp.wait() if wait else cp.start()

                off = off + jnp.where(hit, p, 0)
            return c
        lax.fori_loop(0, n_e, per_expert, 0)

        def per_tile(i, c):
            cp = fill(pl.multiple_of(i * tm, tm), tm)
            cp.wait() if wait else cp.start()
            return c
        lax.fori_loop(nrow_ref[0] // tm, o_ref.shape[0] // tm, per_tile, 0)

    @pl.when(step == 0)
    def _():
        zero_ref[...] = jnp.zeros_like(zero_ref)
        pad_rows(False)

    def copy(r, k):
        dst = o_ref.at[pl.ds(dest_ref[k * n_tok + base + r], 1)]
        return pltpu.make_async_copy(h_ref.at[pl.ds(r, 1)], dst, sem)

    def start(r, c):
        for k in range(TOP_K):
            copy(r, k).start(priority=k % 2)
        return c

    def wait(r, c):
        for k in range(TOP_K):
            copy(r, k).wait()
        return c

    lax.fori_loop(0, tq, start, 0, unroll=2)
    lax.fori_loop(0, tq, wait, 0, unroll=2)

    @pl.when(step == 0)
    def _():
        pad_rows(True)


def _dispatch(h2p, dest_flat, pad_off, pad_n, n_used_rows, n_rows, tm):
    t, width = h2p.shape
    n_e = pad_off.shape[0]
    tq = _pick(t, (256, 128))
    return pl.pallas_call(
        functools.partial(_dispatch_kernel, n_tok=t, tq=tq, tm=tm, n_e=n_e),
        grid_spec=pltpu.PrefetchScalarGridSpec(
            num_scalar_prefetch=4, grid=(t // tq,),
            in_specs=[pl.BlockSpec((tq, width), lambda i, *_: (i, 0))],
            out_specs=pl.BlockSpec(memory_space=pl.ANY),
            scratch_shapes=[pltpu.VMEM((tm, width), h2p.dtype),
                            pltpu.SemaphoreType.DMA(()), pltpu.SemaphoreType.DMA(())]),
        out_shape=jax.ShapeDtypeStruct((n_rows, width), h2p.dtype),
        compiler_params=pltpu.CompilerParams(dimension_semantics=("arbitrary",),
                                             has_side_effects=True,
                                             vmem_limit_bytes=V7X_VMEM_LIMIT),
        name="moe_dispatch",
    )(dest_flat, pad_off, pad_n, n_used_rows, h2p)


def _unpack_bf16_pair(words):
    lo = lax.bitcast_convert_type(words << 16, F32).astype(BF16)
    hi = lax.bitcast_convert_type(words & jnp.uint32(0xFFFF0000), F32).astype(BF16)
    return lo, hi


def _gemm1_kernel(te_ref, nt_ref, x_ref, w_ref, b_ref, cmp_ref, o_ref, wbf_ref):
    i = pl.program_id(1)
    prev = te_ref[jnp.maximum(i - 1, 0)]

    @pl.when((i < nt_ref[0]) & ((i == 0) | (te_ref[i] != prev)))
    def _():
        wbf_ref[...] = w_ref[0].astype(BF16)

    @pl.when(i < nt_ref[0])
    def _():
        x_lo, x_hi = _unpack_bf16_pair(x_ref[...])
        half = x_lo.shape[1]
        hb = (jnp.dot(x_lo, wbf_ref[:half, :], preferred_element_type=F32)
              + jnp.dot(x_hi, wbf_ref[half:, :], preferred_element_type=F32) + b_ref[0])
        tn = hb.shape[1]
        nxt = pltpu.roll(hb, tn - 1, axis=1)
        glu = jnp.minimum(hb, SWIGLU_LIMIT)
        lin = jnp.clip(nxt, -SWIGLU_LIMIT, SWIGLU_LIMIT)
        act = glu * jax.nn.sigmoid(SWIGLU_ALPHA * glu) * (lin + 1.0)
        lane = lax.broadcasted_iota(jnp.int32, hb.shape, 1)
        act = jnp.where(lane % 2 == 0, act, 0.0).astype(BF16)
        cw = cmp_ref.shape[0]
        for c in range(tn // cw):
            o_ref[:, c * (cw // 2):(c + 1) * (cw // 2)] = jnp.dot(
                act[:, c * cw:(c + 1) * cw], cmp_ref[...],
                preferred_element_type=F32).astype(o_ref.dtype)

    @pl.when(i >= nt_ref[0])
    def _():
        o_ref[...] = jnp.zeros_like(o_ref)


def _gemm1(xin, w1, b1, tile_e, n_tiles_used, tm):
    n_rows, half_d = xin.shape
    n_e, d, f2 = w1.shape
    tn = _pick(f2, (1024, 512, 256))
    n_tiles = n_rows // tm
    cw = V7X_MXU_DIM
    cmp_np = np.zeros((cw, cw // 2), np.float32)
    cmp_np[np.arange(0, cw, 2), np.arange(cw // 2)] = 1.0

    def used(i, nt):
        return jnp.minimum(i, nt[0] - 1)

    return pl.pallas_call(
        _gemm1_kernel,
        grid_spec=pltpu.PrefetchScalarGridSpec(
            num_scalar_prefetch=2, grid=(f2 // tn, n_tiles),
            in_specs=[pl.BlockSpec((tm, half_d), lambda j, i, te, nt: (used(i, nt), 0)),
                      pl.BlockSpec((1, d, tn), lambda j, i, te, nt: (te[used(i, nt)], 0, j)),
                      pl.BlockSpec((1, 1, tn), lambda j, i, te, nt: (te[used(i, nt)], 0, j)),
                      pl.BlockSpec((cw, cw // 2), lambda j, i, te, nt: (0, 0))],
            out_specs=pl.BlockSpec((tm, tn // 2), lambda j, i, te, nt: (i, j)),
            scratch_shapes=[pltpu.VMEM((d, tn), BF16)]),
        out_shape=jax.ShapeDtypeStruct((n_rows, f2 // 2), BF16),
        compiler_params=_cparams(("arbitrary", "arbitrary")),
        name="moe_gemm1",
    )(tile_e, n_tiles_used, xin, w1, b1.reshape(n_e, 1, f2), jnp.asarray(cmp_np, BF16))


def _gemm2_kernel(te_ref, nt_ref, a_ref, w_ref, b_ref, o_ref, wbf_ref):
    i = pl.program_id(1)
    prev = te_ref[jnp.maximum(i - 1, 0)]

    @pl.when((i < nt_ref[0]) & ((i == 0) | (te_ref[i] != prev)))
    def _():
        wbf_ref[...] = w_ref[0].astype(BF16)

    @pl.when(i < nt_ref[0])
    def _():
        o_ref[...] = jnp.dot(a_ref[...], wbf_ref[...], preferred_element_type=F32) + b_ref[0]

    @pl.when(i >= nt_ref[0])
    def _():
        o_ref[...] = jnp.zeros_like(o_ref)


def _gemm2(act, w2, b2, tile_e, n_tiles_used, tm):
    n_rows, f = act.shape
    n_e, _, d = w2.shape
    tn = _pick(d, (1024, 512, 256))
    n_tiles = n_rows // tm

    def used(i, nt):
        return jnp.minimum(i, nt[0] - 1)

    return pl.pallas_call(
        _gemm2_kernel,
        grid_spec=pltpu.PrefetchScalarGridSpec(
            num_scalar_prefetch=2, grid=(d // tn, n_tiles),
            in_specs=[pl.BlockSpec((tm, f), lambda j, i, te, nt: (used(i, nt), 0)),
                      pl.BlockSpec((1, f, tn), lambda j, i, te, nt: (te[used(i, nt)], 0, j)),
                      pl.BlockSpec((1, 1, tn), lambda j, i, te, nt: (te[used(i, nt)], 0, j))],
            out_specs=pl.BlockSpec((tm, tn), lambda j, i, te, nt: (i, j)),
            scratch_shapes=[pltpu.VMEM((f, tn), BF16)]),
        out_shape=jax.ShapeDtypeStruct((n_rows, d), F32),
        compiler_params=_cparams(("arbitrary", "arbitrary")),
        name="moe_gemm2",
    )(tile_e, n_tiles_used, act, w2, b2.reshape(n_e, 1, d))


def _combine_kernel(dest_ref, x1_ref, gate_ref, gt_ref, g_ref, b_ref, y_ref, o_ref, buf, sem,
                    *, alpha, tc, n_tok):
    i = pl.program_id(0)
    n_steps = pl.num_programs(0)

    def copy(step, slot, k, r):
        src = y_ref.at[pl.ds(dest_ref[k * n_tok + step * tc + r], 1)]
        return pltpu.make_async_copy(src, buf.at[slot, k, pl.ds(r, 1)], sem.at[slot])

    def issue(step, slot):
        def body(r, c):
            for k in range(TOP_K):
                copy(step, slot, k, r).start()
            return c
        lax.fori_loop(0, tc, body, 0)

    def drain(step, slot):
        def body(r, c):
            for k in range(TOP_K):
                copy(step, slot, k, r).wait()
            return c
        lax.fori_loop(0, tc, body, 0)

    slot = i % 2

    @pl.when(i == 0)
    def _():
        issue(0, 0)

    @pl.when(i + 1 < n_steps)
    def _():
        issue(i + 1, 1 - slot)

    drain(i, slot)
    gates = gate_ref[...]
    ff = gates[:, 0:1] * buf[slot, 0]
    for k in range(1, TOP_K):
        ff = ff + gates[:, k:k + 1] * buf[slot, k]
    y = alpha * x1_ref[...] + (1.0 + gt_ref[0]) * ff
    o_ref[...] = _layer_norm(y, g_ref[...], b_ref[...])


def _combine(dest_flat, x1, gates_tk, gt, g, b, y, seq, alpha):
    t, d = x1.shape
    tc = _pick(seq, (128,))
    per_b = seq // tc
    return pl.pallas_call(
        functools.partial(_combine_kernel, alpha=alpha, tc=tc, n_tok=t),
        grid_spec=pltpu.PrefetchScalarGridSpec(
            num_scalar_prefetch=1, grid=(t // tc,),
            in_specs=[pl.BlockSpec((tc, d), lambda i, ds: (i, 0)),
                      pl.BlockSpec((tc, TOP_K), lambda i, ds: (i, 0)),
                      pl.BlockSpec((1, 1, d), lambda i, ds: (i // per_b, 0, 0)),
                      pl.BlockSpec((1, d), lambda i, ds: (0, 0)),
                      pl.BlockSpec((1, d), lambda i, ds: (0, 0)),
                      pl.BlockSpec(memory_space=pl.ANY)],
            out_specs=pl.BlockSpec((tc, d), lambda i, ds: (i, 0)),
            scratch_shapes=[pltpu.VMEM((2, TOP_K, tc, d), F32),
                            pltpu.SemaphoreType.DMA((2,))]),
        out_shape=jax.ShapeDtypeStruct((t, d), F32),
        compiler_params=_cparams(("arbitrary",)),
        name="moe_combine_ln2",
    )(dest_flat, x1, gates_tk, gt[:, None, :], g.reshape(1, d), b.reshape(1, d), y)


MOE_ROW_TILE = 256


def _moe_ffn(h2p, logits_t, x1, gt_f, ln_g, ln_b, w1, b1, w2, b2, seq, alpha):
    tm = MOE_ROW_TILE
    n_e, t = logits_t.shape
    n_rows = -(-(TOP_K * t) // tm) * tm + n_e * tm
    n_tiles = n_rows // tm
    gate_t, dest, tiles, pads = _routing(logits_t, tm, n_tiles)
    dest = dest.reshape(-1)
    tile_e = tiles[0, :n_tiles]
    n_used_rows = tiles[1, :1]
    n_used_tiles = n_used_rows // tm
    xin = _dispatch(h2p, dest, pads[0, :, 0], pads[1, :, 0], n_used_rows, n_rows, tm)
    act = _gemm1(xin, w1, b1, tile_e, n_used_tiles, tm)
    y = _gemm2(act, w2, b2, tile_e, n_used_tiles, tm)
    return _combine(dest, x1, gate_t.T, gt_f, ln_g, ln_b, y, seq, alpha)


def kernel(x, c, positions, w_ada, b_ada, w_in, hgrn_lb, gnorm_w, w_o, ln1_g, ln1_b,
           router_w, router_b, w1, b1, w2, b2, ln2_g, ln2_b):
    bsz, seq, d = x.shape
    depth = w_ada.shape[0]
    t = bsz * seq
    att_w = d // 2
    hg_w = d - att_w
    alpha = (2.0 * depth) ** 0.25

    lb_all = jnp.cumsum(jax.nn.softmax(hgrn_lb.astype(F32), axis=0), axis=0)
    inv = ROPE_THETA ** (-(jnp.arange(0, ROT_DIM, 2, dtype=F32) / ROT_DIM))
    lane = np.arange(V7X_LANES)
    inv_lane = inv[(lane % ATT_HEAD_DIM) % (ROT_DIM // 2)].reshape(1, V7X_LANES)
    pos_b = jnp.broadcast_to(positions.astype(F32).reshape(t, 1), (t, V7X_LANES))

    x2d = x.reshape(t, d)
    for l in range(depth):
        mod = _adaln(c, w_ada[l], b_ada[l])
        sh_a, sc_a, gt_a, sh_f, sc_f, gt_f = jnp.split(mod, 6, axis=-1)

        proj = _in_proj(x2d, sc_a, sh_a, w_in[l], seq)
        proj3 = proj.reshape(bsz, seq, proj.shape[1])
        q_hp, k_hp = _qk_rope(proj, pos_b, inv_lane, bsz, seq, att_w)
        att = _attention(q_hp, k_hp, proj3, att_w).reshape(t, att_w)
        rec = _hgrn2(proj3, lb_all[l], gnorm_w[l], att_w, hg_w).reshape(t, hg_w)

        x1, h2p, logits_t = _out_proj(att, rec, x2d, w_o[l], gt_a, sc_f, sh_f, ln1_g[l], ln1_b[l],
                                      router_w[l].T, router_b[l], seq, alpha)

        x2d = _moe_ffn(h2p, logits_t, x1, gt_f, ln2_g[l], ln2_b[l], w1[l], b1[l], w2[l], b2[l],
                       seq, alpha)
    return x2d.reshape(bsz, seq, d)
```

```python
import functools

import numpy as np
import jax
import jax.numpy as jnp
from jax import lax
from jax.experimental import pallas as pl
from jax.experimental.pallas import tpu as pltpu

F32 = jnp.float32
BF16 = jnp.bfloat16

V7X_LANES = 128
V7X_MXU_DIM = 256
V7X_VMEM_LIMIT = 56 * 1024 * 1024

ATT_HEAD_DIM = 64
DILATED_PAIRS = ((128, 1), (512, 4), (2048, 16))
ATT_BLOCK = 128
ROT_DIM = ATT_HEAD_DIM // 4
ROPE_THETA = 500000.0
HG_EXPAND = 128
HG_CHUNK = 64
HG_SUB = 16
TOP_K = 4
SWIGLU_ALPHA = 1.702
SWIGLU_LIMIT = 7.0
LN_EPS = 1e-5
RMS_EPS = 1e-6
NEG_INF = -1e30

NT_DIMS = (((1,), (1,)), ((), ()))


def _pick(n, candidates):
    for c in candidates:
        if n % c == 0:
            return c
    raise ValueError(f"no tile in {candidates} divides {n}")


def _cparams(sem, vmem=V7X_VMEM_LIMIT):
    return pltpu.CompilerParams(dimension_semantics=sem, vmem_limit_bytes=vmem)


def _adaln_kernel(c_ref, w_ref, b_ref, o_ref):
    c = c_ref[...]
    s = c * jax.nn.sigmoid(c)
    o_ref[...] = jnp.dot(s.astype(BF16), w_ref[...].astype(BF16),
                         preferred_element_type=F32) + b_ref[...]


def _adaln(c, w, b):
    bsz, d = c.shape
    n = w.shape[1]
    rows = 8
    cp = jnp.zeros((rows, d), F32).at[:bsz].set(c)
    tn = _pick(n, (1024, 512, 256, 128))
    out = pl.pallas_call(
        _adaln_kernel,
        grid=(n // tn,),
        in_specs=[pl.BlockSpec((rows, d), lambda j: (0, 0)),
                  pl.BlockSpec((d, tn), lambda j: (0, j)),
                  pl.BlockSpec((1, tn), lambda j: (0, j))],
        out_specs=pl.BlockSpec((rows, tn), lambda j: (0, j)),
        out_shape=jax.ShapeDtypeStruct((rows, n), F32),
        compiler_params=_cparams(("arbitrary",)),
        name="adaln",
    )(cp, w, b.reshape(1, n))
    return out[:bsz]


def _in_proj_kernel(x_ref, sc_ref, sh_ref, w_ref, o_ref, wbf_ref):
    @pl.when(pl.program_id(1) == 0)
    def _():
        wbf_ref[...] = w_ref[...].astype(BF16)

    h = x_ref[...] * (1.0 + sc_ref[0]) + sh_ref[0]
    o_ref[...] = jnp.dot(h.astype(BF16), wbf_ref[...], preferred_element_type=F32)


def _in_proj(x2d, sc, sh, w, seq):
    t, d = x2d.shape
    nc = w.shape[1]
    tm = _pick(seq, (512, 256, 128))
    tn = _pick(nc, (1024, 512, 256, 128))
    per_b = seq // tm
    vec = pl.BlockSpec((1, 1, d), lambda j, i: (i // per_b, 0, 0))
    return pl.pallas_call(
        _in_proj_kernel,
        grid=(nc // tn, t // tm),
        in_specs=[pl.BlockSpec((tm, d), lambda j, i: (i, 0)), vec, vec,
                  pl.BlockSpec((d, tn), lambda j, i: (0, j))],
        out_specs=pl.BlockSpec((tm, tn), lambda j, i: (i, j)),
        out_shape=jax.ShapeDtypeStruct((t, nc), F32),
        scratch_shapes=[pltpu.VMEM((d, tn), BF16)],
        compiler_params=_cparams(("arbitrary", "arbitrary")),
        name="in_proj",
    )(x2d, sc[:, None, :], sh[:, None, :], w)


def _rope_kernel(q_ref, k_ref, pos_ref, inv_ref, qo_ref, ko_ref, cos_ref, sin_ref):
    lane = lax.broadcasted_iota(jnp.int32, cos_ref.shape, 1)
    lh = lane % ATT_HEAD_DIM
    half = ROT_DIM // 2

    @pl.when(pl.program_id(1) == 0)
    def _():
        ang = pos_ref[...] * inv_ref[...]
        cos_ref[...] = jnp.where(lh < ROT_DIM, jnp.cos(ang), 1.0)
        sn = jnp.sin(ang)
        sin_ref[...] = jnp.where(lh < half, -sn, jnp.where(lh < ROT_DIM, sn, 0.0))

    cs = cos_ref[...]
    sn = sin_ref[...]

    def rope(t):
        swapped = jnp.where(lh < half,
                            pltpu.roll(t, V7X_LANES - half, axis=1),
                            pltpu.roll(t, half, axis=1))
        return t * cs + swapped * sn

    qo_ref[0, 0] = rope(q_ref[...]) * (ATT_HEAD_DIM ** -0.5)
    ko_ref[0, 0] = rope(k_ref[...])


def _qk_rope(proj, pos_b, inv_lane, bsz, seq, att_w):
    t = proj.shape[0]
    hp = att_w // V7X_LANES
    tm = _pick(seq, (512, 256, 128))
    per_b = seq // tm
    out_spec = pl.BlockSpec((1, 1, tm, V7X_LANES),
                            lambda i, h: (i // per_b, h, i % per_b, 0))
    shp = jax.ShapeDtypeStruct((bsz, hp, seq, V7X_LANES), F32)
    return pl.pallas_call(
        _rope_kernel,
        grid=(t // tm, hp),
        in_specs=[pl.BlockSpec((tm, V7X_LANES), lambda i, h: (i, h)),
                  pl.BlockSpec((tm, V7X_LANES), lambda i, h: (i, hp + h)),
                  pl.BlockSpec((tm, V7X_LANES), lambda i, h: (i, 0)),
                  pl.BlockSpec((1, V7X_LANES), lambda i, h: (0, 0))],
        out_specs=[out_spec, out_spec],
        out_shape=[shp, shp],
        scratch_shapes=[pltpu.VMEM((tm, V7X_LANES), F32), pltpu.VMEM((tm, V7X_LANES), F32)],
        compiler_params=_cparams(("arbitrary", "arbitrary")),
        name="qk_rope",
    )(proj, proj, pos_b, inv_lane)


def _attn_block(q_blk, k_cat, v_cat, first):
    qn = q_blk.shape[0]
    nk = k_cat.shape[0]
    qi = lax.broadcasted_iota(jnp.int32, (qn, nk), 0)
    kj = lax.broadcasted_iota(jnp.int32, (qn, nk), 1)
    dist = qi - kj if first else qi + qn - kj
    w_steps = qn
    valid = (dist >= 0) & (dist <= w_steps)
    lane = lax.broadcasted_iota(jnp.int32, (nk, V7X_LANES), 1)
    qb = q_blk.astype(BF16)
    vb = v_cat.astype(BF16)
    outs, lses = [], []
    for head in range(V7X_LANES // ATT_HEAD_DIM):
        in_head = (lane // ATT_HEAD_DIM) == head
        kh = jnp.where(in_head, k_cat, 0.0).astype(BF16)
        s = lax.dot_general(qb, kh, NT_DIMS, preferred_element_type=F32)
        s = jnp.where(valid, s, NEG_INF)
        m = jnp.max(s, axis=-1, keepdims=True)
        p = jnp.exp(s - m)
        den = jnp.sum(p, axis=-1, keepdims=True)
        o = jnp.dot(p.astype(BF16), vb, preferred_element_type=F32) / den
        outs.append(o)
        lses.append(m + jnp.log(den))
    olane = lax.broadcasted_iota(jnp.int32, (qn, V7X_LANES), 1)
    o = outs[-1]
    lse = jnp.broadcast_to(lses[-1], (qn, V7X_LANES))
    for head in range(len(outs) - 2, -1, -1):
        sel = (olane // ATT_HEAD_DIM) == head
        o = jnp.where(sel, outs[head], o)
        lse = jnp.where(sel, lses[head], lse)
    return o, lse


ATT_UNROLL = 4


def _attn_kernel(q_ref, k_ref, v_ref, out_ref, o_scr, l_scr, *, seq):
    qn = ATT_BLOCK

    def rows(start, n, d):
        return pl.ds(start, n) if d == 1 else pl.ds(start, n, stride=d)

    for bi, (window, d) in enumerate(DILATED_PAIRS):
        nb = seq // (d * qn)

        def first_block(r, carry, d=d, bi=bi):
            sl = rows(r, qn, d)
            o, lse = _attn_block(q_ref[0, 0, sl, :], k_ref[0, 0, sl, :], v_ref[0, sl, :], True)
            o_scr[bi, sl, :] = o
            l_scr[bi, sl, :] = lse
            return carry

        def later_block(idx, carry, d=d, bi=bi, nb=nb):
            r = idx // (nb - 1)
            n = idx % (nb - 1) + 1
            qsl = rows(r + n * qn * d, qn, d)
            ksl = rows(r + (n - 1) * qn * d, 2 * qn, d)
            o, lse = _attn_block(q_ref[0, 0, qsl, :], k_ref[0, 0, ksl, :], v_ref[0, ksl, :], False)
            o_scr[bi, qsl, :] = o
            l_scr[bi, qsl, :] = lse
            return carry

        lax.fori_loop(0, d, first_block, 0, unroll=min(d, ATT_UNROLL))
        if nb > 1:
            trips = d * (nb - 1)
            lax.fori_loop(0, trips, later_block, 0,
                          unroll=max(u for u in range(1, ATT_UNROLL + 1) if trips % u == 0))

    def combine(i, carry):
        sl = pl.ds(pl.multiple_of(i * qn, qn), qn)
        ls = [l_scr[bi, sl, :] for bi in range(len(DILATED_PAIRS))]
        mx = functools.reduce(jnp.maximum, ls)
        ws = [jnp.exp(l - mx) for l in ls]
        tot = functools.reduce(lambda a, b: a + b, ws)
        acc = functools.reduce(lambda a, b: a + b,
                               [w * o_scr[bi, sl, :] for bi, w in enumerate(ws)])
        out_ref[0, sl, :] = (acc / tot).astype(out_ref.dtype)
        return carry

    lax.fori_loop(0, seq // qn, combine, 0)


def _attention(q_hp, k_hp, proj3, att_w):
    bsz, hp, seq, _ = q_hp.shape
    for window, d in DILATED_PAIRS:
        assert window // d == ATT_BLOCK and seq % (d * ATT_BLOCK) == 0
    v_blk0 = 2 * att_w // V7X_LANES
    nbr = len(DILATED_PAIRS)
    qk_spec = pl.BlockSpec((1, 1, seq, V7X_LANES), lambda b, h: (b, h, 0, 0))
    return pl.pallas_call(
        functools.partial(_attn_kernel, seq=seq),
        grid=(bsz, hp),
        in_specs=[qk_spec, qk_spec,
                  pl.BlockSpec((1, seq, V7X_LANES), lambda b, h: (b, 0, v_blk0 + h))],
        out_specs=pl.BlockSpec((1, seq, V7X_LANES), lambda b, h: (b, 0, h)),
        out_shape=jax.ShapeDtypeStruct((bsz, seq, att_w), BF16),
        scratch_shapes=[pltpu.VMEM((nbr, seq, V7X_LANES), F32),
                        pltpu.VMEM((nbr, seq, V7X_LANES), F32)],
        compiler_params=_cparams(("arbitrary", "arbitrary")),
        name="dilated_attn",
    )(q_hp, k_hp, proj3)


def _hgrn_kernel(q_ref, f_ref, i_ref, g_ref, lb_ref, gw_ref, tri_ref, o_ref, st_ref, *, heads, ts):
    @pl.when(pl.program_id(2) == 0)
    def _():
        st_ref[...] = jnp.zeros_like(st_ref)

    c_len = HG_CHUNK
    tri = tri_ref[...]

    def chunk(ci, carry):
        r0 = pl.multiple_of(ci * c_len, c_len)
        rsl = pl.ds(r0, c_len)
        for h in range(heads):
            lsl = slice(h * HG_EXPAND, (h + 1) * HG_EXPAND)
            lb = lb_ref[:, lsl]
            f = lb + (1.0 - lb) * jax.nn.sigmoid(f_ref[0, rsl, lsl])
            logf = jnp.log(f)
            kk = 1.0 - f
            q = q_ref[0, rsl, lsl]
            v = i_ref[0, rsl, lsl]
            b = jnp.dot(tri, logf, precision=lax.Precision.HIGHEST, preferred_element_type=F32)
            b_last = b[c_len - 1:c_len, :]
            st = st_ref[h]
            o_inter = lax.dot_general((q * jnp.exp(b)).astype(BF16), st.astype(BF16), NT_DIMS,
                                      preferred_element_type=F32)
            vb = v.astype(BF16)
            o_rows = []
            for i in range(c_len // HG_SUB):
                lo, hi = i * HG_SUB, (i + 1) * HG_SUB
                mid = lo + HG_SUB // 2
                anchor = b[mid:mid + 1, :]
                qt = q[lo:hi] * jnp.exp(b[lo:hi] - anchor)
                kt = kk[:hi] * jnp.exp(anchor - b[:hi])
                a = lax.dot_general(qt.astype(BF16), kt.astype(BF16), NT_DIMS,
                                    preferred_element_type=F32)
                ti = lax.broadcasted_iota(jnp.int32, (HG_SUB, hi), 0) + lo
                si = lax.broadcasted_iota(jnp.int32, (HG_SUB, hi), 1)
                a = jnp.where(si <= ti, a, 0.0)
                o_rows.append(jnp.dot(a.astype(BF16), vb[:hi], preferred_element_type=F32))
            o = o_inter + jnp.concatenate(o_rows, axis=0)
            kl = kk * jnp.exp(b_last - b)
            upd = jnp.dot(v.T.astype(BF16), kl.astype(BF16), preferred_element_type=F32)
            st_ref[h] = st * jnp.exp(b_last) + upd
            o = o * lax.rsqrt(jnp.mean(o * o, axis=-1, keepdims=True) + RMS_EPS)
            g = g_ref[0, rsl, lsl]
            o = o * gw_ref[:, lsl] * (g * jax.nn.sigmoid(g))
            o_ref[0, rsl, lsl] = o.astype(o_ref.dtype)
        return carry

    lax.fori_loop(0, ts // c_len, chunk, 0, unroll=True)


def _hgrn2(proj3, lb, gw, att_w, hg_w):
    bsz, seq, _ = proj3.shape
    n_heads = hg_w // HG_EXPAND
    heads = _pick(n_heads, (4, 2, 1))
    lanes = heads * HG_EXPAND
    ts = _pick(seq, (256, 128, 64))
    base = 3 * att_w
    assert base % lanes == 0 and hg_w % lanes == 0

    def seg(k):
        off = (base + k * hg_w) // lanes
        return pl.BlockSpec((1, ts, lanes), lambda b, h, s: (b, s, off + h))

    vec = pl.BlockSpec((1, lanes), lambda b, h, s: (0, h))
    tri = jnp.asarray(np.tril(np.ones((HG_CHUNK, HG_CHUNK), np.float32)))
    return pl.pallas_call(
        functools.partial(_hgrn_kernel, heads=heads, ts=ts),
        grid=(bsz, hg_w // lanes, seq // ts),
        in_specs=[seg(0), seg(1), seg(2), seg(3), vec, vec,
                  pl.BlockSpec((HG_CHUNK, HG_CHUNK), lambda b, h, s: (0, 0))],
        out_specs=pl.BlockSpec((1, ts, lanes), lambda b, h, s: (b, s, h)),
        out_shape=jax.ShapeDtypeStruct((bsz, seq, hg_w), BF16),
        scratch_shapes=[pltpu.VMEM((heads, HG_EXPAND, HG_EXPAND), F32)],
        compiler_params=_cparams(("arbitrary", "arbitrary", "arbitrary")),
        name="hgrn2",
    )(proj3, proj3, proj3, proj3, lb.reshape(1, hg_w), gw.reshape(1, hg_w), tri)


def _layer_norm(y, g, b):
    mu = jnp.mean(y, axis=-1, keepdims=True)
    yc = y - mu
    var = jnp.mean(yc * yc, axis=-1, keepdims=True)
    return yc * lax.rsqrt(var + LN_EPS) * g + b


def _bf16_split(x):
    hi = x.astype(BF16)
    lo = (x - hi.astype(F32)).astype(BF16)
    return hi, lo


def _out_proj_kernel(att_ref, rec_ref, x_ref, wo_ref, gt_ref, sc_ref, sh_ref, g_ref, b_ref,
                     rw_ref, rb_ref, x1_ref, h2_ref, lg_ref, wobf_ref, rwhi_ref, rwlo_ref,
                     *, alpha, att_w):
    @pl.when(pl.program_id(0) == 0)
    def _():
        wobf_ref[...] = wo_ref[...].astype(BF16)
        hi, lo = _bf16_split(rw_ref[...])
        rwhi_ref[...] = hi
        rwlo_ref[...] = lo

    mix = (jnp.dot(att_ref[...], wobf_ref[:att_w, :], preferred_element_type=F32)
           + jnp.dot(rec_ref[...], wobf_ref[att_w:, :], preferred_element_type=F32))
    y = alpha * x_ref[...] + (1.0 + gt_ref[0]) * mix
    x1 = _layer_norm(y, g_ref[...], b_ref[...])
    x1_ref[...] = x1
    h2 = x1 * (1.0 + sc_ref[0]) + sh_ref[0]
    hi, lo = _bf16_split(h2)
    lg = (lax.dot_general(rwhi_ref[...], hi, NT_DIMS, preferred_element_type=F32)
          + lax.dot_general(rwhi_ref[...], lo, NT_DIMS, preferred_element_type=F32)
          + lax.dot_general(rwlo_ref[...], hi, NT_DIMS, preferred_element_type=F32))
    lg_ref[...] = lg + rb_ref[...]
    half = h2.shape[1] // 2
    bits = lax.bitcast_convert_type(hi.astype(F32), jnp.uint32)
    h2_ref[...] = (bits[:, :half] >> 16) | (bits[:, half:] & jnp.uint32(0xFFFF0000))


def _out_proj(att, rec, x2d, wo, gt, sc, sh, g, b, rw_t, rb, seq, alpha):
    t, d = x2d.shape
    att_w = att.shape[1]
    n_e = rw_t.shape[0]
    tm = _pick(seq, (256, 128))
    per_b = seq // tm
    vec3 = pl.BlockSpec((1, 1, d), lambda i: (i // per_b, 0, 0))
    full = lambda shape: pl.BlockSpec(shape, lambda i: (0,) * len(shape))
    once = lambda shape: pl.BlockSpec(shape, lambda i: (0,) * len(shape),
                                      pipeline_mode=pl.Buffered(1))
    return pl.pallas_call(
        functools.partial(_out_proj_kernel, alpha=alpha, att_w=att_w),
        grid=(t // tm,),
        in_specs=[pl.BlockSpec((tm, att_w), lambda i: (i, 0)),
                  pl.BlockSpec((tm, d - att_w), lambda i: (i, 0)),
                  pl.BlockSpec((tm, d), lambda i: (i, 0)),
                  once((d, d)), vec3, vec3, vec3, full((1, d)), full((1, d)),
                  full((n_e, d)), full((n_e, 1))],
        out_specs=[pl.BlockSpec((tm, d), lambda i: (i, 0)),
                   pl.BlockSpec((tm, d // 2), lambda i: (i, 0)),
                   pl.BlockSpec((n_e, tm), lambda i: (0, i))],
        out_shape=[jax.ShapeDtypeStruct((t, d), F32),
                   jax.ShapeDtypeStruct((t, d // 2), jnp.uint32),
                   jax.ShapeDtypeStruct((n_e, t), F32)],
        scratch_shapes=[pltpu.VMEM((d, d), BF16), pltpu.VMEM((n_e, d), BF16),
                        pltpu.VMEM((n_e, d), BF16)],
        compiler_params=_cparams(("arbitrary",)),
        name="out_proj_ln1",
    )(att, rec, x2d, wo, gt[:, None, :], sc[:, None, :], sh[:, None, :],
      g.reshape(1, d), b.reshape(1, d), rw_t, rb.reshape(n_e, 1))


def _routing_kernel(lg_ref, tri_ref, low_ref, gate_ref, dest_ref, tile_ref, pad_ref, sel_scr, rk_scr,
                    *, blk, tm):
    n_e, t = lg_ref.shape
    eidx = lax.broadcasted_iota(jnp.int32, (n_e, t), 0)
    cur = lg_ref[...]
    vals, idxs = [], []
    for _ in range(TOP_K):
        m = jnp.max(cur, axis=0, keepdims=True)
        ik = jnp.min(jnp.where(cur == m, eidx, n_e), axis=0, keepdims=True)
        cur = jnp.where(eidx == ik, -jnp.inf, cur)
        vals.append(m)
        idxs.append(ik)
    es = [jnp.exp(v - vals[0]) for v in vals]
    den = functools.reduce(lambda a, b: a + b, es)
    for k in range(TOP_K):
        gate_ref[k:k + 1, :] = es[k] / den
    sel = functools.reduce(lambda a, b: a | b, [eidx == ik for ik in idxs])
    sel_scr[...] = jnp.where(sel, 1.0, 0.0)

    tri = tri_ref[...]
    carry = jnp.zeros((n_e, 1), F32)
    for j in range(t // blk):
        sb = sel_scr[:, j * blk:(j + 1) * blk]
        pre = jnp.dot(sb.astype(BF16), tri, preferred_element_type=F32)
        rk_scr[:, j * blk:(j + 1) * blk] = pre + carry
        carry = carry + jnp.sum(sb, axis=1, keepdims=True)

    counts = jnp.broadcast_to(carry, (n_e, V7X_LANES))
    padded = jnp.floor((counts + (tm - 1)) * (1.0 / tm)) * tm
    pends = jnp.dot(low_ref[...], padded, precision=lax.Precision.HIGHEST,
                    preferred_element_type=F32)
    pstarts = pends - padded
    row0 = rk_scr[...] + pstarts[:, 0:1]
    for k in range(TOP_K):
        dest_ref[k:k + 1, :] = jnp.sum(jnp.where(eidx == idxs[k], row0, 0.0), axis=0,
                                       keepdims=True).astype(jnp.int32)
    starts = (lax.broadcasted_iota(jnp.int32, (n_e, tile_ref.shape[1]), 1) * tm).astype(F32)
    owner = jnp.sum(jnp.where(pends[:, 0:1] <= starts, 1.0, 0.0), axis=0, keepdims=True)
    tile_ref[0:1, :] = jnp.minimum(owner, n_e - 1.0).astype(jnp.int32)
    tile_ref[1:2, :] = jnp.broadcast_to(jnp.max(pends[:, 0:1], axis=0, keepdims=True),
                                        (1, tile_ref.shape[1])).astype(jnp.int32)
    pad_ref[0] = (pstarts + counts).astype(jnp.int32)
    pad_ref[1] = (padded - counts).astype(jnp.int32)


def _routing(logits_t, tm, n_tiles):
    n_e, t = logits_t.shape
    assert tm & (tm - 1) == 0
    blk = _pick(t, (256, 128))
    tri = jnp.asarray(np.triu(np.ones((blk, blk), np.float32), 1), BF16)
    low = jnp.asarray(np.tril(np.ones((n_e, n_e), np.float32)))
    ntp = -(-n_tiles // V7X_LANES) * V7X_LANES
    full = lambda shape: pl.BlockSpec(shape, lambda: (0,) * len(shape))
    return pl.pallas_call(
        functools.partial(_routing_kernel, blk=blk, tm=tm),
        in_specs=[full((n_e, t)), full((blk, blk)), full((n_e, n_e))],
        out_specs=[full((TOP_K, t)), full((TOP_K, t)), full((2, ntp)), full((2, n_e, V7X_LANES))],
        out_shape=[jax.ShapeDtypeStruct((TOP_K, t), F32),
                   jax.ShapeDtypeStruct((TOP_K, t), jnp.int32),
                   jax.ShapeDtypeStruct((2, ntp), jnp.int32),
                   jax.ShapeDtypeStruct((2, n_e, V7X_LANES), jnp.int32)],
        scratch_shapes=[pltpu.VMEM((n_e, t), F32), pltpu.VMEM((n_e, t), F32)],
        compiler_params=pltpu.CompilerParams(vmem_limit_bytes=V7X_VMEM_LIMIT),
        name="routing",
    )(logits_t, tri, low)


def _dispatch_kernel(dest_ref, pad_off_ref, pad_n_ref, nrow_ref, h_ref, o_ref, zero_ref, sem, zsem,
                     *, n_tok, tq, tm, n_e):
    step = pl.program_id(0)
    base = step * tq

    def fill(off, n):
        return pltpu.make_async_copy(zero_ref.at[pl.ds(0, n)], o_ref.at[pl.ds(off, n)], zsem)

    sub = 8
    pieces = [1 << s for s in range(tm.bit_length() - 2, sub.bit_length() - 2, -1)]

    def pad_rows(wait):
        def per_expert(e, c):
            off = pad_off_ref[e]
            n = pad_n_ref[e]
            head = jnp.minimum(n, (-off) & (sub - 1))
            for s in range(sub - 1):
                @pl.when(s < head)
                def _(s=s):
                    cp = fill(off + s, 1)
                    cp.wait() if wait else cp.start()
            off = off + head
            n = n - head
            for p in pieces:
                hit = (n & p) != 0

                @pl.when(hit)
                def _(off=off, p=p):
                    cp = fill(pl.multiple_of(off, sub), p)
                    cp.wait() if wait else cp.start()

                off = off + jnp.where(hit, p, 0)
            return c
        lax.fori_loop(0, n_e, per_expert, 0)

        def per_tile(i, c):
            cp = fill(pl.multiple_of(i * tm, tm), tm)
            cp.wait() if wait else cp.start()
            return c
        lax.fori_loop(nrow_ref[0] // tm, o_ref.shape[0] // tm, per_tile, 0)

    @pl.when(step == 0)
    def _():
        zero_ref[...] = jnp.zeros_like(zero_ref)
        pad_rows(False)

    def copy(r, k):
        dst = o_ref.at[pl.ds(dest_ref[k * n_tok + base + r], 1)]
        return pltpu.make_async_copy(h_ref.at[pl.ds(r, 1)], dst, sem)

    def start(r, c):
        for k in range(TOP_K):
            copy(r, k).start(priority=k % 2)
        return c

    def wait(r, c):
        for k in range(TOP_K):
            copy(r, k).wait()
        return c

    lax.fori_loop(0, tq, start, 0, unroll=2)
    lax.fori_loop(0, tq, wait, 0, unroll=2)

    @pl.when(step == 0)
    def _():
        pad_rows(True)


def _dispatch(h2p, dest_flat, pad_off, pad_n, n_used_rows, n_rows, tm):
    t, width = h2p.shape
    n_e = pad_off.shape[0]
    tq = _pick(t, (256, 128))
    return pl.pallas_call(
        functools.partial(_dispatch_kernel, n_tok=t, tq=tq, tm=tm, n_e=n_e),
        grid_spec=pltpu.PrefetchScalarGridSpec(
            num_scalar_prefetch=4, grid=(t // tq,),
            in_specs=[pl.BlockSpec((tq, width), lambda i, *_: (i, 0))],
            out_specs=pl.BlockSpec(memory_space=pl.ANY),
            scratch_shapes=[pltpu.VMEM((tm, width), h2p.dtype),
                            pltpu.SemaphoreType.DMA(()), pltpu.SemaphoreType.DMA(())]),
        out_shape=jax.ShapeDtypeStruct((n_rows, width), h2p.dtype),
        compiler_params=pltpu.CompilerParams(dimension_semantics=("arbitrary",),
                                             has_side_effects=True,
                                             vmem_limit_bytes=V7X_VMEM_LIMIT),
        name="moe_dispatch",
    )(dest_flat, pad_off, pad_n, n_used_rows, h2p)


def _unpack_bf16_pair(words):
    lo = lax.bitcast_convert_type(words << 16, F32).astype(BF16)
    hi = lax.bitcast_convert_type(words & jnp.uint32(0xFFFF0000), F32).astype(BF16)
    return lo, hi


def _gemm1_kernel(te_ref, nt_ref, x_ref, w_ref, b_ref, cmp_ref, o_ref, wbf_ref):
    i = pl.program_id(1)
    prev = te_ref[jnp.maximum(i - 1, 0)]

    @pl.when((i < nt_ref[0]) & ((i == 0) | (te_ref[i] != prev)))
    def _():
        wbf_ref[...] = w_ref[0].astype(BF16)

    @pl.when(i < nt_ref[0])
    def _():
        x_lo, x_hi = _unpack_bf16_pair(x_ref[...])
        half = x_lo.shape[1]
        hb = (jnp.dot(x_lo, wbf_ref[:half, :], preferred_element_type=F32)
              + jnp.dot(x_hi, wbf_ref[half:, :], preferred_element_type=F32) + b_ref[0])
        tn = hb.shape[1]
        nxt = pltpu.roll(hb, tn - 1, axis=1)
        glu = jnp.minimum(hb, SWIGLU_LIMIT)
        lin = jnp.clip(nxt, -SWIGLU_LIMIT, SWIGLU_LIMIT)
        act = glu * jax.nn.sigmoid(SWIGLU_ALPHA * glu) * (lin + 1.0)
        lane = lax.broadcasted_iota(jnp.int32, hb.shape, 1)
        act = jnp.where(lane % 2 == 0, act, 0.0).astype(BF16)
        cw = cmp_ref.shape[0]
        for c in range(tn // cw):
            o_ref[:, c * (cw // 2):(c + 1) * (cw // 2)] = jnp.dot(
                act[:, c * cw:(c + 1) * cw], cmp_ref[...],
                preferred_element_type=F32).astype(o_ref.dtype)

    @pl.when(i >= nt_ref[0])
    def _():
        o_ref[...] = jnp.zeros_like(o_ref)


def _gemm1(xin, w1, b1, tile_e, n_tiles_used, tm):
    n_rows, half_d = xin.shape
    n_e, d, f2 = w1.shape
    tn = _pick(f2, (1024, 512, 256))
    n_tiles = n_rows // tm
    cw = V7X_MXU_DIM
    cmp_np = np.zeros((cw, cw // 2), np.float32)
    cmp_np[np.arange(0, cw, 2), np.arange(cw // 2)] = 1.0

    def used(i, nt):
        return jnp.minimum(i, nt[0] - 1)

    return pl.pallas_call(
        _gemm1_kernel,
        grid_spec=pltpu.PrefetchScalarGridSpec(
            num_scalar_prefetch=2, grid=(f2 // tn, n_tiles),
            in_specs=[pl.BlockSpec((tm, half_d), lambda j, i, te, nt: (used(i, nt), 0)),
                      pl.BlockSpec((1, d, tn), lambda j, i, te, nt: (te[used(i, nt)], 0, j)),
                      pl.BlockSpec((1, 1, tn), lambda j, i, te, nt: (te[used(i, nt)], 0, j)),
                      pl.BlockSpec((cw, cw // 2), lambda j, i, te, nt: (0, 0))],
            out_specs=pl.BlockSpec((tm, tn // 2), lambda j, i, te, nt: (i, j)),
            scratch_shapes=[pltpu.VMEM((d, tn), BF16)]),
        out_shape=jax.ShapeDtypeStruct((n_rows, f2 // 2), BF16),
        compiler_params=_cparams(("arbitrary", "arbitrary")),
        name="moe_gemm1",
    )(tile_e, n_tiles_used, xin, w1, b1.reshape(n_e, 1, f2), jnp.asarray(cmp_np, BF16))


def _gemm2_kernel(te_ref, nt_ref, a_ref, w_ref, b_ref, o_ref, wbf_ref):
    i = pl.program_id(1)
    prev = te_ref[jnp.maximum(i - 1, 0)]

    @pl.when((i < nt_ref[0]) & ((i == 0) | (te_ref[i] != prev)))
    def _():
        wbf_ref[...] = w_ref[0].astype(BF16)

    @pl.when(i < nt_ref[0])
    def _():
        o_ref[...] = jnp.dot(a_ref[...], wbf_ref[...], preferred_element_type=F32) + b_ref[0]

    @pl.when(i >= nt_ref[0])
    def _():
        o_ref[...] = jnp.zeros_like(o_ref)


def _gemm2(act, w2, b2, tile_e, n_tiles_used, tm):
    n_rows, f = act.shape
    n_e, _, d = w2.shape
    tn = _pick(d, (1024, 512, 256))
    n_tiles = n_rows // tm

    def used(i, nt):
        return jnp.minimum(i, nt[0] - 1)

    return pl.pallas_call(
        _gemm2_kernel,
        grid_spec=pltpu.PrefetchScalarGridSpec(
            num_scalar_prefetch=2, grid=(d // tn, n_tiles),
            in_specs=[pl.BlockSpec((tm, f), lambda j, i, te, nt: (used(i, nt), 0)),
                      pl.BlockSpec((1, f, tn), lambda j, i, te, nt: (te[used(i, nt)], 0, j)),
                      pl.BlockSpec((1, 1, tn), lambda j, i, te, nt: (te[used(i, nt)], 0, j))],
            out_specs=pl.BlockSpec((tm, tn), lambda j, i, te, nt: (i, j)),
            scratch_shapes=[pltpu.VMEM((f, tn), BF16)]),
        out_shape=jax.ShapeDtypeStruct((n_rows, d), F32),
        compiler_params=_cparams(("arbitrary", "arbitrary")),
        name="moe_gemm2",
    )(tile_e, n_tiles_used, act, w2, b2.reshape(n_e, 1, d))


def _combine_kernel(dest_ref, x1_ref, gate_ref, gt_ref, g_ref, b_ref, y_ref, o_ref, buf, sem,
                    *, alpha, tc, n_tok):
    i = pl.program_id(0)
    n_steps = pl.num_programs(0)

    def copy(step, slot, k, r):
        src = y_ref.at[pl.ds(dest_ref[k * n_tok + step * tc + r], 1)]
        return pltpu.make_async_copy(src, buf.at[slot, k, pl.ds(r, 1)], sem.at[slot])

    def issue(step, slot):
        def body(r, c):
            for k in range(TOP_K):
                copy(step, slot, k, r).start()
            return c
        lax.fori_loop(0, tc, body, 0)

    def drain(step, slot):
        def body(r, c):
            for k in range(TOP_K):
                copy(step, slot, k, r).wait()
            return c
        lax.fori_loop(0, tc, body, 0)

    slot = i % 2

    @pl.when(i == 0)
    def _():
        issue(0, 0)

    @pl.when(i + 1 < n_steps)
    def _():
        issue(i + 1, 1 - slot)

    drain(i, slot)
    gates = gate_ref[...]
    ff = gates[:, 0:1] * buf[slot, 0]
    for k in range(1, TOP_K):
        ff = ff + gates[:, k:k + 1] * buf[slot, k]
    y = alpha * x1_ref[...] + (1.0 + gt_ref[0]) * ff
    o_ref[...] = _layer_norm(y, g_ref[...], b_ref[...])


def _combine(dest_flat, x1, gates_tk, gt, g, b, y, seq, alpha):
    t, d = x1.shape
    tc = _pick(seq, (128,))
    per_b = seq // tc
    return pl.pallas_call(
        functools.partial(_combine_kernel, alpha=alpha, tc=tc, n_tok=t),
        grid_spec=pltpu.PrefetchScalarGridSpec(
            num_scalar_prefetch=1, grid=(t // tc,),
            in_specs=[pl.BlockSpec((tc, d), lambda i, ds: (i, 0)),
                      pl.BlockSpec((tc, TOP_K), lambda i, ds: (i, 0)),
                      pl.BlockSpec((1, 1, d), lambda i, ds: (i // per_b, 0, 0)),
                      pl.BlockSpec((1, d), lambda i, ds: (0, 0)),
                      pl.BlockSpec((1, d), lambda i, ds: (0, 0)),
                      pl.BlockSpec(memory_space=pl.ANY)],
            out_specs=pl.BlockSpec((tc, d), lambda i, ds: (i, 0)),
            scratch_shapes=[pltpu.VMEM((2, TOP_K, tc, d), F32),
                            pltpu.SemaphoreType.DMA((2,))]),
        out_shape=jax.ShapeDtypeStruct((t, d), F32),
        compiler_params=_cparams(("arbitrary",)),
        name="moe_combine_ln2",
    )(dest_flat, x1, gates_tk, gt[:, None, :], g.reshape(1, d), b.reshape(1, d), y)


MOE_ROW_TILE = 512


def _moe_ffn(h2p, logits_t, x1, gt_f, ln_g, ln_b, w1, b1, w2, b2, seq, alpha):
    tm = MOE_ROW_TILE
    n_e, t = logits_t.shape
    n_rows = -(-(TOP_K * t) // tm) * tm + n_e * tm
    n_tiles = n_rows // tm
    gate_t, dest, tiles, pads = _routing(logits_t, tm, n_tiles)
    dest = dest.reshape(-1)
    tile_e = tiles[0, :n_tiles]
    n_used_rows = tiles[1, :1]
    n_used_tiles = n_used_rows // tm
    xin = _dispatch(h2p, dest, pads[0, :, 0], pads[1, :, 0], n_used_rows, n_rows, tm)
    act = _gemm1(xin, w1, b1, tile_e, n_used_tiles, tm)
    y = _gemm2(act, w2, b2, tile_e, n_used_tiles, tm)
    return _combine(dest, x1, gate_t.T, gt_f, ln_g, ln_b, y, seq, alpha)


def kernel(x, c, positions, w_ada, b_ada, w_in, hgrn_lb, gnorm_w, w_o, ln1_g, ln1_b,
           router_w, router_b, w1, b1, w2, b2, ln2_g, ln2_b):
    bsz, seq, d = x.shape
    depth = w_ada.shape[0]
    t = bsz * seq
    att_w = d // 2
    hg_w = d - att_w
    alpha = (2.0 * depth) ** 0.25

    lb_all = jnp.cumsum(jax.nn.softmax(hgrn_lb.astype(F32), axis=0), axis=0)
    inv = ROPE_THETA ** (-(jnp.arange(0, ROT_DIM, 2, dtype=F32) / ROT_DIM))
    lane = np.arange(V7X_LANES)
    inv_lane = inv[(lane % ATT_HEAD_DIM) % (ROT_DIM // 2)].reshape(1, V7X_LANES)
    pos_b = jnp.broadcast_to(positions.astype(F32).reshape(t, 1), (t, V7X_LANES))

    x2d = x.reshape(t, d)
    for l in range(depth):
        mod = _adaln(c, w_ada[l], b_ada[l])
        sh_a, sc_a, gt_a, sh_f, sc_f, gt_f = jnp.split(mod, 6, axis=-1)

        proj = _in_proj(x2d, sc_a, sh_a, w_in[l], seq)
        proj3 = proj.reshape(bsz, seq, proj.shape[1])
        q_hp, k_hp = _qk_rope(proj, pos_b, inv_lane, bsz, seq, att_w)
        att = _attention(q_hp, k_hp, proj3, att_w).reshape(t, att_w)
        rec = _hgrn2(proj3, lb_all[l], gnorm_w[l], att_w, hg_w).reshape(t, hg_w)

        x1, h2p, logits_t = _out_proj(att, rec, x2d, w_o[l], gt_a, sc_f, sh_f, ln1_g[l], ln1_b[l],
                                      router_w[l].T, router_b[l], seq, alpha)

        x2d = _moe_ffn(h2p, logits_t, x1, gt_f, ln2_g[l], ln2_b[l], w1[l], b1[l], w2[l], b2[l],
                       seq, alpha)
    return x2d.reshape(bsz, seq, d)
```

```python
import functools

import numpy as np
import jax
import jax.numpy as jnp
from jax import lax
from jax.experimental import pallas as pl
from jax.experimental.pallas import tpu as pltpu

F32 = jnp.float32
BF16 = jnp.bfloat16

V7X_LANES = 128
V7X_MXU_DIM = 256
V7X_VMEM_LIMIT = 56 * 1024 * 1024

ATT_HEAD_DIM = 64
DILATED_PAIRS = ((128, 1), (512, 4), (2048, 16))
ATT_BLOCK = 128
ROT_DIM = ATT_HEAD_DIM // 4
ROPE_THETA = 500000.0
HG_EXPAND = 128
HG_CHUNK = 64
HG_SUB = 16
TOP_K = 4
SWIGLU_ALPHA = 1.702
SWIGLU_LIMIT = 7.0
LN_EPS = 1e-5
RMS_EPS = 1e-6
NEG_INF = -1e30

NT_DIMS = (((1,), (1,)), ((), ()))


def _pick(n, candidates):
    for c in candidates:
        if n % c == 0:
            return c
    raise ValueError(f"no tile in {candidates} divides {n}")


def _cparams(sem, vmem=V7X_VMEM_LIMIT, flags=None):
    return pltpu.CompilerParams(dimension_semantics=sem, vmem_limit_bytes=vmem, flags=flags)


def _adaln_kernel(c_ref, w_ref, b_ref, o_ref):
    c = c_ref[...]
    s = c * jax.nn.sigmoid(c)
    o_ref[...] = jnp.dot(s.astype(BF16), w_ref[...].astype(BF16),
                         preferred_element_type=F32) + b_ref[...]


def _adaln(c, w, b):
    bsz, d = c.shape
    n = w.shape[1]
    rows = 8
    cp = jnp.zeros((rows, d), F32).at[:bsz].set(c)
    tn = _pick(n, (1024, 512, 256, 128))
    out = pl.pallas_call(
        _adaln_kernel,
        grid=(n // tn,),
        in_specs=[pl.BlockSpec((rows, d), lambda j: (0, 0)),
                  pl.BlockSpec((d, tn), lambda j: (0, j)),
                  pl.BlockSpec((1, tn), lambda j: (0, j))],
        out_specs=pl.BlockSpec((rows, tn), lambda j: (0, j)),
        out_shape=jax.ShapeDtypeStruct((rows, n), F32),
        compiler_params=_cparams(("arbitrary",)),
        name="adaln",
    )(cp, w, b.reshape(1, n))
    return out[:bsz]


def _in_proj_kernel(x_ref, sc_ref, sh_ref, w_ref, o_ref, wbf_ref):
    @pl.when(pl.program_id(1) == 0)
    def _():
        wbf_ref[...] = w_ref[...].astype(BF16)

    h = x_ref[...] * (1.0 + sc_ref[0]) + sh_ref[0]
    o_ref[...] = jnp.dot(h.astype(BF16), wbf_ref[...], preferred_element_type=F32)


def _in_proj(x2d, sc, sh, w, seq):
    t, d = x2d.shape
    nc = w.shape[1]
    tm = _pick(seq, (512, 256, 128))
    tn = _pick(nc, (1024, 512, 256, 128))
    per_b = seq // tm
    vec = pl.BlockSpec((1, 1, d), lambda j, i: (i // per_b, 0, 0))
    return pl.pallas_call(
        _in_proj_kernel,
        grid=(nc // tn, t // tm),
        in_specs=[pl.BlockSpec((tm, d), lambda j, i: (i, 0)), vec, vec,
                  pl.BlockSpec((d, tn), lambda j, i: (0, j))],
        out_specs=pl.BlockSpec((tm, tn), lambda j, i: (i, j)),
        out_shape=jax.ShapeDtypeStruct((t, nc), F32),
        scratch_shapes=[pltpu.VMEM((d, tn), BF16)],
        compiler_params=_cparams(("arbitrary", "arbitrary")),
        name="in_proj",
    )(x2d, sc[:, None, :], sh[:, None, :], w)


def _rope_kernel(q_ref, k_ref, v_ref, pos_ref, inv_ref, qo_ref, ko_ref, vo_ref, cos_ref, sin_ref):
    lane = lax.broadcasted_iota(jnp.int32, cos_ref.shape, 1)
    lh = lane % ATT_HEAD_DIM
    half = ROT_DIM // 2

    @pl.when(pl.program_id(1) == 0)
    def _():
        ang = pos_ref[...] * inv_ref[...]
        cos_ref[...] = jnp.where(lh < ROT_DIM, jnp.cos(ang), 1.0)
        sn = jnp.sin(ang)
        sin_ref[...] = jnp.where(lh < half, -sn, jnp.where(lh < ROT_DIM, sn, 0.0))

    cs = cos_ref[...]
    sn = sin_ref[...]

    def rope(t):
        swapped = jnp.where(lh < half,
                            pltpu.roll(t, V7X_LANES - half, axis=1),
                            pltpu.roll(t, half, axis=1))
        return t * cs + swapped * sn

    qo_ref[0, 0] = rope(q_ref[...]) * (ATT_HEAD_DIM ** -0.5)
    ko_ref[0, 0] = rope(k_ref[...])
    vo_ref[0, 0] = v_ref[...]


def _qk_rope(proj, pos_b, inv_lane, bsz, seq, att_w):
    t = proj.shape[0]
    hp = att_w // V7X_LANES
    tm = _pick(seq, (512, 256, 128))
    per_b = seq // tm
    out_spec = pl.BlockSpec((1, 1, tm, V7X_LANES),
                            lambda i, h: (i // per_b, h, i % per_b, 0))
    shp = jax.ShapeDtypeStruct((bsz, hp, seq, V7X_LANES), F32)
    return pl.pallas_call(
        _rope_kernel,
        grid=(t // tm, hp),
        in_specs=[pl.BlockSpec((tm, V7X_LANES), lambda i, h: (i, h)),
                  pl.BlockSpec((tm, V7X_LANES), lambda i, h: (i, hp + h)),
                  pl.BlockSpec((tm, V7X_LANES), lambda i, h: (i, 2 * hp + h)),
                  pl.BlockSpec((tm, V7X_LANES), lambda i, h: (i, 0)),
                  pl.BlockSpec((1, V7X_LANES), lambda i, h: (0, 0))],
        out_specs=[out_spec, out_spec, out_spec],
        out_shape=[shp, shp, shp],
        scratch_shapes=[pltpu.VMEM((tm, V7X_LANES), F32), pltpu.VMEM((tm, V7X_LANES), F32)],
        compiler_params=_cparams(("arbitrary", "arbitrary")),
        name="qk_rope",
    )(proj, proj, proj, pos_b, inv_lane)


def _attn_block(q_blk, k_cat, v_cat, first):
    qn = q_blk.shape[0]
    nk = k_cat.shape[0]
    qi = lax.broadcasted_iota(jnp.int32, (qn, nk), 0)
    kj = lax.broadcasted_iota(jnp.int32, (qn, nk), 1)
    dist = qi - kj if first else qi + qn - kj
    w_steps = qn
    valid = (dist >= 0) & (dist <= w_steps)
    lane = lax.broadcasted_iota(jnp.int32, (nk, V7X_LANES), 1)
    qb = q_blk.astype(BF16)
    vb = v_cat.astype(BF16)
    outs, lses = [], []
    for head in range(V7X_LANES // ATT_HEAD_DIM):
        in_head = (lane // ATT_HEAD_DIM) == head
        kh = jnp.where(in_head, k_cat, 0.0).astype(BF16)
        s = lax.dot_general(qb, kh, NT_DIMS, preferred_element_type=F32)
        s = jnp.where(valid, s, NEG_INF)
        m = jnp.max(s, axis=-1, keepdims=True)
        p = jnp.exp(s - m)
        den = jnp.sum(p, axis=-1, keepdims=True)
        o = jnp.dot(p.astype(BF16), vb, preferred_element_type=F32) / den
        outs.append(o)
        lses.append(m + jnp.log(den))
    olane = lax.broadcasted_iota(jnp.int32, (qn, V7X_LANES), 1)
    o = outs[-1]
    lse = jnp.broadcast_to(lses[-1], (qn, V7X_LANES))
    for head in range(len(outs) - 2, -1, -1):
        sel = (olane // ATT_HEAD_DIM) == head
        o = jnp.where(sel, outs[head], o)
        lse = jnp.where(sel, lses[head], lse)
    return o, lse


ATT_UNROLL = 3


def _attn_kernel(*refs, seq):
    qn = ATT_BLOCK
    nbr = len(DILATED_PAIRS)
    ins, out_ref = refs[:3 * nbr], refs[3 * nbr]
    scr = refs[3 * nbr + 1:3 * nbr + 1 + 2 * nbr]
    stage_o, stage_l = refs[3 * nbr + 1 + 2 * nbr:]

    for bi, (window, d) in enumerate(DILATED_PAIRS):
        q_ref, k_ref, v_ref = ins[3 * bi:3 * bi + 3]
        o_scr, l_scr = scr[2 * bi:2 * bi + 2]
        nb = seq // (d * qn)
        for r in range(d):
            lanes = slice(r * V7X_LANES, (r + 1) * V7X_LANES)

            def block(n, first, q_ref=q_ref, k_ref=k_ref, v_ref=v_ref, o_scr=o_scr, l_scr=l_scr,
                      lanes=lanes):
                if first:
                    qsl = ksl = slice(0, qn)
                else:
                    qsl = pl.ds(pl.multiple_of(n * qn, qn), qn)
                    ksl = pl.ds(pl.multiple_of((n - 1) * qn, qn), 2 * qn)
                o, lse = _attn_block(q_ref[0, 0, qsl, lanes], k_ref[0, 0, ksl, lanes],
                                     v_ref[0, 0, ksl, lanes], first)
                o_scr[qsl, lanes] = o
                l_scr[qsl, lanes] = lse

            block(0, True)
            if nb > 1:
                def later(n, carry, block=block):
                    block(n, False)
                    return carry
                trips = nb - 1
                unroll = max(u for u in range(1, ATT_UNROLL + 1) if trips % u == 0)
                lax.fori_loop(1, nb, later, 0, unroll=True if trips <= ATT_UNROLL else unroll)

    (_, d_lo), (_, dm), (_, d_hi) = DILATED_PAIRS
    g = d_hi // dm
    o_lo, l_lo, o_mid, l_mid, o_hi, l_hi = scr
    width = dm * V7X_LANES
    for c in range(seq // dm // qn):
        m0 = c * qn

        def from_lo(ref):
            return jnp.concatenate(
                [ref[pl.ds(dm * m0 + r, qn, stride=dm), :] for r in range(dm)], axis=1)

        def from_hi(ref, stage):
            for a in range(g):
                for r in range(dm):
                    c0 = (a * dm + r) * V7X_LANES
                    stage[r, pl.ds(a, qn // g, stride=g), :] = ref[m0 // g:(m0 + qn) // g,
                                                                   c0:c0 + V7X_LANES]
            return jnp.concatenate([stage[r] for r in range(dm)], axis=1)

        ls = [from_lo(l_lo), l_mid[m0:m0 + qn, :], from_hi(l_hi, stage_l)]
        os_ = [from_lo(o_lo), o_mid[m0:m0 + qn, :], from_hi(o_hi, stage_o)]
        mx = functools.reduce(jnp.maximum, ls)
        ws = [jnp.exp(l - mx) for l in ls]
        tot = functools.reduce(lambda a, b: a + b, ws)
        acc = functools.reduce(lambda a, b: a + b, [w * o for w, o in zip(ws, os_)])
        out_ref[0, 0, m0:m0 + qn, :] = (acc / tot).astype(out_ref.dtype)


def _attention(q_hp, k_hp, v_hp):
    bsz, hp, seq, _ = q_hp.shape
    (_, d_lo), (_, dm), (_, d_hi) = DILATED_PAIRS
    assert d_lo == 1 and d_hi % dm == 0 and ATT_BLOCK % (d_hi // dm) == 0
    for window, d in DILATED_PAIRS:
        assert window // d == ATT_BLOCK and seq % (d * ATT_BLOCK) == 0
    args, specs, scratch = [], [], []
    for _, d in DILATED_PAIRS:
        shape = (bsz, hp, seq // d, d * V7X_LANES)
        for a in (q_hp, k_hp, v_hp):
            args.append(a.reshape(shape))
            specs.append(pl.BlockSpec((1, 1) + shape[2:], lambda b, h: (b, h, 0, 0)))
        scratch += [pltpu.VMEM(shape[2:], F32), pltpu.VMEM(shape[2:], F32)]
    scratch += [pltpu.VMEM((dm, ATT_BLOCK, V7X_LANES), F32)] * 2
    out_shape = (bsz, hp, seq // dm, dm * V7X_LANES)
    out = pl.pallas_call(
        functools.partial(_attn_kernel, seq=seq),
        grid=(bsz, hp),
        in_specs=specs,
        out_specs=pl.BlockSpec((1, 1) + out_shape[2:], lambda b, h: (b, h, 0, 0)),
        out_shape=jax.ShapeDtypeStruct(out_shape, BF16),
        scratch_shapes=scratch,
        compiler_params=_cparams(("arbitrary", "arbitrary")),
        name="dilated_attn",
    )(*args)
    return out.reshape(bsz, hp, seq, V7X_LANES)


def _hgrn_kernel(q_ref, f_ref, i_ref, g_ref, lb_ref, gw_ref, tri_ref, o_ref, st_ref, *, heads, ts):
    @pl.when(pl.program_id(2) == 0)
    def _():
        st_ref[...] = jnp.zeros_like(st_ref)

    c_len = HG_CHUNK
    tri = tri_ref[...]

    def chunk(ci, carry):
        r0 = pl.multiple_of(ci * c_len, c_len)
        rsl = pl.ds(r0, c_len)
        for h in range(heads):
            lsl = slice(h * HG_EXPAND, (h + 1) * HG_EXPAND)
            lb = lb_ref[:, lsl]
            f = lb + (1.0 - lb) * jax.nn.sigmoid(f_ref[0, rsl, lsl])
            logf = jnp.log(f)
            kk = 1.0 - f
            q = q_ref[0, rsl, lsl]
            v = i_ref[0, rsl, lsl]
            b = jnp.dot(tri, logf, precision=lax.Precision.HIGHEST, preferred_element_type=F32)
            b_last = b[c_len - 1:c_len, :]
            st = st_ref[h]
            o_inter = lax.dot_general((q * jnp.exp(b)).astype(BF16), st.astype(BF16), NT_DIMS,
                                      preferred_element_type=F32)
            vb = v.astype(BF16)
            o_rows = []
            for i in range(c_len // HG_SUB):
                lo, hi = i * HG_SUB, (i + 1) * HG_SUB
                mid = lo + HG_SUB // 2
                anchor = b[mid:mid + 1, :]
                qt = q[lo:hi] * jnp.exp(b[lo:hi] - anchor)
                kt = kk[:hi] * jnp.exp(anchor - b[:hi])
                a = lax.dot_general(qt.astype(BF16), kt.astype(BF16), NT_DIMS,
                                    preferred_element_type=F32)
                ti = lax.broadcasted_iota(jnp.int32, (HG_SUB, hi), 0) + lo
                si = lax.broadcasted_iota(jnp.int32, (HG_SUB, hi), 1)
                a = jnp.where(si <= ti, a, 0.0)
                o_rows.append(jnp.dot(a.astype(BF16), vb[:hi], preferred_element_type=F32))
            o = o_inter + jnp.concatenate(o_rows, axis=0)
            kl = kk * jnp.exp(b_last - b)
            upd = jnp.dot(v.T.astype(BF16), kl.astype(BF16), preferred_element_type=F32)
            st_ref[h] = st * jnp.exp(b_last) + upd
            o = o * lax.rsqrt(jnp.mean(o * o, axis=-1, keepdims=True) + RMS_EPS)
            g = g_ref[0, rsl, lsl]
            o = o * gw_ref[:, lsl] * (g * jax.nn.sigmoid(g))
            o_ref[0, rsl, lsl] = o.astype(o_ref.dtype)
        return carry

    lax.fori_loop(0, ts // c_len, chunk, 0, unroll=True)


def _hgrn2(proj3, lb, gw, att_w, hg_w):
    bsz, seq, _ = proj3.shape
    n_heads = hg_w // HG_EXPAND
    heads = _pick(n_heads, (4, 2, 1))
    lanes = heads * HG_EXPAND
    ts = _pick(seq, (256, 128, 64))
    base = 3 * att_w
    assert base % lanes == 0 and hg_w % lanes == 0

    def seg(k):
        off = (base + k * hg_w) // lanes
        return pl.BlockSpec((1, ts, lanes), lambda b, h, s: (b, s, off + h))

    vec = pl.BlockSpec((1, lanes), lambda b, h, s: (0, h))
    tri = jnp.asarray(np.tril(np.ones((HG_CHUNK, HG_CHUNK), np.float32)))
    return pl.pallas_call(
        functools.partial(_hgrn_kernel, heads=heads, ts=ts),
        grid=(bsz, hg_w // lanes, seq // ts),
        in_specs=[seg(0), seg(1), seg(2), seg(3), vec, vec,
                  pl.BlockSpec((HG_CHUNK, HG_CHUNK), lambda b, h, s: (0, 0))],
        out_specs=pl.BlockSpec((1, ts, lanes), lambda b, h, s: (b, s, h)),
        out_shape=jax.ShapeDtypeStruct((bsz, seq, hg_w), BF16),
        scratch_shapes=[pltpu.VMEM((heads, HG_EXPAND, HG_EXPAND), F32)],
        compiler_params=_cparams(("arbitrary", "arbitrary", "arbitrary")),
        name="hgrn2",
    )(proj3, proj3, proj3, proj3, lb.reshape(1, hg_w), gw.reshape(1, hg_w), tri)


def _layer_norm(y, g, b):
    mu = jnp.mean(y, axis=-1, keepdims=True)
    yc = y - mu
    var = jnp.mean(yc * yc, axis=-1, keepdims=True)
    return yc * lax.rsqrt(var + LN_EPS) * g + b


def _bf16_split(x):
    hi = x.astype(BF16)
    lo = (x - hi.astype(F32)).astype(BF16)
    return hi, lo


def _out_proj_kernel(att_ref, rec_ref, x_ref, wo_ref, gt_ref, sc_ref, sh_ref, g_ref, b_ref,
                     rw_ref, rb_ref, x1_ref, h2_ref, lg_ref, wobf_ref, rwhi_ref, rwlo_ref,
                     *, alpha, att_w):
    @pl.when(pl.program_id(0) == 0)
    def _():
        wobf_ref[...] = wo_ref[...].astype(BF16)
        hi, lo = _bf16_split(rw_ref[...])
        rwhi_ref[...] = hi
        rwlo_ref[...] = lo

    att = jnp.concatenate([att_ref[0, h] for h in range(att_ref.shape[1])], axis=1)
    mix = (jnp.dot(att, wobf_ref[:att_w, :], preferred_element_type=F32)
           + jnp.dot(rec_ref[...], wobf_ref[att_w:, :], preferred_element_type=F32))
    y = alpha * x_ref[...] + (1.0 + gt_ref[0]) * mix
    x1 = _layer_norm(y, g_ref[...], b_ref[...])
    x1_ref[...] = x1
    h2 = x1 * (1.0 + sc_ref[0]) + sh_ref[0]
    hi, lo = _bf16_split(h2)
    lg = (lax.dot_general(rwhi_ref[...], hi, NT_DIMS, preferred_element_type=F32)
          + lax.dot_general(rwhi_ref[...], lo, NT_DIMS, preferred_element_type=F32)
          + lax.dot_general(rwlo_ref[...], hi, NT_DIMS, preferred_element_type=F32))
    lg_ref[...] = lg + rb_ref[...]
    half = h2.shape[1] // 2
    bits = lax.bitcast_convert_type(hi.astype(F32), jnp.uint32)
    h2_ref[...] = (bits[:, :half] >> 16) | (bits[:, half:] & jnp.uint32(0xFFFF0000))


def _out_proj(att, rec, x2d, wo, gt, sc, sh, g, b, rw_t, rb, seq, alpha):
    t, d = x2d.shape
    hp = att.shape[1]
    att_w = hp * V7X_LANES
    n_e = rw_t.shape[0]
    tm = _pick(seq, (256, 128))
    per_b = seq // tm
    vec3 = pl.BlockSpec((1, 1, d), lambda i: (i // per_b, 0, 0))
    full = lambda shape: pl.BlockSpec(shape, lambda i: (0,) * len(shape))
    once = lambda shape: pl.BlockSpec(shape, lambda i: (0,) * len(shape),
                                      pipeline_mode=pl.Buffered(1))
    return pl.pallas_call(
        functools.partial(_out_proj_kernel, alpha=alpha, att_w=att_w),
        grid=(t // tm,),
        in_specs=[pl.BlockSpec((1, hp, tm, V7X_LANES), lambda i: (i // per_b, 0, i % per_b, 0)),
                  pl.BlockSpec((tm, d - att_w), lambda i: (i, 0)),
                  pl.BlockSpec((tm, d), lambda i: (i, 0)),
                  once((d, d)), vec3, vec3, vec3, full((1, d)), full((1, d)),
                  full((n_e, d)), full((n_e, 1))],
        out_specs=[pl.BlockSpec((tm, d), lambda i: (i, 0)),
                   pl.BlockSpec((tm, d // 2), lambda i: (i, 0)),
                   pl.BlockSpec((n_e, tm), lambda i: (0, i))],
        out_shape=[jax.ShapeDtypeStruct((t, d), F32),
                   jax.ShapeDtypeStruct((t, d // 2), jnp.uint32),
                   jax.ShapeDtypeStruct((n_e, t), F32)],
        scratch_shapes=[pltpu.VMEM((d, d), BF16), pltpu.VMEM((n_e, d), BF16),
                        pltpu.VMEM((n_e, d), BF16)],
        compiler_params=_cparams(("arbitrary",)),
        name="out_proj_ln1",
    )(att, rec, x2d, wo, gt[:, None, :], sc[:, None, :], sh[:, None, :],
      g.reshape(1, d), b.reshape(1, d), rw_t, rb.reshape(n_e, 1))


def _routing_kernel(lg_ref, tri_ref, low_ref, gate_ref, dest_ref, tile_ref, pad_ref, sel_scr, rk_scr,
                    *, blk, tm):
    n_e, t = lg_ref.shape
    eidx = lax.broadcasted_iota(jnp.int32, (n_e, t), 0)
    cur = lg_ref[...]
    vals, idxs = [], []
    for _ in range(TOP_K):
        m = jnp.max(cur, axis=0, keepdims=True)
        ik = jnp.min(jnp.where(cur == m, eidx, n_e), axis=0, keepdims=True)
        cur = jnp.where(eidx == ik, -jnp.inf, cur)
        vals.append(m)
        idxs.append(ik)
    es = [jnp.exp(v - vals[0]) for v in vals]
    den = functools.reduce(lambda a, b: a + b, es)
    for k in range(TOP_K):
        gate_ref[k:k + 1, :] = es[k] / den
    sel = functools.reduce(lambda a, b: a | b, [eidx == ik for ik in idxs])
    sel_scr[...] = jnp.where(sel, 1.0, 0.0)

    tri = tri_ref[...]
    carry = jnp.zeros((n_e, 1), F32)
    for j in range(t // blk):
        sb = sel_scr[:, j * blk:(j + 1) * blk]
        pre = jnp.dot(sb.astype(BF16), tri, preferred_element_type=F32)
        rk_scr[:, j * blk:(j + 1) * blk] = pre + carry
        carry = carry + jnp.sum(sb, axis=1, keepdims=True)

    counts = jnp.broadcast_to(carry, (n_e, V7X_LANES))
    padded = jnp.floor((counts + (tm - 1)) * (1.0 / tm)) * tm
    pends = jnp.dot(low_ref[...], padded, precision=lax.Precision.HIGHEST,
                    preferred_element_type=F32)
    pstarts = pends - padded
    row0 = rk_scr[...] + pstarts[:, 0:1]
    for k in range(TOP_K):
        dest_ref[k:k + 1, :] = jnp.sum(jnp.where(eidx == idxs[k], row0, 0.0), axis=0,
                                       keepdims=True).astype(jnp.int32)
    starts = (lax.broadcasted_iota(jnp.int32, (n_e, tile_ref.shape[1]), 1) * tm).astype(F32)
    owner = jnp.sum(jnp.where(pends[:, 0:1] <= starts, 1.0, 0.0), axis=0, keepdims=True)
    tile_ref[0:1, :] = jnp.minimum(owner, n_e - 1.0).astype(jnp.int32)
    tile_ref[1:2, :] = jnp.broadcast_to(jnp.max(pends[:, 0:1], axis=0, keepdims=True),
                                        (1, tile_ref.shape[1])).astype(jnp.int32)
    pad_ref[0] = (pstarts + counts).astype(jnp.int32)
    pad_ref[1] = (padded - counts).astype(jnp.int32)


def _routing(logits_t, tm, n_tiles):
    n_e, t = logits_t.shape
    assert tm & (tm - 1) == 0
    blk = _pick(t, (256, 128))
    tri = jnp.asarray(np.triu(np.ones((blk, blk), np.float32), 1), BF16)
    low = jnp.asarray(np.tril(np.ones((n_e, n_e), np.float32)))
    ntp = -(-n_tiles // V7X_LANES) * V7X_LANES
    full = lambda shape: pl.BlockSpec(shape, lambda: (0,) * len(shape))
    return pl.pallas_call(
        functools.partial(_routing_kernel, blk=blk, tm=tm),
        in_specs=[full((n_e, t)), full((blk, blk)), full((n_e, n_e))],
        out_specs=[full((TOP_K, t)), full((TOP_K, t)), full((2, ntp)), full((2, n_e, V7X_LANES))],
        out_shape=[jax.ShapeDtypeStruct((TOP_K, t), F32),
                   jax.ShapeDtypeStruct((TOP_K, t), jnp.int32),
                   jax.ShapeDtypeStruct((2, ntp), jnp.int32),
                   jax.ShapeDtypeStruct((2, n_e, V7X_LANES), jnp.int32)],
        scratch_shapes=[pltpu.VMEM((n_e, t), F32), pltpu.VMEM((n_e, t), F32)],
        compiler_params=pltpu.CompilerParams(vmem_limit_bytes=V7X_VMEM_LIMIT),
        name="routing",
    )(logits_t, tri, low)


def _dispatch_kernel(dest_ref, pad_off_ref, pad_n_ref, nrow_ref, h_ref, o_ref, zero_ref, sem, zsem,
                     *, n_tok, tq, tm, n_e):
    step = pl.program_id(0)
    base = step * tq

    def fill(off, n):
        return pltpu.make_async_copy(zero_ref.at[pl.ds(0, n)], o_ref.at[pl.ds(off, n)], zsem)

    sub = 8
    pieces = [1 << s for s in range(tm.bit_length() - 2, sub.bit_length() - 2, -1)]

    def pad_rows(wait):
        def per_expert(e, c):
            off = pad_off_ref[e]
            n = pad_n_ref[e]
            head = jnp.minimum(n, (-off) & (sub - 1))
            for s in range(sub - 1):
                @pl.when(s < head)
                def _(s=s):
                    cp = fill(off + s, 1)
                    cp.wait() if wait else cp.start()
            off = off + head
            n = n - head
            for p in pieces:
                hit = (n & p) != 0

                @pl.when(hit)
                def _(off=off, p=p):
                    cp = fill(pl.multiple_of(off, sub), p)
                    cp.wait() if wait else cp.start()

                off = off + jnp.where(hit, p, 0)
            return c
        lax.fori_loop(0, n_e, per_expert, 0)

        def per_tile(i, c):
            cp = fill(pl.multiple_of(i * tm, tm), tm)
            cp.wait() if wait else cp.start()
            return c
        lax.fori_loop(nrow_ref[0] // tm, o_ref.shape[0] // tm, per_tile, 0)

    @pl.when(step == 0)
    def _():
        zero_ref[...] = jnp.zeros_like(zero_ref)
        pad_rows(False)

    def copy(r, k):
        dst = o_ref.at[pl.ds(dest_ref[k * n_tok + base + r], 1)]
        return pltpu.make_async_copy(h_ref.at[pl.ds(r, 1)], dst, sem)

    def start(r, c):
        for k in range(TOP_K):
            copy(r, k).start(priority=k % 2)
        return c

    def wait(r, c):
        for k in range(TOP_K):
            copy(r, k).wait()
        return c

    lax.fori_loop(0, tq, start, 0, unroll=2)
    lax.fori_loop(0, tq, wait, 0, unroll=2)

    @pl.when(step == 0)
    def _():
        pad_rows(True)


def _dispatch(h2p, dest_flat, pad_off, pad_n, n_used_rows, n_rows, tm):
    t, width = h2p.shape
    n_e = pad_off.shape[0]
    tq = _pick(t, (256, 128))
    return pl.pallas_call(
        functools.partial(_dispatch_kernel, n_tok=t, tq=tq, tm=tm, n_e=n_e),
        grid_spec=pltpu.PrefetchScalarGridSpec(
            num_scalar_prefetch=4, grid=(t // tq,),
            in_specs=[pl.BlockSpec((tq, width), lambda i, *_: (i, 0))],
            out_specs=pl.BlockSpec(memory_space=pl.ANY),
            scratch_shapes=[pltpu.VMEM((tm, width), h2p.dtype),
                            pltpu.SemaphoreType.DMA(()), pltpu.SemaphoreType.DMA(())]),
        out_shape=jax.ShapeDtypeStruct((n_rows, width), h2p.dtype),
        compiler_params=pltpu.CompilerParams(dimension_semantics=("arbitrary",),
                                             has_side_effects=True,
                                             vmem_limit_bytes=V7X_VMEM_LIMIT),
        name="moe_dispatch",
    )(dest_flat, pad_off, pad_n, n_used_rows, h2p)


def _unpack_bf16_pair(words):
    lo = lax.bitcast_convert_type(words << 16, F32).astype(BF16)
    hi = lax.bitcast_convert_type(words & jnp.uint32(0xFFFF0000), F32).astype(BF16)
    return lo, hi


EXPERT_COL_BLOCK = 2048
CAST_ROWS = 256


def _stream_expert_weights(te_ref, nt_ref, w_hbm, stage_ref, wbf_ref, sem):
    j, i = pl.program_id(0), pl.program_id(1)
    n_pass, n_tiles = pl.num_programs(0), pl.num_programs(1)
    tn = stage_ref.shape[1]
    n_used = nt_ref[0]
    cur = te_ref[i]

    def fetch(e, jj):
        src = w_hbm.at[e, :, pl.ds(pl.multiple_of(jj * tn, tn), tn)]
        return pltpu.make_async_copy(src, stage_ref, sem)

    @pl.when((i == 0) & (j == 0))
    def _():
        fetch(te_ref[0], 0).start()

    @pl.when((i < n_used) & ((i == 0) | (cur != te_ref[jnp.maximum(i - 1, 0)])))
    def _():
        fetch(cur, j).wait()

        def cast(r, c):
            rows = pl.ds(pl.multiple_of(r * CAST_ROWS, CAST_ROWS), CAST_ROWS)
            wbf_ref[rows, :] = stage_ref[rows, :].astype(BF16)
            return c
        lax.fori_loop(0, stage_ref.shape[0] // CAST_ROWS, cast, 0)

        def same_group(k):
            return (k < n_used) & (te_ref[jnp.minimum(k, n_tiles - 1)] == cur)
        nxt = lax.while_loop(same_group, lambda k: k + 1, i + 1)

        @pl.when(nxt < n_used)
        def _():
            fetch(te_ref[jnp.minimum(nxt, n_tiles - 1)], j).start()

        @pl.when((nxt >= n_used) & (j + 1 < n_pass))
        def _():
            fetch(te_ref[0], j + 1).start()


def _gemm1_kernel(te_ref, nt_ref, x_ref, w_hbm, b_ref, cmp_ref, o_ref, stage_ref, wbf_ref, sem, *, nc):
    _stream_expert_weights(te_ref, nt_ref, w_hbm, stage_ref, wbf_ref, sem)
    i = pl.program_id(1)

    @pl.when(i < nt_ref[0])
    def _():
        x_lo, x_hi = _unpack_bf16_pair(x_ref[...])
        half = x_lo.shape[1]
        cw = cmp_ref.shape[0]
        lane = lax.broadcasted_iota(jnp.int32, (x_lo.shape[0], nc), 1)
        for n0 in range(0, wbf_ref.shape[1], nc):
            hb = (jnp.dot(x_lo, wbf_ref[:half, n0:n0 + nc], preferred_element_type=F32)
                  + jnp.dot(x_hi, wbf_ref[half:, n0:n0 + nc], preferred_element_type=F32)
                  + b_ref[0, :, n0:n0 + nc])
            nxt = pltpu.roll(hb, nc - 1, axis=1)
            glu = jnp.minimum(hb, SWIGLU_LIMIT)
            lin = jnp.clip(nxt, -SWIGLU_LIMIT, SWIGLU_LIMIT)
            act = glu * jax.nn.sigmoid(SWIGLU_ALPHA * glu) * (lin + 1.0)
            act = jnp.where(lane % 2 == 0, act, 0.0).astype(BF16)
            for c in range(nc // cw):
                o0 = (n0 + c * cw) // 2
                o_ref[:, o0:o0 + cw // 2] = jnp.dot(
                    act[:, c * cw:(c + 1) * cw], cmp_ref[...],
                    preferred_element_type=F32).astype(o_ref.dtype)

    @pl.when(i >= nt_ref[0])
    def _():
        o_ref[...] = jnp.zeros_like(o_ref)


def _gemm1(xin, w1, b1, tile_e, n_tiles_used, tm):
    n_rows, half_d = xin.shape
    n_e, d, f2 = w1.shape
    tn = min(EXPERT_COL_BLOCK, f2)
    nc = min(512, tn)
    assert f2 % tn == 0 and d % CAST_ROWS == 0
    n_tiles = n_rows // tm
    cw = V7X_MXU_DIM
    cmp_np = np.zeros((cw, cw // 2), np.float32)
    cmp_np[np.arange(0, cw, 2), np.arange(cw // 2)] = 1.0

    def used(i, nt):
        return jnp.minimum(i, nt[0] - 1)

    return pl.pallas_call(
        functools.partial(_gemm1_kernel, nc=nc),
        grid_spec=pltpu.PrefetchScalarGridSpec(
            num_scalar_prefetch=2, grid=(f2 // tn, n_tiles),
            in_specs=[pl.BlockSpec((tm, half_d), lambda j, i, te, nt: (used(i, nt), 0)),
                      pl.BlockSpec(memory_space=pl.ANY),
                      pl.BlockSpec((1, 1, tn), lambda j, i, te, nt: (te[used(i, nt)], 0, j)),
                      pl.BlockSpec((cw, cw // 2), lambda j, i, te, nt: (0, 0))],
            out_specs=pl.BlockSpec((tm, tn // 2), lambda j, i, te, nt: (i, j)),
            scratch_shapes=[pltpu.VMEM((d, tn), F32), pltpu.VMEM((d, tn), BF16),
                            pltpu.SemaphoreType.DMA(())]),
        out_shape=jax.ShapeDtypeStruct((n_rows, f2 // 2), BF16),
        compiler_params=_cparams(("arbitrary", "arbitrary")),
        name="moe_gemm1",
    )(tile_e, n_tiles_used, xin, w1, b1.reshape(n_e, 1, f2), jnp.asarray(cmp_np, BF16))


def _gemm2_kernel(te_ref, nt_ref, a_ref, w_hbm, b_ref, o_ref, stage_ref, wbf_ref, sem, *, nc):
    _stream_expert_weights(te_ref, nt_ref, w_hbm, stage_ref, wbf_ref, sem)
    i = pl.program_id(1)

    @pl.when(i < nt_ref[0])
    def _():
        a = a_ref[...]
        for n0 in range(0, wbf_ref.shape[1], nc):
            o_ref[:, n0:n0 + nc] = (jnp.dot(a, wbf_ref[:, n0:n0 + nc], preferred_element_type=F32)
                                    + b_ref[0, :, n0:n0 + nc])

    @pl.when(i >= nt_ref[0])
    def _():
        o_ref[...] = jnp.zeros_like(o_ref)


def _gemm2(act, w2, b2, tile_e, n_tiles_used, tm):
    n_rows, f = act.shape
    n_e, _, d = w2.shape
    tn = min(EXPERT_COL_BLOCK, d)
    nc = min(512, tn)
    assert d % tn == 0 and f % CAST_ROWS == 0
    n_tiles = n_rows // tm

    def used(i, nt):
        return jnp.minimum(i, nt[0] - 1)

    return pl.pallas_call(
        functools.partial(_gemm2_kernel, nc=nc),
        grid_spec=pltpu.PrefetchScalarGridSpec(
            num_scalar_prefetch=2, grid=(d // tn, n_tiles),
            in_specs=[pl.BlockSpec((tm, f), lambda j, i, te, nt: (used(i, nt), 0)),
                      pl.BlockSpec(memory_space=pl.ANY),
                      pl.BlockSpec((1, 1, tn), lambda j, i, te, nt: (te[used(i, nt)], 0, j))],
            out_specs=pl.BlockSpec((tm, tn), lambda j, i, te, nt: (i, j)),
            scratch_shapes=[pltpu.VMEM((f, tn), F32), pltpu.VMEM((f, tn), BF16),
                            pltpu.SemaphoreType.DMA(())]),
        out_shape=jax.ShapeDtypeStruct((n_rows, d), F32),
        compiler_params=_cparams(("arbitrary", "arbitrary")),
        name="moe_gemm2",
    )(tile_e, n_tiles_used, act, w2, b2.reshape(n_e, 1, d))


def _combine_kernel(dest_ref, x1_ref, gate_ref, gt_ref, g_ref, b_ref, y_ref, o_ref, buf, sem,
                    *, alpha, tc, n_tok):
    i = pl.program_id(0)
    n_steps = pl.num_programs(0)

    def copy(step, slot, k, r):
        src = y_ref.at[pl.ds(dest_ref[k * n_tok + step * tc + r], 1)]
        return pltpu.make_async_copy(src, buf.at[slot, k, pl.ds(r, 1)], sem.at[slot])

    def issue(step, slot):
        def body(r, c):
            for k in range(TOP_K):
                copy(step, slot, k, r).start()
            return c
        lax.fori_loop(0, tc, body, 0)

    def drain(step, slot):
        def body(r, c):
            for k in range(TOP_K):
                copy(step, slot, k, r).wait()
            return c
        lax.fori_loop(0, tc, body, 0)

    slot = i % 2

    @pl.when(i == 0)
    def _():
        issue(0, 0)

    @pl.when(i + 1 < n_steps)
    def _():
        issue(i + 1, 1 - slot)

    drain(i, slot)
    gates = gate_ref[...]
    ff = gates[:, 0:1] * buf[slot, 0]
    for k in range(1, TOP_K):
        ff = ff + gates[:, k:k + 1] * buf[slot, k]
    y = alpha * x1_ref[...] + (1.0 + gt_ref[0]) * ff
    o_ref[...] = _layer_norm(y, g_ref[...], b_ref[...])


def _combine(dest_flat, x1, gates_tk, gt, g, b, y, seq, alpha):
    t, d = x1.shape
    tc = _pick(seq, (128,))
    per_b = seq // tc
    return pl.pallas_call(
        functools.partial(_combine_kernel, alpha=alpha, tc=tc, n_tok=t),
        grid_spec=pltpu.PrefetchScalarGridSpec(
            num_scalar_prefetch=1, grid=(t // tc,),
            in_specs=[pl.BlockSpec((tc, d), lambda i, ds: (i, 0)),
                      pl.BlockSpec((tc, TOP_K), lambda i, ds: (i, 0)),
                      pl.BlockSpec((1, 1, d), lambda i, ds: (i // per_b, 0, 0)),
                      pl.BlockSpec((1, d), lambda i, ds: (0, 0)),
                      pl.BlockSpec((1, d), lambda i, ds: (0, 0)),
                      pl.BlockSpec(memory_space=pl.ANY)],
            out_specs=pl.BlockSpec((tc, d), lambda i, ds: (i, 0)),
            scratch_shapes=[pltpu.VMEM((2, TOP_K, tc, d), F32),
                            pltpu.SemaphoreType.DMA((2,))]),
        out_shape=jax.ShapeDtypeStruct((t, d), F32),
        compiler_params=_cparams(("arbitrary",)),
        name="moe_combine_ln2",
    )(dest_flat, x1, gates_tk, gt[:, None, :], g.reshape(1, d), b.reshape(1, d), y)


MOE_ROW_TILE = 512


def _moe_ffn(h2p, logits_t, x1, gt_f, ln_g, ln_b, w1, b1, w2, b2, seq, alpha):
    tm = MOE_ROW_TILE
    n_e, t = logits_t.shape
    n_rows = -(-(TOP_K * t) // tm) * tm + n_e * tm
    n_tiles = n_rows // tm
    gate_t, dest, tiles, pads = _routing(logits_t, tm, n_tiles)
    dest = dest.reshape(-1)
    tile_e = tiles[0, :n_tiles]
    n_used_rows = tiles[1, :1]
    n_used_tiles = n_used_rows // tm
    xin = _dispatch(h2p, dest, pads[0, :, 0], pads[1, :, 0], n_used_rows, n_rows, tm)
    act = _gemm1(xin, w1, b1, tile_e, n_used_tiles, tm)
    y = _gemm2(act, w2, b2, tile_e, n_used_tiles, tm)
    return _combine(dest, x1, gate_t.T, gt_f, ln_g, ln_b, y, seq, alpha)


def kernel(x, c, positions, w_ada, b_ada, w_in, hgrn_lb, gnorm_w, w_o, ln1_g, ln1_b,
           router_w, router_b, w1, b1, w2, b2, ln2_g, ln2_b):
    bsz, seq, d = x.shape
    depth = w_ada.shape[0]
    t = bsz * seq
    att_w = d // 2
    hg_w = d - att_w
    alpha = (2.0 * depth) ** 0.25

    lb_all = jnp.cumsum(jax.nn.softmax(hgrn_lb.astype(F32), axis=0), axis=0)
    inv = ROPE_THETA ** (-(jnp.arange(0, ROT_DIM, 2, dtype=F32) / ROT_DIM))
    lane = np.arange(V7X_LANES)
    inv_lane = inv[(lane % ATT_HEAD_DIM) % (ROT_DIM // 2)].reshape(1, V7X_LANES)
    pos_b = jnp.broadcast_to(positions.astype(F32).reshape(t, 1), (t, V7X_LANES))

    x2d = x.reshape(t, d)
    for l in range(depth):
        mod = _adaln(c, w_ada[l], b_ada[l])
        sh_a, sc_a, gt_a, sh_f, sc_f, gt_f = jnp.split(mod, 6, axis=-1)

        proj = _in_proj(x2d, sc_a, sh_a, w_in[l], seq)
        proj3 = proj.reshape(bsz, seq, proj.shape[1])
        q_hp, k_hp, v_hp = _qk_rope(proj, pos_b, inv_lane, bsz, seq, att_w)
        att = _attention(q_hp, k_hp, v_hp)
        rec = _hgrn2(proj3, lb_all[l], gnorm_w[l], att_w, hg_w).reshape(t, hg_w)

        x1, h2p, logits_t = _out_proj(att, rec, x2d, w_o[l], gt_a, sc_f, sh_f, ln1_g[l], ln1_b[l],
                                      router_w[l].T, router_b[l], seq, alpha)

        x2d = _moe_ffn(h2p, logits_t, x1, gt_f, ln2_g[l], ln2_b[l], w1[l], b1[l], w2[l], b2[l],
                       seq, alpha)
    return x2d.reshape(bsz, seq, d)
```

```python
import functools

import numpy as np
import jax
import jax.numpy as jnp
from jax import lax
from jax.experimental import pallas as pl
from jax.experimental.pallas import tpu as pltpu

F32 = jnp.float32
BF16 = jnp.bfloat16

V7X_LANES = 128
V7X_MXU_DIM = 256
V7X_VMEM_LIMIT = 56 * 1024 * 1024

ATT_HEAD_DIM = 64
DILATED_PAIRS = ((128, 1), (512, 4), (2048, 16))
ATT_BLOCK = 128
ROT_DIM = ATT_HEAD_DIM // 4
ROPE_THETA = 500000.0
HG_EXPAND = 128
HG_CHUNK = 64
HG_SUB = 16
TOP_K = 4
SWIGLU_ALPHA = 1.702
SWIGLU_LIMIT = 7.0
LN_EPS = 1e-5
RMS_EPS = 1e-6
NEG_INF = -1e30

NT_DIMS = (((1,), (1,)), ((), ()))


def _pick(n, candidates):
    for c in candidates:
        if n % c == 0:
            return c
    raise ValueError(f"no tile in {candidates} divides {n}")


def _cparams(sem, vmem=V7X_VMEM_LIMIT, flags=None):
    return pltpu.CompilerParams(dimension_semantics=sem, vmem_limit_bytes=vmem, flags=flags)


def _adaln_kernel(c_ref, w_ref, b_ref, o_ref):
    c = c_ref[...]
    s = c * jax.nn.sigmoid(c)
    o_ref[...] = jnp.dot(s.astype(BF16), w_ref[...].astype(BF16),
                         preferred_element_type=F32) + b_ref[...]


def _adaln(c, w, b):
    bsz, d = c.shape
    n = w.shape[1]
    rows = 8
    cp = jnp.zeros((rows, d), F32).at[:bsz].set(c)
    tn = _pick(n, (1024, 512, 256, 128))
    out = pl.pallas_call(
        _adaln_kernel,
        grid=(n // tn,),
        in_specs=[pl.BlockSpec((rows, d), lambda j: (0, 0)),
                  pl.BlockSpec((d, tn), lambda j: (0, j)),
                  pl.BlockSpec((1, tn), lambda j: (0, j))],
        out_specs=pl.BlockSpec((rows, tn), lambda j: (0, j)),
        out_shape=jax.ShapeDtypeStruct((rows, n), F32),
        compiler_params=_cparams(("arbitrary",)),
        name="adaln",
    )(cp, w, b.reshape(1, n))
    return out[:bsz]


def _in_proj_kernel(x_ref, sc_ref, sh_ref, w_ref, o_ref, wbf_ref):
    @pl.when(pl.program_id(1) == 0)
    def _():
        wbf_ref[...] = w_ref[...].astype(BF16)

    h = x_ref[...] * (1.0 + sc_ref[0]) + sh_ref[0]
    o_ref[...] = jnp.dot(h.astype(BF16), wbf_ref[...], preferred_element_type=F32)


def _in_proj(x2d, sc, sh, w, seq):
    t, d = x2d.shape
    nc = w.shape[1]
    tm = _pick(seq, (512, 256, 128))
    tn = _pick(nc, (1024, 512, 256, 128))
    per_b = seq // tm
    vec = pl.BlockSpec((1, 1, d), lambda j, i: (i // per_b, 0, 0))
    return pl.pallas_call(
        _in_proj_kernel,
        grid=(nc // tn, t // tm),
        in_specs=[pl.BlockSpec((tm, d), lambda j, i: (i, 0)), vec, vec,
                  pl.BlockSpec((d, tn), lambda j, i: (0, j))],
        out_specs=pl.BlockSpec((tm, tn), lambda j, i: (i, j)),
        out_shape=jax.ShapeDtypeStruct((t, nc), F32),
        scratch_shapes=[pltpu.VMEM((d, tn), BF16)],
        compiler_params=_cparams(("arbitrary", "arbitrary")),
        name="in_proj",
    )(x2d, sc[:, None, :], sh[:, None, :], w)


def _rope_kernel(q_ref, k_ref, v_ref, pos_ref, inv_ref, qo_ref, ko_ref, vo_ref, cos_ref, sin_ref):
    lane = lax.broadcasted_iota(jnp.int32, cos_ref.shape, 1)
    lh = lane % ATT_HEAD_DIM
    half = ROT_DIM // 2

    @pl.when(pl.program_id(1) == 0)
    def _():
        ang = pos_ref[...] * inv_ref[...]
        cos_ref[...] = jnp.where(lh < ROT_DIM, jnp.cos(ang), 1.0)
        sn = jnp.sin(ang)
        sin_ref[...] = jnp.where(lh < half, -sn, jnp.where(lh < ROT_DIM, sn, 0.0))

    cs = cos_ref[...]
    sn = sin_ref[...]

    def rope(t):
        swapped = jnp.where(lh < half,
                            pltpu.roll(t, V7X_LANES - half, axis=1),
                            pltpu.roll(t, half, axis=1))
        return t * cs + swapped * sn

    qo_ref[0, 0] = rope(q_ref[...]) * (ATT_HEAD_DIM ** -0.5)
    ko_ref[0, 0] = rope(k_ref[...])
    vo_ref[0, 0] = v_ref[...]


def _qk_rope(proj, pos_b, inv_lane, bsz, seq, att_w):
    t = proj.shape[0]
    hp = att_w // V7X_LANES
    tm = _pick(seq, (512, 256, 128))
    per_b = seq // tm
    out_spec = pl.BlockSpec((1, 1, tm, V7X_LANES),
                            lambda i, h: (i // per_b, h, i % per_b, 0))
    shp = jax.ShapeDtypeStruct((bsz, hp, seq, V7X_LANES), F32)
    return pl.pallas_call(
        _rope_kernel,
        grid=(t // tm, hp),
        in_specs=[pl.BlockSpec((tm, V7X_LANES), lambda i, h: (i, h)),
                  pl.BlockSpec((tm, V7X_LANES), lambda i, h: (i, hp + h)),
                  pl.BlockSpec((tm, V7X_LANES), lambda i, h: (i, 2 * hp + h)),
                  pl.BlockSpec((tm, V7X_LANES), lambda i, h: (i, 0)),
                  pl.BlockSpec((1, V7X_LANES), lambda i, h: (0, 0))],
        out_specs=[out_spec, out_spec, out_spec],
        out_shape=[shp, shp, shp],
        scratch_shapes=[pltpu.VMEM((tm, V7X_LANES), F32), pltpu.VMEM((tm, V7X_LANES), F32)],
        compiler_params=_cparams(("arbitrary", "arbitrary")),
        name="qk_rope",
    )(proj, proj, proj, pos_b, inv_lane)


ATT_TASKS_PER_STEP = 4


def _attn_kernel(q_in, k_in, v_in, out_ref, qs, ks, vs, no, nl, *, seq):
    qn = ATT_BLOCK
    n_task = seq // qn
    n_head = V7X_LANES // ATT_HEAD_DIM
    gc = ATT_TASKS_PER_STEP
    lane = lax.broadcasted_iota(jnp.int32, (qn, V7X_LANES), 1)
    qi = lax.broadcasted_iota(jnp.int32, (gc, qn, 2 * qn), 1)
    kj = lax.broadcasted_iota(jnp.int32, (gc, qn, 2 * qn), 2)
    dist = qi + qn - kj
    band = (dist >= 0) & (dist <= qn)
    zeros = jnp.zeros((qn, V7X_LANES), BF16)

    for bi, (window, d) in enumerate(DILATED_PAIRS):
        nb = seq // (d * qn)

        def task_rows(r, n, d=d):
            start = r + n * qn * d
            return pl.ds(start, qn) if d == 1 else pl.ds(start, qn, stride=d)

        for r in range(d):
            for n in range(nb):
                g = r * nb + n
                sl = task_rows(r, n)
                qv = q_in[0, 0, sl, :]
                for h in range(n_head):
                    qs[h, g] = jnp.where(lane // ATT_HEAD_DIM == h, qv, 0.0).astype(BF16)
                for src, dst in ((k_in, ks), (v_in, vs)):
                    blk = src[0, 0, sl, :].astype(BF16)
                    dst[g, qn:, :] = blk
                    if n + 1 < nb:
                        dst[g + 1, :qn, :] = blk
                    if n == 0:
                        dst[g, :qn, :] = zeros

        def step(c, carry, bi=bi, d=d, nb=nb):
            g0 = pl.multiple_of(c * gc, gc)
            gsl = pl.ds(g0, gc)
            gidx = g0 + lax.broadcasted_iota(jnp.int32, (gc, qn, 2 * qn), 0)
            valid = band & ((gidx % nb != 0) | (kj >= qn))
            k = ks[gsl]
            v = vs[gsl]
            o = None
            for h in range(n_head):
                s = jnp.einsum("gqd,gkd->gqk", qs[h, gsl], k, preferred_element_type=F32)
                s = jnp.where(valid, s, NEG_INF)
                m = jnp.max(s, axis=-1, keepdims=True)
                p = jnp.exp(s - m)
                den = jnp.sum(p, axis=-1, keepdims=True)
                oh = jnp.einsum("gqk,gkd->gqd", p.astype(BF16), v,
                                preferred_element_type=F32) / den
                lh = jnp.broadcast_to(m + jnp.log(den), oh.shape)
                if o is None:
                    o, lse = oh, lh
                else:
                    sel = lane[None] // ATT_HEAD_DIM == h
                    o, lse = jnp.where(sel, oh, o), jnp.where(sel, lh, lse)
            for t in range(gc):
                g = g0 + t
                start = g // nb + (g % nb) * (qn * d)
                sl = pl.ds(start, qn) if d == 1 else pl.ds(start, qn, stride=d)
                no[bi, sl, :] = o[t]
                nl[bi, sl, :] = lse[t]
            return carry

        lax.fori_loop(0, n_task // gc, step, 0)

    def combine(c, carry):
        sl = pl.ds(pl.multiple_of(c * qn, qn), qn)
        ls = [nl[bi, sl, :] for bi in range(len(DILATED_PAIRS))]
        mx = functools.reduce(jnp.maximum, ls)
        ws = [jnp.exp(l - mx) for l in ls]
        tot = functools.reduce(lambda a, b: a + b, ws)
        acc = functools.reduce(lambda a, b: a + b,
                               [w * no[bi, sl, :] for bi, w in enumerate(ws)])
        out_ref[0, 0, sl, :] = (acc / tot).astype(out_ref.dtype)
        return carry

    lax.fori_loop(0, n_task, combine, 0)


def _attention(q_hp, k_hp, v_hp):
    bsz, hp, seq, _ = q_hp.shape
    for window, d in DILATED_PAIRS:
        assert window // d == ATT_BLOCK and seq % (d * ATT_BLOCK) == 0
    n_task = seq // ATT_BLOCK
    assert n_task % ATT_TASKS_PER_STEP == 0
    n_head = V7X_LANES // ATT_HEAD_DIM
    nbr = len(DILATED_PAIRS)
    spec = pl.BlockSpec((1, 1, seq, V7X_LANES), lambda b, h: (b, h, 0, 0))
    return pl.pallas_call(
        functools.partial(_attn_kernel, seq=seq),
        grid=(bsz, hp),
        in_specs=[spec, spec, spec],
        out_specs=spec,
        out_shape=jax.ShapeDtypeStruct((bsz, hp, seq, V7X_LANES), BF16),
        scratch_shapes=[pltpu.VMEM((n_head, n_task, ATT_BLOCK, V7X_LANES), BF16),
                        pltpu.VMEM((n_task, 2 * ATT_BLOCK, V7X_LANES), BF16),
                        pltpu.VMEM((n_task, 2 * ATT_BLOCK, V7X_LANES), BF16),
                        pltpu.VMEM((nbr, seq, V7X_LANES), F32),
                        pltpu.VMEM((nbr, seq, V7X_LANES), F32)],
        compiler_params=_cparams(("arbitrary", "arbitrary")),
        name="dilated_attn",
    )(q_hp, k_hp, v_hp)


def _hgrn_kernel(q_ref, f_ref, i_ref, g_ref, lb_ref, gw_ref, sums_ref, o_ref,
                 st_ref, hl_ref, kk_ref, cum_ref, *, heads, ts):
    @pl.when(pl.program_id(2) == 0)
    def _():
        st_ref[...] = jnp.zeros_like(st_ref)

    c_len = HG_CHUNK
    lanes = heads * HG_EXPAND
    lb = lb_ref[...]
    for c in range(ts // c_len):
        rows = slice(c * c_len, (c + 1) * c_len)
        f = lb + (1.0 - lb) * jax.nn.sigmoid(f_ref[0, rows, :])
        kk_ref[rows, :] = 1.0 - f
        hi, lo = _bf16_split(jnp.log(f))
        hl_ref[rows, :lanes] = hi
        hl_ref[rows, lanes:] = lo
    for k in range(3):
        both = jnp.dot(sums_ref[k], hl_ref[...], preferred_element_type=F32)
        cum_ref[k] = both[:, :lanes] + both[:, lanes:]

    n_sub = c_len // HG_SUB
    gi = lax.broadcasted_iota(jnp.int32, (heads * n_sub, HG_SUB, c_len), 0) % n_sub
    qi = lax.broadcasted_iota(jnp.int32, (heads * n_sub, HG_SUB, c_len), 1)
    si = lax.broadcasted_iota(jnp.int32, (heads * n_sub, HG_SUB, c_len), 2)
    causal = si <= gi * HG_SUB + qi
    key_row = lax.broadcasted_iota(jnp.int32, (c_len, HG_EXPAND), 0)

    for c in range(ts // c_len):
        rows = slice(c * c_len, (c + 1) * c_len)
        qts, kts, vbs = [], [], []
        for h in range(heads):
            lsl = slice(h * HG_EXPAND, (h + 1) * HG_EXPAND)
            b = cum_ref[0, rows, lsl]
            anchor = cum_ref[1, rows, lsl]
            kk = kk_ref[rows, lsl]
            qt = (q_ref[0, rows, lsl] * jnp.exp(b - anchor)).astype(BF16)
            qts.append(qt.reshape(n_sub, HG_SUB, HG_EXPAND))
            vb = i_ref[0, rows, lsl].astype(BF16)
            for i in range(n_sub):
                hi_r = (i + 1) * HG_SUB
                kt = kk * jnp.exp(anchor[i * HG_SUB:i * HG_SUB + 1, :] - b)
                kts.append(jnp.where(key_row < hi_r, kt, 0.0).astype(BF16))
                vbs.append(vb)
        a = jnp.einsum("gqk,gsk->gqs", jnp.concatenate(qts, axis=0), jnp.stack(kts),
                       preferred_element_type=F32)
        a = jnp.where(causal, a, 0.0).astype(BF16)
        o_intra = jnp.einsum("gqs,gsv->gqv", a, jnp.stack(vbs), preferred_element_type=F32)

        for h in range(heads):
            lsl = slice(h * HG_EXPAND, (h + 1) * HG_EXPAND)
            b = cum_ref[0, rows, lsl]
            b_last = cum_ref[2, rows, lsl]
            kk = kk_ref[rows, lsl]
            q = q_ref[0, rows, lsl]
            v = i_ref[0, rows, lsl]
            st = st_ref[h]
            o_inter = lax.dot_general((q * jnp.exp(b)).astype(BF16), st.astype(BF16), NT_DIMS,
                                      preferred_element_type=F32)
            o = o_inter + o_intra[h * n_sub:(h + 1) * n_sub].reshape(c_len, HG_EXPAND)
            kl = kk * jnp.exp(b_last - b)
            upd = jnp.dot(v.T.astype(BF16), kl.astype(BF16), preferred_element_type=F32)
            st_ref[h] = st * jnp.exp(b_last[0:1, :]) + upd
            o = o * lax.rsqrt(jnp.mean(o * o, axis=-1, keepdims=True) + RMS_EPS)
            g = g_ref[0, rows, lsl]
            o = o * gw_ref[:, lsl] * (g * jax.nn.sigmoid(g))
            o_ref[0, rows, lsl] = o.astype(o_ref.dtype)


def _hgrn_sum_matrices(ts):
    t = np.arange(ts)[:, None]
    s = np.arange(ts)[None, :]
    same = (t // HG_CHUNK) == (s // HG_CHUNK)
    mid = (t // HG_SUB) * HG_SUB + HG_SUB // 2
    return np.stack([same & (s <= t), same & (s <= mid), same]).astype(np.float32)


def _hgrn2(proj3, lb, gw, att_w, hg_w):
    bsz, seq, _ = proj3.shape
    n_heads = hg_w // HG_EXPAND
    heads = _pick(n_heads, (4, 2, 1))
    lanes = heads * HG_EXPAND
    ts = _pick(seq, (256, 128, 64))
    base = 3 * att_w
    assert base % lanes == 0 and hg_w % lanes == 0

    def seg(k):
        off = (base + k * hg_w) // lanes
        return pl.BlockSpec((1, ts, lanes), lambda b, h, s: (b, s, off + h))

    vec = pl.BlockSpec((1, lanes), lambda b, h, s: (0, h))
    sums = jnp.asarray(_hgrn_sum_matrices(ts), BF16)
    return pl.pallas_call(
        functools.partial(_hgrn_kernel, heads=heads, ts=ts),
        grid=(bsz, hg_w // lanes, seq // ts),
        in_specs=[seg(0), seg(1), seg(2), seg(3), vec, vec,
                  pl.BlockSpec((3, ts, ts), lambda b, h, s: (0, 0, 0))],
        out_specs=pl.BlockSpec((1, ts, lanes), lambda b, h, s: (b, s, h)),
        out_shape=jax.ShapeDtypeStruct((bsz, seq, hg_w), BF16),
        scratch_shapes=[pltpu.VMEM((heads, HG_EXPAND, HG_EXPAND), F32),
                        pltpu.VMEM((ts, 2 * lanes), BF16),
                        pltpu.VMEM((ts, lanes), F32),
                        pltpu.VMEM((3, ts, lanes), F32)],
        compiler_params=_cparams(("arbitrary", "arbitrary", "arbitrary")),
        name="hgrn2",
    )(proj3, proj3, proj3, proj3, lb.reshape(1, hg_w), gw.reshape(1, hg_w), sums)


def _layer_norm(y, g, b):
    mu = jnp.mean(y, axis=-1, keepdims=True)
    yc = y - mu
    var = jnp.mean(yc * yc, axis=-1, keepdims=True)
    return yc * lax.rsqrt(var + LN_EPS) * g + b


def _bf16_split(x):
    hi = x.astype(BF16)
    lo = (x - hi.astype(F32)).astype(BF16)
    return hi, lo


def _out_proj_kernel(att_ref, rec_ref, x_ref, wo_ref, gt_ref, sc_ref, sh_ref, g_ref, b_ref,
                     rw_ref, rb_ref, x1_ref, h2_ref, lg_ref, wobf_ref, rwhi_ref, rwlo_ref,
                     *, alpha, att_w):
    @pl.when(pl.program_id(0) == 0)
    def _():
        wobf_ref[...] = wo_ref[...].astype(BF16)
        hi, lo = _bf16_split(rw_ref[...])
        rwhi_ref[...] = hi
        rwlo_ref[...] = lo

    att = jnp.concatenate([att_ref[0, h] for h in range(att_ref.shape[1])], axis=1)
    mix = (jnp.dot(att, wobf_ref[:att_w, :], preferred_element_type=F32)
           + jnp.dot(rec_ref[...], wobf_ref[att_w:, :], preferred_element_type=F32))
    y = alpha * x_ref[...] + (1.0 + gt_ref[0]) * mix
    x1 = _layer_norm(y, g_ref[...], b_ref[...])
    x1_ref[...] = x1
    h2 = x1 * (1.0 + sc_ref[0]) + sh_ref[0]
    hi, lo = _bf16_split(h2)
    lg = (lax.dot_general(rwhi_ref[...], hi, NT_DIMS, preferred_element_type=F32)
          + lax.dot_general(rwhi_ref[...], lo, NT_DIMS, preferred_element_type=F32)
          + lax.dot_general(rwlo_ref[...], hi, NT_DIMS, preferred_element_type=F32))
    lg_ref[...] = lg + rb_ref[...]
    half = h2.shape[1] // 2
    bits = lax.bitcast_convert_type(hi.astype(F32), jnp.uint32)
    h2_ref[...] = (bits[:, :half] >> 16) | (bits[:, half:] & jnp.uint32(0xFFFF0000))


def _out_proj(att, rec, x2d, wo, gt, sc, sh, g, b, rw_t, rb, seq, alpha):
    t, d = x2d.shape
    hp = att.shape[1]
    att_w = hp * V7X_LANES
    n_e = rw_t.shape[0]
    tm = _pick(seq, (256, 128))
    per_b = seq // tm
    vec3 = pl.BlockSpec((1, 1, d), lambda i: (i // per_b, 0, 0))
    full = lambda shape: pl.BlockSpec(shape, lambda i: (0,) * len(shape))
    once = lambda shape: pl.BlockSpec(shape, lambda i: (0,) * len(shape),
                                      pipeline_mode=pl.Buffered(1))
    return pl.pallas_call(
        functools.partial(_out_proj_kernel, alpha=alpha, att_w=att_w),
        grid=(t // tm,),
        in_specs=[pl.BlockSpec((1, hp, tm, V7X_LANES), lambda i: (i // per_b, 0, i % per_b, 0)),
                  pl.BlockSpec((tm, d - att_w), lambda i: (i, 0)),
                  pl.BlockSpec((tm, d), lambda i: (i, 0)),
                  once((d, d)), vec3, vec3, vec3, full((1, d)), full((1, d)),
                  full((n_e, d)), full((n_e, 1))],
        out_specs=[pl.BlockSpec((tm, d), lambda i: (i, 0)),
                   pl.BlockSpec((tm, d // 2), lambda i: (i, 0)),
                   pl.BlockSpec((n_e, tm), lambda i: (0, i))],
        out_shape=[jax.ShapeDtypeStruct((t, d), F32),
                   jax.ShapeDtypeStruct((t, d // 2), jnp.uint32),
                   jax.ShapeDtypeStruct((n_e, t), F32)],
        scratch_shapes=[pltpu.VMEM((d, d), BF16), pltpu.VMEM((n_e, d), BF16),
                        pltpu.VMEM((n_e, d), BF16)],
        compiler_params=_cparams(("arbitrary",)),
        name="out_proj_ln1",
    )(att, rec, x2d, wo, gt[:, None, :], sc[:, None, :], sh[:, None, :],
      g.reshape(1, d), b.reshape(1, d), rw_t, rb.reshape(n_e, 1))


def _routing_kernel(lg_ref, tri_ref, low_ref, gate_ref, dest_ref, tile_ref, pad_ref, sel_scr, rk_scr,
                    *, blk, tm):
    n_e, t = lg_ref.shape
    eidx = lax.broadcasted_iota(jnp.int32, (n_e, t), 0)
    cur = lg_ref[...]
    vals, idxs = [], []
    for _ in range(TOP_K):
        m = jnp.max(cur, axis=0, keepdims=True)
        ik = jnp.min(jnp.where(cur == m, eidx, n_e), axis=0, keepdims=True)
        cur = jnp.where(eidx == ik, -jnp.inf, cur)
        vals.append(m)
        idxs.append(ik)
    es = [jnp.exp(v - vals[0]) for v in vals]
    den = functools.reduce(lambda a, b: a + b, es)
    for k in range(TOP_K):
        gate_ref[k:k + 1, :] = es[k] / den
    sel = functools.reduce(lambda a, b: a | b, [eidx == ik for ik in idxs])
    sel_scr[...] = jnp.where(sel, 1.0, 0.0)

    tri = tri_ref[...]
    carry = jnp.zeros((n_e, 1), F32)
    for j in range(t // blk):
        sb = sel_scr[:, j * blk:(j + 1) * blk]
        pre = jnp.dot(sb.astype(BF16), tri, preferred_element_type=F32)
        rk_scr[:, j * blk:(j + 1) * blk] = pre + carry
        carry = carry + jnp.sum(sb, axis=1, keepdims=True)

    counts = jnp.broadcast_to(carry, (n_e, V7X_LANES))
    padded = jnp.floor((counts + (tm - 1)) * (1.0 / tm)) * tm
    pends = jnp.dot(low_ref[...], padded, precision=lax.Precision.HIGHEST,
                    preferred_element_type=F32)
    pstarts = pends - padded
    row0 = rk_scr[...] + pstarts[:, 0:1]
    for k in range(TOP_K):
        dest_ref[k:k + 1, :] = jnp.sum(jnp.where(eidx == idxs[k], row0, 0.0), axis=0,
                                       keepdims=True).astype(jnp.int32)
    starts = (lax.broadcasted_iota(jnp.int32, (n_e, tile_ref.shape[1]), 1) * tm).astype(F32)
    owner = jnp.sum(jnp.where(pends[:, 0:1] <= starts, 1.0, 0.0), axis=0, keepdims=True)
    tile_ref[0:1, :] = jnp.minimum(owner, n_e - 1.0).astype(jnp.int32)
    tile_ref[1:2, :] = jnp.broadcast_to(jnp.max(pends[:, 0:1], axis=0, keepdims=True),
                                        (1, tile_ref.shape[1])).astype(jnp.int32)
    pad_ref[0] = (pstarts + counts).astype(jnp.int32)
    pad_ref[1] = (padded - counts).astype(jnp.int32)


def _routing(logits_t, tm, n_tiles):
    n_e, t = logits_t.shape
    assert tm & (tm - 1) == 0
    blk = _pick(t, (256, 128))
    tri = jnp.asarray(np.triu(np.ones((blk, blk), np.float32), 1), BF16)
    low = jnp.asarray(np.tril(np.ones((n_e, n_e), np.float32)))
    ntp = -(-n_tiles // V7X_LANES) * V7X_LANES
    full = lambda shape: pl.BlockSpec(shape, lambda: (0,) * len(shape))
    return pl.pallas_call(
        functools.partial(_routing_kernel, blk=blk, tm=tm),
        in_specs=[full((n_e, t)), full((blk, blk)), full((n_e, n_e))],
        out_specs=[full((TOP_K, t)), full((TOP_K, t)), full((2, ntp)), full((2, n_e, V7X_LANES))],
        out_shape=[jax.ShapeDtypeStruct((TOP_K, t), F32),
                   jax.ShapeDtypeStruct((TOP_K, t), jnp.int32),
                   jax.ShapeDtypeStruct((2, ntp), jnp.int32),
                   jax.ShapeDtypeStruct((2, n_e, V7X_LANES), jnp.int32)],
        scratch_shapes=[pltpu.VMEM((n_e, t), F32), pltpu.VMEM((n_e, t), F32)],
        compiler_params=pltpu.CompilerParams(vmem_limit_bytes=V7X_VMEM_LIMIT),
        name="routing",
    )(logits_t, tri, low)


def _dispatch_kernel(dest_ref, pad_off_ref, pad_n_ref, nrow_ref, h_ref, o_ref, zero_ref, sem, zsem,
                     *, n_tok, tq, tm, n_e):
    step = pl.program_id(0)
    base = step * tq

    def fill(off, n):
        return pltpu.make_async_copy(zero_ref.at[pl.ds(0, n)], o_ref.at[pl.ds(off, n)], zsem)

    sub = 8
    pieces = [1 << s for s in range(tm.bit_length() - 2, sub.bit_length() - 2, -1)]

    def pad_rows(wait):
        def per_expert(e, c):
            off = pad_off_ref[e]
            n = pad_n_ref[e]
            head = jnp.minimum(n, (-off) & (sub - 1))
            for s in range(sub - 1):
                @pl.when(s < head)
                def _(s=s):
                    cp = fill(off + s, 1)
                    cp.wait() if wait else cp.start()
            off = off + head
            n = n - head
            for p in pieces:
                hit = (n & p) != 0

                @pl.when(hit)
                def _(off=off, p=p):
                    cp = fill(pl.multiple_of(off, sub), p)
                    cp.wait() if wait else cp.start()

                off = off + jnp.where(hit, p, 0)
            return c
        lax.fori_loop(0, n_e, per_expert, 0)

        def per_tile(i, c):
            cp = fill(pl.multiple_of(i * tm, tm), tm)
            cp.wait() if wait else cp.start()
            return c
        lax.fori_loop(nrow_ref[0] // tm, o_ref.shape[0] // tm, per_tile, 0)

    @pl.when(step == 0)
    def _():
        zero_ref[...] = jnp.zeros_like(zero_ref)
        pad_rows(False)

    def copy(r, k):
        dst = o_ref.at[pl.ds(dest_ref[k * n_tok + base + r], 1)]
        return pltpu.make_async_copy(h_ref.at[pl.ds(r, 1)], dst, sem)

    def start(r, c):
        for k in range(TOP_K):
            copy(r, k).start(priority=k % 2)
        return c

    def wait(r, c):
        for k in range(TOP_K):
            copy(r, k).wait()
        return c

    lax.fori_loop(0, tq, start, 0, unroll=2)
    lax.fori_loop(0, tq, wait, 0, unroll=2)

    @pl.when(step == 0)
    def _():
        pad_rows(True)


def _dispatch(h2p, dest_flat, pad_off, pad_n, n_used_rows, n_rows, tm):
    t, width = h2p.shape
    n_e = pad_off.shape[0]
    tq = _pick(t, (256, 128))
    return pl.pallas_call(
        functools.partial(_dispatch_kernel, n_tok=t, tq=tq, tm=tm, n_e=n_e),
        grid_spec=pltpu.PrefetchScalarGridSpec(
            num_scalar_prefetch=4, grid=(t // tq,),
            in_specs=[pl.BlockSpec((tq, width), lambda i, *_: (i, 0))],
            out_specs=pl.BlockSpec(memory_space=pl.ANY),
            scratch_shapes=[pltpu.VMEM((tm, width), h2p.dtype),
                            pltpu.SemaphoreType.DMA(()), pltpu.SemaphoreType.DMA(())]),
        out_shape=jax.ShapeDtypeStruct((n_rows, width), h2p.dtype),
        compiler_params=pltpu.CompilerParams(dimension_semantics=("arbitrary",),
                                             has_side_effects=True,
                                             vmem_limit_bytes=V7X_VMEM_LIMIT),
        name="moe_dispatch",
    )(dest_flat, pad_off, pad_n, n_used_rows, h2p)


def _unpack_bf16_pair(words):
    lo = lax.bitcast_convert_type(words << 16, F32).astype(BF16)
    hi = lax.bitcast_convert_type(words & jnp.uint32(0xFFFF0000), F32).astype(BF16)
    return lo, hi


EXPERT_COL_BLOCK = 2048
CAST_ROWS = 256


def _stream_expert_weights(te_ref, nt_ref, w_hbm, stage_ref, wbf_ref, sem):
    j, i = pl.program_id(0), pl.program_id(1)
    n_pass, n_tiles = pl.num_programs(0), pl.num_programs(1)
    tn = stage_ref.shape[1]
    n_used = nt_ref[0]
    cur = te_ref[i]

    def fetch(e, jj):
        src = w_hbm.at[e, :, pl.ds(pl.multiple_of(jj * tn, tn), tn)]
        return pltpu.make_async_copy(src, stage_ref, sem)

    @pl.when((i == 0) & (j == 0))
    def _():
        fetch(te_ref[0], 0).start()

    @pl.when((i < n_used) & ((i == 0) | (cur != te_ref[jnp.maximum(i - 1, 0)])))
    def _():
        fetch(cur, j).wait()

        def cast(r, c):
            rows = pl.ds(pl.multiple_of(r * CAST_ROWS, CAST_ROWS), CAST_ROWS)
            wbf_ref[rows, :] = stage_ref[rows, :].astype(BF16)
            return c
        lax.fori_loop(0, stage_ref.shape[0] // CAST_ROWS, cast, 0)

        def same_group(k):
            return (k < n_used) & (te_ref[jnp.minimum(k, n_tiles - 1)] == cur)
        nxt = lax.while_loop(same_group, lambda k: k + 1, i + 1)

        @pl.when(nxt < n_used)
        def _():
            fetch(te_ref[jnp.minimum(nxt, n_tiles - 1)], j).start()

        @pl.when((nxt >= n_used) & (j + 1 < n_pass))
        def _():
            fetch(te_ref[0], j + 1).start()


def _gemm1_kernel(te_ref, nt_ref, x_ref, w_hbm, b_ref, cmp_ref, o_ref, stage_ref, wbf_ref, sem, *, nc):
    _stream_expert_weights(te_ref, nt_ref, w_hbm, stage_ref, wbf_ref, sem)
    i = pl.program_id(1)

    @pl.when(i < nt_ref[0])
    def _():
        x_lo, x_hi = _unpack_bf16_pair(x_ref[...])
        half = x_lo.shape[1]
        cw = cmp_ref.shape[0]
        lane = lax.broadcasted_iota(jnp.int32, (x_lo.shape[0], nc), 1)
        for n0 in range(0, wbf_ref.shape[1], nc):
            hb = (jnp.dot(x_lo, wbf_ref[:half, n0:n0 + nc], preferred_element_type=F32)
                  + jnp.dot(x_hi, wbf_ref[half:, n0:n0 + nc], preferred_element_type=F32)
                  + b_ref[0, :, n0:n0 + nc])
            nxt = pltpu.roll(hb, nc - 1, axis=1)
            glu = jnp.minimum(hb, SWIGLU_LIMIT)
            lin = jnp.clip(nxt, -SWIGLU_LIMIT, SWIGLU_LIMIT)
            act = glu * jax.nn.sigmoid(SWIGLU_ALPHA * glu) * (lin + 1.0)
            act = jnp.where(lane % 2 == 0, act, 0.0).astype(BF16)
            for c in range(nc // cw):
                o0 = (n0 + c * cw) // 2
                o_ref[:, o0:o0 + cw // 2] = jnp.dot(
                    act[:, c * cw:(c + 1) * cw], cmp_ref[...],
                    preferred_element_type=F32).astype(o_ref.dtype)

    @pl.when(i >= nt_ref[0])
    def _():
        o_ref[...] = jnp.zeros_like(o_ref)


def _gemm1(xin, w1, b1, tile_e, n_tiles_used, tm):
    n_rows, half_d = xin.shape
    n_e, d, f2 = w1.shape
    tn = min(EXPERT_COL_BLOCK, f2)
    nc = min(512, tn)
    assert f2 % tn == 0 and d % CAST_ROWS == 0
    n_tiles = n_rows // tm
    cw = V7X_MXU_DIM
    cmp_np = np.zeros((cw, cw // 2), np.float32)
    cmp_np[np.arange(0, cw, 2), np.arange(cw // 2)] = 1.0

    def used(i, nt):
        return jnp.minimum(i, nt[0] - 1)

    return pl.pallas_call(
        functools.partial(_gemm1_kernel, nc=nc),
        grid_spec=pltpu.PrefetchScalarGridSpec(
            num_scalar_prefetch=2, grid=(f2 // tn, n_tiles),
            in_specs=[pl.BlockSpec((tm, half_d), lambda j, i, te, nt: (used(i, nt), 0)),
                      pl.BlockSpec(memory_space=pl.ANY),
                      pl.BlockSpec((1, 1, tn), lambda j, i, te, nt: (te[used(i, nt)], 0, j)),
                      pl.BlockSpec((cw, cw // 2), lambda j, i, te, nt: (0, 0))],
            out_specs=pl.BlockSpec((tm, tn // 2), lambda j, i, te, nt: (i, j)),
            scratch_shapes=[pltpu.VMEM((d, tn), F32), pltpu.VMEM((d, tn), BF16),
                            pltpu.SemaphoreType.DMA(())]),
        out_shape=jax.ShapeDtypeStruct((n_rows, f2 // 2), BF16),
        compiler_params=_cparams(("arbitrary", "arbitrary")),
        name="moe_gemm1",
    )(tile_e, n_tiles_used, xin, w1, b1.reshape(n_e, 1, f2), jnp.asarray(cmp_np, BF16))


def _gemm2_kernel(te_ref, nt_ref, a_ref, w_hbm, b_ref, o_ref, stage_ref, wbf_ref, sem, *, nc):
    _stream_expert_weights(te_ref, nt_ref, w_hbm, stage_ref, wbf_ref, sem)
    i = pl.program_id(1)

    @pl.when(i < nt_ref[0])
    def _():
        a = a_ref[...]
        for n0 in range(0, wbf_ref.shape[1], nc):
            o_ref[:, n0:n0 + nc] = (jnp.dot(a, wbf_ref[:, n0:n0 + nc], preferred_element_type=F32)
                                    + b_ref[0, :, n0:n0 + nc])

    @pl.when(i >= nt_ref[0])
    def _():
        o_ref[...] = jnp.zeros_like(o_ref)


def _gemm2(act, w2, b2, tile_e, n_tiles_used, tm):
    n_rows, f = act.shape
    n_e, _, d = w2.shape
    tn = min(EXPERT_COL_BLOCK, d)
    nc = min(512, tn)
    assert d % tn == 0 and f % CAST_ROWS == 0
    n_tiles = n_rows // tm

    def used(i, nt):
        return jnp.minimum(i, nt[0] - 1)

    return pl.pallas_call(
        functools.partial(_gemm2_kernel, nc=nc),
        grid_spec=pltpu.PrefetchScalarGridSpec(
            num_scalar_prefetch=2, grid=(d // tn, n_tiles),
            in_specs=[pl.BlockSpec((tm, f), lambda j, i, te, nt: (used(i, nt), 0)),
                      pl.BlockSpec(memory_space=pl.ANY),
                      pl.BlockSpec((1, 1, tn), lambda j, i, te, nt: (te[used(i, nt)], 0, j))],
            out_specs=pl.BlockSpec((tm, tn), lambda j, i, te, nt: (i, j)),
            scratch_shapes=[pltpu.VMEM((f, tn), F32), pltpu.VMEM((f, tn), BF16),
                            pltpu.SemaphoreType.DMA(())]),
        out_shape=jax.ShapeDtypeStruct((n_rows, d), F32),
        compiler_params=_cparams(("arbitrary", "arbitrary")),
        name="moe_gemm2",
    )(tile_e, n_tiles_used, act, w2, b2.reshape(n_e, 1, d))


def _combine_kernel(dest_ref, x1_ref, gate_ref, gt_ref, g_ref, b_ref, y_ref, o_ref, buf, sem,
                    *, alpha, tc, n_tok):
    i = pl.program_id(0)
    n_steps = pl.num_programs(0)

    def copy(step, slot, k, r):
        src = y_ref.at[pl.ds(dest_ref[k * n_tok + step * tc + r], 1)]
        return pltpu.make_async_copy(src, buf.at[slot, k, pl.ds(r, 1)], sem.at[slot])

    def issue(step, slot):
        def body(r, c):
            for k in range(TOP_K):
                copy(step, slot, k, r).start()
            return c
        lax.fori_loop(0, tc, body, 0)

    def drain(step, slot):
        def body(r, c):
            for k in range(TOP_K):
                copy(step, slot, k, r).wait()
            return c
        lax.fori_loop(0, tc, body, 0)

    slot = i % 2

    @pl.when(i == 0)
    def _():
        issue(0, 0)

    @pl.when(i + 1 < n_steps)
    def _():
        issue(i + 1, 1 - slot)

    drain(i, slot)
    gates = gate_ref[...]
    ff = gates[:, 0:1] * buf[slot, 0]
    for k in range(1, TOP_K):
        ff = ff + gates[:, k:k + 1] * buf[slot, k]
    y = alpha * x1_ref[...] + (1.0 + gt_ref[0]) * ff
    o_ref[...] = _layer_norm(y, g_ref[...], b_ref[...])


def _combine(dest_flat, x1, gates_tk, gt, g, b, y, seq, alpha):
    t, d = x1.shape
    tc = _pick(seq, (128,))
    per_b = seq // tc
    return pl.pallas_call(
        functools.partial(_combine_kernel, alpha=alpha, tc=tc, n_tok=t),
        grid_spec=pltpu.PrefetchScalarGridSpec(
            num_scalar_prefetch=1, grid=(t // tc,),
            in_specs=[pl.BlockSpec((tc, d), lambda i, ds: (i, 0)),
                      pl.BlockSpec((tc, TOP_K), lambda i, ds: (i, 0)),
                      pl.BlockSpec((1, 1, d), lambda i, ds: (i // per_b, 0, 0)),
                      pl.BlockSpec((1, d), lambda i, ds: (0, 0)),
                      pl.BlockSpec((1, d), lambda i, ds: (0, 0)),
                      pl.BlockSpec(memory_space=pl.ANY)],
            out_specs=pl.BlockSpec((tc, d), lambda i, ds: (i, 0)),
            scratch_shapes=[pltpu.VMEM((2, TOP_K, tc, d), F32),
                            pltpu.SemaphoreType.DMA((2,))]),
        out_shape=jax.ShapeDtypeStruct((t, d), F32),
        compiler_params=_cparams(("arbitrary",)),
        name="moe_combine_ln2",
    )(dest_flat, x1, gates_tk, gt[:, None, :], g.reshape(1, d), b.reshape(1, d), y)


MOE_ROW_TILE = 512


def _moe_ffn(h2p, logits_t, x1, gt_f, ln_g, ln_b, w1, b1, w2, b2, seq, alpha):
    tm = MOE_ROW_TILE
    n_e, t = logits_t.shape
    n_rows = -(-(TOP_K * t) // tm) * tm + n_e * tm
    n_tiles = n_rows // tm
    gate_t, dest, tiles, pads = _routing(logits_t, tm, n_tiles)
    dest = dest.reshape(-1)
    tile_e = tiles[0, :n_tiles]
    n_used_rows = tiles[1, :1]
    n_used_tiles = n_used_rows // tm
    xin = _dispatch(h2p, dest, pads[0, :, 0], pads[1, :, 0], n_used_rows, n_rows, tm)
    act = _gemm1(xin, w1, b1, tile_e, n_used_tiles, tm)
    y = _gemm2(act, w2, b2, tile_e, n_used_tiles, tm)
    return _combine(dest, x1, gate_t.T, gt_f, ln_g, ln_b, y, seq, alpha)


def kernel(x, c, positions, w_ada, b_ada, w_in, hgrn_lb, gnorm_w, w_o, ln1_g, ln1_b,
           router_w, router_b, w1, b1, w2, b2, ln2_g, ln2_b):
    bsz, seq, d = x.shape
    depth = w_ada.shape[0]
    t = bsz * seq
    att_w = d // 2
    hg_w = d - att_w
    alpha = (2.0 * depth) ** 0.25

    lb_all = jnp.cumsum(jax.nn.softmax(hgrn_lb.astype(F32), axis=0), axis=0)
    inv = ROPE_THETA ** (-(jnp.arange(0, ROT_DIM, 2, dtype=F32) / ROT_DIM))
    lane = np.arange(V7X_LANES)
    inv_lane = inv[(lane % ATT_HEAD_DIM) % (ROT_DIM // 2)].reshape(1, V7X_LANES)
    pos_b = jnp.broadcast_to(positions.astype(F32).reshape(t, 1), (t, V7X_LANES))

    x2d = x.reshape(t, d)
    for l in range(depth):
        mod = _adaln(c, w_ada[l], b_ada[l])
        sh_a, sc_a, gt_a, sh_f, sc_f, gt_f = jnp.split(mod, 6, axis=-1)

        proj = _in_proj(x2d, sc_a, sh_a, w_in[l], seq)
        proj3 = proj.reshape(bsz, seq, proj.shape[1])
        q_hp, k_hp, v_hp = _qk_rope(proj, pos_b, inv_lane, bsz, seq, att_w)
        att = _attention(q_hp, k_hp, v_hp)
        rec = _hgrn2(proj3, lb_all[l], gnorm_w[l], att_w, hg_w).reshape(t, hg_w)

        x1, h2p, logits_t = _out_proj(att, rec, x2d, w_o[l], gt_a, sc_f, sh_f, ln1_g[l], ln1_b[l],
                                      router_w[l].T, router_b[l], seq, alpha)

        x2d = _moe_ffn(h2p, logits_t, x1, gt_f, ln2_g[l], ln2_b[l], w1[l], b1[l], w2[l], b2[l],
                       seq, alpha)
    return x2d.reshape(bsz, seq, d)
```

```python
import functools

import numpy as np
import jax
import jax.numpy as jnp
from jax import lax
from jax.experimental import pallas as pl
from jax.experimental.pallas import tpu as pltpu

F32 = jnp.float32
BF16 = jnp.bfloat16

V7X_LANES = 128
V7X_MXU_DIM = 256
V7X_VMEM_LIMIT = 56 * 1024 * 1024

ATT_HEAD_DIM = 64
DILATED_PAIRS = ((128, 1), (512, 4), (2048, 16))
ATT_BLOCK = 128
ROT_DIM = ATT_HEAD_DIM // 4
ROPE_THETA = 500000.0
HG_EXPAND = 128
HG_CHUNK = 64
HG_SUB = 16
TOP_K = 4
SWIGLU_ALPHA = 1.702
SWIGLU_LIMIT = 7.0
LN_EPS = 1e-5
RMS_EPS = 1e-6
NEG_INF = -1e30

NT_DIMS = (((1,), (1,)), ((), ()))


def _pick(n, candidates):
    for c in candidates:
        if n % c == 0:
            return c
    raise ValueError(f"no tile in {candidates} divides {n}")


def _cparams(sem, vmem=V7X_VMEM_LIMIT, flags=None):
    return pltpu.CompilerParams(dimension_semantics=sem, vmem_limit_bytes=vmem, flags=flags)


def _adaln_kernel(c_ref, w_ref, b_ref, o_ref):
    c = c_ref[...]
    s = c * jax.nn.sigmoid(c)
    o_ref[...] = jnp.dot(s.astype(BF16), w_ref[...].astype(BF16),
                         preferred_element_type=F32) + b_ref[...]


def _adaln(c, w, b):
    bsz, d = c.shape
    n = w.shape[1]
    rows = 8
    cp = jnp.zeros((rows, d), F32).at[:bsz].set(c)
    tn = _pick(n, (1024, 512, 256, 128))
    out = pl.pallas_call(
        _adaln_kernel,
        grid=(n // tn,),
        in_specs=[pl.BlockSpec((rows, d), lambda j: (0, 0)),
                  pl.BlockSpec((d, tn), lambda j: (0, j)),
                  pl.BlockSpec((1, tn), lambda j: (0, j))],
        out_specs=pl.BlockSpec((rows, tn), lambda j: (0, j)),
        out_shape=jax.ShapeDtypeStruct((rows, n), F32),
        compiler_params=_cparams(("arbitrary",)),
        name="adaln",
    )(cp, w, b.reshape(1, n))
    return out[:bsz]


def _in_proj_kernel(x_ref, sc_ref, sh_ref, w_ref, o_ref, wbf_ref):
    @pl.when(pl.program_id(1) == 0)
    def _():
        wbf_ref[...] = w_ref[...].astype(BF16)

    h = x_ref[...] * (1.0 + sc_ref[0]) + sh_ref[0]
    o_ref[...] = jnp.dot(h.astype(BF16), wbf_ref[...], preferred_element_type=F32)


def _in_proj(x2d, sc, sh, w, seq):
    t, d = x2d.shape
    nc = w.shape[1]
    tm = _pick(seq, (512, 256, 128))
    tn = _pick(nc, (1024, 512, 256, 128))
    per_b = seq // tm
    vec = pl.BlockSpec((1, 1, d), lambda j, i: (i // per_b, 0, 0))
    return pl.pallas_call(
        _in_proj_kernel,
        grid=(nc // tn, t // tm),
        in_specs=[pl.BlockSpec((tm, d), lambda j, i: (i, 0)), vec, vec,
                  pl.BlockSpec((d, tn), lambda j, i: (0, j))],
        out_specs=pl.BlockSpec((tm, tn), lambda j, i: (i, j)),
        out_shape=jax.ShapeDtypeStruct((t, nc), F32),
        scratch_shapes=[pltpu.VMEM((d, tn), BF16)],
        compiler_params=_cparams(("arbitrary", "arbitrary")),
        name="in_proj",
    )(x2d, sc[:, None, :], sh[:, None, :], w)


def _rope_kernel(q_ref, k_ref, v_ref, pos_ref, inv_ref, qo_ref, ko_ref, vo_ref):
    tm = pos_ref.shape[0]
    lane = lax.broadcasted_iota(jnp.int32, (tm, V7X_LANES), 1)
    lh = lane % ATT_HEAD_DIM
    half = ROT_DIM // 2
    ang = pos_ref[...] * inv_ref[...]
    cs = jnp.where(lh < ROT_DIM, jnp.cos(ang), 1.0)
    sn = jnp.sin(ang)
    sn = jnp.where(lh < half, -sn, jnp.where(lh < ROT_DIM, sn, 0.0))

    def rope(t):
        swapped = jnp.where(lh < half,
                            pltpu.roll(t, V7X_LANES - half, axis=1),
                            pltpu.roll(t, half, axis=1))
        return t * cs + swapped * sn

    for h in range(qo_ref.shape[1]):
        lanes = slice(h * V7X_LANES, (h + 1) * V7X_LANES)
        qo_ref[0, h] = rope(q_ref[:, lanes].astype(F32)) * (ATT_HEAD_DIM ** -0.5)
        ko_ref[0, h] = rope(k_ref[:, lanes].astype(F32))
        vo_ref[0, h] = v_ref[:, lanes].astype(F32)


def _qk_rope(proj, pos_b, inv_lane, bsz, seq, att_w):
    t = proj.shape[0]
    hp = att_w // V7X_LANES
    tm = _pick(seq, (256, 128))
    per_b = seq // tm
    out_spec = pl.BlockSpec((1, hp, tm, V7X_LANES), lambda i: (i // per_b, 0, i % per_b, 0))
    shp = jax.ShapeDtypeStruct((bsz, hp, seq, V7X_LANES), F32)
    return pl.pallas_call(
        _rope_kernel,
        grid=(t // tm,),
        in_specs=[pl.BlockSpec((tm, att_w), lambda i: (i, 0)),
                  pl.BlockSpec((tm, att_w), lambda i: (i, 1)),
                  pl.BlockSpec((tm, att_w), lambda i: (i, 2)),
                  pl.BlockSpec((tm, V7X_LANES), lambda i: (i, 0)),
                  pl.BlockSpec((1, V7X_LANES), lambda i: (0, 0))],
        out_specs=[out_spec, out_spec, out_spec],
        out_shape=[shp, shp, shp],
        compiler_params=_cparams(("arbitrary",)),
        name="qk_rope",
    )(proj, proj, proj, pos_b, inv_lane)


ATT_TASKS_PER_STEP = 4


def _attn_kernel(q_in, k_in, v_in, out_ref, qs, ks, vs, no, nl, *, seq):
    qn = ATT_BLOCK
    n_task = seq // qn
    n_head = V7X_LANES // ATT_HEAD_DIM
    gc = ATT_TASKS_PER_STEP
    lane = lax.broadcasted_iota(jnp.int32, (qn, V7X_LANES), 1)
    qi = lax.broadcasted_iota(jnp.int32, (gc, qn, 2 * qn), 1)
    kj = lax.broadcasted_iota(jnp.int32, (gc, qn, 2 * qn), 2)
    dist = qi + qn - kj
    band = (dist >= 0) & (dist <= qn)
    zeros = jnp.zeros((qn, V7X_LANES), BF16)

    for bi, (window, d) in enumerate(DILATED_PAIRS):
        nb = seq // (d * qn)

        def task_rows(r, n, d=d):
            start = r + n * qn * d
            return pl.ds(start, qn) if d == 1 else pl.ds(start, qn, stride=d)

        for r in range(d):
            for n in range(nb):
                g = r * nb + n
                sl = task_rows(r, n)
                qv = q_in[0, 0, sl, :]
                for h in range(n_head):
                    qs[h, g] = jnp.where(lane // ATT_HEAD_DIM == h, qv, 0.0).astype(BF16)
                for src, dst in ((k_in, ks), (v_in, vs)):
                    blk = src[0, 0, sl, :].astype(BF16)
                    dst[g, qn:, :] = blk
                    if n + 1 < nb:
                        dst[g + 1, :qn, :] = blk
                    if n == 0:
                        dst[g, :qn, :] = zeros

        def step(c, carry, bi=bi, d=d, nb=nb):
            g0 = pl.multiple_of(c * gc, gc)
            gsl = pl.ds(g0, gc)
            gidx = g0 + lax.broadcasted_iota(jnp.int32, (gc, qn, 2 * qn), 0)
            valid = band & ((gidx % nb != 0) | (kj >= qn))
            k = ks[gsl]
            v = vs[gsl]
            o = None
            for h in range(n_head):
                s = jnp.einsum("gqd,gkd->gqk", qs[h, gsl], k, preferred_element_type=F32)
                s = jnp.where(valid, s, NEG_INF)
                m = jnp.max(s, axis=-1, keepdims=True)
                p = jnp.exp(s - m)
                den = jnp.sum(p, axis=-1, keepdims=True)
                oh = jnp.einsum("gqk,gkd->gqd", p.astype(BF16), v,
                                preferred_element_type=F32) / den
                lh = jnp.broadcast_to(m + jnp.log(den), oh.shape)
                if o is None:
                    o, lse = oh, lh
                else:
                    sel = lane[None] // ATT_HEAD_DIM == h
                    o, lse = jnp.where(sel, oh, o), jnp.where(sel, lh, lse)
            for t in range(gc):
                g = g0 + t
                start = g // nb + (g % nb) * (qn * d)
                sl = pl.ds(start, qn) if d == 1 else pl.ds(start, qn, stride=d)
                no[bi, sl, :] = o[t]
                nl[bi, sl, :] = lse[t]
            return carry

        lax.fori_loop(0, n_task // gc, step, 0)

    def combine(c, carry):
        sl = pl.ds(pl.multiple_of(c * qn, qn), qn)
        ls = [nl[bi, sl, :] for bi in range(len(DILATED_PAIRS))]
        mx = functools.reduce(jnp.maximum, ls)
        ws = [jnp.exp(l - mx) for l in ls]
        tot = functools.reduce(lambda a, b: a + b, ws)
        acc = functools.reduce(lambda a, b: a + b,
                               [w * no[bi, sl, :] for bi, w in enumerate(ws)])
        out_ref[0, 0, sl, :] = (acc / tot).astype(out_ref.dtype)
        return carry

    lax.fori_loop(0, n_task, combine, 0)


def _attention(q_hp, k_hp, v_hp):
    bsz, hp, seq, _ = q_hp.shape
    for window, d in DILATED_PAIRS:
        assert window // d == ATT_BLOCK and seq % (d * ATT_BLOCK) == 0
    n_task = seq // ATT_BLOCK
    assert n_task % ATT_TASKS_PER_STEP == 0
    n_head = V7X_LANES // ATT_HEAD_DIM
    nbr = len(DILATED_PAIRS)
    spec = pl.BlockSpec((1, 1, seq, V7X_LANES), lambda b, h: (b, h, 0, 0))
    return pl.pallas_call(
        functools.partial(_attn_kernel, seq=seq),
        grid=(bsz, hp),
        in_specs=[spec, spec, spec],
        out_specs=spec,
        out_shape=jax.ShapeDtypeStruct((bsz, hp, seq, V7X_LANES), BF16),
        scratch_shapes=[pltpu.VMEM((n_head, n_task, ATT_BLOCK, V7X_LANES), BF16),
                        pltpu.VMEM((n_task, 2 * ATT_BLOCK, V7X_LANES), BF16),
                        pltpu.VMEM((n_task, 2 * ATT_BLOCK, V7X_LANES), BF16),
                        pltpu.VMEM((nbr, seq, V7X_LANES), F32),
                        pltpu.VMEM((nbr, seq, V7X_LANES), F32)],
        compiler_params=_cparams(("arbitrary", "arbitrary")),
        name="dilated_attn",
    )(q_hp, k_hp, v_hp)


def _hgrn_kernel(q_ref, f_ref, i_ref, g_ref, lb_ref, gw_ref, sums_ref, o_ref,
                 st_ref, hl_ref, kk_ref, cum_ref, *, heads, ts):
    @pl.when(pl.program_id(2) == 0)
    def _():
        st_ref[...] = jnp.zeros_like(st_ref)

    c_len = HG_CHUNK
    lanes = heads * HG_EXPAND
    lb = lb_ref[...]
    for c in range(ts // c_len):
        rows = slice(c * c_len, (c + 1) * c_len)
        f = lb + (1.0 - lb) * jax.nn.sigmoid(f_ref[0, rows, :])
        kk_ref[rows, :] = 1.0 - f
        hi, lo = _bf16_split(jnp.log(f))
        hl_ref[rows, :lanes] = hi
        hl_ref[rows, lanes:] = lo
    for k in range(3):
        both = jnp.dot(sums_ref[k], hl_ref[...], preferred_element_type=F32)
        cum_ref[k] = both[:, :lanes] + both[:, lanes:]

    n_sub = c_len // HG_SUB
    gi = lax.broadcasted_iota(jnp.int32, (heads * n_sub, HG_SUB, c_len), 0) % n_sub
    qi = lax.broadcasted_iota(jnp.int32, (heads * n_sub, HG_SUB, c_len), 1)
    si = lax.broadcasted_iota(jnp.int32, (heads * n_sub, HG_SUB, c_len), 2)
    causal = si <= gi * HG_SUB + qi
    key_row = lax.broadcasted_iota(jnp.int32, (c_len, HG_EXPAND), 0)

    for c in range(ts // c_len):
        rows = slice(c * c_len, (c + 1) * c_len)
        qts, kts, vbs = [], [], []
        for h in range(heads):
            lsl = slice(h * HG_EXPAND, (h + 1) * HG_EXPAND)
            b = cum_ref[0, rows, lsl]
            anchor = cum_ref[1, rows, lsl]
            kk = kk_ref[rows, lsl]
            qt = (q_ref[0, rows, lsl] * jnp.exp(b - anchor)).astype(BF16)
            qts.append(qt.reshape(n_sub, HG_SUB, HG_EXPAND))
            vb = i_ref[0, rows, lsl].astype(BF16)
            for i in range(n_sub):
                hi_r = (i + 1) * HG_SUB
                kt = kk * jnp.exp(anchor[i * HG_SUB:i * HG_SUB + 1, :] - b)
                kts.append(jnp.where(key_row < hi_r, kt, 0.0).astype(BF16))
                vbs.append(vb)
        a = jnp.einsum("gqk,gsk->gqs", jnp.concatenate(qts, axis=0), jnp.stack(kts),
                       preferred_element_type=F32)
        a = jnp.where(causal, a, 0.0).astype(BF16)
        o_intra = jnp.einsum("gqs,gsv->gqv", a, jnp.stack(vbs), preferred_element_type=F32)

        for h in range(heads):
            lsl = slice(h * HG_EXPAND, (h + 1) * HG_EXPAND)
            b = cum_ref[0, rows, lsl]
            b_last = cum_ref[2, rows, lsl]
            kk = kk_ref[rows, lsl]
            q = q_ref[0, rows, lsl]
            v = i_ref[0, rows, lsl]
            st = st_ref[h]
            o_inter = lax.dot_general((q * jnp.exp(b)).astype(BF16), st.astype(BF16), NT_DIMS,
                                      preferred_element_type=F32)
            o = o_inter + o_intra[h * n_sub:(h + 1) * n_sub].reshape(c_len, HG_EXPAND)
            kl = kk * jnp.exp(b_last - b)
            upd = jnp.dot(v.T.astype(BF16), kl.astype(BF16), preferred_element_type=F32)
            st_ref[h] = st * jnp.exp(b_last[0:1, :]) + upd
            o = o * lax.rsqrt(jnp.mean(o * o, axis=-1, keepdims=True) + RMS_EPS)
            g = g_ref[0, rows, lsl]
            o = o * gw_ref[:, lsl] * (g * jax.nn.sigmoid(g))
            o_ref[0, rows, lsl] = o.astype(o_ref.dtype)


def _hgrn_sum_matrices(ts):
    t = np.arange(ts)[:, None]
    s = np.arange(ts)[None, :]
    same = (t // HG_CHUNK) == (s // HG_CHUNK)
    mid = (t // HG_SUB) * HG_SUB + HG_SUB // 2
    return np.stack([same & (s <= t), same & (s <= mid), same]).astype(np.float32)


def _hgrn2(proj3, lb, gw, att_w, hg_w):
    bsz, seq, _ = proj3.shape
    n_heads = hg_w // HG_EXPAND
    heads = _pick(n_heads, (4, 2, 1))
    lanes = heads * HG_EXPAND
    ts = _pick(seq, (256, 128, 64))
    base = 3 * att_w
    assert base % lanes == 0 and hg_w % lanes == 0

    def seg(k):
        off = (base + k * hg_w) // lanes
        return pl.BlockSpec((1, ts, lanes), lambda b, h, s: (b, s, off + h))

    vec = pl.BlockSpec((1, lanes), lambda b, h, s: (0, h))
    sums = jnp.asarray(_hgrn_sum_matrices(ts), BF16)
    return pl.pallas_call(
        functools.partial(_hgrn_kernel, heads=heads, ts=ts),
        grid=(bsz, hg_w // lanes, seq // ts),
        in_specs=[seg(0), seg(1), seg(2), seg(3), vec, vec,
                  pl.BlockSpec((3, ts, ts), lambda b, h, s: (0, 0, 0))],
        out_specs=pl.BlockSpec((1, ts, lanes), lambda b, h, s: (b, s, h)),
        out_shape=jax.ShapeDtypeStruct((bsz, seq, hg_w), BF16),
        scratch_shapes=[pltpu.VMEM((heads, HG_EXPAND, HG_EXPAND), F32),
                        pltpu.VMEM((ts, 2 * lanes), BF16),
                        pltpu.VMEM((ts, lanes), F32),
                        pltpu.VMEM((3, ts, lanes), F32)],
        compiler_params=_cparams(("arbitrary", "arbitrary", "arbitrary")),
        name="hgrn2",
    )(proj3, proj3, proj3, proj3, lb.reshape(1, hg_w), gw.reshape(1, hg_w), sums)


def _layer_norm(y, g, b):
    mu = jnp.mean(y, axis=-1, keepdims=True)
    yc = y - mu
    var = jnp.mean(yc * yc, axis=-1, keepdims=True)
    return yc * lax.rsqrt(var + LN_EPS) * g + b


def _bf16_split(x):
    hi = x.astype(BF16)
    lo = (x - hi.astype(F32)).astype(BF16)
    return hi, lo


def _pack_bf16_pair(lo, hi):
    lo_bits = lax.bitcast_convert_type(lo.astype(BF16).astype(F32), jnp.uint32)
    hi_bits = lax.bitcast_convert_type(hi.astype(BF16).astype(F32), jnp.uint32)
    return (lo_bits >> 16) | (hi_bits & jnp.uint32(0xFFFF0000))


def _unpack_bf16_pair(words):
    lo = lax.bitcast_convert_type(words << 16, F32)
    hi = lax.bitcast_convert_type(words & jnp.uint32(0xFFFF0000), F32)
    return lo, hi


def _load_weight_bf16(w_hbm, wbf_ref, stage_ref, sem):
    rows = stage_ref.shape[1]
    n_slab = w_hbm.shape[0] // rows

    def fetch(s, slot):
        return pltpu.make_async_copy(w_hbm.at[pl.ds(s * rows, rows)], stage_ref.at[slot],
                                     sem.at[slot])

    fetch(0, 0).start()
    for s in range(n_slab):
        if s + 1 < n_slab:
            fetch(s + 1, (s + 1) % 2).start()
        fetch(s, s % 2).wait()
        wbf_ref[s * rows:(s + 1) * rows, :] = stage_ref[s % 2].astype(BF16)


def _out_proj_kernel(att_ref, rec_ref, x_ref, wo_hbm, gt_ref, sc_ref, sh_ref, g_ref, b_ref,
                     rw_ref, rb_ref, x1_ref, h2_ref, lg_ref, wobf_ref, rwhi_ref, rwlo_ref,
                     stage_ref, mix_ref, hi_ref, lo_ref, sem, *, alpha, att_w):
    @pl.when(pl.program_id(0) == 0)
    def _():
        _load_weight_bf16(wo_hbm, wobf_ref, stage_ref, sem)
        hi, lo = _bf16_split(rw_ref[...])
        rwhi_ref[...] = hi
        rwlo_ref[...] = lo

    att = jnp.concatenate([att_ref[0, h] for h in range(att_ref.shape[1])], axis=1)
    mix_ref[...] = (jnp.dot(att, wobf_ref[:att_w, :], preferred_element_type=F32)
                    + jnp.dot(rec_ref[...], wobf_ref[att_w:, :], preferred_element_type=F32))
    half = x_ref.shape[1] // 2

    def row_group(r, carry):
        rows = pl.ds(pl.multiple_of(r * LN_ROW_GROUP, LN_ROW_GROUP), LN_ROW_GROUP)
        y = alpha * x_ref[rows, :] + (1.0 + gt_ref[0]) * mix_ref[rows, :]
        x1 = _layer_norm(y, g_ref[...], b_ref[...])
        x1_ref[rows, :] = x1
        h2 = x1 * (1.0 + sc_ref[0]) + sh_ref[0]
        hi, lo = _bf16_split(h2)
        hi_ref[rows, :] = hi
        lo_ref[rows, :] = lo
        h2_ref[rows, :] = _pack_bf16_pair(h2[:, :half], h2[:, half:])
        return carry

    lax.fori_loop(0, x_ref.shape[0] // LN_ROW_GROUP, row_group, 0, unroll=4)

    lg = (jnp.dot(hi_ref[...], rwhi_ref[...], preferred_element_type=F32)
          + jnp.dot(lo_ref[...], rwhi_ref[...], preferred_element_type=F32)
          + jnp.dot(hi_ref[...], rwlo_ref[...], preferred_element_type=F32))
    lg_ref[...] = lg.T[:lg_ref.shape[0], :] + rb_ref[...]


OUT_PROJ_STAGE_ROWS = 256
LN_ROW_GROUP = 16


def _out_proj(att, rec, x2d, wo, gt, sc, sh, g, b, rw, rb, seq, alpha):
    t, d = x2d.shape
    hp = att.shape[1]
    att_w = hp * V7X_LANES
    n_e = rw.shape[1]
    assert n_e <= V7X_LANES and d % OUT_PROJ_STAGE_ROWS == 0
    rw_pad = jnp.zeros((d, V7X_LANES), F32).at[:, :n_e].set(rw)
    tm = _pick(seq, (512, 256, 128))
    per_b = seq // tm
    vec3 = pl.BlockSpec((1, 1, d), lambda i: (i // per_b, 0, 0))
    full = lambda shape: pl.BlockSpec(shape, lambda i: (0,) * len(shape))
    return pl.pallas_call(
        functools.partial(_out_proj_kernel, alpha=alpha, att_w=att_w),
        grid=(t // tm,),
        in_specs=[pl.BlockSpec((1, hp, tm, V7X_LANES), lambda i: (i // per_b, 0, i % per_b, 0)),
                  pl.BlockSpec((tm, d - att_w), lambda i: (i, 0)),
                  pl.BlockSpec((tm, d), lambda i: (i, 0)),
                  pl.BlockSpec(memory_space=pl.ANY), vec3, vec3, vec3, full((1, d)), full((1, d)),
                  full((d, V7X_LANES)), full((n_e, 1))],
        out_specs=[pl.BlockSpec((tm, d), lambda i: (i, 0)),
                   pl.BlockSpec((tm, d // 2), lambda i: (i, 0)),
                   pl.BlockSpec((n_e, tm), lambda i: (0, i))],
        out_shape=[jax.ShapeDtypeStruct((t, d), F32),
                   jax.ShapeDtypeStruct((t, d // 2), jnp.uint32),
                   jax.ShapeDtypeStruct((n_e, t), F32)],
        scratch_shapes=[pltpu.VMEM((d, d), BF16), pltpu.VMEM((d, V7X_LANES), BF16),
                        pltpu.VMEM((d, V7X_LANES), BF16),
                        pltpu.VMEM((2, OUT_PROJ_STAGE_ROWS, d), F32),
                        pltpu.VMEM((tm, d), F32), pltpu.VMEM((tm, d), BF16),
                        pltpu.VMEM((tm, d), BF16),
                        pltpu.SemaphoreType.DMA((2,))],
        compiler_params=_cparams(("arbitrary",)),
        name="out_proj_ln1",
    )(att, rec, x2d, wo, gt[:, None, :], sc[:, None, :], sh[:, None, :],
      g.reshape(1, d), b.reshape(1, d), rw_pad, rb.reshape(n_e, 1))


def _routing_kernel(lg_ref, tri_ref, low_ref, gate_ref, dest_ref, tile_ref, pad_ref, sel_scr, rk_scr,
                    *, blk, tm):
    n_e, t = lg_ref.shape
    eidx = lax.broadcasted_iota(jnp.int32, (n_e, t), 0)
    cur = lg_ref[...]
    vals, idxs = [], []
    for _ in range(TOP_K):
        m = jnp.max(cur, axis=0, keepdims=True)
        ik = jnp.min(jnp.where(cur == m, eidx, n_e), axis=0, keepdims=True)
        cur = jnp.where(eidx == ik, -jnp.inf, cur)
        vals.append(m)
        idxs.append(ik)
    es = [jnp.exp(v - vals[0]) for v in vals]
    den = functools.reduce(lambda a, b: a + b, es)
    for k in range(TOP_K):
        gate_ref[k:k + 1, :] = es[k] / den
    sel = functools.reduce(lambda a, b: a | b, [eidx == ik for ik in idxs])
    sel_scr[...] = jnp.where(sel, 1.0, 0.0)

    tri = tri_ref[...]
    carry = jnp.zeros((n_e, 1), F32)
    for j in range(t // blk):
        sb = sel_scr[:, j * blk:(j + 1) * blk]
        pre = jnp.dot(sb.astype(BF16), tri, preferred_element_type=F32)
        rk_scr[:, j * blk:(j + 1) * blk] = pre + carry
        carry = carry + jnp.sum(sb, axis=1, keepdims=True)

    counts = jnp.broadcast_to(carry, (n_e, V7X_LANES))
    padded = jnp.floor((counts + (tm - 1)) * (1.0 / tm)) * tm
    pends = jnp.dot(low_ref[...], padded, precision=lax.Precision.HIGHEST,
                    preferred_element_type=F32)
    pstarts = pends - padded
    row0 = rk_scr[...] + pstarts[:, 0:1]
    for k in range(TOP_K):
        dest_ref[k:k + 1, :] = jnp.sum(jnp.where(eidx == idxs[k], row0, 0.0), axis=0,
                                       keepdims=True).astype(jnp.int32)
    starts = (lax.broadcasted_iota(jnp.int32, (n_e, tile_ref.shape[1]), 1) * tm).astype(F32)
    owner = jnp.sum(jnp.where(pends[:, 0:1] <= starts, 1.0, 0.0), axis=0, keepdims=True)
    tile_ref[0:1, :] = jnp.minimum(owner, n_e - 1.0).astype(jnp.int32)
    tile_ref[1:2, :] = jnp.broadcast_to(jnp.max(pends[:, 0:1], axis=0, keepdims=True),
                                        (1, tile_ref.shape[1])).astype(jnp.int32)
    pad_ref[0] = (pstarts + counts).astype(jnp.int32)
    pad_ref[1] = (padded - counts).astype(jnp.int32)


def _routing(logits_t, tm, n_tiles):
    n_e, t = logits_t.shape
    assert tm & (tm - 1) == 0
    blk = _pick(t, (256, 128))
    tri = jnp.asarray(np.triu(np.ones((blk, blk), np.float32), 1), BF16)
    low = jnp.asarray(np.tril(np.ones((n_e, n_e), np.float32)))
    ntp = -(-n_tiles // V7X_LANES) * V7X_LANES
    full = lambda shape: pl.BlockSpec(shape, lambda: (0,) * len(shape))
    return pl.pallas_call(
        functools.partial(_routing_kernel, blk=blk, tm=tm),
        in_specs=[full((n_e, t)), full((blk, blk)), full((n_e, n_e))],
        out_specs=[full((TOP_K, t)), full((TOP_K, t)), full((2, ntp)), full((2, n_e, V7X_LANES))],
        out_shape=[jax.ShapeDtypeStruct((TOP_K, t), F32),
                   jax.ShapeDtypeStruct((TOP_K, t), jnp.int32),
                   jax.ShapeDtypeStruct((2, ntp), jnp.int32),
                   jax.ShapeDtypeStruct((2, n_e, V7X_LANES), jnp.int32)],
        scratch_shapes=[pltpu.VMEM((n_e, t), F32), pltpu.VMEM((n_e, t), F32)],
        compiler_params=pltpu.CompilerParams(vmem_limit_bytes=V7X_VMEM_LIMIT),
        name="routing",
    )(logits_t, tri, low)


def _dispatch_kernel(dest_ref, pad_off_ref, pad_n_ref, nrow_ref, h_ref, o_ref, zero_ref, sem, zsem,
                     *, n_tok, tq, tm, n_e):
    step = pl.program_id(0)
    base = step * tq

    def fill(off, n):
        return pltpu.make_async_copy(zero_ref.at[pl.ds(0, n)], o_ref.at[pl.ds(off, n)], zsem)

    sub = 8
    pieces = [1 << s for s in range(tm.bit_length() - 2, sub.bit_length() - 2, -1)]

    def pad_rows(wait):
        def per_expert(e, c):
            off = pad_off_ref[e]
            n = pad_n_ref[e]
            head = jnp.minimum(n, (-off) & (sub - 1))
            for s in range(sub - 1):
                @pl.when(s < head)
                def _(s=s):
                    cp = fill(off + s, 1)
                    cp.wait() if wait else cp.start()
            off = off + head
            n = n - head
            for p in pieces:
                hit = (n & p) != 0

                @pl.when(hit)
                def _(off=off, p=p):
                    cp = fill(pl.multiple_of(off, sub), p)
                    cp.wait() if wait else cp.start()

                off = off + jnp.where(hit, p, 0)
            return c
        lax.fori_loop(0, n_e, per_expert, 0)

        def per_tile(i, c):
            cp = fill(pl.multiple_of(i * tm, tm), tm)
            cp.wait() if wait else cp.start()
            return c
        lax.fori_loop(nrow_ref[0] // tm, o_ref.shape[0] // tm, per_tile, 0)

    @pl.when(step == 0)
    def _():
        zero_ref[...] = jnp.zeros_like(zero_ref)
        pad_rows(False)

    def copy(r, k):
        dst = o_ref.at[pl.ds(dest_ref[k * n_tok + base + r], 1)]
        return pltpu.make_async_copy(h_ref.at[pl.ds(r, 1)], dst, sem)

    def start(r, c):
        for k in range(TOP_K):
            copy(r, k).start(priority=k % 2)
        return c

    lax.fori_loop(0, tq, start, 0, unroll=2)
    for _ in range(TOP_K):
        pltpu.make_async_copy(h_ref, h_ref, sem).wait()

    @pl.when(step == 0)
    def _():
        pad_rows(True)


def _dispatch(h2p, dest_flat, pad_off, pad_n, n_used_rows, n_rows, tm):
    t, width = h2p.shape
    n_e = pad_off.shape[0]
    tq = _pick(t, (256, 128))
    return pl.pallas_call(
        functools.partial(_dispatch_kernel, n_tok=t, tq=tq, tm=tm, n_e=n_e),
        grid_spec=pltpu.PrefetchScalarGridSpec(
            num_scalar_prefetch=4, grid=(t // tq,),
            in_specs=[pl.BlockSpec((tq, width), lambda i, *_: (i, 0))],
            out_specs=pl.BlockSpec(memory_space=pl.ANY),
            scratch_shapes=[pltpu.VMEM((tm, width), h2p.dtype),
                            pltpu.SemaphoreType.DMA(()), pltpu.SemaphoreType.DMA(())]),
        out_shape=jax.ShapeDtypeStruct((n_rows, width), h2p.dtype),
        compiler_params=pltpu.CompilerParams(dimension_semantics=("arbitrary",),
                                             has_side_effects=True,
                                             vmem_limit_bytes=V7X_VMEM_LIMIT),
        name="moe_dispatch",
    )(dest_flat, pad_off, pad_n, n_used_rows, h2p)


EXPERT_COL_BLOCK = 2048
CAST_ROWS = 256


def _stream_expert_weights(te_ref, nt_ref, w_hbm, stage_ref, wbf_ref, sem):
    j, i = pl.program_id(0), pl.program_id(1)
    n_pass, n_tiles = pl.num_programs(0), pl.num_programs(1)
    tn = stage_ref.shape[1]
    n_used = nt_ref[0]
    cur = te_ref[i]

    def fetch(e, jj):
        src = w_hbm.at[e, :, pl.ds(pl.multiple_of(jj * tn, tn), tn)]
        return pltpu.make_async_copy(src, stage_ref, sem)

    @pl.when((i == 0) & (j == 0))
    def _():
        fetch(te_ref[0], 0).start()

    @pl.when((i < n_used) & ((i == 0) | (cur != te_ref[jnp.maximum(i - 1, 0)])))
    def _():
        fetch(cur, j).wait()

        def cast(r, c):
            rows = pl.ds(pl.multiple_of(r * CAST_ROWS, CAST_ROWS), CAST_ROWS)
            wbf_ref[rows, :] = stage_ref[rows, :].astype(BF16)
            return c
        lax.fori_loop(0, stage_ref.shape[0] // CAST_ROWS, cast, 0)

        def same_group(k):
            return (k < n_used) & (te_ref[jnp.minimum(k, n_tiles - 1)] == cur)
        nxt = lax.while_loop(same_group, lambda k: k + 1, i + 1)

        @pl.when(nxt < n_used)
        def _():
            fetch(te_ref[jnp.minimum(nxt, n_tiles - 1)], j).start()

        @pl.when((nxt >= n_used) & (j + 1 < n_pass))
        def _():
            fetch(te_ref[0], j + 1).start()


def _gemm1_kernel(te_ref, nt_ref, x_ref, w_hbm, b_ref, cmp_ref, o_ref, stage_ref, wbf_ref, sem, *, nc):
    _stream_expert_weights(te_ref, nt_ref, w_hbm, stage_ref, wbf_ref, sem)
    i = pl.program_id(1)

    @pl.when(i < nt_ref[0])
    def _():
        x_lo, x_hi = (v.astype(BF16) for v in _unpack_bf16_pair(x_ref[...]))
        half = x_lo.shape[1]
        cw = cmp_ref.shape[0]
        lane = lax.broadcasted_iota(jnp.int32, (x_lo.shape[0], nc), 1)
        for n0 in range(0, wbf_ref.shape[1], nc):
            hb = (jnp.dot(x_lo, wbf_ref[:half, n0:n0 + nc], preferred_element_type=F32)
                  + jnp.dot(x_hi, wbf_ref[half:, n0:n0 + nc], preferred_element_type=F32)
                  + b_ref[0, :, n0:n0 + nc])
            nxt = pltpu.roll(hb, nc - 1, axis=1)
            glu = jnp.minimum(hb, SWIGLU_LIMIT)
            lin = jnp.clip(nxt, -SWIGLU_LIMIT, SWIGLU_LIMIT)
            act = glu * jax.nn.sigmoid(SWIGLU_ALPHA * glu) * (lin + 1.0)
            act = jnp.where(lane % 2 == 0, act, 0.0).astype(BF16)
            for c in range(nc // cw):
                o0 = (n0 + c * cw) // 2
                o_ref[:, o0:o0 + cw // 2] = jnp.dot(
                    act[:, c * cw:(c + 1) * cw], cmp_ref[...],
                    preferred_element_type=F32).astype(o_ref.dtype)

    @pl.when(i >= nt_ref[0])
    def _():
        o_ref[...] = jnp.zeros_like(o_ref)


def _gemm1(xin, w1, b1, tile_e, n_tiles_used, tm):
    n_rows, half_d = xin.shape
    n_e, d, f2 = w1.shape
    tn = min(EXPERT_COL_BLOCK, f2)
    nc = min(512, tn)
    assert f2 % tn == 0 and d % CAST_ROWS == 0
    n_tiles = n_rows // tm
    cw = V7X_MXU_DIM
    cmp_np = np.zeros((cw, cw // 2), np.float32)
    cmp_np[np.arange(0, cw, 2), np.arange(cw // 2)] = 1.0

    def used(i, nt):
        return jnp.minimum(i, nt[0] - 1)

    return pl.pallas_call(
        functools.partial(_gemm1_kernel, nc=nc),
        grid_spec=pltpu.PrefetchScalarGridSpec(
            num_scalar_prefetch=2, grid=(f2 // tn, n_tiles),
            in_specs=[pl.BlockSpec((tm, half_d), lambda j, i, te, nt: (used(i, nt), 0)),
                      pl.BlockSpec(memory_space=pl.ANY),
                      pl.BlockSpec((1, 1, tn), lambda j, i, te, nt: (te[used(i, nt)], 0, j)),
                      pl.BlockSpec((cw, cw // 2), lambda j, i, te, nt: (0, 0))],
            out_specs=pl.BlockSpec((tm, tn // 2), lambda j, i, te, nt: (i, j)),
            scratch_shapes=[pltpu.VMEM((d, tn), F32), pltpu.VMEM((d, tn), BF16),
                            pltpu.SemaphoreType.DMA(())]),
        out_shape=jax.ShapeDtypeStruct((n_rows, f2 // 2), BF16),
        compiler_params=_cparams(("arbitrary", "arbitrary")),
        name="moe_gemm1",
    )(tile_e, n_tiles_used, xin, w1, b1.reshape(n_e, 1, f2), jnp.asarray(cmp_np, BF16))


def _gemm2_kernel(te_ref, nt_ref, a_ref, w_hbm, b_ref, o_ref, stage_ref, wbf_ref, sem, *, nc):
    _stream_expert_weights(te_ref, nt_ref, w_hbm, stage_ref, wbf_ref, sem)
    i = pl.program_id(1)

    @pl.when(i < nt_ref[0])
    def _():
        a = a_ref[...]
        half = wbf_ref.shape[1] // 2

        def cols(n0):
            return (jnp.dot(a, wbf_ref[:, n0:n0 + nc], preferred_element_type=F32)
                    + b_ref[0, :, n0:n0 + nc])

        for n0 in range(0, half, nc):
            o_ref[:, n0:n0 + nc] = _pack_bf16_pair(cols(n0), cols(half + n0))

    @pl.when(i >= nt_ref[0])
    def _():
        o_ref[...] = jnp.zeros_like(o_ref)


def _gemm2(act, w2, b2, tile_e, n_tiles_used, tm):
    n_rows, f = act.shape
    n_e, _, d = w2.shape
    tn = d
    nc = min(512, tn // 2)
    assert f % CAST_ROWS == 0
    n_tiles = n_rows // tm

    def used(i, nt):
        return jnp.minimum(i, nt[0] - 1)

    return pl.pallas_call(
        functools.partial(_gemm2_kernel, nc=nc),
        grid_spec=pltpu.PrefetchScalarGridSpec(
            num_scalar_prefetch=2, grid=(d // tn, n_tiles),
            in_specs=[pl.BlockSpec((tm, f), lambda j, i, te, nt: (used(i, nt), 0)),
                      pl.BlockSpec(memory_space=pl.ANY),
                      pl.BlockSpec((1, 1, tn), lambda j, i, te, nt: (te[used(i, nt)], 0, j))],
            out_specs=pl.BlockSpec((tm, tn // 2), lambda j, i, te, nt: (i, j)),
            scratch_shapes=[pltpu.VMEM((f, tn), F32), pltpu.VMEM((f, tn), BF16),
                            pltpu.SemaphoreType.DMA(())]),
        out_shape=jax.ShapeDtypeStruct((n_rows, d // 2), jnp.uint32),
        compiler_params=_cparams(("arbitrary", "arbitrary")),
        name="moe_gemm2",
    )(tile_e, n_tiles_used, act, w2, b2.reshape(n_e, 1, d))


def _combine_kernel(dest_ref, x1_ref, gate_ref, gt_ref, g_ref, b_ref, y_ref, o_ref, buf, sem,
                    *, alpha, tc, n_tok):
    i = pl.program_id(0)
    n_steps = pl.num_programs(0)

    def copy(step, slot, k, r):
        src = y_ref.at[pl.ds(dest_ref[k * n_tok + step * tc + r], 1)]
        return pltpu.make_async_copy(src, buf.at[slot, k, pl.ds(r, 1)], sem.at[slot])

    def issue(step, slot):
        def body(r, c):
            for k in range(TOP_K):
                copy(step, slot, k, r).start(priority=k % 2)
            return c
        lax.fori_loop(0, tc, body, 0, unroll=2)

    def drain(slot):
        pltpu.make_async_copy(buf.at[slot], buf.at[slot], sem.at[slot]).wait()

    slot = i % 2

    @pl.when(i == 0)
    def _():
        issue(0, 0)

    @pl.when(i + 1 < n_steps)
    def _():
        issue(i + 1, 1 - slot)

    drain(slot)
    gates = gate_ref[...]
    lo, hi = _unpack_bf16_pair(buf[slot, 0])
    ff_lo, ff_hi = gates[:, 0:1] * lo, gates[:, 0:1] * hi
    for k in range(1, TOP_K):
        lo, hi = _unpack_bf16_pair(buf[slot, k])
        ff_lo, ff_hi = ff_lo + gates[:, k:k + 1] * lo, ff_hi + gates[:, k:k + 1] * hi
    ff = jnp.concatenate([ff_lo, ff_hi], axis=1)
    y = alpha * x1_ref[...] + (1.0 + gt_ref[0]) * ff
    o_ref[...] = _layer_norm(y, g_ref[...], b_ref[...])


def _combine(dest_flat, x1, gates_tk, gt, g, b, y, seq, alpha):
    t, d = x1.shape
    tc = _pick(seq, (256, 128))
    per_b = seq // tc
    return pl.pallas_call(
        functools.partial(_combine_kernel, alpha=alpha, tc=tc, n_tok=t),
        grid_spec=pltpu.PrefetchScalarGridSpec(
            num_scalar_prefetch=1, grid=(t // tc,),
            in_specs=[pl.BlockSpec((tc, d), lambda i, ds: (i, 0)),
                      pl.BlockSpec((tc, TOP_K), lambda i, ds: (i, 0)),
                      pl.BlockSpec((1, 1, d), lambda i, ds: (i // per_b, 0, 0)),
                      pl.BlockSpec((1, d), lambda i, ds: (0, 0)),
                      pl.BlockSpec((1, d), lambda i, ds: (0, 0)),
                      pl.BlockSpec(memory_space=pl.ANY)],
            out_specs=pl.BlockSpec((tc, d), lambda i, ds: (i, 0)),
            scratch_shapes=[pltpu.VMEM((2, TOP_K, tc, d // 2), jnp.uint32),
                            pltpu.SemaphoreType.DMA((2,))]),
        out_shape=jax.ShapeDtypeStruct((t, d), F32),
        compiler_params=_cparams(("arbitrary",)),
        name="moe_combine_ln2",
    )(dest_flat, x1, gates_tk, gt[:, None, :], g.reshape(1, d), b.reshape(1, d), y)


MOE_ROW_TILE = 512


def _moe_ffn(h2p, logits_t, x1, gt_f, ln_g, ln_b, w1, b1, w2, b2, seq, alpha):
    tm = MOE_ROW_TILE
    n_e, t = logits_t.shape
    n_rows = -(-(TOP_K * t) // tm) * tm + n_e * tm
    n_tiles = n_rows // tm
    gate_t, dest, tiles, pads = _routing(logits_t, tm, n_tiles)
    dest = dest.reshape(-1)
    tile_e = tiles[0, :n_tiles]
    n_used_rows = tiles[1, :1]
    n_used_tiles = n_used_rows // tm
    xin = _dispatch(h2p, dest, pads[0, :, 0], pads[1, :, 0], n_used_rows, n_rows, tm)
    act = _gemm1(xin, w1, b1, tile_e, n_used_tiles, tm)
    y = _gemm2(act, w2, b2, tile_e, n_used_tiles, tm)
    return _combine(dest, x1, gate_t.T, gt_f, ln_g, ln_b, y, seq, alpha)


def kernel(x, c, positions, w_ada, b_ada, w_in, hgrn_lb, gnorm_w, w_o, ln1_g, ln1_b,
           router_w, router_b, w1, b1, w2, b2, ln2_g, ln2_b):
    bsz, seq, d = x.shape
    depth = w_ada.shape[0]
    t = bsz * seq
    att_w = d // 2
    hg_w = d - att_w
    alpha = (2.0 * depth) ** 0.25

    lb_all = jnp.cumsum(jax.nn.softmax(hgrn_lb.astype(F32), axis=0), axis=0)
    inv = ROPE_THETA ** (-(jnp.arange(0, ROT_DIM, 2, dtype=F32) / ROT_DIM))
    lane = np.arange(V7X_LANES)
    inv_lane = inv[(lane % ATT_HEAD_DIM) % (ROT_DIM // 2)].reshape(1, V7X_LANES)
    pos_b = jnp.broadcast_to(positions.astype(F32).reshape(t, 1), (t, V7X_LANES))

    x2d = x.reshape(t, d)
    for l in range(depth):
        mod = _adaln(c, w_ada[l], b_ada[l])
        sh_a, sc_a, gt_a, sh_f, sc_f, gt_f = jnp.split(mod, 6, axis=-1)

        proj = _in_proj(x2d, sc_a, sh_a, w_in[l], seq)
        proj3 = proj.reshape(bsz, seq, proj.shape[1])
        q_hp, k_hp, v_hp = _qk_rope(proj, pos_b, inv_lane, bsz, seq, att_w)
        att = _attention(q_hp, k_hp, v_hp)
        rec = _hgrn2(proj3, lb_all[l], gnorm_w[l], att_w, hg_w).reshape(t, hg_w)

        x1, h2p, logits_t = _out_proj(att, rec, x2d, w_o[l], gt_a, sc_f, sh_f, ln1_g[l], ln1_b[l],
                                      router_w[l], router_b[l], seq, alpha)

        x2d = _moe_ffn(h2p, logits_t, x1, gt_f, ln2_g[l], ln2_b[l], w1[l], b1[l], w2[l], b2[l],
                       seq, alpha)
    return x2d.reshape(bsz, seq, d)
```

```python
import functools

import numpy as np
import jax
import jax.numpy as jnp
from jax import lax
from jax.experimental import pallas as pl
from jax.experimental.pallas import tpu as pltpu

F32 = jnp.float32
BF16 = jnp.bfloat16

V7X_LANES = 128
V7X_MXU_DIM = 256
V7X_VMEM_LIMIT = 56 * 1024 * 1024

ATT_HEAD_DIM = 64
DILATED_PAIRS = ((128, 1), (512, 4), (2048, 16))
ATT_BLOCK = 128
ROT_DIM = ATT_HEAD_DIM // 4
ROPE_THETA = 500000.0
HG_EXPAND = 128
HG_CHUNK = 64
HG_SUB = 16
TOP_K = 4
SWIGLU_ALPHA = 1.702
SWIGLU_LIMIT = 7.0
LN_EPS = 1e-5
RMS_EPS = 1e-6
NEG_INF = -1e30

NT_DIMS = (((1,), (1,)), ((), ()))


def _pick(n, candidates):
    for c in candidates:
        if n % c == 0:
            return c
    raise ValueError(f"no tile in {candidates} divides {n}")


def _cparams(sem, vmem=V7X_VMEM_LIMIT, flags=None):
    return pltpu.CompilerParams(dimension_semantics=sem, vmem_limit_bytes=vmem, flags=flags)


def _adaln_kernel(c_ref, w_ref, b_ref, o_ref):
    c = c_ref[...]
    s = c * jax.nn.sigmoid(c)
    o_ref[...] = jnp.dot(s.astype(BF16), w_ref[...].astype(BF16),
                         preferred_element_type=F32) + b_ref[...]


def _adaln(c, w, b):
    bsz, d = c.shape
    n = w.shape[1]
    rows = 8
    cp = jnp.zeros((rows, d), F32).at[:bsz].set(c)
    tn = _pick(n, (1024, 512, 256, 128))
    out = pl.pallas_call(
        _adaln_kernel,
        grid=(n // tn,),
        in_specs=[pl.BlockSpec((rows, d), lambda j: (0, 0)),
                  pl.BlockSpec((d, tn), lambda j: (0, j)),
                  pl.BlockSpec((1, tn), lambda j: (0, j))],
        out_specs=pl.BlockSpec((rows, tn), lambda j: (0, j)),
        out_shape=jax.ShapeDtypeStruct((rows, n), F32),
        compiler_params=_cparams(("arbitrary",)),
        name="adaln",
    )(cp, w, b.reshape(1, n))
    return out[:bsz]


def _in_proj_kernel(x_ref, sc_ref, sh_ref, w_ref, o_ref, wbf_ref):
    @pl.when(pl.program_id(1) == 0)
    def _():
        wbf_ref[...] = w_ref[...].astype(BF16)

    h = x_ref[...] * (1.0 + sc_ref[0]) + sh_ref[0]
    o_ref[...] = jnp.dot(h.astype(BF16), wbf_ref[...], preferred_element_type=F32)


def _in_proj(x2d, sc, sh, w, seq):
    t, d = x2d.shape
    nc = w.shape[1]
    tm = _pick(seq, (512, 256, 128))
    tn = _pick(nc, (1024, 512, 256, 128))
    per_b = seq // tm
    vec = pl.BlockSpec((1, 1, d), lambda j, i: (i // per_b, 0, 0))
    return pl.pallas_call(
        _in_proj_kernel,
        grid=(nc // tn, t // tm),
        in_specs=[pl.BlockSpec((tm, d), lambda j, i: (i, 0)), vec, vec,
                  pl.BlockSpec((d, tn), lambda j, i: (0, j))],
        out_specs=pl.BlockSpec((tm, tn), lambda j, i: (i, j)),
        out_shape=jax.ShapeDtypeStruct((t, nc), F32),
        scratch_shapes=[pltpu.VMEM((d, tn), BF16)],
        compiler_params=_cparams(("arbitrary", "arbitrary")),
        name="in_proj",
    )(x2d, sc[:, None, :], sh[:, None, :], w)


def _rope_kernel(q_ref, k_ref, v_ref, pos_ref, inv_ref, qo_ref, ko_ref, vo_ref):
    tm = pos_ref.shape[0]
    lane = lax.broadcasted_iota(jnp.int32, (tm, V7X_LANES), 1)
    lh = lane % ATT_HEAD_DIM
    half = ROT_DIM // 2
    ang = pos_ref[...] * inv_ref[...]
    cs = jnp.where(lh < ROT_DIM, jnp.cos(ang), 1.0)
    sn = jnp.sin(ang)
    sn = jnp.where(lh < half, -sn, jnp.where(lh < ROT_DIM, sn, 0.0))

    def rope(t):
        swapped = jnp.where(lh < half,
                            pltpu.roll(t, V7X_LANES - half, axis=1),
                            pltpu.roll(t, half, axis=1))
        return t * cs + swapped * sn

    for h in range(qo_ref.shape[1]):
        lanes = slice(h * V7X_LANES, (h + 1) * V7X_LANES)
        qo_ref[0, h] = rope(q_ref[:, lanes].astype(F32)) * (ATT_HEAD_DIM ** -0.5)
        ko_ref[0, h] = rope(k_ref[:, lanes].astype(F32))
        vo_ref[0, h] = v_ref[:, lanes].astype(F32)


def _qk_rope(proj, pos_b, inv_lane, bsz, seq, att_w):
    t = proj.shape[0]
    hp = att_w // V7X_LANES
    tm = _pick(seq, (256, 128))
    per_b = seq // tm
    out_spec = pl.BlockSpec((1, hp, tm, V7X_LANES), lambda i: (i // per_b, 0, i % per_b, 0))
    shp = jax.ShapeDtypeStruct((bsz, hp, seq, V7X_LANES), F32)
    return pl.pallas_call(
        _rope_kernel,
        grid=(t // tm,),
        in_specs=[pl.BlockSpec((tm, att_w), lambda i: (i, 0)),
                  pl.BlockSpec((tm, att_w), lambda i: (i, 1)),
                  pl.BlockSpec((tm, att_w), lambda i: (i, 2)),
                  pl.BlockSpec((tm, V7X_LANES), lambda i: (i, 0)),
                  pl.BlockSpec((1, V7X_LANES), lambda i: (0, 0))],
        out_specs=[out_spec, out_spec, out_spec],
        out_shape=[shp, shp, shp],
        compiler_params=_cparams(("arbitrary",)),
        name="qk_rope",
    )(proj, proj, proj, pos_b, inv_lane)


ATT_TASKS_PER_STEP = 16


def _attn_kernel(q_in, k_in, v_in, out_ref, qs, ks, vs, no, nl, *, seq):
    qn = ATT_BLOCK
    n_task = seq // qn
    n_head = V7X_LANES // ATT_HEAD_DIM
    gc = ATT_TASKS_PER_STEP
    lane = lax.broadcasted_iota(jnp.int32, (qn, V7X_LANES), 1)
    qi = lax.broadcasted_iota(jnp.int32, (gc, qn, 2 * qn), 1)
    kj = lax.broadcasted_iota(jnp.int32, (gc, qn, 2 * qn), 2)
    dist = qi + qn - kj
    band = (dist >= 0) & (dist <= qn)
    zeros = jnp.zeros((qn, V7X_LANES), BF16)

    for bi, (window, d) in enumerate(DILATED_PAIRS):
        nb = seq // (d * qn)

        def task_rows(r, n, d=d):
            start = r + n * qn * d
            return pl.ds(start, qn) if d == 1 else pl.ds(start, qn, stride=d)

        for r in range(d):
            for n in range(nb):
                g = r * nb + n
                sl = task_rows(r, n)
                qv = q_in[0, 0, sl, :]
                for h in range(n_head):
                    qs[h, g] = jnp.where(lane // ATT_HEAD_DIM == h, qv, 0.0).astype(BF16)
                for src, dst in ((k_in, ks), (v_in, vs)):
                    blk = src[0, 0, sl, :].astype(BF16)
                    dst[g, qn:, :] = blk
                    if n + 1 < nb:
                        dst[g + 1, :qn, :] = blk
                    if n == 0:
                        dst[g, :qn, :] = zeros

        def step(c, carry, bi=bi, d=d, nb=nb):
            g0 = pl.multiple_of(c * gc, gc)
            gsl = pl.ds(g0, gc)
            gidx = g0 + lax.broadcasted_iota(jnp.int32, (gc, qn, 2 * qn), 0)
            valid = band & ((gidx % nb != 0) | (kj >= qn))
            k = ks[gsl]
            v = vs[gsl]
            o = None
            for h in range(n_head):
                s = jnp.einsum("gqd,gkd->gqk", qs[h, gsl], k, preferred_element_type=F32)
                s = jnp.where(valid, s, NEG_INF)
                m = jnp.max(s, axis=-1, keepdims=True)
                p = jnp.exp(s - m)
                den = jnp.sum(p, axis=-1, keepdims=True)
                oh = jnp.einsum("gqk,gkd->gqd", p.astype(BF16), v,
                                preferred_element_type=F32) / den
                lh = jnp.broadcast_to(m + jnp.log(den), oh.shape)
                if o is None:
                    o, lse = oh, lh
                else:
                    sel = lane[None] // ATT_HEAD_DIM == h
                    o, lse = jnp.where(sel, oh, o), jnp.where(sel, lh, lse)
            for t in range(gc):
                g = g0 + t
                start = g // nb + (g % nb) * (qn * d)
                sl = pl.ds(start, qn) if d == 1 else pl.ds(start, qn, stride=d)
                no[bi, sl, :] = o[t]
                nl[bi, sl, :] = lse[t]
            return carry

        lax.fori_loop(0, n_task // gc, step, 0)

    def combine(c, carry):
        sl = pl.ds(pl.multiple_of(c * qn, qn), qn)
        ls = [nl[bi, sl, :] for bi in range(len(DILATED_PAIRS))]
        mx = functools.reduce(jnp.maximum, ls)
        ws = [jnp.exp(l - mx) for l in ls]
        tot = functools.reduce(lambda a, b: a + b, ws)
        acc = functools.reduce(lambda a, b: a + b,
                               [w * no[bi, sl, :] for bi, w in enumerate(ws)])
        out_ref[0, 0, sl, :] = (acc / tot).astype(out_ref.dtype)
        return carry

    lax.fori_loop(0, n_task, combine, 0)


def _attention(q_hp, k_hp, v_hp):
    bsz, hp, seq, _ = q_hp.shape
    for window, d in DILATED_PAIRS:
        assert window // d == ATT_BLOCK and seq % (d * ATT_BLOCK) == 0
    n_task = seq // ATT_BLOCK
    assert n_task % ATT_TASKS_PER_STEP == 0
    n_head = V7X_LANES // ATT_HEAD_DIM
    nbr = len(DILATED_PAIRS)
    spec = pl.BlockSpec((1, 1, seq, V7X_LANES), lambda b, h: (b, h, 0, 0))
    return pl.pallas_call(
        functools.partial(_attn_kernel, seq=seq),
        grid=(bsz, hp),
        in_specs=[spec, spec, spec],
        out_specs=spec,
        out_shape=jax.ShapeDtypeStruct((bsz, hp, seq, V7X_LANES), BF16),
        scratch_shapes=[pltpu.VMEM((n_head, n_task, ATT_BLOCK, V7X_LANES), BF16),
                        pltpu.VMEM((n_task, 2 * ATT_BLOCK, V7X_LANES), BF16),
                        pltpu.VMEM((n_task, 2 * ATT_BLOCK, V7X_LANES), BF16),
                        pltpu.VMEM((nbr, seq, V7X_LANES), F32),
                        pltpu.VMEM((nbr, seq, V7X_LANES), F32)],
        compiler_params=_cparams(("arbitrary", "arbitrary")),
        name="dilated_attn",
    )(q_hp, k_hp, v_hp)


def _hgrn_kernel(q_ref, f_ref, i_ref, g_ref, lb_ref, gw_ref, sums_ref, o_ref,
                 st_ref, hl_ref, kk_ref, cum_ref, *, heads, ts):
    @pl.when(pl.program_id(2) == 0)
    def _():
        st_ref[...] = jnp.zeros_like(st_ref)

    c_len = HG_CHUNK
    lanes = heads * HG_EXPAND
    lb = lb_ref[...]
    for c in range(ts // c_len):
        rows = slice(c * c_len, (c + 1) * c_len)
        f = lb + (1.0 - lb) * jax.nn.sigmoid(f_ref[0, rows, :])
        kk_ref[rows, :] = 1.0 - f
        hi, lo = _bf16_split(jnp.log(f))
        hl_ref[rows, :lanes] = hi
        hl_ref[rows, lanes:] = lo
    for k in range(3):
        both = jnp.dot(sums_ref[k], hl_ref[...], preferred_element_type=F32)
        cum_ref[k] = both[:, :lanes] + both[:, lanes:]

    n_sub = c_len // HG_SUB
    gi = lax.broadcasted_iota(jnp.int32, (heads * n_sub, HG_SUB, c_len), 0) % n_sub
    qi = lax.broadcasted_iota(jnp.int32, (heads * n_sub, HG_SUB, c_len), 1)
    si = lax.broadcasted_iota(jnp.int32, (heads * n_sub, HG_SUB, c_len), 2)
    causal = si <= gi * HG_SUB + qi
    key_row = lax.broadcasted_iota(jnp.int32, (c_len, HG_EXPAND), 0)

    for c in range(ts // c_len):
        rows = slice(c * c_len, (c + 1) * c_len)
        qts, kts, vbs = [], [], []
        for h in range(heads):
            lsl = slice(h * HG_EXPAND, (h + 1) * HG_EXPAND)
            b = cum_ref[0, rows, lsl]
            anchor = cum_ref[1, rows, lsl]
            kk = kk_ref[rows, lsl]
            qt = (q_ref[0, rows, lsl] * jnp.exp(b - anchor)).astype(BF16)
            qts.append(qt.reshape(n_sub, HG_SUB, HG_EXPAND))
            vb = i_ref[0, rows, lsl].astype(BF16)
            for i in range(n_sub):
                hi_r = (i + 1) * HG_SUB
                kt = kk * jnp.exp(anchor[i * HG_SUB:i * HG_SUB + 1, :] - b)
                kts.append(jnp.where(key_row < hi_r, kt, 0.0).astype(BF16))
                vbs.append(vb)
        a = jnp.einsum("gqk,gsk->gqs", jnp.concatenate(qts, axis=0), jnp.stack(kts),
                       preferred_element_type=F32)
        a = jnp.where(causal, a, 0.0).astype(BF16)
        o_intra = jnp.einsum("gqs,gsv->gqv", a, jnp.stack(vbs), preferred_element_type=F32)

        for h in range(heads):
            lsl = slice(h * HG_EXPAND, (h + 1) * HG_EXPAND)
            b = cum_ref[0, rows, lsl]
            b_last = cum_ref[2, rows, lsl]
            kk = kk_ref[rows, lsl]
            q = q_ref[0, rows, lsl]
            v = i_ref[0, rows, lsl]
            st = st_ref[h]
            o_inter = lax.dot_general((q * jnp.exp(b)).astype(BF16), st.astype(BF16), NT_DIMS,
                                      preferred_element_type=F32)
            o = o_inter + o_intra[h * n_sub:(h + 1) * n_sub].reshape(c_len, HG_EXPAND)
            kl = kk * jnp.exp(b_last - b)
            upd = jnp.dot(v.T.astype(BF16), kl.astype(BF16), preferred_element_type=F32)
            st_ref[h] = st * jnp.exp(b_last[0:1, :]) + upd
            o = o * lax.rsqrt(jnp.mean(o * o, axis=-1, keepdims=True) + RMS_EPS)
            g = g_ref[0, rows, lsl]
            o = o * gw_ref[:, lsl] * (g * jax.nn.sigmoid(g))
            o_ref[0, rows, lsl] = o.astype(o_ref.dtype)


def _hgrn_sum_matrices(ts):
    t = np.arange(ts)[:, None]
    s = np.arange(ts)[None, :]
    same = (t // HG_CHUNK) == (s // HG_CHUNK)
    mid = (t // HG_SUB) * HG_SUB + HG_SUB // 2
    return np.stack([same & (s <= t), same & (s <= mid), same]).astype(np.float32)


def _hgrn2(proj3, lb, gw, att_w, hg_w):
    bsz, seq, _ = proj3.shape
    n_heads = hg_w // HG_EXPAND
    heads = _pick(n_heads, (4, 2, 1))
    lanes = heads * HG_EXPAND
    ts = _pick(seq, (256, 128, 64))
    base = 3 * att_w
    assert base % lanes == 0 and hg_w % lanes == 0

    def seg(k):
        off = (base + k * hg_w) // lanes
        return pl.BlockSpec((1, ts, lanes), lambda b, h, s: (b, s, off + h))

    vec = pl.BlockSpec((1, lanes), lambda b, h, s: (0, h))
    sums = jnp.asarray(_hgrn_sum_matrices(ts), BF16)
    return pl.pallas_call(
        functools.partial(_hgrn_kernel, heads=heads, ts=ts),
        grid=(bsz, hg_w // lanes, seq // ts),
        in_specs=[seg(0), seg(1), seg(2), seg(3), vec, vec,
                  pl.BlockSpec((3, ts, ts), lambda b, h, s: (0, 0, 0))],
        out_specs=pl.BlockSpec((1, ts, lanes), lambda b, h, s: (b, s, h)),
        out_shape=jax.ShapeDtypeStruct((bsz, seq, hg_w), BF16),
        scratch_shapes=[pltpu.VMEM((heads, HG_EXPAND, HG_EXPAND), F32),
                        pltpu.VMEM((ts, 2 * lanes), BF16),
                        pltpu.VMEM((ts, lanes), F32),
                        pltpu.VMEM((3, ts, lanes), F32)],
        compiler_params=_cparams(("arbitrary", "arbitrary", "arbitrary")),
        name="hgrn2",
    )(proj3, proj3, proj3, proj3, lb.reshape(1, hg_w), gw.reshape(1, hg_w), sums)


def _layer_norm(y, g, b):
    mu = jnp.mean(y, axis=-1, keepdims=True)
    yc = y - mu
    var = jnp.mean(yc * yc, axis=-1, keepdims=True)
    return yc * lax.rsqrt(var + LN_EPS) * g + b


def _bf16_split(x):
    hi = x.astype(BF16)
    lo = (x - hi.astype(F32)).astype(BF16)
    return hi, lo


def _pack_bf16_pair(lo, hi):
    lo_bits = lax.bitcast_convert_type(lo.astype(BF16).astype(F32), jnp.uint32)
    hi_bits = lax.bitcast_convert_type(hi.astype(BF16).astype(F32), jnp.uint32)
    return (lo_bits >> 16) | (hi_bits & jnp.uint32(0xFFFF0000))


def _unpack_bf16_pair(words):
    lo = lax.bitcast_convert_type(words << 16, F32)
    hi = lax.bitcast_convert_type(words & jnp.uint32(0xFFFF0000), F32)
    return lo, hi


def _load_weight_bf16(w_hbm, wbf_ref, stage_ref, sem):
    rows = stage_ref.shape[1]
    n_slab = w_hbm.shape[0] // rows

    def fetch(s, slot):
        return pltpu.make_async_copy(w_hbm.at[pl.ds(s * rows, rows)], stage_ref.at[slot],
                                     sem.at[slot])

    fetch(0, 0).start()
    for s in range(n_slab):
        if s + 1 < n_slab:
            fetch(s + 1, (s + 1) % 2).start()
        fetch(s, s % 2).wait()
        wbf_ref[s * rows:(s + 1) * rows, :] = stage_ref[s % 2].astype(BF16)


def _out_proj_kernel(att_ref, rec_ref, x_ref, wo_hbm, gt_ref, sc_ref, sh_ref, g_ref, b_ref,
                     rw_ref, rb_ref, x1_ref, h2_ref, lg_ref, wobf_ref, rwhi_ref, rwlo_ref,
                     stage_ref, hi_ref, lo_ref, sem, *, alpha, att_w):
    @pl.when(pl.program_id(0) == 0)
    def _():
        _load_weight_bf16(wo_hbm, wobf_ref, stage_ref, sem)
        hi, lo = _bf16_split(rw_ref[...])
        rwhi_ref[...] = hi
        rwlo_ref[...] = lo

    att = jnp.concatenate([att_ref[0, h] for h in range(att_ref.shape[1])], axis=1)
    mix = (jnp.dot(att, wobf_ref[:att_w, :], preferred_element_type=F32)
           + jnp.dot(rec_ref[...], wobf_ref[att_w:, :], preferred_element_type=F32))
    half = x_ref.shape[1] // 2

    for r in range(x_ref.shape[0] // LN_ROW_GROUP):
        rows = slice(r * LN_ROW_GROUP, (r + 1) * LN_ROW_GROUP)
        y = alpha * x_ref[rows, :] + (1.0 + gt_ref[0]) * mix[rows]
        x1 = _layer_norm(y, g_ref[...], b_ref[...])
        x1_ref[rows, :] = x1
        h2 = x1 * (1.0 + sc_ref[0]) + sh_ref[0]
        hi, lo = _bf16_split(h2)
        hi_ref[rows, :] = hi
        lo_ref[rows, :] = lo
        h2_ref[rows, :] = _pack_bf16_pair(h2[:, :half], h2[:, half:])

    lg = (jnp.dot(hi_ref[...], rwhi_ref[...], preferred_element_type=F32)
          + jnp.dot(lo_ref[...], rwhi_ref[...], preferred_element_type=F32)
          + jnp.dot(hi_ref[...], rwlo_ref[...], preferred_element_type=F32))
    lg_ref[...] = lg.T[:lg_ref.shape[0], :] + rb_ref[...]


OUT_PROJ_STAGE_ROWS = 256
LN_ROW_GROUP = 16


def _out_proj(att, rec, x2d, wo, gt, sc, sh, g, b, rw, rb, seq, alpha):
    t, d = x2d.shape
    hp = att.shape[1]
    att_w = hp * V7X_LANES
    n_e = rw.shape[1]
    assert n_e <= V7X_LANES and d % OUT_PROJ_STAGE_ROWS == 0
    rw_pad = jnp.zeros((d, V7X_LANES), F32).at[:, :n_e].set(rw)
    tm = _pick(seq, (512, 256, 128))
    per_b = seq // tm
    vec3 = pl.BlockSpec((1, 1, d), lambda i: (i // per_b, 0, 0))
    full = lambda shape: pl.BlockSpec(shape, lambda i: (0,) * len(shape))
    return pl.pallas_call(
        functools.partial(_out_proj_kernel, alpha=alpha, att_w=att_w),
        grid=(t // tm,),
        in_specs=[pl.BlockSpec((1, hp, tm, V7X_LANES), lambda i: (i // per_b, 0, i % per_b, 0)),
                  pl.BlockSpec((tm, d - att_w), lambda i: (i, 0)),
                  pl.BlockSpec((tm, d), lambda i: (i, 0)),
                  pl.BlockSpec(memory_space=pl.ANY), vec3, vec3, vec3, full((1, d)), full((1, d)),
                  full((d, V7X_LANES)), full((n_e, 1))],
        out_specs=[pl.BlockSpec((tm, d), lambda i: (i, 0)),
                   pl.BlockSpec((tm, d // 2), lambda i: (i, 0)),
                   pl.BlockSpec((n_e, tm), lambda i: (0, i))],
        out_shape=[jax.ShapeDtypeStruct((t, d), F32),
                   jax.ShapeDtypeStruct((t, d // 2), jnp.uint32),
                   jax.ShapeDtypeStruct((n_e, t), F32)],
        scratch_shapes=[pltpu.VMEM((d, d), BF16), pltpu.VMEM((d, V7X_LANES), BF16),
                        pltpu.VMEM((d, V7X_LANES), BF16),
                        pltpu.VMEM((2, OUT_PROJ_STAGE_ROWS, d), F32),
                        pltpu.VMEM((tm, d), BF16), pltpu.VMEM((tm, d), BF16),
                        pltpu.SemaphoreType.DMA((2,))],
        compiler_params=_cparams(("arbitrary",)),
        name="out_proj_ln1",
    )(att, rec, x2d, wo, gt[:, None, :], sc[:, None, :], sh[:, None, :],
      g.reshape(1, d), b.reshape(1, d), rw_pad, rb.reshape(n_e, 1))


def _routing_kernel(lg_ref, tri_ref, low_ref, gate_ref, dest_ref, tile_ref, pad_ref, sel_scr, rk_scr,
                    *, blk, tm):
    n_e, t = lg_ref.shape
    eidx = lax.broadcasted_iota(jnp.int32, (n_e, t), 0)
    cur = lg_ref[...]
    vals, idxs = [], []
    for _ in range(TOP_K):
        m = jnp.max(cur, axis=0, keepdims=True)
        ik = jnp.min(jnp.where(cur == m, eidx, n_e), axis=0, keepdims=True)
        cur = jnp.where(eidx == ik, -jnp.inf, cur)
        vals.append(m)
        idxs.append(ik)
    es = [jnp.exp(v - vals[0]) for v in vals]
    den = functools.reduce(lambda a, b: a + b, es)
    for k in range(TOP_K):
        gate_ref[k:k + 1, :] = es[k] / den
    sel = functools.reduce(lambda a, b: a | b, [eidx == ik for ik in idxs])
    sel_scr[...] = jnp.where(sel, 1.0, 0.0)

    tri = tri_ref[...]
    carry = jnp.zeros((n_e, 1), F32)
    for j in range(t // blk):
        sb = sel_scr[:, j * blk:(j + 1) * blk]
        pre = jnp.dot(sb.astype(BF16), tri, preferred_element_type=F32)
        rk_scr[:, j * blk:(j + 1) * blk] = pre + carry
        carry = carry + jnp.sum(sb, axis=1, keepdims=True)

    counts = jnp.broadcast_to(carry, (n_e, V7X_LANES))
    padded = jnp.floor((counts + (tm - 1)) * (1.0 / tm)) * tm
    pends = jnp.dot(low_ref[...], padded, precision=lax.Precision.HIGHEST,
                    preferred_element_type=F32)
    pstarts = pends - padded
    row0 = rk_scr[...] + pstarts[:, 0:1]
    for k in range(TOP_K):
        dest_ref[k:k + 1, :] = jnp.sum(jnp.where(eidx == idxs[k], row0, 0.0), axis=0,
                                       keepdims=True).astype(jnp.int32)
    starts = (lax.broadcasted_iota(jnp.int32, (n_e, tile_ref.shape[1]), 1) * tm).astype(F32)
    owner = jnp.sum(jnp.where(pends[:, 0:1] <= starts, 1.0, 0.0), axis=0, keepdims=True)
    tile_ref[0:1, :] = jnp.minimum(owner, n_e - 1.0).astype(jnp.int32)
    tile_ref[1:2, :] = jnp.broadcast_to(jnp.max(pends[:, 0:1], axis=0, keepdims=True),
                                        (1, tile_ref.shape[1])).astype(jnp.int32)
    e_tile = lax.broadcasted_iota(jnp.int32, starts.shape, 0).astype(F32)
    row_end = jnp.sum(jnp.where(e_tile == owner, (pstarts + counts)[:, 0:1], 0.0), axis=0,
                      keepdims=True)
    tile_ref[2:3, :] = jnp.clip(row_end - starts[0:1, :], 0.0, float(tm)).astype(jnp.int32)
    pad_ref[0] = (pstarts + counts).astype(jnp.int32)
    pad_ref[1] = (padded - counts).astype(jnp.int32)


def _routing(logits_t, tm, n_tiles):
    n_e, t = logits_t.shape
    assert tm & (tm - 1) == 0
    blk = _pick(t, (256, 128))
    tri = jnp.asarray(np.triu(np.ones((blk, blk), np.float32), 1), BF16)
    low = jnp.asarray(np.tril(np.ones((n_e, n_e), np.float32)))
    ntp = -(-n_tiles // V7X_LANES) * V7X_LANES
    full = lambda shape: pl.BlockSpec(shape, lambda: (0,) * len(shape))
    return pl.pallas_call(
        functools.partial(_routing_kernel, blk=blk, tm=tm),
        in_specs=[full((n_e, t)), full((blk, blk)), full((n_e, n_e))],
        out_specs=[full((TOP_K, t)), full((TOP_K, t)), full((3, ntp)), full((2, n_e, V7X_LANES))],
        out_shape=[jax.ShapeDtypeStruct((TOP_K, t), F32),
                   jax.ShapeDtypeStruct((TOP_K, t), jnp.int32),
                   jax.ShapeDtypeStruct((3, ntp), jnp.int32),
                   jax.ShapeDtypeStruct((2, n_e, V7X_LANES), jnp.int32)],
        scratch_shapes=[pltpu.VMEM((n_e, t), F32), pltpu.VMEM((n_e, t), F32)],
        compiler_params=pltpu.CompilerParams(vmem_limit_bytes=V7X_VMEM_LIMIT),
        name="routing",
    )(logits_t, tri, low)


def _dispatch_kernel(dest_ref, pad_off_ref, pad_n_ref, nrow_ref, h_ref, o_ref, zero_ref, sem, zsem,
                     *, n_tok, tq, tm, n_e):
    step = pl.program_id(0)
    base = step * tq

    def fill(off, n):
        return pltpu.make_async_copy(zero_ref.at[pl.ds(0, n)], o_ref.at[pl.ds(off, n)], zsem)

    sub = 8
    pieces = [1 << s for s in range(tm.bit_length() - 2, sub.bit_length() - 2, -1)]

    def pad_rows(wait):
        def per_expert(e, c):
            off = pad_off_ref[e]
            n = pad_n_ref[e]
            head = jnp.minimum(n, (-off) & (sub - 1))
            for s in range(sub - 1):
                @pl.when(s < head)
                def _(s=s):
                    cp = fill(off + s, 1)
                    cp.wait() if wait else cp.start()
            off = off + head
            n = n - head
            for p in pieces:
                hit = (n & p) != 0

                @pl.when(hit)
                def _(off=off, p=p):
                    cp = fill(pl.multiple_of(off, sub), p)
                    cp.wait() if wait else cp.start()

                off = off + jnp.where(hit, p, 0)
            return c
        lax.fori_loop(0, n_e, per_expert, 0)

        def per_tile(i, c):
            cp = fill(pl.multiple_of(i * tm, tm), tm)
            cp.wait() if wait else cp.start()
            return c
        lax.fori_loop(nrow_ref[0] // tm, o_ref.shape[0] // tm, per_tile, 0)

    @pl.when(step == 0)
    def _():
        zero_ref[...] = jnp.zeros_like(zero_ref)
        pad_rows(False)

    def copy(r, k):
        dst = o_ref.at[pl.ds(dest_ref[k * n_tok + base + r], 1)]
        return pltpu.make_async_copy(h_ref.at[pl.ds(r, 1)], dst, sem)

    def start(r, c):
        for k in range(TOP_K):
            copy(r, k).start(priority=k % 2)
        return c

    lax.fori_loop(0, tq, start, 0, unroll=2)
    for _ in range(TOP_K):
        pltpu.make_async_copy(h_ref, h_ref, sem).wait()

    @pl.when(step == 0)
    def _():
        pad_rows(True)


def _dispatch(h2p, dest_flat, pad_off, pad_n, n_used_rows, n_rows, tm):
    t, width = h2p.shape
    n_e = pad_off.shape[0]
    tq = _pick(t, (256, 128))
    return pl.pallas_call(
        functools.partial(_dispatch_kernel, n_tok=t, tq=tq, tm=tm, n_e=n_e),
        grid_spec=pltpu.PrefetchScalarGridSpec(
            num_scalar_prefetch=4, grid=(t // tq,),
            in_specs=[pl.BlockSpec((tq, width), lambda i, *_: (i, 0))],
            out_specs=pl.BlockSpec(memory_space=pl.ANY),
            scratch_shapes=[pltpu.VMEM((tm, width), h2p.dtype),
                            pltpu.SemaphoreType.DMA(()), pltpu.SemaphoreType.DMA(())]),
        out_shape=jax.ShapeDtypeStruct((n_rows, width), h2p.dtype),
        compiler_params=pltpu.CompilerParams(dimension_semantics=("arbitrary",),
                                             has_side_effects=True,
                                             vmem_limit_bytes=V7X_VMEM_LIMIT),
        name="moe_dispatch",
    )(dest_flat, pad_off, pad_n, n_used_rows, h2p)


EXPERT_COL_BLOCK = 2048
GEMM1_COL_CHUNK = 256
CAST_ROWS = 256


def _stream_expert_weights(te_ref, nt_ref, w_hbm, stage_ref, wbf_ref, sem):
    j, i = pl.program_id(0), pl.program_id(1)
    n_pass, n_tiles = pl.num_programs(0), pl.num_programs(1)
    tn = stage_ref.shape[1]
    n_used = nt_ref[0]
    cur = te_ref[i]

    def fetch(e, jj):
        src = w_hbm.at[e, :, pl.ds(pl.multiple_of(jj * tn, tn), tn)]
        return pltpu.make_async_copy(src, stage_ref, sem)

    @pl.when((i == 0) & (j == 0))
    def _():
        fetch(te_ref[0], 0).start()

    @pl.when((i < n_used) & ((i == 0) | (cur != te_ref[jnp.maximum(i - 1, 0)])))
    def _():
        fetch(cur, j).wait()

        def cast(r, c):
            rows = pl.ds(pl.multiple_of(r * CAST_ROWS, CAST_ROWS), CAST_ROWS)
            wbf_ref[rows, :] = stage_ref[rows, :].astype(BF16)
            return c
        lax.fori_loop(0, stage_ref.shape[0] // CAST_ROWS, cast, 0)

        def same_group(k):
            return (k < n_used) & (te_ref[jnp.minimum(k, n_tiles - 1)] == cur)
        nxt = lax.while_loop(same_group, lambda k: k + 1, i + 1)

        @pl.when(nxt < n_used)
        def _():
            fetch(te_ref[jnp.minimum(nxt, n_tiles - 1)], j).start()

        @pl.when((nxt >= n_used) & (j + 1 < n_pass))
        def _():
            fetch(te_ref[0], j + 1).start()


def _gemm1_kernel(te_ref, nt_ref, tv_ref, x_ref, w_hbm, b_ref, cmp_ref, o_ref, stage_ref, wbf_ref, sem,
                  *, nc):
    _stream_expert_weights(te_ref, nt_ref, w_hbm, stage_ref, wbf_ref, sem)

    def compute(rows):
        x_lo, x_hi = (v.astype(BF16) for v in _unpack_bf16_pair(x_ref[:rows, :]))
        half = x_lo.shape[1]
        cw = cmp_ref.shape[0]
        lane = lax.broadcasted_iota(jnp.int32, (rows, nc), 1)
        pieces = []
        for n0 in range(0, wbf_ref.shape[1], nc):
            hb = (jnp.dot(x_lo, wbf_ref[:half, n0:n0 + nc], preferred_element_type=F32)
                  + jnp.dot(x_hi, wbf_ref[half:, n0:n0 + nc], preferred_element_type=F32)
                  + b_ref[0, :, n0:n0 + nc])
            nxt = pltpu.roll(hb, nc - 1, axis=1)
            glu = jnp.minimum(hb, SWIGLU_LIMIT)
            lin = jnp.clip(nxt, -SWIGLU_LIMIT, SWIGLU_LIMIT)
            act = glu * jax.nn.sigmoid(SWIGLU_ALPHA * glu) * (lin + 1.0)
            act = jnp.where(lane % 2 == 0, act, 0.0).astype(BF16)
            pieces += [act[:, c * cw:(c + 1) * cw] for c in range(nc // cw)]
        packed = jnp.dot(jnp.concatenate(pieces, axis=0), cmp_ref[...],
                         preferred_element_type=F32).astype(o_ref.dtype)
        for p in range(len(pieces)):
            o_ref[:rows, p * (cw // 2):(p + 1) * (cw // 2)] = packed[p * rows:(p + 1) * rows]
        if rows < o_ref.shape[0]:
            o_ref[rows:, :] = jnp.zeros((o_ref.shape[0] - rows, o_ref.shape[1]), o_ref.dtype)

    _for_real_rows(pl.program_id(1), nt_ref, tv_ref, o_ref, compute)


def _for_real_rows(i, nt_ref, tv_ref, o_ref, compute):
    tm = o_ref.shape[0]
    used = i < nt_ref[0]
    whole = tv_ref[i] > tm // 2

    @pl.when(used & whole)
    def _():
        compute(tm)

    @pl.when(used & jnp.logical_not(whole))
    def _():
        compute(tm // 2)

    @pl.when(jnp.logical_not(used))
    def _():
        o_ref[...] = jnp.zeros_like(o_ref)


def _gemm1(xin, w1, b1, tile_e, n_tiles_used, tile_rows, tm):
    n_rows, half_d = xin.shape
    n_e, d, f2 = w1.shape
    tn = min(EXPERT_COL_BLOCK, f2)
    nc = min(GEMM1_COL_CHUNK, tn)
    assert f2 % tn == 0 and d % CAST_ROWS == 0
    n_tiles = n_rows // tm
    cw = V7X_MXU_DIM
    cmp_np = np.zeros((cw, cw // 2), np.float32)
    cmp_np[np.arange(0, cw, 2), np.arange(cw // 2)] = 1.0

    def used(i, nt):
        return jnp.minimum(i, nt[0] - 1)

    return pl.pallas_call(
        functools.partial(_gemm1_kernel, nc=nc),
        grid_spec=pltpu.PrefetchScalarGridSpec(
            num_scalar_prefetch=3, grid=(f2 // tn, n_tiles),
            in_specs=[pl.BlockSpec((tm, half_d), lambda j, i, te, nt, tv: (used(i, nt), 0)),
                      pl.BlockSpec(memory_space=pl.ANY),
                      pl.BlockSpec((1, 1, tn), lambda j, i, te, nt, tv: (te[used(i, nt)], 0, j)),
                      pl.BlockSpec((cw, cw // 2), lambda j, i, te, nt, tv: (0, 0))],
            out_specs=pl.BlockSpec((tm, tn // 2), lambda j, i, te, nt, tv: (i, j)),
            scratch_shapes=[pltpu.VMEM((d, tn), F32), pltpu.VMEM((d, tn), BF16),
                            pltpu.SemaphoreType.DMA(())]),
        out_shape=jax.ShapeDtypeStruct((n_rows, f2 // 2), BF16),
        compiler_params=_cparams(("arbitrary", "arbitrary")),
        name="moe_gemm1",
    )(tile_e, n_tiles_used, tile_rows, xin, w1, b1.reshape(n_e, 1, f2), jnp.asarray(cmp_np, BF16))


def _gemm2_kernel(te_ref, nt_ref, tv_ref, a_ref, w_hbm, b_ref, o_ref, stage_ref, wbf_ref, sem, *, nc):
    _stream_expert_weights(te_ref, nt_ref, w_hbm, stage_ref, wbf_ref, sem)

    def compute(rows):
        a = a_ref[:rows, :]
        half = wbf_ref.shape[1] // 2

        def cols(n0):
            return (jnp.dot(a, wbf_ref[:, n0:n0 + nc], preferred_element_type=F32)
                    + b_ref[0, :, n0:n0 + nc])

        for n0 in range(0, half, nc):
            o_ref[:rows, n0:n0 + nc] = _pack_bf16_pair(cols(n0), cols(half + n0))
        if rows < o_ref.shape[0]:
            o_ref[rows:, :] = jnp.zeros((o_ref.shape[0] - rows, o_ref.shape[1]), o_ref.dtype)

    _for_real_rows(pl.program_id(1), nt_ref, tv_ref, o_ref, compute)


def _gemm2(act, w2, b2, tile_e, n_tiles_used, tile_rows, tm):
    n_rows, f = act.shape
    n_e, _, d = w2.shape
    tn = d
    nc = min(512, tn // 2)
    assert f % CAST_ROWS == 0
    n_tiles = n_rows // tm

    def used(i, nt):
        return jnp.minimum(i, nt[0] - 1)

    return pl.pallas_call(
        functools.partial(_gemm2_kernel, nc=nc),
        grid_spec=pltpu.PrefetchScalarGridSpec(
            num_scalar_prefetch=3, grid=(d // tn, n_tiles),
            in_specs=[pl.BlockSpec((tm, f), lambda j, i, te, nt, tv: (used(i, nt), 0)),
                      pl.BlockSpec(memory_space=pl.ANY),
                      pl.BlockSpec((1, 1, tn), lambda j, i, te, nt, tv: (te[used(i, nt)], 0, j))],
            out_specs=pl.BlockSpec((tm, tn // 2), lambda j, i, te, nt, tv: (i, j)),
            scratch_shapes=[pltpu.VMEM((f, tn), F32), pltpu.VMEM((f, tn), BF16),
                            pltpu.SemaphoreType.DMA(())]),
        out_shape=jax.ShapeDtypeStruct((n_rows, d // 2), jnp.uint32),
        compiler_params=_cparams(("arbitrary", "arbitrary")),
        name="moe_gemm2",
    )(tile_e, n_tiles_used, tile_rows, act, w2, b2.reshape(n_e, 1, d))


def _combine_kernel(dest_ref, x1_ref, gate_ref, gt_ref, g_ref, b_ref, y_ref, o_ref, buf, sem,
                    *, alpha, tc, n_tok):
    i = pl.program_id(0)
    n_steps = pl.num_programs(0)

    def copy(step, slot, k, r):
        src = y_ref.at[pl.ds(dest_ref[k * n_tok + step * tc + r], 1)]
        return pltpu.make_async_copy(src, buf.at[slot, k, pl.ds(r, 1)], sem.at[slot])

    def issue(step, slot):
        def body(r, c):
            for k in range(TOP_K):
                copy(step, slot, k, r).start(priority=k % 2)
            return c
        lax.fori_loop(0, tc, body, 0, unroll=2)

    def drain(slot):
        pltpu.make_async_copy(buf.at[slot], buf.at[slot], sem.at[slot]).wait()

    slot = i % 2

    @pl.when(i == 0)
    def _():
        issue(0, 0)

    @pl.when(i + 1 < n_steps)
    def _():
        issue(i + 1, 1 - slot)

    drain(slot)
    gates = gate_ref[...]
    lo, hi = _unpack_bf16_pair(buf[slot, 0])
    ff_lo, ff_hi = gates[:, 0:1] * lo, gates[:, 0:1] * hi
    for k in range(1, TOP_K):
        lo, hi = _unpack_bf16_pair(buf[slot, k])
        ff_lo, ff_hi = ff_lo + gates[:, k:k + 1] * lo, ff_hi + gates[:, k:k + 1] * hi
    ff = jnp.concatenate([ff_lo, ff_hi], axis=1)
    y = alpha * x1_ref[...] + (1.0 + gt_ref[0]) * ff
    o_ref[...] = _layer_norm(y, g_ref[...], b_ref[...])


def _combine(dest_flat, x1, gates_tk, gt, g, b, y, seq, alpha):
    t, d = x1.shape
    tc = _pick(seq, (256, 128))
    per_b = seq // tc
    return pl.pallas_call(
        functools.partial(_combine_kernel, alpha=alpha, tc=tc, n_tok=t),
        grid_spec=pltpu.PrefetchScalarGridSpec(
            num_scalar_prefetch=1, grid=(t // tc,),
            in_specs=[pl.BlockSpec((tc, d), lambda i, ds: (i, 0)),
                      pl.BlockSpec((tc, TOP_K), lambda i, ds: (i, 0)),
                      pl.BlockSpec((1, 1, d), lambda i, ds: (i // per_b, 0, 0)),
                      pl.BlockSpec((1, d), lambda i, ds: (0, 0)),
                      pl.BlockSpec((1, d), lambda i, ds: (0, 0)),
                      pl.BlockSpec(memory_space=pl.ANY)],
            out_specs=pl.BlockSpec((tc, d), lambda i, ds: (i, 0)),
            scratch_shapes=[pltpu.VMEM((2, TOP_K, tc, d // 2), jnp.uint32),
                            pltpu.SemaphoreType.DMA((2,))]),
        out_shape=jax.ShapeDtypeStruct((t, d), F32),
        compiler_params=_cparams(("arbitrary",)),
        name="moe_combine_ln2",
    )(dest_flat, x1, gates_tk, gt[:, None, :], g.reshape(1, d), b.reshape(1, d), y)


MOE_ROW_TILE = 512


def _moe_ffn(h2p, logits_t, x1, gt_f, ln_g, ln_b, w1, b1, w2, b2, seq, alpha):
    tm = MOE_ROW_TILE
    n_e, t = logits_t.shape
    n_rows = -(-(TOP_K * t) // tm) * tm + n_e * tm
    n_tiles = n_rows // tm
    gate_t, dest, tiles, pads = _routing(logits_t, tm, n_tiles)
    dest = dest.reshape(-1)
    tile_e = tiles[0, :n_tiles]
    n_used_rows = tiles[1, :1]
    tile_rows = tiles[2, :n_tiles]
    n_used_tiles = n_used_rows // tm
    xin = _dispatch(h2p, dest, pads[0, :, 0], pads[1, :, 0], n_used_rows, n_rows, tm)
    act = _gemm1(xin, w1, b1, tile_e, n_used_tiles, tile_rows, tm)
    y = _gemm2(act, w2, b2, tile_e, n_used_tiles, tile_rows, tm)
    return _combine(dest, x1, gate_t.T, gt_f, ln_g, ln_b, y, seq, alpha)


def kernel(x, c, positions, w_ada, b_ada, w_in, hgrn_lb, gnorm_w, w_o, ln1_g, ln1_b,
           router_w, router_b, w1, b1, w2, b2, ln2_g, ln2_b):
    bsz, seq, d = x.shape
    depth = w_ada.shape[0]
    t = bsz * seq
    att_w = d // 2
    hg_w = d - att_w
    alpha = (2.0 * depth) ** 0.25

    lb_all = jnp.cumsum(jax.nn.softmax(hgrn_lb.astype(F32), axis=0), axis=0)
    inv = ROPE_THETA ** (-(jnp.arange(0, ROT_DIM, 2, dtype=F32) / ROT_DIM))
    lane = np.arange(V7X_LANES)
    inv_lane = inv[(lane % ATT_HEAD_DIM) % (ROT_DIM // 2)].reshape(1, V7X_LANES)
    pos_b = jnp.broadcast_to(positions.astype(F32).reshape(t, 1), (t, V7X_LANES))

    x2d = x.reshape(t, d)
    for l in range(depth):
        mod = _adaln(c, w_ada[l], b_ada[l])
        sh_a, sc_a, gt_a, sh_f, sc_f, gt_f = jnp.split(mod, 6, axis=-1)

        proj = _in_proj(x2d, sc_a, sh_a, w_in[l], seq)
        proj3 = proj.reshape(bsz, seq, proj.shape[1])
        q_hp, k_hp, v_hp = _qk_rope(proj, pos_b, inv_lane, bsz, seq, att_w)
        att = _attention(q_hp, k_hp, v_hp)
        rec = _hgrn2(proj3, lb_all[l], gnorm_w[l], att_w, hg_w).reshape(t, hg_w)

        x1, h2p, logits_t = _out_proj(att, rec, x2d, w_o[l], gt_a, sc_f, sh_f, ln1_g[l], ln1_b[l],
                                      router_w[l], router_b[l], seq, alpha)

        x2d = _moe_ffn(h2p, logits_t, x1, gt_f, ln2_g[l], ln2_b[l], w1[l], b1[l], w2[l], b2[l],
                       seq, alpha)
    return x2d.reshape(bsz, seq, d)
```

```python
import functools

import numpy as np
import jax
import jax.numpy as jnp
from jax import lax
from jax.experimental import pallas as pl
from jax.experimental.pallas import tpu as pltpu

F32 = jnp.float32
BF16 = jnp.bfloat16

V7X_LANES = 128
V7X_MXU_DIM = 256
V7X_VMEM_LIMIT = 56 * 1024 * 1024

ATT_HEAD_DIM = 64
DILATED_PAIRS = ((128, 1), (512, 4), (2048, 16))
ATT_BLOCK = 128
ROT_DIM = ATT_HEAD_DIM // 4
ROPE_THETA = 500000.0
HG_EXPAND = 128
HG_CHUNK = 64
HG_SUB = 16
TOP_K = 4
SWIGLU_ALPHA = 1.702
SWIGLU_LIMIT = 7.0
LN_EPS = 1e-5
RMS_EPS = 1e-6
NEG_INF = -1e30

NT_DIMS = (((1,), (1,)), ((), ()))


def _pick(n, candidates):
    for c in candidates:
        if n % c == 0:
            return c
    raise ValueError(f"no tile in {candidates} divides {n}")


def _cparams(sem, vmem=V7X_VMEM_LIMIT, flags=None):
    return pltpu.CompilerParams(dimension_semantics=sem, vmem_limit_bytes=vmem, flags=flags)


def _adaln_kernel(c_ref, w_ref, b_ref, o_ref):
    c = c_ref[...]
    s = c * jax.nn.sigmoid(c)
    o_ref[...] = jnp.dot(s.astype(BF16), w_ref[...].astype(BF16),
                         preferred_element_type=F32) + b_ref[...]


def _adaln(c, w, b):
    bsz, d = c.shape
    n = w.shape[1]
    rows = 8
    cp = jnp.zeros((rows, d), F32).at[:bsz].set(c)
    tn = _pick(n, (1024, 512, 256, 128))
    out = pl.pallas_call(
        _adaln_kernel,
        grid=(n // tn,),
        in_specs=[pl.BlockSpec((rows, d), lambda j: (0, 0)),
                  pl.BlockSpec((d, tn), lambda j: (0, j)),
                  pl.BlockSpec((1, tn), lambda j: (0, j))],
        out_specs=pl.BlockSpec((rows, tn), lambda j: (0, j)),
        out_shape=jax.ShapeDtypeStruct((rows, n), F32),
        compiler_params=_cparams(("arbitrary",)),
        name="adaln",
    )(cp, w, b.reshape(1, n))
    return out[:bsz]


def _in_proj_kernel(x_ref, sc_ref, sh_ref, w_ref, o_ref, wbf_ref):
    @pl.when(pl.program_id(1) == 0)
    def _():
        wbf_ref[...] = w_ref[...].astype(BF16)

    h = x_ref[...] * (1.0 + sc_ref[0]) + sh_ref[0]
    o_ref[...] = jnp.dot(h.astype(BF16), wbf_ref[...],
                         preferred_element_type=F32).astype(o_ref.dtype)


def _in_proj(x2d, sc, sh, w, seq):
    t, d = x2d.shape
    nc = w.shape[1]
    tm = _pick(seq, (512, 256, 128))
    tn = _pick(nc, (1024, 512, 256, 128))
    per_b = seq // tm
    vec = pl.BlockSpec((1, 1, d), lambda j, i: (i // per_b, 0, 0))
    return pl.pallas_call(
        _in_proj_kernel,
        grid=(nc // tn, t // tm),
        in_specs=[pl.BlockSpec((tm, d), lambda j, i: (i, 0)), vec, vec,
                  pl.BlockSpec((d, tn), lambda j, i: (0, j))],
        out_specs=pl.BlockSpec((tm, tn), lambda j, i: (i, j)),
        out_shape=jax.ShapeDtypeStruct((t, nc), BF16),
        scratch_shapes=[pltpu.VMEM((d, tn), BF16)],
        compiler_params=_cparams(("arbitrary", "arbitrary")),
        name="in_proj",
    )(x2d, sc[:, None, :], sh[:, None, :], w)


def _rope_kernel(q_ref, k_ref, v_ref, pos_ref, inv_ref, qo_ref, ko_ref, vo_ref):
    tm = pos_ref.shape[0]
    lane = lax.broadcasted_iota(jnp.int32, (tm, V7X_LANES), 1)
    lh = lane % ATT_HEAD_DIM
    half = ROT_DIM // 2
    ang = pos_ref[...] * inv_ref[...]
    cs = jnp.where(lh < ROT_DIM, jnp.cos(ang), 1.0)
    sn = jnp.sin(ang)
    sn = jnp.where(lh < half, -sn, jnp.where(lh < ROT_DIM, sn, 0.0))

    def rope(t):
        swapped = jnp.where(lh < half,
                            pltpu.roll(t, V7X_LANES - half, axis=1),
                            pltpu.roll(t, half, axis=1))
        return t * cs + swapped * sn

    for h in range(qo_ref.shape[1]):
        lanes = slice(h * V7X_LANES, (h + 1) * V7X_LANES)
        qo_ref[0, h] = rope(q_ref[:, lanes].astype(F32)) * (ATT_HEAD_DIM ** -0.5)
        ko_ref[0, h] = rope(k_ref[:, lanes].astype(F32))
        vo_ref[0, h] = v_ref[:, lanes].astype(F32)


def _qk_rope(proj, pos_b, inv_lane, bsz, seq, att_w):
    t = proj.shape[0]
    hp = att_w // V7X_LANES
    tm = _pick(seq, (256, 128))
    per_b = seq // tm
    out_spec = pl.BlockSpec((1, hp, tm, V7X_LANES), lambda i: (i // per_b, 0, i % per_b, 0))
    shp = jax.ShapeDtypeStruct((bsz, hp, seq, V7X_LANES), F32)
    return pl.pallas_call(
        _rope_kernel,
        grid=(t // tm,),
        in_specs=[pl.BlockSpec((tm, att_w), lambda i: (i, 0)),
                  pl.BlockSpec((tm, att_w), lambda i: (i, 1)),
                  pl.BlockSpec((tm, att_w), lambda i: (i, 2)),
                  pl.BlockSpec((tm, V7X_LANES), lambda i: (i, 0)),
                  pl.BlockSpec((1, V7X_LANES), lambda i: (0, 0))],
        out_specs=[out_spec, out_spec, out_spec],
        out_shape=[shp, shp, shp],
        compiler_params=_cparams(("arbitrary",)),
        name="qk_rope",
    )(proj, proj, proj, pos_b, inv_lane)


ATT_TASKS_PER_STEP = 16


def _attn_kernel(q_in, k_in, v_in, out_ref, qs, ks, vs, no, nl, *, seq):
    qn = ATT_BLOCK
    n_task = seq // qn
    n_head = V7X_LANES // ATT_HEAD_DIM
    gc = ATT_TASKS_PER_STEP
    lane = lax.broadcasted_iota(jnp.int32, (qn, V7X_LANES), 1)
    qi = lax.broadcasted_iota(jnp.int32, (gc, qn, 2 * qn), 1)
    kj = lax.broadcasted_iota(jnp.int32, (gc, qn, 2 * qn), 2)
    dist = qi + qn - kj
    band = (dist >= 0) & (dist <= qn)
    zeros = jnp.zeros((qn, V7X_LANES), BF16)

    for bi, (window, d) in enumerate(DILATED_PAIRS):
        nb = seq // (d * qn)

        def task_rows(r, n, d=d):
            start = r + n * qn * d
            return pl.ds(start, qn) if d == 1 else pl.ds(start, qn, stride=d)

        for r in range(d):
            for n in range(nb):
                g = r * nb + n
                sl = task_rows(r, n)
                qv = q_in[0, 0, sl, :]
                for h in range(n_head):
                    qs[h, g] = jnp.where(lane // ATT_HEAD_DIM == h, qv, 0.0).astype(BF16)
                for src, dst in ((k_in, ks), (v_in, vs)):
                    blk = src[0, 0, sl, :].astype(BF16)
                    dst[g, qn:, :] = blk
                    if n + 1 < nb:
                        dst[g + 1, :qn, :] = blk
                    if n == 0:
                        dst[g, :qn, :] = zeros

        def step(c, carry, bi=bi, d=d, nb=nb):
            g0 = pl.multiple_of(c * gc, gc)
            gsl = pl.ds(g0, gc)
            gidx = g0 + lax.broadcasted_iota(jnp.int32, (gc, qn, 2 * qn), 0)
            valid = band & ((gidx % nb != 0) | (kj >= qn))
            k = ks[gsl]
            v = vs[gsl]
            o = None
            for h in range(n_head):
                s = jnp.einsum("gqd,gkd->gqk", qs[h, gsl], k, preferred_element_type=F32)
                s = jnp.where(valid, s, NEG_INF)
                m = jnp.max(s, axis=-1, keepdims=True)
                p = jnp.exp(s - m)
                den = jnp.sum(p, axis=-1, keepdims=True)
                oh = jnp.einsum("gqk,gkd->gqd", p.astype(BF16), v,
                                preferred_element_type=F32) / den
                lh = jnp.broadcast_to(m + jnp.log(den), oh.shape)
                if o is None:
                    o, lse = oh, lh
                else:
                    sel = lane[None] // ATT_HEAD_DIM == h
                    o, lse = jnp.where(sel, oh, o), jnp.where(sel, lh, lse)
            for t in range(gc):
                g = g0 + t
                start = g // nb + (g % nb) * (qn * d)
                sl = pl.ds(start, qn) if d == 1 else pl.ds(start, qn, stride=d)
                no[bi, sl, :] = o[t]
                nl[bi, sl, :] = lse[t]
            return carry

        lax.fori_loop(0, n_task // gc, step, 0)

    def combine(c, carry):
        sl = pl.ds(pl.multiple_of(c * qn, qn), qn)
        ls = [nl[bi, sl, :] for bi in range(len(DILATED_PAIRS))]
        mx = functools.reduce(jnp.maximum, ls)
        ws = [jnp.exp(l - mx) for l in ls]
        tot = functools.reduce(lambda a, b: a + b, ws)
        acc = functools.reduce(lambda a, b: a + b,
                               [w * no[bi, sl, :] for bi, w in enumerate(ws)])
        out_ref[0, 0, sl, :] = (acc / tot).astype(out_ref.dtype)
        return carry

    lax.fori_loop(0, n_task, combine, 0)


def _attention(q_hp, k_hp, v_hp):
    bsz, hp, seq, _ = q_hp.shape
    for window, d in DILATED_PAIRS:
        assert window // d == ATT_BLOCK and seq % (d * ATT_BLOCK) == 0
    n_task = seq // ATT_BLOCK
    assert n_task % ATT_TASKS_PER_STEP == 0
    n_head = V7X_LANES // ATT_HEAD_DIM
    nbr = len(DILATED_PAIRS)
    spec = pl.BlockSpec((1, 1, seq, V7X_LANES), lambda b, h: (b, h, 0, 0))
    return pl.pallas_call(
        functools.partial(_attn_kernel, seq=seq),
        grid=(bsz, hp),
        in_specs=[spec, spec, spec],
        out_specs=spec,
        out_shape=jax.ShapeDtypeStruct((bsz, hp, seq, V7X_LANES), BF16),
        scratch_shapes=[pltpu.VMEM((n_head, n_task, ATT_BLOCK, V7X_LANES), BF16),
                        pltpu.VMEM((n_task, 2 * ATT_BLOCK, V7X_LANES), BF16),
                        pltpu.VMEM((n_task, 2 * ATT_BLOCK, V7X_LANES), BF16),
                        pltpu.VMEM((nbr, seq, V7X_LANES), F32),
                        pltpu.VMEM((nbr, seq, V7X_LANES), F32)],
        compiler_params=_cparams(("arbitrary", "arbitrary")),
        name="dilated_attn",
    )(q_hp, k_hp, v_hp)


def _hgrn_kernel(q_ref, f_ref, i_ref, g_ref, lb_ref, gw_ref, sums_ref, o_ref,
                 st_ref, hl_ref, kk_ref, cum_ref, *, heads, ts):
    @pl.when(pl.program_id(2) == 0)
    def _():
        st_ref[...] = jnp.zeros_like(st_ref)

    c_len = HG_CHUNK
    lanes = heads * HG_EXPAND
    lb = lb_ref[...]
    for c in range(ts // c_len):
        rows = slice(c * c_len, (c + 1) * c_len)
        f = lb + (1.0 - lb) * jax.nn.sigmoid(f_ref[0, rows, :].astype(F32))
        kk_ref[rows, :] = 1.0 - f
        hi, lo = _bf16_split(jnp.log(f))
        hl_ref[rows, :lanes] = hi
        hl_ref[rows, lanes:] = lo
    for k in range(3):
        both = jnp.dot(sums_ref[k], hl_ref[...], preferred_element_type=F32)
        cum_ref[k] = both[:, :lanes] + both[:, lanes:]

    n_sub = c_len // HG_SUB
    gi = lax.broadcasted_iota(jnp.int32, (heads * n_sub, HG_SUB, c_len), 0) % n_sub
    qi = lax.broadcasted_iota(jnp.int32, (heads * n_sub, HG_SUB, c_len), 1)
    si = lax.broadcasted_iota(jnp.int32, (heads * n_sub, HG_SUB, c_len), 2)
    causal = si <= gi * HG_SUB + qi
    key_row = lax.broadcasted_iota(jnp.int32, (c_len, HG_EXPAND), 0)

    for c in range(ts // c_len):
        rows = slice(c * c_len, (c + 1) * c_len)
        qts, kts, vbs = [], [], []
        for h in range(heads):
            lsl = slice(h * HG_EXPAND, (h + 1) * HG_EXPAND)
            b = cum_ref[0, rows, lsl]
            anchor = cum_ref[1, rows, lsl]
            kk = kk_ref[rows, lsl]
            qt = (q_ref[0, rows, lsl].astype(F32) * jnp.exp(b - anchor)).astype(BF16)
            qts.append(qt.reshape(n_sub, HG_SUB, HG_EXPAND))
            vb = i_ref[0, rows, lsl].astype(BF16)
            for i in range(n_sub):
                hi_r = (i + 1) * HG_SUB
                kt = kk * jnp.exp(anchor[i * HG_SUB:i * HG_SUB + 1, :] - b)
                kts.append(jnp.where(key_row < hi_r, kt, 0.0).astype(BF16))
                vbs.append(vb)
        a = jnp.einsum("gqk,gsk->gqs", jnp.concatenate(qts, axis=0), jnp.stack(kts),
                       preferred_element_type=F32)
        a = jnp.where(causal, a, 0.0).astype(BF16)
        o_intra = jnp.einsum("gqs,gsv->gqv", a, jnp.stack(vbs), preferred_element_type=F32)

        for h in range(heads):
            lsl = slice(h * HG_EXPAND, (h + 1) * HG_EXPAND)
            b = cum_ref[0, rows, lsl]
            b_last = cum_ref[2, rows, lsl]
            kk = kk_ref[rows, lsl]
            q = q_ref[0, rows, lsl].astype(F32)
            v = i_ref[0, rows, lsl].astype(F32)
            st = st_ref[h]
            o_inter = lax.dot_general((q * jnp.exp(b)).astype(BF16), st.astype(BF16), NT_DIMS,
                                      preferred_element_type=F32)
            o = o_inter + o_intra[h * n_sub:(h + 1) * n_sub].reshape(c_len, HG_EXPAND)
            kl = kk * jnp.exp(b_last - b)
            upd = jnp.dot(v.T.astype(BF16), kl.astype(BF16), preferred_element_type=F32)
            st_ref[h] = st * jnp.exp(b_last[0:1, :]) + upd
            o = o * lax.rsqrt(jnp.mean(o * o, axis=-1, keepdims=True) + RMS_EPS)
            g = g_ref[0, rows, lsl].astype(F32)
            o = o * gw_ref[:, lsl] * (g * jax.nn.sigmoid(g))
            o_ref[0, rows, lsl] = o.astype(o_ref.dtype)


def _hgrn_sum_matrices(ts):
    t = np.arange(ts)[:, None]
    s = np.arange(ts)[None, :]
    same = (t // HG_CHUNK) == (s // HG_CHUNK)
    mid = (t // HG_SUB) * HG_SUB + HG_SUB // 2
    return np.stack([same & (s <= t), same & (s <= mid), same]).astype(np.float32)


def _hgrn2(proj3, lb, gw, att_w, hg_w):
    bsz, seq, _ = proj3.shape
    n_heads = hg_w // HG_EXPAND
    heads = _pick(n_heads, (4, 2, 1))
    lanes = heads * HG_EXPAND
    ts = _pick(seq, (256, 128, 64))
    base = 3 * att_w
    assert base % lanes == 0 and hg_w % lanes == 0

    def seg(k):
        off = (base + k * hg_w) // lanes
        return pl.BlockSpec((1, ts, lanes), lambda b, h, s: (b, s, off + h))

    vec = pl.BlockSpec((1, lanes), lambda b, h, s: (0, h))
    sums = jnp.asarray(_hgrn_sum_matrices(ts), BF16)
    return pl.pallas_call(
        functools.partial(_hgrn_kernel, heads=heads, ts=ts),
        grid=(bsz, hg_w // lanes, seq // ts),
        in_specs=[seg(0), seg(1), seg(2), seg(3), vec, vec,
                  pl.BlockSpec((3, ts, ts), lambda b, h, s: (0, 0, 0))],
        out_specs=pl.BlockSpec((1, ts, lanes), lambda b, h, s: (b, s, h)),
        out_shape=jax.ShapeDtypeStruct((bsz, seq, hg_w), BF16),
        scratch_shapes=[pltpu.VMEM((heads, HG_EXPAND, HG_EXPAND), F32),
                        pltpu.VMEM((ts, 2 * lanes), BF16),
                        pltpu.VMEM((ts, lanes), F32),
                        pltpu.VMEM((3, ts, lanes), F32)],
        compiler_params=_cparams(("arbitrary", "arbitrary", "arbitrary")),
        name="hgrn2",
    )(proj3, proj3, proj3, proj3, lb.reshape(1, hg_w), gw.reshape(1, hg_w), sums)


def _layer_norm(y, g, b):
    mu = jnp.mean(y, axis=-1, keepdims=True)
    yc = y - mu
    var = jnp.mean(yc * yc, axis=-1, keepdims=True)
    return yc * lax.rsqrt(var + LN_EPS) * g + b


def _bf16_split(x):
    hi = x.astype(BF16)
    lo = (x - hi.astype(F32)).astype(BF16)
    return hi, lo


def _pack_bf16_pair(lo, hi):
    lo_bits = lax.bitcast_convert_type(lo.astype(BF16).astype(F32), jnp.uint32)
    hi_bits = lax.bitcast_convert_type(hi.astype(BF16).astype(F32), jnp.uint32)
    return (lo_bits >> 16) | (hi_bits & jnp.uint32(0xFFFF0000))


def _unpack_bf16_pair(words):
    lo = lax.bitcast_convert_type(words << 16, F32)
    hi = lax.bitcast_convert_type(words & jnp.uint32(0xFFFF0000), F32)
    return lo, hi


def _load_weight_bf16(w_hbm, wbf_ref, stage_ref, sem):
    rows = stage_ref.shape[1]
    n_slab = w_hbm.shape[0] // rows

    def fetch(s, slot):
        return pltpu.make_async_copy(w_hbm.at[pl.ds(s * rows, rows)], stage_ref.at[slot],
                                     sem.at[slot])

    fetch(0, 0).start()
    for s in range(n_slab):
        if s + 1 < n_slab:
            fetch(s + 1, (s + 1) % 2).start()
        fetch(s, s % 2).wait()
        wbf_ref[s * rows:(s + 1) * rows, :] = stage_ref[s % 2].astype(BF16)


def _out_proj_kernel(att_ref, rec_ref, x_ref, wo_hbm, gt_ref, sc_ref, sh_ref, g_ref, b_ref,
                     rw_ref, rb_ref, x1_ref, h2_ref, lg_ref, wobf_ref, rwhi_ref, rwlo_ref,
                     stage_ref, hi_ref, lo_ref, sem, *, alpha, att_w):
    @pl.when(pl.program_id(0) == 0)
    def _():
        _load_weight_bf16(wo_hbm, wobf_ref, stage_ref, sem)
        hi, lo = _bf16_split(rw_ref[...])
        rwhi_ref[...] = hi
        rwlo_ref[...] = lo

    att = jnp.concatenate([att_ref[0, h] for h in range(att_ref.shape[1])], axis=1)
    mix = (jnp.dot(att, wobf_ref[:att_w, :], preferred_element_type=F32)
           + jnp.dot(rec_ref[...], wobf_ref[att_w:, :], preferred_element_type=F32))
    half = x_ref.shape[1] // 2

    for r in range(x_ref.shape[0] // LN_ROW_GROUP):
        rows = slice(r * LN_ROW_GROUP, (r + 1) * LN_ROW_GROUP)
        y = alpha * x_ref[rows, :] + (1.0 + gt_ref[0]) * mix[rows]
        x1 = _layer_norm(y, g_ref[...], b_ref[...])
        x1_ref[rows, :] = x1
        h2 = x1 * (1.0 + sc_ref[0]) + sh_ref[0]
        hi, lo = _bf16_split(h2)
        hi_ref[rows, :] = hi
        lo_ref[rows, :] = lo
        h2_ref[rows, :] = _pack_bf16_pair(h2[:, :half], h2[:, half:])

    lg = (jnp.dot(hi_ref[...], rwhi_ref[...], preferred_element_type=F32)
          + jnp.dot(lo_ref[...], rwhi_ref[...], preferred_element_type=F32)
          + jnp.dot(hi_ref[...], rwlo_ref[...], preferred_element_type=F32))
    lg_ref[...] = lg.T[:lg_ref.shape[0], :] + rb_ref[...]


OUT_PROJ_STAGE_ROWS = 256
LN_ROW_GROUP = 16


def _out_proj(att, rec, x2d, wo, gt, sc, sh, g, b, rw, rb, seq, alpha):
    t, d = x2d.shape
    hp = att.shape[1]
    att_w = hp * V7X_LANES
    n_e = rw.shape[1]
    assert n_e <= V7X_LANES and d % OUT_PROJ_STAGE_ROWS == 0
    rw_pad = jnp.zeros((d, V7X_LANES), F32).at[:, :n_e].set(rw)
    tm = _pick(seq, (512, 256, 128))
    per_b = seq // tm
    vec3 = pl.BlockSpec((1, 1, d), lambda i: (i // per_b, 0, 0))
    full = lambda shape: pl.BlockSpec(shape, lambda i: (0,) * len(shape))
    return pl.pallas_call(
        functools.partial(_out_proj_kernel, alpha=alpha, att_w=att_w),
        grid=(t // tm,),
        in_specs=[pl.BlockSpec((1, hp, tm, V7X_LANES), lambda i: (i // per_b, 0, i % per_b, 0)),
                  pl.BlockSpec((tm, d - att_w), lambda i: (i, 0)),
                  pl.BlockSpec((tm, d), lambda i: (i, 0)),
                  pl.BlockSpec(memory_space=pl.ANY), vec3, vec3, vec3, full((1, d)), full((1, d)),
                  full((d, V7X_LANES)), full((n_e, 1))],
        out_specs=[pl.BlockSpec((tm, d), lambda i: (i, 0)),
                   pl.BlockSpec((tm, d // 2), lambda i: (i, 0)),
                   pl.BlockSpec((n_e, tm), lambda i: (0, i))],
        out_shape=[jax.ShapeDtypeStruct((t, d), F32),
                   jax.ShapeDtypeStruct((t, d // 2), jnp.uint32),
                   jax.ShapeDtypeStruct((n_e, t), F32)],
        scratch_shapes=[pltpu.VMEM((d, d), BF16), pltpu.VMEM((d, V7X_LANES), BF16),
                        pltpu.VMEM((d, V7X_LANES), BF16),
                        pltpu.VMEM((2, OUT_PROJ_STAGE_ROWS, d), F32),
                        pltpu.VMEM((tm, d), BF16), pltpu.VMEM((tm, d), BF16),
                        pltpu.SemaphoreType.DMA((2,))],
        compiler_params=_cparams(("arbitrary",)),
        name="out_proj_ln1",
    )(att, rec, x2d, wo, gt[:, None, :], sc[:, None, :], sh[:, None, :],
      g.reshape(1, d), b.reshape(1, d), rw_pad, rb.reshape(n_e, 1))


def _routing_kernel(lg_ref, tri_ref, low_ref, gate_ref, dest_ref, tile_ref, pad_ref, sel_scr, rk_scr,
                    *, blk, tm):
    n_e, t = lg_ref.shape
    eidx = lax.broadcasted_iota(jnp.int32, (n_e, t), 0)
    cur = lg_ref[...]
    vals, idxs = [], []
    for _ in range(TOP_K):
        m = jnp.max(cur, axis=0, keepdims=True)
        ik = jnp.min(jnp.where(cur == m, eidx, n_e), axis=0, keepdims=True)
        cur = jnp.where(eidx == ik, -jnp.inf, cur)
        vals.append(m)
        idxs.append(ik)
    es = [jnp.exp(v - vals[0]) for v in vals]
    den = functools.reduce(lambda a, b: a + b, es)
    for k in range(TOP_K):
        gate_ref[k:k + 1, :] = es[k] / den
    sel = functools.reduce(lambda a, b: a | b, [eidx == ik for ik in idxs])
    sel_scr[...] = jnp.where(sel, 1.0, 0.0)

    tri = tri_ref[...]
    carry = jnp.zeros((n_e, 1), F32)
    for j in range(t // blk):
        sb = sel_scr[:, j * blk:(j + 1) * blk]
        pre = jnp.dot(sb.astype(BF16), tri, preferred_element_type=F32)
        rk_scr[:, j * blk:(j + 1) * blk] = pre + carry
        carry = carry + jnp.sum(sb, axis=1, keepdims=True)

    counts = jnp.broadcast_to(carry, (n_e, V7X_LANES))
    padded = jnp.floor((counts + (tm - 1)) * (1.0 / tm)) * tm
    pends = jnp.dot(low_ref[...], padded, precision=lax.Precision.HIGHEST,
                    preferred_element_type=F32)
    pstarts = pends - padded
    row0 = rk_scr[...] + pstarts[:, 0:1]
    for k in range(TOP_K):
        dest_ref[k:k + 1, :] = jnp.sum(jnp.where(eidx == idxs[k], row0, 0.0), axis=0,
                                       keepdims=True).astype(jnp.int32)
    starts = (lax.broadcasted_iota(jnp.int32, (n_e, tile_ref.shape[1]), 1) * tm).astype(F32)
    owner = jnp.sum(jnp.where(pends[:, 0:1] <= starts, 1.0, 0.0), axis=0, keepdims=True)
    tile_ref[0:1, :] = jnp.minimum(owner, n_e - 1.0).astype(jnp.int32)
    tile_ref[1:2, :] = jnp.broadcast_to(jnp.max(pends[:, 0:1], axis=0, keepdims=True),
                                        (1, tile_ref.shape[1])).astype(jnp.int32)
    e_tile = lax.broadcasted_iota(jnp.int32, starts.shape, 0).astype(F32)
    row_end = jnp.sum(jnp.where(e_tile == owner, (pstarts + counts)[:, 0:1], 0.0), axis=0,
                      keepdims=True)
    tile_ref[2:3, :] = jnp.clip(row_end - starts[0:1, :], 0.0, float(tm)).astype(jnp.int32)
    pad_ref[0] = (pstarts + counts).astype(jnp.int32)
    pad_ref[1] = (padded - counts).astype(jnp.int32)


def _routing(logits_t, tm, n_tiles):
    n_e, t = logits_t.shape
    assert tm & (tm - 1) == 0
    blk = _pick(t, (256, 128))
    tri = jnp.asarray(np.triu(np.ones((blk, blk), np.float32), 1), BF16)
    low = jnp.asarray(np.tril(np.ones((n_e, n_e), np.float32)))
    ntp = -(-n_tiles // V7X_LANES) * V7X_LANES
    full = lambda shape: pl.BlockSpec(shape, lambda: (0,) * len(shape))
    return pl.pallas_call(
        functools.partial(_routing_kernel, blk=blk, tm=tm),
        in_specs=[full((n_e, t)), full((blk, blk)), full((n_e, n_e))],
        out_specs=[full((TOP_K, t)), full((TOP_K, t)), full((3, ntp)), full((2, n_e, V7X_LANES))],
        out_shape=[jax.ShapeDtypeStruct((TOP_K, t), F32),
                   jax.ShapeDtypeStruct((TOP_K, t), jnp.int32),
                   jax.ShapeDtypeStruct((3, ntp), jnp.int32),
                   jax.ShapeDtypeStruct((2, n_e, V7X_LANES), jnp.int32)],
        scratch_shapes=[pltpu.VMEM((n_e, t), F32), pltpu.VMEM((n_e, t), F32)],
        compiler_params=pltpu.CompilerParams(vmem_limit_bytes=V7X_VMEM_LIMIT),
        name="routing",
    )(logits_t, tri, low)


def _dispatch_kernel(dest_ref, pad_off_ref, pad_n_ref, nrow_ref, h_ref, o_ref, zero_ref, sem, zsem,
                     *, n_tok, tq, tm, n_e):
    step = pl.program_id(0)
    base = step * tq

    def fill(off, n):
        return pltpu.make_async_copy(zero_ref.at[pl.ds(0, n)], o_ref.at[pl.ds(off, n)], zsem)

    sub = 8
    pieces = [1 << s for s in range(tm.bit_length() - 2, sub.bit_length() - 2, -1)]

    def pad_rows(wait):
        def per_expert(e, c):
            off = pad_off_ref[e]
            n = pad_n_ref[e]
            head = jnp.minimum(n, (-off) & (sub - 1))
            for s in range(sub - 1):
                @pl.when(s < head)
                def _(s=s):
                    cp = fill(off + s, 1)
                    cp.wait() if wait else cp.start()
            off = off + head
            n = n - head
            for p in pieces:
                hit = (n & p) != 0

                @pl.when(hit)
                def _(off=off, p=p):
                    cp = fill(pl.multiple_of(off, sub), p)
                    cp.wait() if wait else cp.start()

                off = off + jnp.where(hit, p, 0)
            return c
        lax.fori_loop(0, n_e, per_expert, 0)

        def per_tile(i, c):
            cp = fill(pl.multiple_of(i * tm, tm), tm)
            cp.wait() if wait else cp.start()
            return c
        lax.fori_loop(nrow_ref[0] // tm, o_ref.shape[0] // tm, per_tile, 0)

    @pl.when(step == 0)
    def _():
        zero_ref[...] = jnp.zeros_like(zero_ref)
        pad_rows(False)

    def copy(r, k):
        dst = o_ref.at[pl.ds(dest_ref[k * n_tok + base + r], 1)]
        return pltpu.make_async_copy(h_ref.at[pl.ds(r, 1)], dst, sem)

    def start(r, c):
        for k in range(TOP_K):
            copy(r, k).start(priority=k % 2)
        return c

    lax.fori_loop(0, tq, start, 0, unroll=2)
    for _ in range(TOP_K):
        pltpu.make_async_copy(h_ref, h_ref, sem).wait()

    @pl.when(step == 0)
    def _():
        pad_rows(True)


def _dispatch(h2p, dest_flat, pad_off, pad_n, n_used_rows, n_rows, tm):
    t, width = h2p.shape
    n_e = pad_off.shape[0]
    tq = _pick(t, (256, 128))
    return pl.pallas_call(
        functools.partial(_dispatch_kernel, n_tok=t, tq=tq, tm=tm, n_e=n_e),
        grid_spec=pltpu.PrefetchScalarGridSpec(
            num_scalar_prefetch=4, grid=(t // tq,),
            in_specs=[pl.BlockSpec((tq, width), lambda i, *_: (i, 0))],
            out_specs=pl.BlockSpec(memory_space=pl.ANY),
            scratch_shapes=[pltpu.VMEM((tm, width), h2p.dtype),
                            pltpu.SemaphoreType.DMA(()), pltpu.SemaphoreType.DMA(())]),
        out_shape=jax.ShapeDtypeStruct((n_rows, width), h2p.dtype),
        compiler_params=pltpu.CompilerParams(dimension_semantics=("arbitrary",),
                                             has_side_effects=True,
                                             vmem_limit_bytes=V7X_VMEM_LIMIT),
        name="moe_dispatch",
    )(dest_flat, pad_off, pad_n, n_used_rows, h2p)


EXPERT_COL_BLOCK = 2048
GEMM1_COL_CHUNK = 256
CAST_ROWS = 256


def _stream_expert_weights(te_ref, nt_ref, w_hbm, stage_ref, wbf_ref, sem):
    j, i = pl.program_id(0), pl.program_id(1)
    n_pass, n_tiles = pl.num_programs(0), pl.num_programs(1)
    tn = stage_ref.shape[1]
    n_used = nt_ref[0]
    cur = te_ref[i]

    def fetch(e, jj):
        src = w_hbm.at[e, :, pl.ds(pl.multiple_of(jj * tn, tn), tn)]
        return pltpu.make_async_copy(src, stage_ref, sem)

    @pl.when((i == 0) & (j == 0))
    def _():
        fetch(te_ref[0], 0).start()

    @pl.when((i < n_used) & ((i == 0) | (cur != te_ref[jnp.maximum(i - 1, 0)])))
    def _():
        fetch(cur, j).wait()

        def cast(r, c):
            rows = pl.ds(pl.multiple_of(r * CAST_ROWS, CAST_ROWS), CAST_ROWS)
            wbf_ref[rows, :] = stage_ref[rows, :].astype(BF16)
            return c
        lax.fori_loop(0, stage_ref.shape[0] // CAST_ROWS, cast, 0)

        def same_group(k):
            return (k < n_used) & (te_ref[jnp.minimum(k, n_tiles - 1)] == cur)
        nxt = lax.while_loop(same_group, lambda k: k + 1, i + 1)

        @pl.when(nxt < n_used)
        def _():
            fetch(te_ref[jnp.minimum(nxt, n_tiles - 1)], j).start()

        @pl.when((nxt >= n_used) & (j + 1 < n_pass))
        def _():
            fetch(te_ref[0], j + 1).start()


def _gemm1_kernel(te_ref, nt_ref, tv_ref, x_ref, w_hbm, b_ref, cmp_ref, o_ref, stage_ref, wbf_ref, sem,
                  *, nc):
    _stream_expert_weights(te_ref, nt_ref, w_hbm, stage_ref, wbf_ref, sem)

    def compute(rows):
        x_lo, x_hi = (v.astype(BF16) for v in _unpack_bf16_pair(x_ref[:rows, :]))
        half = x_lo.shape[1]
        cw = cmp_ref.shape[0]
        lane = lax.broadcasted_iota(jnp.int32, (rows, nc), 1)
        pieces = []
        for n0 in range(0, wbf_ref.shape[1], nc):
            hb = (jnp.dot(x_lo, wbf_ref[:half, n0:n0 + nc], preferred_element_type=F32)
                  + jnp.dot(x_hi, wbf_ref[half:, n0:n0 + nc], preferred_element_type=F32)
                  + b_ref[0, :, n0:n0 + nc])
            nxt = pltpu.roll(hb, nc - 1, axis=1)
            glu = jnp.minimum(hb, SWIGLU_LIMIT)
            lin = jnp.clip(nxt, -SWIGLU_LIMIT, SWIGLU_LIMIT)
            act = glu * jax.nn.sigmoid(SWIGLU_ALPHA * glu) * (lin + 1.0)
            act = jnp.where(lane % 2 == 0, act, 0.0).astype(BF16)
            pieces += [act[:, c * cw:(c + 1) * cw] for c in range(nc // cw)]
        packed = jnp.dot(jnp.concatenate(pieces, axis=0), cmp_ref[...],
                         preferred_element_type=F32).astype(o_ref.dtype)
        for p in range(len(pieces)):
            o_ref[:rows, p * (cw // 2):(p + 1) * (cw // 2)] = packed[p * rows:(p + 1) * rows]
        if rows < o_ref.shape[0]:
            o_ref[rows:, :] = jnp.zeros((o_ref.shape[0] - rows, o_ref.shape[1]), o_ref.dtype)

    _for_real_rows(pl.program_id(1), nt_ref, tv_ref, o_ref, compute)


def _for_real_rows(i, nt_ref, tv_ref, o_ref, compute):
    tm = o_ref.shape[0]
    used = i < nt_ref[0]
    whole = tv_ref[i] > tm // 2

    @pl.when(used & whole)
    def _():
        compute(tm)

    @pl.when(used & jnp.logical_not(whole))
    def _():
        compute(tm // 2)

    @pl.when(jnp.logical_not(used))
    def _():
        o_ref[...] = jnp.zeros_like(o_ref)


def _gemm1(xin, w1, b1, tile_e, n_tiles_used, tile_rows, tm):
    n_rows, half_d = xin.shape
    n_e, d, f2 = w1.shape
    tn = min(EXPERT_COL_BLOCK, f2)
    nc = min(GEMM1_COL_CHUNK, tn)
    assert f2 % tn == 0 and d % CAST_ROWS == 0
    n_tiles = n_rows // tm
    cw = V7X_MXU_DIM
    cmp_np = np.zeros((cw, cw // 2), np.float32)
    cmp_np[np.arange(0, cw, 2), np.arange(cw // 2)] = 1.0

    def used(i, nt):
        return jnp.minimum(i, nt[0] - 1)

    return pl.pallas_call(
        functools.partial(_gemm1_kernel, nc=nc),
        grid_spec=pltpu.PrefetchScalarGridSpec(
            num_scalar_prefetch=3, grid=(f2 // tn, n_tiles),
            in_specs=[pl.BlockSpec((tm, half_d), lambda j, i, te, nt, tv: (used(i, nt), 0)),
                      pl.BlockSpec(memory_space=pl.ANY),
                      pl.BlockSpec((1, 1, tn), lambda j, i, te, nt, tv: (te[used(i, nt)], 0, j)),
                      pl.BlockSpec((cw, cw // 2), lambda j, i, te, nt, tv: (0, 0))],
            out_specs=pl.BlockSpec((tm, tn // 2), lambda j, i, te, nt, tv: (i, j)),
            scratch_shapes=[pltpu.VMEM((d, tn), F32), pltpu.VMEM((d, tn), BF16),
                            pltpu.SemaphoreType.DMA(())]),
        out_shape=jax.ShapeDtypeStruct((n_rows, f2 // 2), BF16),
        compiler_params=_cparams(("arbitrary", "arbitrary")),
        name="moe_gemm1",
    )(tile_e, n_tiles_used, tile_rows, xin, w1, b1.reshape(n_e, 1, f2), jnp.asarray(cmp_np, BF16))


def _gemm2_kernel(te_ref, nt_ref, tv_ref, a_ref, w_hbm, b_ref, o_ref, stage_ref, wbf_ref, sem, *, nc):
    _stream_expert_weights(te_ref, nt_ref, w_hbm, stage_ref, wbf_ref, sem)

    def compute(rows):
        a = a_ref[:rows, :]
        half = wbf_ref.shape[1] // 2

        def cols(n0):
            return (jnp.dot(a, wbf_ref[:, n0:n0 + nc], preferred_element_type=F32)
                    + b_ref[0, :, n0:n0 + nc])

        for n0 in range(0, half, nc):
            o_ref[:rows, n0:n0 + nc] = _pack_bf16_pair(cols(n0), cols(half + n0))
        if rows < o_ref.shape[0]:
            o_ref[rows:, :] = jnp.zeros((o_ref.shape[0] - rows, o_ref.shape[1]), o_ref.dtype)

    _for_real_rows(pl.program_id(1), nt_ref, tv_ref, o_ref, compute)


def _gemm2(act, w2, b2, tile_e, n_tiles_used, tile_rows, tm):
    n_rows, f = act.shape
    n_e, _, d = w2.shape
    tn = d
    nc = min(512, tn // 2)
    assert f % CAST_ROWS == 0
    n_tiles = n_rows // tm

    def used(i, nt):
        return jnp.minimum(i, nt[0] - 1)

    return pl.pallas_call(
        functools.partial(_gemm2_kernel, nc=nc),
        grid_spec=pltpu.PrefetchScalarGridSpec(
            num_scalar_prefetch=3, grid=(d // tn, n_tiles),
            in_specs=[pl.BlockSpec((tm, f), lambda j, i, te, nt, tv: (used(i, nt), 0)),
                      pl.BlockSpec(memory_space=pl.ANY),
                      pl.BlockSpec((1, 1, tn), lambda j, i, te, nt, tv: (te[used(i, nt)], 0, j))],
            out_specs=pl.BlockSpec((tm, tn // 2), lambda j, i, te, nt, tv: (i, j)),
            scratch_shapes=[pltpu.VMEM((f, tn), F32), pltpu.VMEM((f, tn), BF16),
                            pltpu.SemaphoreType.DMA(())]),
        out_shape=jax.ShapeDtypeStruct((n_rows, d // 2), jnp.uint32),
        compiler_params=_cparams(("arbitrary", "arbitrary")),
        name="moe_gemm2",
    )(tile_e, n_tiles_used, tile_rows, act, w2, b2.reshape(n_e, 1, d))


def _combine_kernel(dest_ref, x1_ref, gate_ref, gt_ref, g_ref, b_ref, y_ref, o_ref, buf, sem,
                    *, alpha, tc, n_tok):
    i = pl.program_id(0)
    n_steps = pl.num_programs(0)

    def copy(step, slot, k, r):
        src = y_ref.at[pl.ds(dest_ref[k * n_tok + step * tc + r], 1)]
        return pltpu.make_async_copy(src, buf.at[slot, k, pl.ds(r, 1)], sem.at[slot])

    def issue(step, slot):
        def body(r, c):
            for k in range(TOP_K):
                copy(step, slot, k, r).start(priority=k % 2)
            return c
        lax.fori_loop(0, tc, body, 0, unroll=2)

    def drain(slot):
        pltpu.make_async_copy(buf.at[slot], buf.at[slot], sem.at[slot]).wait()

    slot = i % 2

    @pl.when(i == 0)
    def _():
        issue(0, 0)

    @pl.when(i + 1 < n_steps)
    def _():
        issue(i + 1, 1 - slot)

    drain(slot)
    gates = gate_ref[...]
    lo, hi = _unpack_bf16_pair(buf[slot, 0])
    ff_lo, ff_hi = gates[:, 0:1] * lo, gates[:, 0:1] * hi
    for k in range(1, TOP_K):
        lo, hi = _unpack_bf16_pair(buf[slot, k])
        ff_lo, ff_hi = ff_lo + gates[:, k:k + 1] * lo, ff_hi + gates[:, k:k + 1] * hi
    ff = jnp.concatenate([ff_lo, ff_hi], axis=1)
    y = alpha * x1_ref[...] + (1.0 + gt_ref[0]) * ff
    o_ref[...] = _layer_norm(y, g_ref[...], b_ref[...])


def _combine(dest_flat, x1, gates_tk, gt, g, b, y, seq, alpha):
    t, d = x1.shape
    tc = _pick(seq, (256, 128))
    per_b = seq // tc
    return pl.pallas_call(
        functools.partial(_combine_kernel, alpha=alpha, tc=tc, n_tok=t),
        grid_spec=pltpu.PrefetchScalarGridSpec(
            num_scalar_prefetch=1, grid=(t // tc,),
            in_specs=[pl.BlockSpec((tc, d), lambda i, ds: (i, 0)),
                      pl.BlockSpec((tc, TOP_K), lambda i, ds: (i, 0)),
                      pl.BlockSpec((1, 1, d), lambda i, ds: (i // per_b, 0, 0)),
                      pl.BlockSpec((1, d), lambda i, ds: (0, 0)),
                      pl.BlockSpec((1, d), lambda i, ds: (0, 0)),
                      pl.BlockSpec(memory_space=pl.ANY)],
            out_specs=pl.BlockSpec((tc, d), lambda i, ds: (i, 0)),
            scratch_shapes=[pltpu.VMEM((2, TOP_K, tc, d // 2), jnp.uint32),
                            pltpu.SemaphoreType.DMA((2,))]),
        out_shape=jax.ShapeDtypeStruct((t, d), F32),
        compiler_params=_cparams(("arbitrary",)),
        name="moe_combine_ln2",
    )(dest_flat, x1, gates_tk, gt[:, None, :], g.reshape(1, d), b.reshape(1, d), y)


MOE_ROW_TILE = 512


def _moe_ffn(h2p, logits_t, x1, gt_f, ln_g, ln_b, w1, b1, w2, b2, seq, alpha):
    tm = MOE_ROW_TILE
    n_e, t = logits_t.shape
    n_rows = -(-(TOP_K * t) // tm) * tm + n_e * tm
    n_tiles = n_rows // tm
    gate_t, dest, tiles, pads = _routing(logits_t, tm, n_tiles)
    dest = dest.reshape(-1)
    tile_e = tiles[0, :n_tiles]
    n_used_rows = tiles[1, :1]
    tile_rows = tiles[2, :n_tiles]
    n_used_tiles = n_used_rows // tm
    xin = _dispatch(h2p, dest, pads[0, :, 0], pads[1, :, 0], n_used_rows, n_rows, tm)
    act = _gemm1(xin, w1, b1, tile_e, n_used_tiles, tile_rows, tm)
    y = _gemm2(act, w2, b2, tile_e, n_used_tiles, tile_rows, tm)
    return _combine(dest, x1, gate_t.T, gt_f, ln_g, ln_b, y, seq, alpha)


def kernel(x, c, positions, w_ada, b_ada, w_in, hgrn_lb, gnorm_w, w_o, ln1_g, ln1_b,
           router_w, router_b, w1, b1, w2, b2, ln2_g, ln2_b):
    bsz, seq, d = x.shape
    depth = w_ada.shape[0]
    t = bsz * seq
    att_w = d // 2
    hg_w = d - att_w
    alpha = (2.0 * depth) ** 0.25

    lb_all = jnp.cumsum(jax.nn.softmax(hgrn_lb.astype(F32), axis=0), axis=0)
    inv = ROPE_THETA ** (-(jnp.arange(0, ROT_DIM, 2, dtype=F32) / ROT_DIM))
    lane = np.arange(V7X_LANES)
    inv_lane = inv[(lane % ATT_HEAD_DIM) % (ROT_DIM // 2)].reshape(1, V7X_LANES)
    pos_b = jnp.broadcast_to(positions.astype(F32).reshape(t, 1), (t, V7X_LANES))

    x2d = x.reshape(t, d)
    for l in range(depth):
        mod = _adaln(c, w_ada[l], b_ada[l])
        sh_a, sc_a, gt_a, sh_f, sc_f, gt_f = jnp.split(mod, 6, axis=-1)

        proj = _in_proj(x2d, sc_a, sh_a, w_in[l], seq)
        proj3 = proj.reshape(bsz, seq, proj.shape[1])
        q_hp, k_hp, v_hp = _qk_rope(proj, pos_b, inv_lane, bsz, seq, att_w)
        att = _attention(q_hp, k_hp, v_hp)
        rec = _hgrn2(proj3, lb_all[l], gnorm_w[l], att_w, hg_w).reshape(t, hg_w)

        x1, h2p, logits_t = _out_proj(att, rec, x2d, w_o[l], gt_a, sc_f, sh_f, ln1_g[l], ln1_b[l],
                                      router_w[l], router_b[l], seq, alpha)

        x2d = _moe_ffn(h2p, logits_t, x1, gt_f, ln2_g[l], ln2_b[l], w1[l], b1[l], w2[l], b2[l],
                       seq, alpha)
    return x2d.reshape(bsz, seq, d)
```

```python
import functools

import numpy as np
import jax
import jax.numpy as jnp
from jax import lax
from jax.experimental import pallas as pl
from jax.experimental.pallas import tpu as pltpu

F32 = jnp.float32
BF16 = jnp.bfloat16

V7X_LANES = 128
V7X_MXU_DIM = 256
V7X_VMEM_LIMIT = 56 * 1024 * 1024

ATT_HEAD_DIM = 64
DILATED_PAIRS = ((128, 1), (512, 4), (2048, 16))
ATT_BLOCK = 128
ROT_DIM = ATT_HEAD_DIM // 4
ROPE_THETA = 500000.0
HG_EXPAND = 128
HG_CHUNK = 64
HG_SUB = 16
TOP_K = 4
SWIGLU_ALPHA = 1.702
SWIGLU_LIMIT = 7.0
LN_EPS = 1e-5
RMS_EPS = 1e-6
NEG_INF = -1e30

NT_DIMS = (((1,), (1,)), ((), ()))


def _pick(n, candidates):
    for c in candidates:
        if n % c == 0:
            return c
    raise ValueError(f"no tile in {candidates} divides {n}")


def _cparams(sem, vmem=V7X_VMEM_LIMIT, flags=None):
    return pltpu.CompilerParams(dimension_semantics=sem, vmem_limit_bytes=vmem, flags=flags)


def _adaln_kernel(c_ref, w_ref, b_ref, o_ref):
    c = c_ref[...]
    s = c * jax.nn.sigmoid(c)
    o_ref[...] = jnp.dot(s.astype(BF16), w_ref[...].astype(BF16),
                         preferred_element_type=F32) + b_ref[...]


def _adaln(c, w, b):
    bsz, d = c.shape
    n = w.shape[1]
    rows = 8
    cp = jnp.zeros((rows, d), F32).at[:bsz].set(c)
    tn = _pick(n, (1024, 512, 256, 128))
    out = pl.pallas_call(
        _adaln_kernel,
        grid=(n // tn,),
        in_specs=[pl.BlockSpec((rows, d), lambda j: (0, 0)),
                  pl.BlockSpec((d, tn), lambda j: (0, j)),
                  pl.BlockSpec((1, tn), lambda j: (0, j))],
        out_specs=pl.BlockSpec((rows, tn), lambda j: (0, j)),
        out_shape=jax.ShapeDtypeStruct((rows, n), F32),
        compiler_params=_cparams(("arbitrary",)),
        name="adaln",
    )(cp, w, b.reshape(1, n))
    return out[:bsz]


def _in_proj_kernel(x_ref, sc_ref, sh_ref, w_ref, o_ref, wbf_ref):
    @pl.when(pl.program_id(1) == 0)
    def _():
        wbf_ref[...] = w_ref[...].astype(BF16)

    h = x_ref[...] * (1.0 + sc_ref[0]) + sh_ref[0]
    o_ref[...] = jnp.dot(h.astype(BF16), wbf_ref[...],
                         preferred_element_type=F32).astype(o_ref.dtype)


def _in_proj(x2d, sc, sh, w, seq):
    t, d = x2d.shape
    nc = w.shape[1]
    tm = _pick(seq, (512, 256, 128))
    tn = _pick(nc, (1024, 512, 256, 128))
    per_b = seq // tm
    vec = pl.BlockSpec((1, 1, d), lambda j, i: (i // per_b, 0, 0))
    return pl.pallas_call(
        _in_proj_kernel,
        grid=(nc // tn, t // tm),
        in_specs=[pl.BlockSpec((tm, d), lambda j, i: (i, 0)), vec, vec,
                  pl.BlockSpec((d, tn), lambda j, i: (0, j))],
        out_specs=pl.BlockSpec((tm, tn), lambda j, i: (i, j)),
        out_shape=jax.ShapeDtypeStruct((t, nc), BF16),
        scratch_shapes=[pltpu.VMEM((d, tn), BF16)],
        compiler_params=_cparams(("arbitrary", "arbitrary")),
        name="in_proj",
    )(x2d, sc[:, None, :], sh[:, None, :], w)


def _rope_kernel(q_ref, k_ref, v_ref, pos_ref, inv_ref, qo_ref, ko_ref, vo_ref):
    tm = pos_ref.shape[0]
    lane = lax.broadcasted_iota(jnp.int32, (tm, V7X_LANES), 1)
    lh = lane % ATT_HEAD_DIM
    half = ROT_DIM // 2
    ang = pos_ref[...] * inv_ref[...]
    cs = jnp.where(lh < ROT_DIM, jnp.cos(ang), 1.0)
    sn = jnp.sin(ang)
    sn = jnp.where(lh < half, -sn, jnp.where(lh < ROT_DIM, sn, 0.0))

    def rope(t):
        swapped = jnp.where(lh < half,
                            pltpu.roll(t, V7X_LANES - half, axis=1),
                            pltpu.roll(t, half, axis=1))
        return t * cs + swapped * sn

    for h in range(qo_ref.shape[1]):
        lanes = slice(h * V7X_LANES, (h + 1) * V7X_LANES)
        qo_ref[0, h] = rope(q_ref[:, lanes].astype(F32)) * (ATT_HEAD_DIM ** -0.5)
        ko_ref[0, h] = rope(k_ref[:, lanes].astype(F32))
        vo_ref[0, h] = v_ref[:, lanes].astype(F32)


def _qk_rope(proj, pos_b, inv_lane, bsz, seq, att_w):
    t = proj.shape[0]
    hp = att_w // V7X_LANES
    tm = _pick(seq, (256, 128))
    per_b = seq // tm
    out_spec = pl.BlockSpec((1, hp, tm, V7X_LANES), lambda i: (i // per_b, 0, i % per_b, 0))
    shp = jax.ShapeDtypeStruct((bsz, hp, seq, V7X_LANES), F32)
    return pl.pallas_call(
        _rope_kernel,
        grid=(t // tm,),
        in_specs=[pl.BlockSpec((tm, att_w), lambda i: (i, 0)),
                  pl.BlockSpec((tm, att_w), lambda i: (i, 1)),
                  pl.BlockSpec((tm, att_w), lambda i: (i, 2)),
                  pl.BlockSpec((tm, V7X_LANES), lambda i: (i, 0)),
                  pl.BlockSpec((1, V7X_LANES), lambda i: (0, 0))],
        out_specs=[out_spec, out_spec, out_spec],
        out_shape=[shp, shp, shp],
        compiler_params=_cparams(("arbitrary",)),
        name="qk_rope",
    )(proj, proj, proj, pos_b, inv_lane)


ATT_TASKS_PER_STEP = 16


def _attn_kernel(q_in, k_in, v_in, out_ref, qs, ks, vs, no, nl, *, seq):
    qn = ATT_BLOCK
    n_task = seq // qn
    n_head = V7X_LANES // ATT_HEAD_DIM
    gc = ATT_TASKS_PER_STEP
    lane = lax.broadcasted_iota(jnp.int32, (qn, V7X_LANES), 1)
    qi = lax.broadcasted_iota(jnp.int32, (gc, qn, 2 * qn), 1)
    kj = lax.broadcasted_iota(jnp.int32, (gc, qn, 2 * qn), 2)
    dist = qi + qn - kj
    band = (dist >= 0) & (dist <= qn)
    zeros = jnp.zeros((qn, V7X_LANES), BF16)

    for bi, (window, d) in enumerate(DILATED_PAIRS):
        nb = seq // (d * qn)

        def task_rows(r, n, d=d):
            start = r + n * qn * d
            return pl.ds(start, qn) if d == 1 else pl.ds(start, qn, stride=d)

        for r in range(d):
            for n in range(nb):
                g = r * nb + n
                sl = task_rows(r, n)
                qv = q_in[0, 0, sl, :]
                for h in range(n_head):
                    qs[h, g] = jnp.where(lane // ATT_HEAD_DIM == h, qv, 0.0).astype(BF16)
                for src, dst in ((k_in, ks), (v_in, vs)):
                    blk = src[0, 0, sl, :].astype(BF16)
                    dst[g, qn:, :] = blk
                    if n + 1 < nb:
                        dst[g + 1, :qn, :] = blk
                    if n == 0:
                        dst[g, :qn, :] = zeros

        def step(c, carry, bi=bi, d=d, nb=nb):
            g0 = pl.multiple_of(c * gc, gc)
            gsl = pl.ds(g0, gc)
            gidx = g0 + lax.broadcasted_iota(jnp.int32, (gc, qn, 2 * qn), 0)
            valid = band & ((gidx % nb != 0) | (kj >= qn))
            k = ks[gsl]
            v = vs[gsl]
            o = None
            for h in range(n_head):
                s = jnp.einsum("gqd,gkd->gqk", qs[h, gsl], k, preferred_element_type=F32)
                s = jnp.where(valid, s, NEG_INF)
                m = jnp.max(s, axis=-1, keepdims=True)
                p = jnp.exp(s - m)
                den = jnp.sum(p, axis=-1, keepdims=True)
                oh = jnp.einsum("gqk,gkd->gqd", p.astype(BF16), v,
                                preferred_element_type=F32) / den
                lh = jnp.broadcast_to(m + jnp.log(den), oh.shape)
                if o is None:
                    o, lse = oh, lh
                else:
                    sel = lane[None] // ATT_HEAD_DIM == h
                    o, lse = jnp.where(sel, oh, o), jnp.where(sel, lh, lse)
            for t in range(gc):
                g = g0 + t
                start = g // nb + (g % nb) * (qn * d)
                sl = pl.ds(start, qn) if d == 1 else pl.ds(start, qn, stride=d)
                no[bi, sl, :] = o[t]
                nl[bi, sl, :] = lse[t]
            return carry

        lax.fori_loop(0, n_task // gc, step, 0)

    def combine(c, carry):
        sl = pl.ds(pl.multiple_of(c * qn, qn), qn)
        ls = [nl[bi, sl, :] for bi in range(len(DILATED_PAIRS))]
        mx = functools.reduce(jnp.maximum, ls)
        ws = [jnp.exp(l - mx) for l in ls]
        tot = functools.reduce(lambda a, b: a + b, ws)
        acc = functools.reduce(lambda a, b: a + b,
                               [w * no[bi, sl, :] for bi, w in enumerate(ws)])
        out_ref[0, 0, sl, :] = (acc / tot).astype(out_ref.dtype)
        return carry

    lax.fori_loop(0, n_task, combine, 0)


def _attention(q_hp, k_hp, v_hp):
    bsz, hp, seq, _ = q_hp.shape
    for window, d in DILATED_PAIRS:
        assert window // d == ATT_BLOCK and seq % (d * ATT_BLOCK) == 0
    n_task = seq // ATT_BLOCK
    assert n_task % ATT_TASKS_PER_STEP == 0
    n_head = V7X_LANES // ATT_HEAD_DIM
    nbr = len(DILATED_PAIRS)
    spec = pl.BlockSpec((1, 1, seq, V7X_LANES), lambda b, h: (b, h, 0, 0))
    return pl.pallas_call(
        functools.partial(_attn_kernel, seq=seq),
        grid=(bsz, hp),
        in_specs=[spec, spec, spec],
        out_specs=spec,
        out_shape=jax.ShapeDtypeStruct((bsz, hp, seq, V7X_LANES), BF16),
        scratch_shapes=[pltpu.VMEM((n_head, n_task, ATT_BLOCK, V7X_LANES), BF16),
                        pltpu.VMEM((n_task, 2 * ATT_BLOCK, V7X_LANES), BF16),
                        pltpu.VMEM((n_task, 2 * ATT_BLOCK, V7X_LANES), BF16),
                        pltpu.VMEM((nbr, seq, V7X_LANES), F32),
                        pltpu.VMEM((nbr, seq, V7X_LANES), F32)],
        compiler_params=_cparams(("arbitrary", "arbitrary")),
        name="dilated_attn",
    )(q_hp, k_hp, v_hp)


def _hgrn_kernel(q_ref, f_ref, i_ref, g_ref, lb_ref, gw_ref, sums_ref, o_ref,
                 st_ref, hl_ref, kk_ref, cum_ref, *, heads, ts):
    @pl.when(pl.program_id(2) == 0)
    def _():
        st_ref[...] = jnp.zeros_like(st_ref)

    c_len = HG_CHUNK
    lanes = heads * HG_EXPAND
    lb = lb_ref[...]
    for c in range(ts // c_len):
        rows = slice(c * c_len, (c + 1) * c_len)
        f = lb + (1.0 - lb) * jax.nn.sigmoid(f_ref[0, rows, :].astype(F32))
        kk_ref[rows, :] = 1.0 - f
        hi, lo = _bf16_split(jnp.log(f))
        hl_ref[rows, :lanes] = hi
        hl_ref[rows, lanes:] = lo
    for k in range(3):
        both = jnp.dot(sums_ref[k], hl_ref[...], preferred_element_type=F32)
        cum_ref[k] = both[:, :lanes] + both[:, lanes:]

    n_sub = c_len // HG_SUB
    gi = lax.broadcasted_iota(jnp.int32, (heads * n_sub, HG_SUB, c_len), 0) % n_sub
    qi = lax.broadcasted_iota(jnp.int32, (heads * n_sub, HG_SUB, c_len), 1)
    si = lax.broadcasted_iota(jnp.int32, (heads * n_sub, HG_SUB, c_len), 2)
    causal = si <= gi * HG_SUB + qi
    key_row = lax.broadcasted_iota(jnp.int32, (c_len, HG_EXPAND), 0)

    for c in range(ts // c_len):
        rows = slice(c * c_len, (c + 1) * c_len)
        qts, kts, vbs = [], [], []
        for h in range(heads):
            lsl = slice(h * HG_EXPAND, (h + 1) * HG_EXPAND)
            b = cum_ref[0, rows, lsl]
            anchor = cum_ref[1, rows, lsl]
            kk = kk_ref[rows, lsl]
            qt = (q_ref[0, rows, lsl].astype(F32) * jnp.exp(b - anchor)).astype(BF16)
            qts.append(qt.reshape(n_sub, HG_SUB, HG_EXPAND))
            vb = i_ref[0, rows, lsl].astype(BF16)
            for i in range(n_sub):
                hi_r = (i + 1) * HG_SUB
                kt = kk * jnp.exp(anchor[i * HG_SUB:i * HG_SUB + 1, :] - b)
                kts.append(jnp.where(key_row < hi_r, kt, 0.0).astype(BF16))
                vbs.append(vb)
        a = jnp.einsum("gqk,gsk->gqs", jnp.concatenate(qts, axis=0), jnp.stack(kts),
                       preferred_element_type=F32)
        a = jnp.where(causal, a, 0.0).astype(BF16)
        o_intra = jnp.einsum("gqs,gsv->gqv", a, jnp.stack(vbs), preferred_element_type=F32)

        for h in range(heads):
            lsl = slice(h * HG_EXPAND, (h + 1) * HG_EXPAND)
            b = cum_ref[0, rows, lsl]
            b_last = cum_ref[2, rows, lsl]
            kk = kk_ref[rows, lsl]
            q = q_ref[0, rows, lsl].astype(F32)
            v = i_ref[0, rows, lsl].astype(F32)
            st = st_ref[h]
            o_inter = lax.dot_general((q * jnp.exp(b)).astype(BF16), st.astype(BF16), NT_DIMS,
                                      preferred_element_type=F32)
            o = o_inter + o_intra[h * n_sub:(h + 1) * n_sub].reshape(c_len, HG_EXPAND)
            kl = kk * jnp.exp(b_last - b)
            upd = jnp.dot(v.T.astype(BF16), kl.astype(BF16), preferred_element_type=F32)
            st_ref[h] = st * jnp.exp(b_last[0:1, :]) + upd
            o = o * lax.rsqrt(jnp.mean(o * o, axis=-1, keepdims=True) + RMS_EPS)
            g = g_ref[0, rows, lsl].astype(F32)
            o = o * gw_ref[:, lsl] * (g * jax.nn.sigmoid(g))
            o_ref[0, rows, lsl] = o.astype(o_ref.dtype)


def _hgrn_sum_matrices(ts):
    t = np.arange(ts)[:, None]
    s = np.arange(ts)[None, :]
    same = (t // HG_CHUNK) == (s // HG_CHUNK)
    mid = (t // HG_SUB) * HG_SUB + HG_SUB // 2
    return np.stack([same & (s <= t), same & (s <= mid), same]).astype(np.float32)


def _hgrn2(proj3, lb, gw, att_w, hg_w):
    bsz, seq, _ = proj3.shape
    n_heads = hg_w // HG_EXPAND
    heads = _pick(n_heads, (4, 2, 1))
    lanes = heads * HG_EXPAND
    ts = _pick(seq, (256, 128, 64))
    base = 3 * att_w
    assert base % lanes == 0 and hg_w % lanes == 0

    def seg(k):
        off = (base + k * hg_w) // lanes
        return pl.BlockSpec((1, ts, lanes), lambda b, h, s: (b, s, off + h))

    vec = pl.BlockSpec((1, lanes), lambda b, h, s: (0, h))
    sums = jnp.asarray(_hgrn_sum_matrices(ts), BF16)
    return pl.pallas_call(
        functools.partial(_hgrn_kernel, heads=heads, ts=ts),
        grid=(bsz, hg_w // lanes, seq // ts),
        in_specs=[seg(0), seg(1), seg(2), seg(3), vec, vec,
                  pl.BlockSpec((3, ts, ts), lambda b, h, s: (0, 0, 0))],
        out_specs=pl.BlockSpec((1, ts, lanes), lambda b, h, s: (b, s, h)),
        out_shape=jax.ShapeDtypeStruct((bsz, seq, hg_w), BF16),
        scratch_shapes=[pltpu.VMEM((heads, HG_EXPAND, HG_EXPAND), F32),
                        pltpu.VMEM((ts, 2 * lanes), BF16),
                        pltpu.VMEM((ts, lanes), F32),
                        pltpu.VMEM((3, ts, lanes), F32)],
        compiler_params=_cparams(("arbitrary", "arbitrary", "arbitrary")),
        name="hgrn2",
    )(proj3, proj3, proj3, proj3, lb.reshape(1, hg_w), gw.reshape(1, hg_w), sums)


def _layer_norm(y, g, b):
    mu = jnp.mean(y, axis=-1, keepdims=True)
    yc = y - mu
    var = jnp.mean(yc * yc, axis=-1, keepdims=True)
    return yc * lax.rsqrt(var + LN_EPS) * g + b


def _bf16_split(x):
    hi = x.astype(BF16)
    lo = (x - hi.astype(F32)).astype(BF16)
    return hi, lo


def _pack_bf16_pair(lo, hi):
    lo_bits = lax.bitcast_convert_type(lo.astype(BF16).astype(F32), jnp.uint32)
    hi_bits = lax.bitcast_convert_type(hi.astype(BF16).astype(F32), jnp.uint32)
    return (lo_bits >> 16) | (hi_bits & jnp.uint32(0xFFFF0000))


def _store_token_rows(ref, tok0, words):
    n, width = words.shape
    nblk = width // V7X_LANES
    for c in range(nblk):
        ref[pl.ds(tok0 * nblk + c, n, stride=nblk), :] = words[:, c * V7X_LANES:(c + 1) * V7X_LANES]


def _load_token_rows(ref, n, nblk):
    return jnp.concatenate([ref[pl.ds(c, n, stride=nblk), :] for c in range(nblk)], axis=1)


def _unpack_bf16_pair(words):
    lo = lax.bitcast_convert_type(words << 16, F32)
    hi = lax.bitcast_convert_type(words & jnp.uint32(0xFFFF0000), F32)
    return lo, hi


def _load_weight_bf16(w_hbm, wbf_ref, stage_ref, sem):
    rows = stage_ref.shape[1]
    n_slab = w_hbm.shape[0] // rows

    def fetch(s, slot):
        return pltpu.make_async_copy(w_hbm.at[pl.ds(s * rows, rows)], stage_ref.at[slot],
                                     sem.at[slot])

    fetch(0, 0).start()
    for s in range(n_slab):
        if s + 1 < n_slab:
            fetch(s + 1, (s + 1) % 2).start()
        fetch(s, s % 2).wait()
        wbf_ref[s * rows:(s + 1) * rows, :] = stage_ref[s % 2].astype(BF16)


def _out_proj_kernel(att_ref, rec_ref, x_ref, wo_hbm, gt_ref, sc_ref, sh_ref, g_ref, b_ref,
                     rw_ref, rb_ref, x1_ref, h2_ref, lg_ref, wobf_ref, rwhi_ref, rwlo_ref,
                     stage_ref, hi_ref, lo_ref, sem, *, alpha, att_w):
    @pl.when(pl.program_id(0) == 0)
    def _():
        _load_weight_bf16(wo_hbm, wobf_ref, stage_ref, sem)
        hi, lo = _bf16_split(rw_ref[...])
        rwhi_ref[...] = hi
        rwlo_ref[...] = lo

    att = jnp.concatenate([att_ref[0, h] for h in range(att_ref.shape[1])], axis=1)
    mix = (jnp.dot(att, wobf_ref[:att_w, :], preferred_element_type=F32)
           + jnp.dot(rec_ref[...], wobf_ref[att_w:, :], preferred_element_type=F32))
    half = x_ref.shape[1] // 2

    for r in range(x_ref.shape[0] // LN_ROW_GROUP):
        rows = slice(r * LN_ROW_GROUP, (r + 1) * LN_ROW_GROUP)
        y = alpha * x_ref[rows, :] + (1.0 + gt_ref[0]) * mix[rows]
        x1 = _layer_norm(y, g_ref[...], b_ref[...])
        x1_ref[rows, :] = x1
        h2 = x1 * (1.0 + sc_ref[0]) + sh_ref[0]
        hi, lo = _bf16_split(h2)
        hi_ref[rows, :] = hi
        lo_ref[rows, :] = lo
        _store_token_rows(h2_ref, r * LN_ROW_GROUP, _pack_bf16_pair(h2[:, :half], h2[:, half:]))

    lg = (jnp.dot(hi_ref[...], rwhi_ref[...], preferred_element_type=F32)
          + jnp.dot(lo_ref[...], rwhi_ref[...], preferred_element_type=F32)
          + jnp.dot(hi_ref[...], rwlo_ref[...], preferred_element_type=F32))
    lg_ref[...] = lg.T[:lg_ref.shape[0], :] + rb_ref[...]


OUT_PROJ_STAGE_ROWS = 256
LN_ROW_GROUP = 16


def _out_proj(att, rec, x2d, wo, gt, sc, sh, g, b, rw, rb, seq, alpha):
    t, d = x2d.shape
    hp = att.shape[1]
    att_w = hp * V7X_LANES
    n_e = rw.shape[1]
    assert n_e <= V7X_LANES and d % OUT_PROJ_STAGE_ROWS == 0
    rw_pad = jnp.zeros((d, V7X_LANES), F32).at[:, :n_e].set(rw)
    nblk = d // 2 // V7X_LANES
    tm = _pick(seq, (512, 256, 128))
    per_b = seq // tm
    vec3 = pl.BlockSpec((1, 1, d), lambda i: (i // per_b, 0, 0))
    full = lambda shape: pl.BlockSpec(shape, lambda i: (0,) * len(shape))
    return pl.pallas_call(
        functools.partial(_out_proj_kernel, alpha=alpha, att_w=att_w),
        grid=(t // tm,),
        in_specs=[pl.BlockSpec((1, hp, tm, V7X_LANES), lambda i: (i // per_b, 0, i % per_b, 0)),
                  pl.BlockSpec((tm, d - att_w), lambda i: (i, 0)),
                  pl.BlockSpec((tm, d), lambda i: (i, 0)),
                  pl.BlockSpec(memory_space=pl.ANY), vec3, vec3, vec3, full((1, d)), full((1, d)),
                  full((d, V7X_LANES)), full((n_e, 1))],
        out_specs=[pl.BlockSpec((tm, d), lambda i: (i, 0)),
                   pl.BlockSpec((tm * nblk, V7X_LANES), lambda i: (i, 0)),
                   pl.BlockSpec((n_e, tm), lambda i: (0, i))],
        out_shape=[jax.ShapeDtypeStruct((t, d), F32),
                   jax.ShapeDtypeStruct((t * nblk, V7X_LANES), jnp.uint32),
                   jax.ShapeDtypeStruct((n_e, t), F32)],
        scratch_shapes=[pltpu.VMEM((d, d), BF16), pltpu.VMEM((d, V7X_LANES), BF16),
                        pltpu.VMEM((d, V7X_LANES), BF16),
                        pltpu.VMEM((2, OUT_PROJ_STAGE_ROWS, d), F32),
                        pltpu.VMEM((tm, d), BF16), pltpu.VMEM((tm, d), BF16),
                        pltpu.SemaphoreType.DMA((2,))],
        compiler_params=_cparams(("arbitrary",)),
        name="out_proj_ln1",
    )(att, rec, x2d, wo, gt[:, None, :], sc[:, None, :], sh[:, None, :],
      g.reshape(1, d), b.reshape(1, d), rw_pad, rb.reshape(n_e, 1))


def _routing_kernel(lg_ref, tri_ref, low_ref, gate_ref, dest_ref, tile_ref, pad_ref, sel_scr, rk_scr,
                    *, blk, tm):
    n_e, t = lg_ref.shape
    eidx = lax.broadcasted_iota(jnp.int32, (n_e, t), 0)
    cur = lg_ref[...]
    vals, idxs = [], []
    for _ in range(TOP_K):
        m = jnp.max(cur, axis=0, keepdims=True)
        ik = jnp.min(jnp.where(cur == m, eidx, n_e), axis=0, keepdims=True)
        cur = jnp.where(eidx == ik, -jnp.inf, cur)
        vals.append(m)
        idxs.append(ik)
    es = [jnp.exp(v - vals[0]) for v in vals]
    den = functools.reduce(lambda a, b: a + b, es)
    for k in range(TOP_K):
        gate_ref[k:k + 1, :] = es[k] / den
    sel = functools.reduce(lambda a, b: a | b, [eidx == ik for ik in idxs])
    sel_scr[...] = jnp.where(sel, 1.0, 0.0)

    tri = tri_ref[...]
    carry = jnp.zeros((n_e, 1), F32)
    for j in range(t // blk):
        sb = sel_scr[:, j * blk:(j + 1) * blk]
        pre = jnp.dot(sb.astype(BF16), tri, preferred_element_type=F32)
        rk_scr[:, j * blk:(j + 1) * blk] = pre + carry
        carry = carry + jnp.sum(sb, axis=1, keepdims=True)

    counts = jnp.broadcast_to(carry, (n_e, V7X_LANES))
    padded = jnp.floor((counts + (tm - 1)) * (1.0 / tm)) * tm
    pends = jnp.dot(low_ref[...], padded, precision=lax.Precision.HIGHEST,
                    preferred_element_type=F32)
    pstarts = pends - padded
    row0 = rk_scr[...] + pstarts[:, 0:1]
    for k in range(TOP_K):
        dest_ref[k:k + 1, :] = jnp.sum(jnp.where(eidx == idxs[k], row0, 0.0), axis=0,
                                       keepdims=True).astype(jnp.int32)
    starts = (lax.broadcasted_iota(jnp.int32, (n_e, tile_ref.shape[1]), 1) * tm).astype(F32)
    owner = jnp.sum(jnp.where(pends[:, 0:1] <= starts, 1.0, 0.0), axis=0, keepdims=True)
    tile_ref[0:1, :] = jnp.minimum(owner, n_e - 1.0).astype(jnp.int32)
    tile_ref[1:2, :] = jnp.broadcast_to(jnp.max(pends[:, 0:1], axis=0, keepdims=True),
                                        (1, tile_ref.shape[1])).astype(jnp.int32)
    e_tile = lax.broadcasted_iota(jnp.int32, starts.shape, 0).astype(F32)
    row_end = jnp.sum(jnp.where(e_tile == owner, (pstarts + counts)[:, 0:1], 0.0), axis=0,
                      keepdims=True)
    tile_ref[2:3, :] = jnp.clip(row_end - starts[0:1, :], 0.0, float(tm)).astype(jnp.int32)
    pad_ref[0] = (pstarts + counts).astype(jnp.int32)
    pad_ref[1] = (padded - counts).astype(jnp.int32)


def _routing(logits_t, tm, n_tiles):
    n_e, t = logits_t.shape
    assert tm & (tm - 1) == 0
    blk = _pick(t, (256, 128))
    tri = jnp.asarray(np.triu(np.ones((blk, blk), np.float32), 1), BF16)
    low = jnp.asarray(np.tril(np.ones((n_e, n_e), np.float32)))
    ntp = -(-n_tiles // V7X_LANES) * V7X_LANES
    full = lambda shape: pl.BlockSpec(shape, lambda: (0,) * len(shape))
    return pl.pallas_call(
        functools.partial(_routing_kernel, blk=blk, tm=tm),
        in_specs=[full((n_e, t)), full((blk, blk)), full((n_e, n_e))],
        out_specs=[full((TOP_K, t)), full((TOP_K, t)), full((3, ntp)), full((2, n_e, V7X_LANES))],
        out_shape=[jax.ShapeDtypeStruct((TOP_K, t), F32),
                   jax.ShapeDtypeStruct((TOP_K, t), jnp.int32),
                   jax.ShapeDtypeStruct((3, ntp), jnp.int32),
                   jax.ShapeDtypeStruct((2, n_e, V7X_LANES), jnp.int32)],
        scratch_shapes=[pltpu.VMEM((n_e, t), F32), pltpu.VMEM((n_e, t), F32)],
        compiler_params=pltpu.CompilerParams(vmem_limit_bytes=V7X_VMEM_LIMIT),
        name="routing",
    )(logits_t, tri, low)


def _dispatch_kernel(dest_ref, pad_off_ref, pad_n_ref, nrow_ref, h_ref, o_ref, zero_ref, sem, zsem,
                     *, n_tok, tq, tm, n_e, nblk):
    step = pl.program_id(0)
    base = step * tq

    def rows(tok, n=1):
        return pl.ds(pl.multiple_of(tok * nblk, nblk), n * nblk)

    def fill(off, n):
        return pltpu.make_async_copy(zero_ref.at[rows(0, n)], o_ref.at[rows(off, n)], zsem)

    pieces = [1 << s for s in range(tm.bit_length() - 2, -1, -1)]

    def pad_rows(wait):
        def per_expert(e, c):
            off = pad_off_ref[e]
            n = pad_n_ref[e]
            for p in pieces:
                hit = (n & p) != 0

                @pl.when(hit)
                def _(off=off, p=p):
                    cp = fill(off, p)
                    cp.wait() if wait else cp.start()

                off = off + jnp.where(hit, p, 0)
            return c
        lax.fori_loop(0, n_e, per_expert, 0)

        def per_tile(i, c):
            cp = fill(i * tm, tm)
            cp.wait() if wait else cp.start()
            return c
        lax.fori_loop(nrow_ref[0] // tm, o_ref.shape[0] // (tm * nblk), per_tile, 0)

    @pl.when(step == 0)
    def _():
        zero_ref[...] = jnp.zeros_like(zero_ref)
        pad_rows(False)

    def start(r, c):
        for k in range(TOP_K):
            dst = o_ref.at[rows(dest_ref[k * n_tok + base + r])]
            pltpu.make_async_copy(h_ref.at[rows(r)], dst, sem).start(priority=k % 2)
        return c

    lax.fori_loop(0, tq, start, 0, unroll=2)
    for _ in range(TOP_K):
        pltpu.make_async_copy(h_ref, h_ref, sem).wait()

    @pl.when(step == 0)
    def _():
        pad_rows(True)


def _dispatch(h2p, dest_flat, pad_off, pad_n, n_used_rows, n_rows, tm, nblk):
    t = h2p.shape[0] // nblk
    n_e = pad_off.shape[0]
    tq = _pick(t, (256, 128))
    return pl.pallas_call(
        functools.partial(_dispatch_kernel, n_tok=t, tq=tq, tm=tm, n_e=n_e, nblk=nblk),
        grid_spec=pltpu.PrefetchScalarGridSpec(
            num_scalar_prefetch=4, grid=(t // tq,),
            in_specs=[pl.BlockSpec((tq * nblk, V7X_LANES), lambda i, *_: (i, 0))],
            out_specs=pl.BlockSpec(memory_space=pl.ANY),
            scratch_shapes=[pltpu.VMEM((tm * nblk, V7X_LANES), h2p.dtype),
                            pltpu.SemaphoreType.DMA(()), pltpu.SemaphoreType.DMA(())]),
        out_shape=jax.ShapeDtypeStruct((n_rows * nblk, V7X_LANES), h2p.dtype),
        compiler_params=pltpu.CompilerParams(dimension_semantics=("arbitrary",),
                                             has_side_effects=True,
                                             vmem_limit_bytes=V7X_VMEM_LIMIT),
        name="moe_dispatch",
    )(dest_flat, pad_off, pad_n, n_used_rows, h2p)


EXPERT_COL_BLOCK = 2048
GEMM1_COL_CHUNK = 256
CAST_ROWS = 256


def _stream_expert_weights(te_ref, nt_ref, w_hbm, stage_ref, wbf_ref, sem):
    j, i = pl.program_id(0), pl.program_id(1)
    n_pass, n_tiles = pl.num_programs(0), pl.num_programs(1)
    tn = stage_ref.shape[1]
    n_used = nt_ref[0]
    cur = te_ref[i]

    def fetch(e, jj):
        src = w_hbm.at[e, :, pl.ds(pl.multiple_of(jj * tn, tn), tn)]
        return pltpu.make_async_copy(src, stage_ref, sem)

    @pl.when((i == 0) & (j == 0))
    def _():
        fetch(te_ref[0], 0).start()

    @pl.when((i < n_used) & ((i == 0) | (cur != te_ref[jnp.maximum(i - 1, 0)])))
    def _():
        fetch(cur, j).wait()

        def cast(r, c):
            rows = pl.ds(pl.multiple_of(r * CAST_ROWS, CAST_ROWS), CAST_ROWS)
            wbf_ref[rows, :] = stage_ref[rows, :].astype(BF16)
            return c
        lax.fori_loop(0, stage_ref.shape[0] // CAST_ROWS, cast, 0)

        def same_group(k):
            return (k < n_used) & (te_ref[jnp.minimum(k, n_tiles - 1)] == cur)
        nxt = lax.while_loop(same_group, lambda k: k + 1, i + 1)

        @pl.when(nxt < n_used)
        def _():
            fetch(te_ref[jnp.minimum(nxt, n_tiles - 1)], j).start()

        @pl.when((nxt >= n_used) & (j + 1 < n_pass))
        def _():
            fetch(te_ref[0], j + 1).start()


def _gemm1_kernel(te_ref, nt_ref, tv_ref, x_ref, w_hbm, b_ref, cmp_ref, o_ref, stage_ref, wbf_ref, sem,
                  *, nc, nblk):
    _stream_expert_weights(te_ref, nt_ref, w_hbm, stage_ref, wbf_ref, sem)

    def compute(rows):
        words = _load_token_rows(x_ref, rows, nblk)
        x_lo, x_hi = (v.astype(BF16) for v in _unpack_bf16_pair(words))
        half = x_lo.shape[1]
        cw = cmp_ref.shape[0]
        lane = lax.broadcasted_iota(jnp.int32, (rows, nc), 1)
        pieces = []
        for n0 in range(0, wbf_ref.shape[1], nc):
            hb = (jnp.dot(x_lo, wbf_ref[:half, n0:n0 + nc], preferred_element_type=F32)
                  + jnp.dot(x_hi, wbf_ref[half:, n0:n0 + nc], preferred_element_type=F32)
                  + b_ref[0, :, n0:n0 + nc])
            nxt = pltpu.roll(hb, nc - 1, axis=1)
            glu = jnp.minimum(hb, SWIGLU_LIMIT)
            lin = jnp.clip(nxt, -SWIGLU_LIMIT, SWIGLU_LIMIT)
            act = glu * jax.nn.sigmoid(SWIGLU_ALPHA * glu) * (lin + 1.0)
            act = jnp.where(lane % 2 == 0, act, 0.0).astype(BF16)
            pieces += [act[:, c * cw:(c + 1) * cw] for c in range(nc // cw)]
        packed = jnp.dot(jnp.concatenate(pieces, axis=0), cmp_ref[...],
                         preferred_element_type=F32).astype(o_ref.dtype)
        for p in range(len(pieces)):
            o_ref[:rows, p * (cw // 2):(p + 1) * (cw // 2)] = packed[p * rows:(p + 1) * rows]
        if rows < o_ref.shape[0]:
            o_ref[rows:, :] = jnp.zeros((o_ref.shape[0] - rows, o_ref.shape[1]), o_ref.dtype)

    _for_real_rows(pl.program_id(1), nt_ref, tv_ref, o_ref, o_ref.shape[0], compute)


def _for_real_rows(i, nt_ref, tv_ref, o_ref, tm, compute):
    used = i < nt_ref[0]
    whole = tv_ref[i] > tm // 2

    @pl.when(used & whole)
    def _():
        compute(tm)

    @pl.when(used & jnp.logical_not(whole))
    def _():
        compute(tm // 2)

    @pl.when(jnp.logical_not(used))
    def _():
        o_ref[...] = jnp.zeros_like(o_ref)


def _gemm1(xin, w1, b1, tile_e, n_tiles_used, tile_rows, tm, nblk):
    n_rows = xin.shape[0] // nblk
    n_e, d, f2 = w1.shape
    tn = min(EXPERT_COL_BLOCK, f2)
    nc = min(GEMM1_COL_CHUNK, tn)
    assert f2 % tn == 0 and d % CAST_ROWS == 0
    n_tiles = n_rows // tm
    cw = V7X_MXU_DIM
    cmp_np = np.zeros((cw, cw // 2), np.float32)
    cmp_np[np.arange(0, cw, 2), np.arange(cw // 2)] = 1.0

    def used(i, nt):
        return jnp.minimum(i, nt[0] - 1)

    return pl.pallas_call(
        functools.partial(_gemm1_kernel, nc=nc, nblk=nblk),
        grid_spec=pltpu.PrefetchScalarGridSpec(
            num_scalar_prefetch=3, grid=(f2 // tn, n_tiles),
            in_specs=[pl.BlockSpec((tm * nblk, V7X_LANES),
                                   lambda j, i, te, nt, tv: (used(i, nt), 0)),
                      pl.BlockSpec(memory_space=pl.ANY),
                      pl.BlockSpec((1, 1, tn), lambda j, i, te, nt, tv: (te[used(i, nt)], 0, j)),
                      pl.BlockSpec((cw, cw // 2), lambda j, i, te, nt, tv: (0, 0))],
            out_specs=pl.BlockSpec((tm, tn // 2), lambda j, i, te, nt, tv: (i, j)),
            scratch_shapes=[pltpu.VMEM((d, tn), F32), pltpu.VMEM((d, tn), BF16),
                            pltpu.SemaphoreType.DMA(())]),
        out_shape=jax.ShapeDtypeStruct((n_rows, f2 // 2), BF16),
        compiler_params=_cparams(("arbitrary", "arbitrary")),
        name="moe_gemm1",
    )(tile_e, n_tiles_used, tile_rows, xin, w1, b1.reshape(n_e, 1, f2), jnp.asarray(cmp_np, BF16))


def _gemm2_kernel(te_ref, nt_ref, tv_ref, a_ref, w_hbm, b_ref, o_ref, stage_ref, wbf_ref, sem,
                  *, nc, nblk):
    _stream_expert_weights(te_ref, nt_ref, w_hbm, stage_ref, wbf_ref, sem)
    tm = a_ref.shape[0]

    def compute(rows):
        a = a_ref[:rows, :]
        half = wbf_ref.shape[1] // 2

        def cols(n0):
            return (jnp.dot(a, wbf_ref[:, n0:n0 + nc], preferred_element_type=F32)
                    + b_ref[0, :, n0:n0 + nc])

        for n0 in range(0, half, nc):
            words = _pack_bf16_pair(cols(n0), cols(half + n0))
            for c in range(nc // V7X_LANES):
                blk = n0 // V7X_LANES + c
                o_ref[pl.ds(blk, rows, stride=nblk), :] = words[:, c * V7X_LANES:(c + 1) * V7X_LANES]
        if rows < tm:
            o_ref[rows * nblk:, :] = jnp.zeros(((tm - rows) * nblk, o_ref.shape[1]), o_ref.dtype)

    _for_real_rows(pl.program_id(1), nt_ref, tv_ref, o_ref, tm, compute)


def _gemm2(act, w2, b2, tile_e, n_tiles_used, tile_rows, tm):
    n_rows, f = act.shape
    n_e, _, d = w2.shape
    tn = d
    nc = min(512, tn // 2)
    nblk = d // 2 // V7X_LANES
    assert f % CAST_ROWS == 0
    n_tiles = n_rows // tm

    def used(i, nt):
        return jnp.minimum(i, nt[0] - 1)

    return pl.pallas_call(
        functools.partial(_gemm2_kernel, nc=nc, nblk=nblk),
        grid_spec=pltpu.PrefetchScalarGridSpec(
            num_scalar_prefetch=3, grid=(d // tn, n_tiles),
            in_specs=[pl.BlockSpec((tm, f), lambda j, i, te, nt, tv: (used(i, nt), 0)),
                      pl.BlockSpec(memory_space=pl.ANY),
                      pl.BlockSpec((1, 1, tn), lambda j, i, te, nt, tv: (te[used(i, nt)], 0, j))],
            out_specs=pl.BlockSpec((tm * nblk, V7X_LANES), lambda j, i, te, nt, tv: (i, 0)),
            scratch_shapes=[pltpu.VMEM((f, tn), F32), pltpu.VMEM((f, tn), BF16),
                            pltpu.SemaphoreType.DMA(())]),
        out_shape=jax.ShapeDtypeStruct((n_rows * nblk, V7X_LANES), jnp.uint32),
        compiler_params=_cparams(("arbitrary", "arbitrary")),
        name="moe_gemm2",
    )(tile_e, n_tiles_used, tile_rows, act, w2, b2.reshape(n_e, 1, d))


def _combine_kernel(dest_ref, x1_ref, gate_ref, gt_ref, g_ref, b_ref, y_ref, o_ref, buf, sem,
                    *, alpha, tc, n_tok, nblk):
    i = pl.program_id(0)
    n_steps = pl.num_programs(0)

    def rows(tok):
        return pl.ds(pl.multiple_of(tok * nblk, nblk), nblk)

    def copy(step, slot, k, r):
        src = y_ref.at[rows(dest_ref[k * n_tok + step * tc + r])]
        return pltpu.make_async_copy(src, buf.at[slot, k, rows(r)], sem.at[slot])

    def issue(step, slot):
        def body(r, c):
            for k in range(TOP_K):
                copy(step, slot, k, r).start(priority=k % 2)
            return c
        lax.fori_loop(0, tc, body, 0, unroll=2)

    def drain(slot):
        pltpu.make_async_copy(buf.at[slot], buf.at[slot], sem.at[slot]).wait()

    slot = i % 2

    @pl.when(i == 0)
    def _():
        issue(0, 0)

    @pl.when(i + 1 < n_steps)
    def _():
        issue(i + 1, 1 - slot)

    drain(slot)
    gates = gate_ref[...]
    lo, hi = _unpack_bf16_pair(_load_token_rows(buf.at[slot, 0], tc, nblk))
    ff_lo, ff_hi = gates[:, 0:1] * lo, gates[:, 0:1] * hi
    for k in range(1, TOP_K):
        lo, hi = _unpack_bf16_pair(_load_token_rows(buf.at[slot, k], tc, nblk))
        ff_lo, ff_hi = ff_lo + gates[:, k:k + 1] * lo, ff_hi + gates[:, k:k + 1] * hi
    ff = jnp.concatenate([ff_lo, ff_hi], axis=1)
    y = alpha * x1_ref[...] + (1.0 + gt_ref[0]) * ff
    o_ref[...] = _layer_norm(y, g_ref[...], b_ref[...])


def _combine(dest_flat, x1, gates_tk, gt, g, b, y, seq, alpha):
    t, d = x1.shape
    tc = _pick(seq, (256, 128))
    per_b = seq // tc
    nblk = d // 2 // V7X_LANES
    return pl.pallas_call(
        functools.partial(_combine_kernel, alpha=alpha, tc=tc, n_tok=t, nblk=nblk),
        grid_spec=pltpu.PrefetchScalarGridSpec(
            num_scalar_prefetch=1, grid=(t // tc,),
            in_specs=[pl.BlockSpec((tc, d), lambda i, ds: (i, 0)),
                      pl.BlockSpec((tc, TOP_K), lambda i, ds: (i, 0)),
                      pl.BlockSpec((1, 1, d), lambda i, ds: (i // per_b, 0, 0)),
                      pl.BlockSpec((1, d), lambda i, ds: (0, 0)),
                      pl.BlockSpec((1, d), lambda i, ds: (0, 0)),
                      pl.BlockSpec(memory_space=pl.ANY)],
            out_specs=pl.BlockSpec((tc, d), lambda i, ds: (i, 0)),
            scratch_shapes=[pltpu.VMEM((2, TOP_K, tc * nblk, V7X_LANES), jnp.uint32),
                            pltpu.SemaphoreType.DMA((2,))]),
        out_shape=jax.ShapeDtypeStruct((t, d), F32),
        compiler_params=_cparams(("arbitrary",)),
        name="moe_combine_ln2",
    )(dest_flat, x1, gates_tk, gt[:, None, :], g.reshape(1, d), b.reshape(1, d), y)


MOE_ROW_TILE = 512


def _moe_ffn(h2p, logits_t, x1, gt_f, ln_g, ln_b, w1, b1, w2, b2, seq, alpha):
    tm = MOE_ROW_TILE
    n_e, t = logits_t.shape
    n_rows = -(-(TOP_K * t) // tm) * tm + n_e * tm
    n_tiles = n_rows // tm
    gate_t, dest, tiles, pads = _routing(logits_t, tm, n_tiles)
    dest = dest.reshape(-1)
    tile_e = tiles[0, :n_tiles]
    n_used_rows = tiles[1, :1]
    tile_rows = tiles[2, :n_tiles]
    n_used_tiles = n_used_rows // tm
    nblk = x1.shape[1] // 2 // V7X_LANES
    xin = _dispatch(h2p, dest, pads[0, :, 0], pads[1, :, 0], n_used_rows, n_rows, tm, nblk)
    act = _gemm1(xin, w1, b1, tile_e, n_used_tiles, tile_rows, tm, nblk)
    y = _gemm2(act, w2, b2, tile_e, n_used_tiles, tile_rows, tm)
    return _combine(dest, x1, gate_t.T, gt_f, ln_g, ln_b, y, seq, alpha)


def kernel(x, c, positions, w_ada, b_ada, w_in, hgrn_lb, gnorm_w, w_o, ln1_g, ln1_b,
           router_w, router_b, w1, b1, w2, b2, ln2_g, ln2_b):
    bsz, seq, d = x.shape
    depth = w_ada.shape[0]
    t = bsz * seq
    att_w = d // 2
    hg_w = d - att_w
    alpha = (2.0 * depth) ** 0.25

    lb_all = jnp.cumsum(jax.nn.softmax(hgrn_lb.astype(F32), axis=0), axis=0)
    inv = ROPE_THETA ** (-(jnp.arange(0, ROT_DIM, 2, dtype=F32) / ROT_DIM))
    lane = np.arange(V7X_LANES)
    inv_lane = inv[(lane % ATT_HEAD_DIM) % (ROT_DIM // 2)].reshape(1, V7X_LANES)
    pos_b = jnp.broadcast_to(positions.astype(F32).reshape(t, 1), (t, V7X_LANES))

    x2d = x.reshape(t, d)
    for l in range(depth):
        mod = _adaln(c, w_ada[l], b_ada[l])
        sh_a, sc_a, gt_a, sh_f, sc_f, gt_f = jnp.split(mod, 6, axis=-1)

        proj = _in_proj(x2d, sc_a, sh_a, w_in[l], seq)
        proj3 = proj.reshape(bsz, seq, proj.shape[1])
        q_hp, k_hp, v_hp = _qk_rope(proj, pos_b, inv_lane, bsz, seq, att_w)
        att = _attention(q_hp, k_hp, v_hp)
        rec = _hgrn2(proj3, lb_all[l], gnorm_w[l], att_w, hg_w).reshape(t, hg_w)

        x1, h2p, logits_t = _out_proj(att, rec, x2d, w_o[l], gt_a, sc_f, sh_f, ln1_g[l], ln1_b[l],
                                      router_w[l], router_b[l], seq, alpha)

        x2d = _moe_ffn(h2p, logits_t, x1, gt_f, ln2_g[l], ln2_b[l], w1[l], b1[l], w2[l], b2[l],
                       seq, alpha)
    return x2d.reshape(bsz, seq, d)
```

```python
import functools

import numpy as np
import jax
import jax.numpy as jnp
from jax import lax
from jax.experimental import pallas as pl
from jax.experimental.pallas import tpu as pltpu

F32 = jnp.float32
BF16 = jnp.bfloat16

V7X_LANES = 128
V7X_MXU_DIM = 256
V7X_VMEM_LIMIT = 56 * 1024 * 1024

ATT_HEAD_DIM = 64
DILATED_PAIRS = ((128, 1), (512, 4), (2048, 16))
ATT_BLOCK = 128
ROT_DIM = ATT_HEAD_DIM // 4
ROPE_THETA = 500000.0
HG_EXPAND = 128
HG_CHUNK = 64
HG_SUB = 16
TOP_K = 4
SWIGLU_ALPHA = 1.702
SWIGLU_LIMIT = 7.0
LN_EPS = 1e-5
RMS_EPS = 1e-6
NEG_INF = -1e30

NT_DIMS = (((1,), (1,)), ((), ()))


def _pick(n, candidates):
    for c in candidates:
        if n % c == 0:
            return c
    raise ValueError(f"no tile in {candidates} divides {n}")


def _cparams(sem, vmem=V7X_VMEM_LIMIT, flags=None):
    return pltpu.CompilerParams(dimension_semantics=sem, vmem_limit_bytes=vmem, flags=flags)


def _adaln_kernel(c_ref, w_ref, b_ref, o_ref):
    c = c_ref[...]
    s = c * jax.nn.sigmoid(c)
    o_ref[...] = jnp.dot(s.astype(BF16), w_ref[...].astype(BF16),
                         preferred_element_type=F32) + b_ref[...]


def _adaln(c, w, b):
    bsz, d = c.shape
    n = w.shape[1]
    rows = 8
    cp = jnp.zeros((rows, d), F32).at[:bsz].set(c)
    tn = _pick(n, (1024, 512, 256, 128))
    out = pl.pallas_call(
        _adaln_kernel,
        grid=(n // tn,),
        in_specs=[pl.BlockSpec((rows, d), lambda j: (0, 0)),
                  pl.BlockSpec((d, tn), lambda j: (0, j)),
                  pl.BlockSpec((1, tn), lambda j: (0, j))],
        out_specs=pl.BlockSpec((rows, tn), lambda j: (0, j)),
        out_shape=jax.ShapeDtypeStruct((rows, n), F32),
        compiler_params=_cparams(("arbitrary",)),
        name="adaln",
    )(cp, w, b.reshape(1, n))
    return out[:bsz]


def _in_proj_kernel(x_ref, sc_ref, sh_ref, w_ref, o_ref, wbf_ref):
    @pl.when(pl.program_id(1) == 0)
    def _():
        wbf_ref[...] = w_ref[...].astype(BF16)

    h = x_ref[...] * (1.0 + sc_ref[0]) + sh_ref[0]
    o_ref[...] = jnp.dot(h.astype(BF16), wbf_ref[...],
                         preferred_element_type=F32).astype(o_ref.dtype)


def _in_proj(x2d, sc, sh, w, seq):
    t, d = x2d.shape
    nc = w.shape[1]
    tm = _pick(seq, (512, 256, 128))
    tn = _pick(nc, (1024, 512, 256, 128))
    per_b = seq // tm
    vec = pl.BlockSpec((1, 1, d), lambda j, i: (i // per_b, 0, 0))
    return pl.pallas_call(
        _in_proj_kernel,
        grid=(nc // tn, t // tm),
        in_specs=[pl.BlockSpec((tm, d), lambda j, i: (i, 0)), vec, vec,
                  pl.BlockSpec((d, tn), lambda j, i: (0, j))],
        out_specs=pl.BlockSpec((tm, tn), lambda j, i: (i, j)),
        out_shape=jax.ShapeDtypeStruct((t, nc), BF16),
        scratch_shapes=[pltpu.VMEM((d, tn), BF16)],
        compiler_params=_cparams(("arbitrary", "arbitrary")),
        name="in_proj",
    )(x2d, sc[:, None, :], sh[:, None, :], w)


def _rope_kernel(q_ref, k_ref, v_ref, pos_ref, inv_ref, qo_ref, ko_ref, vo_ref):
    tm = pos_ref.shape[0]
    lane = lax.broadcasted_iota(jnp.int32, (tm, V7X_LANES), 1)
    lh = lane % ATT_HEAD_DIM
    half = ROT_DIM // 2
    ang = pos_ref[...] * inv_ref[...]
    cs = jnp.where(lh < ROT_DIM, jnp.cos(ang), 1.0)
    sn = jnp.sin(ang)
    sn = jnp.where(lh < half, -sn, jnp.where(lh < ROT_DIM, sn, 0.0))

    def rope(t):
        swapped = jnp.where(lh < half,
                            pltpu.roll(t, V7X_LANES - half, axis=1),
                            pltpu.roll(t, half, axis=1))
        return t * cs + swapped * sn

    for h in range(qo_ref.shape[1]):
        lanes = slice(h * V7X_LANES, (h + 1) * V7X_LANES)
        qo_ref[0, h] = rope(q_ref[:, lanes].astype(F32)) * (ATT_HEAD_DIM ** -0.5)
        ko_ref[0, h] = rope(k_ref[:, lanes].astype(F32))
        vo_ref[0, h] = v_ref[:, lanes].astype(F32)


def _qk_rope(proj, pos_b, inv_lane, bsz, seq, att_w):
    t = proj.shape[0]
    hp = att_w // V7X_LANES
    tm = _pick(seq, (256, 128))
    per_b = seq // tm
    out_spec = pl.BlockSpec((1, hp, tm, V7X_LANES), lambda i: (i // per_b, 0, i % per_b, 0))
    shp = jax.ShapeDtypeStruct((bsz, hp, seq, V7X_LANES), F32)
    return pl.pallas_call(
        _rope_kernel,
        grid=(t // tm,),
        in_specs=[pl.BlockSpec((tm, att_w), lambda i: (i, 0)),
                  pl.BlockSpec((tm, att_w), lambda i: (i, 1)),
                  pl.BlockSpec((tm, att_w), lambda i: (i, 2)),
                  pl.BlockSpec((tm, V7X_LANES), lambda i: (i, 0)),
                  pl.BlockSpec((1, V7X_LANES), lambda i: (0, 0))],
        out_specs=[out_spec, out_spec, out_spec],
        out_shape=[shp, shp, shp],
        compiler_params=_cparams(("arbitrary",)),
        name="qk_rope",
    )(proj, proj, proj, pos_b, inv_lane)


ATT_TASKS_PER_STEP = 8


def _attn_kernel(q_in, k_in, v_in, out_ref, qs, ks, vs, sm, no_lo, nl_lo, no_s, nl_s, nat, *, seq):
    qn = ATT_BLOCK
    n_task = seq // qn
    n_head = V7X_LANES // ATT_HEAD_DIM
    gc = min(ATT_TASKS_PER_STEP, n_task)
    (_, d_lo), (_, dm), (_, d_hi) = DILATED_PAIRS
    g_hi = d_hi // dm
    rows_m = seq // dm
    lane = lax.broadcasted_iota(jnp.int32, (qn, V7X_LANES), 1)
    qi = lax.broadcasted_iota(jnp.int32, (gc, qn, 2 * qn), 1)
    kj = lax.broadcasted_iota(jnp.int32, (gc, qn, 2 * qn), 2)
    dist = qi + qn - kj
    band = (dist >= 0) & (dist <= qn)
    zeros = jnp.zeros((qn, V7X_LANES), BF16)
    srcs = (q_in, k_in, v_in)

    for a, src in enumerate(srcs):
        for r in range(dm):
            sm[a, r] = src[0, 0, pl.ds(r, rows_m, stride=dm), :]

    def task_index(d, r, n):
        if d == d_lo:
            return None, slice(n * qn, (n + 1) * qn)
        if d == dm:
            return r, slice(n * qn, (n + 1) * qn)
        return r % dm, pl.ds(r // dm + n * qn * g_hi, qn, stride=g_hi)

    for bi, (window, d) in enumerate(DILATED_PAIRS):
        nb = seq // (d * qn)
        tasks = [(r, n) for r in range(d) for n in range(nb)]
        for g, (r, n) in enumerate(tasks):
            sel, rows = task_index(d, r, n)

            def operand(a, sel=sel, rows=rows):
                return srcs[a][0, 0, rows, :] if sel is None else sm[a, sel, rows, :]

            qv = operand(0)
            for h in range(n_head):
                qs[h, g] = jnp.where(lane // ATT_HEAD_DIM == h, qv, 0.0).astype(BF16)
            for a, dst in ((1, ks), (2, vs)):
                blk = operand(a).astype(BF16)
                dst[g, qn:, :] = blk
                if n + 1 < nb:
                    dst[g + 1, :qn, :] = blk
                if n == 0:
                    dst[g, :qn, :] = zeros

        for g0 in range(0, n_task, gc):
            gsl = slice(g0, g0 + gc)
            gidx = g0 + lax.broadcasted_iota(jnp.int32, (gc, qn, 2 * qn), 0)
            valid = band & ((gidx % nb != 0) | (kj >= qn))
            k = ks[gsl]
            v = vs[gsl]
            o = None
            for h in range(n_head):
                s = jnp.einsum("gqd,gkd->gqk", qs[h, gsl], k, preferred_element_type=F32)
                s = jnp.where(valid, s, NEG_INF)
                m = jnp.max(s, axis=-1, keepdims=True)
                p = jnp.exp(s - m)
                den = jnp.sum(p, axis=-1, keepdims=True)
                oh = jnp.einsum("gqk,gkd->gqd", p.astype(BF16), v,
                                preferred_element_type=F32) / den
                lh = jnp.broadcast_to(m + jnp.log(den), oh.shape)
                if o is None:
                    o, lse = oh, lh
                else:
                    in_head = lane[None] // ATT_HEAD_DIM == h
                    o, lse = jnp.where(in_head, oh, o), jnp.where(in_head, lh, lse)
            for t in range(gc):
                sel, rows = task_index(d, *tasks[g0 + t])
                if sel is None:
                    no_lo[rows, :] = o[t]
                    nl_lo[rows, :] = lse[t]
                else:
                    no_s[bi - 1, sel, rows, :] = o[t]
                    nl_s[bi - 1, sel, rows, :] = lse[t]

    for r in range(dm):
        for c in range(rows_m // qn):
            rows = slice(c * qn, (c + 1) * qn)
            seq_rows = pl.ds(r + dm * c * qn, qn, stride=dm)
            ls = [nl_lo[seq_rows, :], nl_s[0, r, rows, :], nl_s[1, r, rows, :]]
            os_ = [no_lo[seq_rows, :], no_s[0, r, rows, :], no_s[1, r, rows, :]]
            mx = functools.reduce(jnp.maximum, ls)
            ws = [jnp.exp(l - mx) for l in ls]
            tot = functools.reduce(lambda x, y: x + y, ws)
            acc = functools.reduce(lambda x, y: x + y, [w * o for w, o in zip(ws, os_)])
            nat[seq_rows, :] = acc / tot
    out_ref[0, 0] = nat[...].astype(out_ref.dtype)


def _attention(q_hp, k_hp, v_hp):
    bsz, hp, seq, _ = q_hp.shape
    (_, d_lo), (_, dm), (_, d_hi) = DILATED_PAIRS
    assert d_lo == 1 and d_hi % dm == 0
    for window, d in DILATED_PAIRS:
        assert window // d == ATT_BLOCK and seq % (d * ATT_BLOCK) == 0
    n_task = seq // ATT_BLOCK
    assert n_task % min(ATT_TASKS_PER_STEP, n_task) == 0
    n_head = V7X_LANES // ATT_HEAD_DIM
    streams = (dm, seq // dm, V7X_LANES)
    spec = pl.BlockSpec((1, 1, seq, V7X_LANES), lambda b, h: (b, h, 0, 0))
    return pl.pallas_call(
        functools.partial(_attn_kernel, seq=seq),
        grid=(bsz, hp),
        in_specs=[spec, spec, spec],
        out_specs=spec,
        out_shape=jax.ShapeDtypeStruct((bsz, hp, seq, V7X_LANES), BF16),
        scratch_shapes=[pltpu.VMEM((n_head, n_task, ATT_BLOCK, V7X_LANES), BF16),
                        pltpu.VMEM((n_task, 2 * ATT_BLOCK, V7X_LANES), BF16),
                        pltpu.VMEM((n_task, 2 * ATT_BLOCK, V7X_LANES), BF16),
                        pltpu.VMEM((3,) + streams, F32),
                        pltpu.VMEM((seq, V7X_LANES), F32), pltpu.VMEM((seq, V7X_LANES), F32),
                        pltpu.VMEM((2,) + streams, F32), pltpu.VMEM((2,) + streams, F32),
                        pltpu.VMEM((seq, V7X_LANES), F32)],
        compiler_params=_cparams(("arbitrary", "arbitrary")),
        name="dilated_attn",
    )(q_hp, k_hp, v_hp)


def _hgrn_kernel(q_ref, f_ref, i_ref, g_ref, lb_ref, gw_ref, sums_ref, o_ref,
                 st_ref, hl_ref, kk_ref, cum_ref, *, heads, ts):
    @pl.when(pl.program_id(2) == 0)
    def _():
        st_ref[...] = jnp.zeros_like(st_ref)

    c_len = HG_CHUNK
    lanes = heads * HG_EXPAND
    lb = lb_ref[...]
    for c in range(ts // c_len):
        rows = slice(c * c_len, (c + 1) * c_len)
        f = lb + (1.0 - lb) * jax.nn.sigmoid(f_ref[0, rows, :].astype(F32))
        kk_ref[rows, :] = 1.0 - f
        hi, lo = _bf16_split(jnp.log(f))
        hl_ref[rows, :lanes] = hi
        hl_ref[rows, lanes:] = lo
    for k in range(3):
        both = jnp.dot(sums_ref[k], hl_ref[...], preferred_element_type=F32)
        cum_ref[k] = both[:, :lanes] + both[:, lanes:]

    n_sub = c_len // HG_SUB
    gi = lax.broadcasted_iota(jnp.int32, (heads * n_sub, HG_SUB, c_len), 0) % n_sub
    qi = lax.broadcasted_iota(jnp.int32, (heads * n_sub, HG_SUB, c_len), 1)
    si = lax.broadcasted_iota(jnp.int32, (heads * n_sub, HG_SUB, c_len), 2)
    causal = si <= gi * HG_SUB + qi
    key_row = lax.broadcasted_iota(jnp.int32, (c_len, HG_EXPAND), 0)

    for c in range(ts // c_len):
        rows = slice(c * c_len, (c + 1) * c_len)
        qts, kts, vbs = [], [], []
        for h in range(heads):
            lsl = slice(h * HG_EXPAND, (h + 1) * HG_EXPAND)
            b = cum_ref[0, rows, lsl]
            anchor = cum_ref[1, rows, lsl]
            kk = kk_ref[rows, lsl]
            qt = (q_ref[0, rows, lsl].astype(F32) * jnp.exp(b - anchor)).astype(BF16)
            qts.append(qt.reshape(n_sub, HG_SUB, HG_EXPAND))
            vb = i_ref[0, rows, lsl].astype(BF16)
            for i in range(n_sub):
                hi_r = (i + 1) * HG_SUB
                kt = kk * jnp.exp(anchor[i * HG_SUB:i * HG_SUB + 1, :] - b)
                kts.append(jnp.where(key_row < hi_r, kt, 0.0).astype(BF16))
                vbs.append(vb)
        a = jnp.einsum("gqk,gsk->gqs", jnp.concatenate(qts, axis=0), jnp.stack(kts),
                       preferred_element_type=F32)
        a = jnp.where(causal, a, 0.0).astype(BF16)
        o_intra = jnp.einsum("gqs,gsv->gqv", a, jnp.stack(vbs), preferred_element_type=F32)

        for h in range(heads):
            lsl = slice(h * HG_EXPAND, (h + 1) * HG_EXPAND)
            b = cum_ref[0, rows, lsl]
            b_last = cum_ref[2, rows, lsl]
            kk = kk_ref[rows, lsl]
            q = q_ref[0, rows, lsl].astype(F32)
            v = i_ref[0, rows, lsl].astype(F32)
            st = st_ref[h]
            o_inter = lax.dot_general((q * jnp.exp(b)).astype(BF16), st.astype(BF16), NT_DIMS,
                                      preferred_element_type=F32)
            o = o_inter + o_intra[h * n_sub:(h + 1) * n_sub].reshape(c_len, HG_EXPAND)
            kl = kk * jnp.exp(b_last - b)
            upd = jnp.dot(v.T.astype(BF16), kl.astype(BF16), preferred_element_type=F32)
            st_ref[h] = st * jnp.exp(b_last[0:1, :]) + upd
            o = o * lax.rsqrt(jnp.mean(o * o, axis=-1, keepdims=True) + RMS_EPS)
            g = g_ref[0, rows, lsl].astype(F32)
            o = o * gw_ref[:, lsl] * (g * jax.nn.sigmoid(g))
            o_ref[0, rows, lsl] = o.astype(o_ref.dtype)


def _hgrn_sum_matrices(ts):
    t = np.arange(ts)[:, None]
    s = np.arange(ts)[None, :]
    same = (t // HG_CHUNK) == (s // HG_CHUNK)
    mid = (t // HG_SUB) * HG_SUB + HG_SUB // 2
    return np.stack([same & (s <= t), same & (s <= mid), same]).astype(np.float32)


def _hgrn2(proj3, lb, gw, att_w, hg_w):
    bsz, seq, _ = proj3.shape
    n_heads = hg_w // HG_EXPAND
    heads = _pick(n_heads, (4, 2, 1))
    lanes = heads * HG_EXPAND
    ts = _pick(seq, (256, 128, 64))
    base = 3 * att_w
    assert base % lanes == 0 and hg_w % lanes == 0

    def seg(k):
        off = (base + k * hg_w) // lanes
        return pl.BlockSpec((1, ts, lanes), lambda b, h, s: (b, s, off + h))

    vec = pl.BlockSpec((1, lanes), lambda b, h, s: (0, h))
    sums = jnp.asarray(_hgrn_sum_matrices(ts), BF16)
    return pl.pallas_call(
        functools.partial(_hgrn_kernel, heads=heads, ts=ts),
        grid=(bsz, hg_w // lanes, seq // ts),
        in_specs=[seg(0), seg(1), seg(2), seg(3), vec, vec,
                  pl.BlockSpec((3, ts, ts), lambda b, h, s: (0, 0, 0))],
        out_specs=pl.BlockSpec((1, ts, lanes), lambda b, h, s: (b, s, h)),
        out_shape=jax.ShapeDtypeStruct((bsz, seq, hg_w), BF16),
        scratch_shapes=[pltpu.VMEM((heads, HG_EXPAND, HG_EXPAND), F32),
                        pltpu.VMEM((ts, 2 * lanes), BF16),
                        pltpu.VMEM((ts, lanes), F32),
                        pltpu.VMEM((3, ts, lanes), F32)],
        compiler_params=_cparams(("arbitrary", "arbitrary", "arbitrary")),
        name="hgrn2",
    )(proj3, proj3, proj3, proj3, lb.reshape(1, hg_w), gw.reshape(1, hg_w), sums)


def _layer_norm(y, g, b):
    mu = jnp.mean(y, axis=-1, keepdims=True)
    yc = y - mu
    var = jnp.mean(yc * yc, axis=-1, keepdims=True)
    return yc * lax.rsqrt(var + LN_EPS) * g + b


def _bf16_split(x):
    hi = x.astype(BF16)
    lo = (x - hi.astype(F32)).astype(BF16)
    return hi, lo


def _pack_bf16_pair(lo, hi):
    lo_bits = lax.bitcast_convert_type(lo.astype(BF16).astype(F32), jnp.uint32)
    hi_bits = lax.bitcast_convert_type(hi.astype(BF16).astype(F32), jnp.uint32)
    return (lo_bits >> 16) | (hi_bits & jnp.uint32(0xFFFF0000))


def _store_token_rows(ref, tok0, words):
    n, width = words.shape
    nblk = width // V7X_LANES
    for c in range(nblk):
        ref[pl.ds(tok0 * nblk + c, n, stride=nblk), :] = words[:, c * V7X_LANES:(c + 1) * V7X_LANES]


def _load_token_rows(ref, n, nblk):
    return jnp.concatenate([ref[pl.ds(c, n, stride=nblk), :] for c in range(nblk)], axis=1)


def _unpack_bf16_pair(words):
    lo = lax.bitcast_convert_type(words << 16, F32)
    hi = lax.bitcast_convert_type(words & jnp.uint32(0xFFFF0000), F32)
    return lo, hi


def _load_weight_bf16(w_hbm, wbf_ref, stage_ref, sem):
    rows = stage_ref.shape[1]
    n_slab = w_hbm.shape[0] // rows

    def fetch(s, slot):
        return pltpu.make_async_copy(w_hbm.at[pl.ds(s * rows, rows)], stage_ref.at[slot],
                                     sem.at[slot])

    fetch(0, 0).start()
    for s in range(n_slab):
        if s + 1 < n_slab:
            fetch(s + 1, (s + 1) % 2).start()
        fetch(s, s % 2).wait()
        wbf_ref[s * rows:(s + 1) * rows, :] = stage_ref[s % 2].astype(BF16)


def _out_proj_kernel(att_ref, rec_ref, x_ref, wo_hbm, gt_ref, sc_ref, sh_ref, g_ref, b_ref,
                     rw_ref, rb_ref, x1_ref, h2_ref, lg_ref, wobf_ref, rwhi_ref, rwlo_ref,
                     stage_ref, hi_ref, lo_ref, sem, *, alpha, att_w):
    @pl.when(pl.program_id(0) == 0)
    def _():
        _load_weight_bf16(wo_hbm, wobf_ref, stage_ref, sem)
        hi, lo = _bf16_split(rw_ref[...])
        rwhi_ref[...] = hi
        rwlo_ref[...] = lo

    att = jnp.concatenate([att_ref[0, h] for h in range(att_ref.shape[1])], axis=1)
    mix = (jnp.dot(att, wobf_ref[:att_w, :], preferred_element_type=F32)
           + jnp.dot(rec_ref[...], wobf_ref[att_w:, :], preferred_element_type=F32))
    half = x_ref.shape[1] // 2

    for r in range(x_ref.shape[0] // LN_ROW_GROUP):
        rows = slice(r * LN_ROW_GROUP, (r + 1) * LN_ROW_GROUP)
        y = alpha * x_ref[rows, :] + (1.0 + gt_ref[0]) * mix[rows]
        x1 = _layer_norm(y, g_ref[...], b_ref[...])
        x1_ref[rows, :] = x1
        h2 = x1 * (1.0 + sc_ref[0]) + sh_ref[0]
        hi, lo = _bf16_split(h2)
        hi_ref[rows, :] = hi
        lo_ref[rows, :] = lo
        _store_token_rows(h2_ref, r * LN_ROW_GROUP, _pack_bf16_pair(h2[:, :half], h2[:, half:]))

    lg = (jnp.dot(hi_ref[...], rwhi_ref[...], preferred_element_type=F32)
          + jnp.dot(lo_ref[...], rwhi_ref[...], preferred_element_type=F32)
          + jnp.dot(hi_ref[...], rwlo_ref[...], preferred_element_type=F32))
    lg_ref[...] = lg.T[:lg_ref.shape[0], :] + rb_ref[...]


OUT_PROJ_STAGE_ROWS = 256
LN_ROW_GROUP = 16


def _out_proj(att, rec, x2d, wo, gt, sc, sh, g, b, rw, rb, seq, alpha):
    t, d = x2d.shape
    hp = att.shape[1]
    att_w = hp * V7X_LANES
    n_e = rw.shape[1]
    assert n_e <= V7X_LANES and d % OUT_PROJ_STAGE_ROWS == 0
    rw_pad = jnp.zeros((d, V7X_LANES), F32).at[:, :n_e].set(rw)
    nblk = d // 2 // V7X_LANES
    tm = _pick(seq, (512, 256, 128))
    per_b = seq // tm
    vec3 = pl.BlockSpec((1, 1, d), lambda i: (i // per_b, 0, 0))
    full = lambda shape: pl.BlockSpec(shape, lambda i: (0,) * len(shape))
    return pl.pallas_call(
        functools.partial(_out_proj_kernel, alpha=alpha, att_w=att_w),
        grid=(t // tm,),
        in_specs=[pl.BlockSpec((1, hp, tm, V7X_LANES), lambda i: (i // per_b, 0, i % per_b, 0)),
                  pl.BlockSpec((tm, d - att_w), lambda i: (i, 0)),
                  pl.BlockSpec((tm, d), lambda i: (i, 0)),
                  pl.BlockSpec(memory_space=pl.ANY), vec3, vec3, vec3, full((1, d)), full((1, d)),
                  full((d, V7X_LANES)), full((n_e, 1))],
        out_specs=[pl.BlockSpec((tm, d), lambda i: (i, 0)),
                   pl.BlockSpec((tm * nblk, V7X_LANES), lambda i: (i, 0)),
                   pl.BlockSpec((n_e, tm), lambda i: (0, i))],
        out_shape=[jax.ShapeDtypeStruct((t, d), F32),
                   jax.ShapeDtypeStruct((t * nblk, V7X_LANES), jnp.uint32),
                   jax.ShapeDtypeStruct((n_e, t), F32)],
        scratch_shapes=[pltpu.VMEM((d, d), BF16), pltpu.VMEM((d, V7X_LANES), BF16),
                        pltpu.VMEM((d, V7X_LANES), BF16),
                        pltpu.VMEM((2, OUT_PROJ_STAGE_ROWS, d), F32),
                        pltpu.VMEM((tm, d), BF16), pltpu.VMEM((tm, d), BF16),
                        pltpu.SemaphoreType.DMA((2,))],
        compiler_params=_cparams(("arbitrary",)),
        name="out_proj_ln1",
    )(att, rec, x2d, wo, gt[:, None, :], sc[:, None, :], sh[:, None, :],
      g.reshape(1, d), b.reshape(1, d), rw_pad, rb.reshape(n_e, 1))


def _routing_kernel(lg_ref, tri_ref, low_ref, gate_ref, dest_ref, tile_ref, pad_ref, sel_scr, rk_scr,
                    *, blk, tm):
    n_e, t = lg_ref.shape
    eidx = lax.broadcasted_iota(jnp.int32, (n_e, t), 0)
    cur = lg_ref[...]
    vals, idxs = [], []
    for _ in range(TOP_K):
        m = jnp.max(cur, axis=0, keepdims=True)
        ik = jnp.min(jnp.where(cur == m, eidx, n_e), axis=0, keepdims=True)
        cur = jnp.where(eidx == ik, -jnp.inf, cur)
        vals.append(m)
        idxs.append(ik)
    es = [jnp.exp(v - vals[0]) for v in vals]
    den = functools.reduce(lambda a, b: a + b, es)
    for k in range(TOP_K):
        gate_ref[k:k + 1, :] = es[k] / den
    sel = functools.reduce(lambda a, b: a | b, [eidx == ik for ik in idxs])
    sel_scr[...] = jnp.where(sel, 1.0, 0.0)

    tri = tri_ref[...]
    carry = jnp.zeros((n_e, 1), F32)
    for j in range(t // blk):
        sb = sel_scr[:, j * blk:(j + 1) * blk]
        pre = jnp.dot(sb.astype(BF16), tri, preferred_element_type=F32)
        rk_scr[:, j * blk:(j + 1) * blk] = pre + carry
        carry = carry + jnp.sum(sb, axis=1, keepdims=True)

    counts = jnp.broadcast_to(carry, (n_e, V7X_LANES))
    padded = jnp.floor((counts + (tm - 1)) * (1.0 / tm)) * tm
    pends = jnp.dot(low_ref[...], padded, precision=lax.Precision.HIGHEST,
                    preferred_element_type=F32)
    pstarts = pends - padded
    row0 = rk_scr[...] + pstarts[:, 0:1]
    for k in range(TOP_K):
        dest_ref[k:k + 1, :] = jnp.sum(jnp.where(eidx == idxs[k], row0, 0.0), axis=0,
                                       keepdims=True).astype(jnp.int32)
    starts = (lax.broadcasted_iota(jnp.int32, (n_e, tile_ref.shape[1]), 1) * tm).astype(F32)
    owner = jnp.sum(jnp.where(pends[:, 0:1] <= starts, 1.0, 0.0), axis=0, keepdims=True)
    tile_ref[0:1, :] = jnp.minimum(owner, n_e - 1.0).astype(jnp.int32)
    tile_ref[1:2, :] = jnp.broadcast_to(jnp.max(pends[:, 0:1], axis=0, keepdims=True),
                                        (1, tile_ref.shape[1])).astype(jnp.int32)
    e_tile = lax.broadcasted_iota(jnp.int32, starts.shape, 0).astype(F32)
    row_end = jnp.sum(jnp.where(e_tile == owner, (pstarts + counts)[:, 0:1], 0.0), axis=0,
                      keepdims=True)
    tile_ref[2:3, :] = jnp.clip(row_end - starts[0:1, :], 0.0, float(tm)).astype(jnp.int32)
    pad_ref[0] = (pstarts + counts).astype(jnp.int32)
    pad_ref[1] = (padded - counts).astype(jnp.int32)


def _routing(logits_t, tm, n_tiles):
    n_e, t = logits_t.shape
    assert tm & (tm - 1) == 0
    blk = _pick(t, (256, 128))
    tri = jnp.asarray(np.triu(np.ones((blk, blk), np.float32), 1), BF16)
    low = jnp.asarray(np.tril(np.ones((n_e, n_e), np.float32)))
    ntp = -(-n_tiles // V7X_LANES) * V7X_LANES
    full = lambda shape: pl.BlockSpec(shape, lambda: (0,) * len(shape))
    return pl.pallas_call(
        functools.partial(_routing_kernel, blk=blk, tm=tm),
        in_specs=[full((n_e, t)), full((blk, blk)), full((n_e, n_e))],
        out_specs=[full((TOP_K, t)), full((TOP_K, t)), full((3, ntp)), full((2, n_e, V7X_LANES))],
        out_shape=[jax.ShapeDtypeStruct((TOP_K, t), F32),
                   jax.ShapeDtypeStruct((TOP_K, t), jnp.int32),
                   jax.ShapeDtypeStruct((3, ntp), jnp.int32),
                   jax.ShapeDtypeStruct((2, n_e, V7X_LANES), jnp.int32)],
        scratch_shapes=[pltpu.VMEM((n_e, t), F32), pltpu.VMEM((n_e, t), F32)],
        compiler_params=pltpu.CompilerParams(vmem_limit_bytes=V7X_VMEM_LIMIT),
        name="routing",
    )(logits_t, tri, low)


def _dispatch_kernel(dest_ref, pad_off_ref, pad_n_ref, nrow_ref, h_ref, o_ref, zero_ref, sem, zsem,
                     *, n_tok, tq, tm, n_e, nblk):
    step = pl.program_id(0)
    base = step * tq

    def rows(tok, n=1):
        return pl.ds(pl.multiple_of(tok * nblk, nblk), n * nblk)

    def fill(off, n):
        return pltpu.make_async_copy(zero_ref.at[rows(0, n)], o_ref.at[rows(off, n)], zsem)

    pieces = [1 << s for s in range(tm.bit_length() - 2, -1, -1)]

    def pad_rows(wait):
        def per_expert(e, c):
            off = pad_off_ref[e]
            n = pad_n_ref[e]
            for p in pieces:
                hit = (n & p) != 0

                @pl.when(hit)
                def _(off=off, p=p):
                    cp = fill(off, p)
                    cp.wait() if wait else cp.start()

                off = off + jnp.where(hit, p, 0)
            return c
        lax.fori_loop(0, n_e, per_expert, 0)

        def per_tile(i, c):
            cp = fill(i * tm, tm)
            cp.wait() if wait else cp.start()
            return c
        lax.fori_loop(nrow_ref[0] // tm, o_ref.shape[0] // (tm * nblk), per_tile, 0)

    @pl.when(step == 0)
    def _():
        zero_ref[...] = jnp.zeros_like(zero_ref)
        pad_rows(False)

    def start(r, c):
        for k in range(TOP_K):
            dst = o_ref.at[rows(dest_ref[k * n_tok + base + r])]
            pltpu.make_async_copy(h_ref.at[rows(r)], dst, sem).start(priority=k % 2)
        return c

    lax.fori_loop(0, tq, start, 0, unroll=2)
    for _ in range(TOP_K):
        pltpu.make_async_copy(h_ref, h_ref, sem).wait()

    @pl.when(step == 0)
    def _():
        pad_rows(True)


def _dispatch(h2p, dest_flat, pad_off, pad_n, n_used_rows, n_rows, tm, nblk):
    t = h2p.shape[0] // nblk
    n_e = pad_off.shape[0]
    tq = _pick(t, (256, 128))
    return pl.pallas_call(
        functools.partial(_dispatch_kernel, n_tok=t, tq=tq, tm=tm, n_e=n_e, nblk=nblk),
        grid_spec=pltpu.PrefetchScalarGridSpec(
            num_scalar_prefetch=4, grid=(t // tq,),
            in_specs=[pl.BlockSpec((tq * nblk, V7X_LANES), lambda i, *_: (i, 0))],
            out_specs=pl.BlockSpec(memory_space=pl.ANY),
            scratch_shapes=[pltpu.VMEM((tm * nblk, V7X_LANES), h2p.dtype),
                            pltpu.SemaphoreType.DMA(()), pltpu.SemaphoreType.DMA(())]),
        out_shape=jax.ShapeDtypeStruct((n_rows * nblk, V7X_LANES), h2p.dtype),
        compiler_params=pltpu.CompilerParams(dimension_semantics=("arbitrary",),
                                             has_side_effects=True,
                                             vmem_limit_bytes=V7X_VMEM_LIMIT),
        name="moe_dispatch",
    )(dest_flat, pad_off, pad_n, n_used_rows, h2p)


EXPERT_COL_BLOCK = 2048
GEMM1_COL_CHUNK = 256
MOE_TILE_LEVELS = 3
MOE_MIN_ROWS = 16
CAST_ROWS = 256


def _stream_expert_weights(te_ref, nt_ref, w_hbm, stage_ref, wbf_ref, sem):
    j, i = pl.program_id(0), pl.program_id(1)
    n_pass, n_tiles = pl.num_programs(0), pl.num_programs(1)
    tn = stage_ref.shape[1]
    n_used = nt_ref[0]
    cur = te_ref[i]

    def fetch(e, jj):
        src = w_hbm.at[e, :, pl.ds(pl.multiple_of(jj * tn, tn), tn)]
        return pltpu.make_async_copy(src, stage_ref, sem)

    @pl.when((i == 0) & (j == 0))
    def _():
        fetch(te_ref[0], 0).start()

    @pl.when((i < n_used) & ((i == 0) | (cur != te_ref[jnp.maximum(i - 1, 0)])))
    def _():
        fetch(cur, j).wait()

        def cast(r, c):
            rows = pl.ds(pl.multiple_of(r * CAST_ROWS, CAST_ROWS), CAST_ROWS)
            wbf_ref[rows, :] = stage_ref[rows, :].astype(BF16)
            return c
        lax.fori_loop(0, stage_ref.shape[0] // CAST_ROWS, cast, 0)

        def same_group(k):
            return (k < n_used) & (te_ref[jnp.minimum(k, n_tiles - 1)] == cur)
        nxt = lax.while_loop(same_group, lambda k: k + 1, i + 1)

        @pl.when(nxt < n_used)
        def _():
            fetch(te_ref[jnp.minimum(nxt, n_tiles - 1)], j).start()

        @pl.when((nxt >= n_used) & (j + 1 < n_pass))
        def _():
            fetch(te_ref[0], j + 1).start()


def _gemm1_kernel(te_ref, nt_ref, tv_ref, x_ref, w_hbm, b_ref, cmp_ref, o_ref, stage_ref, wbf_ref, sem,
                  *, nc, nblk):
    _stream_expert_weights(te_ref, nt_ref, w_hbm, stage_ref, wbf_ref, sem)

    def compute(rows):
        words = _load_token_rows(x_ref, rows, nblk)
        x_lo, x_hi = (v.astype(BF16) for v in _unpack_bf16_pair(words))
        half = x_lo.shape[1]
        cw = cmp_ref.shape[0]
        lane = lax.broadcasted_iota(jnp.int32, (rows, nc), 1)
        pieces = []
        for n0 in range(0, wbf_ref.shape[1], nc):
            hb = (jnp.dot(x_lo, wbf_ref[:half, n0:n0 + nc], preferred_element_type=F32)
                  + jnp.dot(x_hi, wbf_ref[half:, n0:n0 + nc], preferred_element_type=F32)
                  + b_ref[0, :, n0:n0 + nc])
            nxt = pltpu.roll(hb, nc - 1, axis=1)
            glu = jnp.minimum(hb, SWIGLU_LIMIT)
            lin = jnp.clip(nxt, -SWIGLU_LIMIT, SWIGLU_LIMIT)
            act = glu * jax.nn.sigmoid(SWIGLU_ALPHA * glu) * (lin + 1.0)
            act = jnp.where(lane % 2 == 0, act, 0.0).astype(BF16)
            pieces += [act[:, c * cw:(c + 1) * cw] for c in range(nc // cw)]
        packed = jnp.dot(jnp.concatenate(pieces, axis=0), cmp_ref[...],
                         preferred_element_type=F32).astype(o_ref.dtype)
        for p in range(len(pieces)):
            o_ref[:rows, p * (cw // 2):(p + 1) * (cw // 2)] = packed[p * rows:(p + 1) * rows]
        if rows < o_ref.shape[0]:
            o_ref[rows:, :] = jnp.zeros((o_ref.shape[0] - rows, o_ref.shape[1]), o_ref.dtype)

    _for_real_rows(pl.program_id(1), nt_ref, tv_ref, o_ref, o_ref.shape[0], compute)


def _for_real_rows(i, nt_ref, tv_ref, o_ref, tm, compute):
    used = i < nt_ref[0]
    real = tv_ref[i]
    sizes = [tm >> s for s in range(MOE_TILE_LEVELS) if (tm >> s) % MOE_MIN_ROWS == 0]
    for n, rows in enumerate(sizes):
        fits = real <= rows
        if n + 1 < len(sizes):
            fits = fits & (real > sizes[n + 1])

        @pl.when(used & fits)
        def _(rows=rows):
            compute(rows)

    @pl.when(jnp.logical_not(used))
    def _():
        o_ref[...] = jnp.zeros_like(o_ref)


def _gemm1(xin, w1, b1, tile_e, n_tiles_used, tile_rows, tm, nblk):
    n_rows = xin.shape[0] // nblk
    n_e, d, f2 = w1.shape
    tn = min(EXPERT_COL_BLOCK, f2)
    nc = min(GEMM1_COL_CHUNK, tn)
    assert f2 % tn == 0 and d % CAST_ROWS == 0
    n_tiles = n_rows // tm
    cw = V7X_MXU_DIM
    cmp_np = np.zeros((cw, cw // 2), np.float32)
    cmp_np[np.arange(0, cw, 2), np.arange(cw // 2)] = 1.0

    def used(i, nt):
        return jnp.minimum(i, nt[0] - 1)

    return pl.pallas_call(
        functools.partial(_gemm1_kernel, nc=nc, nblk=nblk),
        grid_spec=pltpu.PrefetchScalarGridSpec(
            num_scalar_prefetch=3, grid=(f2 // tn, n_tiles),
            in_specs=[pl.BlockSpec((tm * nblk, V7X_LANES),
                                   lambda j, i, te, nt, tv: (used(i, nt), 0)),
                      pl.BlockSpec(memory_space=pl.ANY),
                      pl.BlockSpec((1, 1, tn), lambda j, i, te, nt, tv: (te[used(i, nt)], 0, j)),
                      pl.BlockSpec((cw, cw // 2), lambda j, i, te, nt, tv: (0, 0))],
            out_specs=pl.BlockSpec((tm, tn // 2), lambda j, i, te, nt, tv: (i, j)),
            scratch_shapes=[pltpu.VMEM((d, tn), F32), pltpu.VMEM((d, tn), BF16),
                            pltpu.SemaphoreType.DMA(())]),
        out_shape=jax.ShapeDtypeStruct((n_rows, f2 // 2), BF16),
        compiler_params=_cparams(("arbitrary", "arbitrary")),
        name="moe_gemm1",
    )(tile_e, n_tiles_used, tile_rows, xin, w1, b1.reshape(n_e, 1, f2), jnp.asarray(cmp_np, BF16))


def _gemm2_kernel(te_ref, nt_ref, tv_ref, a_ref, w_hbm, b_ref, o_ref, stage_ref, wbf_ref, sem,
                  *, nc, nblk):
    _stream_expert_weights(te_ref, nt_ref, w_hbm, stage_ref, wbf_ref, sem)
    tm = a_ref.shape[0]

    def compute(rows):
        a = a_ref[:rows, :]
        half = wbf_ref.shape[1] // 2

        def cols(n0):
            return (jnp.dot(a, wbf_ref[:, n0:n0 + nc], preferred_element_type=F32)
                    + b_ref[0, :, n0:n0 + nc])

        for n0 in range(0, half, nc):
            words = _pack_bf16_pair(cols(n0), cols(half + n0))
            for c in range(nc // V7X_LANES):
                blk = n0 // V7X_LANES + c
                o_ref[pl.ds(blk, rows, stride=nblk), :] = words[:, c * V7X_LANES:(c + 1) * V7X_LANES]
        if rows < tm:
            o_ref[rows * nblk:, :] = jnp.zeros(((tm - rows) * nblk, o_ref.shape[1]), o_ref.dtype)

    _for_real_rows(pl.program_id(1), nt_ref, tv_ref, o_ref, tm, compute)


def _gemm2(act, w2, b2, tile_e, n_tiles_used, tile_rows, tm):
    n_rows, f = act.shape
    n_e, _, d = w2.shape
    tn = d
    nc = min(512, tn // 2)
    nblk = d // 2 // V7X_LANES
    assert f % CAST_ROWS == 0
    n_tiles = n_rows // tm

    def used(i, nt):
        return jnp.minimum(i, nt[0] - 1)

    return pl.pallas_call(
        functools.partial(_gemm2_kernel, nc=nc, nblk=nblk),
        grid_spec=pltpu.PrefetchScalarGridSpec(
            num_scalar_prefetch=3, grid=(d // tn, n_tiles),
            in_specs=[pl.BlockSpec((tm, f), lambda j, i, te, nt, tv: (used(i, nt), 0)),
                      pl.BlockSpec(memory_space=pl.ANY),
                      pl.BlockSpec((1, 1, tn), lambda j, i, te, nt, tv: (te[used(i, nt)], 0, j))],
            out_specs=pl.BlockSpec((tm * nblk, V7X_LANES), lambda j, i, te, nt, tv: (i, 0)),
            scratch_shapes=[pltpu.VMEM((f, tn), F32), pltpu.VMEM((f, tn), BF16),
                            pltpu.SemaphoreType.DMA(())]),
        out_shape=jax.ShapeDtypeStruct((n_rows * nblk, V7X_LANES), jnp.uint32),
        compiler_params=_cparams(("arbitrary", "arbitrary")),
        name="moe_gemm2",
    )(tile_e, n_tiles_used, tile_rows, act, w2, b2.reshape(n_e, 1, d))


def _combine_kernel(dest_ref, x1_ref, gate_ref, gt_ref, g_ref, b_ref, y_ref, o_ref, buf, sem,
                    *, alpha, tc, n_tok, nblk):
    i = pl.program_id(0)
    n_steps = pl.num_programs(0)

    def rows(tok):
        return pl.ds(pl.multiple_of(tok * nblk, nblk), nblk)

    def copy(step, slot, k, r):
        src = y_ref.at[rows(dest_ref[k * n_tok + step * tc + r])]
        return pltpu.make_async_copy(src, buf.at[slot, k, rows(r)], sem.at[slot])

    def issue(step, slot):
        def body(r, c):
            for k in range(TOP_K):
                copy(step, slot, k, r).start(priority=k % 2)
            return c
        lax.fori_loop(0, tc, body, 0, unroll=2)

    def drain(slot):
        pltpu.make_async_copy(buf.at[slot], buf.at[slot], sem.at[slot]).wait()

    slot = i % 2

    @pl.when(i == 0)
    def _():
        issue(0, 0)

    @pl.when(i + 1 < n_steps)
    def _():
        issue(i + 1, 1 - slot)

    drain(slot)
    gates = gate_ref[...]
    lo, hi = _unpack_bf16_pair(_load_token_rows(buf.at[slot, 0], tc, nblk))
    ff_lo, ff_hi = gates[:, 0:1] * lo, gates[:, 0:1] * hi
    for k in range(1, TOP_K):
        lo, hi = _unpack_bf16_pair(_load_token_rows(buf.at[slot, k], tc, nblk))
        ff_lo, ff_hi = ff_lo + gates[:, k:k + 1] * lo, ff_hi + gates[:, k:k + 1] * hi
    ff = jnp.concatenate([ff_lo, ff_hi], axis=1)
    y = alpha * x1_ref[...] + (1.0 + gt_ref[0]) * ff
    o_ref[...] = _layer_norm(y, g_ref[...], b_ref[...])


def _combine(dest_flat, x1, gates_tk, gt, g, b, y, seq, alpha):
    t, d = x1.shape
    tc = _pick(seq, (256, 128))
    per_b = seq // tc
    nblk = d // 2 // V7X_LANES
    return pl.pallas_call(
        functools.partial(_combine_kernel, alpha=alpha, tc=tc, n_tok=t, nblk=nblk),
        grid_spec=pltpu.PrefetchScalarGridSpec(
            num_scalar_prefetch=1, grid=(t // tc,),
            in_specs=[pl.BlockSpec((tc, d), lambda i, ds: (i, 0)),
                      pl.BlockSpec((tc, TOP_K), lambda i, ds: (i, 0)),
                      pl.BlockSpec((1, 1, d), lambda i, ds: (i // per_b, 0, 0)),
                      pl.BlockSpec((1, d), lambda i, ds: (0, 0)),
                      pl.BlockSpec((1, d), lambda i, ds: (0, 0)),
                      pl.BlockSpec(memory_space=pl.ANY)],
            out_specs=pl.BlockSpec((tc, d), lambda i, ds: (i, 0)),
            scratch_shapes=[pltpu.VMEM((2, TOP_K, tc * nblk, V7X_LANES), jnp.uint32),
                            pltpu.SemaphoreType.DMA((2,))]),
        out_shape=jax.ShapeDtypeStruct((t, d), F32),
        compiler_params=_cparams(("arbitrary",)),
        name="moe_combine_ln2",
    )(dest_flat, x1, gates_tk, gt[:, None, :], g.reshape(1, d), b.reshape(1, d), y)


MOE_ROW_TILE = 512


def _moe_ffn(h2p, logits_t, x1, gt_f, ln_g, ln_b, w1, b1, w2, b2, seq, alpha):
    tm = MOE_ROW_TILE
    n_e, t = logits_t.shape
    n_rows = -(-(TOP_K * t) // tm) * tm + n_e * tm
    n_tiles = n_rows // tm
    gate_t, dest, tiles, pads = _routing(logits_t, tm, n_tiles)
    dest = dest.reshape(-1)
    tile_e = tiles[0, :n_tiles]
    n_used_rows = tiles[1, :1]
    tile_rows = tiles[2, :n_tiles]
    n_used_tiles = n_used_rows // tm
    nblk = x1.shape[1] // 2 // V7X_LANES
    xin = _dispatch(h2p, dest, pads[0, :, 0], pads[1, :, 0], n_used_rows, n_rows, tm, nblk)
    act = _gemm1(xin, w1, b1, tile_e, n_used_tiles, tile_rows, tm, nblk)
    y = _gemm2(act, w2, b2, tile_e, n_used_tiles, tile_rows, tm)
    return _combine(dest, x1, gate_t.T, gt_f, ln_g, ln_b, y, seq, alpha)


def kernel(x, c, positions, w_ada, b_ada, w_in, hgrn_lb, gnorm_w, w_o, ln1_g, ln1_b,
           router_w, router_b, w1, b1, w2, b2, ln2_g, ln2_b):
    bsz, seq, d = x.shape
    depth = w_ada.shape[0]
    t = bsz * seq
    att_w = d // 2
    hg_w = d - att_w
    alpha = (2.0 * depth) ** 0.25

    lb_all = jnp.cumsum(jax.nn.softmax(hgrn_lb.astype(F32), axis=0), axis=0)
    inv = ROPE_THETA ** (-(jnp.arange(0, ROT_DIM, 2, dtype=F32) / ROT_DIM))
    lane = np.arange(V7X_LANES)
    inv_lane = inv[(lane % ATT_HEAD_DIM) % (ROT_DIM // 2)].reshape(1, V7X_LANES)
    pos_b = jnp.broadcast_to(positions.astype(F32).reshape(t, 1), (t, V7X_LANES))

    x2d = x.reshape(t, d)
    for l in range(depth):
        mod = _adaln(c, w_ada[l], b_ada[l])
        sh_a, sc_a, gt_a, sh_f, sc_f, gt_f = jnp.split(mod, 6, axis=-1)

        proj = _in_proj(x2d, sc_a, sh_a, w_in[l], seq)
        proj3 = proj.reshape(bsz, seq, proj.shape[1])
        q_hp, k_hp, v_hp = _qk_rope(proj, pos_b, inv_lane, bsz, seq, att_w)
        att = _attention(q_hp, k_hp, v_hp)
        rec = _hgrn2(proj3, lb_all[l], gnorm_w[l], att_w, hg_w).reshape(t, hg_w)

        x1, h2p, logits_t = _out_proj(att, rec, x2d, w_o[l], gt_a, sc_f, sh_f, ln1_g[l], ln1_b[l],
                                      router_w[l], router_b[l], seq, alpha)

        x2d = _moe_ffn(h2p, logits_t, x1, gt_f, ln2_g[l], ln2_b[l], w1[l], b1[l], w2[l], b2[l],
                       seq, alpha)
    return x2d.reshape(bsz, seq, d)
```

```python
import functools

import numpy as np
import jax
import jax.numpy as jnp
from jax import lax
from jax.experimental import pallas as pl
from jax.experimental.pallas import tpu as pltpu

F32 = jnp.float32
BF16 = jnp.bfloat16

V7X_LANES = 128
V7X_MXU_DIM = 256
V7X_VMEM_LIMIT = 56 * 1024 * 1024

ATT_HEAD_DIM = 64
DILATED_PAIRS = ((128, 1), (512, 4), (2048, 16))
ATT_BLOCK = 128
ROT_DIM = ATT_HEAD_DIM // 4
ROPE_THETA = 500000.0
HG_EXPAND = 128
HG_CHUNK = 64
HG_SUB = 16
TOP_K = 4
SWIGLU_ALPHA = 1.702
SWIGLU_LIMIT = 7.0
LN_EPS = 1e-5
RMS_EPS = 1e-6
NEG_INF = -1e30

NT_DIMS = (((1,), (1,)), ((), ()))


def _pick(n, candidates):
    for c in candidates:
        if n % c == 0:
            return c
    raise ValueError(f"no tile in {candidates} divides {n}")


def _cparams(sem, vmem=V7X_VMEM_LIMIT, flags=None):
    return pltpu.CompilerParams(dimension_semantics=sem, vmem_limit_bytes=vmem, flags=flags)


def _adaln_kernel(c_ref, w_ref, b_ref, o_ref):
    c = c_ref[...]
    s = c * jax.nn.sigmoid(c)
    o_ref[...] = jnp.dot(s.astype(BF16), w_ref[...].astype(BF16),
                         preferred_element_type=F32) + b_ref[...]


def _adaln(c, w, b):
    bsz, d = c.shape
    n = w.shape[1]
    rows = 8
    cp = jnp.zeros((rows, d), F32).at[:bsz].set(c)
    tn = _pick(n, (1024, 512, 256, 128))
    out = pl.pallas_call(
        _adaln_kernel,
        grid=(n // tn,),
        in_specs=[pl.BlockSpec((rows, d), lambda j: (0, 0)),
                  pl.BlockSpec((d, tn), lambda j: (0, j)),
                  pl.BlockSpec((1, tn), lambda j: (0, j))],
        out_specs=pl.BlockSpec((rows, tn), lambda j: (0, j)),
        out_shape=jax.ShapeDtypeStruct((rows, n), F32),
        compiler_params=_cparams(("arbitrary",)),
        name="adaln",
    )(cp, w, b.reshape(1, n))
    return out[:bsz]


def _in_proj_kernel(x_ref, sc_ref, sh_ref, w_ref, o_ref, wbf_ref):
    @pl.when(pl.program_id(1) == 0)
    def _():
        wbf_ref[...] = w_ref[...].astype(BF16)

    h = x_ref[...] * (1.0 + sc_ref[0]) + sh_ref[0]
    o_ref[...] = jnp.dot(h.astype(BF16), wbf_ref[...],
                         preferred_element_type=F32).astype(o_ref.dtype)


def _in_proj(x2d, sc, sh, w, seq):
    t, d = x2d.shape
    nc = w.shape[1]
    tm = _pick(seq, (512, 256, 128))
    tn = _pick(nc, (1024, 512, 256, 128))
    per_b = seq // tm
    vec = pl.BlockSpec((1, 1, d), lambda j, i: (i // per_b, 0, 0))
    return pl.pallas_call(
        _in_proj_kernel,
        grid=(nc // tn, t // tm),
        in_specs=[pl.BlockSpec((tm, d), lambda j, i: (i, 0)), vec, vec,
                  pl.BlockSpec((d, tn), lambda j, i: (0, j))],
        out_specs=pl.BlockSpec((tm, tn), lambda j, i: (i, j)),
        out_shape=jax.ShapeDtypeStruct((t, nc), BF16),
        scratch_shapes=[pltpu.VMEM((d, tn), BF16)],
        compiler_params=_cparams(("arbitrary", "arbitrary")),
        name="in_proj",
    )(x2d, sc[:, None, :], sh[:, None, :], w)


def _rope_kernel(q_ref, k_ref, v_ref, pos_ref, inv_ref, qo_ref, ko_ref, vo_ref):
    tm = pos_ref.shape[0]
    lane = lax.broadcasted_iota(jnp.int32, (tm, V7X_LANES), 1)
    lh = lane % ATT_HEAD_DIM
    half = ROT_DIM // 2
    ang = pos_ref[...] * inv_ref[...]
    cs = jnp.where(lh < ROT_DIM, jnp.cos(ang), 1.0)
    sn = jnp.sin(ang)
    sn = jnp.where(lh < half, -sn, jnp.where(lh < ROT_DIM, sn, 0.0))

    def rope(t):
        swapped = jnp.where(lh < half,
                            pltpu.roll(t, V7X_LANES - half, axis=1),
                            pltpu.roll(t, half, axis=1))
        return t * cs + swapped * sn

    for h in range(qo_ref.shape[1]):
        lanes = slice(h * V7X_LANES, (h + 1) * V7X_LANES)
        qo_ref[0, h] = rope(q_ref[:, lanes].astype(F32)) * (ATT_HEAD_DIM ** -0.5)
        ko_ref[0, h] = rope(k_ref[:, lanes].astype(F32))
        vo_ref[0, h] = v_ref[:, lanes].astype(F32)


def _qk_rope(proj, pos_b, inv_lane, bsz, seq, att_w):
    t = proj.shape[0]
    hp = att_w // V7X_LANES
    tm = _pick(seq, (256, 128))
    per_b = seq // tm
    out_spec = pl.BlockSpec((1, hp, tm, V7X_LANES), lambda i: (i // per_b, 0, i % per_b, 0))
    shp = jax.ShapeDtypeStruct((bsz, hp, seq, V7X_LANES), F32)
    return pl.pallas_call(
        _rope_kernel,
        grid=(t // tm,),
        in_specs=[pl.BlockSpec((tm, att_w), lambda i: (i, 0)),
                  pl.BlockSpec((tm, att_w), lambda i: (i, 1)),
                  pl.BlockSpec((tm, att_w), lambda i: (i, 2)),
                  pl.BlockSpec((tm, V7X_LANES), lambda i: (i, 0)),
                  pl.BlockSpec((1, V7X_LANES), lambda i: (0, 0))],
        out_specs=[out_spec, out_spec, out_spec],
        out_shape=[shp, shp, shp],
        compiler_params=_cparams(("arbitrary",)),
        name="qk_rope",
    )(proj, proj, proj, pos_b, inv_lane)


ATT_TASKS_PER_STEP = 8


def _attn_kernel(q_in, k_in, v_in, out_ref, qs, ks, vs, sm, no_lo, nl_lo, no_s, nl_s, nat, *, seq):
    qn = ATT_BLOCK
    n_task = seq // qn
    n_head = V7X_LANES // ATT_HEAD_DIM
    gc = min(ATT_TASKS_PER_STEP, n_task)
    (_, d_lo), (_, dm), (_, d_hi) = DILATED_PAIRS
    g_hi = d_hi // dm
    rows_m = seq // dm
    lane = lax.broadcasted_iota(jnp.int32, (qn, V7X_LANES), 1)
    qi = lax.broadcasted_iota(jnp.int32, (gc, qn, 2 * qn), 1)
    kj = lax.broadcasted_iota(jnp.int32, (gc, qn, 2 * qn), 2)
    dist = qi + qn - kj
    band = (dist >= 0) & (dist <= qn)
    zeros = jnp.zeros((qn, V7X_LANES), BF16)
    srcs = (q_in, k_in, v_in)

    for a, src in enumerate(srcs):
        for r in range(dm):
            sm[a, r] = src[0, 0, pl.ds(r, rows_m, stride=dm), :]

    def task_index(d, r, n):
        if d == d_lo:
            return None, slice(n * qn, (n + 1) * qn)
        if d == dm:
            return r, slice(n * qn, (n + 1) * qn)
        return r % dm, pl.ds(r // dm + n * qn * g_hi, qn, stride=g_hi)

    for bi, (window, d) in enumerate(DILATED_PAIRS):
        nb = seq // (d * qn)
        tasks = [(r, n) for r in range(d) for n in range(nb)]
        for g, (r, n) in enumerate(tasks):
            sel, rows = task_index(d, r, n)

            def operand(a, sel=sel, rows=rows):
                return srcs[a][0, 0, rows, :] if sel is None else sm[a, sel, rows, :]

            qv = operand(0)
            for h in range(n_head):
                qs[h, g] = jnp.where(lane // ATT_HEAD_DIM == h, qv, 0.0).astype(BF16)
            for a, dst in ((1, ks), (2, vs)):
                blk = operand(a).astype(BF16)
                dst[g, qn:, :] = blk
                if n + 1 < nb:
                    dst[g + 1, :qn, :] = blk
                if n == 0:
                    dst[g, :qn, :] = zeros

        for g0 in range(0, n_task, gc):
            gsl = slice(g0, g0 + gc)
            gidx = g0 + lax.broadcasted_iota(jnp.int32, (gc, qn, 2 * qn), 0)
            valid = band & ((gidx % nb != 0) | (kj >= qn))
            k = ks[gsl]
            v = vs[gsl]
            o = None
            for h in range(n_head):
                s = jnp.einsum("gqd,gkd->gqk", qs[h, gsl], k, preferred_element_type=F32)
                s = jnp.where(valid, s, NEG_INF)
                m = jnp.max(s, axis=-1, keepdims=True)
                p = jnp.exp(s - m)
                den = jnp.sum(p, axis=-1, keepdims=True)
                oh = jnp.einsum("gqk,gkd->gqd", p.astype(BF16), v,
                                preferred_element_type=F32) / den
                lh = jnp.broadcast_to(m + jnp.log(den), oh.shape)
                if o is None:
                    o, lse = oh, lh
                else:
                    in_head = lane[None] // ATT_HEAD_DIM == h
                    o, lse = jnp.where(in_head, oh, o), jnp.where(in_head, lh, lse)
            for t in range(gc):
                sel, rows = task_index(d, *tasks[g0 + t])
                if sel is None:
                    no_lo[rows, :] = o[t]
                    nl_lo[rows, :] = lse[t]
                else:
                    no_s[bi - 1, sel, rows, :] = o[t]
                    nl_s[bi - 1, sel, rows, :] = lse[t]

    for r in range(dm):
        for c in range(rows_m // qn):
            rows = slice(c * qn, (c + 1) * qn)
            seq_rows = pl.ds(r + dm * c * qn, qn, stride=dm)
            ls = [nl_lo[seq_rows, :], nl_s[0, r, rows, :], nl_s[1, r, rows, :]]
            os_ = [no_lo[seq_rows, :], no_s[0, r, rows, :], no_s[1, r, rows, :]]
            mx = functools.reduce(jnp.maximum, ls)
            ws = [jnp.exp(l - mx) for l in ls]
            tot = functools.reduce(lambda x, y: x + y, ws)
            acc = functools.reduce(lambda x, y: x + y, [w * o for w, o in zip(ws, os_)])
            nat[seq_rows, :] = acc / tot
    out_ref[0, 0] = nat[...].astype(out_ref.dtype)


def _attention(q_hp, k_hp, v_hp):
    bsz, hp, seq, _ = q_hp.shape
    (_, d_lo), (_, dm), (_, d_hi) = DILATED_PAIRS
    assert d_lo == 1 and d_hi % dm == 0
    for window, d in DILATED_PAIRS:
        assert window // d == ATT_BLOCK and seq % (d * ATT_BLOCK) == 0
    n_task = seq // ATT_BLOCK
    assert n_task % min(ATT_TASKS_PER_STEP, n_task) == 0
    n_head = V7X_LANES // ATT_HEAD_DIM
    streams = (dm, seq // dm, V7X_LANES)
    spec = pl.BlockSpec((1, 1, seq, V7X_LANES), lambda b, h: (b, h, 0, 0))
    return pl.pallas_call(
        functools.partial(_attn_kernel, seq=seq),
        grid=(bsz, hp),
        in_specs=[spec, spec, spec],
        out_specs=spec,
        out_shape=jax.ShapeDtypeStruct((bsz, hp, seq, V7X_LANES), BF16),
        scratch_shapes=[pltpu.VMEM((n_head, n_task, ATT_BLOCK, V7X_LANES), BF16),
                        pltpu.VMEM((n_task, 2 * ATT_BLOCK, V7X_LANES), BF16),
                        pltpu.VMEM((n_task, 2 * ATT_BLOCK, V7X_LANES), BF16),
                        pltpu.VMEM((3,) + streams, F32),
                        pltpu.VMEM((seq, V7X_LANES), F32), pltpu.VMEM((seq, V7X_LANES), F32),
                        pltpu.VMEM((2,) + streams, F32), pltpu.VMEM((2,) + streams, F32),
                        pltpu.VMEM((seq, V7X_LANES), F32)],
        compiler_params=_cparams(("arbitrary", "arbitrary")),
        name="dilated_attn",
    )(q_hp, k_hp, v_hp)


def _hgrn_kernel(q_ref, f_ref, i_ref, g_ref, lb_ref, gw_ref, sums_ref, o_ref,
                 st_ref, hl_ref, kk_ref, cum_ref, *, heads, ts):
    @pl.when(pl.program_id(2) == 0)
    def _():
        st_ref[...] = jnp.zeros_like(st_ref)

    c_len = HG_CHUNK
    lanes = heads * HG_EXPAND
    lb = lb_ref[...]
    for c in range(ts // c_len):
        rows = slice(c * c_len, (c + 1) * c_len)
        f = lb + (1.0 - lb) * jax.nn.sigmoid(f_ref[0, rows, :].astype(F32))
        kk_ref[rows, :] = 1.0 - f
        hi, lo = _bf16_split(jnp.log(f))
        hl_ref[rows, :lanes] = hi
        hl_ref[rows, lanes:] = lo
    for k in range(3):
        both = jnp.dot(sums_ref[k], hl_ref[...], preferred_element_type=F32)
        cum_ref[k] = both[:, :lanes] + both[:, lanes:]

    n_sub = c_len // HG_SUB
    gi = lax.broadcasted_iota(jnp.int32, (heads * n_sub, HG_SUB, c_len), 0) % n_sub
    qi = lax.broadcasted_iota(jnp.int32, (heads * n_sub, HG_SUB, c_len), 1)
    si = lax.broadcasted_iota(jnp.int32, (heads * n_sub, HG_SUB, c_len), 2)
    causal = si <= gi * HG_SUB + qi
    key_row = lax.broadcasted_iota(jnp.int32, (c_len, HG_EXPAND), 0)

    for c in range(ts // c_len):
        rows = slice(c * c_len, (c + 1) * c_len)
        qts, kts, vbs = [], [], []
        for h in range(heads):
            lsl = slice(h * HG_EXPAND, (h + 1) * HG_EXPAND)
            b = cum_ref[0, rows, lsl]
            anchor = cum_ref[1, rows, lsl]
            kk = kk_ref[rows, lsl]
            qt = (q_ref[0, rows, lsl].astype(F32) * jnp.exp(b - anchor)).astype(BF16)
            qts.append(qt.reshape(n_sub, HG_SUB, HG_EXPAND))
            vb = i_ref[0, rows, lsl].astype(BF16)
            for i in range(n_sub):
                hi_r = (i + 1) * HG_SUB
                kt = kk * jnp.exp(anchor[i * HG_SUB:i * HG_SUB + 1, :] - b)
                kts.append(jnp.where(key_row < hi_r, kt, 0.0).astype(BF16))
                vbs.append(vb)
        a = jnp.einsum("gqk,gsk->gqs", jnp.concatenate(qts, axis=0), jnp.stack(kts),
                       preferred_element_type=F32)
        a = jnp.where(causal, a, 0.0).astype(BF16)
        o_intra = jnp.einsum("gqs,gsv->gqv", a, jnp.stack(vbs), preferred_element_type=F32)

        for h in range(heads):
            lsl = slice(h * HG_EXPAND, (h + 1) * HG_EXPAND)
            b = cum_ref[0, rows, lsl]
            b_last = cum_ref[2, rows, lsl]
            kk = kk_ref[rows, lsl]
            q = q_ref[0, rows, lsl].astype(F32)
            v = i_ref[0, rows, lsl].astype(F32)
            st = st_ref[h]
            o_inter = lax.dot_general((q * jnp.exp(b)).astype(BF16), st.astype(BF16), NT_DIMS,
                                      preferred_element_type=F32)
            o = o_inter + o_intra[h * n_sub:(h + 1) * n_sub].reshape(c_len, HG_EXPAND)
            kl = kk * jnp.exp(b_last - b)
            upd = jnp.dot(v.T.astype(BF16), kl.astype(BF16), preferred_element_type=F32)
            st_ref[h] = st * jnp.exp(b_last[0:1, :]) + upd
            o = o * lax.rsqrt(jnp.mean(o * o, axis=-1, keepdims=True) + RMS_EPS)
            g = g_ref[0, rows, lsl].astype(F32)
            o = o * gw_ref[:, lsl] * (g * jax.nn.sigmoid(g))
            o_ref[0, rows, lsl] = o.astype(o_ref.dtype)


def _hgrn_sum_matrices(ts):
    t = np.arange(ts)[:, None]
    s = np.arange(ts)[None, :]
    same = (t // HG_CHUNK) == (s // HG_CHUNK)
    mid = (t // HG_SUB) * HG_SUB + HG_SUB // 2
    return np.stack([same & (s <= t), same & (s <= mid), same]).astype(np.float32)


def _hgrn2(proj3, lb, gw, att_w, hg_w):
    bsz, seq, _ = proj3.shape
    n_heads = hg_w // HG_EXPAND
    heads = _pick(n_heads, (4, 2, 1))
    lanes = heads * HG_EXPAND
    ts = _pick(seq, (256, 128, 64))
    base = 3 * att_w
    assert base % lanes == 0 and hg_w % lanes == 0

    def seg(k):
        off = (base + k * hg_w) // lanes
        return pl.BlockSpec((1, ts, lanes), lambda b, h, s: (b, s, off + h))

    vec = pl.BlockSpec((1, lanes), lambda b, h, s: (0, h))
    sums = jnp.asarray(_hgrn_sum_matrices(ts), BF16)
    return pl.pallas_call(
        functools.partial(_hgrn_kernel, heads=heads, ts=ts),
        grid=(bsz, hg_w // lanes, seq // ts),
        in_specs=[seg(0), seg(1), seg(2), seg(3), vec, vec,
                  pl.BlockSpec((3, ts, ts), lambda b, h, s: (0, 0, 0))],
        out_specs=pl.BlockSpec((1, ts, lanes), lambda b, h, s: (b, s, h)),
        out_shape=jax.ShapeDtypeStruct((bsz, seq, hg_w), BF16),
        scratch_shapes=[pltpu.VMEM((heads, HG_EXPAND, HG_EXPAND), F32),
                        pltpu.VMEM((ts, 2 * lanes), BF16),
                        pltpu.VMEM((ts, lanes), F32),
                        pltpu.VMEM((3, ts, lanes), F32)],
        compiler_params=_cparams(("arbitrary", "arbitrary", "arbitrary")),
        name="hgrn2",
    )(proj3, proj3, proj3, proj3, lb.reshape(1, hg_w), gw.reshape(1, hg_w), sums)


def _layer_norm(y, g, b):
    mu = jnp.mean(y, axis=-1, keepdims=True)
    yc = y - mu
    var = jnp.mean(yc * yc, axis=-1, keepdims=True)
    return yc * lax.rsqrt(var + LN_EPS) * g + b


def _bf16_split(x):
    hi = x.astype(BF16)
    lo = (x - hi.astype(F32)).astype(BF16)
    return hi, lo


def _pack_bf16_pair(lo, hi):
    lo_bits = lax.bitcast_convert_type(lo.astype(BF16).astype(F32), jnp.uint32)
    hi_bits = lax.bitcast_convert_type(hi.astype(BF16).astype(F32), jnp.uint32)
    return (lo_bits >> 16) | (hi_bits & jnp.uint32(0xFFFF0000))


def _store_token_rows(ref, tok0, words):
    n, width = words.shape
    nblk = width // V7X_LANES
    for c in range(nblk):
        ref[pl.ds(tok0 * nblk + c, n, stride=nblk), :] = words[:, c * V7X_LANES:(c + 1) * V7X_LANES]


def _load_token_rows(ref, n, nblk):
    return jnp.concatenate([ref[pl.ds(c, n, stride=nblk), :] for c in range(nblk)], axis=1)


def _unpack_bf16_pair(words):
    lo = lax.bitcast_convert_type(words << 16, F32)
    hi = lax.bitcast_convert_type(words & jnp.uint32(0xFFFF0000), F32)
    return lo, hi


def _load_weight_bf16(w_hbm, wbf_ref, stage_ref, sem):
    rows = stage_ref.shape[1]
    n_slab = w_hbm.shape[0] // rows

    def fetch(s, slot):
        return pltpu.make_async_copy(w_hbm.at[pl.ds(s * rows, rows)], stage_ref.at[slot],
                                     sem.at[slot])

    fetch(0, 0).start()
    for s in range(n_slab):
        if s + 1 < n_slab:
            fetch(s + 1, (s + 1) % 2).start()
        fetch(s, s % 2).wait()
        wbf_ref[s * rows:(s + 1) * rows, :] = stage_ref[s % 2].astype(BF16)


def _out_proj_kernel(att_ref, rec_ref, x_ref, wo_hbm, gt_ref, sc_ref, sh_ref, g_ref, b_ref,
                     rw_ref, rb_ref, x1_ref, h2_ref, lg_ref, wobf_ref, rwhi_ref, rwlo_ref,
                     stage_ref, hi_ref, lo_ref, sem, *, alpha, att_w):
    @pl.when(pl.program_id(0) == 0)
    def _():
        _load_weight_bf16(wo_hbm, wobf_ref, stage_ref, sem)
        hi, lo = _bf16_split(rw_ref[...])
        rwhi_ref[...] = hi
        rwlo_ref[...] = lo

    att = jnp.concatenate([att_ref[0, h] for h in range(att_ref.shape[1])], axis=1)
    mix = (jnp.dot(att, wobf_ref[:att_w, :], preferred_element_type=F32)
           + jnp.dot(rec_ref[...], wobf_ref[att_w:, :], preferred_element_type=F32))
    half = x_ref.shape[1] // 2

    for r in range(x_ref.shape[0] // LN_ROW_GROUP):
        rows = slice(r * LN_ROW_GROUP, (r + 1) * LN_ROW_GROUP)
        y = alpha * x_ref[rows, :] + (1.0 + gt_ref[0]) * mix[rows]
        x1 = _layer_norm(y, g_ref[...], b_ref[...])
        x1_ref[rows, :] = x1
        h2 = x1 * (1.0 + sc_ref[0]) + sh_ref[0]
        hi, lo = _bf16_split(h2)
        hi_ref[rows, :] = hi
        lo_ref[rows, :] = lo
        _store_token_rows(h2_ref, r * LN_ROW_GROUP, _pack_bf16_pair(h2[:, :half], h2[:, half:]))

    lg = (jnp.dot(hi_ref[...], rwhi_ref[...], preferred_element_type=F32)
          + jnp.dot(lo_ref[...], rwhi_ref[...], preferred_element_type=F32)
          + jnp.dot(hi_ref[...], rwlo_ref[...], preferred_element_type=F32))
    lg_ref[...] = lg.T[:lg_ref.shape[0], :] + rb_ref[...]


OUT_PROJ_STAGE_ROWS = 256
LN_ROW_GROUP = 16


def _out_proj(att, rec, x2d, wo, gt, sc, sh, g, b, rw, rb, seq, alpha):
    t, d = x2d.shape
    hp = att.shape[1]
    att_w = hp * V7X_LANES
    n_e = rw.shape[1]
    assert n_e <= V7X_LANES and d % OUT_PROJ_STAGE_ROWS == 0
    rw_pad = jnp.zeros((d, V7X_LANES), F32).at[:, :n_e].set(rw)
    nblk = d // 2 // V7X_LANES
    tm = _pick(seq, (512, 256, 128))
    per_b = seq // tm
    vec3 = pl.BlockSpec((1, 1, d), lambda i: (i // per_b, 0, 0))
    full = lambda shape: pl.BlockSpec(shape, lambda i: (0,) * len(shape))
    return pl.pallas_call(
        functools.partial(_out_proj_kernel, alpha=alpha, att_w=att_w),
        grid=(t // tm,),
        in_specs=[pl.BlockSpec((1, hp, tm, V7X_LANES), lambda i: (i // per_b, 0, i % per_b, 0)),
                  pl.BlockSpec((tm, d - att_w), lambda i: (i, 0)),
                  pl.BlockSpec((tm, d), lambda i: (i, 0)),
                  pl.BlockSpec(memory_space=pl.ANY), vec3, vec3, vec3, full((1, d)), full((1, d)),
                  full((d, V7X_LANES)), full((n_e, 1))],
        out_specs=[pl.BlockSpec((tm, d), lambda i: (i, 0)),
                   pl.BlockSpec((tm * nblk, V7X_LANES), lambda i: (i, 0)),
                   pl.BlockSpec((n_e, tm), lambda i: (0, i))],
        out_shape=[jax.ShapeDtypeStruct((t, d), F32),
                   jax.ShapeDtypeStruct((t * nblk, V7X_LANES), jnp.uint32),
                   jax.ShapeDtypeStruct((n_e, t), F32)],
        scratch_shapes=[pltpu.VMEM((d, d), BF16), pltpu.VMEM((d, V7X_LANES), BF16),
                        pltpu.VMEM((d, V7X_LANES), BF16),
                        pltpu.VMEM((2, OUT_PROJ_STAGE_ROWS, d), F32),
                        pltpu.VMEM((tm, d), BF16), pltpu.VMEM((tm, d), BF16),
                        pltpu.SemaphoreType.DMA((2,))],
        compiler_params=_cparams(("arbitrary",)),
        name="out_proj_ln1",
    )(att, rec, x2d, wo, gt[:, None, :], sc[:, None, :], sh[:, None, :],
      g.reshape(1, d), b.reshape(1, d), rw_pad, rb.reshape(n_e, 1))


def _routing_kernel(lg_ref, tri_ref, low_ref, gate_ref, dest_ref, tile_ref, pad_ref, sel_scr, rk_scr,
                    *, blk, tm):
    n_e, t = lg_ref.shape
    eidx = lax.broadcasted_iota(jnp.int32, (n_e, t), 0)
    cur = lg_ref[...]
    vals, idxs = [], []
    for _ in range(TOP_K):
        m = jnp.max(cur, axis=0, keepdims=True)
        ik = jnp.min(jnp.where(cur == m, eidx, n_e), axis=0, keepdims=True)
        cur = jnp.where(eidx == ik, -jnp.inf, cur)
        vals.append(m)
        idxs.append(ik)
    es = [jnp.exp(v - vals[0]) for v in vals]
    den = functools.reduce(lambda a, b: a + b, es)
    for k in range(TOP_K):
        gate_ref[k:k + 1, :] = es[k] / den
    sel = functools.reduce(lambda a, b: a | b, [eidx == ik for ik in idxs])
    sel_scr[...] = jnp.where(sel, 1.0, 0.0)

    tri = tri_ref[...]
    carry = jnp.zeros((n_e, 1), F32)
    for j in range(t // blk):
        sb = sel_scr[:, j * blk:(j + 1) * blk]
        pre = jnp.dot(sb.astype(BF16), tri, preferred_element_type=F32)
        rk_scr[:, j * blk:(j + 1) * blk] = pre + carry
        carry = carry + jnp.sum(sb, axis=1, keepdims=True)

    counts = jnp.broadcast_to(carry, (n_e, V7X_LANES))
    padded = jnp.floor((counts + (tm - 1)) * (1.0 / tm)) * tm
    pends = jnp.dot(low_ref[...], padded, precision=lax.Precision.HIGHEST,
                    preferred_element_type=F32)
    pstarts = pends - padded
    row0 = rk_scr[...] + pstarts[:, 0:1]
    for k in range(TOP_K):
        dest_ref[k:k + 1, :] = jnp.sum(jnp.where(eidx == idxs[k], row0, 0.0), axis=0,
                                       keepdims=True).astype(jnp.int32)
    starts = (lax.broadcasted_iota(jnp.int32, (n_e, tile_ref.shape[1]), 1) * tm).astype(F32)
    owner = jnp.sum(jnp.where(pends[:, 0:1] <= starts, 1.0, 0.0), axis=0, keepdims=True)
    tile_ref[0:1, :] = jnp.minimum(owner, n_e - 1.0).astype(jnp.int32)
    tile_ref[1:2, :] = jnp.broadcast_to(jnp.max(pends[:, 0:1], axis=0, keepdims=True),
                                        (1, tile_ref.shape[1])).astype(jnp.int32)
    e_tile = lax.broadcasted_iota(jnp.int32, starts.shape, 0).astype(F32)
    row_end = jnp.sum(jnp.where(e_tile == owner, (pstarts + counts)[:, 0:1], 0.0), axis=0,
                      keepdims=True)
    tile_ref[2:3, :] = jnp.clip(row_end - starts[0:1, :], 0.0, float(tm)).astype(jnp.int32)
    pad_ref[0] = (pstarts + counts).astype(jnp.int32)
    pad_ref[1] = (padded - counts).astype(jnp.int32)


def _routing(logits_t, tm, n_tiles):
    n_e, t = logits_t.shape
    assert tm & (tm - 1) == 0
    blk = _pick(t, (256, 128))
    tri = jnp.asarray(np.triu(np.ones((blk, blk), np.float32), 1), BF16)
    low = jnp.asarray(np.tril(np.ones((n_e, n_e), np.float32)))
    ntp = -(-n_tiles // V7X_LANES) * V7X_LANES
    full = lambda shape: pl.BlockSpec(shape, lambda: (0,) * len(shape))
    return pl.pallas_call(
        functools.partial(_routing_kernel, blk=blk, tm=tm),
        in_specs=[full((n_e, t)), full((blk, blk)), full((n_e, n_e))],
        out_specs=[full((TOP_K, t)), full((TOP_K, t)), full((3, ntp)), full((2, n_e, V7X_LANES))],
        out_shape=[jax.ShapeDtypeStruct((TOP_K, t), F32),
                   jax.ShapeDtypeStruct((TOP_K, t), jnp.int32),
                   jax.ShapeDtypeStruct((3, ntp), jnp.int32),
                   jax.ShapeDtypeStruct((2, n_e, V7X_LANES), jnp.int32)],
        scratch_shapes=[pltpu.VMEM((n_e, t), F32), pltpu.VMEM((n_e, t), F32)],
        compiler_params=pltpu.CompilerParams(vmem_limit_bytes=V7X_VMEM_LIMIT),
        name="routing",
    )(logits_t, tri, low)


def _dispatch_kernel(dest_ref, pad_off_ref, pad_n_ref, nrow_ref, h_ref, o_ref, zero_ref, sem, zsem,
                     *, n_tok, tq, tm, n_e, nblk):
    step = pl.program_id(0)
    base = step * tq

    def rows(tok, n=1):
        return pl.ds(pl.multiple_of(tok * nblk, nblk), n * nblk)

    def fill(off, n):
        return pltpu.make_async_copy(zero_ref.at[rows(0, n)], o_ref.at[rows(off, n)], zsem)

    pieces = [1 << s for s in range(tm.bit_length() - 2, -1, -1)]

    def pad_rows(wait):
        def per_expert(e, c):
            off = pad_off_ref[e]
            n = pad_n_ref[e]
            for p in pieces:
                hit = (n & p) != 0

                @pl.when(hit)
                def _(off=off, p=p):
                    cp = fill(off, p)
                    cp.wait() if wait else cp.start()

                off = off + jnp.where(hit, p, 0)
            return c
        lax.fori_loop(0, n_e, per_expert, 0)

        def per_tile(i, c):
            cp = fill(i * tm, tm)
            cp.wait() if wait else cp.start()
            return c
        lax.fori_loop(nrow_ref[0] // tm, o_ref.shape[0] // (tm * nblk), per_tile, 0)

    @pl.when(step == 0)
    def _():
        zero_ref[...] = jnp.zeros_like(zero_ref)
        pad_rows(False)

    def start(r, c):
        for k in range(TOP_K):
            dst = o_ref.at[rows(dest_ref[k * n_tok + base + r])]
            pltpu.make_async_copy(h_ref.at[rows(r)], dst, sem).start(priority=k % 2)
        return c

    lax.fori_loop(0, tq, start, 0, unroll=2)
    for _ in range(TOP_K):
        pltpu.make_async_copy(h_ref, h_ref, sem).wait()

    @pl.when(step == 0)
    def _():
        pad_rows(True)


def _dispatch(h2p, dest_flat, pad_off, pad_n, n_used_rows, n_rows, tm, nblk):
    t = h2p.shape[0] // nblk
    n_e = pad_off.shape[0]
    tq = _pick(t, (256, 128))
    return pl.pallas_call(
        functools.partial(_dispatch_kernel, n_tok=t, tq=tq, tm=tm, n_e=n_e, nblk=nblk),
        grid_spec=pltpu.PrefetchScalarGridSpec(
            num_scalar_prefetch=4, grid=(t // tq,),
            in_specs=[pl.BlockSpec((tq * nblk, V7X_LANES), lambda i, *_: (i, 0))],
            out_specs=pl.BlockSpec(memory_space=pl.ANY),
            scratch_shapes=[pltpu.VMEM((tm * nblk, V7X_LANES), h2p.dtype),
                            pltpu.SemaphoreType.DMA(()), pltpu.SemaphoreType.DMA(())]),
        out_shape=jax.ShapeDtypeStruct((n_rows * nblk, V7X_LANES), h2p.dtype),
        compiler_params=pltpu.CompilerParams(dimension_semantics=("arbitrary",),
                                             has_side_effects=True,
                                             vmem_limit_bytes=V7X_VMEM_LIMIT),
        name="moe_dispatch",
    )(dest_flat, pad_off, pad_n, n_used_rows, h2p)


EXPERT_COL_BLOCK = 2048
GEMM1_COL_CHUNK = 256
MOE_TILE_LEVELS = 3
MOE_MIN_ROWS = 16
CAST_ROWS = 256


def _stream_expert_weights(te_ref, nt_ref, w_hbm, stage_ref, slot_ref, sem):
    j, i = pl.program_id(0), pl.program_id(1)
    n_pass, n_tiles = pl.num_programs(0), pl.num_programs(1)
    tn = stage_ref.shape[2]
    n_used = nt_ref[0]
    cur = te_ref[i]
    first = (i < n_used) & ((i == 0) | (cur != te_ref[jnp.maximum(i - 1, 0)]))

    def fetch(e, jj, slot):
        src = w_hbm.at[e, :, pl.ds(pl.multiple_of(jj * tn, tn), tn)]
        return pltpu.make_async_copy(src, stage_ref.at[slot], sem.at[slot])

    @pl.when((i == 0) & (j == 0))
    def _():
        slot_ref[0] = 1
        fetch(te_ref[0], 0, 0).start()

    @pl.when(first)
    def _():
        slot = 1 - slot_ref[0]
        slot_ref[0] = slot
        fetch(cur, j, slot).wait()

        def same_group(k):
            return (k < n_used) & (te_ref[jnp.minimum(k, n_tiles - 1)] == cur)
        nxt = lax.while_loop(same_group, lambda k: k + 1, i + 1)

        @pl.when(nxt < n_used)
        def _():
            fetch(te_ref[jnp.minimum(nxt, n_tiles - 1)], j, 1 - slot).start()

        @pl.when((nxt >= n_used) & (j + 1 < n_pass))
        def _():
            fetch(te_ref[0], j + 1, 1 - slot).start()

    return first, slot_ref[0]


def _cast_block(stage_ref, slot, wbf_ref):
    def cast(r, c):
        rows = pl.ds(pl.multiple_of(r * CAST_ROWS, CAST_ROWS), CAST_ROWS)
        wbf_ref[rows, :] = stage_ref[slot, rows, :].astype(BF16)
        return c
    lax.fori_loop(0, wbf_ref.shape[0] // CAST_ROWS, cast, 0)


def _gemm1_kernel(te_ref, nt_ref, tv_ref, x_ref, w_hbm, b_ref, cmp_ref, o_ref, stage_ref, wbf_ref,
                  slot_ref, sem, *, nc, nblk):
    first, slot = _stream_expert_weights(te_ref, nt_ref, w_hbm, stage_ref, slot_ref, sem)

    def compute(rows, convert):
        words = _load_token_rows(x_ref, rows, nblk)
        x_lo, x_hi = (v.astype(BF16) for v in _unpack_bf16_pair(words))
        half = x_lo.shape[1]
        cw = cmp_ref.shape[0]
        lane = lax.broadcasted_iota(jnp.int32, (rows, nc), 1)
        pieces = []
        for n0 in range(0, wbf_ref.shape[1], nc):
            if convert:
                wbf_ref[:, n0:n0 + nc] = stage_ref[slot, :, n0:n0 + nc].astype(BF16)
            hb = (jnp.dot(x_lo, wbf_ref[:half, n0:n0 + nc], preferred_element_type=F32)
                  + jnp.dot(x_hi, wbf_ref[half:, n0:n0 + nc], preferred_element_type=F32)
                  + b_ref[0, :, n0:n0 + nc])
            nxt = pltpu.roll(hb, nc - 1, axis=1)
            glu = jnp.minimum(hb, SWIGLU_LIMIT)
            lin = jnp.clip(nxt, -SWIGLU_LIMIT, SWIGLU_LIMIT)
            act = glu * jax.nn.sigmoid(SWIGLU_ALPHA * glu) * (lin + 1.0)
            act = jnp.where(lane % 2 == 0, act, 0.0).astype(BF16)
            pieces += [act[:, c * cw:(c + 1) * cw] for c in range(nc // cw)]
        packed = jnp.dot(jnp.concatenate(pieces, axis=0), cmp_ref[...],
                         preferred_element_type=F32).astype(o_ref.dtype)
        for p in range(len(pieces)):
            o_ref[:rows, p * (cw // 2):(p + 1) * (cw // 2)] = packed[p * rows:(p + 1) * rows]
        if rows < o_ref.shape[0]:
            o_ref[rows:, :] = jnp.zeros((o_ref.shape[0] - rows, o_ref.shape[1]), o_ref.dtype)

    _for_real_rows(pl.program_id(1), nt_ref, tv_ref, o_ref, o_ref.shape[0], compute,
                   first, lambda: _cast_block(stage_ref, slot, wbf_ref))


def _for_real_rows(i, nt_ref, tv_ref, o_ref, tm, compute, first, convert_all):
    used = i < nt_ref[0]
    real = tv_ref[i]
    sizes = [tm >> s for s in range(MOE_TILE_LEVELS) if (tm >> s) % MOE_MIN_ROWS == 0]

    @pl.when(first & (real <= sizes[1]))
    def _():
        convert_all()

    for n, rows in enumerate(sizes):
        fits = real <= rows
        if n + 1 < len(sizes):
            fits = fits & (real > sizes[n + 1])
        if n == 0:
            @pl.when(used & fits & first)
            def _():
                compute(rows, True)
            fits = fits & jnp.logical_not(first)

        @pl.when(used & fits)
        def _(rows=rows):
            compute(rows, False)

    @pl.when(jnp.logical_not(used))
    def _():
        o_ref[...] = jnp.zeros_like(o_ref)


def _gemm1(xin, w1, b1, tile_e, n_tiles_used, tile_rows, tm, nblk):
    n_rows = xin.shape[0] // nblk
    n_e, d, f2 = w1.shape
    tn = min(EXPERT_COL_BLOCK, f2)
    nc = min(GEMM1_COL_CHUNK, tn)
    assert f2 % tn == 0 and d % CAST_ROWS == 0
    n_tiles = n_rows // tm
    cw = V7X_MXU_DIM
    cmp_np = np.zeros((cw, cw // 2), np.float32)
    cmp_np[np.arange(0, cw, 2), np.arange(cw // 2)] = 1.0

    def used(i, nt):
        return jnp.minimum(i, nt[0] - 1)

    return pl.pallas_call(
        functools.partial(_gemm1_kernel, nc=nc, nblk=nblk),
        grid_spec=pltpu.PrefetchScalarGridSpec(
            num_scalar_prefetch=3, grid=(f2 // tn, n_tiles),
            in_specs=[pl.BlockSpec((tm * nblk, V7X_LANES),
                                   lambda j, i, te, nt, tv: (used(i, nt), 0)),
                      pl.BlockSpec(memory_space=pl.ANY),
                      pl.BlockSpec((1, 1, tn), lambda j, i, te, nt, tv: (te[used(i, nt)], 0, j)),
                      pl.BlockSpec((cw, cw // 2), lambda j, i, te, nt, tv: (0, 0))],
            out_specs=pl.BlockSpec((tm, tn // 2), lambda j, i, te, nt, tv: (i, j)),
            scratch_shapes=[pltpu.VMEM((2, d, tn), F32), pltpu.VMEM((d, tn), BF16),
                            pltpu.SMEM((1,), jnp.int32), pltpu.SemaphoreType.DMA((2,))]),
        out_shape=jax.ShapeDtypeStruct((n_rows, f2 // 2), BF16),
        compiler_params=_cparams(("arbitrary", "arbitrary")),
        name="moe_gemm1",
    )(tile_e, n_tiles_used, tile_rows, xin, w1, b1.reshape(n_e, 1, f2), jnp.asarray(cmp_np, BF16))


def _gemm2_kernel(te_ref, nt_ref, tv_ref, a_ref, w_hbm, b_ref, o_ref, stage_ref, wbf_ref, slot_ref,
                  sem, *, nc, nblk):
    first, slot = _stream_expert_weights(te_ref, nt_ref, w_hbm, stage_ref, slot_ref, sem)
    tm = a_ref.shape[0]

    def compute(rows, convert):
        a = a_ref[:rows, :]
        half = wbf_ref.shape[1] // 2

        def cols(n0):
            if convert:
                wbf_ref[:, n0:n0 + nc] = stage_ref[slot, :, n0:n0 + nc].astype(BF16)
            return (jnp.dot(a, wbf_ref[:, n0:n0 + nc], preferred_element_type=F32)
                    + b_ref[0, :, n0:n0 + nc])

        for n0 in range(0, half, nc):
            words = _pack_bf16_pair(cols(n0), cols(half + n0))
            for c in range(nc // V7X_LANES):
                blk = n0 // V7X_LANES + c
                o_ref[pl.ds(blk, rows, stride=nblk), :] = words[:, c * V7X_LANES:(c + 1) * V7X_LANES]
        if rows < tm:
            o_ref[rows * nblk:, :] = jnp.zeros(((tm - rows) * nblk, o_ref.shape[1]), o_ref.dtype)

    _for_real_rows(pl.program_id(1), nt_ref, tv_ref, o_ref, tm, compute,
                   first, lambda: _cast_block(stage_ref, slot, wbf_ref))


def _gemm2(act, w2, b2, tile_e, n_tiles_used, tile_rows, tm):
    n_rows, f = act.shape
    n_e, _, d = w2.shape
    tn = d
    nc = min(512, tn // 2)
    nblk = d // 2 // V7X_LANES
    assert f % CAST_ROWS == 0
    n_tiles = n_rows // tm

    def used(i, nt):
        return jnp.minimum(i, nt[0] - 1)

    return pl.pallas_call(
        functools.partial(_gemm2_kernel, nc=nc, nblk=nblk),
        grid_spec=pltpu.PrefetchScalarGridSpec(
            num_scalar_prefetch=3, grid=(d // tn, n_tiles),
            in_specs=[pl.BlockSpec((tm, f), lambda j, i, te, nt, tv: (used(i, nt), 0)),
                      pl.BlockSpec(memory_space=pl.ANY),
                      pl.BlockSpec((1, 1, tn), lambda j, i, te, nt, tv: (te[used(i, nt)], 0, j))],
            out_specs=pl.BlockSpec((tm * nblk, V7X_LANES), lambda j, i, te, nt, tv: (i, 0)),
            scratch_shapes=[pltpu.VMEM((2, f, tn), F32), pltpu.VMEM((f, tn), BF16),
                            pltpu.SMEM((1,), jnp.int32), pltpu.SemaphoreType.DMA((2,))]),
        out_shape=jax.ShapeDtypeStruct((n_rows * nblk, V7X_LANES), jnp.uint32),
        compiler_params=_cparams(("arbitrary", "arbitrary")),
        name="moe_gemm2",
    )(tile_e, n_tiles_used, tile_rows, act, w2, b2.reshape(n_e, 1, d))


def _combine_kernel(dest_ref, x1_ref, gate_ref, gt_ref, g_ref, b_ref, y_ref, o_ref, buf, sem,
                    *, alpha, tc, n_tok, nblk):
    i = pl.program_id(0)
    n_steps = pl.num_programs(0)

    def rows(tok):
        return pl.ds(pl.multiple_of(tok * nblk, nblk), nblk)

    def copy(step, slot, k, r):
        src = y_ref.at[rows(dest_ref[k * n_tok + step * tc + r])]
        return pltpu.make_async_copy(src, buf.at[slot, k, rows(r)], sem.at[slot])

    def issue(step, slot):
        def body(r, c):
            for k in range(TOP_K):
                copy(step, slot, k, r).start(priority=k % 2)
            return c
        lax.fori_loop(0, tc, body, 0, unroll=2)

    def drain(slot):
        pltpu.make_async_copy(buf.at[slot], buf.at[slot], sem.at[slot]).wait()

    slot = i % 2

    @pl.when(i == 0)
    def _():
        issue(0, 0)

    @pl.when(i + 1 < n_steps)
    def _():
        issue(i + 1, 1 - slot)

    drain(slot)
    gates = gate_ref[...]
    lo, hi = _unpack_bf16_pair(_load_token_rows(buf.at[slot, 0], tc, nblk))
    ff_lo, ff_hi = gates[:, 0:1] * lo, gates[:, 0:1] * hi
    for k in range(1, TOP_K):
        lo, hi = _unpack_bf16_pair(_load_token_rows(buf.at[slot, k], tc, nblk))
        ff_lo, ff_hi = ff_lo + gates[:, k:k + 1] * lo, ff_hi + gates[:, k:k + 1] * hi
    ff = jnp.concatenate([ff_lo, ff_hi], axis=1)
    y = alpha * x1_ref[...] + (1.0 + gt_ref[0]) * ff
    o_ref[...] = _layer_norm(y, g_ref[...], b_ref[...])


def _combine(dest_flat, x1, gates_tk, gt, g, b, y, seq, alpha):
    t, d = x1.shape
    tc = _pick(seq, (256, 128))
    per_b = seq // tc
    nblk = d // 2 // V7X_LANES
    return pl.pallas_call(
        functools.partial(_combine_kernel, alpha=alpha, tc=tc, n_tok=t, nblk=nblk),
        grid_spec=pltpu.PrefetchScalarGridSpec(
            num_scalar_prefetch=1, grid=(t // tc,),
            in_specs=[pl.BlockSpec((tc, d), lambda i, ds: (i, 0)),
                      pl.BlockSpec((tc, TOP_K), lambda i, ds: (i, 0)),
                      pl.BlockSpec((1, 1, d), lambda i, ds: (i // per_b, 0, 0)),
                      pl.BlockSpec((1, d), lambda i, ds: (0, 0)),
                      pl.BlockSpec((1, d), lambda i, ds: (0, 0)),
                      pl.BlockSpec(memory_space=pl.ANY)],
            out_specs=pl.BlockSpec((tc, d), lambda i, ds: (i, 0)),
            scratch_shapes=[pltpu.VMEM((2, TOP_K, tc * nblk, V7X_LANES), jnp.uint32),
                            pltpu.SemaphoreType.DMA((2,))]),
        out_shape=jax.ShapeDtypeStruct((t, d), F32),
        compiler_params=_cparams(("arbitrary",)),
        name="moe_combine_ln2",
    )(dest_flat, x1, gates_tk, gt[:, None, :], g.reshape(1, d), b.reshape(1, d), y)


MOE_ROW_TILE = 512


def _moe_ffn(h2p, logits_t, x1, gt_f, ln_g, ln_b, w1, b1, w2, b2, seq, alpha):
    tm = MOE_ROW_TILE
    n_e, t = logits_t.shape
    n_rows = -(-(TOP_K * t) // tm) * tm + n_e * tm
    n_tiles = n_rows // tm
    gate_t, dest, tiles, pads = _routing(logits_t, tm, n_tiles)
    dest = dest.reshape(-1)
    tile_e = tiles[0, :n_tiles]
    n_used_rows = tiles[1, :1]
    tile_rows = tiles[2, :n_tiles]
    n_used_tiles = n_used_rows // tm
    nblk = x1.shape[1] // 2 // V7X_LANES
    xin = _dispatch(h2p, dest, pads[0, :, 0], pads[1, :, 0], n_used_rows, n_rows, tm, nblk)
    act = _gemm1(xin, w1, b1, tile_e, n_used_tiles, tile_rows, tm, nblk)
    y = _gemm2(act, w2, b2, tile_e, n_used_tiles, tile_rows, tm)
    return _combine(dest, x1, gate_t.T, gt_f, ln_g, ln_b, y, seq, alpha)


def kernel(x, c, positions, w_ada, b_ada, w_in, hgrn_lb, gnorm_w, w_o, ln1_g, ln1_b,
           router_w, router_b, w1, b1, w2, b2, ln2_g, ln2_b):
    bsz, seq, d = x.shape
    depth = w_ada.shape[0]
    t = bsz * seq
    att_w = d // 2
    hg_w = d - att_w
    alpha = (2.0 * depth) ** 0.25

    lb_all = jnp.cumsum(jax.nn.softmax(hgrn_lb.astype(F32), axis=0), axis=0)
    inv = ROPE_THETA ** (-(jnp.arange(0, ROT_DIM, 2, dtype=F32) / ROT_DIM))
    lane = np.arange(V7X_LANES)
    inv_lane = inv[(lane % ATT_HEAD_DIM) % (ROT_DIM // 2)].reshape(1, V7X_LANES)
    pos_b = jnp.broadcast_to(positions.astype(F32).reshape(t, 1), (t, V7X_LANES))

    x2d = x.reshape(t, d)
    for l in range(depth):
        mod = _adaln(c, w_ada[l], b_ada[l])
        sh_a, sc_a, gt_a, sh_f, sc_f, gt_f = jnp.split(mod, 6, axis=-1)

        proj = _in_proj(x2d, sc_a, sh_a, w_in[l], seq)
        proj3 = proj.reshape(bsz, seq, proj.shape[1])
        q_hp, k_hp, v_hp = _qk_rope(proj, pos_b, inv_lane, bsz, seq, att_w)
        att = _attention(q_hp, k_hp, v_hp)
        rec = _hgrn2(proj3, lb_all[l], gnorm_w[l], att_w, hg_w).reshape(t, hg_w)

        x1, h2p, logits_t = _out_proj(att, rec, x2d, w_o[l], gt_a, sc_f, sh_f, ln1_g[l], ln1_b[l],
                                      router_w[l], router_b[l], seq, alpha)

        x2d = _moe_ffn(h2p, logits_t, x1, gt_f, ln2_g[l], ln2_b[l], w1[l], b1[l], w2[l], b2[l],
                       seq, alpha)
    return x2d.reshape(bsz, seq, d)
```

```python
import functools

import numpy as np
import jax
import jax.numpy as jnp
from jax import lax
from jax.experimental import pallas as pl
from jax.experimental.pallas import tpu as pltpu

F32 = jnp.float32
BF16 = jnp.bfloat16

V7X_LANES = 128
V7X_VMEM_LIMIT = 56 * 1024 * 1024

ATT_HEAD_DIM = 64
DILATED_PAIRS = ((128, 1), (512, 4), (2048, 16))
ATT_BLOCK = 128
ROT_DIM = ATT_HEAD_DIM // 4
ROPE_THETA = 500000.0
HG_EXPAND = 128
HG_CHUNK = 64
HG_SUB = 16
TOP_K = 4
SWIGLU_ALPHA = 1.702
SWIGLU_LIMIT = 7.0
LN_EPS = 1e-5
RMS_EPS = 1e-6
NEG_INF = -1e30

NT_DIMS = (((1,), (1,)), ((), ()))


def _pick(n, candidates):
    for c in candidates:
        if n % c == 0:
            return c
    raise ValueError(f"no tile in {candidates} divides {n}")


def _cparams(sem, vmem=V7X_VMEM_LIMIT, flags=None):
    return pltpu.CompilerParams(dimension_semantics=sem, vmem_limit_bytes=vmem, flags=flags)


def _adaln_kernel(c_ref, w_ref, b_ref, o_ref):
    c = c_ref[...]
    s = c * jax.nn.sigmoid(c)
    o_ref[...] = jnp.dot(s.astype(BF16), w_ref[...].astype(BF16),
                         preferred_element_type=F32) + b_ref[...]


def _adaln(c, w, b):
    bsz, d = c.shape
    n = w.shape[1]
    rows = 8
    cp = jnp.zeros((rows, d), F32).at[:bsz].set(c)
    tn = _pick(n, (1024, 512, 256, 128))
    out = pl.pallas_call(
        _adaln_kernel,
        grid=(n // tn,),
        in_specs=[pl.BlockSpec((rows, d), lambda j: (0, 0)),
                  pl.BlockSpec((d, tn), lambda j: (0, j)),
                  pl.BlockSpec((1, tn), lambda j: (0, j))],
        out_specs=pl.BlockSpec((rows, tn), lambda j: (0, j)),
        out_shape=jax.ShapeDtypeStruct((rows, n), F32),
        compiler_params=_cparams(("arbitrary",)),
        name="adaln",
    )(cp, w, b.reshape(1, n))
    return out[:bsz]


def _in_proj_kernel(x_ref, sc_ref, sh_ref, w_ref, o_ref, wbf_ref):
    @pl.when(pl.program_id(1) == 0)
    def _():
        wbf_ref[...] = w_ref[...].astype(BF16)

    h = x_ref[...] * (1.0 + sc_ref[0]) + sh_ref[0]
    o_ref[...] = jnp.dot(h.astype(BF16), wbf_ref[...],
                         preferred_element_type=F32).astype(o_ref.dtype)


def _in_proj(x2d, sc, sh, w, seq):
    t, d = x2d.shape
    nc = w.shape[1]
    tm = _pick(seq, (512, 256, 128))
    tn = _pick(nc, (1024, 512, 256, 128))
    per_b = seq // tm
    vec = pl.BlockSpec((1, 1, d), lambda j, i: (i // per_b, 0, 0))
    return pl.pallas_call(
        _in_proj_kernel,
        grid=(nc // tn, t // tm),
        in_specs=[pl.BlockSpec((tm, d), lambda j, i: (i, 0)), vec, vec,
                  pl.BlockSpec((d, tn), lambda j, i: (0, j))],
        out_specs=pl.BlockSpec((tm, tn), lambda j, i: (i, j)),
        out_shape=jax.ShapeDtypeStruct((t, nc), BF16),
        scratch_shapes=[pltpu.VMEM((d, tn), BF16)],
        compiler_params=_cparams(("arbitrary", "arbitrary")),
        name="in_proj",
    )(x2d, sc[:, None, :], sh[:, None, :], w)


def _rope_kernel(q_ref, k_ref, v_ref, pos_ref, inv_ref, qo_ref, ko_ref, vo_ref):
    tm = pos_ref.shape[0]
    lane = lax.broadcasted_iota(jnp.int32, (tm, V7X_LANES), 1)
    lh = lane % ATT_HEAD_DIM
    half = ROT_DIM // 2
    ang = pos_ref[...] * inv_ref[...]
    cs = jnp.where(lh < ROT_DIM, jnp.cos(ang), 1.0)
    sn = jnp.sin(ang)
    sn = jnp.where(lh < half, -sn, jnp.where(lh < ROT_DIM, sn, 0.0))

    def rope(t):
        swapped = jnp.where(lh < half,
                            pltpu.roll(t, V7X_LANES - half, axis=1),
                            pltpu.roll(t, half, axis=1))
        return t * cs + swapped * sn

    for h in range(qo_ref.shape[1]):
        lanes = slice(h * V7X_LANES, (h + 1) * V7X_LANES)
        qo_ref[0, h] = rope(q_ref[:, lanes].astype(F32)) * (ATT_HEAD_DIM ** -0.5)
        ko_ref[0, h] = rope(k_ref[:, lanes].astype(F32))
        vo_ref[0, h] = v_ref[:, lanes].astype(F32)


def _qk_rope(proj, pos_b, inv_lane, bsz, seq, att_w):
    t = proj.shape[0]
    hp = att_w // V7X_LANES
    tm = _pick(seq, (256, 128))
    per_b = seq // tm
    out_spec = pl.BlockSpec((1, hp, tm, V7X_LANES), lambda i: (i // per_b, 0, i % per_b, 0))
    shp = jax.ShapeDtypeStruct((bsz, hp, seq, V7X_LANES), F32)
    return pl.pallas_call(
        _rope_kernel,
        grid=(t // tm,),
        in_specs=[pl.BlockSpec((tm, att_w), lambda i: (i, 0)),
                  pl.BlockSpec((tm, att_w), lambda i: (i, 1)),
                  pl.BlockSpec((tm, att_w), lambda i: (i, 2)),
                  pl.BlockSpec((tm, V7X_LANES), lambda i: (i, 0)),
                  pl.BlockSpec((1, V7X_LANES), lambda i: (0, 0))],
        out_specs=[out_spec, out_spec, out_spec],
        out_shape=[shp, shp, shp],
        compiler_params=_cparams(("arbitrary",)),
        name="qk_rope",
    )(proj, proj, proj, pos_b, inv_lane)


ATT_TASKS_PER_STEP = 8


def _attn_kernel(q_in, k_in, v_in, out_ref, qs, ks, vs, sm, no_lo, nl_lo, no_s, nl_s, nat, *, seq):
    qn = ATT_BLOCK
    n_task = seq // qn
    n_head = V7X_LANES // ATT_HEAD_DIM
    gc = min(ATT_TASKS_PER_STEP, n_task)
    (_, d_lo), (_, dm), (_, d_hi) = DILATED_PAIRS
    g_hi = d_hi // dm
    rows_m = seq // dm
    lane = lax.broadcasted_iota(jnp.int32, (qn, V7X_LANES), 1)
    qi = lax.broadcasted_iota(jnp.int32, (gc, qn, 2 * qn), 1)
    kj = lax.broadcasted_iota(jnp.int32, (gc, qn, 2 * qn), 2)
    dist = qi + qn - kj
    band = (dist >= 0) & (dist <= qn)
    zeros = jnp.zeros((qn, V7X_LANES), BF16)
    srcs = (q_in, k_in, v_in)

    for a, src in enumerate(srcs):
        for r in range(dm):
            sm[a, r] = src[0, 0, pl.ds(r, rows_m, stride=dm), :]

    def task_index(d, r, n):
        if d == d_lo:
            return None, slice(n * qn, (n + 1) * qn)
        if d == dm:
            return r, slice(n * qn, (n + 1) * qn)
        return r % dm, pl.ds(r // dm + n * qn * g_hi, qn, stride=g_hi)

    for bi, (window, d) in enumerate(DILATED_PAIRS):
        nb = seq // (d * qn)
        tasks = [(r, n) for r in range(d) for n in range(nb)]
        for g, (r, n) in enumerate(tasks):
            sel, rows = task_index(d, r, n)

            def operand(a, sel=sel, rows=rows):
                return srcs[a][0, 0, rows, :] if sel is None else sm[a, sel, rows, :]

            qv = operand(0)
            for h in range(n_head):
                qs[h, g] = jnp.where(lane // ATT_HEAD_DIM == h, qv, 0.0).astype(BF16)
            for a, dst in ((1, ks), (2, vs)):
                blk = operand(a).astype(BF16)
                dst[g, qn:, :] = blk
                if n + 1 < nb:
                    dst[g + 1, :qn, :] = blk
                if n == 0:
                    dst[g, :qn, :] = zeros

        for g0 in range(0, n_task, gc):
            gsl = slice(g0, g0 + gc)
            gidx = g0 + lax.broadcasted_iota(jnp.int32, (gc, qn, 2 * qn), 0)
            valid = band & ((gidx % nb != 0) | (kj >= qn))
            k = ks[gsl]
            v = vs[gsl]
            o = None
            for h in range(n_head):
                s = jnp.einsum("gqd,gkd->gqk", qs[h, gsl], k, preferred_element_type=F32)
                s = jnp.where(valid, s, NEG_INF)
                m = jnp.max(s, axis=-1, keepdims=True)
                p = jnp.exp(s - m)
                den = jnp.sum(p, axis=-1, keepdims=True)
                oh = jnp.einsum("gqk,gkd->gqd", p.astype(BF16), v,
                                preferred_element_type=F32) / den
                lh = jnp.broadcast_to(m + jnp.log(den), oh.shape)
                if o is None:
                    o, lse = oh, lh
                else:
                    in_head = lane[None] // ATT_HEAD_DIM == h
                    o, lse = jnp.where(in_head, oh, o), jnp.where(in_head, lh, lse)
            for t in range(gc):
                sel, rows = task_index(d, *tasks[g0 + t])
                if sel is None:
                    no_lo[rows, :] = o[t]
                    nl_lo[rows, :] = lse[t]
                else:
                    no_s[bi - 1, sel, rows, :] = o[t]
                    nl_s[bi - 1, sel, rows, :] = lse[t]

    for r in range(dm):
        for c in range(rows_m // qn):
            rows = slice(c * qn, (c + 1) * qn)
            seq_rows = pl.ds(r + dm * c * qn, qn, stride=dm)
            ls = [nl_lo[seq_rows, :], nl_s[0, r, rows, :], nl_s[1, r, rows, :]]
            os_ = [no_lo[seq_rows, :], no_s[0, r, rows, :], no_s[1, r, rows, :]]
            mx = functools.reduce(jnp.maximum, ls)
            ws = [jnp.exp(l - mx) for l in ls]
            tot = functools.reduce(lambda x, y: x + y, ws)
            acc = functools.reduce(lambda x, y: x + y, [w * o for w, o in zip(ws, os_)])
            nat[seq_rows, :] = acc / tot
    out_ref[0, 0] = nat[...].astype(out_ref.dtype)


def _attention(q_hp, k_hp, v_hp):
    bsz, hp, seq, _ = q_hp.shape
    (_, d_lo), (_, dm), (_, d_hi) = DILATED_PAIRS
    assert d_lo == 1 and d_hi % dm == 0
    for window, d in DILATED_PAIRS:
        assert window // d == ATT_BLOCK and seq % (d * ATT_BLOCK) == 0
    n_task = seq // ATT_BLOCK
    assert n_task % min(ATT_TASKS_PER_STEP, n_task) == 0
    n_head = V7X_LANES // ATT_HEAD_DIM
    streams = (dm, seq // dm, V7X_LANES)
    spec = pl.BlockSpec((1, 1, seq, V7X_LANES), lambda b, h: (b, h, 0, 0))
    return pl.pallas_call(
        functools.partial(_attn_kernel, seq=seq),
        grid=(bsz, hp),
        in_specs=[spec, spec, spec],
        out_specs=spec,
        out_shape=jax.ShapeDtypeStruct((bsz, hp, seq, V7X_LANES), BF16),
        scratch_shapes=[pltpu.VMEM((n_head, n_task, ATT_BLOCK, V7X_LANES), BF16),
                        pltpu.VMEM((n_task, 2 * ATT_BLOCK, V7X_LANES), BF16),
                        pltpu.VMEM((n_task, 2 * ATT_BLOCK, V7X_LANES), BF16),
                        pltpu.VMEM((3,) + streams, F32),
                        pltpu.VMEM((seq, V7X_LANES), F32), pltpu.VMEM((seq, V7X_LANES), F32),
                        pltpu.VMEM((2,) + streams, F32), pltpu.VMEM((2,) + streams, F32),
                        pltpu.VMEM((seq, V7X_LANES), F32)],
        compiler_params=_cparams(("arbitrary", "arbitrary")),
        name="dilated_attn",
    )(q_hp, k_hp, v_hp)


def _hgrn_kernel(q_ref, f_ref, i_ref, g_ref, lb_ref, gw_ref, sums_ref, o_ref,
                 st_ref, hl_ref, kk_ref, cum_ref, *, heads, ts):
    @pl.when(pl.program_id(2) == 0)
    def _():
        st_ref[...] = jnp.zeros_like(st_ref)

    c_len = HG_CHUNK
    lanes = heads * HG_EXPAND
    lb = lb_ref[...]
    for c in range(ts // c_len):
        rows = slice(c * c_len, (c + 1) * c_len)
        f = lb + (1.0 - lb) * jax.nn.sigmoid(f_ref[0, rows, :].astype(F32))
        kk_ref[rows, :] = 1.0 - f
        hi, lo = _bf16_split(jnp.log(f))
        hl_ref[rows, :lanes] = hi
        hl_ref[rows, lanes:] = lo
    both = jnp.dot(sums_ref[...], hl_ref[...], preferred_element_type=F32)
    cum_ref[...] = both[:, :lanes] + both[:, lanes:]

    n_sub = c_len // HG_SUB

    def sub_block_anchor(b):
        mids = [b[i * HG_SUB + HG_SUB // 2:i * HG_SUB + HG_SUB // 2 + 1, :] for i in range(n_sub)]
        return jnp.concatenate([jnp.broadcast_to(m, (HG_SUB, m.shape[1])) for m in mids], axis=0)

    gi = lax.broadcasted_iota(jnp.int32, (heads * n_sub, HG_SUB, c_len), 0) % n_sub
    qi = lax.broadcasted_iota(jnp.int32, (heads * n_sub, HG_SUB, c_len), 1)
    si = lax.broadcasted_iota(jnp.int32, (heads * n_sub, HG_SUB, c_len), 2)
    causal = si <= gi * HG_SUB + qi
    key_row = lax.broadcasted_iota(jnp.int32, (c_len, HG_EXPAND), 0)

    for c in range(ts // c_len):
        rows = slice(c * c_len, (c + 1) * c_len)
        qts, kts, vbs = [], [], []
        for h in range(heads):
            lsl = slice(h * HG_EXPAND, (h + 1) * HG_EXPAND)
            b = cum_ref[rows, lsl]
            anchor = sub_block_anchor(b)
            kk = kk_ref[rows, lsl]
            qt = (q_ref[0, rows, lsl].astype(F32) * jnp.exp(b - anchor)).astype(BF16)
            qts.append(qt.reshape(n_sub, HG_SUB, HG_EXPAND))
            vb = i_ref[0, rows, lsl].astype(BF16)
            for i in range(n_sub):
                hi_r = (i + 1) * HG_SUB
                kt = kk * jnp.exp(anchor[i * HG_SUB:i * HG_SUB + 1, :] - b)
                kts.append(jnp.where(key_row < hi_r, kt, 0.0).astype(BF16))
                vbs.append(vb)
        a = jnp.einsum("gqk,gsk->gqs", jnp.concatenate(qts, axis=0), jnp.stack(kts),
                       preferred_element_type=F32)
        a = jnp.where(causal, a, 0.0).astype(BF16)
        o_intra = jnp.einsum("gqs,gsv->gqv", a, jnp.stack(vbs), preferred_element_type=F32)

        for h in range(heads):
            lsl = slice(h * HG_EXPAND, (h + 1) * HG_EXPAND)
            b = cum_ref[rows, lsl]
            b_last = jnp.broadcast_to(b[c_len - 1:c_len, :], b.shape)
            kk = kk_ref[rows, lsl]
            q = q_ref[0, rows, lsl].astype(F32)
            v = i_ref[0, rows, lsl].astype(F32)
            st = st_ref[h]
            o_inter = lax.dot_general((q * jnp.exp(b)).astype(BF16), st.astype(BF16), NT_DIMS,
                                      preferred_element_type=F32)
            o = o_inter + o_intra[h * n_sub:(h + 1) * n_sub].reshape(c_len, HG_EXPAND)
            kl = kk * jnp.exp(b_last - b)
            upd = jnp.dot(v.T.astype(BF16), kl.astype(BF16), preferred_element_type=F32)
            st_ref[h] = st * jnp.exp(b_last[0:1, :]) + upd
            o = o * lax.rsqrt(jnp.mean(o * o, axis=-1, keepdims=True) + RMS_EPS)
            g = g_ref[0, rows, lsl].astype(F32)
            o = o * gw_ref[:, lsl] * (g * jax.nn.sigmoid(g))
            o_ref[0, rows, lsl] = o.astype(o_ref.dtype)


def _hgrn_cumsum_matrix(ts):
    t = np.arange(ts)[:, None]
    s = np.arange(ts)[None, :]
    return (((t // HG_CHUNK) == (s // HG_CHUNK)) & (s <= t)).astype(np.float32)


def _hgrn2(proj3, lb, gw, att_w, hg_w):
    bsz, seq, _ = proj3.shape
    n_heads = hg_w // HG_EXPAND
    heads = _pick(n_heads, (4, 2, 1))
    lanes = heads * HG_EXPAND
    ts = _pick(seq, (256, 128, 64))
    base = 3 * att_w
    assert base % lanes == 0 and hg_w % lanes == 0

    def seg(k):
        off = (base + k * hg_w) // lanes
        return pl.BlockSpec((1, ts, lanes), lambda b, h, s: (b, s, off + h))

    vec = pl.BlockSpec((1, lanes), lambda b, h, s: (0, h))
    sums = jnp.asarray(_hgrn_cumsum_matrix(ts), BF16)
    return pl.pallas_call(
        functools.partial(_hgrn_kernel, heads=heads, ts=ts),
        grid=(bsz, hg_w // lanes, seq // ts),
        in_specs=[seg(0), seg(1), seg(2), seg(3), vec, vec,
                  pl.BlockSpec((ts, ts), lambda b, h, s: (0, 0))],
        out_specs=pl.BlockSpec((1, ts, lanes), lambda b, h, s: (b, s, h)),
        out_shape=jax.ShapeDtypeStruct((bsz, seq, hg_w), BF16),
        scratch_shapes=[pltpu.VMEM((heads, HG_EXPAND, HG_EXPAND), F32),
                        pltpu.VMEM((ts, 2 * lanes), BF16),
                        pltpu.VMEM((ts, lanes), F32),
                        pltpu.VMEM((ts, lanes), F32)],
        compiler_params=_cparams(("arbitrary", "arbitrary", "arbitrary")),
        name="hgrn2",
    )(proj3, proj3, proj3, proj3, lb.reshape(1, hg_w), gw.reshape(1, hg_w), sums)


def _layer_norm(y, g, b):
    mu = jnp.mean(y, axis=-1, keepdims=True)
    yc = y - mu
    var = jnp.mean(yc * yc, axis=-1, keepdims=True)
    return yc * lax.rsqrt(var + LN_EPS) * g + b


def _bf16_split(x):
    hi = x.astype(BF16)
    lo = (x - hi.astype(F32)).astype(BF16)
    return hi, lo


def _pack_bf16_pair(lo, hi):
    lo_bits = lax.bitcast_convert_type(lo.astype(BF16).astype(F32), jnp.uint32)
    hi_bits = lax.bitcast_convert_type(hi.astype(BF16).astype(F32), jnp.uint32)
    return (lo_bits >> 16) | (hi_bits & jnp.uint32(0xFFFF0000))


def _store_token_rows(ref, tok0, words):
    n, width = words.shape
    nblk = width // V7X_LANES
    for c in range(nblk):
        ref[pl.ds(tok0 * nblk + c, n, stride=nblk), :] = words[:, c * V7X_LANES:(c + 1) * V7X_LANES]


def _load_token_rows(ref, n, nblk):
    return jnp.concatenate([ref[pl.ds(c, n, stride=nblk), :] for c in range(nblk)], axis=1)


def _unpack_bf16_pair(words):
    lo = lax.bitcast_convert_type(words << 16, F32)
    hi = lax.bitcast_convert_type(words & jnp.uint32(0xFFFF0000), F32)
    return lo, hi


def _load_weight_bf16(w_hbm, wbf_ref, stage_ref, sem):
    rows = stage_ref.shape[1]
    n_slab = w_hbm.shape[0] // rows

    def fetch(s, slot):
        return pltpu.make_async_copy(w_hbm.at[pl.ds(s * rows, rows)], stage_ref.at[slot],
                                     sem.at[slot])

    fetch(0, 0).start()
    for s in range(n_slab):
        if s + 1 < n_slab:
            fetch(s + 1, (s + 1) % 2).start()
        fetch(s, s % 2).wait()
        wbf_ref[s * rows:(s + 1) * rows, :] = stage_ref[s % 2].astype(BF16)


def _out_proj_kernel(att_ref, rec_ref, x_ref, wo_hbm, gt_ref, sc_ref, sh_ref, g_ref, b_ref,
                     rw_ref, rb_ref, x1_ref, h2_ref, lg_ref, wobf_ref, rwhi_ref, rwlo_ref,
                     stage_ref, hi_ref, lo_ref, sem, *, alpha, att_w):
    @pl.when(pl.program_id(0) == 0)
    def _():
        _load_weight_bf16(wo_hbm, wobf_ref, stage_ref, sem)
        hi, lo = _bf16_split(rw_ref[...])
        rwhi_ref[...] = hi
        rwlo_ref[...] = lo

    att = jnp.concatenate([att_ref[0, h] for h in range(att_ref.shape[1])], axis=1)
    mix = (jnp.dot(att, wobf_ref[:att_w, :], preferred_element_type=F32)
           + jnp.dot(rec_ref[...], wobf_ref[att_w:, :], preferred_element_type=F32))
    half = x_ref.shape[1] // 2

    for r in range(x_ref.shape[0] // LN_ROW_GROUP):
        rows = slice(r * LN_ROW_GROUP, (r + 1) * LN_ROW_GROUP)
        y = alpha * x_ref[rows, :] + (1.0 + gt_ref[0]) * mix[rows]
        x1 = _layer_norm(y, g_ref[...], b_ref[...])
        x1_ref[rows, :] = x1
        h2 = x1 * (1.0 + sc_ref[0]) + sh_ref[0]
        hi, lo = _bf16_split(h2)
        hi_ref[rows, :] = hi
        lo_ref[rows, :] = lo
        _store_token_rows(h2_ref, r * LN_ROW_GROUP, _pack_bf16_pair(h2[:, :half], h2[:, half:]))

    lg = (jnp.dot(hi_ref[...], rwhi_ref[...], preferred_element_type=F32)
          + jnp.dot(lo_ref[...], rwhi_ref[...], preferred_element_type=F32)
          + jnp.dot(hi_ref[...], rwlo_ref[...], preferred_element_type=F32))
    lg_ref[...] = lg.T[:lg_ref.shape[0], :] + rb_ref[...]


OUT_PROJ_STAGE_ROWS = 256
LN_ROW_GROUP = 16


def _out_proj(att, rec, x2d, wo, gt, sc, sh, g, b, rw, rb, seq, alpha):
    t, d = x2d.shape
    hp = att.shape[1]
    att_w = hp * V7X_LANES
    n_e = rw.shape[1]
    assert n_e <= V7X_LANES and d % OUT_PROJ_STAGE_ROWS == 0
    rw_pad = jnp.zeros((d, V7X_LANES), F32).at[:, :n_e].set(rw)
    nblk = d // 2 // V7X_LANES
    tm = _pick(seq, (512, 256, 128))
    per_b = seq // tm
    vec3 = pl.BlockSpec((1, 1, d), lambda i: (i // per_b, 0, 0))
    full = lambda shape: pl.BlockSpec(shape, lambda i: (0,) * len(shape))
    return pl.pallas_call(
        functools.partial(_out_proj_kernel, alpha=alpha, att_w=att_w),
        grid=(t // tm,),
        in_specs=[pl.BlockSpec((1, hp, tm, V7X_LANES), lambda i: (i // per_b, 0, i % per_b, 0)),
                  pl.BlockSpec((tm, d - att_w), lambda i: (i, 0)),
                  pl.BlockSpec((tm, d), lambda i: (i, 0)),
                  pl.BlockSpec(memory_space=pl.ANY), vec3, vec3, vec3, full((1, d)), full((1, d)),
                  full((d, V7X_LANES)), full((n_e, 1))],
        out_specs=[pl.BlockSpec((tm, d), lambda i: (i, 0)),
                   pl.BlockSpec((tm * nblk, V7X_LANES), lambda i: (i, 0)),
                   pl.BlockSpec((n_e, tm), lambda i: (0, i))],
        out_shape=[jax.ShapeDtypeStruct((t, d), F32),
                   jax.ShapeDtypeStruct((t * nblk, V7X_LANES), jnp.uint32),
                   jax.ShapeDtypeStruct((n_e, t), F32)],
        scratch_shapes=[pltpu.VMEM((d, d), BF16), pltpu.VMEM((d, V7X_LANES), BF16),
                        pltpu.VMEM((d, V7X_LANES), BF16),
                        pltpu.VMEM((2, OUT_PROJ_STAGE_ROWS, d), F32),
                        pltpu.VMEM((tm, d), BF16), pltpu.VMEM((tm, d), BF16),
                        pltpu.SemaphoreType.DMA((2,))],
        compiler_params=_cparams(("arbitrary",)),
        name="out_proj_ln1",
    )(att, rec, x2d, wo, gt[:, None, :], sc[:, None, :], sh[:, None, :],
      g.reshape(1, d), b.reshape(1, d), rw_pad, rb.reshape(n_e, 1))


def _routing_kernel(lg_ref, tri_ref, low_ref, gate_ref, dest_ref, tile_ref, pad_ref, sel_scr, rk_scr,
                    *, blk, tm):
    n_e, t = lg_ref.shape
    eidx = lax.broadcasted_iota(jnp.int32, (n_e, t), 0)
    cur = lg_ref[...]
    vals, idxs = [], []
    for _ in range(TOP_K):
        m = jnp.max(cur, axis=0, keepdims=True)
        ik = jnp.min(jnp.where(cur == m, eidx, n_e), axis=0, keepdims=True)
        cur = jnp.where(eidx == ik, -jnp.inf, cur)
        vals.append(m)
        idxs.append(ik)
    es = [jnp.exp(v - vals[0]) for v in vals]
    den = functools.reduce(lambda a, b: a + b, es)
    for k in range(TOP_K):
        gate_ref[k:k + 1, :] = es[k] / den
    sel = functools.reduce(lambda a, b: a | b, [eidx == ik for ik in idxs])
    sel_scr[...] = jnp.where(sel, 1.0, 0.0)

    tri = tri_ref[...]
    carry = jnp.zeros((n_e, 1), F32)
    for j in range(t // blk):
        sb = sel_scr[:, j * blk:(j + 1) * blk]
        pre = jnp.dot(sb.astype(BF16), tri, preferred_element_type=F32)
        rk_scr[:, j * blk:(j + 1) * blk] = pre + carry
        carry = carry + jnp.sum(sb, axis=1, keepdims=True)

    counts = jnp.broadcast_to(carry, (n_e, V7X_LANES))
    padded = jnp.floor((counts + (tm - 1)) * (1.0 / tm)) * tm
    pends = jnp.dot(low_ref[...], padded, precision=lax.Precision.HIGHEST,
                    preferred_element_type=F32)
    pstarts = pends - padded
    row0 = rk_scr[...] + pstarts[:, 0:1]
    for k in range(TOP_K):
        dest_ref[k:k + 1, :] = jnp.sum(jnp.where(eidx == idxs[k], row0, 0.0), axis=0,
                                       keepdims=True).astype(jnp.int32)
    starts = (lax.broadcasted_iota(jnp.int32, (n_e, tile_ref.shape[1]), 1) * tm).astype(F32)
    owner = jnp.sum(jnp.where(pends[:, 0:1] <= starts, 1.0, 0.0), axis=0, keepdims=True)
    tile_ref[0:1, :] = jnp.minimum(owner, n_e - 1.0).astype(jnp.int32)
    tile_ref[1:2, :] = jnp.broadcast_to(jnp.max(pends[:, 0:1], axis=0, keepdims=True),
                                        (1, tile_ref.shape[1])).astype(jnp.int32)
    e_tile = lax.broadcasted_iota(jnp.int32, starts.shape, 0).astype(F32)
    row_end = jnp.sum(jnp.where(e_tile == owner, (pstarts + counts)[:, 0:1], 0.0), axis=0,
                      keepdims=True)
    tile_ref[2:3, :] = jnp.clip(row_end - starts[0:1, :], 0.0, float(tm)).astype(jnp.int32)
    pad_ref[0] = (pstarts + counts).astype(jnp.int32)
    pad_ref[1] = (padded - counts).astype(jnp.int32)


def _routing(logits_t, tm, n_tiles):
    n_e, t = logits_t.shape
    assert tm & (tm - 1) == 0
    blk = _pick(t, (256, 128))
    tri = jnp.asarray(np.triu(np.ones((blk, blk), np.float32), 1), BF16)
    low = jnp.asarray(np.tril(np.ones((n_e, n_e), np.float32)))
    ntp = -(-n_tiles // V7X_LANES) * V7X_LANES
    full = lambda shape: pl.BlockSpec(shape, lambda: (0,) * len(shape))
    return pl.pallas_call(
        functools.partial(_routing_kernel, blk=blk, tm=tm),
        in_specs=[full((n_e, t)), full((blk, blk)), full((n_e, n_e))],
        out_specs=[full((TOP_K, t)), full((TOP_K, t)), full((3, ntp)), full((2, n_e, V7X_LANES))],
        out_shape=[jax.ShapeDtypeStruct((TOP_K, t), F32),
                   jax.ShapeDtypeStruct((TOP_K, t), jnp.int32),
                   jax.ShapeDtypeStruct((3, ntp), jnp.int32),
                   jax.ShapeDtypeStruct((2, n_e, V7X_LANES), jnp.int32)],
        scratch_shapes=[pltpu.VMEM((n_e, t), F32), pltpu.VMEM((n_e, t), F32)],
        compiler_params=pltpu.CompilerParams(vmem_limit_bytes=V7X_VMEM_LIMIT),
        name="routing",
    )(logits_t, tri, low)


def _dispatch_kernel(dest_ref, pad_off_ref, pad_n_ref, nrow_ref, h_ref, o_ref, zero_ref, sem, zsem,
                     *, n_tok, tq, tm, n_e, nblk):
    step = pl.program_id(0)
    base = step * tq

    def rows(tok, n=1):
        return pl.ds(pl.multiple_of(tok * nblk, nblk), n * nblk)

    def fill(off, n):
        return pltpu.make_async_copy(zero_ref.at[rows(0, n)], o_ref.at[rows(off, n)], zsem)

    pieces = [1 << s for s in range(tm.bit_length() - 2, -1, -1)]

    def pad_rows(wait):
        def per_expert(e, c):
            off = pad_off_ref[e]
            n = pad_n_ref[e]
            for p in pieces:
                hit = (n & p) != 0

                @pl.when(hit)
                def _(off=off, p=p):
                    cp = fill(off, p)
                    cp.wait() if wait else cp.start()

                off = off + jnp.where(hit, p, 0)
            return c
        lax.fori_loop(0, n_e, per_expert, 0)

        def per_tile(i, c):
            cp = fill(i * tm, tm)
            cp.wait() if wait else cp.start()
            return c
        lax.fori_loop(nrow_ref[0] // tm, o_ref.shape[0] // (tm * nblk), per_tile, 0)

    @pl.when(step == 0)
    def _():
        zero_ref[...] = jnp.zeros_like(zero_ref)
        pad_rows(False)

    def start(r, c):
        for k in range(TOP_K):
            dst = o_ref.at[rows(dest_ref[k * n_tok + base + r])]
            pltpu.make_async_copy(h_ref.at[rows(r)], dst, sem).start(priority=k % 2)
        return c

    lax.fori_loop(0, tq, start, 0, unroll=2)
    for _ in range(TOP_K):
        pltpu.make_async_copy(h_ref, h_ref, sem).wait()

    @pl.when(step == 0)
    def _():
        pad_rows(True)


def _dispatch(h2p, dest_flat, pad_off, pad_n, n_used_rows, n_rows, tm, nblk):
    t = h2p.shape[0] // nblk
    n_e = pad_off.shape[0]
    tq = _pick(t, (256, 128))
    return pl.pallas_call(
        functools.partial(_dispatch_kernel, n_tok=t, tq=tq, tm=tm, n_e=n_e, nblk=nblk),
        grid_spec=pltpu.PrefetchScalarGridSpec(
            num_scalar_prefetch=4, grid=(t // tq,),
            in_specs=[pl.BlockSpec((tq * nblk, V7X_LANES), lambda i, *_: (i, 0))],
            out_specs=pl.BlockSpec(memory_space=pl.ANY),
            scratch_shapes=[pltpu.VMEM((tm * nblk, V7X_LANES), h2p.dtype),
                            pltpu.SemaphoreType.DMA(()), pltpu.SemaphoreType.DMA(())]),
        out_shape=jax.ShapeDtypeStruct((n_rows * nblk, V7X_LANES), h2p.dtype),
        compiler_params=pltpu.CompilerParams(dimension_semantics=("arbitrary",),
                                             has_side_effects=True,
                                             vmem_limit_bytes=V7X_VMEM_LIMIT),
        name="moe_dispatch",
    )(dest_flat, pad_off, pad_n, n_used_rows, h2p)


EXPERT_COL_BLOCK = 2048
GEMM1_COL_CHUNK = 256
MOE_TILE_LEVELS = 3
MOE_MIN_ROWS = 16
CAST_ROWS = 256


def _stream_expert_weights(te_ref, nt_ref, w_hbm, stage_ref, slot_ref, sem):
    j, i = pl.program_id(0), pl.program_id(1)
    n_pass, n_tiles = pl.num_programs(0), pl.num_programs(1)
    tn = stage_ref.shape[2]
    n_used = nt_ref[0]
    cur = te_ref[i]
    first = (i < n_used) & ((i == 0) | (cur != te_ref[jnp.maximum(i - 1, 0)]))

    def fetch(e, jj, slot):
        src = w_hbm.at[e, :, pl.ds(pl.multiple_of(jj * tn, tn), tn)]
        return pltpu.make_async_copy(src, stage_ref.at[slot], sem.at[slot])

    @pl.when((i == 0) & (j == 0))
    def _():
        slot_ref[0] = 1
        fetch(te_ref[0], 0, 0).start()

    @pl.when(first)
    def _():
        slot = 1 - slot_ref[0]
        slot_ref[0] = slot
        fetch(cur, j, slot).wait()

        def same_group(k):
            return (k < n_used) & (te_ref[jnp.minimum(k, n_tiles - 1)] == cur)
        nxt = lax.while_loop(same_group, lambda k: k + 1, i + 1)

        @pl.when(nxt < n_used)
        def _():
            fetch(te_ref[jnp.minimum(nxt, n_tiles - 1)], j, 1 - slot).start()

        @pl.when((nxt >= n_used) & (j + 1 < n_pass))
        def _():
            fetch(te_ref[0], j + 1, 1 - slot).start()

    return first, slot_ref[0]


def _cast_block(stage_ref, slot, wbf_ref):
    def cast(r, c):
        rows = pl.ds(pl.multiple_of(r * CAST_ROWS, CAST_ROWS), CAST_ROWS)
        wbf_ref[rows, :] = stage_ref[slot, rows, :].astype(BF16)
        return c
    lax.fori_loop(0, wbf_ref.shape[0] // CAST_ROWS, cast, 0)


def _gemm1_kernel(te_ref, nt_ref, tv_ref, x_ref, w_hbm, b_ref, o_ref, stage_ref, wbf_ref,
                  slot_ref, sem, *, nc, nblk):
    first, slot = _stream_expert_weights(te_ref, nt_ref, w_hbm, stage_ref, slot_ref, sem)

    def compute(rows, convert):
        words = _load_token_rows(x_ref, rows, nblk)
        x_lo, x_hi = (v.astype(BF16) for v in _unpack_bf16_pair(words))
        half = x_lo.shape[1]
        lane128 = lax.broadcasted_iota(jnp.int32, (rows, V7X_LANES), 1)
        even_idx = (2 * lane128) % V7X_LANES
        for n0 in range(0, wbf_ref.shape[1], nc):
            if convert:
                wbf_ref[:, n0:n0 + nc] = stage_ref[slot, :, n0:n0 + nc].astype(BF16)
            hb = (jnp.dot(x_lo, wbf_ref[:half, n0:n0 + nc], preferred_element_type=F32)
                  + jnp.dot(x_hi, wbf_ref[half:, n0:n0 + nc], preferred_element_type=F32)
                  + b_ref[0, :, n0:n0 + nc])
            nxt = pltpu.roll(hb, nc - 1, axis=1)
            glu = jnp.minimum(hb, SWIGLU_LIMIT)
            lin = jnp.clip(nxt, -SWIGLU_LIMIT, SWIGLU_LIMIT)
            act = glu * jax.nn.sigmoid(SWIGLU_ALPHA * glu) * (lin + 1.0)
            for c in range(nc // (2 * V7X_LANES)):
                lo_blk = act[:, (2 * c) * V7X_LANES:(2 * c + 1) * V7X_LANES]
                hi_blk = act[:, (2 * c + 1) * V7X_LANES:(2 * c + 2) * V7X_LANES]
                out = jnp.where(lane128 < V7X_LANES // 2,
                                jnp.take_along_axis(lo_blk, even_idx, axis=1),
                                jnp.take_along_axis(hi_blk, even_idx, axis=1))
                o0 = n0 // 2 + c * V7X_LANES
                o_ref[:rows, o0:o0 + V7X_LANES] = out.astype(o_ref.dtype)
        if rows < o_ref.shape[0]:
            o_ref[rows:, :] = jnp.zeros((o_ref.shape[0] - rows, o_ref.shape[1]), o_ref.dtype)

    _for_real_rows(pl.program_id(1), nt_ref, tv_ref, o_ref, o_ref.shape[0], compute,
                   first, lambda: _cast_block(stage_ref, slot, wbf_ref))


def _for_real_rows(i, nt_ref, tv_ref, o_ref, tm, compute, first, convert_all):
    used = i < nt_ref[0]
    real = tv_ref[i]
    sizes = [tm >> s for s in range(MOE_TILE_LEVELS) if (tm >> s) % MOE_MIN_ROWS == 0]

    @pl.when(first & (real <= sizes[1]))
    def _():
        convert_all()

    for n, rows in enumerate(sizes):
        fits = real <= rows
        if n + 1 < len(sizes):
            fits = fits & (real > sizes[n + 1])
        if n == 0:
            @pl.when(used & fits & first)
            def _():
                compute(rows, True)
            fits = fits & jnp.logical_not(first)

        @pl.when(used & fits)
        def _(rows=rows):
            compute(rows, False)

    @pl.when(jnp.logical_not(used))
    def _():
        o_ref[...] = jnp.zeros_like(o_ref)


def _gemm1(xin, w1, b1, tile_e, n_tiles_used, tile_rows, tm, nblk):
    n_rows = xin.shape[0] // nblk
    n_e, d, f2 = w1.shape
    tn = min(EXPERT_COL_BLOCK, f2)
    nc = min(GEMM1_COL_CHUNK, tn)
    assert f2 % tn == 0 and d % CAST_ROWS == 0
    n_tiles = n_rows // tm
    assert nc % (2 * V7X_LANES) == 0

    def used(i, nt):
        return jnp.minimum(i, nt[0] - 1)

    return pl.pallas_call(
        functools.partial(_gemm1_kernel, nc=nc, nblk=nblk),
        grid_spec=pltpu.PrefetchScalarGridSpec(
            num_scalar_prefetch=3, grid=(f2 // tn, n_tiles),
            in_specs=[pl.BlockSpec((tm * nblk, V7X_LANES),
                                   lambda j, i, te, nt, tv: (used(i, nt), 0)),
                      pl.BlockSpec(memory_space=pl.ANY),
                      pl.BlockSpec((1, 1, tn), lambda j, i, te, nt, tv: (te[used(i, nt)], 0, j))],
            out_specs=pl.BlockSpec((tm, tn // 2), lambda j, i, te, nt, tv: (i, j)),
            scratch_shapes=[pltpu.VMEM((2, d, tn), F32), pltpu.VMEM((d, tn), BF16),
                            pltpu.SMEM((1,), jnp.int32), pltpu.SemaphoreType.DMA((2,))]),
        out_shape=jax.ShapeDtypeStruct((n_rows, f2 // 2), BF16),
        compiler_params=_cparams(("arbitrary", "arbitrary")),
        name="moe_gemm1",
    )(tile_e, n_tiles_used, tile_rows, xin, w1, b1.reshape(n_e, 1, f2))


def _gemm2_kernel(te_ref, nt_ref, tv_ref, a_ref, w_hbm, b_ref, o_ref, stage_ref, wbf_ref, slot_ref,
                  sem, *, nc, nblk):
    first, slot = _stream_expert_weights(te_ref, nt_ref, w_hbm, stage_ref, slot_ref, sem)
    tm = a_ref.shape[0]

    def compute(rows, convert):
        a = a_ref[:rows, :]
        half = wbf_ref.shape[1] // 2

        def cols(n0):
            if convert:
                wbf_ref[:, n0:n0 + nc] = stage_ref[slot, :, n0:n0 + nc].astype(BF16)
            return (jnp.dot(a, wbf_ref[:, n0:n0 + nc], preferred_element_type=F32)
                    + b_ref[0, :, n0:n0 + nc])

        for n0 in range(0, half, nc):
            words = _pack_bf16_pair(cols(n0), cols(half + n0))
            for c in range(nc // V7X_LANES):
                blk = n0 // V7X_LANES + c
                o_ref[pl.ds(blk, rows, stride=nblk), :] = words[:, c * V7X_LANES:(c + 1) * V7X_LANES]
        if rows < tm:
            o_ref[rows * nblk:, :] = jnp.zeros(((tm - rows) * nblk, o_ref.shape[1]), o_ref.dtype)

    _for_real_rows(pl.program_id(1), nt_ref, tv_ref, o_ref, tm, compute,
                   first, lambda: _cast_block(stage_ref, slot, wbf_ref))


def _gemm2(act, w2, b2, tile_e, n_tiles_used, tile_rows, tm):
    n_rows, f = act.shape
    n_e, _, d = w2.shape
    tn = d
    nc = min(512, tn // 2)
    nblk = d // 2 // V7X_LANES
    assert f % CAST_ROWS == 0
    n_tiles = n_rows // tm

    def used(i, nt):
        return jnp.minimum(i, nt[0] - 1)

    return pl.pallas_call(
        functools.partial(_gemm2_kernel, nc=nc, nblk=nblk),
        grid_spec=pltpu.PrefetchScalarGridSpec(
            num_scalar_prefetch=3, grid=(d // tn, n_tiles),
            in_specs=[pl.BlockSpec((tm, f), lambda j, i, te, nt, tv: (used(i, nt), 0)),
                      pl.BlockSpec(memory_space=pl.ANY),
                      pl.BlockSpec((1, 1, tn), lambda j, i, te, nt, tv: (te[used(i, nt)], 0, j))],
            out_specs=pl.BlockSpec((tm * nblk, V7X_LANES), lambda j, i, te, nt, tv: (i, 0)),
            scratch_shapes=[pltpu.VMEM((2, f, tn), F32), pltpu.VMEM((f, tn), BF16),
                            pltpu.SMEM((1,), jnp.int32), pltpu.SemaphoreType.DMA((2,))]),
        out_shape=jax.ShapeDtypeStruct((n_rows * nblk, V7X_LANES), jnp.uint32),
        compiler_params=_cparams(("arbitrary", "arbitrary")),
        name="moe_gemm2",
    )(tile_e, n_tiles_used, tile_rows, act, w2, b2.reshape(n_e, 1, d))


def _combine_kernel(dest_ref, x1_ref, gate_ref, gt_ref, g_ref, b_ref, y_ref, o_ref, buf, sem,
                    *, alpha, tc, n_tok, nblk):
    i = pl.program_id(0)
    n_steps = pl.num_programs(0)

    def rows(tok):
        return pl.ds(pl.multiple_of(tok * nblk, nblk), nblk)

    def copy(step, slot, k, r):
        src = y_ref.at[rows(dest_ref[k * n_tok + step * tc + r])]
        return pltpu.make_async_copy(src, buf.at[slot, k, rows(r)], sem.at[slot])

    def issue(step, slot):
        def body(r, c):
            for k in range(TOP_K):
                copy(step, slot, k, r).start(priority=k % 2)
            return c
        lax.fori_loop(0, tc, body, 0, unroll=2)

    def drain(slot):
        pltpu.make_async_copy(buf.at[slot], buf.at[slot], sem.at[slot]).wait()

    slot = i % 2

    @pl.when(i == 0)
    def _():
        issue(0, 0)

    @pl.when(i + 1 < n_steps)
    def _():
        issue(i + 1, 1 - slot)

    drain(slot)
    gates = gate_ref[...]
    lo, hi = _unpack_bf16_pair(_load_token_rows(buf.at[slot, 0], tc, nblk))
    ff_lo, ff_hi = gates[:, 0:1] * lo, gates[:, 0:1] * hi
    for k in range(1, TOP_K):
        lo, hi = _unpack_bf16_pair(_load_token_rows(buf.at[slot, k], tc, nblk))
        ff_lo, ff_hi = ff_lo + gates[:, k:k + 1] * lo, ff_hi + gates[:, k:k + 1] * hi
    ff = jnp.concatenate([ff_lo, ff_hi], axis=1)
    y = alpha * x1_ref[...] + (1.0 + gt_ref[0]) * ff
    o_ref[...] = _layer_norm(y, g_ref[...], b_ref[...])


def _combine(dest_flat, x1, gates_tk, gt, g, b, y, seq, alpha):
    t, d = x1.shape
    tc = _pick(seq, (256, 128))
    per_b = seq // tc
    nblk = d // 2 // V7X_LANES
    return pl.pallas_call(
        functools.partial(_combine_kernel, alpha=alpha, tc=tc, n_tok=t, nblk=nblk),
        grid_spec=pltpu.PrefetchScalarGridSpec(
            num_scalar_prefetch=1, grid=(t // tc,),
            in_specs=[pl.BlockSpec((tc, d), lambda i, ds: (i, 0)),
                      pl.BlockSpec((tc, TOP_K), lambda i, ds: (i, 0)),
                      pl.BlockSpec((1, 1, d), lambda i, ds: (i // per_b, 0, 0)),
                      pl.BlockSpec((1, d), lambda i, ds: (0, 0)),
                      pl.BlockSpec((1, d), lambda i, ds: (0, 0)),
                      pl.BlockSpec(memory_space=pl.ANY)],
            out_specs=pl.BlockSpec((tc, d), lambda i, ds: (i, 0)),
            scratch_shapes=[pltpu.VMEM((2, TOP_K, tc * nblk, V7X_LANES), jnp.uint32),
                            pltpu.SemaphoreType.DMA((2,))]),
        out_shape=jax.ShapeDtypeStruct((t, d), F32),
        compiler_params=_cparams(("arbitrary",)),
        name="moe_combine_ln2",
    )(dest_flat, x1, gates_tk, gt[:, None, :], g.reshape(1, d), b.reshape(1, d), y)


MOE_ROW_TILE = 512


def _moe_ffn(h2p, logits_t, x1, gt_f, ln_g, ln_b, w1, b1, w2, b2, seq, alpha):
    tm = MOE_ROW_TILE
    n_e, t = logits_t.shape
    n_rows = -(-(TOP_K * t) // tm) * tm + n_e * tm
    n_tiles = n_rows // tm
    gate_t, dest, tiles, pads = _routing(logits_t, tm, n_tiles)
    dest = dest.reshape(-1)
    tile_e = tiles[0, :n_tiles]
    n_used_rows = tiles[1, :1]
    tile_rows = tiles[2, :n_tiles]
    n_used_tiles = n_used_rows // tm
    nblk = x1.shape[1] // 2 // V7X_LANES
    xin = _dispatch(h2p, dest, pads[0, :, 0], pads[1, :, 0], n_used_rows, n_rows, tm, nblk)
    act = _gemm1(xin, w1, b1, tile_e, n_used_tiles, tile_rows, tm, nblk)
    y = _gemm2(act, w2, b2, tile_e, n_used_tiles, tile_rows, tm)
    return _combine(dest, x1, gate_t.T, gt_f, ln_g, ln_b, y, seq, alpha)


def kernel(x, c, positions, w_ada, b_ada, w_in, hgrn_lb, gnorm_w, w_o, ln1_g, ln1_b,
           router_w, router_b, w1, b1, w2, b2, ln2_g, ln2_b):
    bsz, seq, d = x.shape
    depth = w_ada.shape[0]
    t = bsz * seq
    att_w = d // 2
    hg_w = d - att_w
    alpha = (2.0 * depth) ** 0.25

    lb_all = jnp.cumsum(jax.nn.softmax(hgrn_lb.astype(F32), axis=0), axis=0)
    inv = ROPE_THETA ** (-(jnp.arange(0, ROT_DIM, 2, dtype=F32) / ROT_DIM))
    lane = np.arange(V7X_LANES)
    inv_lane = inv[(lane % ATT_HEAD_DIM) % (ROT_DIM // 2)].reshape(1, V7X_LANES)
    pos_b = jnp.broadcast_to(positions.astype(F32).reshape(t, 1), (t, V7X_LANES))

    x2d = x.reshape(t, d)
    for l in range(depth):
        mod = _adaln(c, w_ada[l], b_ada[l])
        sh_a, sc_a, gt_a, sh_f, sc_f, gt_f = jnp.split(mod, 6, axis=-1)

        proj = _in_proj(x2d, sc_a, sh_a, w_in[l], seq)
        proj3 = proj.reshape(bsz, seq, proj.shape[1])
        q_hp, k_hp, v_hp = _qk_rope(proj, pos_b, inv_lane, bsz, seq, att_w)
        att = _attention(q_hp, k_hp, v_hp)
        rec = _hgrn2(proj3, lb_all[l], gnorm_w[l], att_w, hg_w).reshape(t, hg_w)

        x1, h2p, logits_t = _out_proj(att, rec, x2d, w_o[l], gt_a, sc_f, sh_f, ln1_g[l], ln1_b[l],
                                      router_w[l], router_b[l], seq, alpha)

        x2d = _moe_ffn(h2p, logits_t, x1, gt_f, ln2_g[l], ln2_b[l], w1[l], b1[l], w2[l], b2[l],
                       seq, alpha)
    return x2d.reshape(bsz, seq, d)
```

```python
import functools

import numpy as np
import jax
import jax.numpy as jnp
from jax import lax
from jax.experimental import pallas as pl
from jax.experimental.pallas import tpu as pltpu

F32 = jnp.float32
BF16 = jnp.bfloat16

V7X_LANES = 128
V7X_VMEM_LIMIT = 56 * 1024 * 1024

ATT_HEAD_DIM = 64
DILATED_PAIRS = ((128, 1), (512, 4), (2048, 16))
ATT_BLOCK = 128
ROT_DIM = ATT_HEAD_DIM // 4
ROPE_THETA = 500000.0
HG_EXPAND = 128
HG_CHUNK = 64
HG_SUB = 16
TOP_K = 4
SWIGLU_ALPHA = 1.702
SWIGLU_LIMIT = 7.0
LN_EPS = 1e-5
RMS_EPS = 1e-6
NEG_INF = -1e30

NT_DIMS = (((1,), (1,)), ((), ()))


def _pick(n, candidates):
    for c in candidates:
        if n % c == 0:
            return c
    raise ValueError(f"no tile in {candidates} divides {n}")


def _cparams(sem, vmem=V7X_VMEM_LIMIT, flags=None):
    return pltpu.CompilerParams(dimension_semantics=sem, vmem_limit_bytes=vmem, flags=flags)


def _adaln_kernel(c_ref, w_ref, b_ref, o_ref):
    c = c_ref[...]
    s = c * jax.nn.sigmoid(c)
    o_ref[...] = jnp.dot(s.astype(BF16), w_ref[...].astype(BF16),
                         preferred_element_type=F32) + b_ref[...]


def _adaln(c, w, b):
    bsz, d = c.shape
    n = w.shape[1]
    rows = 8
    cp = jnp.zeros((rows, d), F32).at[:bsz].set(c)
    tn = _pick(n, (1024, 512, 256, 128))
    out = pl.pallas_call(
        _adaln_kernel,
        grid=(n // tn,),
        in_specs=[pl.BlockSpec((rows, d), lambda j: (0, 0)),
                  pl.BlockSpec((d, tn), lambda j: (0, j)),
                  pl.BlockSpec((1, tn), lambda j: (0, j))],
        out_specs=pl.BlockSpec((rows, tn), lambda j: (0, j)),
        out_shape=jax.ShapeDtypeStruct((rows, n), F32),
        compiler_params=_cparams(("arbitrary",)),
        name="adaln",
    )(cp, w, b.reshape(1, n))
    return out[:bsz]


def _in_proj_kernel(x_ref, sc_ref, sh_ref, w_ref, o_ref, wbf_ref):
    @pl.when(pl.program_id(1) == 0)
    def _():
        wbf_ref[...] = w_ref[...].astype(BF16)

    h = x_ref[...] * (1.0 + sc_ref[0]) + sh_ref[0]
    o_ref[...] = jnp.dot(h.astype(BF16), wbf_ref[...],
                         preferred_element_type=F32).astype(o_ref.dtype)


def _in_proj(x2d, sc, sh, w, seq):
    t, d = x2d.shape
    nc = w.shape[1]
    tm = _pick(seq, (512, 256, 128))
    tn = _pick(nc, (1024, 512, 256, 128))
    per_b = seq // tm
    vec = pl.BlockSpec((1, 1, d), lambda j, i: (i // per_b, 0, 0))
    return pl.pallas_call(
        _in_proj_kernel,
        grid=(nc // tn, t // tm),
        in_specs=[pl.BlockSpec((tm, d), lambda j, i: (i, 0)), vec, vec,
                  pl.BlockSpec((d, tn), lambda j, i: (0, j))],
        out_specs=pl.BlockSpec((tm, tn), lambda j, i: (i, j)),
        out_shape=jax.ShapeDtypeStruct((t, nc), BF16),
        scratch_shapes=[pltpu.VMEM((d, tn), BF16)],
        compiler_params=_cparams(("arbitrary", "arbitrary")),
        name="in_proj",
    )(x2d, sc[:, None, :], sh[:, None, :], w)


def _rope_kernel(q_ref, k_ref, v_ref, pos_ref, inv_ref, qo_ref, ko_ref, vo_ref):
    tm = pos_ref.shape[0]
    lane = lax.broadcasted_iota(jnp.int32, (tm, V7X_LANES), 1)
    lh = lane % ATT_HEAD_DIM
    half = ROT_DIM // 2
    ang = pos_ref[...] * inv_ref[...]
    cs = jnp.where(lh < ROT_DIM, jnp.cos(ang), 1.0)
    sn = jnp.sin(ang)
    sn = jnp.where(lh < half, -sn, jnp.where(lh < ROT_DIM, sn, 0.0))

    def rope(t):
        swapped = jnp.where(lh < half,
                            pltpu.roll(t, V7X_LANES - half, axis=1),
                            pltpu.roll(t, half, axis=1))
        return t * cs + swapped * sn

    for h in range(qo_ref.shape[1]):
        lanes = slice(h * V7X_LANES, (h + 1) * V7X_LANES)
        qo_ref[0, h] = rope(q_ref[:, lanes].astype(F32)) * (ATT_HEAD_DIM ** -0.5)
        ko_ref[0, h] = rope(k_ref[:, lanes].astype(F32))
        vo_ref[0, h] = v_ref[:, lanes].astype(F32)


def _qk_rope(proj, pos_b, inv_lane, bsz, seq, att_w):
    t = proj.shape[0]
    hp = att_w // V7X_LANES
    tm = _pick(seq, (256, 128))
    per_b = seq // tm
    out_spec = pl.BlockSpec((1, hp, tm, V7X_LANES), lambda i: (i // per_b, 0, i % per_b, 0))
    shp = jax.ShapeDtypeStruct((bsz, hp, seq, V7X_LANES), F32)
    return pl.pallas_call(
        _rope_kernel,
        grid=(t // tm,),
        in_specs=[pl.BlockSpec((tm, att_w), lambda i: (i, 0)),
                  pl.BlockSpec((tm, att_w), lambda i: (i, 1)),
                  pl.BlockSpec((tm, att_w), lambda i: (i, 2)),
                  pl.BlockSpec((tm, V7X_LANES), lambda i: (i, 0)),
                  pl.BlockSpec((1, V7X_LANES), lambda i: (0, 0))],
        out_specs=[out_spec, out_spec, out_spec],
        out_shape=[shp, shp, shp],
        compiler_params=_cparams(("arbitrary",)),
        name="qk_rope",
    )(proj, proj, proj, pos_b, inv_lane)


ATT_TASKS_PER_STEP = 8


def _attn_kernel(q_in, k_in, v_in, out_ref, qs, ks, vs, sm, no_lo, nl_lo, no_s, nl_s, nat, *, seq):
    qn = ATT_BLOCK
    n_task = seq // qn
    n_head = V7X_LANES // ATT_HEAD_DIM
    gc = min(ATT_TASKS_PER_STEP, n_task)
    (_, d_lo), (_, dm), (_, d_hi) = DILATED_PAIRS
    g_hi = d_hi // dm
    rows_m = seq // dm
    lane = lax.broadcasted_iota(jnp.int32, (qn, V7X_LANES), 1)
    qi = lax.broadcasted_iota(jnp.int32, (gc, qn, 2 * qn), 1)
    kj = lax.broadcasted_iota(jnp.int32, (gc, qn, 2 * qn), 2)
    dist = qi + qn - kj
    band = (dist >= 0) & (dist <= qn)
    zeros = jnp.zeros((qn, V7X_LANES), BF16)
    srcs = (q_in, k_in, v_in)

    for a, src in enumerate(srcs):
        for r in range(dm):
            sm[a, r] = src[0, 0, pl.ds(r, rows_m, stride=dm), :]

    def task_index(d, r, n):
        if d == d_lo:
            return None, slice(n * qn, (n + 1) * qn)
        if d == dm:
            return r, slice(n * qn, (n + 1) * qn)
        return r % dm, pl.ds(r // dm + n * qn * g_hi, qn, stride=g_hi)

    for bi, (window, d) in enumerate(DILATED_PAIRS):
        nb = seq // (d * qn)
        tasks = [(r, n) for r in range(d) for n in range(nb)]
        for g, (r, n) in enumerate(tasks):
            sel, rows = task_index(d, r, n)

            def operand(a, sel=sel, rows=rows):
                return srcs[a][0, 0, rows, :] if sel is None else sm[a, sel, rows, :]

            qv = operand(0)
            for h in range(n_head):
                qs[h, g] = jnp.where(lane // ATT_HEAD_DIM == h, qv, 0.0).astype(BF16)
            for a, dst in ((1, ks), (2, vs)):
                blk = operand(a).astype(BF16)
                dst[g, qn:, :] = blk
                if n + 1 < nb:
                    dst[g + 1, :qn, :] = blk
                if n == 0:
                    dst[g, :qn, :] = zeros

        for g0 in range(0, n_task, gc):
            gsl = slice(g0, g0 + gc)
            gidx = g0 + lax.broadcasted_iota(jnp.int32, (gc, qn, 2 * qn), 0)
            valid = band & ((gidx % nb != 0) | (kj >= qn))
            k = ks[gsl]
            v = vs[gsl]
            o = None
            for h in range(n_head):
                s = jnp.einsum("gqd,gkd->gqk", qs[h, gsl], k, preferred_element_type=F32)
                s = jnp.where(valid, s, NEG_INF)
                m = jnp.max(s, axis=-1, keepdims=True)
                p = jnp.exp(s - m)
                den = jnp.sum(p, axis=-1, keepdims=True)
                oh = jnp.einsum("gqk,gkd->gqd", p.astype(BF16), v,
                                preferred_element_type=F32) / den
                lh = jnp.broadcast_to(m + jnp.log(den), oh.shape)
                if o is None:
                    o, lse = oh, lh
                else:
                    in_head = lane[None] // ATT_HEAD_DIM == h
                    o, lse = jnp.where(in_head, oh, o), jnp.where(in_head, lh, lse)
            for t in range(gc):
                sel, rows = task_index(d, *tasks[g0 + t])
                if sel is None:
                    no_lo[rows, :] = o[t]
                    nl_lo[rows, :] = lse[t]
                else:
                    no_s[bi - 1, sel, rows, :] = o[t]
                    nl_s[bi - 1, sel, rows, :] = lse[t]

    for r in range(dm):
        for c in range(rows_m // qn):
            rows = slice(c * qn, (c + 1) * qn)
            seq_rows = pl.ds(r + dm * c * qn, qn, stride=dm)
            ls = [nl_lo[seq_rows, :], nl_s[0, r, rows, :], nl_s[1, r, rows, :]]
            os_ = [no_lo[seq_rows, :], no_s[0, r, rows, :], no_s[1, r, rows, :]]
            mx = functools.reduce(jnp.maximum, ls)
            ws = [jnp.exp(l - mx) for l in ls]
            tot = functools.reduce(lambda x, y: x + y, ws)
            acc = functools.reduce(lambda x, y: x + y, [w * o for w, o in zip(ws, os_)])
            nat[seq_rows, :] = acc / tot
    out_ref[0, 0] = nat[...].astype(out_ref.dtype)


def _attention(q_hp, k_hp, v_hp):
    bsz, hp, seq, _ = q_hp.shape
    (_, d_lo), (_, dm), (_, d_hi) = DILATED_PAIRS
    assert d_lo == 1 and d_hi % dm == 0
    for window, d in DILATED_PAIRS:
        assert window // d == ATT_BLOCK and seq % (d * ATT_BLOCK) == 0
    n_task = seq // ATT_BLOCK
    assert n_task % min(ATT_TASKS_PER_STEP, n_task) == 0
    n_head = V7X_LANES // ATT_HEAD_DIM
    streams = (dm, seq // dm, V7X_LANES)
    spec = pl.BlockSpec((1, 1, seq, V7X_LANES), lambda b, h: (b, h, 0, 0))
    return pl.pallas_call(
        functools.partial(_attn_kernel, seq=seq),
        grid=(bsz, hp),
        in_specs=[spec, spec, spec],
        out_specs=spec,
        out_shape=jax.ShapeDtypeStruct((bsz, hp, seq, V7X_LANES), BF16),
        scratch_shapes=[pltpu.VMEM((n_head, n_task, ATT_BLOCK, V7X_LANES), BF16),
                        pltpu.VMEM((n_task, 2 * ATT_BLOCK, V7X_LANES), BF16),
                        pltpu.VMEM((n_task, 2 * ATT_BLOCK, V7X_LANES), BF16),
                        pltpu.VMEM((3,) + streams, F32),
                        pltpu.VMEM((seq, V7X_LANES), F32), pltpu.VMEM((seq, V7X_LANES), F32),
                        pltpu.VMEM((2,) + streams, F32), pltpu.VMEM((2,) + streams, F32),
                        pltpu.VMEM((seq, V7X_LANES), F32)],
        compiler_params=_cparams(("arbitrary", "arbitrary")),
        name="dilated_attn",
    )(q_hp, k_hp, v_hp)


def _hgrn_kernel(q_ref, f_ref, i_ref, g_ref, lb_ref, gw_ref, sums_ref, o_ref,
                 st_ref, hl_ref, kk_ref, cum_ref, *, heads, ts):
    @pl.when(pl.program_id(2) == 0)
    def _():
        st_ref[...] = jnp.zeros_like(st_ref)

    c_len = HG_CHUNK
    lanes = heads * HG_EXPAND
    lb = lb_ref[...]
    for c in range(ts // c_len):
        rows = slice(c * c_len, (c + 1) * c_len)
        f = lb + (1.0 - lb) * jax.nn.sigmoid(f_ref[0, rows, :].astype(F32))
        kk_ref[rows, :] = 1.0 - f
        hi, lo = _bf16_split(jnp.log(f))
        hl_ref[rows, :lanes] = hi
        hl_ref[rows, lanes:] = lo
    both = jnp.dot(sums_ref[...], hl_ref[...], preferred_element_type=F32)
    cum_ref[...] = both[:, :lanes] + both[:, lanes:]

    n_sub = c_len // HG_SUB

    def sub_block_anchor(b):
        mids = [b[i * HG_SUB + HG_SUB // 2:i * HG_SUB + HG_SUB // 2 + 1, :] for i in range(n_sub)]
        return jnp.concatenate([jnp.broadcast_to(m, (HG_SUB, m.shape[1])) for m in mids], axis=0)

    gi = lax.broadcasted_iota(jnp.int32, (heads * n_sub, HG_SUB, c_len), 0) % n_sub
    qi = lax.broadcasted_iota(jnp.int32, (heads * n_sub, HG_SUB, c_len), 1)
    si = lax.broadcasted_iota(jnp.int32, (heads * n_sub, HG_SUB, c_len), 2)
    causal = si <= gi * HG_SUB + qi
    key_row = lax.broadcasted_iota(jnp.int32, (c_len, HG_EXPAND), 0)

    for c in range(ts // c_len):
        rows = slice(c * c_len, (c + 1) * c_len)
        qts, kts, vbs = [], [], []
        for h in range(heads):
            lsl = slice(h * HG_EXPAND, (h + 1) * HG_EXPAND)
            b = cum_ref[rows, lsl]
            anchor = sub_block_anchor(b)
            kk = kk_ref[rows, lsl]
            qt = (q_ref[0, rows, lsl].astype(F32) * jnp.exp(b - anchor)).astype(BF16)
            qts.append(qt.reshape(n_sub, HG_SUB, HG_EXPAND))
            vb = i_ref[0, rows, lsl].astype(BF16)
            for i in range(n_sub):
                hi_r = (i + 1) * HG_SUB
                kt = kk * jnp.exp(anchor[i * HG_SUB:i * HG_SUB + 1, :] - b)
                kts.append(jnp.where(key_row < hi_r, kt, 0.0).astype(BF16))
                vbs.append(vb)
        a = jnp.einsum("gqk,gsk->gqs", jnp.concatenate(qts, axis=0), jnp.stack(kts),
                       preferred_element_type=F32)
        a = jnp.where(causal, a, 0.0).astype(BF16)
        o_intra = jnp.einsum("gqs,gsv->gqv", a, jnp.stack(vbs), preferred_element_type=F32)

        for h in range(heads):
            lsl = slice(h * HG_EXPAND, (h + 1) * HG_EXPAND)
            b = cum_ref[rows, lsl]
            b_last = jnp.broadcast_to(b[c_len - 1:c_len, :], b.shape)
            kk = kk_ref[rows, lsl]
            q = q_ref[0, rows, lsl].astype(F32)
            v = i_ref[0, rows, lsl].astype(F32)
            st = st_ref[h]
            o_inter = lax.dot_general((q * jnp.exp(b)).astype(BF16), st.astype(BF16), NT_DIMS,
                                      preferred_element_type=F32)
            o = o_inter + o_intra[h * n_sub:(h + 1) * n_sub].reshape(c_len, HG_EXPAND)
            kl = kk * jnp.exp(b_last - b)
            upd = jnp.dot(v.T.astype(BF16), kl.astype(BF16), preferred_element_type=F32)
            st_ref[h] = st * jnp.exp(b_last[0:1, :]) + upd
            o = o * lax.rsqrt(jnp.mean(o * o, axis=-1, keepdims=True) + RMS_EPS)
            g = g_ref[0, rows, lsl].astype(F32)
            o = o * gw_ref[:, lsl] * (g * jax.nn.sigmoid(g))
            o_ref[0, rows, lsl] = o.astype(o_ref.dtype)


def _hgrn_cumsum_matrix(ts):
    t = np.arange(ts)[:, None]
    s = np.arange(ts)[None, :]
    return (((t // HG_CHUNK) == (s // HG_CHUNK)) & (s <= t)).astype(np.float32)


def _hgrn2(proj3, lb, gw, att_w, hg_w):
    bsz, seq, _ = proj3.shape
    n_heads = hg_w // HG_EXPAND
    heads = _pick(n_heads, (4, 2, 1))
    lanes = heads * HG_EXPAND
    ts = _pick(seq, (256, 128, 64))
    base = 3 * att_w
    assert base % lanes == 0 and hg_w % lanes == 0

    def seg(k):
        off = (base + k * hg_w) // lanes
        return pl.BlockSpec((1, ts, lanes), lambda b, h, s: (b, s, off + h))

    vec = pl.BlockSpec((1, lanes), lambda b, h, s: (0, h))
    sums = jnp.asarray(_hgrn_cumsum_matrix(ts), BF16)
    return pl.pallas_call(
        functools.partial(_hgrn_kernel, heads=heads, ts=ts),
        grid=(bsz, hg_w // lanes, seq // ts),
        in_specs=[seg(0), seg(1), seg(2), seg(3), vec, vec,
                  pl.BlockSpec((ts, ts), lambda b, h, s: (0, 0))],
        out_specs=pl.BlockSpec((1, ts, lanes), lambda b, h, s: (b, s, h)),
        out_shape=jax.ShapeDtypeStruct((bsz, seq, hg_w), BF16),
        scratch_shapes=[pltpu.VMEM((heads, HG_EXPAND, HG_EXPAND), F32),
                        pltpu.VMEM((ts, 2 * lanes), BF16),
                        pltpu.VMEM((ts, lanes), F32),
                        pltpu.VMEM((ts, lanes), F32)],
        compiler_params=_cparams(("arbitrary", "arbitrary", "arbitrary")),
        name="hgrn2",
    )(proj3, proj3, proj3, proj3, lb.reshape(1, hg_w), gw.reshape(1, hg_w), sums)


def _layer_norm(y, g, b):
    mu = jnp.mean(y, axis=-1, keepdims=True)
    yc = y - mu
    var = jnp.mean(yc * yc, axis=-1, keepdims=True)
    return yc * lax.rsqrt(var + LN_EPS) * g + b


def _bf16_split(x):
    hi = x.astype(BF16)
    lo = (x - hi.astype(F32)).astype(BF16)
    return hi, lo


def _pack_bf16_pair(lo, hi):
    lo_bits = lax.bitcast_convert_type(lo.astype(BF16).astype(F32), jnp.uint32)
    hi_bits = lax.bitcast_convert_type(hi.astype(BF16).astype(F32), jnp.uint32)
    return (lo_bits >> 16) | (hi_bits & jnp.uint32(0xFFFF0000))


def _store_token_rows(ref, tok0, words):
    n, width = words.shape
    nblk = width // V7X_LANES
    for c in range(nblk):
        ref[pl.ds(tok0 * nblk + c, n, stride=nblk), :] = words[:, c * V7X_LANES:(c + 1) * V7X_LANES]


def _load_token_rows(ref, n, nblk):
    return jnp.concatenate([ref[pl.ds(c, n, stride=nblk), :] for c in range(nblk)], axis=1)


def _unpack_bf16_pair(words):
    lo = lax.bitcast_convert_type(words << 16, F32)
    hi = lax.bitcast_convert_type(words & jnp.uint32(0xFFFF0000), F32)
    return lo, hi


def _load_weight_bf16(w_hbm, wbf_ref, stage_ref, sem):
    rows = stage_ref.shape[1]
    n_slab = w_hbm.shape[0] // rows

    def fetch(s, slot):
        return pltpu.make_async_copy(w_hbm.at[pl.ds(s * rows, rows)], stage_ref.at[slot],
                                     sem.at[slot])

    fetch(0, 0).start()
    for s in range(n_slab):
        if s + 1 < n_slab:
            fetch(s + 1, (s + 1) % 2).start()
        fetch(s, s % 2).wait()
        wbf_ref[s * rows:(s + 1) * rows, :] = stage_ref[s % 2].astype(BF16)


def _out_proj_kernel(att_ref, rec_ref, x_ref, wo_hbm, gt_ref, sc_ref, sh_ref, g_ref, b_ref,
                     rw_ref, rb_ref, x1_ref, h2_ref, lg_ref, wobf_ref, rwhi_ref, rwlo_ref,
                     stage_ref, hi_ref, lo_ref, mix_a, mix_b, sem, *, alpha, att_w):
    i = pl.program_id(0)

    @pl.when(i == 0)
    def _():
        _load_weight_bf16(wo_hbm, wobf_ref, stage_ref, sem)
        hi, lo = _bf16_split(rw_ref[...])
        rwhi_ref[...] = hi
        rwlo_ref[...] = lo
        mix_b[...] = jnp.zeros_like(mix_b)

    def step(mix_out, mix_in):
        att = jnp.concatenate([att_ref[0, h] for h in range(att_ref.shape[1])], axis=1)
        mix_out[...] = (jnp.dot(att, wobf_ref[:att_w, :], preferred_element_type=F32)
                        + jnp.dot(rec_ref[...], wobf_ref[att_w:, :], preferred_element_type=F32))
        half = x_ref.shape[1] // 2
        for r in range(x_ref.shape[0] // LN_ROW_GROUP):
            rows = slice(r * LN_ROW_GROUP, (r + 1) * LN_ROW_GROUP)
            y = alpha * x_ref[rows, :] + (1.0 + gt_ref[0]) * mix_in[rows, :]
            x1 = _layer_norm(y, g_ref[...], b_ref[...])
            x1_ref[rows, :] = x1
            h2 = x1 * (1.0 + sc_ref[0]) + sh_ref[0]
            hi, lo = _bf16_split(h2)
            hi_ref[rows, :] = hi
            lo_ref[rows, :] = lo
            _store_token_rows(h2_ref, r * LN_ROW_GROUP,
                              _pack_bf16_pair(h2[:, :half], h2[:, half:]))
        lg = (jnp.dot(hi_ref[...], rwhi_ref[...], preferred_element_type=F32)
              + jnp.dot(lo_ref[...], rwhi_ref[...], preferred_element_type=F32)
              + jnp.dot(hi_ref[...], rwlo_ref[...], preferred_element_type=F32))
        lg_ref[...] = lg.T[:lg_ref.shape[0], :] + rb_ref[...]

    @pl.when(i % 2 == 0)
    def _():
        step(mix_a, mix_b)

    @pl.when(i % 2 == 1)
    def _():
        step(mix_b, mix_a)


OUT_PROJ_STAGE_ROWS = 256
LN_ROW_GROUP = 16


def _out_proj(att, rec, x2d, wo, gt, sc, sh, g, b, rw, rb, seq, alpha):
    t, d = x2d.shape
    hp = att.shape[1]
    att_w = hp * V7X_LANES
    n_e = rw.shape[1]
    assert n_e <= V7X_LANES and d % OUT_PROJ_STAGE_ROWS == 0
    rw_pad = jnp.zeros((d, V7X_LANES), F32).at[:, :n_e].set(rw)
    nblk = d // 2 // V7X_LANES
    tm = _pick(seq, (512, 256, 128))
    per_b = seq // tm
    n_tile = t // tm
    mm = lambda i: jnp.minimum(i, n_tile - 1)
    ep = lambda i: jnp.maximum(i - 1, 0)
    vec3 = pl.BlockSpec((1, 1, d), lambda i: (ep(i) // per_b, 0, 0))
    full = lambda shape: pl.BlockSpec(shape, lambda i: (0,) * len(shape))
    return pl.pallas_call(
        functools.partial(_out_proj_kernel, alpha=alpha, att_w=att_w),
        grid=(n_tile + 1,),
        in_specs=[pl.BlockSpec((1, hp, tm, V7X_LANES),
                               lambda i: (mm(i) // per_b, 0, mm(i) % per_b, 0)),
                  pl.BlockSpec((tm, d - att_w), lambda i: (mm(i), 0)),
                  pl.BlockSpec((tm, d), lambda i: (ep(i), 0)),
                  pl.BlockSpec(memory_space=pl.ANY), vec3, vec3, vec3, full((1, d)), full((1, d)),
                  full((d, V7X_LANES)), full((n_e, 1))],
        out_specs=[pl.BlockSpec((tm, d), lambda i: (ep(i), 0)),
                   pl.BlockSpec((tm * nblk, V7X_LANES), lambda i: (ep(i), 0)),
                   pl.BlockSpec((n_e, tm), lambda i: (0, ep(i)))],
        out_shape=[jax.ShapeDtypeStruct((t, d), F32),
                   jax.ShapeDtypeStruct((t * nblk, V7X_LANES), jnp.uint32),
                   jax.ShapeDtypeStruct((n_e, t), F32)],
        scratch_shapes=[pltpu.VMEM((d, d), BF16), pltpu.VMEM((d, V7X_LANES), BF16),
                        pltpu.VMEM((d, V7X_LANES), BF16),
                        pltpu.VMEM((2, OUT_PROJ_STAGE_ROWS, d), F32),
                        pltpu.VMEM((tm, d), BF16), pltpu.VMEM((tm, d), BF16),
                        pltpu.VMEM((tm, d), F32), pltpu.VMEM((tm, d), F32),
                        pltpu.SemaphoreType.DMA((2,))],
        compiler_params=_cparams(("arbitrary",)),
        name="out_proj_ln1",
    )(att, rec, x2d, wo, gt[:, None, :], sc[:, None, :], sh[:, None, :],
      g.reshape(1, d), b.reshape(1, d), rw_pad, rb.reshape(n_e, 1))


def _routing_kernel(lg_ref, tri_ref, low_ref, gate_ref, dest_ref, tile_ref, pad_ref, sel_scr, rk_scr,
                    *, blk, tm):
    n_e, t = lg_ref.shape
    eidx = lax.broadcasted_iota(jnp.int32, (n_e, t), 0)
    cur = lg_ref[...]
    vals, idxs = [], []
    for _ in range(TOP_K):
        m = jnp.max(cur, axis=0, keepdims=True)
        ik = jnp.min(jnp.where(cur == m, eidx, n_e), axis=0, keepdims=True)
        cur = jnp.where(eidx == ik, -jnp.inf, cur)
        vals.append(m)
        idxs.append(ik)
    es = [jnp.exp(v - vals[0]) for v in vals]
    den = functools.reduce(lambda a, b: a + b, es)
    for k in range(TOP_K):
        gate_ref[k:k + 1, :] = es[k] / den
    sel = functools.reduce(lambda a, b: a | b, [eidx == ik for ik in idxs])
    sel_scr[...] = jnp.where(sel, 1.0, 0.0)

    tri = tri_ref[...]
    carry = jnp.zeros((n_e, 1), F32)
    for j in range(t // blk):
        sb = sel_scr[:, j * blk:(j + 1) * blk]
        pre = jnp.dot(sb.astype(BF16), tri, preferred_element_type=F32)
        rk_scr[:, j * blk:(j + 1) * blk] = pre + carry
        carry = carry + jnp.sum(sb, axis=1, keepdims=True)

    counts = jnp.broadcast_to(carry, (n_e, V7X_LANES))
    padded = jnp.floor((counts + (tm - 1)) * (1.0 / tm)) * tm
    pends = jnp.dot(low_ref[...], padded, precision=lax.Precision.HIGHEST,
                    preferred_element_type=F32)
    pstarts = pends - padded
    row0 = rk_scr[...] + pstarts[:, 0:1]
    for k in range(TOP_K):
        dest_ref[k:k + 1, :] = jnp.sum(jnp.where(eidx == idxs[k], row0, 0.0), axis=0,
                                       keepdims=True).astype(jnp.int32)
    starts = (lax.broadcasted_iota(jnp.int32, (n_e, tile_ref.shape[1]), 1) * tm).astype(F32)
    owner = jnp.sum(jnp.where(pends[:, 0:1] <= starts, 1.0, 0.0), axis=0, keepdims=True)
    tile_ref[0:1, :] = jnp.minimum(owner, n_e - 1.0).astype(jnp.int32)
    tile_ref[1:2, :] = jnp.broadcast_to(jnp.max(pends[:, 0:1], axis=0, keepdims=True),
                                        (1, tile_ref.shape[1])).astype(jnp.int32)
    e_tile = lax.broadcasted_iota(jnp.int32, starts.shape, 0).astype(F32)
    row_end = jnp.sum(jnp.where(e_tile == owner, (pstarts + counts)[:, 0:1], 0.0), axis=0,
                      keepdims=True)
    tile_ref[2:3, :] = jnp.clip(row_end - starts[0:1, :], 0.0, float(tm)).astype(jnp.int32)
    pad_ref[0] = (pstarts + counts).astype(jnp.int32)
    pad_ref[1] = (padded - counts).astype(jnp.int32)


def _routing(logits_t, tm, n_tiles):
    n_e, t = logits_t.shape
    assert tm & (tm - 1) == 0
    blk = _pick(t, (256, 128))
    tri = jnp.asarray(np.triu(np.ones((blk, blk), np.float32), 1), BF16)
    low = jnp.asarray(np.tril(np.ones((n_e, n_e), np.float32)))
    ntp = -(-n_tiles // V7X_LANES) * V7X_LANES
    full = lambda shape: pl.BlockSpec(shape, lambda: (0,) * len(shape))
    return pl.pallas_call(
        functools.partial(_routing_kernel, blk=blk, tm=tm),
        in_specs=[full((n_e, t)), full((blk, blk)), full((n_e, n_e))],
        out_specs=[full((TOP_K, t)), full((TOP_K, t)), full((3, ntp)), full((2, n_e, V7X_LANES))],
        out_shape=[jax.ShapeDtypeStruct((TOP_K, t), F32),
                   jax.ShapeDtypeStruct((TOP_K, t), jnp.int32),
                   jax.ShapeDtypeStruct((3, ntp), jnp.int32),
                   jax.ShapeDtypeStruct((2, n_e, V7X_LANES), jnp.int32)],
        scratch_shapes=[pltpu.VMEM((n_e, t), F32), pltpu.VMEM((n_e, t), F32)],
        compiler_params=pltpu.CompilerParams(vmem_limit_bytes=V7X_VMEM_LIMIT),
        name="routing",
    )(logits_t, tri, low)


def _dispatch_kernel(dest_ref, pad_off_ref, pad_n_ref, nrow_ref, h_ref, o_ref, zero_ref, sem, zsem,
                     *, n_tok, tq, tm, n_e, nblk):
    step = pl.program_id(0)
    base = step * tq

    def rows(tok, n=1):
        return pl.ds(pl.multiple_of(tok * nblk, nblk), n * nblk)

    def fill(off, n):
        return pltpu.make_async_copy(zero_ref.at[rows(0, n)], o_ref.at[rows(off, n)], zsem)

    pieces = [1 << s for s in range(tm.bit_length() - 2, -1, -1)]

    def pad_rows(wait):
        def per_expert(e, c):
            off = pad_off_ref[e]
            n = pad_n_ref[e]
            for p in pieces:
                hit = (n & p) != 0

                @pl.when(hit)
                def _(off=off, p=p):
                    cp = fill(off, p)
                    cp.wait() if wait else cp.start()

                off = off + jnp.where(hit, p, 0)
            return c
        lax.fori_loop(0, n_e, per_expert, 0)

        def per_tile(i, c):
            cp = fill(i * tm, tm)
            cp.wait() if wait else cp.start()
            return c
        lax.fori_loop(nrow_ref[0] // tm, o_ref.shape[0] // (tm * nblk), per_tile, 0)

    @pl.when(step == 0)
    def _():
        zero_ref[...] = jnp.zeros_like(zero_ref)
        pad_rows(False)

    def start(r, c):
        for k in range(TOP_K):
            dst = o_ref.at[rows(dest_ref[k * n_tok + base + r])]
            pltpu.make_async_copy(h_ref.at[rows(r)], dst, sem).start(priority=k % 2)
        return c

    lax.fori_loop(0, tq, start, 0, unroll=2)
    for _ in range(TOP_K):
        pltpu.make_async_copy(h_ref, h_ref, sem).wait()

    @pl.when(step == 0)
    def _():
        pad_rows(True)


def _dispatch(h2p, dest_flat, pad_off, pad_n, n_used_rows, n_rows, tm, nblk):
    t = h2p.shape[0] // nblk
    n_e = pad_off.shape[0]
    tq = _pick(t, (256, 128))
    return pl.pallas_call(
        functools.partial(_dispatch_kernel, n_tok=t, tq=tq, tm=tm, n_e=n_e, nblk=nblk),
        grid_spec=pltpu.PrefetchScalarGridSpec(
            num_scalar_prefetch=4, grid=(t // tq,),
            in_specs=[pl.BlockSpec((tq * nblk, V7X_LANES), lambda i, *_: (i, 0))],
            out_specs=pl.BlockSpec(memory_space=pl.ANY),
            scratch_shapes=[pltpu.VMEM((tm * nblk, V7X_LANES), h2p.dtype),
                            pltpu.SemaphoreType.DMA(()), pltpu.SemaphoreType.DMA(())]),
        out_shape=jax.ShapeDtypeStruct((n_rows * nblk, V7X_LANES), h2p.dtype),
        compiler_params=pltpu.CompilerParams(dimension_semantics=("arbitrary",),
                                             has_side_effects=True,
                                             vmem_limit_bytes=V7X_VMEM_LIMIT),
        name="moe_dispatch",
    )(dest_flat, pad_off, pad_n, n_used_rows, h2p)


EXPERT_COL_BLOCK = 2048
GEMM1_COL_CHUNK = 256
MOE_TILE_LEVELS = 3
MOE_MIN_ROWS = 16
CAST_ROWS = 256


def _stream_expert_weights(te_ref, nt_ref, w_hbm, stage_ref, slot_ref, sem):
    j, i = pl.program_id(0), pl.program_id(1)
    n_pass, n_tiles = pl.num_programs(0), pl.num_programs(1)
    tn = stage_ref.shape[2]
    n_used = nt_ref[0]
    cur = te_ref[i]
    first = (i < n_used) & ((i == 0) | (cur != te_ref[jnp.maximum(i - 1, 0)]))

    def fetch(e, jj, slot):
        src = w_hbm.at[e, :, pl.ds(pl.multiple_of(jj * tn, tn), tn)]
        return pltpu.make_async_copy(src, stage_ref.at[slot], sem.at[slot])

    @pl.when((i == 0) & (j == 0))
    def _():
        slot_ref[0] = 1
        fetch(te_ref[0], 0, 0).start()

    @pl.when(first)
    def _():
        slot = 1 - slot_ref[0]
        slot_ref[0] = slot
        fetch(cur, j, slot).wait()

        def same_group(k):
            return (k < n_used) & (te_ref[jnp.minimum(k, n_tiles - 1)] == cur)
        nxt = lax.while_loop(same_group, lambda k: k + 1, i + 1)

        @pl.when(nxt < n_used)
        def _():
            fetch(te_ref[jnp.minimum(nxt, n_tiles - 1)], j, 1 - slot).start()

        @pl.when((nxt >= n_used) & (j + 1 < n_pass))
        def _():
            fetch(te_ref[0], j + 1, 1 - slot).start()

    return first, slot_ref[0]


def _cast_block(stage_ref, slot, wbf_ref):
    def cast(r, c):
        rows = pl.ds(pl.multiple_of(r * CAST_ROWS, CAST_ROWS), CAST_ROWS)
        wbf_ref[rows, :] = stage_ref[slot, rows, :].astype(BF16)
        return c
    lax.fori_loop(0, wbf_ref.shape[0] // CAST_ROWS, cast, 0)


def _gemm1_kernel(te_ref, nt_ref, tv_ref, x_ref, w_hbm, b_ref, o_ref, stage_ref, wbf_ref,
                  slot_ref, sem, *, nc, nblk):
    first, slot = _stream_expert_weights(te_ref, nt_ref, w_hbm, stage_ref, slot_ref, sem)

    def compute(rows, convert):
        words = _load_token_rows(x_ref, rows, nblk)
        x_lo, x_hi = (v.astype(BF16) for v in _unpack_bf16_pair(words))
        half = x_lo.shape[1]
        lane128 = lax.broadcasted_iota(jnp.int32, (rows, V7X_LANES), 1)
        even_idx = (2 * lane128) % V7X_LANES
        for n0 in range(0, wbf_ref.shape[1], nc):
            if convert:
                wbf_ref[:, n0:n0 + nc] = stage_ref[slot, :, n0:n0 + nc].astype(BF16)
            hb = (jnp.dot(x_lo, wbf_ref[:half, n0:n0 + nc], preferred_element_type=F32)
                  + jnp.dot(x_hi, wbf_ref[half:, n0:n0 + nc], preferred_element_type=F32)
                  + b_ref[0, :, n0:n0 + nc])
            nxt = pltpu.roll(hb, nc - 1, axis=1)
            glu = jnp.minimum(hb, SWIGLU_LIMIT)
            lin = jnp.clip(nxt, -SWIGLU_LIMIT, SWIGLU_LIMIT)
            act = glu * jax.nn.sigmoid(SWIGLU_ALPHA * glu) * (lin + 1.0)
            for c in range(nc // (2 * V7X_LANES)):
                lo_blk = act[:, (2 * c) * V7X_LANES:(2 * c + 1) * V7X_LANES]
                hi_blk = act[:, (2 * c + 1) * V7X_LANES:(2 * c + 2) * V7X_LANES]
                out = jnp.where(lane128 < V7X_LANES // 2,
                                jnp.take_along_axis(lo_blk, even_idx, axis=1),
                                jnp.take_along_axis(hi_blk, even_idx, axis=1))
                o0 = n0 // 2 + c * V7X_LANES
                o_ref[:rows, o0:o0 + V7X_LANES] = out.astype(o_ref.dtype)
        if rows < o_ref.shape[0]:
            o_ref[rows:, :] = jnp.zeros((o_ref.shape[0] - rows, o_ref.shape[1]), o_ref.dtype)

    _for_real_rows(pl.program_id(1), nt_ref, tv_ref, o_ref, o_ref.shape[0], compute,
                   first, lambda: _cast_block(stage_ref, slot, wbf_ref))


def _for_real_rows(i, nt_ref, tv_ref, o_ref, tm, compute, first, convert_all):
    used = i < nt_ref[0]
    real = tv_ref[i]
    sizes = [tm >> s for s in range(MOE_TILE_LEVELS) if (tm >> s) % MOE_MIN_ROWS == 0]

    @pl.when(first & (real <= sizes[1]))
    def _():
        convert_all()

    for n, rows in enumerate(sizes):
        fits = real <= rows
        if n + 1 < len(sizes):
            fits = fits & (real > sizes[n + 1])
        if n == 0:
            @pl.when(used & fits & first)
            def _():
                compute(rows, True)
            fits = fits & jnp.logical_not(first)

        @pl.when(used & fits)
        def _(rows=rows):
            compute(rows, False)

    @pl.when(jnp.logical_not(used))
    def _():
        o_ref[...] = jnp.zeros_like(o_ref)


def _gemm1(xin, w1, b1, tile_e, n_tiles_used, tile_rows, tm, nblk):
    n_rows = xin.shape[0] // nblk
    n_e, d, f2 = w1.shape
    tn = min(EXPERT_COL_BLOCK, f2)
    nc = min(GEMM1_COL_CHUNK, tn)
    assert f2 % tn == 0 and d % CAST_ROWS == 0
    n_tiles = n_rows // tm
    assert nc % (2 * V7X_LANES) == 0

    def used(i, nt):
        return jnp.minimum(i, nt[0] - 1)

    return pl.pallas_call(
        functools.partial(_gemm1_kernel, nc=nc, nblk=nblk),
        grid_spec=pltpu.PrefetchScalarGridSpec(
            num_scalar_prefetch=3, grid=(f2 // tn, n_tiles),
            in_specs=[pl.BlockSpec((tm * nblk, V7X_LANES),
                                   lambda j, i, te, nt, tv: (used(i, nt), 0)),
                      pl.BlockSpec(memory_space=pl.ANY),
                      pl.BlockSpec((1, 1, tn), lambda j, i, te, nt, tv: (te[used(i, nt)], 0, j))],
            out_specs=pl.BlockSpec((tm, tn // 2), lambda j, i, te, nt, tv: (i, j)),
            scratch_shapes=[pltpu.VMEM((2, d, tn), F32), pltpu.VMEM((d, tn), BF16),
                            pltpu.SMEM((1,), jnp.int32), pltpu.SemaphoreType.DMA((2,))]),
        out_shape=jax.ShapeDtypeStruct((n_rows, f2 // 2), BF16),
        compiler_params=_cparams(("arbitrary", "arbitrary")),
        name="moe_gemm1",
    )(tile_e, n_tiles_used, tile_rows, xin, w1, b1.reshape(n_e, 1, f2))


def _gemm2_kernel(te_ref, nt_ref, tv_ref, a_ref, w_hbm, b_ref, o_ref, stage_ref, wbf_ref, slot_ref,
                  sem, *, nc, nblk):
    first, slot = _stream_expert_weights(te_ref, nt_ref, w_hbm, stage_ref, slot_ref, sem)
    tm = a_ref.shape[0]

    def compute(rows, convert):
        a = a_ref[:rows, :]
        half = wbf_ref.shape[1] // 2

        def cols(n0):
            if convert:
                wbf_ref[:, n0:n0 + nc] = stage_ref[slot, :, n0:n0 + nc].astype(BF16)
            return (jnp.dot(a, wbf_ref[:, n0:n0 + nc], preferred_element_type=F32)
                    + b_ref[0, :, n0:n0 + nc])

        for n0 in range(0, half, nc):
            words = _pack_bf16_pair(cols(n0), cols(half + n0))
            for c in range(nc // V7X_LANES):
                blk = n0 // V7X_LANES + c
                o_ref[pl.ds(blk, rows, stride=nblk), :] = words[:, c * V7X_LANES:(c + 1) * V7X_LANES]
        if rows < tm:
            o_ref[rows * nblk:, :] = jnp.zeros(((tm - rows) * nblk, o_ref.shape[1]), o_ref.dtype)

    _for_real_rows(pl.program_id(1), nt_ref, tv_ref, o_ref, tm, compute,
                   first, lambda: _cast_block(stage_ref, slot, wbf_ref))


def _gemm2(act, w2, b2, tile_e, n_tiles_used, tile_rows, tm):
    n_rows, f = act.shape
    n_e, _, d = w2.shape
    tn = d
    nc = min(512, tn // 2)
    nblk = d // 2 // V7X_LANES
    assert f % CAST_ROWS == 0
    n_tiles = n_rows // tm

    def used(i, nt):
        return jnp.minimum(i, nt[0] - 1)

    return pl.pallas_call(
        functools.partial(_gemm2_kernel, nc=nc, nblk=nblk),
        grid_spec=pltpu.PrefetchScalarGridSpec(
            num_scalar_prefetch=3, grid=(d // tn, n_tiles),
            in_specs=[pl.BlockSpec((tm, f), lambda j, i, te, nt, tv: (used(i, nt), 0)),
                      pl.BlockSpec(memory_space=pl.ANY),
                      pl.BlockSpec((1, 1, tn), lambda j, i, te, nt, tv: (te[used(i, nt)], 0, j))],
            out_specs=pl.BlockSpec((tm * nblk, V7X_LANES), lambda j, i, te, nt, tv: (i, 0)),
            scratch_shapes=[pltpu.VMEM((2, f, tn), F32), pltpu.VMEM((f, tn), BF16),
                            pltpu.SMEM((1,), jnp.int32), pltpu.SemaphoreType.DMA((2,))]),
        out_shape=jax.ShapeDtypeStruct((n_rows * nblk, V7X_LANES), jnp.uint32),
        compiler_params=_cparams(("arbitrary", "arbitrary")),
        name="moe_gemm2",
    )(tile_e, n_tiles_used, tile_rows, act, w2, b2.reshape(n_e, 1, d))


def _combine_kernel(dest_ref, x1_ref, gate_ref, gt_ref, g_ref, b_ref, y_ref, o_ref, buf, sem,
                    *, alpha, tc, n_tok, nblk):
    i = pl.program_id(0)
    n_steps = pl.num_programs(0)

    def rows(tok):
        return pl.ds(pl.multiple_of(tok * nblk, nblk), nblk)

    def copy(step, slot, k, r):
        src = y_ref.at[rows(dest_ref[k * n_tok + step * tc + r])]
        return pltpu.make_async_copy(src, buf.at[slot, k, rows(r)], sem.at[slot])

    def issue(step, slot):
        def body(r, c):
            for k in range(TOP_K):
                copy(step, slot, k, r).start(priority=k % 2)
            return c
        lax.fori_loop(0, tc, body, 0, unroll=2)

    def drain(slot):
        pltpu.make_async_copy(buf.at[slot], buf.at[slot], sem.at[slot]).wait()

    slot = i % 2

    @pl.when(i == 0)
    def _():
        issue(0, 0)

    @pl.when(i + 1 < n_steps)
    def _():
        issue(i + 1, 1 - slot)

    drain(slot)
    gates = gate_ref[...]
    lo, hi = _unpack_bf16_pair(_load_token_rows(buf.at[slot, 0], tc, nblk))
    ff_lo, ff_hi = gates[:, 0:1] * lo, gates[:, 0:1] * hi
    for k in range(1, TOP_K):
        lo, hi = _unpack_bf16_pair(_load_token_rows(buf.at[slot, k], tc, nblk))
        ff_lo, ff_hi = ff_lo + gates[:, k:k + 1] * lo, ff_hi + gates[:, k:k + 1] * hi
    ff = jnp.concatenate([ff_lo, ff_hi], axis=1)
    y = alpha * x1_ref[...] + (1.0 + gt_ref[0]) * ff
    o_ref[...] = _layer_norm(y, g_ref[...], b_ref[...])


def _combine(dest_flat, x1, gates_tk, gt, g, b, y, seq, alpha):
    t, d = x1.shape
    tc = _pick(seq, (256, 128))
    per_b = seq // tc
    nblk = d // 2 // V7X_LANES
    return pl.pallas_call(
        functools.partial(_combine_kernel, alpha=alpha, tc=tc, n_tok=t, nblk=nblk),
        grid_spec=pltpu.PrefetchScalarGridSpec(
            num_scalar_prefetch=1, grid=(t // tc,),
            in_specs=[pl.BlockSpec((tc, d), lambda i, ds: (i, 0)),
                      pl.BlockSpec((tc, TOP_K), lambda i, ds: (i, 0)),
                      pl.BlockSpec((1, 1, d), lambda i, ds: (i // per_b, 0, 0)),
                      pl.BlockSpec((1, d), lambda i, ds: (0, 0)),
                      pl.BlockSpec((1, d), lambda i, ds: (0, 0)),
                      pl.BlockSpec(memory_space=pl.ANY)],
            out_specs=pl.BlockSpec((tc, d), lambda i, ds: (i, 0)),
            scratch_shapes=[pltpu.VMEM((2, TOP_K, tc * nblk, V7X_LANES), jnp.uint32),
                            pltpu.SemaphoreType.DMA((2,))]),
        out_shape=jax.ShapeDtypeStruct((t, d), F32),
        compiler_params=_cparams(("arbitrary",)),
        name="moe_combine_ln2",
    )(dest_flat, x1, gates_tk, gt[:, None, :], g.reshape(1, d), b.reshape(1, d), y)


MOE_ROW_TILE = 512


def _moe_ffn(h2p, logits_t, x1, gt_f, ln_g, ln_b, w1, b1, w2, b2, seq, alpha):
    tm = MOE_ROW_TILE
    n_e, t = logits_t.shape
    n_rows = -(-(TOP_K * t) // tm) * tm + n_e * tm
    n_tiles = n_rows // tm
    gate_t, dest, tiles, pads = _routing(logits_t, tm, n_tiles)
    dest = dest.reshape(-1)
    tile_e = tiles[0, :n_tiles]
    n_used_rows = tiles[1, :1]
    tile_rows = tiles[2, :n_tiles]
    n_used_tiles = n_used_rows // tm
    nblk = x1.shape[1] // 2 // V7X_LANES
    xin = _dispatch(h2p, dest, pads[0, :, 0], pads[1, :, 0], n_used_rows, n_rows, tm, nblk)
    act = _gemm1(xin, w1, b1, tile_e, n_used_tiles, tile_rows, tm, nblk)
    y = _gemm2(act, w2, b2, tile_e, n_used_tiles, tile_rows, tm)
    return _combine(dest, x1, gate_t.T, gt_f, ln_g, ln_b, y, seq, alpha)


def kernel(x, c, positions, w_ada, b_ada, w_in, hgrn_lb, gnorm_w, w_o, ln1_g, ln1_b,
           router_w, router_b, w1, b1, w2, b2, ln2_g, ln2_b):
    bsz, seq, d = x.shape
    depth = w_ada.shape[0]
    t = bsz * seq
    att_w = d // 2
    hg_w = d - att_w
    alpha = (2.0 * depth) ** 0.25

    lb_all = jnp.cumsum(jax.nn.softmax(hgrn_lb.astype(F32), axis=0), axis=0)
    inv = ROPE_THETA ** (-(jnp.arange(0, ROT_DIM, 2, dtype=F32) / ROT_DIM))
    lane = np.arange(V7X_LANES)
    inv_lane = inv[(lane % ATT_HEAD_DIM) % (ROT_DIM // 2)].reshape(1, V7X_LANES)
    pos_b = jnp.broadcast_to(positions.astype(F32).reshape(t, 1), (t, V7X_LANES))

    x2d = x.reshape(t, d)
    for l in range(depth):
        mod = _adaln(c, w_ada[l], b_ada[l])
        sh_a, sc_a, gt_a, sh_f, sc_f, gt_f = jnp.split(mod, 6, axis=-1)

        proj = _in_proj(x2d, sc_a, sh_a, w_in[l], seq)
        proj3 = proj.reshape(bsz, seq, proj.shape[1])
        q_hp, k_hp, v_hp = _qk_rope(proj, pos_b, inv_lane, bsz, seq, att_w)
        att = _attention(q_hp, k_hp, v_hp)
        rec = _hgrn2(proj3, lb_all[l], gnorm_w[l], att_w, hg_w).reshape(t, hg_w)

        x1, h2p, logits_t = _out_proj(att, rec, x2d, w_o[l], gt_a, sc_f, sh_f, ln1_g[l], ln1_b[l],
                                      router_w[l], router_b[l], seq, alpha)

        x2d = _moe_ffn(h2p, logits_t, x1, gt_f, ln2_g[l], ln2_b[l], w1[l], b1[l], w2[l], b2[l],
                       seq, alpha)
    return x2d.reshape(bsz, seq, d)
```

```python
import functools

import numpy as np
import jax
import jax.numpy as jnp
from jax import lax
from jax.experimental import pallas as pl
from jax.experimental.pallas import tpu as pltpu

F32 = jnp.float32
BF16 = jnp.bfloat16

V7X_LANES = 128
V7X_VMEM_LIMIT = 56 * 1024 * 1024

ATT_HEAD_DIM = 64
DILATED_PAIRS = ((128, 1), (512, 4), (2048, 16))
ATT_BLOCK = 128
ROT_DIM = ATT_HEAD_DIM // 4
ROPE_THETA = 500000.0
HG_EXPAND = 128
HG_CHUNK = 64
HG_SUB = 16
TOP_K = 4
SWIGLU_ALPHA = 1.702
SWIGLU_LIMIT = 7.0
LN_EPS = 1e-5
RMS_EPS = 1e-6
NEG_INF = -1e30

NT_DIMS = (((1,), (1,)), ((), ()))


def _pick(n, candidates):
    for c in candidates:
        if n % c == 0:
            return c
    raise ValueError(f"no tile in {candidates} divides {n}")


def _cparams(sem, vmem=V7X_VMEM_LIMIT, flags=None):
    return pltpu.CompilerParams(dimension_semantics=sem, vmem_limit_bytes=vmem, flags=flags)


def _adaln_kernel(c_ref, w_ref, b_ref, o_ref):
    c = c_ref[...]
    s = c * jax.nn.sigmoid(c)
    o_ref[...] = jnp.dot(s.astype(BF16), w_ref[...].astype(BF16),
                         preferred_element_type=F32) + b_ref[...]


def _adaln(c, w, b):
    bsz, d = c.shape
    n = w.shape[1]
    rows = 8
    cp = jnp.zeros((rows, d), F32).at[:bsz].set(c)
    tn = _pick(n, (1024, 512, 256, 128))
    out = pl.pallas_call(
        _adaln_kernel,
        grid=(n // tn,),
        in_specs=[pl.BlockSpec((rows, d), lambda j: (0, 0)),
                  pl.BlockSpec((d, tn), lambda j: (0, j)),
                  pl.BlockSpec((1, tn), lambda j: (0, j))],
        out_specs=pl.BlockSpec((rows, tn), lambda j: (0, j)),
        out_shape=jax.ShapeDtypeStruct((rows, n), F32),
        compiler_params=_cparams(("arbitrary",)),
        name="adaln",
    )(cp, w, b.reshape(1, n))
    return out[:bsz]


def _in_proj_kernel(x_ref, sc_ref, sh_ref, w_hbm, o_ref, wbf_ref, h_ref, stage_ref, sem):
    m, n = pl.program_id(0), pl.program_id(1)
    n_col = pl.num_programs(1)
    tn = o_ref.shape[1]

    def fetch(nn):
        src = w_hbm.at[:, pl.ds(pl.multiple_of(nn * tn, tn), tn)]
        return pltpu.make_async_copy(src, stage_ref, sem)

    @pl.when((m == 0) & (n == 0))
    def _():
        fetch(0).start()

    @pl.when(m == 0)
    def _():
        fetch(n).wait()
        wbf_ref[n] = stage_ref[...].astype(BF16)

        @pl.when(n + 1 < n_col)
        def _():
            fetch(n + 1).start()

    @pl.when(n == 0)
    def _():
        h_ref[...] = (x_ref[...] * (1.0 + sc_ref[0]) + sh_ref[0]).astype(BF16)

    o_ref[...] = jnp.dot(h_ref[...], wbf_ref[n], preferred_element_type=F32).astype(o_ref.dtype)


def _in_proj(x2d, sc, sh, w, seq):
    t, d = x2d.shape
    nc = w.shape[1]
    tm = _pick(seq, (512, 256, 128))
    tn = _pick(nc, (1024, 512, 256, 128))
    per_b = seq // tm
    vec = pl.BlockSpec((1, 1, d), lambda i, j: (i // per_b, 0, 0))
    return pl.pallas_call(
        _in_proj_kernel,
        grid=(t // tm, nc // tn),
        in_specs=[pl.BlockSpec((tm, d), lambda i, j: (i, 0)), vec, vec,
                  pl.BlockSpec(memory_space=pl.ANY)],
        out_specs=pl.BlockSpec((tm, tn), lambda i, j: (i, j)),
        out_shape=jax.ShapeDtypeStruct((t, nc), BF16),
        scratch_shapes=[pltpu.VMEM((nc // tn, d, tn), BF16), pltpu.VMEM((tm, d), BF16),
                        pltpu.VMEM((d, tn), F32), pltpu.SemaphoreType.DMA(())],
        compiler_params=_cparams(("arbitrary", "arbitrary")),
        name="in_proj",
    )(x2d, sc[:, None, :], sh[:, None, :], w)


def _rope_kernel(q_ref, k_ref, v_ref, pos_ref, inv_ref, qo_ref, ko_ref, vo_ref):
    tm = pos_ref.shape[0]
    lane = lax.broadcasted_iota(jnp.int32, (tm, V7X_LANES), 1)
    lh = lane % ATT_HEAD_DIM
    half = ROT_DIM // 2
    ang = pos_ref[...] * inv_ref[...]
    cs = jnp.where(lh < ROT_DIM, jnp.cos(ang), 1.0)
    sn = jnp.sin(ang)
    sn = jnp.where(lh < half, -sn, jnp.where(lh < ROT_DIM, sn, 0.0))

    def rope(t):
        swapped = jnp.where(lh < half,
                            pltpu.roll(t, V7X_LANES - half, axis=1),
                            pltpu.roll(t, half, axis=1))
        return t * cs + swapped * sn

    for h in range(qo_ref.shape[1]):
        lanes = slice(h * V7X_LANES, (h + 1) * V7X_LANES)
        qo_ref[0, h] = rope(q_ref[:, lanes].astype(F32)) * (ATT_HEAD_DIM ** -0.5)
        ko_ref[0, h] = rope(k_ref[:, lanes].astype(F32))
        vo_ref[0, h] = v_ref[:, lanes].astype(F32)


def _qk_rope(proj, pos_b, inv_lane, bsz, seq, att_w):
    t = proj.shape[0]
    hp = att_w // V7X_LANES
    tm = _pick(seq, (256, 128))
    per_b = seq // tm
    out_spec = pl.BlockSpec((1, hp, tm, V7X_LANES), lambda i: (i // per_b, 0, i % per_b, 0))
    shp = jax.ShapeDtypeStruct((bsz, hp, seq, V7X_LANES), F32)
    return pl.pallas_call(
        _rope_kernel,
        grid=(t // tm,),
        in_specs=[pl.BlockSpec((tm, att_w), lambda i: (i, 0)),
                  pl.BlockSpec((tm, att_w), lambda i: (i, 1)),
                  pl.BlockSpec((tm, att_w), lambda i: (i, 2)),
                  pl.BlockSpec((tm, V7X_LANES), lambda i: (i, 0)),
                  pl.BlockSpec((1, V7X_LANES), lambda i: (0, 0))],
        out_specs=[out_spec, out_spec, out_spec],
        out_shape=[shp, shp, shp],
        compiler_params=_cparams(("arbitrary",)),
        name="qk_rope",
    )(proj, proj, proj, pos_b, inv_lane)


ATT_TASKS_PER_STEP = 8


def _attn_kernel(q_in, k_in, v_in, out_ref, qs, ks, vs, sm, no_lo, nl_lo, no_s, nl_s, nat, *, seq):
    qn = ATT_BLOCK
    n_task = seq // qn
    n_head = V7X_LANES // ATT_HEAD_DIM
    gc = min(ATT_TASKS_PER_STEP, n_task)
    (_, d_lo), (_, dm), (_, d_hi) = DILATED_PAIRS
    g_hi = d_hi // dm
    rows_m = seq // dm
    lane = lax.broadcasted_iota(jnp.int32, (qn, V7X_LANES), 1)
    qi = lax.broadcasted_iota(jnp.int32, (gc, qn, 2 * qn), 1)
    kj = lax.broadcasted_iota(jnp.int32, (gc, qn, 2 * qn), 2)
    dist = qi + qn - kj
    band = (dist >= 0) & (dist <= qn)
    zeros = jnp.zeros((qn, V7X_LANES), BF16)
    srcs = (q_in, k_in, v_in)

    for a, src in enumerate(srcs):
        for r in range(dm):
            sm[a, r] = src[0, 0, pl.ds(r, rows_m, stride=dm), :]

    def task_index(d, r, n):
        if d == d_lo:
            return None, slice(n * qn, (n + 1) * qn)
        if d == dm:
            return r, slice(n * qn, (n + 1) * qn)
        return r % dm, pl.ds(r // dm + n * qn * g_hi, qn, stride=g_hi)

    for bi, (window, d) in enumerate(DILATED_PAIRS):
        nb = seq // (d * qn)
        tasks = [(r, n) for r in range(d) for n in range(nb)]
        for g, (r, n) in enumerate(tasks):
            sel, rows = task_index(d, r, n)

            def operand(a, sel=sel, rows=rows):
                return srcs[a][0, 0, rows, :] if sel is None else sm[a, sel, rows, :]

            qv = operand(0)
            for h in range(n_head):
                qs[h, g] = jnp.where(lane // ATT_HEAD_DIM == h, qv, 0.0).astype(BF16)
            for a, dst in ((1, ks), (2, vs)):
                blk = operand(a).astype(BF16)
                dst[g, qn:, :] = blk
                if n + 1 < nb:
                    dst[g + 1, :qn, :] = blk
                if n == 0:
                    dst[g, :qn, :] = zeros

        for g0 in range(0, n_task, gc):
            gsl = slice(g0, g0 + gc)
            gidx = g0 + lax.broadcasted_iota(jnp.int32, (gc, qn, 2 * qn), 0)
            valid = band & ((gidx % nb != 0) | (kj >= qn))
            k = ks[gsl]
            v = vs[gsl]
            o = None
            for h in range(n_head):
                s = jnp.einsum("gqd,gkd->gqk", qs[h, gsl], k, preferred_element_type=F32)
                s = jnp.where(valid, s, NEG_INF)
                m = jnp.max(s, axis=-1, keepdims=True)
                p = jnp.exp(s - m)
                den = jnp.sum(p, axis=-1, keepdims=True)
                oh = jnp.einsum("gqk,gkd->gqd", p.astype(BF16), v,
                                preferred_element_type=F32) / den
                lh = jnp.broadcast_to(m + jnp.log(den), oh.shape)
                if o is None:
                    o, lse = oh, lh
                else:
                    in_head = lane[None] // ATT_HEAD_DIM == h
                    o, lse = jnp.where(in_head, oh, o), jnp.where(in_head, lh, lse)
            for t in range(gc):
                sel, rows = task_index(d, *tasks[g0 + t])
                if sel is None:
                    no_lo[rows, :] = o[t]
                    nl_lo[rows, :] = lse[t]
                else:
                    no_s[bi - 1, sel, rows, :] = o[t]
                    nl_s[bi - 1, sel, rows, :] = lse[t]

    for r in range(dm):
        for c in range(rows_m // qn):
            rows = slice(c * qn, (c + 1) * qn)
            seq_rows = pl.ds(r + dm * c * qn, qn, stride=dm)
            ls = [nl_lo[seq_rows, :], nl_s[0, r, rows, :], nl_s[1, r, rows, :]]
            os_ = [no_lo[seq_rows, :], no_s[0, r, rows, :], no_s[1, r, rows, :]]
            mx = functools.reduce(jnp.maximum, ls)
            ws = [jnp.exp(l - mx) for l in ls]
            tot = functools.reduce(lambda x, y: x + y, ws)
            acc = functools.reduce(lambda x, y: x + y, [w * o for w, o in zip(ws, os_)])
            nat[seq_rows, :] = acc / tot
    out_ref[0, 0] = nat[...].astype(out_ref.dtype)


def _attention(q_hp, k_hp, v_hp):
    bsz, hp, seq, _ = q_hp.shape
    (_, d_lo), (_, dm), (_, d_hi) = DILATED_PAIRS
    assert d_lo == 1 and d_hi % dm == 0
    for window, d in DILATED_PAIRS:
        assert window // d == ATT_BLOCK and seq % (d * ATT_BLOCK) == 0
    n_task = seq // ATT_BLOCK
    assert n_task % min(ATT_TASKS_PER_STEP, n_task) == 0
    n_head = V7X_LANES // ATT_HEAD_DIM
    streams = (dm, seq // dm, V7X_LANES)
    spec = pl.BlockSpec((1, 1, seq, V7X_LANES), lambda b, h: (b, h, 0, 0))
    return pl.pallas_call(
        functools.partial(_attn_kernel, seq=seq),
        grid=(bsz, hp),
        in_specs=[spec, spec, spec],
        out_specs=spec,
        out_shape=jax.ShapeDtypeStruct((bsz, hp, seq, V7X_LANES), BF16),
        scratch_shapes=[pltpu.VMEM((n_head, n_task, ATT_BLOCK, V7X_LANES), BF16),
                        pltpu.VMEM((n_task, 2 * ATT_BLOCK, V7X_LANES), BF16),
                        pltpu.VMEM((n_task, 2 * ATT_BLOCK, V7X_LANES), BF16),
                        pltpu.VMEM((3,) + streams, F32),
                        pltpu.VMEM((seq, V7X_LANES), F32), pltpu.VMEM((seq, V7X_LANES), F32),
                        pltpu.VMEM((2,) + streams, F32), pltpu.VMEM((2,) + streams, F32),
                        pltpu.VMEM((seq, V7X_LANES), F32)],
        compiler_params=_cparams(("arbitrary", "arbitrary")),
        name="dilated_attn",
    )(q_hp, k_hp, v_hp)


def _hgrn_kernel(q_ref, f_ref, i_ref, g_ref, lb_ref, gw_ref, sums_ref, o_ref,
                 st_ref, hl_ref, kk_ref, cum_ref, *, heads, ts):
    @pl.when(pl.program_id(2) == 0)
    def _():
        st_ref[...] = jnp.zeros_like(st_ref)

    c_len = HG_CHUNK
    lanes = heads * HG_EXPAND
    lb = lb_ref[...]
    for c in range(ts // c_len):
        rows = slice(c * c_len, (c + 1) * c_len)
        f = lb + (1.0 - lb) * jax.nn.sigmoid(f_ref[0, rows, :].astype(F32))
        kk_ref[rows, :] = 1.0 - f
        hi, lo = _bf16_split(jnp.log(f))
        hl_ref[rows, :lanes] = hi
        hl_ref[rows, lanes:] = lo
    both = jnp.dot(sums_ref[...], hl_ref[...], preferred_element_type=F32)
    cum_ref[...] = both[:, :lanes] + both[:, lanes:]

    n_sub = c_len // HG_SUB

    def sub_block_anchor(b):
        mids = [b[i * HG_SUB + HG_SUB // 2:i * HG_SUB + HG_SUB // 2 + 1, :] for i in range(n_sub)]
        return jnp.concatenate([jnp.broadcast_to(m, (HG_SUB, m.shape[1])) for m in mids], axis=0)

    gi = lax.broadcasted_iota(jnp.int32, (heads * n_sub, HG_SUB, c_len), 0) % n_sub
    qi = lax.broadcasted_iota(jnp.int32, (heads * n_sub, HG_SUB, c_len), 1)
    si = lax.broadcasted_iota(jnp.int32, (heads * n_sub, HG_SUB, c_len), 2)
    causal = si <= gi * HG_SUB + qi
    key_row = lax.broadcasted_iota(jnp.int32, (c_len, HG_EXPAND), 0)

    for c in range(ts // c_len):
        rows = slice(c * c_len, (c + 1) * c_len)
        qts, kts, vbs = [], [], []
        for h in range(heads):
            lsl = slice(h * HG_EXPAND, (h + 1) * HG_EXPAND)
            b = cum_ref[rows, lsl]
            anchor = sub_block_anchor(b)
            kk = kk_ref[rows, lsl]
            qt = (q_ref[0, rows, lsl].astype(F32) * jnp.exp(b - anchor)).astype(BF16)
            qts.append(qt.reshape(n_sub, HG_SUB, HG_EXPAND))
            vb = i_ref[0, rows, lsl].astype(BF16)
            for i in range(n_sub):
                hi_r = (i + 1) * HG_SUB
                kt = kk * jnp.exp(anchor[i * HG_SUB:i * HG_SUB + 1, :] - b)
                kts.append(jnp.where(key_row < hi_r, kt, 0.0).astype(BF16))
                vbs.append(vb)
        a = jnp.einsum("gqk,gsk->gqs", jnp.concatenate(qts, axis=0), jnp.stack(kts),
                       preferred_element_type=F32)
        a = jnp.where(causal, a, 0.0).astype(BF16)
        o_intra = jnp.einsum("gqs,gsv->gqv", a, jnp.stack(vbs), preferred_element_type=F32)

        for h in range(heads):
            lsl = slice(h * HG_EXPAND, (h + 1) * HG_EXPAND)
            b = cum_ref[rows, lsl]
            b_last = jnp.broadcast_to(b[c_len - 1:c_len, :], b.shape)
            kk = kk_ref[rows, lsl]
            q = q_ref[0, rows, lsl].astype(F32)
            v = i_ref[0, rows, lsl].astype(F32)
            st = st_ref[h]
            o_inter = lax.dot_general((q * jnp.exp(b)).astype(BF16), st.astype(BF16), NT_DIMS,
                                      preferred_element_type=F32)
            o = o_inter + o_intra[h * n_sub:(h + 1) * n_sub].reshape(c_len, HG_EXPAND)
            kl = kk * jnp.exp(b_last - b)
            upd = jnp.dot(v.T.astype(BF16), kl.astype(BF16), preferred_element_type=F32)
            st_ref[h] = st * jnp.exp(b_last[0:1, :]) + upd
            o = o * lax.rsqrt(jnp.mean(o * o, axis=-1, keepdims=True) + RMS_EPS)
            g = g_ref[0, rows, lsl].astype(F32)
            o = o * gw_ref[:, lsl] * (g * jax.nn.sigmoid(g))
            o_ref[0, rows, lsl] = o.astype(o_ref.dtype)


def _hgrn_cumsum_matrix(ts):
    t = np.arange(ts)[:, None]
    s = np.arange(ts)[None, :]
    return (((t // HG_CHUNK) == (s // HG_CHUNK)) & (s <= t)).astype(np.float32)


def _hgrn2(proj3, lb, gw, att_w, hg_w):
    bsz, seq, _ = proj3.shape
    n_heads = hg_w // HG_EXPAND
    heads = _pick(n_heads, (4, 2, 1))
    lanes = heads * HG_EXPAND
    ts = _pick(seq, (256, 128, 64))
    base = 3 * att_w
    assert base % lanes == 0 and hg_w % lanes == 0

    def seg(k):
        off = (base + k * hg_w) // lanes
        return pl.BlockSpec((1, ts, lanes), lambda b, h, s: (b, s, off + h))

    vec = pl.BlockSpec((1, lanes), lambda b, h, s: (0, h))
    sums = jnp.asarray(_hgrn_cumsum_matrix(ts), BF16)
    return pl.pallas_call(
        functools.partial(_hgrn_kernel, heads=heads, ts=ts),
        grid=(bsz, hg_w // lanes, seq // ts),
        in_specs=[seg(0), seg(1), seg(2), seg(3), vec, vec,
                  pl.BlockSpec((ts, ts), lambda b, h, s: (0, 0))],
        out_specs=pl.BlockSpec((1, ts, lanes), lambda b, h, s: (b, s, h)),
        out_shape=jax.ShapeDtypeStruct((bsz, seq, hg_w), BF16),
        scratch_shapes=[pltpu.VMEM((heads, HG_EXPAND, HG_EXPAND), F32),
                        pltpu.VMEM((ts, 2 * lanes), BF16),
                        pltpu.VMEM((ts, lanes), F32),
                        pltpu.VMEM((ts, lanes), F32)],
        compiler_params=_cparams(("arbitrary", "arbitrary", "arbitrary")),
        name="hgrn2",
    )(proj3, proj3, proj3, proj3, lb.reshape(1, hg_w), gw.reshape(1, hg_w), sums)


def _layer_norm(y, g, b):
    mu = jnp.mean(y, axis=-1, keepdims=True)
    yc = y - mu
    var = jnp.mean(yc * yc, axis=-1, keepdims=True)
    return yc * lax.rsqrt(var + LN_EPS) * g + b


def _bf16_split(x):
    hi = x.astype(BF16)
    lo = (x - hi.astype(F32)).astype(BF16)
    return hi, lo


def _pack_bf16_pair(lo, hi):
    lo_bits = lax.bitcast_convert_type(lo.astype(BF16).astype(F32), jnp.uint32)
    hi_bits = lax.bitcast_convert_type(hi.astype(BF16).astype(F32), jnp.uint32)
    return (lo_bits >> 16) | (hi_bits & jnp.uint32(0xFFFF0000))


def _store_token_rows(ref, tok0, words):
    n, width = words.shape
    nblk = width // V7X_LANES
    for c in range(nblk):
        ref[pl.ds(tok0 * nblk + c, n, stride=nblk), :] = words[:, c * V7X_LANES:(c + 1) * V7X_LANES]


def _load_token_rows(ref, n, nblk, tok0=0):
    return jnp.concatenate([ref[pl.ds(tok0 * nblk + c, n, stride=nblk), :] for c in range(nblk)],
                           axis=1)


def _unpack_bf16_pair(words):
    lo = lax.bitcast_convert_type(words << 16, F32)
    hi = lax.bitcast_convert_type(words & jnp.uint32(0xFFFF0000), F32)
    return lo, hi


def _load_weight_bf16(w_hbm, wbf_ref, stage_ref, sem):
    rows = stage_ref.shape[1]
    n_slab = w_hbm.shape[0] // rows

    def fetch(s, slot):
        return pltpu.make_async_copy(w_hbm.at[pl.ds(s * rows, rows)], stage_ref.at[slot],
                                     sem.at[slot])

    fetch(0, 0).start()
    for s in range(n_slab):
        if s + 1 < n_slab:
            fetch(s + 1, (s + 1) % 2).start()
        fetch(s, s % 2).wait()
        wbf_ref[s * rows:(s + 1) * rows, :] = stage_ref[s % 2].astype(BF16)


def _out_proj_kernel(att_ref, rec_ref, x_ref, wo_hbm, gt_ref, sc_ref, sh_ref, g_ref, b_ref,
                     rw_ref, rb_ref, x1_ref, h2_ref, lg_ref, wobf_ref, rwhi_ref, rwlo_ref,
                     stage_ref, hi_ref, lo_ref, mix_a, mix_b, sem, *, alpha, att_w):
    i = pl.program_id(0)

    @pl.when(i == 0)
    def _():
        _load_weight_bf16(wo_hbm, wobf_ref, stage_ref, sem)
        hi, lo = _bf16_split(rw_ref[...])
        rwhi_ref[...] = hi
        rwlo_ref[...] = lo
        mix_b[...] = jnp.zeros_like(mix_b)

    def step(mix_out, mix_in):
        att = jnp.concatenate([att_ref[0, h] for h in range(att_ref.shape[1])], axis=1)
        mix_out[...] = (jnp.dot(att, wobf_ref[:att_w, :], preferred_element_type=F32)
                        + jnp.dot(rec_ref[...], wobf_ref[att_w:, :], preferred_element_type=F32))
        half = x_ref.shape[1] // 2
        for r in range(x_ref.shape[0] // LN_ROW_GROUP):
            rows = slice(r * LN_ROW_GROUP, (r + 1) * LN_ROW_GROUP)
            y = alpha * x_ref[rows, :] + (1.0 + gt_ref[0]) * mix_in[rows, :]
            x1 = _layer_norm(y, g_ref[...], b_ref[...])
            x1_ref[rows, :] = x1
            h2 = x1 * (1.0 + sc_ref[0]) + sh_ref[0]
            hi, lo = _bf16_split(h2)
            hi_ref[rows, :] = hi
            lo_ref[rows, :] = lo
            _store_token_rows(h2_ref, r * LN_ROW_GROUP,
                              _pack_bf16_pair(h2[:, :half], h2[:, half:]))
        lg = (jnp.dot(hi_ref[...], rwhi_ref[...], preferred_element_type=F32)
              + jnp.dot(lo_ref[...], rwhi_ref[...], preferred_element_type=F32)
              + jnp.dot(hi_ref[...], rwlo_ref[...], preferred_element_type=F32))
        lg_ref[...] = lg.T[:lg_ref.shape[0], :] + rb_ref[...]

    @pl.when(i % 2 == 0)
    def _():
        step(mix_a, mix_b)

    @pl.when(i % 2 == 1)
    def _():
        step(mix_b, mix_a)


OUT_PROJ_STAGE_ROWS = 256
LN_ROW_GROUP = 16


def _out_proj(att, rec, x2d, wo, gt, sc, sh, g, b, rw, rb, seq, alpha):
    t, d = x2d.shape
    hp = att.shape[1]
    att_w = hp * V7X_LANES
    n_e = rw.shape[1]
    assert n_e <= V7X_LANES and d % OUT_PROJ_STAGE_ROWS == 0
    rw_pad = jnp.zeros((d, V7X_LANES), F32).at[:, :n_e].set(rw)
    nblk = d // 2 // V7X_LANES
    tm = _pick(seq, (512, 256, 128))
    per_b = seq // tm
    n_tile = t // tm
    mm = lambda i: jnp.minimum(i, n_tile - 1)
    ep = lambda i: jnp.maximum(i - 1, 0)
    vec3 = pl.BlockSpec((1, 1, d), lambda i: (ep(i) // per_b, 0, 0))
    full = lambda shape: pl.BlockSpec(shape, lambda i: (0,) * len(shape))
    return pl.pallas_call(
        functools.partial(_out_proj_kernel, alpha=alpha, att_w=att_w),
        grid=(n_tile + 1,),
        in_specs=[pl.BlockSpec((1, hp, tm, V7X_LANES),
                               lambda i: (mm(i) // per_b, 0, mm(i) % per_b, 0)),
                  pl.BlockSpec((tm, d - att_w), lambda i: (mm(i), 0)),
                  pl.BlockSpec((tm, d), lambda i: (ep(i), 0)),
                  pl.BlockSpec(memory_space=pl.ANY), vec3, vec3, vec3, full((1, d)), full((1, d)),
                  full((d, V7X_LANES)), full((n_e, 1))],
        out_specs=[pl.BlockSpec((tm, d), lambda i: (ep(i), 0)),
                   pl.BlockSpec((tm * nblk, V7X_LANES), lambda i: (ep(i), 0)),
                   pl.BlockSpec((n_e, tm), lambda i: (0, ep(i)))],
        out_shape=[jax.ShapeDtypeStruct((t, d), F32),
                   jax.ShapeDtypeStruct((t * nblk, V7X_LANES), jnp.uint32),
                   jax.ShapeDtypeStruct((n_e, t), F32)],
        scratch_shapes=[pltpu.VMEM((d, d), BF16), pltpu.VMEM((d, V7X_LANES), BF16),
                        pltpu.VMEM((d, V7X_LANES), BF16),
                        pltpu.VMEM((2, OUT_PROJ_STAGE_ROWS, d), F32),
                        pltpu.VMEM((tm, d), BF16), pltpu.VMEM((tm, d), BF16),
                        pltpu.VMEM((tm, d), F32), pltpu.VMEM((tm, d), F32),
                        pltpu.SemaphoreType.DMA((2,))],
        compiler_params=_cparams(("arbitrary",)),
        name="out_proj_ln1",
    )(att, rec, x2d, wo, gt[:, None, :], sc[:, None, :], sh[:, None, :],
      g.reshape(1, d), b.reshape(1, d), rw_pad, rb.reshape(n_e, 1))


def _routing_kernel(lg_ref, tri_ref, low_ref, gate_ref, dest_ref, tile_ref, pad_ref, sel_scr, rk_scr,
                    *, blk, tm):
    n_e, t = lg_ref.shape
    eidx = lax.broadcasted_iota(jnp.int32, (n_e, t), 0)
    cur = lg_ref[...]
    vals, idxs = [], []
    for _ in range(TOP_K):
        m = jnp.max(cur, axis=0, keepdims=True)
        ik = jnp.min(jnp.where(cur == m, eidx, n_e), axis=0, keepdims=True)
        cur = jnp.where(eidx == ik, -jnp.inf, cur)
        vals.append(m)
        idxs.append(ik)
    es = [jnp.exp(v - vals[0]) for v in vals]
    den = functools.reduce(lambda a, b: a + b, es)
    for k in range(TOP_K):
        gate_ref[k:k + 1, :] = es[k] / den
    sel = functools.reduce(lambda a, b: a | b, [eidx == ik for ik in idxs])
    sel_scr[...] = jnp.where(sel, 1.0, 0.0)

    tri = tri_ref[...]
    carry = jnp.zeros((n_e, 1), F32)
    for j in range(t // blk):
        sb = sel_scr[:, j * blk:(j + 1) * blk]
        pre = jnp.dot(sb.astype(BF16), tri, preferred_element_type=F32)
        rk_scr[:, j * blk:(j + 1) * blk] = pre + carry
        carry = carry + jnp.sum(sb, axis=1, keepdims=True)

    counts = jnp.broadcast_to(carry, (n_e, V7X_LANES))
    padded = jnp.floor((counts + (tm - 1)) * (1.0 / tm)) * tm
    pends = jnp.dot(low_ref[...], padded, precision=lax.Precision.HIGHEST,
                    preferred_element_type=F32)
    pstarts = pends - padded
    row0 = rk_scr[...] + pstarts[:, 0:1]
    for k in range(TOP_K):
        dest_ref[k:k + 1, :] = jnp.sum(jnp.where(eidx == idxs[k], row0, 0.0), axis=0,
                                       keepdims=True).astype(jnp.int32)
    starts = (lax.broadcasted_iota(jnp.int32, (n_e, tile_ref.shape[1]), 1) * tm).astype(F32)
    owner = jnp.sum(jnp.where(pends[:, 0:1] <= starts, 1.0, 0.0), axis=0, keepdims=True)
    tile_ref[0:1, :] = jnp.minimum(owner, n_e - 1.0).astype(jnp.int32)
    tile_ref[1:2, :] = jnp.broadcast_to(jnp.max(pends[:, 0:1], axis=0, keepdims=True),
                                        (1, tile_ref.shape[1])).astype(jnp.int32)
    e_tile = lax.broadcasted_iota(jnp.int32, starts.shape, 0).astype(F32)
    row_end = jnp.sum(jnp.where(e_tile == owner, (pstarts + counts)[:, 0:1], 0.0), axis=0,
                      keepdims=True)
    tile_ref[2:3, :] = jnp.clip(row_end - starts[0:1, :], 0.0, float(tm)).astype(jnp.int32)
    pad_ref[0] = (pstarts + counts).astype(jnp.int32)
    pad_ref[1] = (padded - counts).astype(jnp.int32)


def _routing(logits_t, tm, n_tiles):
    n_e, t = logits_t.shape
    assert tm & (tm - 1) == 0
    blk = _pick(t, (256, 128))
    tri = jnp.asarray(np.triu(np.ones((blk, blk), np.float32), 1), BF16)
    low = jnp.asarray(np.tril(np.ones((n_e, n_e), np.float32)))
    ntp = -(-n_tiles // V7X_LANES) * V7X_LANES
    full = lambda shape: pl.BlockSpec(shape, lambda: (0,) * len(shape))
    return pl.pallas_call(
        functools.partial(_routing_kernel, blk=blk, tm=tm),
        in_specs=[full((n_e, t)), full((blk, blk)), full((n_e, n_e))],
        out_specs=[full((TOP_K, t)), full((TOP_K, t)), full((3, ntp)), full((2, n_e, V7X_LANES))],
        out_shape=[jax.ShapeDtypeStruct((TOP_K, t), F32),
                   jax.ShapeDtypeStruct((TOP_K, t), jnp.int32),
                   jax.ShapeDtypeStruct((3, ntp), jnp.int32),
                   jax.ShapeDtypeStruct((2, n_e, V7X_LANES), jnp.int32)],
        scratch_shapes=[pltpu.VMEM((n_e, t), F32), pltpu.VMEM((n_e, t), F32)],
        compiler_params=pltpu.CompilerParams(vmem_limit_bytes=V7X_VMEM_LIMIT),
        name="routing",
    )(logits_t, tri, low)


def _dispatch_kernel(dest_ref, pad_off_ref, pad_n_ref, nrow_ref, h_ref, o_ref, zero_ref, sem, zsem,
                     *, n_tok, tq, tm, n_e, nblk):
    step = pl.program_id(0)
    base = step * tq

    def rows(tok, n=1):
        return pl.ds(pl.multiple_of(tok * nblk, nblk), n * nblk)

    def fill(off, n):
        return pltpu.make_async_copy(zero_ref.at[rows(0, n)], o_ref.at[rows(off, n)], zsem)

    pieces = [1 << s for s in range(tm.bit_length() - 2, -1, -1)]

    def pad_rows(wait):
        def per_expert(e, c):
            off = pad_off_ref[e]
            n = pad_n_ref[e]
            for p in pieces:
                hit = (n & p) != 0

                @pl.when(hit)
                def _(off=off, p=p):
                    cp = fill(off, p)
                    cp.wait() if wait else cp.start()

                off = off + jnp.where(hit, p, 0)
            return c
        lax.fori_loop(0, n_e, per_expert, 0)

        def per_tile(i, c):
            cp = fill(i * tm, tm)
            cp.wait() if wait else cp.start()
            return c
        lax.fori_loop(nrow_ref[0] // tm, o_ref.shape[0] // (tm * nblk), per_tile, 0)

    @pl.when(step == 0)
    def _():
        zero_ref[...] = jnp.zeros_like(zero_ref)
        pad_rows(False)

    def start(r, c):
        for k in range(TOP_K):
            dst = o_ref.at[rows(dest_ref[k * n_tok + base + r])]
            pltpu.make_async_copy(h_ref.at[rows(r)], dst, sem).start(priority=k % 2)
        return c

    lax.fori_loop(0, tq, start, 0, unroll=2)
    for _ in range(TOP_K):
        pltpu.make_async_copy(h_ref, h_ref, sem).wait()

    @pl.when(step == 0)
    def _():
        pad_rows(True)


def _dispatch(h2p, dest_flat, pad_off, pad_n, n_used_rows, n_rows, tm, nblk):
    t = h2p.shape[0] // nblk
    n_e = pad_off.shape[0]
    tq = _pick(t, (256, 128))
    return pl.pallas_call(
        functools.partial(_dispatch_kernel, n_tok=t, tq=tq, tm=tm, n_e=n_e, nblk=nblk),
        grid_spec=pltpu.PrefetchScalarGridSpec(
            num_scalar_prefetch=4, grid=(t // tq,),
            in_specs=[pl.BlockSpec((tq * nblk, V7X_LANES), lambda i, *_: (i, 0))],
            out_specs=pl.BlockSpec(memory_space=pl.ANY),
            scratch_shapes=[pltpu.VMEM((tm * nblk, V7X_LANES), h2p.dtype),
                            pltpu.SemaphoreType.DMA(()), pltpu.SemaphoreType.DMA(())]),
        out_shape=jax.ShapeDtypeStruct((n_rows * nblk, V7X_LANES), h2p.dtype),
        compiler_params=pltpu.CompilerParams(dimension_semantics=("arbitrary",),
                                             has_side_effects=True,
                                             vmem_limit_bytes=V7X_VMEM_LIMIT),
        name="moe_dispatch",
    )(dest_flat, pad_off, pad_n, n_used_rows, h2p)


EXPERT_COL_BLOCK = 2048
GEMM1_COL_CHUNK = 256
MOE_TILE_LEVELS = 3
MOE_MIN_ROWS = 16
CAST_ROWS = 256


def _stream_expert_weights(te_ref, nt_ref, w_hbm, stage_ref, slot_ref, sem):
    j, i = pl.program_id(0), pl.program_id(1)
    n_pass, n_tiles = pl.num_programs(0), pl.num_programs(1)
    tn = stage_ref.shape[2]
    n_used = nt_ref[0]
    cur = te_ref[i]
    first = (i < n_used) & ((i == 0) | (cur != te_ref[jnp.maximum(i - 1, 0)]))

    def fetch(e, jj, slot):
        src = w_hbm.at[e, :, pl.ds(pl.multiple_of(jj * tn, tn), tn)]
        return pltpu.make_async_copy(src, stage_ref.at[slot], sem.at[slot])

    @pl.when((i == 0) & (j == 0))
    def _():
        slot_ref[0] = 1
        fetch(te_ref[0], 0, 0).start()

    @pl.when(first)
    def _():
        slot = 1 - slot_ref[0]
        slot_ref[0] = slot
        fetch(cur, j, slot).wait()

        def same_group(k):
            return (k < n_used) & (te_ref[jnp.minimum(k, n_tiles - 1)] == cur)
        nxt = lax.while_loop(same_group, lambda k: k + 1, i + 1)

        @pl.when(nxt < n_used)
        def _():
            fetch(te_ref[jnp.minimum(nxt, n_tiles - 1)], j, 1 - slot).start()

        @pl.when((nxt >= n_used) & (j + 1 < n_pass))
        def _():
            fetch(te_ref[0], j + 1, 1 - slot).start()

    return first, slot_ref[0]


def _cast_block(stage_ref, slot, wbf_ref):
    def cast(r, c):
        rows = pl.ds(pl.multiple_of(r * CAST_ROWS, CAST_ROWS), CAST_ROWS)
        wbf_ref[rows, :] = stage_ref[slot, rows, :].astype(BF16)
        return c
    lax.fori_loop(0, wbf_ref.shape[0] // CAST_ROWS, cast, 0)


def _gemm1_kernel(te_ref, nt_ref, tv_ref, x_ref, w_hbm, b_ref, o_ref, stage_ref, wbf_ref,
                  slot_ref, sem, *, nc, nblk):
    first, slot = _stream_expert_weights(te_ref, nt_ref, w_hbm, stage_ref, slot_ref, sem)

    def compute(rows, convert):
        words = _load_token_rows(x_ref, rows, nblk)
        x_lo, x_hi = (v.astype(BF16) for v in _unpack_bf16_pair(words))
        half = x_lo.shape[1]
        lane128 = lax.broadcasted_iota(jnp.int32, (rows, V7X_LANES), 1)
        even_idx = (2 * lane128) % V7X_LANES
        for n0 in range(0, wbf_ref.shape[1], nc):
            if convert:
                wbf_ref[:, n0:n0 + nc] = stage_ref[slot, :, n0:n0 + nc].astype(BF16)
            hb = (jnp.dot(x_lo, wbf_ref[:half, n0:n0 + nc], preferred_element_type=F32)
                  + jnp.dot(x_hi, wbf_ref[half:, n0:n0 + nc], preferred_element_type=F32)
                  + b_ref[0, :, n0:n0 + nc])
            nxt = pltpu.roll(hb, nc - 1, axis=1)
            glu = jnp.minimum(hb, SWIGLU_LIMIT)
            lin = jnp.clip(nxt, -SWIGLU_LIMIT, SWIGLU_LIMIT)
            act = glu * jax.nn.sigmoid(SWIGLU_ALPHA * glu) * (lin + 1.0)
            for c in range(nc // (2 * V7X_LANES)):
                lo_blk = act[:, (2 * c) * V7X_LANES:(2 * c + 1) * V7X_LANES]
                hi_blk = act[:, (2 * c + 1) * V7X_LANES:(2 * c + 2) * V7X_LANES]
                out = jnp.where(lane128 < V7X_LANES // 2,
                                jnp.take_along_axis(lo_blk, even_idx, axis=1),
                                jnp.take_along_axis(hi_blk, even_idx, axis=1))
                o0 = n0 // 2 + c * V7X_LANES
                o_ref[:rows, o0:o0 + V7X_LANES] = out.astype(o_ref.dtype)
        if rows < o_ref.shape[0]:
            o_ref[rows:, :] = jnp.zeros((o_ref.shape[0] - rows, o_ref.shape[1]), o_ref.dtype)

    _for_real_rows(pl.program_id(1), nt_ref, tv_ref, o_ref, o_ref.shape[0], compute,
                   first, lambda: _cast_block(stage_ref, slot, wbf_ref))


def _for_real_rows(i, nt_ref, tv_ref, o_ref, tm, compute, first, convert_all):
    used = i < nt_ref[0]
    real = tv_ref[i]
    sizes = [tm >> s for s in range(MOE_TILE_LEVELS) if (tm >> s) % MOE_MIN_ROWS == 0]

    @pl.when(first & (real <= sizes[1]))
    def _():
        convert_all()

    for n, rows in enumerate(sizes):
        fits = real <= rows
        if n + 1 < len(sizes):
            fits = fits & (real > sizes[n + 1])
        if n == 0:
            @pl.when(used & fits & first)
            def _():
                compute(rows, True)
            fits = fits & jnp.logical_not(first)

        @pl.when(used & fits)
        def _(rows=rows):
            compute(rows, False)

    @pl.when(jnp.logical_not(used))
    def _():
        o_ref[...] = jnp.zeros_like(o_ref)


def _gemm1(xin, w1, b1, tile_e, n_tiles_used, tile_rows, tm, nblk):
    n_rows = xin.shape[0] // nblk
    n_e, d, f2 = w1.shape
    tn = min(EXPERT_COL_BLOCK, f2)
    nc = min(GEMM1_COL_CHUNK, tn)
    assert f2 % tn == 0 and d % CAST_ROWS == 0
    n_tiles = n_rows // tm
    assert nc % (2 * V7X_LANES) == 0

    def used(i, nt):
        return jnp.minimum(i, nt[0] - 1)

    return pl.pallas_call(
        functools.partial(_gemm1_kernel, nc=nc, nblk=nblk),
        grid_spec=pltpu.PrefetchScalarGridSpec(
            num_scalar_prefetch=3, grid=(f2 // tn, n_tiles),
            in_specs=[pl.BlockSpec((tm * nblk, V7X_LANES),
                                   lambda j, i, te, nt, tv: (used(i, nt), 0)),
                      pl.BlockSpec(memory_space=pl.ANY),
                      pl.BlockSpec((1, 1, tn), lambda j, i, te, nt, tv: (te[used(i, nt)], 0, j))],
            out_specs=pl.BlockSpec((tm, tn // 2), lambda j, i, te, nt, tv: (i, j)),
            scratch_shapes=[pltpu.VMEM((2, d, tn), F32), pltpu.VMEM((d, tn), BF16),
                            pltpu.SMEM((1,), jnp.int32), pltpu.SemaphoreType.DMA((2,))]),
        out_shape=jax.ShapeDtypeStruct((n_rows, f2 // 2), BF16),
        compiler_params=_cparams(("arbitrary", "arbitrary")),
        name="moe_gemm1",
    )(tile_e, n_tiles_used, tile_rows, xin, w1, b1.reshape(n_e, 1, f2))


def _gemm2_kernel(te_ref, nt_ref, tv_ref, a_ref, w_hbm, b_ref, o_ref, stage_ref, wbf_ref, slot_ref,
                  sem, *, nc, nblk):
    first, slot = _stream_expert_weights(te_ref, nt_ref, w_hbm, stage_ref, slot_ref, sem)
    tm = a_ref.shape[0]

    def compute(rows, convert):
        a = a_ref[:rows, :]
        half = wbf_ref.shape[1] // 2

        def cols(n0):
            if convert:
                wbf_ref[:, n0:n0 + nc] = stage_ref[slot, :, n0:n0 + nc].astype(BF16)
            return (jnp.dot(a, wbf_ref[:, n0:n0 + nc], preferred_element_type=F32)
                    + b_ref[0, :, n0:n0 + nc])

        for n0 in range(0, half, nc):
            words = _pack_bf16_pair(cols(n0), cols(half + n0))
            for c in range(nc // V7X_LANES):
                blk = n0 // V7X_LANES + c
                o_ref[pl.ds(blk, rows, stride=nblk), :] = words[:, c * V7X_LANES:(c + 1) * V7X_LANES]
        if rows < tm:
            o_ref[rows * nblk:, :] = jnp.zeros(((tm - rows) * nblk, o_ref.shape[1]), o_ref.dtype)

    _for_real_rows(pl.program_id(1), nt_ref, tv_ref, o_ref, tm, compute,
                   first, lambda: _cast_block(stage_ref, slot, wbf_ref))


def _gemm2(act, w2, b2, tile_e, n_tiles_used, tile_rows, tm):
    n_rows, f = act.shape
    n_e, _, d = w2.shape
    tn = d
    nc = min(512, tn // 2)
    nblk = d // 2 // V7X_LANES
    assert f % CAST_ROWS == 0
    n_tiles = n_rows // tm

    def used(i, nt):
        return jnp.minimum(i, nt[0] - 1)

    return pl.pallas_call(
        functools.partial(_gemm2_kernel, nc=nc, nblk=nblk),
        grid_spec=pltpu.PrefetchScalarGridSpec(
            num_scalar_prefetch=3, grid=(d // tn, n_tiles),
            in_specs=[pl.BlockSpec((tm, f), lambda j, i, te, nt, tv: (used(i, nt), 0)),
                      pl.BlockSpec(memory_space=pl.ANY),
                      pl.BlockSpec((1, 1, tn), lambda j, i, te, nt, tv: (te[used(i, nt)], 0, j))],
            out_specs=pl.BlockSpec((tm * nblk, V7X_LANES), lambda j, i, te, nt, tv: (i, 0)),
            scratch_shapes=[pltpu.VMEM((2, f, tn), F32), pltpu.VMEM((f, tn), BF16),
                            pltpu.SMEM((1,), jnp.int32), pltpu.SemaphoreType.DMA((2,))]),
        out_shape=jax.ShapeDtypeStruct((n_rows * nblk, V7X_LANES), jnp.uint32),
        compiler_params=_cparams(("arbitrary", "arbitrary")),
        name="moe_gemm2",
    )(tile_e, n_tiles_used, tile_rows, act, w2, b2.reshape(n_e, 1, d))


COMBINE_ROW_GROUP = 64


def _combine_kernel(dest_ref, x1_ref, gate_ref, gt_ref, g_ref, b_ref, y_ref, o_ref, buf, sem,
                    *, alpha, tc, n_tok, nblk):
    i = pl.program_id(0)
    n_steps = pl.num_programs(0)

    def rows(tok):
        return pl.ds(pl.multiple_of(tok * nblk, nblk), nblk)

    def copy(step, slot, k, r):
        src = y_ref.at[rows(dest_ref[k * n_tok + step * tc + r])]
        return pltpu.make_async_copy(src, buf.at[slot, k, rows(r)], sem.at[slot])

    def issue(step, slot):
        def body(r, c):
            for k in range(TOP_K):
                copy(step, slot, k, r).start(priority=k % 2)
            return c
        lax.fori_loop(0, tc, body, 0, unroll=2)

    def drain(slot):
        pltpu.make_async_copy(buf.at[slot], buf.at[slot], sem.at[slot]).wait()

    @pl.when(i == 0)
    def _():
        issue(0, 0)

    def step(slot, prefetch):
        drain(slot)
        for rg in range(tc // COMBINE_ROW_GROUP):
            r0 = rg * COMBINE_ROW_GROUP
            if prefetch:
                for r in range(r0, r0 + COMBINE_ROW_GROUP):
                    for k in range(TOP_K):
                        copy(i + 1, 1 - slot, k, r).start(priority=k % 2)
            rows = slice(r0, r0 + COMBINE_ROW_GROUP)
            gates = gate_ref[rows, :]
            ff_lo = ff_hi = None
            for k in range(TOP_K):
                lo, hi = _unpack_bf16_pair(
                    _load_token_rows(buf.at[slot, k], COMBINE_ROW_GROUP, nblk, tok0=r0))
                gk = gates[:, k:k + 1]
                ff_lo = gk * lo if ff_lo is None else ff_lo + gk * lo
                ff_hi = gk * hi if ff_hi is None else ff_hi + gk * hi
            ff = jnp.concatenate([ff_lo, ff_hi], axis=1)
            y = alpha * x1_ref[rows, :] + (1.0 + gt_ref[0]) * ff
            o_ref[rows, :] = _layer_norm(y, g_ref[...], b_ref[...])

    for slot in range(2):
        for prefetch in (True, False):
            @pl.when((i % 2 == slot) & ((i + 1 < n_steps) == prefetch))
            def _(slot=slot, prefetch=prefetch):
                step(slot, prefetch)


def _combine(dest_flat, x1, gates_tk, gt, g, b, y, seq, alpha):
    t, d = x1.shape
    tc = _pick(seq, (256, 128))
    per_b = seq // tc
    nblk = d // 2 // V7X_LANES
    return pl.pallas_call(
        functools.partial(_combine_kernel, alpha=alpha, tc=tc, n_tok=t, nblk=nblk),
        grid_spec=pltpu.PrefetchScalarGridSpec(
            num_scalar_prefetch=1, grid=(t // tc,),
            in_specs=[pl.BlockSpec((tc, d), lambda i, ds: (i, 0)),
                      pl.BlockSpec((tc, TOP_K), lambda i, ds: (i, 0)),
                      pl.BlockSpec((1, 1, d), lambda i, ds: (i // per_b, 0, 0)),
                      pl.BlockSpec((1, d), lambda i, ds: (0, 0)),
                      pl.BlockSpec((1, d), lambda i, ds: (0, 0)),
                      pl.BlockSpec(memory_space=pl.ANY)],
            out_specs=pl.BlockSpec((tc, d), lambda i, ds: (i, 0)),
            scratch_shapes=[pltpu.VMEM((2, TOP_K, tc * nblk, V7X_LANES), jnp.uint32),
                            pltpu.SemaphoreType.DMA((2,))]),
        out_shape=jax.ShapeDtypeStruct((t, d), F32),
        compiler_params=_cparams(("arbitrary",)),
        name="moe_combine_ln2",
    )(dest_flat, x1, gates_tk, gt[:, None, :], g.reshape(1, d), b.reshape(1, d), y)


MOE_ROW_TILE = 512


def _moe_ffn(h2p, logits_t, x1, gt_f, ln_g, ln_b, w1, b1, w2, b2, seq, alpha):
    tm = MOE_ROW_TILE
    n_e, t = logits_t.shape
    n_rows = -(-(TOP_K * t) // tm) * tm + n_e * tm
    n_tiles = n_rows // tm
    gate_t, dest, tiles, pads = _routing(logits_t, tm, n_tiles)
    dest = dest.reshape(-1)
    tile_e = tiles[0, :n_tiles]
    n_used_rows = tiles[1, :1]
    tile_rows = tiles[2, :n_tiles]
    n_used_tiles = n_used_rows // tm
    nblk = x1.shape[1] // 2 // V7X_LANES
    xin = _dispatch(h2p, dest, pads[0, :, 0], pads[1, :, 0], n_used_rows, n_rows, tm, nblk)
    act = _gemm1(xin, w1, b1, tile_e, n_used_tiles, tile_rows, tm, nblk)
    y = _gemm2(act, w2, b2, tile_e, n_used_tiles, tile_rows, tm)
    return _combine(dest, x1, gate_t.T, gt_f, ln_g, ln_b, y, seq, alpha)


def kernel(x, c, positions, w_ada, b_ada, w_in, hgrn_lb, gnorm_w, w_o, ln1_g, ln1_b,
           router_w, router_b, w1, b1, w2, b2, ln2_g, ln2_b):
    bsz, seq, d = x.shape
    depth = w_ada.shape[0]
    t = bsz * seq
    att_w = d // 2
    hg_w = d - att_w
    alpha = (2.0 * depth) ** 0.25

    lb_all = jnp.cumsum(jax.nn.softmax(hgrn_lb.astype(F32), axis=0), axis=0)
    inv = ROPE_THETA ** (-(jnp.arange(0, ROT_DIM, 2, dtype=F32) / ROT_DIM))
    lane = np.arange(V7X_LANES)
    inv_lane = inv[(lane % ATT_HEAD_DIM) % (ROT_DIM // 2)].reshape(1, V7X_LANES)
    pos_b = jnp.broadcast_to(positions.astype(F32).reshape(t, 1), (t, V7X_LANES))

    x2d = x.reshape(t, d)
    for l in range(depth):
        mod = _adaln(c, w_ada[l], b_ada[l])
        sh_a, sc_a, gt_a, sh_f, sc_f, gt_f = jnp.split(mod, 6, axis=-1)

        proj = _in_proj(x2d, sc_a, sh_a, w_in[l], seq)
        proj3 = proj.reshape(bsz, seq, proj.shape[1])
        q_hp, k_hp, v_hp = _qk_rope(proj, pos_b, inv_lane, bsz, seq, att_w)
        att = _attention(q_hp, k_hp, v_hp)
        rec = _hgrn2(proj3, lb_all[l], gnorm_w[l], att_w, hg_w).reshape(t, hg_w)

        x1, h2p, logits_t = _out_proj(att, rec, x2d, w_o[l], gt_a, sc_f, sh_f, ln1_g[l], ln1_b[l],
                                      router_w[l], router_b[l], seq, alpha)

        x2d = _moe_ffn(h2p, logits_t, x1, gt_f, ln2_g[l], ln2_b[l], w1[l], b1[l], w2[l], b2[l],
                       seq, alpha)
    return x2d.reshape(bsz, seq, d)
```

```python
import functools

import numpy as np
import jax
import jax.numpy as jnp
from jax import lax
from jax.experimental import pallas as pl
from jax.experimental.pallas import tpu as pltpu

F32 = jnp.float32
BF16 = jnp.bfloat16

V7X_LANES = 128
V7X_VMEM_LIMIT = 56 * 1024 * 1024

ATT_HEAD_DIM = 64
DILATED_PAIRS = ((128, 1), (512, 4), (2048, 16))
ATT_BLOCK = 128
ROT_DIM = ATT_HEAD_DIM // 4
ROPE_THETA = 500000.0
HG_EXPAND = 128
HG_CHUNK = 64
HG_SUB = 16
TOP_K = 4
SWIGLU_ALPHA = 1.702
SWIGLU_LIMIT = 7.0
LN_EPS = 1e-5
RMS_EPS = 1e-6
NEG_INF = -1e30

LN_2 = float(np.log(2.0))
Q_SCALE = ATT_HEAD_DIM ** -0.5 / LN_2

NT_DIMS = (((1,), (1,)), ((), ()))


def _pick(n, candidates):
    for c in candidates:
        if n % c == 0:
            return c
    raise ValueError(f"no tile in {candidates} divides {n}")


def _cparams(sem):
    return pltpu.CompilerParams(dimension_semantics=sem, vmem_limit_bytes=V7X_VMEM_LIMIT)


def _adaln_kernel(c_ref, w_ref, b_ref, o_ref):
    c = c_ref[...]
    s = c * jax.nn.sigmoid(c)
    o_ref[...] = jnp.dot(s.astype(BF16), w_ref[...].astype(BF16),
                         preferred_element_type=F32) + b_ref[...]


def _adaln(c, w, b):
    bsz, d = c.shape
    n = w.shape[1]
    rows = 8
    cp = jnp.zeros((rows, d), F32).at[:bsz].set(c)
    tn = _pick(n, (1024, 512, 256, 128))
    out = pl.pallas_call(
        _adaln_kernel,
        grid=(n // tn,),
        in_specs=[pl.BlockSpec((rows, d), lambda j: (0, 0)),
                  pl.BlockSpec((d, tn), lambda j: (0, j)),
                  pl.BlockSpec((1, tn), lambda j: (0, j))],
        out_specs=pl.BlockSpec((rows, tn), lambda j: (0, j)),
        out_shape=jax.ShapeDtypeStruct((rows, n), F32),
        compiler_params=_cparams(("arbitrary",)),
        name="adaln",
    )(cp, w, b.reshape(1, n))
    return out[:bsz]


def _in_proj_kernel(x_ref, sc_ref, sh_ref, w_hbm, o_ref, wbf_ref, h_ref, stage_ref, sem):
    m, n = pl.program_id(0), pl.program_id(1)
    n_col = pl.num_programs(1)
    tn = o_ref.shape[1]

    def fetch(nn):
        src = w_hbm.at[:, pl.ds(pl.multiple_of(nn * tn, tn), tn)]
        return pltpu.make_async_copy(src, stage_ref, sem)

    @pl.when((m == 0) & (n == 0))
    def _():
        fetch(0).start()

    @pl.when(m == 0)
    def _():
        fetch(n).wait()
        wbf_ref[n] = stage_ref[...].astype(BF16)

        @pl.when(n + 1 < n_col)
        def _():
            fetch(n + 1).start()

    @pl.when(n == 0)
    def _():
        h_ref[...] = (x_ref[...] * (1.0 + sc_ref[0]) + sh_ref[0]).astype(BF16)

    o_ref[...] = jnp.dot(h_ref[...], wbf_ref[n], preferred_element_type=F32).astype(o_ref.dtype)


def _in_proj(x2d, sc, sh, w, seq):
    t, d = x2d.shape
    nc = w.shape[1]
    tm = _pick(seq, (512, 256, 128))
    tn = _pick(nc, (1024, 512, 256, 128))
    per_b = seq // tm
    vec = pl.BlockSpec((1, 1, d), lambda i, j: (i // per_b, 0, 0))
    return pl.pallas_call(
        _in_proj_kernel,
        grid=(t // tm, nc // tn),
        in_specs=[pl.BlockSpec((tm, d), lambda i, j: (i, 0)), vec, vec,
                  pl.BlockSpec(memory_space=pl.ANY)],
        out_specs=pl.BlockSpec((tm, tn), lambda i, j: (i, j)),
        out_shape=jax.ShapeDtypeStruct((t, nc), BF16),
        scratch_shapes=[pltpu.VMEM((nc // tn, d, tn), BF16), pltpu.VMEM((tm, d), BF16),
                        pltpu.VMEM((d, tn), F32), pltpu.SemaphoreType.DMA(())],
        compiler_params=_cparams(("arbitrary", "arbitrary")),
        name="in_proj",
    )(x2d, sc[:, None, :], sh[:, None, :], w)


def _rope_kernel(q_ref, k_ref, v_ref, pos_ref, inv_ref, qo_ref, ko_ref, vo_ref):
    tm = pos_ref.shape[0]
    lane = lax.broadcasted_iota(jnp.int32, (tm, V7X_LANES), 1)
    lh = lane % ATT_HEAD_DIM
    half = ROT_DIM // 2
    ang = pos_ref[...] * inv_ref[...]
    cs = jnp.where(lh < ROT_DIM, jnp.cos(ang), 1.0)
    sn = jnp.sin(ang)
    sn = jnp.where(lh < half, -sn, jnp.where(lh < ROT_DIM, sn, 0.0))

    def rope(t):
        swapped = jnp.where(lh < half,
                            pltpu.roll(t, V7X_LANES - half, axis=1),
                            pltpu.roll(t, half, axis=1))
        return t * cs + swapped * sn

    for h in range(qo_ref.shape[1]):
        lanes = slice(h * V7X_LANES, (h + 1) * V7X_LANES)
        qo_ref[0, h] = rope(q_ref[:, lanes].astype(F32)) * Q_SCALE
        ko_ref[0, h] = rope(k_ref[:, lanes].astype(F32))
        vo_ref[0, h] = v_ref[:, lanes].astype(F32)


def _qk_rope(proj, pos_b, inv_lane, bsz, seq, att_w):
    t = proj.shape[0]
    hp = att_w // V7X_LANES
    tm = _pick(seq, (256, 128))
    per_b = seq // tm
    out_spec = pl.BlockSpec((1, hp, tm, V7X_LANES), lambda i: (i // per_b, 0, i % per_b, 0))
    shp = jax.ShapeDtypeStruct((bsz, hp, seq, V7X_LANES), F32)
    return pl.pallas_call(
        _rope_kernel,
        grid=(t // tm,),
        in_specs=[pl.BlockSpec((tm, att_w), lambda i: (i, 0)),
                  pl.BlockSpec((tm, att_w), lambda i: (i, 1)),
                  pl.BlockSpec((tm, att_w), lambda i: (i, 2)),
                  pl.BlockSpec((tm, V7X_LANES), lambda i: (i, 0)),
                  pl.BlockSpec((1, V7X_LANES), lambda i: (0, 0))],
        out_specs=[out_spec, out_spec, out_spec],
        out_shape=[shp, shp, shp],
        compiler_params=_cparams(("arbitrary",)),
        name="qk_rope",
    )(proj, proj, proj, pos_b, inv_lane)


ATT_TASKS_PER_STEP = 8


def _attn_kernel(q_in, k_in, v_in, out_ref, qs, ks, vs, sm, no_lo, nl_lo, no_s, nl_s, nat, *, seq):
    qn = ATT_BLOCK
    n_task = seq // qn
    n_head = V7X_LANES // ATT_HEAD_DIM
    gc = min(ATT_TASKS_PER_STEP, n_task)
    (_, d_lo), (_, dm), (_, d_hi) = DILATED_PAIRS
    g_hi = d_hi // dm
    rows_m = seq // dm
    lane = lax.broadcasted_iota(jnp.int32, (qn, V7X_LANES), 1)
    qi = lax.broadcasted_iota(jnp.int32, (gc, qn, 2 * qn), 1)
    kj = lax.broadcasted_iota(jnp.int32, (gc, qn, 2 * qn), 2)
    dist = qi + qn - kj
    band = (dist >= 0) & (dist <= qn)
    zeros = jnp.zeros((qn, V7X_LANES), BF16)
    srcs = (q_in, k_in, v_in)

    for a, src in enumerate(srcs):
        for r in range(dm):
            sm[a, r] = src[0, 0, pl.ds(r, rows_m, stride=dm), :]

    def task_index(d, r, n):
        if d == d_lo:
            return None, slice(n * qn, (n + 1) * qn)
        if d == dm:
            return r, slice(n * qn, (n + 1) * qn)
        return r % dm, pl.ds(r // dm + n * qn * g_hi, qn, stride=g_hi)

    for bi, (window, d) in enumerate(DILATED_PAIRS):
        nb = seq // (d * qn)
        tasks = [(r, n) for r in range(d) for n in range(nb)]
        for g, (r, n) in enumerate(tasks):
            sel, rows = task_index(d, r, n)

            def operand(a, sel=sel, rows=rows):
                return srcs[a][0, 0, rows, :] if sel is None else sm[a, sel, rows, :]

            qv = operand(0)
            for h in range(n_head):
                qs[h, g] = jnp.where(lane // ATT_HEAD_DIM == h, qv, 0.0).astype(BF16)
            for a, dst in ((1, ks), (2, vs)):
                blk = operand(a).astype(BF16)
                dst[g, qn:, :] = blk
                if n + 1 < nb:
                    dst[g + 1, :qn, :] = blk
                if n == 0:
                    dst[g, :qn, :] = zeros

        for g0 in range(0, n_task, gc):
            gsl = slice(g0, g0 + gc)
            gidx = g0 + lax.broadcasted_iota(jnp.int32, (gc, qn, 2 * qn), 0)
            valid = band & ((gidx % nb != 0) | (kj >= qn))
            k = ks[gsl]
            v = vs[gsl]
            o = None
            for h in range(n_head):
                s = jnp.einsum("gqd,gkd->gqk", qs[h, gsl], k, preferred_element_type=F32)
                s = jnp.where(valid, s, NEG_INF)
                m = jnp.max(s, axis=-1, keepdims=True)
                p = jnp.exp2(s - m)
                den = jnp.sum(p, axis=-1, keepdims=True)
                oh = jnp.einsum("gqk,gkd->gqd", p.astype(BF16), v,
                                preferred_element_type=F32) / den
                lh = jnp.broadcast_to(m * LN_2 + jnp.log(den), oh.shape)
                if o is None:
                    o, lse = oh, lh
                else:
                    in_head = lane[None] // ATT_HEAD_DIM == h
                    o, lse = jnp.where(in_head, oh, o), jnp.where(in_head, lh, lse)
            for t in range(gc):
                sel, rows = task_index(d, *tasks[g0 + t])
                if sel is None:
                    no_lo[rows, :] = o[t]
                    nl_lo[rows, :] = lse[t]
                else:
                    no_s[bi - 1, sel, rows, :] = o[t]
                    nl_s[bi - 1, sel, rows, :] = lse[t]

    for r in range(dm):
        for c in range(rows_m // qn):
            rows = slice(c * qn, (c + 1) * qn)
            seq_rows = pl.ds(r + dm * c * qn, qn, stride=dm)
            ls = [nl_lo[seq_rows, :], nl_s[0, r, rows, :], nl_s[1, r, rows, :]]
            os_ = [no_lo[seq_rows, :], no_s[0, r, rows, :], no_s[1, r, rows, :]]
            mx = functools.reduce(jnp.maximum, ls)
            ws = [jnp.exp(l - mx) for l in ls]
            tot = functools.reduce(lambda x, y: x + y, ws)
            acc = functools.reduce(lambda x, y: x + y, [w * o for w, o in zip(ws, os_)])
            nat[seq_rows, :] = acc / tot
    out_ref[0, 0] = nat[...].astype(out_ref.dtype)


def _attention(q_hp, k_hp, v_hp):
    bsz, hp, seq, _ = q_hp.shape
    (_, d_lo), (_, dm), (_, d_hi) = DILATED_PAIRS
    assert d_lo == 1 and d_hi % dm == 0
    for window, d in DILATED_PAIRS:
        assert window // d == ATT_BLOCK and seq % (d * ATT_BLOCK) == 0
    n_task = seq // ATT_BLOCK
    assert n_task % min(ATT_TASKS_PER_STEP, n_task) == 0
    n_head = V7X_LANES // ATT_HEAD_DIM
    streams = (dm, seq // dm, V7X_LANES)
    spec = pl.BlockSpec((1, 1, seq, V7X_LANES), lambda b, h: (b, h, 0, 0))
    return pl.pallas_call(
        functools.partial(_attn_kernel, seq=seq),
        grid=(bsz, hp),
        in_specs=[spec, spec, spec],
        out_specs=spec,
        out_shape=jax.ShapeDtypeStruct((bsz, hp, seq, V7X_LANES), BF16),
        scratch_shapes=[pltpu.VMEM((n_head, n_task, ATT_BLOCK, V7X_LANES), BF16),
                        pltpu.VMEM((n_task, 2 * ATT_BLOCK, V7X_LANES), BF16),
                        pltpu.VMEM((n_task, 2 * ATT_BLOCK, V7X_LANES), BF16),
                        pltpu.VMEM((3,) + streams, F32),
                        pltpu.VMEM((seq, V7X_LANES), F32), pltpu.VMEM((seq, V7X_LANES), F32),
                        pltpu.VMEM((2,) + streams, F32), pltpu.VMEM((2,) + streams, F32),
                        pltpu.VMEM((seq, V7X_LANES), F32)],
        compiler_params=_cparams(("arbitrary", "arbitrary")),
        name="dilated_attn",
    )(q_hp, k_hp, v_hp)


def _hgrn_kernel(q_ref, f_ref, i_ref, g_ref, lb_ref, gw_ref, sums_ref, o_ref,
                 st_ref, hl_ref, kk_ref, cum_ref, *, heads, ts):
    @pl.when(pl.program_id(2) == 0)
    def _():
        st_ref[...] = jnp.zeros_like(st_ref)

    c_len = HG_CHUNK
    lanes = heads * HG_EXPAND
    lb = lb_ref[...]
    for c in range(ts // c_len):
        rows = slice(c * c_len, (c + 1) * c_len)
        f = lb + (1.0 - lb) * jax.nn.sigmoid(f_ref[0, rows, :].astype(F32))
        kk_ref[rows, :] = 1.0 - f
        hi, lo = _bf16_split(jnp.log(f))
        hl_ref[rows, :lanes] = hi
        hl_ref[rows, lanes:] = lo
    both = jnp.dot(sums_ref[...], hl_ref[...], preferred_element_type=F32)
    cum_ref[...] = both[:, :lanes] + both[:, lanes:]

    n_sub = c_len // HG_SUB

    def sub_block_anchor(b):
        mids = [b[i * HG_SUB + HG_SUB // 2:i * HG_SUB + HG_SUB // 2 + 1, :] for i in range(n_sub)]
        return jnp.concatenate([jnp.broadcast_to(m, (HG_SUB, m.shape[1])) for m in mids], axis=0)

    gi = lax.broadcasted_iota(jnp.int32, (heads * n_sub, HG_SUB, c_len), 0) % n_sub
    qi = lax.broadcasted_iota(jnp.int32, (heads * n_sub, HG_SUB, c_len), 1)
    si = lax.broadcasted_iota(jnp.int32, (heads * n_sub, HG_SUB, c_len), 2)
    causal = si <= gi * HG_SUB + qi
    key_row = lax.broadcasted_iota(jnp.int32, (c_len, HG_EXPAND), 0)

    for c in range(ts // c_len):
        rows = slice(c * c_len, (c + 1) * c_len)
        qts, kts, vbs = [], [], []
        for h in range(heads):
            lsl = slice(h * HG_EXPAND, (h + 1) * HG_EXPAND)
            b = cum_ref[rows, lsl]
            anchor = sub_block_anchor(b)
            kk = kk_ref[rows, lsl]
            qt = (q_ref[0, rows, lsl].astype(F32) * jnp.exp(b - anchor)).astype(BF16)
            qts.append(qt.reshape(n_sub, HG_SUB, HG_EXPAND))
            vb = i_ref[0, rows, lsl].astype(BF16)
            for i in range(n_sub):
                hi_r = (i + 1) * HG_SUB
                kt = kk * jnp.exp(anchor[i * HG_SUB:i * HG_SUB + 1, :] - b)
                kts.append(jnp.where(key_row < hi_r, kt, 0.0).astype(BF16))
                vbs.append(vb)
        a = jnp.einsum("gqk,gsk->gqs", jnp.concatenate(qts, axis=0), jnp.stack(kts),
                       preferred_element_type=F32)
        a = jnp.where(causal, a, 0.0).astype(BF16)
        o_intra = jnp.einsum("gqs,gsv->gqv", a, jnp.stack(vbs), preferred_element_type=F32)

        for h in range(heads):
            lsl = slice(h * HG_EXPAND, (h + 1) * HG_EXPAND)
            b = cum_ref[rows, lsl]
            b_last = jnp.broadcast_to(b[c_len - 1:c_len, :], b.shape)
            kk = kk_ref[rows, lsl]
            q = q_ref[0, rows, lsl].astype(F32)
            v = i_ref[0, rows, lsl].astype(F32)
            st = st_ref[h]
            o_inter = lax.dot_general((q * jnp.exp(b)).astype(BF16), st.astype(BF16), NT_DIMS,
                                      preferred_element_type=F32)
            o = o_inter + o_intra[h * n_sub:(h + 1) * n_sub].reshape(c_len, HG_EXPAND)
            kl = kk * jnp.exp(b_last - b)
            upd = jnp.dot(v.T.astype(BF16), kl.astype(BF16), preferred_element_type=F32)
            st_ref[h] = st * jnp.exp(b_last[0:1, :]) + upd
            o = o * lax.rsqrt(jnp.mean(o * o, axis=-1, keepdims=True) + RMS_EPS)
            g = g_ref[0, rows, lsl].astype(F32)
            o = o * gw_ref[:, lsl] * (g * jax.nn.sigmoid(g))
            o_ref[0, rows, lsl] = o.astype(o_ref.dtype)


def _hgrn_cumsum_matrix(ts):
    t = np.arange(ts)[:, None]
    s = np.arange(ts)[None, :]
    return (((t // HG_CHUNK) == (s // HG_CHUNK)) & (s <= t)).astype(np.float32)


def _hgrn2(proj3, lb, gw, att_w, hg_w):
    bsz, seq, _ = proj3.shape
    n_heads = hg_w // HG_EXPAND
    heads = _pick(n_heads, (4, 2, 1))
    lanes = heads * HG_EXPAND
    ts = _pick(seq, (256, 128, 64))
    base = 3 * att_w
    assert base % lanes == 0 and hg_w % lanes == 0

    def seg(k):
        off = (base + k * hg_w) // lanes
        return pl.BlockSpec((1, ts, lanes), lambda b, h, s: (b, s, off + h))

    vec = pl.BlockSpec((1, lanes), lambda b, h, s: (0, h))
    sums = jnp.asarray(_hgrn_cumsum_matrix(ts), BF16)
    return pl.pallas_call(
        functools.partial(_hgrn_kernel, heads=heads, ts=ts),
        grid=(bsz, hg_w // lanes, seq // ts),
        in_specs=[seg(0), seg(1), seg(2), seg(3), vec, vec,
                  pl.BlockSpec((ts, ts), lambda b, h, s: (0, 0))],
        out_specs=pl.BlockSpec((1, ts, lanes), lambda b, h, s: (b, s, h)),
        out_shape=jax.ShapeDtypeStruct((bsz, seq, hg_w), BF16),
        scratch_shapes=[pltpu.VMEM((heads, HG_EXPAND, HG_EXPAND), F32),
                        pltpu.VMEM((ts, 2 * lanes), BF16),
                        pltpu.VMEM((ts, lanes), F32),
                        pltpu.VMEM((ts, lanes), F32)],
        compiler_params=_cparams(("arbitrary", "arbitrary", "arbitrary")),
        name="hgrn2",
    )(proj3, proj3, proj3, proj3, lb.reshape(1, hg_w), gw.reshape(1, hg_w), sums)


def _layer_norm(y, g, b):
    mu = jnp.mean(y, axis=-1, keepdims=True)
    yc = y - mu
    var = jnp.mean(yc * yc, axis=-1, keepdims=True)
    return yc * lax.rsqrt(var + LN_EPS) * g + b


def _bf16_split(x):
    hi = x.astype(BF16)
    lo = (x - hi.astype(F32)).astype(BF16)
    return hi, lo


def _pack_bf16_pair(lo, hi):
    lo_bits = lax.bitcast_convert_type(lo.astype(BF16).astype(F32), jnp.uint32)
    hi_bits = lax.bitcast_convert_type(hi.astype(BF16).astype(F32), jnp.uint32)
    return (lo_bits >> 16) | (hi_bits & jnp.uint32(0xFFFF0000))


def _store_token_rows(ref, tok0, words):
    n, width = words.shape
    nblk = width // V7X_LANES
    for c in range(nblk):
        ref[pl.ds(tok0 * nblk + c, n, stride=nblk), :] = words[:, c * V7X_LANES:(c + 1) * V7X_LANES]


def _load_token_rows(ref, n, nblk, tok0=0):
    return jnp.concatenate([ref[pl.ds(tok0 * nblk + c, n, stride=nblk), :] for c in range(nblk)],
                           axis=1)


def _unpack_bf16_pair(words):
    lo = lax.bitcast_convert_type(words << 16, F32)
    hi = lax.bitcast_convert_type(words & jnp.uint32(0xFFFF0000), F32)
    return lo, hi


def _load_weight_bf16(w_hbm, wbf_ref, stage_ref, sem):
    rows = stage_ref.shape[1]
    n_slab = w_hbm.shape[0] // rows

    def fetch(s, slot):
        return pltpu.make_async_copy(w_hbm.at[pl.ds(s * rows, rows)], stage_ref.at[slot],
                                     sem.at[slot])

    fetch(0, 0).start()
    for s in range(n_slab):
        if s + 1 < n_slab:
            fetch(s + 1, (s + 1) % 2).start()
        fetch(s, s % 2).wait()
        wbf_ref[s * rows:(s + 1) * rows, :] = stage_ref[s % 2].astype(BF16)


def _out_proj_kernel(att_ref, rec_ref, x_ref, wo_hbm, gt_ref, sc_ref, sh_ref, g_ref, b_ref,
                     rw_ref, rb_ref, x1_ref, h2_ref, lg_ref, wobf_ref, rwhi_ref, rwlo_ref,
                     stage_ref, hi_ref, lo_ref, mix_a, mix_b, sem, *, alpha, att_w):
    i = pl.program_id(0)

    @pl.when(i == 0)
    def _():
        _load_weight_bf16(wo_hbm, wobf_ref, stage_ref, sem)
        hi, lo = _bf16_split(rw_ref[...])
        rwhi_ref[...] = hi
        rwlo_ref[...] = lo
        mix_b[...] = jnp.zeros_like(mix_b)

    def step(mix_out, mix_in):
        att = jnp.concatenate([att_ref[0, h] for h in range(att_ref.shape[1])], axis=1)
        mix_out[...] = (jnp.dot(att, wobf_ref[:att_w, :], preferred_element_type=F32)
                        + jnp.dot(rec_ref[...], wobf_ref[att_w:, :], preferred_element_type=F32))
        half = x_ref.shape[1] // 2
        for r in range(x_ref.shape[0] // LN_ROW_GROUP):
            rows = slice(r * LN_ROW_GROUP, (r + 1) * LN_ROW_GROUP)
            y = alpha * x_ref[rows, :] + (1.0 + gt_ref[0]) * mix_in[rows, :]
            x1 = _layer_norm(y, g_ref[...], b_ref[...])
            x1_ref[rows, :] = x1
            h2 = x1 * (1.0 + sc_ref[0]) + sh_ref[0]
            hi, lo = _bf16_split(h2)
            hi_ref[rows, :] = hi
            lo_ref[rows, :] = lo
            _store_token_rows(h2_ref, r * LN_ROW_GROUP,
                              _pack_bf16_pair(h2[:, :half], h2[:, half:]))
        lg = (jnp.dot(hi_ref[...], rwhi_ref[...], preferred_element_type=F32)
              + jnp.dot(lo_ref[...], rwhi_ref[...], preferred_element_type=F32)
              + jnp.dot(hi_ref[...], rwlo_ref[...], preferred_element_type=F32))
        lg_ref[...] = lg.T[:lg_ref.shape[0], :] + rb_ref[...]

    @pl.when(i % 2 == 0)
    def _():
        step(mix_a, mix_b)

    @pl.when(i % 2 == 1)
    def _():
        step(mix_b, mix_a)


OUT_PROJ_STAGE_ROWS = 256
LN_ROW_GROUP = 16


def _out_proj(att, rec, x2d, wo, gt, sc, sh, g, b, rw, rb, seq, alpha):
    t, d = x2d.shape
    hp = att.shape[1]
    att_w = hp * V7X_LANES
    n_e = rw.shape[1]
    assert n_e <= V7X_LANES and d % OUT_PROJ_STAGE_ROWS == 0
    rw_pad = jnp.zeros((d, V7X_LANES), F32).at[:, :n_e].set(rw)
    nblk = d // 2 // V7X_LANES
    tm = _pick(seq, (512, 256, 128))
    per_b = seq // tm
    n_tile = t // tm
    mm = lambda i: jnp.minimum(i, n_tile - 1)
    ep = lambda i: jnp.maximum(i - 1, 0)
    vec3 = pl.BlockSpec((1, 1, d), lambda i: (ep(i) // per_b, 0, 0))
    full = lambda shape: pl.BlockSpec(shape, lambda i: (0,) * len(shape))
    return pl.pallas_call(
        functools.partial(_out_proj_kernel, alpha=alpha, att_w=att_w),
        grid=(n_tile + 1,),
        in_specs=[pl.BlockSpec((1, hp, tm, V7X_LANES),
                               lambda i: (mm(i) // per_b, 0, mm(i) % per_b, 0)),
                  pl.BlockSpec((tm, d - att_w), lambda i: (mm(i), 0)),
                  pl.BlockSpec((tm, d), lambda i: (ep(i), 0)),
                  pl.BlockSpec(memory_space=pl.ANY), vec3, vec3, vec3, full((1, d)), full((1, d)),
                  full((d, V7X_LANES)), full((n_e, 1))],
        out_specs=[pl.BlockSpec((tm, d), lambda i: (ep(i), 0)),
                   pl.BlockSpec((tm * nblk, V7X_LANES), lambda i: (ep(i), 0)),
                   pl.BlockSpec((n_e, tm), lambda i: (0, ep(i)))],
        out_shape=[jax.ShapeDtypeStruct((t, d), F32),
                   jax.ShapeDtypeStruct((t * nblk, V7X_LANES), jnp.uint32),
                   jax.ShapeDtypeStruct((n_e, t), F32)],
        scratch_shapes=[pltpu.VMEM((d, d), BF16), pltpu.VMEM((d, V7X_LANES), BF16),
                        pltpu.VMEM((d, V7X_LANES), BF16),
                        pltpu.VMEM((2, OUT_PROJ_STAGE_ROWS, d), F32),
                        pltpu.VMEM((tm, d), BF16), pltpu.VMEM((tm, d), BF16),
                        pltpu.VMEM((tm, d), F32), pltpu.VMEM((tm, d), F32),
                        pltpu.SemaphoreType.DMA((2,))],
        compiler_params=_cparams(("arbitrary",)),
        name="out_proj_ln1",
    )(att, rec, x2d, wo, gt[:, None, :], sc[:, None, :], sh[:, None, :],
      g.reshape(1, d), b.reshape(1, d), rw_pad, rb.reshape(n_e, 1))


def _routing_kernel(lg_ref, tri_ref, low_ref, gate_ref, dest_ref, tile_ref, pad_ref, sel_scr, rk_scr,
                    *, blk, tm):
    n_e, t = lg_ref.shape
    eidx = lax.broadcasted_iota(jnp.int32, (n_e, t), 0)
    cur = lg_ref[...]
    vals, idxs = [], []
    for _ in range(TOP_K):
        m = jnp.max(cur, axis=0, keepdims=True)
        ik = jnp.min(jnp.where(cur == m, eidx, n_e), axis=0, keepdims=True)
        cur = jnp.where(eidx == ik, -jnp.inf, cur)
        vals.append(m)
        idxs.append(ik)
    es = [jnp.exp(v - vals[0]) for v in vals]
    den = functools.reduce(lambda a, b: a + b, es)
    for k in range(TOP_K):
        gate_ref[k:k + 1, :] = es[k] / den
    sel = functools.reduce(lambda a, b: a | b, [eidx == ik for ik in idxs])
    sel_scr[...] = jnp.where(sel, 1.0, 0.0)

    tri = tri_ref[...]
    carry = jnp.zeros((n_e, 1), F32)
    for j in range(t // blk):
        sb = sel_scr[:, j * blk:(j + 1) * blk]
        pre = jnp.dot(sb.astype(BF16), tri, preferred_element_type=F32)
        rk_scr[:, j * blk:(j + 1) * blk] = pre + carry
        carry = carry + jnp.sum(sb, axis=1, keepdims=True)

    counts = jnp.broadcast_to(carry, (n_e, V7X_LANES))
    padded = jnp.floor((counts + (tm - 1)) * (1.0 / tm)) * tm
    pends = jnp.dot(low_ref[...], padded, precision=lax.Precision.HIGHEST,
                    preferred_element_type=F32)
    pstarts = pends - padded
    row0 = rk_scr[...] + pstarts[:, 0:1]
    for k in range(TOP_K):
        dest_ref[k:k + 1, :] = jnp.sum(jnp.where(eidx == idxs[k], row0, 0.0), axis=0,
                                       keepdims=True).astype(jnp.int32)
    starts = (lax.broadcasted_iota(jnp.int32, (n_e, tile_ref.shape[1]), 1) * tm).astype(F32)
    owner = jnp.sum(jnp.where(pends[:, 0:1] <= starts, 1.0, 0.0), axis=0, keepdims=True)
    tile_ref[0:1, :] = jnp.minimum(owner, n_e - 1.0).astype(jnp.int32)
    tile_ref[1:2, :] = jnp.broadcast_to(jnp.max(pends[:, 0:1], axis=0, keepdims=True),
                                        (1, tile_ref.shape[1])).astype(jnp.int32)
    e_tile = lax.broadcasted_iota(jnp.int32, starts.shape, 0).astype(F32)
    row_end = jnp.sum(jnp.where(e_tile == owner, (pstarts + counts)[:, 0:1], 0.0), axis=0,
                      keepdims=True)
    tile_ref[2:3, :] = jnp.clip(row_end - starts[0:1, :], 0.0, float(tm)).astype(jnp.int32)
    pad_ref[0] = (pstarts + counts).astype(jnp.int32)
    pad_ref[1] = (padded - counts).astype(jnp.int32)


def _routing(logits_t, tm, n_tiles):
    n_e, t = logits_t.shape
    assert tm & (tm - 1) == 0
    blk = _pick(t, (256, 128))
    tri = jnp.asarray(np.triu(np.ones((blk, blk), np.float32), 1), BF16)
    low = jnp.asarray(np.tril(np.ones((n_e, n_e), np.float32)))
    ntp = -(-n_tiles // V7X_LANES) * V7X_LANES
    full = lambda shape: pl.BlockSpec(shape, lambda: (0,) * len(shape))
    return pl.pallas_call(
        functools.partial(_routing_kernel, blk=blk, tm=tm),
        in_specs=[full((n_e, t)), full((blk, blk)), full((n_e, n_e))],
        out_specs=[full((TOP_K, t)), full((TOP_K, t)), full((3, ntp)), full((2, n_e, V7X_LANES))],
        out_shape=[jax.ShapeDtypeStruct((TOP_K, t), F32),
                   jax.ShapeDtypeStruct((TOP_K, t), jnp.int32),
                   jax.ShapeDtypeStruct((3, ntp), jnp.int32),
                   jax.ShapeDtypeStruct((2, n_e, V7X_LANES), jnp.int32)],
        scratch_shapes=[pltpu.VMEM((n_e, t), F32), pltpu.VMEM((n_e, t), F32)],
        compiler_params=pltpu.CompilerParams(vmem_limit_bytes=V7X_VMEM_LIMIT),
        name="routing",
    )(logits_t, tri, low)


def _dispatch_kernel(dest_ref, pad_off_ref, pad_n_ref, nrow_ref, h_ref, o_ref, zero_ref, sem, zsem,
                     *, n_tok, tq, tm, n_e, nblk):
    step = pl.program_id(0)
    base = step * tq

    def rows(tok, n=1):
        return pl.ds(pl.multiple_of(tok * nblk, nblk), n * nblk)

    def fill(off, n):
        return pltpu.make_async_copy(zero_ref.at[rows(0, n)], o_ref.at[rows(off, n)], zsem)

    pieces = [1 << s for s in range(tm.bit_length() - 2, -1, -1)]

    def pad_rows(wait):
        def per_expert(e, c):
            off = pad_off_ref[e]
            n = pad_n_ref[e]
            for p in pieces:
                hit = (n & p) != 0

                @pl.when(hit)
                def _(off=off, p=p):
                    cp = fill(off, p)
                    cp.wait() if wait else cp.start()

                off = off + jnp.where(hit, p, 0)
            return c
        lax.fori_loop(0, n_e, per_expert, 0)

        def per_tile(i, c):
            cp = fill(i * tm, tm)
            cp.wait() if wait else cp.start()
            return c
        lax.fori_loop(nrow_ref[0] // tm, o_ref.shape[0] // (tm * nblk), per_tile, 0)

    @pl.when(step == 0)
    def _():
        zero_ref[...] = jnp.zeros_like(zero_ref)
        pad_rows(False)

    def start(r, c):
        for k in range(TOP_K):
            dst = o_ref.at[rows(dest_ref[k * n_tok + base + r])]
            pltpu.make_async_copy(h_ref.at[rows(r)], dst, sem).start(priority=k % 2)
        return c

    lax.fori_loop(0, tq, start, 0, unroll=4)
    for _ in range(TOP_K):
        pltpu.make_async_copy(h_ref, h_ref, sem).wait()

    @pl.when(step == 0)
    def _():
        pad_rows(True)


def _dispatch(h2p, dest_flat, pad_off, pad_n, n_used_rows, n_rows, tm, nblk):
    t = h2p.shape[0] // nblk
    n_e = pad_off.shape[0]
    tq = _pick(t, (256, 128))
    return pl.pallas_call(
        functools.partial(_dispatch_kernel, n_tok=t, tq=tq, tm=tm, n_e=n_e, nblk=nblk),
        grid_spec=pltpu.PrefetchScalarGridSpec(
            num_scalar_prefetch=4, grid=(t // tq,),
            in_specs=[pl.BlockSpec((tq * nblk, V7X_LANES), lambda i, *_: (i, 0))],
            out_specs=pl.BlockSpec(memory_space=pl.ANY),
            scratch_shapes=[pltpu.VMEM((tm * nblk, V7X_LANES), h2p.dtype),
                            pltpu.SemaphoreType.DMA(()), pltpu.SemaphoreType.DMA(())]),
        out_shape=jax.ShapeDtypeStruct((n_rows * nblk, V7X_LANES), h2p.dtype),
        compiler_params=pltpu.CompilerParams(dimension_semantics=("arbitrary",),
                                             has_side_effects=True,
                                             vmem_limit_bytes=V7X_VMEM_LIMIT),
        name="moe_dispatch",
    )(dest_flat, pad_off, pad_n, n_used_rows, h2p)


EXPERT_COL_BLOCK = 2048
GEMM1_COL_CHUNK = 256
MOE_TILE_LEVELS = 3
MOE_MIN_ROWS = 16
CAST_ROWS = 256


def _stream_expert_weights(te_ref, nt_ref, w_hbm, stage_ref, slot_ref, sem):
    j, i = pl.program_id(0), pl.program_id(1)
    n_pass, n_tiles = pl.num_programs(0), pl.num_programs(1)
    tn = stage_ref.shape[2]
    n_used = nt_ref[0]
    cur = te_ref[i]
    first = (i < n_used) & ((i == 0) | (cur != te_ref[jnp.maximum(i - 1, 0)]))

    def fetch(e, jj, slot):
        src = w_hbm.at[e, :, pl.ds(pl.multiple_of(jj * tn, tn), tn)]
        return pltpu.make_async_copy(src, stage_ref.at[slot], sem.at[slot])

    @pl.when((i == 0) & (j == 0))
    def _():
        slot_ref[0] = 1
        fetch(te_ref[0], 0, 0).start()

    @pl.when(first)
    def _():
        slot = 1 - slot_ref[0]
        slot_ref[0] = slot
        fetch(cur, j, slot).wait()

        def same_group(k):
            return (k < n_used) & (te_ref[jnp.minimum(k, n_tiles - 1)] == cur)
        nxt = lax.while_loop(same_group, lambda k: k + 1, i + 1)

        @pl.when(nxt < n_used)
        def _():
            fetch(te_ref[jnp.minimum(nxt, n_tiles - 1)], j, 1 - slot).start()

        @pl.when((nxt >= n_used) & (j + 1 < n_pass))
        def _():
            fetch(te_ref[0], j + 1, 1 - slot).start()

    return first, slot_ref[0]


def _cast_block(stage_ref, slot, wbf_ref):
    def cast(r, c):
        rows = pl.ds(pl.multiple_of(r * CAST_ROWS, CAST_ROWS), CAST_ROWS)
        wbf_ref[rows, :] = stage_ref[slot, rows, :].astype(BF16)
        return c
    lax.fori_loop(0, wbf_ref.shape[0] // CAST_ROWS, cast, 0)


def _gemm1_kernel(te_ref, nt_ref, tv_ref, x_ref, w_hbm, b_ref, o_ref, stage_ref, wbf_ref,
                  slot_ref, sem, *, nc, nblk):
    first, slot = _stream_expert_weights(te_ref, nt_ref, w_hbm, stage_ref, slot_ref, sem)

    def compute(rows, convert):
        words = _load_token_rows(x_ref, rows, nblk)
        x_lo, x_hi = (v.astype(BF16) for v in _unpack_bf16_pair(words))
        half = x_lo.shape[1]
        lane128 = lax.broadcasted_iota(jnp.int32, (rows, V7X_LANES), 1)
        even_idx = (2 * lane128) % V7X_LANES
        for n0 in range(0, wbf_ref.shape[1], nc):
            if convert:
                wbf_ref[:, n0:n0 + nc] = stage_ref[slot, :, n0:n0 + nc].astype(BF16)
            hb = (jnp.dot(x_lo, wbf_ref[:half, n0:n0 + nc], preferred_element_type=F32)
                  + jnp.dot(x_hi, wbf_ref[half:, n0:n0 + nc], preferred_element_type=F32)
                  + b_ref[0, :, n0:n0 + nc])
            nxt = pltpu.roll(hb, nc - 1, axis=1)
            glu = jnp.minimum(hb, SWIGLU_LIMIT)
            lin = jnp.clip(nxt, -SWIGLU_LIMIT, SWIGLU_LIMIT)
            act = glu * jax.nn.sigmoid(SWIGLU_ALPHA * glu) * (lin + 1.0)
            for c in range(nc // (2 * V7X_LANES)):
                lo_blk = act[:, (2 * c) * V7X_LANES:(2 * c + 1) * V7X_LANES]
                hi_blk = act[:, (2 * c + 1) * V7X_LANES:(2 * c + 2) * V7X_LANES]
                out = jnp.where(lane128 < V7X_LANES // 2,
                                jnp.take_along_axis(lo_blk, even_idx, axis=1),
                                jnp.take_along_axis(hi_blk, even_idx, axis=1))
                o0 = n0 // 2 + c * V7X_LANES
                o_ref[:rows, o0:o0 + V7X_LANES] = out.astype(o_ref.dtype)
        if rows < o_ref.shape[0]:
            o_ref[rows:, :] = jnp.zeros((o_ref.shape[0] - rows, o_ref.shape[1]), o_ref.dtype)

    _for_real_rows(pl.program_id(1), nt_ref, tv_ref, o_ref, o_ref.shape[0], compute,
                   first, lambda: _cast_block(stage_ref, slot, wbf_ref))


def _for_real_rows(i, nt_ref, tv_ref, o_ref, tm, compute, first, convert_all):
    used = i < nt_ref[0]
    real = tv_ref[i]
    sizes = [tm >> s for s in range(MOE_TILE_LEVELS) if (tm >> s) % MOE_MIN_ROWS == 0]

    @pl.when(first & (real <= sizes[1]))
    def _():
        convert_all()

    for n, rows in enumerate(sizes):
        fits = real <= rows
        if n + 1 < len(sizes):
            fits = fits & (real > sizes[n + 1])
        if n == 0:
            @pl.when(used & fits & first)
            def _():
                compute(rows, True)
            fits = fits & jnp.logical_not(first)

        @pl.when(used & fits)
        def _(rows=rows):
            compute(rows, False)

    @pl.when(jnp.logical_not(used))
    def _():
        o_ref[...] = jnp.zeros_like(o_ref)


def _gemm1(xin, w1, b1, tile_e, n_tiles_used, tile_rows, tm, nblk):
    n_rows = xin.shape[0] // nblk
    n_e, d, f2 = w1.shape
    tn = min(EXPERT_COL_BLOCK, f2)
    nc = min(GEMM1_COL_CHUNK, tn)
    assert f2 % tn == 0 and d % CAST_ROWS == 0
    n_tiles = n_rows // tm
    assert nc % (2 * V7X_LANES) == 0

    def used(i, nt):
        return jnp.minimum(i, nt[0] - 1)

    return pl.pallas_call(
        functools.partial(_gemm1_kernel, nc=nc, nblk=nblk),
        grid_spec=pltpu.PrefetchScalarGridSpec(
            num_scalar_prefetch=3, grid=(f2 // tn, n_tiles),
            in_specs=[pl.BlockSpec((tm * nblk, V7X_LANES),
                                   lambda j, i, te, nt, tv: (used(i, nt), 0)),
                      pl.BlockSpec(memory_space=pl.ANY),
                      pl.BlockSpec((1, 1, tn), lambda j, i, te, nt, tv: (te[used(i, nt)], 0, j))],
            out_specs=pl.BlockSpec((tm, tn // 2), lambda j, i, te, nt, tv: (i, j)),
            scratch_shapes=[pltpu.VMEM((2, d, tn), F32), pltpu.VMEM((d, tn), BF16),
                            pltpu.SMEM((1,), jnp.int32), pltpu.SemaphoreType.DMA((2,))]),
        out_shape=jax.ShapeDtypeStruct((n_rows, f2 // 2), BF16),
        compiler_params=_cparams(("arbitrary", "arbitrary")),
        name="moe_gemm1",
    )(tile_e, n_tiles_used, tile_rows, xin, w1, b1.reshape(n_e, 1, f2))


def _gemm2_kernel(te_ref, nt_ref, tv_ref, a_ref, w_hbm, b_ref, o_ref, stage_ref, wbf_ref, slot_ref,
                  sem, *, nc, nblk):
    first, slot = _stream_expert_weights(te_ref, nt_ref, w_hbm, stage_ref, slot_ref, sem)
    tm = a_ref.shape[0]

    def compute(rows, convert):
        a = a_ref[:rows, :]
        half = wbf_ref.shape[1] // 2

        def cols(n0):
            if convert:
                wbf_ref[:, n0:n0 + nc] = stage_ref[slot, :, n0:n0 + nc].astype(BF16)
            return (jnp.dot(a, wbf_ref[:, n0:n0 + nc], preferred_element_type=F32)
                    + b_ref[0, :, n0:n0 + nc])

        for n0 in range(0, half, nc):
            words = _pack_bf16_pair(cols(n0), cols(half + n0))
            for c in range(nc // V7X_LANES):
                blk = n0 // V7X_LANES + c
                o_ref[pl.ds(blk, rows, stride=nblk), :] = words[:, c * V7X_LANES:(c + 1) * V7X_LANES]
        if rows < tm:
            o_ref[rows * nblk:, :] = jnp.zeros(((tm - rows) * nblk, o_ref.shape[1]), o_ref.dtype)

    _for_real_rows(pl.program_id(1), nt_ref, tv_ref, o_ref, tm, compute,
                   first, lambda: _cast_block(stage_ref, slot, wbf_ref))


def _gemm2(act, w2, b2, tile_e, n_tiles_used, tile_rows, tm):
    n_rows, f = act.shape
    n_e, _, d = w2.shape
    tn = d
    nc = min(512, tn // 2)
    nblk = d // 2 // V7X_LANES
    assert f % CAST_ROWS == 0
    n_tiles = n_rows // tm

    def used(i, nt):
        return jnp.minimum(i, nt[0] - 1)

    return pl.pallas_call(
        functools.partial(_gemm2_kernel, nc=nc, nblk=nblk),
        grid_spec=pltpu.PrefetchScalarGridSpec(
            num_scalar_prefetch=3, grid=(d // tn, n_tiles),
            in_specs=[pl.BlockSpec((tm, f), lambda j, i, te, nt, tv: (used(i, nt), 0)),
                      pl.BlockSpec(memory_space=pl.ANY),
                      pl.BlockSpec((1, 1, tn), lambda j, i, te, nt, tv: (te[used(i, nt)], 0, j))],
            out_specs=pl.BlockSpec((tm * nblk, V7X_LANES), lambda j, i, te, nt, tv: (i, 0)),
            scratch_shapes=[pltpu.VMEM((2, f, tn), F32), pltpu.VMEM((f, tn), BF16),
                            pltpu.SMEM((1,), jnp.int32), pltpu.SemaphoreType.DMA((2,))]),
        out_shape=jax.ShapeDtypeStruct((n_rows * nblk, V7X_LANES), jnp.uint32),
        compiler_params=_cparams(("arbitrary", "arbitrary")),
        name="moe_gemm2",
    )(tile_e, n_tiles_used, tile_rows, act, w2, b2.reshape(n_e, 1, d))


COMBINE_ROW_GROUP = 64


def _combine_kernel(dest_ref, x1_ref, gate_ref, gt_ref, g_ref, b_ref, y_ref, o_ref, buf, sem,
                    *, alpha, tc, n_tok, nblk):
    i = pl.program_id(0)
    n_steps = pl.num_programs(0)

    def rows(tok):
        return pl.ds(pl.multiple_of(tok * nblk, nblk), nblk)

    def copy(step, slot, k, r):
        src = y_ref.at[rows(dest_ref[k * n_tok + step * tc + r])]
        return pltpu.make_async_copy(src, buf.at[slot, k, rows(r)], sem.at[slot])

    def issue(step, slot):
        def body(r, c):
            for k in range(TOP_K):
                copy(step, slot, k, r).start(priority=k % 2)
            return c
        lax.fori_loop(0, tc, body, 0, unroll=2)

    def drain(slot):
        pltpu.make_async_copy(buf.at[slot], buf.at[slot], sem.at[slot]).wait()

    @pl.when(i == 0)
    def _():
        issue(0, 0)

    def step(slot, prefetch):
        drain(slot)
        for rg in range(tc // COMBINE_ROW_GROUP):
            r0 = rg * COMBINE_ROW_GROUP
            if prefetch:
                for r in range(r0, r0 + COMBINE_ROW_GROUP):
                    for k in range(TOP_K):
                        copy(i + 1, 1 - slot, k, r).start(priority=k % 2)
            rows = slice(r0, r0 + COMBINE_ROW_GROUP)
            gates = gate_ref[rows, :]
            ff_lo = ff_hi = None
            for k in range(TOP_K):
                lo, hi = _unpack_bf16_pair(
                    _load_token_rows(buf.at[slot, k], COMBINE_ROW_GROUP, nblk, tok0=r0))
                gk = gates[:, k:k + 1]
                ff_lo = gk * lo if ff_lo is None else ff_lo + gk * lo
                ff_hi = gk * hi if ff_hi is None else ff_hi + gk * hi
            ff = jnp.concatenate([ff_lo, ff_hi], axis=1)
            y = alpha * x1_ref[rows, :] + (1.0 + gt_ref[0]) * ff
            o_ref[rows, :] = _layer_norm(y, g_ref[...], b_ref[...])

    for slot in range(2):
        for prefetch in (True, False):
            @pl.when((i % 2 == slot) & ((i + 1 < n_steps) == prefetch))
            def _(slot=slot, prefetch=prefetch):
                step(slot, prefetch)


def _combine(dest_flat, x1, gates_tk, gt, g, b, y, seq, alpha):
    t, d = x1.shape
    tc = _pick(seq, (256, 128))
    per_b = seq // tc
    nblk = d // 2 // V7X_LANES
    return pl.pallas_call(
        functools.partial(_combine_kernel, alpha=alpha, tc=tc, n_tok=t, nblk=nblk),
        grid_spec=pltpu.PrefetchScalarGridSpec(
            num_scalar_prefetch=1, grid=(t // tc,),
            in_specs=[pl.BlockSpec((tc, d), lambda i, ds: (i, 0)),
                      pl.BlockSpec((tc, TOP_K), lambda i, ds: (i, 0)),
                      pl.BlockSpec((1, 1, d), lambda i, ds: (i // per_b, 0, 0)),
                      pl.BlockSpec((1, d), lambda i, ds: (0, 0)),
                      pl.BlockSpec((1, d), lambda i, ds: (0, 0)),
                      pl.BlockSpec(memory_space=pl.ANY)],
            out_specs=pl.BlockSpec((tc, d), lambda i, ds: (i, 0)),
            scratch_shapes=[pltpu.VMEM((2, TOP_K, tc * nblk, V7X_LANES), jnp.uint32),
                            pltpu.SemaphoreType.DMA((2,))]),
        out_shape=jax.ShapeDtypeStruct((t, d), F32),
        compiler_params=_cparams(("arbitrary",)),
        name="moe_combine_ln2",
    )(dest_flat, x1, gates_tk, gt[:, None, :], g.reshape(1, d), b.reshape(1, d), y)


MOE_ROW_TILE = 512


def _moe_ffn(h2p, logits_t, x1, gt_f, ln_g, ln_b, w1, b1, w2, b2, seq, alpha):
    tm = MOE_ROW_TILE
    n_e, t = logits_t.shape
    n_rows = -(-(TOP_K * t) // tm) * tm + n_e * tm
    n_tiles = n_rows // tm
    gate_t, dest, tiles, pads = _routing(logits_t, tm, n_tiles)
    dest = dest.reshape(-1)
    tile_e = tiles[0, :n_tiles]
    n_used_rows = tiles[1, :1]
    tile_rows = tiles[2, :n_tiles]
    n_used_tiles = n_used_rows // tm
    nblk = x1.shape[1] // 2 // V7X_LANES
    xin = _dispatch(h2p, dest, pads[0, :, 0], pads[1, :, 0], n_used_rows, n_rows, tm, nblk)
    act = _gemm1(xin, w1, b1, tile_e, n_used_tiles, tile_rows, tm, nblk)
    y = _gemm2(act, w2, b2, tile_e, n_used_tiles, tile_rows, tm)
    return _combine(dest, x1, gate_t.T, gt_f, ln_g, ln_b, y, seq, alpha)


def kernel(x, c, positions, w_ada, b_ada, w_in, hgrn_lb, gnorm_w, w_o, ln1_g, ln1_b,
           router_w, router_b, w1, b1, w2, b2, ln2_g, ln2_b):
    bsz, seq, d = x.shape
    depth = w_ada.shape[0]
    t = bsz * seq
    att_w = d // 2
    hg_w = d - att_w
    alpha = (2.0 * depth) ** 0.25

    lb_all = jnp.cumsum(jax.nn.softmax(hgrn_lb.astype(F32), axis=0), axis=0)
    inv = ROPE_THETA ** (-(jnp.arange(0, ROT_DIM, 2, dtype=F32) / ROT_DIM))
    lane = np.arange(V7X_LANES)
    inv_lane = inv[(lane % ATT_HEAD_DIM) % (ROT_DIM // 2)].reshape(1, V7X_LANES)
    pos_b = jnp.broadcast_to(positions.astype(F32).reshape(t, 1), (t, V7X_LANES))

    x2d = x.reshape(t, d)
    for l in range(depth):
        mod = _adaln(c, w_ada[l], b_ada[l])
        sh_a, sc_a, gt_a, sh_f, sc_f, gt_f = jnp.split(mod, 6, axis=-1)

        proj = _in_proj(x2d, sc_a, sh_a, w_in[l], seq)
        proj3 = proj.reshape(bsz, seq, proj.shape[1])
        q_hp, k_hp, v_hp = _qk_rope(proj, pos_b, inv_lane, bsz, seq, att_w)
        att = _attention(q_hp, k_hp, v_hp)
        rec = _hgrn2(proj3, lb_all[l], gnorm_w[l], att_w, hg_w).reshape(t, hg_w)

        x1, h2p, logits_t = _out_proj(att, rec, x2d, w_o[l], gt_a, sc_f, sh_f, ln1_g[l], ln1_b[l],
                                      router_w[l], router_b[l], seq, alpha)

        x2d = _moe_ffn(h2p, logits_t, x1, gt_f, ln2_g[l], ln2_b[l], w1[l], b1[l], w2[l], b2[l],
                       seq, alpha)
    return x2d.reshape(bsz, seq, d)
```

```python
import functools

import numpy as np
import jax
import jax.numpy as jnp
from jax import lax
from jax.experimental import pallas as pl
from jax.experimental.pallas import tpu as pltpu

F32 = jnp.float32
BF16 = jnp.bfloat16

V7X_LANES = 128
V7X_VMEM_LIMIT = 56 * 1024 * 1024

ATT_HEAD_DIM = 64
DILATED_PAIRS = ((128, 1), (512, 4), (2048, 16))
ATT_BLOCK = 128
ROT_DIM = ATT_HEAD_DIM // 4
ROPE_THETA = 500000.0
HG_EXPAND = 128
HG_CHUNK = 64
HG_SUB = 16
TOP_K = 4
SWIGLU_ALPHA = 1.702
SWIGLU_LIMIT = 7.0
LN_EPS = 1e-5
RMS_EPS = 1e-6
NEG_INF = -1e30

LN_2 = float(np.log(2.0))
Q_SCALE = ATT_HEAD_DIM ** -0.5 / LN_2

NT_DIMS = (((1,), (1,)), ((), ()))


def _pick(n, candidates):
    for c in candidates:
        if n % c == 0:
            return c
    raise ValueError(f"no tile in {candidates} divides {n}")


def _cparams(sem):
    return pltpu.CompilerParams(dimension_semantics=sem, vmem_limit_bytes=V7X_VMEM_LIMIT)


def _adaln_kernel(c_ref, w_ref, b_ref, o_ref):
    c = c_ref[...]
    s = c * jax.nn.sigmoid(c)
    o_ref[...] = jnp.dot(s.astype(BF16), w_ref[...].astype(BF16),
                         preferred_element_type=F32) + b_ref[...]


def _adaln(c, w, b):
    bsz, d = c.shape
    n = w.shape[1]
    rows = 8
    cp = jnp.zeros((rows, d), F32).at[:bsz].set(c)
    tn = _pick(n, (1024, 512, 256, 128))
    out = pl.pallas_call(
        _adaln_kernel,
        grid=(n // tn,),
        in_specs=[pl.BlockSpec((rows, d), lambda j: (0, 0)),
                  pl.BlockSpec((d, tn), lambda j: (0, j)),
                  pl.BlockSpec((1, tn), lambda j: (0, j))],
        out_specs=pl.BlockSpec((rows, tn), lambda j: (0, j)),
        out_shape=jax.ShapeDtypeStruct((rows, n), F32),
        compiler_params=_cparams(("arbitrary",)),
        name="adaln",
    )(cp, w, b.reshape(1, n))
    return out[:bsz]


def _in_proj_kernel(x_ref, sc_ref, sh_ref, w_hbm, o_ref, wbf_ref, h_ref, stage_ref, sem):
    m, n = pl.program_id(0), pl.program_id(1)
    n_col = pl.num_programs(1)
    tn = o_ref.shape[1]

    def fetch(nn):
        src = w_hbm.at[:, pl.ds(pl.multiple_of(nn * tn, tn), tn)]
        return pltpu.make_async_copy(src, stage_ref, sem)

    @pl.when((m == 0) & (n == 0))
    def _():
        fetch(0).start()

    @pl.when(m == 0)
    def _():
        fetch(n).wait()
        wbf_ref[n] = stage_ref[...].astype(BF16)

        @pl.when(n + 1 < n_col)
        def _():
            fetch(n + 1).start()

    @pl.when(n == 0)
    def _():
        h_ref[...] = (x_ref[...] * (1.0 + sc_ref[0]) + sh_ref[0]).astype(BF16)

    o_ref[...] = jnp.dot(h_ref[...], wbf_ref[n], preferred_element_type=F32).astype(o_ref.dtype)


def _in_proj(x2d, sc, sh, w, seq):
    t, d = x2d.shape
    nc = w.shape[1]
    tm = _pick(seq, (512, 256, 128))
    tn = _pick(nc, (1024, 512, 256, 128))
    per_b = seq // tm
    vec = pl.BlockSpec((1, 1, d), lambda i, j: (i // per_b, 0, 0))
    return pl.pallas_call(
        _in_proj_kernel,
        grid=(t // tm, nc // tn),
        in_specs=[pl.BlockSpec((tm, d), lambda i, j: (i, 0)), vec, vec,
                  pl.BlockSpec(memory_space=pl.ANY)],
        out_specs=pl.BlockSpec((tm, tn), lambda i, j: (i, j)),
        out_shape=jax.ShapeDtypeStruct((t, nc), BF16),
        scratch_shapes=[pltpu.VMEM((nc // tn, d, tn), BF16), pltpu.VMEM((tm, d), BF16),
                        pltpu.VMEM((d, tn), F32), pltpu.SemaphoreType.DMA(())],
        compiler_params=_cparams(("arbitrary", "arbitrary")),
        name="in_proj",
    )(x2d, sc[:, None, :], sh[:, None, :], w)


def _rope_kernel(q_ref, k_ref, v_ref, pos_ref, inv_ref, qo_ref, ko_ref, vo_ref):
    tm = pos_ref.shape[0]
    lane = lax.broadcasted_iota(jnp.int32, (tm, V7X_LANES), 1)
    lh = lane % ATT_HEAD_DIM
    half = ROT_DIM // 2
    ang = pos_ref[...] * inv_ref[...]
    cs = jnp.where(lh < ROT_DIM, jnp.cos(ang), 1.0)
    sn = jnp.sin(ang)
    sn = jnp.where(lh < half, -sn, jnp.where(lh < ROT_DIM, sn, 0.0))

    def rope(t):
        swapped = jnp.where(lh < half,
                            pltpu.roll(t, V7X_LANES - half, axis=1),
                            pltpu.roll(t, half, axis=1))
        return t * cs + swapped * sn

    for h in range(qo_ref.shape[1]):
        lanes = slice(h * V7X_LANES, (h + 1) * V7X_LANES)
        qo_ref[0, h] = rope(q_ref[:, lanes].astype(F32)) * Q_SCALE
        ko_ref[0, h] = rope(k_ref[:, lanes].astype(F32))
        vo_ref[0, h] = v_ref[:, lanes].astype(F32)


def _qk_rope(proj, pos_b, inv_lane, bsz, seq, att_w):
    t = proj.shape[0]
    hp = att_w // V7X_LANES
    tm = _pick(seq, (256, 128))
    per_b = seq // tm
    out_spec = pl.BlockSpec((1, hp, tm, V7X_LANES), lambda i: (i // per_b, 0, i % per_b, 0))
    shp = jax.ShapeDtypeStruct((bsz, hp, seq, V7X_LANES), F32)
    return pl.pallas_call(
        _rope_kernel,
        grid=(t // tm,),
        in_specs=[pl.BlockSpec((tm, att_w), lambda i: (i, 0)),
                  pl.BlockSpec((tm, att_w), lambda i: (i, 1)),
                  pl.BlockSpec((tm, att_w), lambda i: (i, 2)),
                  pl.BlockSpec((tm, V7X_LANES), lambda i: (i, 0)),
                  pl.BlockSpec((1, V7X_LANES), lambda i: (0, 0))],
        out_specs=[out_spec, out_spec, out_spec],
        out_shape=[shp, shp, shp],
        compiler_params=_cparams(("arbitrary",)),
        name="qk_rope",
    )(proj, proj, proj, pos_b, inv_lane)


ATT_TASKS_PER_STEP = 8


def _attn_kernel(q_in, k_in, v_in, out_ref, qs, ks, vs, sm, no_lo, nl_lo, no_s, nl_s, nat, *, seq):
    qn = ATT_BLOCK
    n_task = seq // qn
    n_head = V7X_LANES // ATT_HEAD_DIM
    gc = min(ATT_TASKS_PER_STEP, n_task)
    (_, d_lo), (_, dm), (_, d_hi) = DILATED_PAIRS
    g_hi = d_hi // dm
    rows_m = seq // dm
    lane = lax.broadcasted_iota(jnp.int32, (qn, V7X_LANES), 1)
    qi = lax.broadcasted_iota(jnp.int32, (gc, qn, 2 * qn), 1)
    kj = lax.broadcasted_iota(jnp.int32, (gc, qn, 2 * qn), 2)
    dist = qi + qn - kj
    band = (dist >= 0) & (dist <= qn)
    zeros = jnp.zeros((qn, V7X_LANES), BF16)
    srcs = (q_in, k_in, v_in)

    for a, src in enumerate(srcs):
        for r in range(dm):
            sm[a, r] = src[0, 0, pl.ds(r, rows_m, stride=dm), :]

    def task_index(d, r, n):
        if d == d_lo:
            return None, slice(n * qn, (n + 1) * qn)
        if d == dm:
            return r, slice(n * qn, (n + 1) * qn)
        return r % dm, pl.ds(r // dm + n * qn * g_hi, qn, stride=g_hi)

    for bi, (window, d) in enumerate(DILATED_PAIRS):
        nb = seq // (d * qn)
        tasks = [(r, n) for r in range(d) for n in range(nb)]
        for g, (r, n) in enumerate(tasks):
            sel, rows = task_index(d, r, n)

            def operand(a, sel=sel, rows=rows):
                return srcs[a][0, 0, rows, :] if sel is None else sm[a, sel, rows, :]

            qv = operand(0)
            for h in range(n_head):
                qs[h, g] = jnp.where(lane // ATT_HEAD_DIM == h, qv, 0.0).astype(BF16)
            for a, dst in ((1, ks), (2, vs)):
                blk = operand(a).astype(BF16)
                dst[g, qn:, :] = blk
                if n + 1 < nb:
                    dst[g + 1, :qn, :] = blk
                if n == 0:
                    dst[g, :qn, :] = zeros

        for g0 in range(0, n_task, gc):
            gsl = slice(g0, g0 + gc)
            gidx = g0 + lax.broadcasted_iota(jnp.int32, (gc, qn, 2 * qn), 0)
            valid = band & ((gidx % nb != 0) | (kj >= qn))
            k = ks[gsl]
            v = vs[gsl]
            o = None
            for h in range(n_head):
                s = jnp.einsum("gqd,gkd->gqk", qs[h, gsl], k, preferred_element_type=F32)
                s = jnp.where(valid, s, NEG_INF)
                m = jnp.max(s, axis=-1, keepdims=True)
                p = jnp.exp2(s - m)
                den = jnp.sum(p, axis=-1, keepdims=True)
                oh = jnp.einsum("gqk,gkd->gqd", p.astype(BF16), v,
                                preferred_element_type=F32) / den
                lh = jnp.broadcast_to(m * LN_2 + jnp.log(den), oh.shape)
                if o is None:
                    o, lse = oh, lh
                else:
                    in_head = lane[None] // ATT_HEAD_DIM == h
                    o, lse = jnp.where(in_head, oh, o), jnp.where(in_head, lh, lse)
            for t in range(gc):
                sel, rows = task_index(d, *tasks[g0 + t])
                if sel is None:
                    no_lo[rows, :] = o[t]
                    nl_lo[rows, :] = lse[t]
                else:
                    no_s[bi - 1, sel, rows, :] = o[t]
                    nl_s[bi - 1, sel, rows, :] = lse[t]

    for r in range(dm):
        for c in range(rows_m // qn):
            rows = slice(c * qn, (c + 1) * qn)
            seq_rows = pl.ds(r + dm * c * qn, qn, stride=dm)
            ls = [nl_lo[seq_rows, :], nl_s[0, r, rows, :], nl_s[1, r, rows, :]]
            os_ = [no_lo[seq_rows, :], no_s[0, r, rows, :], no_s[1, r, rows, :]]
            mx = functools.reduce(jnp.maximum, ls)
            ws = [jnp.exp(l - mx) for l in ls]
            tot = functools.reduce(lambda x, y: x + y, ws)
            acc = functools.reduce(lambda x, y: x + y, [w * o for w, o in zip(ws, os_)])
            nat[seq_rows, :] = acc / tot
    out_ref[0, 0] = nat[...].astype(out_ref.dtype)


def _attention(q_hp, k_hp, v_hp):
    bsz, hp, seq, _ = q_hp.shape
    (_, d_lo), (_, dm), (_, d_hi) = DILATED_PAIRS
    assert d_lo == 1 and d_hi % dm == 0
    for window, d in DILATED_PAIRS:
        assert window // d == ATT_BLOCK and seq % (d * ATT_BLOCK) == 0
    n_task = seq // ATT_BLOCK
    assert n_task % min(ATT_TASKS_PER_STEP, n_task) == 0
    n_head = V7X_LANES // ATT_HEAD_DIM
    streams = (dm, seq // dm, V7X_LANES)
    spec = pl.BlockSpec((1, 1, seq, V7X_LANES), lambda b, h: (b, h, 0, 0))
    return pl.pallas_call(
        functools.partial(_attn_kernel, seq=seq),
        grid=(bsz, hp),
        in_specs=[spec, spec, spec],
        out_specs=spec,
        out_shape=jax.ShapeDtypeStruct((bsz, hp, seq, V7X_LANES), BF16),
        scratch_shapes=[pltpu.VMEM((n_head, n_task, ATT_BLOCK, V7X_LANES), BF16),
                        pltpu.VMEM((n_task, 2 * ATT_BLOCK, V7X_LANES), BF16),
                        pltpu.VMEM((n_task, 2 * ATT_BLOCK, V7X_LANES), BF16),
                        pltpu.VMEM((3,) + streams, F32),
                        pltpu.VMEM((seq, V7X_LANES), F32), pltpu.VMEM((seq, V7X_LANES), F32),
                        pltpu.VMEM((2,) + streams, F32), pltpu.VMEM((2,) + streams, F32),
                        pltpu.VMEM((seq, V7X_LANES), F32)],
        compiler_params=_cparams(("arbitrary", "arbitrary")),
        name="dilated_attn",
    )(q_hp, k_hp, v_hp)


def _hgrn_kernel(q_ref, f_ref, i_ref, g_ref, lb_ref, gw_ref, sums_ref, o_ref,
                 st_ref, hl_ref, kk_ref, cum_ref, *, heads, ts):
    @pl.when(pl.program_id(2) == 0)
    def _():
        st_ref[...] = jnp.zeros_like(st_ref)

    c_len = HG_CHUNK
    lanes = heads * HG_EXPAND
    lb = lb_ref[...]
    for c in range(ts // c_len):
        rows = slice(c * c_len, (c + 1) * c_len)
        f = lb + (1.0 - lb) * jax.nn.sigmoid(f_ref[0, rows, :].astype(F32))
        kk_ref[rows, :] = 1.0 - f
        hi, lo = _bf16_split(jnp.log(f))
        hl_ref[rows, :lanes] = hi
        hl_ref[rows, lanes:] = lo
    both = jnp.dot(sums_ref[...], hl_ref[...], preferred_element_type=F32)
    cum_ref[...] = both[:, :lanes] + both[:, lanes:]

    n_sub = c_len // HG_SUB

    def sub_block_anchor(b):
        mids = [b[i * HG_SUB + HG_SUB // 2:i * HG_SUB + HG_SUB // 2 + 1, :] for i in range(n_sub)]
        return jnp.concatenate([jnp.broadcast_to(m, (HG_SUB, m.shape[1])) for m in mids], axis=0)

    gi = lax.broadcasted_iota(jnp.int32, (heads * n_sub, HG_SUB, c_len), 0) % n_sub
    qi = lax.broadcasted_iota(jnp.int32, (heads * n_sub, HG_SUB, c_len), 1)
    si = lax.broadcasted_iota(jnp.int32, (heads * n_sub, HG_SUB, c_len), 2)
    causal = si <= gi * HG_SUB + qi
    key_row = lax.broadcasted_iota(jnp.int32, (c_len, HG_EXPAND), 0)

    for c in range(ts // c_len):
        rows = slice(c * c_len, (c + 1) * c_len)
        qts, kts, vbs = [], [], []
        for h in range(heads):
            lsl = slice(h * HG_EXPAND, (h + 1) * HG_EXPAND)
            b = cum_ref[rows, lsl]
            anchor = sub_block_anchor(b)
            kk = kk_ref[rows, lsl]
            qt = (q_ref[0, rows, lsl].astype(F32) * jnp.exp(b - anchor)).astype(BF16)
            qts.append(qt.reshape(n_sub, HG_SUB, HG_EXPAND))
            vb = i_ref[0, rows, lsl].astype(BF16)
            for i in range(n_sub):
                hi_r = (i + 1) * HG_SUB
                kt = kk * jnp.exp(anchor[i * HG_SUB:i * HG_SUB + 1, :] - b)
                kts.append(jnp.where(key_row < hi_r, kt, 0.0).astype(BF16))
                vbs.append(vb)
        a = jnp.einsum("gqk,gsk->gqs", jnp.concatenate(qts, axis=0), jnp.stack(kts),
                       preferred_element_type=F32)
        a = jnp.where(causal, a, 0.0).astype(BF16)
        o_intra = jnp.einsum("gqs,gsv->gqv", a, jnp.stack(vbs), preferred_element_type=F32)

        for h in range(heads):
            lsl = slice(h * HG_EXPAND, (h + 1) * HG_EXPAND)
            b = cum_ref[rows, lsl]
            b_last = jnp.broadcast_to(b[c_len - 1:c_len, :], b.shape)
            kk = kk_ref[rows, lsl]
            q = q_ref[0, rows, lsl].astype(F32)
            v = i_ref[0, rows, lsl].astype(F32)
            st = st_ref[h]
            o_inter = lax.dot_general((q * jnp.exp(b)).astype(BF16), st.astype(BF16), NT_DIMS,
                                      preferred_element_type=F32)
            o = o_inter + o_intra[h * n_sub:(h + 1) * n_sub].reshape(c_len, HG_EXPAND)
            kl = kk * jnp.exp(b_last - b)
            upd = jnp.dot(v.T.astype(BF16), kl.astype(BF16), preferred_element_type=F32)
            st_ref[h] = st * jnp.exp(b_last[0:1, :]) + upd
            o = o * lax.rsqrt(jnp.mean(o * o, axis=-1, keepdims=True) + RMS_EPS)
            g = g_ref[0, rows, lsl].astype(F32)
            o = o * gw_ref[:, lsl] * (g * jax.nn.sigmoid(g))
            o_ref[0, rows, lsl] = o.astype(o_ref.dtype)


def _hgrn_cumsum_matrix(ts):
    t = np.arange(ts)[:, None]
    s = np.arange(ts)[None, :]
    return (((t // HG_CHUNK) == (s // HG_CHUNK)) & (s <= t)).astype(np.float32)


def _hgrn2(proj3, lb, gw, att_w, hg_w):
    bsz, seq, _ = proj3.shape
    n_heads = hg_w // HG_EXPAND
    heads = _pick(n_heads, (8, 4, 2, 1))
    lanes = heads * HG_EXPAND
    ts = _pick(seq, (256, 128, 64))
    base = 3 * att_w
    assert base % lanes == 0 and hg_w % lanes == 0

    def seg(k):
        off = (base + k * hg_w) // lanes
        return pl.BlockSpec((1, ts, lanes), lambda b, h, s: (b, s, off + h))

    vec = pl.BlockSpec((1, lanes), lambda b, h, s: (0, h))
    sums = jnp.asarray(_hgrn_cumsum_matrix(ts), BF16)
    return pl.pallas_call(
        functools.partial(_hgrn_kernel, heads=heads, ts=ts),
        grid=(bsz, hg_w // lanes, seq // ts),
        in_specs=[seg(0), seg(1), seg(2), seg(3), vec, vec,
                  pl.BlockSpec((ts, ts), lambda b, h, s: (0, 0))],
        out_specs=pl.BlockSpec((1, ts, lanes), lambda b, h, s: (b, s, h)),
        out_shape=jax.ShapeDtypeStruct((bsz, seq, hg_w), BF16),
        scratch_shapes=[pltpu.VMEM((heads, HG_EXPAND, HG_EXPAND), F32),
                        pltpu.VMEM((ts, 2 * lanes), BF16),
                        pltpu.VMEM((ts, lanes), F32),
                        pltpu.VMEM((ts, lanes), F32)],
        compiler_params=_cparams(("arbitrary", "arbitrary", "arbitrary")),
        name="hgrn2",
    )(proj3, proj3, proj3, proj3, lb.reshape(1, hg_w), gw.reshape(1, hg_w), sums)


def _layer_norm(y, g, b):
    mu = jnp.mean(y, axis=-1, keepdims=True)
    yc = y - mu
    var = jnp.mean(yc * yc, axis=-1, keepdims=True)
    return yc * lax.rsqrt(var + LN_EPS) * g + b


def _bf16_split(x):
    hi = x.astype(BF16)
    lo = (x - hi.astype(F32)).astype(BF16)
    return hi, lo


def _pack_bf16_pair(lo, hi):
    lo_bits = lax.bitcast_convert_type(lo.astype(BF16).astype(F32), jnp.uint32)
    hi_bits = lax.bitcast_convert_type(hi.astype(BF16).astype(F32), jnp.uint32)
    return (lo_bits >> 16) | (hi_bits & jnp.uint32(0xFFFF0000))


def _store_token_rows(ref, tok0, words):
    n, width = words.shape
    nblk = width // V7X_LANES
    for c in range(nblk):
        ref[pl.ds(tok0 * nblk + c, n, stride=nblk), :] = words[:, c * V7X_LANES:(c + 1) * V7X_LANES]


def _load_token_rows(ref, n, nblk, tok0=0):
    return jnp.concatenate([ref[pl.ds(tok0 * nblk + c, n, stride=nblk), :] for c in range(nblk)],
                           axis=1)


def _unpack_bf16_pair(words):
    lo = lax.bitcast_convert_type(words << 16, F32)
    hi = lax.bitcast_convert_type(words & jnp.uint32(0xFFFF0000), F32)
    return lo, hi


def _load_weight_bf16(w_hbm, wbf_ref, stage_ref, sem):
    rows = stage_ref.shape[1]
    n_slab = w_hbm.shape[0] // rows

    def fetch(s, slot):
        return pltpu.make_async_copy(w_hbm.at[pl.ds(s * rows, rows)], stage_ref.at[slot],
                                     sem.at[slot])

    fetch(0, 0).start()
    for s in range(n_slab):
        if s + 1 < n_slab:
            fetch(s + 1, (s + 1) % 2).start()
        fetch(s, s % 2).wait()
        wbf_ref[s * rows:(s + 1) * rows, :] = stage_ref[s % 2].astype(BF16)


def _out_proj_kernel(att_ref, rec_ref, x_ref, wo_hbm, gt_ref, sc_ref, sh_ref, g_ref, b_ref,
                     rw_ref, rb_ref, x1_ref, h2_ref, lg_ref, wobf_ref, rwhi_ref, rwlo_ref,
                     stage_ref, hi_ref, lo_ref, mix_a, mix_b, sem, *, alpha, att_w):
    i = pl.program_id(0)

    @pl.when(i == 0)
    def _():
        _load_weight_bf16(wo_hbm, wobf_ref, stage_ref, sem)
        hi, lo = _bf16_split(rw_ref[...])
        rwhi_ref[...] = hi
        rwlo_ref[...] = lo
        mix_b[...] = jnp.zeros_like(mix_b)

    def step(mix_out, mix_in):
        att = jnp.concatenate([att_ref[0, h] for h in range(att_ref.shape[1])], axis=1)
        mix_out[...] = (jnp.dot(att, wobf_ref[:att_w, :], preferred_element_type=F32)
                        + jnp.dot(rec_ref[...], wobf_ref[att_w:, :], preferred_element_type=F32))
        half = x_ref.shape[1] // 2
        for r in range(x_ref.shape[0] // LN_ROW_GROUP):
            rows = slice(r * LN_ROW_GROUP, (r + 1) * LN_ROW_GROUP)
            y = alpha * x_ref[rows, :] + (1.0 + gt_ref[0]) * mix_in[rows, :]
            x1 = _layer_norm(y, g_ref[...], b_ref[...])
            x1_ref[rows, :] = x1
            h2 = x1 * (1.0 + sc_ref[0]) + sh_ref[0]
            hi, lo = _bf16_split(h2)
            hi_ref[rows, :] = hi
            lo_ref[rows, :] = lo
            _store_token_rows(h2_ref, r * LN_ROW_GROUP,
                              _pack_bf16_pair(h2[:, :half], h2[:, half:]))
        lg = (jnp.dot(hi_ref[...], rwhi_ref[...], preferred_element_type=F32)
              + jnp.dot(lo_ref[...], rwhi_ref[...], preferred_element_type=F32)
              + jnp.dot(hi_ref[...], rwlo_ref[...], preferred_element_type=F32))
        lg_ref[...] = lg.T[:lg_ref.shape[0], :] + rb_ref[...]

    @pl.when(i % 2 == 0)
    def _():
        step(mix_a, mix_b)

    @pl.when(i % 2 == 1)
    def _():
        step(mix_b, mix_a)


OUT_PROJ_STAGE_ROWS = 256
LN_ROW_GROUP = 16


def _out_proj(att, rec, x2d, wo, gt, sc, sh, g, b, rw, rb, seq, alpha):
    t, d = x2d.shape
    hp = att.shape[1]
    att_w = hp * V7X_LANES
    n_e = rw.shape[1]
    assert n_e <= V7X_LANES and d % OUT_PROJ_STAGE_ROWS == 0
    rw_pad = jnp.zeros((d, V7X_LANES), F32).at[:, :n_e].set(rw)
    nblk = d // 2 // V7X_LANES
    tm = _pick(seq, (512, 256, 128))
    per_b = seq // tm
    n_tile = t // tm
    mm = lambda i: jnp.minimum(i, n_tile - 1)
    ep = lambda i: jnp.maximum(i - 1, 0)
    vec3 = pl.BlockSpec((1, 1, d), lambda i: (ep(i) // per_b, 0, 0))
    full = lambda shape: pl.BlockSpec(shape, lambda i: (0,) * len(shape))
    return pl.pallas_call(
        functools.partial(_out_proj_kernel, alpha=alpha, att_w=att_w),
        grid=(n_tile + 1,),
        in_specs=[pl.BlockSpec((1, hp, tm, V7X_LANES),
                               lambda i: (mm(i) // per_b, 0, mm(i) % per_b, 0)),
                  pl.BlockSpec((tm, d - att_w), lambda i: (mm(i), 0)),
                  pl.BlockSpec((tm, d), lambda i: (ep(i), 0)),
                  pl.BlockSpec(memory_space=pl.ANY), vec3, vec3, vec3, full((1, d)), full((1, d)),
                  full((d, V7X_LANES)), full((n_e, 1))],
        out_specs=[pl.BlockSpec((tm, d), lambda i: (ep(i), 0)),
                   pl.BlockSpec((tm * nblk, V7X_LANES), lambda i: (ep(i), 0)),
                   pl.BlockSpec((n_e, tm), lambda i: (0, ep(i)))],
        out_shape=[jax.ShapeDtypeStruct((t, d), F32),
                   jax.ShapeDtypeStruct((t * nblk, V7X_LANES), jnp.uint32),
                   jax.ShapeDtypeStruct((n_e, t), F32)],
        scratch_shapes=[pltpu.VMEM((d, d), BF16), pltpu.VMEM((d, V7X_LANES), BF16),
                        pltpu.VMEM((d, V7X_LANES), BF16),
                        pltpu.VMEM((2, OUT_PROJ_STAGE_ROWS, d), F32),
                        pltpu.VMEM((tm, d), BF16), pltpu.VMEM((tm, d), BF16),
                        pltpu.VMEM((tm, d), F32), pltpu.VMEM((tm, d), F32),
                        pltpu.SemaphoreType.DMA((2,))],
        compiler_params=_cparams(("arbitrary",)),
        name="out_proj_ln1",
    )(att, rec, x2d, wo, gt[:, None, :], sc[:, None, :], sh[:, None, :],
      g.reshape(1, d), b.reshape(1, d), rw_pad, rb.reshape(n_e, 1))


def _routing_kernel(lg_ref, tri_ref, low_ref, gate_ref, dest_ref, tile_ref, pad_ref, sel_scr, rk_scr,
                    *, blk, tm):
    n_e, t = lg_ref.shape
    eidx = lax.broadcasted_iota(jnp.int32, (n_e, t), 0)
    cur = lg_ref[...]
    vals, idxs = [], []
    for _ in range(TOP_K):
        m = jnp.max(cur, axis=0, keepdims=True)
        ik = jnp.min(jnp.where(cur == m, eidx, n_e), axis=0, keepdims=True)
        cur = jnp.where(eidx == ik, -jnp.inf, cur)
        vals.append(m)
        idxs.append(ik)
    es = [jnp.exp(v - vals[0]) for v in vals]
    den = functools.reduce(lambda a, b: a + b, es)
    for k in range(TOP_K):
        gate_ref[k:k + 1, :] = es[k] / den
    sel = functools.reduce(lambda a, b: a | b, [eidx == ik for ik in idxs])
    sel_scr[...] = jnp.where(sel, 1.0, 0.0)

    tri = tri_ref[...]
    carry = jnp.zeros((n_e, 1), F32)
    for j in range(t // blk):
        sb = sel_scr[:, j * blk:(j + 1) * blk]
        pre = jnp.dot(sb.astype(BF16), tri, preferred_element_type=F32)
        rk_scr[:, j * blk:(j + 1) * blk] = pre + carry
        carry = carry + jnp.sum(sb, axis=1, keepdims=True)

    counts = jnp.broadcast_to(carry, (n_e, V7X_LANES))
    padded = jnp.floor((counts + (tm - 1)) * (1.0 / tm)) * tm
    pends = jnp.dot(low_ref[...], padded, precision=lax.Precision.HIGHEST,
                    preferred_element_type=F32)
    pstarts = pends - padded
    row0 = rk_scr[...] + pstarts[:, 0:1]
    for k in range(TOP_K):
        dest_ref[k:k + 1, :] = jnp.sum(jnp.where(eidx == idxs[k], row0, 0.0), axis=0,
                                       keepdims=True).astype(jnp.int32)
    starts = (lax.broadcasted_iota(jnp.int32, (n_e, tile_ref.shape[1]), 1) * tm).astype(F32)
    owner = jnp.sum(jnp.where(pends[:, 0:1] <= starts, 1.0, 0.0), axis=0, keepdims=True)
    tile_ref[0:1, :] = jnp.minimum(owner, n_e - 1.0).astype(jnp.int32)
    tile_ref[1:2, :] = jnp.broadcast_to(jnp.max(pends[:, 0:1], axis=0, keepdims=True),
                                        (1, tile_ref.shape[1])).astype(jnp.int32)
    e_tile = lax.broadcasted_iota(jnp.int32, starts.shape, 0).astype(F32)
    row_end = jnp.sum(jnp.where(e_tile == owner, (pstarts + counts)[:, 0:1], 0.0), axis=0,
                      keepdims=True)
    tile_ref[2:3, :] = jnp.clip(row_end - starts[0:1, :], 0.0, float(tm)).astype(jnp.int32)
    pad_ref[0] = (pstarts + counts).astype(jnp.int32)
    pad_ref[1] = (padded - counts).astype(jnp.int32)


def _routing(logits_t, tm, n_tiles):
    n_e, t = logits_t.shape
    assert tm & (tm - 1) == 0
    blk = _pick(t, (256, 128))
    tri = jnp.asarray(np.triu(np.ones((blk, blk), np.float32), 1), BF16)
    low = jnp.asarray(np.tril(np.ones((n_e, n_e), np.float32)))
    ntp = -(-n_tiles // V7X_LANES) * V7X_LANES
    full = lambda shape: pl.BlockSpec(shape, lambda: (0,) * len(shape))
    return pl.pallas_call(
        functools.partial(_routing_kernel, blk=blk, tm=tm),
        in_specs=[full((n_e, t)), full((blk, blk)), full((n_e, n_e))],
        out_specs=[full((TOP_K, t)), full((TOP_K, t)), full((3, ntp)), full((2, n_e, V7X_LANES))],
        out_shape=[jax.ShapeDtypeStruct((TOP_K, t), F32),
                   jax.ShapeDtypeStruct((TOP_K, t), jnp.int32),
                   jax.ShapeDtypeStruct((3, ntp), jnp.int32),
                   jax.ShapeDtypeStruct((2, n_e, V7X_LANES), jnp.int32)],
        scratch_shapes=[pltpu.VMEM((n_e, t), F32), pltpu.VMEM((n_e, t), F32)],
        compiler_params=pltpu.CompilerParams(vmem_limit_bytes=V7X_VMEM_LIMIT),
        name="routing",
    )(logits_t, tri, low)


def _dispatch_kernel(dest_ref, pad_off_ref, pad_n_ref, nrow_ref, h_ref, o_ref, zero_ref, sem, zsem,
                     *, n_tok, tq, tm, n_e, nblk):
    step = pl.program_id(0)
    base = step * tq

    def rows(tok, n=1):
        return pl.ds(pl.multiple_of(tok * nblk, nblk), n * nblk)

    def fill(off, n):
        return pltpu.make_async_copy(zero_ref.at[rows(0, n)], o_ref.at[rows(off, n)], zsem)

    pieces = [1 << s for s in range(tm.bit_length() - 2, -1, -1)]

    def pad_rows(wait):
        def per_expert(e, c):
            off = pad_off_ref[e]
            n = pad_n_ref[e]
            for p in pieces:
                hit = (n & p) != 0

                @pl.when(hit)
                def _(off=off, p=p):
                    cp = fill(off, p)
                    cp.wait() if wait else cp.start()

                off = off + jnp.where(hit, p, 0)
            return c
        lax.fori_loop(0, n_e, per_expert, 0)

        def per_tile(i, c):
            cp = fill(i * tm, tm)
            cp.wait() if wait else cp.start()
            return c
        lax.fori_loop(nrow_ref[0] // tm, o_ref.shape[0] // (tm * nblk), per_tile, 0)

    @pl.when(step == 0)
    def _():
        zero_ref[...] = jnp.zeros_like(zero_ref)
        pad_rows(False)

    def start(r, c):
        for k in range(TOP_K):
            dst = o_ref.at[rows(dest_ref[k * n_tok + base + r])]
            pltpu.make_async_copy(h_ref.at[rows(r)], dst, sem).start(priority=k % 2)
        return c

    lax.fori_loop(0, tq, start, 0, unroll=4)
    for _ in range(TOP_K):
        pltpu.make_async_copy(h_ref, h_ref, sem).wait()

    @pl.when(step == 0)
    def _():
        pad_rows(True)


def _dispatch(h2p, dest_flat, pad_off, pad_n, n_used_rows, n_rows, tm, nblk):
    t = h2p.shape[0] // nblk
    n_e = pad_off.shape[0]
    tq = _pick(t, (256, 128))
    return pl.pallas_call(
        functools.partial(_dispatch_kernel, n_tok=t, tq=tq, tm=tm, n_e=n_e, nblk=nblk),
        grid_spec=pltpu.PrefetchScalarGridSpec(
            num_scalar_prefetch=4, grid=(t // tq,),
            in_specs=[pl.BlockSpec((tq * nblk, V7X_LANES), lambda i, *_: (i, 0))],
            out_specs=pl.BlockSpec(memory_space=pl.ANY),
            scratch_shapes=[pltpu.VMEM((tm * nblk, V7X_LANES), h2p.dtype),
                            pltpu.SemaphoreType.DMA(()), pltpu.SemaphoreType.DMA(())]),
        out_shape=jax.ShapeDtypeStruct((n_rows * nblk, V7X_LANES), h2p.dtype),
        compiler_params=pltpu.CompilerParams(dimension_semantics=("arbitrary",),
                                             has_side_effects=True,
                                             vmem_limit_bytes=V7X_VMEM_LIMIT),
        name="moe_dispatch",
    )(dest_flat, pad_off, pad_n, n_used_rows, h2p)


EXPERT_COL_BLOCK = 2048
GEMM1_COL_CHUNK = 256
MOE_TILE_LEVELS = 3
MOE_MIN_ROWS = 16
CAST_ROWS = 256


def _stream_expert_weights(te_ref, nt_ref, w_hbm, stage_ref, slot_ref, sem):
    j, i = pl.program_id(0), pl.program_id(1)
    n_pass, n_tiles = pl.num_programs(0), pl.num_programs(1)
    tn = stage_ref.shape[2]
    n_used = nt_ref[0]
    cur = te_ref[i]
    first = (i < n_used) & ((i == 0) | (cur != te_ref[jnp.maximum(i - 1, 0)]))

    def fetch(e, jj, slot):
        src = w_hbm.at[e, :, pl.ds(pl.multiple_of(jj * tn, tn), tn)]
        return pltpu.make_async_copy(src, stage_ref.at[slot], sem.at[slot])

    @pl.when((i == 0) & (j == 0))
    def _():
        slot_ref[0] = 1
        fetch(te_ref[0], 0, 0).start()

    @pl.when(first)
    def _():
        slot = 1 - slot_ref[0]
        slot_ref[0] = slot
        fetch(cur, j, slot).wait()

        def same_group(k):
            return (k < n_used) & (te_ref[jnp.minimum(k, n_tiles - 1)] == cur)
        nxt = lax.while_loop(same_group, lambda k: k + 1, i + 1)

        @pl.when(nxt < n_used)
        def _():
            fetch(te_ref[jnp.minimum(nxt, n_tiles - 1)], j, 1 - slot).start()

        @pl.when((nxt >= n_used) & (j + 1 < n_pass))
        def _():
            fetch(te_ref[0], j + 1, 1 - slot).start()

    return first, slot_ref[0]


def _cast_block(stage_ref, slot, wbf_ref):
    def cast(r, c):
        rows = pl.ds(pl.multiple_of(r * CAST_ROWS, CAST_ROWS), CAST_ROWS)
        wbf_ref[rows, :] = stage_ref[slot, rows, :].astype(BF16)
        return c
    lax.fori_loop(0, wbf_ref.shape[0] // CAST_ROWS, cast, 0)


def _gemm1_kernel(te_ref, nt_ref, tv_ref, x_ref, w_hbm, b_ref, o_ref, stage_ref, wbf_ref,
                  slot_ref, sem, *, nc, nblk):
    first, slot = _stream_expert_weights(te_ref, nt_ref, w_hbm, stage_ref, slot_ref, sem)

    def compute(rows, convert):
        words = _load_token_rows(x_ref, rows, nblk)
        x_lo, x_hi = (v.astype(BF16) for v in _unpack_bf16_pair(words))
        half = x_lo.shape[1]
        lane128 = lax.broadcasted_iota(jnp.int32, (rows, V7X_LANES), 1)
        even_idx = (2 * lane128) % V7X_LANES
        for n0 in range(0, wbf_ref.shape[1], nc):
            if convert:
                wbf_ref[:, n0:n0 + nc] = stage_ref[slot, :, n0:n0 + nc].astype(BF16)
            hb = (jnp.dot(x_lo, wbf_ref[:half, n0:n0 + nc], preferred_element_type=F32)
                  + jnp.dot(x_hi, wbf_ref[half:, n0:n0 + nc], preferred_element_type=F32)
                  + b_ref[0, :, n0:n0 + nc])
            nxt = pltpu.roll(hb, nc - 1, axis=1)
            glu = jnp.minimum(hb, SWIGLU_LIMIT)
            lin = jnp.clip(nxt, -SWIGLU_LIMIT, SWIGLU_LIMIT)
            act = glu * jax.nn.sigmoid(SWIGLU_ALPHA * glu) * (lin + 1.0)
            for c in range(nc // (2 * V7X_LANES)):
                lo_blk = act[:, (2 * c) * V7X_LANES:(2 * c + 1) * V7X_LANES]
                hi_blk = act[:, (2 * c + 1) * V7X_LANES:(2 * c + 2) * V7X_LANES]
                out = jnp.where(lane128 < V7X_LANES // 2,
                                jnp.take_along_axis(lo_blk, even_idx, axis=1),
                                jnp.take_along_axis(hi_blk, even_idx, axis=1))
                o0 = n0 // 2 + c * V7X_LANES
                o_ref[:rows, o0:o0 + V7X_LANES] = out.astype(o_ref.dtype)
        if rows < o_ref.shape[0]:
            o_ref[rows:, :] = jnp.zeros((o_ref.shape[0] - rows, o_ref.shape[1]), o_ref.dtype)

    _for_real_rows(pl.program_id(1), nt_ref, tv_ref, o_ref, o_ref.shape[0], compute,
                   first, lambda: _cast_block(stage_ref, slot, wbf_ref))


def _for_real_rows(i, nt_ref, tv_ref, o_ref, tm, compute, first, convert_all):
    used = i < nt_ref[0]
    real = tv_ref[i]
    sizes = [tm >> s for s in range(MOE_TILE_LEVELS) if (tm >> s) % MOE_MIN_ROWS == 0]

    @pl.when(first & (real <= sizes[1]))
    def _():
        convert_all()

    for n, rows in enumerate(sizes):
        fits = real <= rows
        if n + 1 < len(sizes):
            fits = fits & (real > sizes[n + 1])
        if n == 0:
            @pl.when(used & fits & first)
            def _():
                compute(rows, True)
            fits = fits & jnp.logical_not(first)

        @pl.when(used & fits)
        def _(rows=rows):
            compute(rows, False)

    @pl.when(jnp.logical_not(used))
    def _():
        o_ref[...] = jnp.zeros_like(o_ref)


def _gemm1(xin, w1, b1, tile_e, n_tiles_used, tile_rows, tm, nblk):
    n_rows = xin.shape[0] // nblk
    n_e, d, f2 = w1.shape
    tn = min(EXPERT_COL_BLOCK, f2)
    nc = min(GEMM1_COL_CHUNK, tn)
    assert f2 % tn == 0 and d % CAST_ROWS == 0
    n_tiles = n_rows // tm
    assert nc % (2 * V7X_LANES) == 0

    def used(i, nt):
        return jnp.minimum(i, nt[0] - 1)

    return pl.pallas_call(
        functools.partial(_gemm1_kernel, nc=nc, nblk=nblk),
        grid_spec=pltpu.PrefetchScalarGridSpec(
            num_scalar_prefetch=3, grid=(f2 // tn, n_tiles),
            in_specs=[pl.BlockSpec((tm * nblk, V7X_LANES),
                                   lambda j, i, te, nt, tv: (used(i, nt), 0)),
                      pl.BlockSpec(memory_space=pl.ANY),
                      pl.BlockSpec((1, 1, tn), lambda j, i, te, nt, tv: (te[used(i, nt)], 0, j))],
            out_specs=pl.BlockSpec((tm, tn // 2), lambda j, i, te, nt, tv: (i, j)),
            scratch_shapes=[pltpu.VMEM((2, d, tn), F32), pltpu.VMEM((d, tn), BF16),
                            pltpu.SMEM((1,), jnp.int32), pltpu.SemaphoreType.DMA((2,))]),
        out_shape=jax.ShapeDtypeStruct((n_rows, f2 // 2), BF16),
        compiler_params=_cparams(("arbitrary", "arbitrary")),
        name="moe_gemm1",
    )(tile_e, n_tiles_used, tile_rows, xin, w1, b1.reshape(n_e, 1, f2))


def _gemm2_kernel(te_ref, nt_ref, tv_ref, a_ref, w_hbm, b_ref, o_ref, stage_ref, wbf_ref, slot_ref,
                  sem, *, nc, nblk):
    first, slot = _stream_expert_weights(te_ref, nt_ref, w_hbm, stage_ref, slot_ref, sem)
    tm = a_ref.shape[0]

    def compute(rows, convert):
        a = a_ref[:rows, :]
        half = wbf_ref.shape[1] // 2

        def cols(n0):
            if convert:
                wbf_ref[:, n0:n0 + nc] = stage_ref[slot, :, n0:n0 + nc].astype(BF16)
            return (jnp.dot(a, wbf_ref[:, n0:n0 + nc], preferred_element_type=F32)
                    + b_ref[0, :, n0:n0 + nc])

        for n0 in range(0, half, nc):
            words = _pack_bf16_pair(cols(n0), cols(half + n0))
            for c in range(nc // V7X_LANES):
                blk = n0 // V7X_LANES + c
                o_ref[pl.ds(blk, rows, stride=nblk), :] = words[:, c * V7X_LANES:(c + 1) * V7X_LANES]
        if rows < tm:
            o_ref[rows * nblk:, :] = jnp.zeros(((tm - rows) * nblk, o_ref.shape[1]), o_ref.dtype)

    _for_real_rows(pl.program_id(1), nt_ref, tv_ref, o_ref, tm, compute,
                   first, lambda: _cast_block(stage_ref, slot, wbf_ref))


def _gemm2(act, w2, b2, tile_e, n_tiles_used, tile_rows, tm):
    n_rows, f = act.shape
    n_e, _, d = w2.shape
    tn = d
    nc = min(512, tn // 2)
    nblk = d // 2 // V7X_LANES
    assert f % CAST_ROWS == 0
    n_tiles = n_rows // tm

    def used(i, nt):
        return jnp.minimum(i, nt[0] - 1)

    return pl.pallas_call(
        functools.partial(_gemm2_kernel, nc=nc, nblk=nblk),
        grid_spec=pltpu.PrefetchScalarGridSpec(
            num_scalar_prefetch=3, grid=(d // tn, n_tiles),
            in_specs=[pl.BlockSpec((tm, f), lambda j, i, te, nt, tv: (used(i, nt), 0)),
                      pl.BlockSpec(memory_space=pl.ANY),
                      pl.BlockSpec((1, 1, tn), lambda j, i, te, nt, tv: (te[used(i, nt)], 0, j))],
            out_specs=pl.BlockSpec((tm * nblk, V7X_LANES), lambda j, i, te, nt, tv: (i, 0)),
            scratch_shapes=[pltpu.VMEM((2, f, tn), F32), pltpu.VMEM((f, tn), BF16),
                            pltpu.SMEM((1,), jnp.int32), pltpu.SemaphoreType.DMA((2,))]),
        out_shape=jax.ShapeDtypeStruct((n_rows * nblk, V7X_LANES), jnp.uint32),
        compiler_params=_cparams(("arbitrary", "arbitrary")),
        name="moe_gemm2",
    )(tile_e, n_tiles_used, tile_rows, act, w2, b2.reshape(n_e, 1, d))


COMBINE_ROW_GROUP = 64


def _combine_kernel(dest_ref, x1_ref, gate_ref, gt_ref, g_ref, b_ref, y_ref, o_ref, buf, sem,
                    *, alpha, tc, n_tok, nblk):
    i = pl.program_id(0)
    n_steps = pl.num_programs(0)

    def rows(tok):
        return pl.ds(pl.multiple_of(tok * nblk, nblk), nblk)

    def copy(step, slot, k, r):
        src = y_ref.at[rows(dest_ref[k * n_tok + step * tc + r])]
        return pltpu.make_async_copy(src, buf.at[slot, k, rows(r)], sem.at[slot])

    def issue(step, slot):
        def body(r, c):
            for k in range(TOP_K):
                copy(step, slot, k, r).start(priority=k % 2)
            return c
        lax.fori_loop(0, tc, body, 0, unroll=2)

    def drain(slot):
        pltpu.make_async_copy(buf.at[slot], buf.at[slot], sem.at[slot]).wait()

    @pl.when(i == 0)
    def _():
        issue(0, 0)

    def step(slot, prefetch):
        drain(slot)
        for rg in range(tc // COMBINE_ROW_GROUP):
            r0 = rg * COMBINE_ROW_GROUP
            if prefetch:
                for r in range(r0, r0 + COMBINE_ROW_GROUP):
                    for k in range(TOP_K):
                        copy(i + 1, 1 - slot, k, r).start(priority=k % 2)
            rows = slice(r0, r0 + COMBINE_ROW_GROUP)
            gates = gate_ref[rows, :]
            ff_lo = ff_hi = None
            for k in range(TOP_K):
                lo, hi = _unpack_bf16_pair(
                    _load_token_rows(buf.at[slot, k], COMBINE_ROW_GROUP, nblk, tok0=r0))
                gk = gates[:, k:k + 1]
                ff_lo = gk * lo if ff_lo is None else ff_lo + gk * lo
                ff_hi = gk * hi if ff_hi is None else ff_hi + gk * hi
            ff = jnp.concatenate([ff_lo, ff_hi], axis=1)
            y = alpha * x1_ref[rows, :] + (1.0 + gt_ref[0]) * ff
            o_ref[rows, :] = _layer_norm(y, g_ref[...], b_ref[...])

    for slot in range(2):
        for prefetch in (True, False):
            @pl.when((i % 2 == slot) & ((i + 1 < n_steps) == prefetch))
            def _(slot=slot, prefetch=prefetch):
                step(slot, prefetch)


def _combine(dest_flat, x1, gates_tk, gt, g, b, y, seq, alpha):
    t, d = x1.shape
    tc = _pick(seq, (256, 128))
    per_b = seq // tc
    nblk = d // 2 // V7X_LANES
    return pl.pallas_call(
        functools.partial(_combine_kernel, alpha=alpha, tc=tc, n_tok=t, nblk=nblk),
        grid_spec=pltpu.PrefetchScalarGridSpec(
            num_scalar_prefetch=1, grid=(t // tc,),
            in_specs=[pl.BlockSpec((tc, d), lambda i, ds: (i, 0)),
                      pl.BlockSpec((tc, TOP_K), lambda i, ds: (i, 0)),
                      pl.BlockSpec((1, 1, d), lambda i, ds: (i // per_b, 0, 0)),
                      pl.BlockSpec((1, d), lambda i, ds: (0, 0)),
                      pl.BlockSpec((1, d), lambda i, ds: (0, 0)),
                      pl.BlockSpec(memory_space=pl.ANY)],
            out_specs=pl.BlockSpec((tc, d), lambda i, ds: (i, 0)),
            scratch_shapes=[pltpu.VMEM((2, TOP_K, tc * nblk, V7X_LANES), jnp.uint32),
                            pltpu.SemaphoreType.DMA((2,))]),
        out_shape=jax.ShapeDtypeStruct((t, d), F32),
        compiler_params=_cparams(("arbitrary",)),
        name="moe_combine_ln2",
    )(dest_flat, x1, gates_tk, gt[:, None, :], g.reshape(1, d), b.reshape(1, d), y)


MOE_ROW_TILE = 512


def _moe_ffn(h2p, logits_t, x1, gt_f, ln_g, ln_b, w1, b1, w2, b2, seq, alpha):
    tm = MOE_ROW_TILE
    n_e, t = logits_t.shape
    n_rows = -(-(TOP_K * t) // tm) * tm + n_e * tm
    n_tiles = n_rows // tm
    gate_t, dest, tiles, pads = _routing(logits_t, tm, n_tiles)
    dest = dest.reshape(-1)
    tile_e = tiles[0, :n_tiles]
    n_used_rows = tiles[1, :1]
    tile_rows = tiles[2, :n_tiles]
    n_used_tiles = n_used_rows // tm
    nblk = x1.shape[1] // 2 // V7X_LANES
    xin = _dispatch(h2p, dest, pads[0, :, 0], pads[1, :, 0], n_used_rows, n_rows, tm, nblk)
    act = _gemm1(xin, w1, b1, tile_e, n_used_tiles, tile_rows, tm, nblk)
    y = _gemm2(act, w2, b2, tile_e, n_used_tiles, tile_rows, tm)
    return _combine(dest, x1, gate_t.T, gt_f, ln_g, ln_b, y, seq, alpha)


def kernel(x, c, positions, w_ada, b_ada, w_in, hgrn_lb, gnorm_w, w_o, ln1_g, ln1_b,
           router_w, router_b, w1, b1, w2, b2, ln2_g, ln2_b):
    bsz, seq, d = x.shape
    depth = w_ada.shape[0]
    t = bsz * seq
    att_w = d // 2
    hg_w = d - att_w
    alpha = (2.0 * depth) ** 0.25

    lb_all = jnp.cumsum(jax.nn.softmax(hgrn_lb.astype(F32), axis=0), axis=0)
    inv = ROPE_THETA ** (-(jnp.arange(0, ROT_DIM, 2, dtype=F32) / ROT_DIM))
    lane = np.arange(V7X_LANES)
    inv_lane = inv[(lane % ATT_HEAD_DIM) % (ROT_DIM // 2)].reshape(1, V7X_LANES)
    pos_b = jnp.broadcast_to(positions.astype(F32).reshape(t, 1), (t, V7X_LANES))

    x2d = x.reshape(t, d)
    for l in range(depth):
        mod = _adaln(c, w_ada[l], b_ada[l])
        sh_a, sc_a, gt_a, sh_f, sc_f, gt_f = jnp.split(mod, 6, axis=-1)

        proj = _in_proj(x2d, sc_a, sh_a, w_in[l], seq)
        proj3 = proj.reshape(bsz, seq, proj.shape[1])
        q_hp, k_hp, v_hp = _qk_rope(proj, pos_b, inv_lane, bsz, seq, att_w)
        att = _attention(q_hp, k_hp, v_hp)
        rec = _hgrn2(proj3, lb_all[l], gnorm_w[l], att_w, hg_w).reshape(t, hg_w)

        x1, h2p, logits_t = _out_proj(att, rec, x2d, w_o[l], gt_a, sc_f, sh_f, ln1_g[l], ln1_b[l],
                                      router_w[l], router_b[l], seq, alpha)

        x2d = _moe_ffn(h2p, logits_t, x1, gt_f, ln2_g[l], ln2_b[l], w1[l], b1[l], w2[l], b2[l],
                       seq, alpha)
    return x2d.reshape(bsz, seq, d)
```

```python
import functools

import numpy as np
import jax
import jax.numpy as jnp
from jax import lax
from jax.experimental import pallas as pl
from jax.experimental.pallas import tpu as pltpu

F32 = jnp.float32
BF16 = jnp.bfloat16

V7X_LANES = 128
V7X_VMEM_LIMIT = 56 * 1024 * 1024

ATT_HEAD_DIM = 64
DILATED_PAIRS = ((128, 1), (512, 4), (2048, 16))
ATT_BLOCK = 128
ROT_DIM = ATT_HEAD_DIM // 4
ROPE_THETA = 500000.0
HG_EXPAND = 128
HG_CHUNK = 64
HG_SUB = 16
TOP_K = 4
SWIGLU_ALPHA = 1.702
SWIGLU_LIMIT = 7.0
LN_EPS = 1e-5
RMS_EPS = 1e-6
NEG_INF = -1e30

LN_2 = float(np.log(2.0))
Q_SCALE = ATT_HEAD_DIM ** -0.5 / LN_2

NT_DIMS = (((1,), (1,)), ((), ()))


def _pick(n, candidates):
    for c in candidates:
        if n % c == 0:
            return c
    raise ValueError(f"no tile in {candidates} divides {n}")


def _cparams(sem):
    return pltpu.CompilerParams(dimension_semantics=sem, vmem_limit_bytes=V7X_VMEM_LIMIT)


def _adaln_kernel(c_ref, w_ref, b_ref, o_ref):
    c = c_ref[...]
    s = c * jax.nn.sigmoid(c)
    o_ref[...] = jnp.dot(s.astype(BF16), w_ref[...].astype(BF16),
                         preferred_element_type=F32) + b_ref[...]


def _adaln(c, w, b):
    bsz, d = c.shape
    n = w.shape[1]
    rows = 8
    cp = jnp.zeros((rows, d), F32).at[:bsz].set(c)
    tn = _pick(n, (1024, 512, 256, 128))
    out = pl.pallas_call(
        _adaln_kernel,
        grid=(n // tn,),
        in_specs=[pl.BlockSpec((rows, d), lambda j: (0, 0)),
                  pl.BlockSpec((d, tn), lambda j: (0, j)),
                  pl.BlockSpec((1, tn), lambda j: (0, j))],
        out_specs=pl.BlockSpec((rows, tn), lambda j: (0, j)),
        out_shape=jax.ShapeDtypeStruct((rows, n), F32),
        compiler_params=_cparams(("arbitrary",)),
        name="adaln",
    )(cp, w, b.reshape(1, n))
    return out[:bsz]


def _in_proj_kernel(x_ref, sc_ref, sh_ref, w_hbm, o_ref, wbf_ref, h_ref, stage_ref, sem):
    m, n = pl.program_id(0), pl.program_id(1)
    n_col = pl.num_programs(1)
    tn = o_ref.shape[1]

    def fetch(nn):
        src = w_hbm.at[:, pl.ds(pl.multiple_of(nn * tn, tn), tn)]
        return pltpu.make_async_copy(src, stage_ref, sem)

    @pl.when((m == 0) & (n == 0))
    def _():
        fetch(0).start()

    @pl.when(m == 0)
    def _():
        fetch(n).wait()
        wbf_ref[n] = stage_ref[...].astype(BF16)

        @pl.when(n + 1 < n_col)
        def _():
            fetch(n + 1).start()

    @pl.when(n == 0)
    def _():
        h_ref[...] = (x_ref[...] * (1.0 + sc_ref[0]) + sh_ref[0]).astype(BF16)

    o_ref[...] = jnp.dot(h_ref[...], wbf_ref[n], preferred_element_type=F32).astype(o_ref.dtype)


def _in_proj(x2d, sc, sh, w, seq):
    t, d = x2d.shape
    nc = w.shape[1]
    tm = _pick(seq, (512, 256, 128))
    tn = _pick(nc, (1024, 512, 256, 128))
    per_b = seq // tm
    vec = pl.BlockSpec((1, 1, d), lambda i, j: (i // per_b, 0, 0))
    return pl.pallas_call(
        _in_proj_kernel,
        grid=(t // tm, nc // tn),
        in_specs=[pl.BlockSpec((tm, d), lambda i, j: (i, 0)), vec, vec,
                  pl.BlockSpec(memory_space=pl.ANY)],
        out_specs=pl.BlockSpec((tm, tn), lambda i, j: (i, j)),
        out_shape=jax.ShapeDtypeStruct((t, nc), BF16),
        scratch_shapes=[pltpu.VMEM((nc // tn, d, tn), BF16), pltpu.VMEM((tm, d), BF16),
                        pltpu.VMEM((d, tn), F32), pltpu.SemaphoreType.DMA(())],
        compiler_params=_cparams(("arbitrary", "arbitrary")),
        name="in_proj",
    )(x2d, sc[:, None, :], sh[:, None, :], w)


def _rope_kernel(q_ref, k_ref, v_ref, pos_ref, inv_ref, qo_ref, ko_ref, vo_ref):
    tm = pos_ref.shape[0]
    lane = lax.broadcasted_iota(jnp.int32, (tm, V7X_LANES), 1)
    lh = lane % ATT_HEAD_DIM
    half = ROT_DIM // 2
    ang = pos_ref[...] * inv_ref[...]
    cs = jnp.where(lh < ROT_DIM, jnp.cos(ang), 1.0)
    sn = jnp.sin(ang)
    sn = jnp.where(lh < half, -sn, jnp.where(lh < ROT_DIM, sn, 0.0))

    def rope(t):
        swapped = jnp.where(lh < half,
                            pltpu.roll(t, V7X_LANES - half, axis=1),
                            pltpu.roll(t, half, axis=1))
        return t * cs + swapped * sn

    for h in range(qo_ref.shape[1]):
        lanes = slice(h * V7X_LANES, (h + 1) * V7X_LANES)
        qo_ref[0, h] = rope(q_ref[:, lanes].astype(F32)) * Q_SCALE
        ko_ref[0, h] = rope(k_ref[:, lanes].astype(F32))
        vo_ref[0, h] = v_ref[:, lanes].astype(F32)


def _qk_rope(proj, pos_b, inv_lane, bsz, seq, att_w):
    t = proj.shape[0]
    hp = att_w // V7X_LANES
    tm = _pick(seq, (512, 256, 128))
    per_b = seq // tm
    out_spec = pl.BlockSpec((1, hp, tm, V7X_LANES), lambda i: (i // per_b, 0, i % per_b, 0))
    shp = jax.ShapeDtypeStruct((bsz, hp, seq, V7X_LANES), F32)
    return pl.pallas_call(
        _rope_kernel,
        grid=(t // tm,),
        in_specs=[pl.BlockSpec((tm, att_w), lambda i: (i, 0)),
                  pl.BlockSpec((tm, att_w), lambda i: (i, 1)),
                  pl.BlockSpec((tm, att_w), lambda i: (i, 2)),
                  pl.BlockSpec((tm, V7X_LANES), lambda i: (i, 0)),
                  pl.BlockSpec((1, V7X_LANES), lambda i: (0, 0))],
        out_specs=[out_spec, out_spec, out_spec],
        out_shape=[shp, shp, shp],
        compiler_params=_cparams(("arbitrary",)),
        name="qk_rope",
    )(proj, proj, proj, pos_b, inv_lane)


ATT_TASKS_PER_STEP = 8


def _attn_kernel(q_in, k_in, v_in, out_ref, qs, ks, vs, sm, no_lo, nl_lo, no_s, nl_s, nat, *, seq):
    qn = ATT_BLOCK
    n_task = seq // qn
    n_head = V7X_LANES // ATT_HEAD_DIM
    gc = min(ATT_TASKS_PER_STEP, n_task)
    (_, d_lo), (_, dm), (_, d_hi) = DILATED_PAIRS
    g_hi = d_hi // dm
    rows_m = seq // dm
    lane = lax.broadcasted_iota(jnp.int32, (qn, V7X_LANES), 1)
    qi = lax.broadcasted_iota(jnp.int32, (gc, qn, 2 * qn), 1)
    kj = lax.broadcasted_iota(jnp.int32, (gc, qn, 2 * qn), 2)
    dist = qi + qn - kj
    band = (dist >= 0) & (dist <= qn)
    zeros = jnp.zeros((qn, V7X_LANES), BF16)
    srcs = (q_in, k_in, v_in)

    for a, src in enumerate(srcs):
        for r in range(dm):
            sm[a, r] = src[0, 0, pl.ds(r, rows_m, stride=dm), :]

    def task_index(d, r, n):
        if d == d_lo:
            return None, slice(n * qn, (n + 1) * qn)
        if d == dm:
            return r, slice(n * qn, (n + 1) * qn)
        return r % dm, pl.ds(r // dm + n * qn * g_hi, qn, stride=g_hi)

    for bi, (window, d) in enumerate(DILATED_PAIRS):
        nb = seq // (d * qn)
        tasks = [(r, n) for r in range(d) for n in range(nb)]
        for g, (r, n) in enumerate(tasks):
            sel, rows = task_index(d, r, n)

            def operand(a, sel=sel, rows=rows):
                return srcs[a][0, 0, rows, :] if sel is None else sm[a, sel, rows, :]

            qv = operand(0)
            for h in range(n_head):
                qs[h, g] = jnp.where(lane // ATT_HEAD_DIM == h, qv, 0.0).astype(BF16)
            for a, dst in ((1, ks), (2, vs)):
                blk = operand(a).astype(BF16)
                dst[g, qn:, :] = blk
                if n + 1 < nb:
                    dst[g + 1, :qn, :] = blk
                if n == 0:
                    dst[g, :qn, :] = zeros

        for g0 in range(0, n_task, gc):
            gsl = slice(g0, g0 + gc)
            gidx = g0 + lax.broadcasted_iota(jnp.int32, (gc, qn, 2 * qn), 0)
            valid = band & ((gidx % nb != 0) | (kj >= qn))
            k = ks[gsl]
            v = vs[gsl]
            o = None
            for h in range(n_head):
                s = jnp.einsum("gqd,gkd->gqk", qs[h, gsl], k, preferred_element_type=F32)
                s = jnp.where(valid, s, NEG_INF)
                m = jnp.max(s, axis=-1, keepdims=True)
                p = jnp.exp2(s - m)
                den = jnp.sum(p, axis=-1, keepdims=True)
                oh = jnp.einsum("gqk,gkd->gqd", p.astype(BF16), v,
                                preferred_element_type=F32) / den
                lh = jnp.broadcast_to(m * LN_2 + jnp.log(den), oh.shape)
                if o is None:
                    o, lse = oh, lh
                else:
                    in_head = lane[None] // ATT_HEAD_DIM == h
                    o, lse = jnp.where(in_head, oh, o), jnp.where(in_head, lh, lse)
            for t in range(gc):
                sel, rows = task_index(d, *tasks[g0 + t])
                if sel is None:
                    no_lo[rows, :] = o[t]
                    nl_lo[rows, :] = lse[t]
                else:
                    no_s[bi - 1, sel, rows, :] = o[t]
                    nl_s[bi - 1, sel, rows, :] = lse[t]

    for r in range(dm):
        for c in range(rows_m // qn):
            rows = slice(c * qn, (c + 1) * qn)
            seq_rows = pl.ds(r + dm * c * qn, qn, stride=dm)
            ls = [nl_lo[seq_rows, :], nl_s[0, r, rows, :], nl_s[1, r, rows, :]]
            os_ = [no_lo[seq_rows, :], no_s[0, r, rows, :], no_s[1, r, rows, :]]
            mx = functools.reduce(jnp.maximum, ls)
            ws = [jnp.exp(l - mx) for l in ls]
            tot = functools.reduce(lambda x, y: x + y, ws)
            acc = functools.reduce(lambda x, y: x + y, [w * o for w, o in zip(ws, os_)])
            nat[seq_rows, :] = acc / tot
    out_ref[0, 0] = nat[...].astype(out_ref.dtype)


def _attention(q_hp, k_hp, v_hp):
    bsz, hp, seq, _ = q_hp.shape
    (_, d_lo), (_, dm), (_, d_hi) = DILATED_PAIRS
    assert d_lo == 1 and d_hi % dm == 0
    for window, d in DILATED_PAIRS:
        assert window // d == ATT_BLOCK and seq % (d * ATT_BLOCK) == 0
    n_task = seq // ATT_BLOCK
    assert n_task % min(ATT_TASKS_PER_STEP, n_task) == 0
    n_head = V7X_LANES // ATT_HEAD_DIM
    streams = (dm, seq // dm, V7X_LANES)
    spec = pl.BlockSpec((1, 1, seq, V7X_LANES), lambda b, h: (b, h, 0, 0))
    return pl.pallas_call(
        functools.partial(_attn_kernel, seq=seq),
        grid=(bsz, hp),
        in_specs=[spec, spec, spec],
        out_specs=spec,
        out_shape=jax.ShapeDtypeStruct((bsz, hp, seq, V7X_LANES), BF16),
        scratch_shapes=[pltpu.VMEM((n_head, n_task, ATT_BLOCK, V7X_LANES), BF16),
                        pltpu.VMEM((n_task, 2 * ATT_BLOCK, V7X_LANES), BF16),
                        pltpu.VMEM((n_task, 2 * ATT_BLOCK, V7X_LANES), BF16),
                        pltpu.VMEM((3,) + streams, F32),
                        pltpu.VMEM((seq, V7X_LANES), F32), pltpu.VMEM((seq, V7X_LANES), F32),
                        pltpu.VMEM((2,) + streams, F32), pltpu.VMEM((2,) + streams, F32),
                        pltpu.VMEM((seq, V7X_LANES), F32)],
        compiler_params=_cparams(("arbitrary", "arbitrary")),
        name="dilated_attn",
    )(q_hp, k_hp, v_hp)


def _hgrn_kernel(q_ref, f_ref, i_ref, g_ref, lb_ref, gw_ref, sums_ref, o_ref,
                 st_ref, hl_ref, kk_ref, cum_ref, *, heads, ts):
    @pl.when(pl.program_id(2) == 0)
    def _():
        st_ref[...] = jnp.zeros_like(st_ref)

    c_len = HG_CHUNK
    lanes = heads * HG_EXPAND
    lb = lb_ref[...]
    for c in range(ts // c_len):
        rows = slice(c * c_len, (c + 1) * c_len)
        f = lb + (1.0 - lb) * jax.nn.sigmoid(f_ref[0, rows, :].astype(F32))
        kk_ref[rows, :] = 1.0 - f
        hi, lo = _bf16_split(jnp.log(f))
        hl_ref[rows, :lanes] = hi
        hl_ref[rows, lanes:] = lo
    both = jnp.dot(sums_ref[...], hl_ref[...], preferred_element_type=F32)
    cum_ref[...] = both[:, :lanes] + both[:, lanes:]

    n_sub = c_len // HG_SUB

    def sub_block_anchor(b):
        mids = [b[i * HG_SUB + HG_SUB // 2:i * HG_SUB + HG_SUB // 2 + 1, :] for i in range(n_sub)]
        return jnp.concatenate([jnp.broadcast_to(m, (HG_SUB, m.shape[1])) for m in mids], axis=0)

    gi = lax.broadcasted_iota(jnp.int32, (heads * n_sub, HG_SUB, c_len), 0) % n_sub
    qi = lax.broadcasted_iota(jnp.int32, (heads * n_sub, HG_SUB, c_len), 1)
    si = lax.broadcasted_iota(jnp.int32, (heads * n_sub, HG_SUB, c_len), 2)
    causal = si <= gi * HG_SUB + qi
    key_row = lax.broadcasted_iota(jnp.int32, (c_len, HG_EXPAND), 0)

    for c in range(ts // c_len):
        rows = slice(c * c_len, (c + 1) * c_len)
        qts, kts, vbs = [], [], []
        for h in range(heads):
            lsl = slice(h * HG_EXPAND, (h + 1) * HG_EXPAND)
            b = cum_ref[rows, lsl]
            anchor = sub_block_anchor(b)
            kk = kk_ref[rows, lsl]
            qt = (q_ref[0, rows, lsl].astype(F32) * jnp.exp(b - anchor)).astype(BF16)
            qts.append(qt.reshape(n_sub, HG_SUB, HG_EXPAND))
            vb = i_ref[0, rows, lsl].astype(BF16)
            for i in range(n_sub):
                hi_r = (i + 1) * HG_SUB
                kt = kk * jnp.exp(anchor[i * HG_SUB:i * HG_SUB + 1, :] - b)
                kts.append(jnp.where(key_row < hi_r, kt, 0.0).astype(BF16))
                vbs.append(vb)
        a = jnp.einsum("gqk,gsk->gqs", jnp.concatenate(qts, axis=0), jnp.stack(kts),
                       preferred_element_type=F32)
        a = jnp.where(causal, a, 0.0).astype(BF16)
        o_intra = jnp.einsum("gqs,gsv->gqv", a, jnp.stack(vbs), preferred_element_type=F32)

        for h in range(heads):
            lsl = slice(h * HG_EXPAND, (h + 1) * HG_EXPAND)
            b = cum_ref[rows, lsl]
            b_last = jnp.broadcast_to(b[c_len - 1:c_len, :], b.shape)
            kk = kk_ref[rows, lsl]
            q = q_ref[0, rows, lsl].astype(F32)
            v = i_ref[0, rows, lsl].astype(F32)
            st = st_ref[h]
            o_inter = lax.dot_general((q * jnp.exp(b)).astype(BF16), st.astype(BF16), NT_DIMS,
                                      preferred_element_type=F32)
            o = o_inter + o_intra[h * n_sub:(h + 1) * n_sub].reshape(c_len, HG_EXPAND)
            kl = kk * jnp.exp(b_last - b)
            upd = jnp.dot(v.T.astype(BF16), kl.astype(BF16), preferred_element_type=F32)
            st_ref[h] = st * jnp.exp(b_last[0:1, :]) + upd
            o = o * lax.rsqrt(jnp.mean(o * o, axis=-1, keepdims=True) + RMS_EPS)
            g = g_ref[0, rows, lsl].astype(F32)
            o = o * gw_ref[:, lsl] * (g * jax.nn.sigmoid(g))
            o_ref[0, rows, lsl] = o.astype(o_ref.dtype)


def _hgrn_cumsum_matrix(ts):
    t = np.arange(ts)[:, None]
    s = np.arange(ts)[None, :]
    return (((t // HG_CHUNK) == (s // HG_CHUNK)) & (s <= t)).astype(np.float32)


def _hgrn2(proj3, lb, gw, att_w, hg_w):
    bsz, seq, _ = proj3.shape
    n_heads = hg_w // HG_EXPAND
    heads = _pick(n_heads, (8, 4, 2, 1))
    lanes = heads * HG_EXPAND
    ts = _pick(seq, (256, 128, 64))
    base = 3 * att_w
    assert base % lanes == 0 and hg_w % lanes == 0

    def seg(k):
        off = (base + k * hg_w) // lanes
        return pl.BlockSpec((1, ts, lanes), lambda b, h, s: (b, s, off + h))

    vec = pl.BlockSpec((1, lanes), lambda b, h, s: (0, h))
    sums = jnp.asarray(_hgrn_cumsum_matrix(ts), BF16)
    return pl.pallas_call(
        functools.partial(_hgrn_kernel, heads=heads, ts=ts),
        grid=(bsz, hg_w // lanes, seq // ts),
        in_specs=[seg(0), seg(1), seg(2), seg(3), vec, vec,
                  pl.BlockSpec((ts, ts), lambda b, h, s: (0, 0))],
        out_specs=pl.BlockSpec((1, ts, lanes), lambda b, h, s: (b, s, h)),
        out_shape=jax.ShapeDtypeStruct((bsz, seq, hg_w), BF16),
        scratch_shapes=[pltpu.VMEM((heads, HG_EXPAND, HG_EXPAND), F32),
                        pltpu.VMEM((ts, 2 * lanes), BF16),
                        pltpu.VMEM((ts, lanes), F32),
                        pltpu.VMEM((ts, lanes), F32)],
        compiler_params=_cparams(("arbitrary", "arbitrary", "arbitrary")),
        name="hgrn2",
    )(proj3, proj3, proj3, proj3, lb.reshape(1, hg_w), gw.reshape(1, hg_w), sums)


def _layer_norm(y, g, b):
    mu = jnp.mean(y, axis=-1, keepdims=True)
    yc = y - mu
    var = jnp.mean(yc * yc, axis=-1, keepdims=True)
    return yc * lax.rsqrt(var + LN_EPS) * g + b


def _bf16_split(x):
    hi = x.astype(BF16)
    lo = (x - hi.astype(F32)).astype(BF16)
    return hi, lo


def _pack_bf16_pair(lo, hi):
    lo_bits = lax.bitcast_convert_type(lo.astype(BF16).astype(F32), jnp.uint32)
    hi_bits = lax.bitcast_convert_type(hi.astype(BF16).astype(F32), jnp.uint32)
    return (lo_bits >> 16) | (hi_bits & jnp.uint32(0xFFFF0000))


def _store_token_rows(ref, tok0, words):
    n, width = words.shape
    nblk = width // V7X_LANES
    for c in range(nblk):
        ref[pl.ds(tok0 * nblk + c, n, stride=nblk), :] = words[:, c * V7X_LANES:(c + 1) * V7X_LANES]


def _load_token_rows(ref, n, nblk, tok0=0):
    return jnp.concatenate([ref[pl.ds(tok0 * nblk + c, n, stride=nblk), :] for c in range(nblk)],
                           axis=1)


def _unpack_bf16_pair(words):
    lo = lax.bitcast_convert_type(words << 16, F32)
    hi = lax.bitcast_convert_type(words & jnp.uint32(0xFFFF0000), F32)
    return lo, hi


def _load_weight_bf16(w_hbm, wbf_ref, stage_ref, sem):
    rows = stage_ref.shape[1]
    n_slab = w_hbm.shape[0] // rows

    def fetch(s, slot):
        return pltpu.make_async_copy(w_hbm.at[pl.ds(s * rows, rows)], stage_ref.at[slot],
                                     sem.at[slot])

    fetch(0, 0).start()
    for s in range(n_slab):
        if s + 1 < n_slab:
            fetch(s + 1, (s + 1) % 2).start()
        fetch(s, s % 2).wait()
        wbf_ref[s * rows:(s + 1) * rows, :] = stage_ref[s % 2].astype(BF16)


def _out_proj_kernel(att_ref, rec_ref, x_ref, wo_hbm, gt_ref, sc_ref, sh_ref, g_ref, b_ref,
                     rw_ref, rb_ref, x1_ref, h2_ref, lg_ref, wobf_ref, rwhi_ref, rwlo_ref,
                     stage_ref, hi_ref, lo_ref, mix_a, mix_b, sem, *, alpha, att_w):
    i = pl.program_id(0)

    @pl.when(i == 0)
    def _():
        _load_weight_bf16(wo_hbm, wobf_ref, stage_ref, sem)
        hi, lo = _bf16_split(rw_ref[...])
        rwhi_ref[...] = hi
        rwlo_ref[...] = lo
        mix_b[...] = jnp.zeros_like(mix_b)

    def step(mix_out, mix_in):
        att = jnp.concatenate([att_ref[0, h] for h in range(att_ref.shape[1])], axis=1)
        mix_out[...] = (jnp.dot(att, wobf_ref[:att_w, :], preferred_element_type=F32)
                        + jnp.dot(rec_ref[...], wobf_ref[att_w:, :], preferred_element_type=F32))
        half = x_ref.shape[1] // 2
        for r in range(x_ref.shape[0] // LN_ROW_GROUP):
            rows = slice(r * LN_ROW_GROUP, (r + 1) * LN_ROW_GROUP)
            y = alpha * x_ref[rows, :] + (1.0 + gt_ref[0]) * mix_in[rows, :]
            x1 = _layer_norm(y, g_ref[...], b_ref[...])
            x1_ref[rows, :] = x1
            h2 = x1 * (1.0 + sc_ref[0]) + sh_ref[0]
            hi, lo = _bf16_split(h2)
            hi_ref[rows, :] = hi
            lo_ref[rows, :] = lo
            _store_token_rows(h2_ref, r * LN_ROW_GROUP,
                              _pack_bf16_pair(h2[:, :half], h2[:, half:]))
        lg = (jnp.dot(hi_ref[...], rwhi_ref[...], preferred_element_type=F32)
              + jnp.dot(lo_ref[...], rwhi_ref[...], preferred_element_type=F32)
              + jnp.dot(hi_ref[...], rwlo_ref[...], preferred_element_type=F32))
        lg_ref[...] = lg.T[:lg_ref.shape[0], :] + rb_ref[...]

    @pl.when(i % 2 == 0)
    def _():
        step(mix_a, mix_b)

    @pl.when(i % 2 == 1)
    def _():
        step(mix_b, mix_a)


OUT_PROJ_STAGE_ROWS = 256
LN_ROW_GROUP = 16


def _out_proj(att, rec, x2d, wo, gt, sc, sh, g, b, rw, rb, seq, alpha):
    t, d = x2d.shape
    hp = att.shape[1]
    att_w = hp * V7X_LANES
    n_e = rw.shape[1]
    assert n_e <= V7X_LANES and d % OUT_PROJ_STAGE_ROWS == 0
    rw_pad = jnp.zeros((d, V7X_LANES), F32).at[:, :n_e].set(rw)
    nblk = d // 2 // V7X_LANES
    tm = _pick(seq, (512, 256, 128))
    per_b = seq // tm
    n_tile = t // tm
    mm = lambda i: jnp.minimum(i, n_tile - 1)
    ep = lambda i: jnp.maximum(i - 1, 0)
    vec3 = pl.BlockSpec((1, 1, d), lambda i: (ep(i) // per_b, 0, 0))
    full = lambda shape: pl.BlockSpec(shape, lambda i: (0,) * len(shape))
    return pl.pallas_call(
        functools.partial(_out_proj_kernel, alpha=alpha, att_w=att_w),
        grid=(n_tile + 1,),
        in_specs=[pl.BlockSpec((1, hp, tm, V7X_LANES),
                               lambda i: (mm(i) // per_b, 0, mm(i) % per_b, 0)),
                  pl.BlockSpec((tm, d - att_w), lambda i: (mm(i), 0)),
                  pl.BlockSpec((tm, d), lambda i: (ep(i), 0)),
                  pl.BlockSpec(memory_space=pl.ANY), vec3, vec3, vec3, full((1, d)), full((1, d)),
                  full((d, V7X_LANES)), full((n_e, 1))],
        out_specs=[pl.BlockSpec((tm, d), lambda i: (ep(i), 0)),
                   pl.BlockSpec((tm * nblk, V7X_LANES), lambda i: (ep(i), 0)),
                   pl.BlockSpec((n_e, tm), lambda i: (0, ep(i)))],
        out_shape=[jax.ShapeDtypeStruct((t, d), F32),
                   jax.ShapeDtypeStruct((t * nblk, V7X_LANES), jnp.uint32),
                   jax.ShapeDtypeStruct((n_e, t), F32)],
        scratch_shapes=[pltpu.VMEM((d, d), BF16), pltpu.VMEM((d, V7X_LANES), BF16),
                        pltpu.VMEM((d, V7X_LANES), BF16),
                        pltpu.VMEM((2, OUT_PROJ_STAGE_ROWS, d), F32),
                        pltpu.VMEM((tm, d), BF16), pltpu.VMEM((tm, d), BF16),
                        pltpu.VMEM((tm, d), F32), pltpu.VMEM((tm, d), F32),
                        pltpu.SemaphoreType.DMA((2,))],
        compiler_params=_cparams(("arbitrary",)),
        name="out_proj_ln1",
    )(att, rec, x2d, wo, gt[:, None, :], sc[:, None, :], sh[:, None, :],
      g.reshape(1, d), b.reshape(1, d), rw_pad, rb.reshape(n_e, 1))


def _routing_kernel(lg_ref, tri_ref, low_ref, gate_ref, dest_ref, tile_ref, pad_ref, sel_scr, rk_scr,
                    *, blk, tm):
    n_e, t = lg_ref.shape
    eidx = lax.broadcasted_iota(jnp.int32, (n_e, t), 0)
    cur = lg_ref[...]
    vals, idxs = [], []
    for _ in range(TOP_K):
        m = jnp.max(cur, axis=0, keepdims=True)
        ik = jnp.min(jnp.where(cur == m, eidx, n_e), axis=0, keepdims=True)
        cur = jnp.where(eidx == ik, -jnp.inf, cur)
        vals.append(m)
        idxs.append(ik)
    es = [jnp.exp(v - vals[0]) for v in vals]
    den = functools.reduce(lambda a, b: a + b, es)
    for k in range(TOP_K):
        gate_ref[k:k + 1, :] = es[k] / den
    sel = functools.reduce(lambda a, b: a | b, [eidx == ik for ik in idxs])
    sel_scr[...] = jnp.where(sel, 1.0, 0.0)

    tri = tri_ref[...]
    carry = jnp.zeros((n_e, 1), F32)
    for j in range(t // blk):
        sb = sel_scr[:, j * blk:(j + 1) * blk]
        pre = jnp.dot(sb.astype(BF16), tri, preferred_element_type=F32)
        rk_scr[:, j * blk:(j + 1) * blk] = pre + carry
        carry = carry + jnp.sum(sb, axis=1, keepdims=True)

    counts = jnp.broadcast_to(carry, (n_e, V7X_LANES))
    padded = jnp.floor((counts + (tm - 1)) * (1.0 / tm)) * tm
    pends = jnp.dot(low_ref[...], padded, precision=lax.Precision.HIGHEST,
                    preferred_element_type=F32)
    pstarts = pends - padded
    row0 = rk_scr[...] + pstarts[:, 0:1]
    for k in range(TOP_K):
        dest_ref[k:k + 1, :] = jnp.sum(jnp.where(eidx == idxs[k], row0, 0.0), axis=0,
                                       keepdims=True).astype(jnp.int32)
    starts = (lax.broadcasted_iota(jnp.int32, (n_e, tile_ref.shape[1]), 1) * tm).astype(F32)
    owner = jnp.sum(jnp.where(pends[:, 0:1] <= starts, 1.0, 0.0), axis=0, keepdims=True)
    tile_ref[0:1, :] = jnp.minimum(owner, n_e - 1.0).astype(jnp.int32)
    tile_ref[1:2, :] = jnp.broadcast_to(jnp.max(pends[:, 0:1], axis=0, keepdims=True),
                                        (1, tile_ref.shape[1])).astype(jnp.int32)
    e_tile = lax.broadcasted_iota(jnp.int32, starts.shape, 0).astype(F32)
    row_end = jnp.sum(jnp.where(e_tile == owner, (pstarts + counts)[:, 0:1], 0.0), axis=0,
                      keepdims=True)
    tile_ref[2:3, :] = jnp.clip(row_end - starts[0:1, :], 0.0, float(tm)).astype(jnp.int32)
    pad_ref[0] = (pstarts + counts).astype(jnp.int32)
    pad_ref[1] = (padded - counts).astype(jnp.int32)


def _routing(logits_t, tm, n_tiles):
    n_e, t = logits_t.shape
    assert tm & (tm - 1) == 0
    blk = _pick(t, (256, 128))
    tri = jnp.asarray(np.triu(np.ones((blk, blk), np.float32), 1), BF16)
    low = jnp.asarray(np.tril(np.ones((n_e, n_e), np.float32)))
    ntp = -(-n_tiles // V7X_LANES) * V7X_LANES
    full = lambda shape: pl.BlockSpec(shape, lambda: (0,) * len(shape))
    return pl.pallas_call(
        functools.partial(_routing_kernel, blk=blk, tm=tm),
        in_specs=[full((n_e, t)), full((blk, blk)), full((n_e, n_e))],
        out_specs=[full((TOP_K, t)), full((TOP_K, t)), full((3, ntp)), full((2, n_e, V7X_LANES))],
        out_shape=[jax.ShapeDtypeStruct((TOP_K, t), F32),
                   jax.ShapeDtypeStruct((TOP_K, t), jnp.int32),
                   jax.ShapeDtypeStruct((3, ntp), jnp.int32),
                   jax.ShapeDtypeStruct((2, n_e, V7X_LANES), jnp.int32)],
        scratch_shapes=[pltpu.VMEM((n_e, t), F32), pltpu.VMEM((n_e, t), F32)],
        compiler_params=pltpu.CompilerParams(vmem_limit_bytes=V7X_VMEM_LIMIT),
        name="routing",
    )(logits_t, tri, low)


def _dispatch_kernel(dest_ref, pad_off_ref, pad_n_ref, nrow_ref, h_ref, o_ref, zero_ref, sem, zsem,
                     *, n_tok, tq, tm, n_e, nblk):
    step = pl.program_id(0)
    base = step * tq

    def rows(tok, n=1):
        return pl.ds(pl.multiple_of(tok * nblk, nblk), n * nblk)

    def fill(off, n):
        return pltpu.make_async_copy(zero_ref.at[rows(0, n)], o_ref.at[rows(off, n)], zsem)

    pieces = [1 << s for s in range(tm.bit_length() - 2, -1, -1)]

    def pad_rows(wait):
        def per_expert(e, c):
            off = pad_off_ref[e]
            n = pad_n_ref[e]
            for p in pieces:
                hit = (n & p) != 0

                @pl.when(hit)
                def _(off=off, p=p):
                    cp = fill(off, p)
                    cp.wait() if wait else cp.start()

                off = off + jnp.where(hit, p, 0)
            return c
        lax.fori_loop(0, n_e, per_expert, 0)

        def per_tile(i, c):
            cp = fill(i * tm, tm)
            cp.wait() if wait else cp.start()
            return c
        lax.fori_loop(nrow_ref[0] // tm, o_ref.shape[0] // (tm * nblk), per_tile, 0)

    @pl.when(step == 0)
    def _():
        zero_ref[...] = jnp.zeros_like(zero_ref)
        pad_rows(False)

    def start(r, c):
        for k in range(TOP_K):
            dst = o_ref.at[rows(dest_ref[k * n_tok + base + r])]
            pltpu.make_async_copy(h_ref.at[rows(r)], dst, sem).start(priority=k % 2)
        return c

    lax.fori_loop(0, tq, start, 0, unroll=4)
    for _ in range(TOP_K):
        pltpu.make_async_copy(h_ref, h_ref, sem).wait()

    @pl.when(step == 0)
    def _():
        pad_rows(True)


def _dispatch(h2p, dest_flat, pad_off, pad_n, n_used_rows, n_rows, tm, nblk):
    t = h2p.shape[0] // nblk
    n_e = pad_off.shape[0]
    tq = _pick(t, (256, 128))
    return pl.pallas_call(
        functools.partial(_dispatch_kernel, n_tok=t, tq=tq, tm=tm, n_e=n_e, nblk=nblk),
        grid_spec=pltpu.PrefetchScalarGridSpec(
            num_scalar_prefetch=4, grid=(t // tq,),
            in_specs=[pl.BlockSpec((tq * nblk, V7X_LANES), lambda i, *_: (i, 0))],
            out_specs=pl.BlockSpec(memory_space=pl.ANY),
            scratch_shapes=[pltpu.VMEM((tm * nblk, V7X_LANES), h2p.dtype),
                            pltpu.SemaphoreType.DMA(()), pltpu.SemaphoreType.DMA(())]),
        out_shape=jax.ShapeDtypeStruct((n_rows * nblk, V7X_LANES), h2p.dtype),
        compiler_params=pltpu.CompilerParams(dimension_semantics=("arbitrary",),
                                             has_side_effects=True,
                                             vmem_limit_bytes=V7X_VMEM_LIMIT),
        name="moe_dispatch",
    )(dest_flat, pad_off, pad_n, n_used_rows, h2p)


EXPERT_COL_BLOCK = 2048
GEMM1_COL_CHUNK = 256
MOE_TILE_LEVELS = 3
MOE_MIN_ROWS = 16
CAST_ROWS = 256


def _stream_expert_weights(te_ref, nt_ref, w_hbm, stage_ref, slot_ref, sem):
    j, i = pl.program_id(0), pl.program_id(1)
    n_pass, n_tiles = pl.num_programs(0), pl.num_programs(1)
    tn = stage_ref.shape[2]
    n_used = nt_ref[0]
    cur = te_ref[i]
    first = (i < n_used) & ((i == 0) | (cur != te_ref[jnp.maximum(i - 1, 0)]))

    def fetch(e, jj, slot):
        src = w_hbm.at[e, :, pl.ds(pl.multiple_of(jj * tn, tn), tn)]
        return pltpu.make_async_copy(src, stage_ref.at[slot], sem.at[slot])

    @pl.when((i == 0) & (j == 0))
    def _():
        slot_ref[0] = 1
        fetch(te_ref[0], 0, 0).start()

    @pl.when(first)
    def _():
        slot = 1 - slot_ref[0]
        slot_ref[0] = slot
        fetch(cur, j, slot).wait()

        def same_group(k):
            return (k < n_used) & (te_ref[jnp.minimum(k, n_tiles - 1)] == cur)
        nxt = lax.while_loop(same_group, lambda k: k + 1, i + 1)

        @pl.when(nxt < n_used)
        def _():
            fetch(te_ref[jnp.minimum(nxt, n_tiles - 1)], j, 1 - slot).start()

        @pl.when((nxt >= n_used) & (j + 1 < n_pass))
        def _():
            fetch(te_ref[0], j + 1, 1 - slot).start()

    return first, slot_ref[0]


def _cast_block(stage_ref, slot, wbf_ref):
    def cast(r, c):
        rows = pl.ds(pl.multiple_of(r * CAST_ROWS, CAST_ROWS), CAST_ROWS)
        wbf_ref[rows, :] = stage_ref[slot, rows, :].astype(BF16)
        return c
    lax.fori_loop(0, wbf_ref.shape[0] // CAST_ROWS, cast, 0)


def _gemm1_kernel(te_ref, nt_ref, tv_ref, x_ref, w_hbm, b_ref, o_ref, stage_ref, wbf_ref,
                  slot_ref, sem, *, nc, nblk):
    first, slot = _stream_expert_weights(te_ref, nt_ref, w_hbm, stage_ref, slot_ref, sem)

    def compute(rows, convert):
        words = _load_token_rows(x_ref, rows, nblk)
        x_lo, x_hi = (v.astype(BF16) for v in _unpack_bf16_pair(words))
        half = x_lo.shape[1]
        lane128 = lax.broadcasted_iota(jnp.int32, (rows, V7X_LANES), 1)
        even_idx = (2 * lane128) % V7X_LANES
        for n0 in range(0, wbf_ref.shape[1], nc):
            if convert:
                wbf_ref[:, n0:n0 + nc] = stage_ref[slot, :, n0:n0 + nc].astype(BF16)
            hb = (jnp.dot(x_lo, wbf_ref[:half, n0:n0 + nc], preferred_element_type=F32)
                  + jnp.dot(x_hi, wbf_ref[half:, n0:n0 + nc], preferred_element_type=F32)
                  + b_ref[0, :, n0:n0 + nc])
            nxt = pltpu.roll(hb, nc - 1, axis=1)
            glu = jnp.minimum(hb, SWIGLU_LIMIT)
            lin = jnp.clip(nxt, -SWIGLU_LIMIT, SWIGLU_LIMIT)
            act = glu * jax.nn.sigmoid(SWIGLU_ALPHA * glu) * (lin + 1.0)
            for c in range(nc // (2 * V7X_LANES)):
                lo_blk = act[:, (2 * c) * V7X_LANES:(2 * c + 1) * V7X_LANES]
                hi_blk = act[:, (2 * c + 1) * V7X_LANES:(2 * c + 2) * V7X_LANES]
                out = jnp.where(lane128 < V7X_LANES // 2,
                                jnp.take_along_axis(lo_blk, even_idx, axis=1),
                                jnp.take_along_axis(hi_blk, even_idx, axis=1))
                o0 = n0 // 2 + c * V7X_LANES
                o_ref[:rows, o0:o0 + V7X_LANES] = out.astype(o_ref.dtype)
        if rows < o_ref.shape[0]:
            o_ref[rows:, :] = jnp.zeros((o_ref.shape[0] - rows, o_ref.shape[1]), o_ref.dtype)

    _for_real_rows(pl.program_id(1), nt_ref, tv_ref, o_ref, o_ref.shape[0], compute,
                   first, lambda: _cast_block(stage_ref, slot, wbf_ref))


def _for_real_rows(i, nt_ref, tv_ref, o_ref, tm, compute, first, convert_all):
    used = i < nt_ref[0]
    real = tv_ref[i]
    sizes = [tm >> s for s in range(MOE_TILE_LEVELS) if (tm >> s) % MOE_MIN_ROWS == 0]

    @pl.when(first & (real <= sizes[1]))
    def _():
        convert_all()

    for n, rows in enumerate(sizes):
        fits = real <= rows
        if n + 1 < len(sizes):
            fits = fits & (real > sizes[n + 1])
        if n == 0:
            @pl.when(used & fits & first)
            def _():
                compute(rows, True)
            fits = fits & jnp.logical_not(first)

        @pl.when(used & fits)
        def _(rows=rows):
            compute(rows, False)

    @pl.when(jnp.logical_not(used))
    def _():
        o_ref[...] = jnp.zeros_like(o_ref)


def _gemm1(xin, w1, b1, tile_e, n_tiles_used, tile_rows, tm, nblk):
    n_rows = xin.shape[0] // nblk
    n_e, d, f2 = w1.shape
    tn = min(EXPERT_COL_BLOCK, f2)
    nc = min(GEMM1_COL_CHUNK, tn)
    assert f2 % tn == 0 and d % CAST_ROWS == 0
    n_tiles = n_rows // tm
    assert nc % (2 * V7X_LANES) == 0

    def used(i, nt):
        return jnp.minimum(i, nt[0] - 1)

    return pl.pallas_call(
        functools.partial(_gemm1_kernel, nc=nc, nblk=nblk),
        grid_spec=pltpu.PrefetchScalarGridSpec(
            num_scalar_prefetch=3, grid=(f2 // tn, n_tiles),
            in_specs=[pl.BlockSpec((tm * nblk, V7X_LANES),
                                   lambda j, i, te, nt, tv: (used(i, nt), 0)),
                      pl.BlockSpec(memory_space=pl.ANY),
                      pl.BlockSpec((1, 1, tn), lambda j, i, te, nt, tv: (te[used(i, nt)], 0, j))],
            out_specs=pl.BlockSpec((tm, tn // 2), lambda j, i, te, nt, tv: (i, j)),
            scratch_shapes=[pltpu.VMEM((2, d, tn), F32), pltpu.VMEM((d, tn), BF16),
                            pltpu.SMEM((1,), jnp.int32), pltpu.SemaphoreType.DMA((2,))]),
        out_shape=jax.ShapeDtypeStruct((n_rows, f2 // 2), BF16),
        compiler_params=_cparams(("arbitrary", "arbitrary")),
        name="moe_gemm1",
    )(tile_e, n_tiles_used, tile_rows, xin, w1, b1.reshape(n_e, 1, f2))


def _gemm2_kernel(te_ref, nt_ref, tv_ref, a_ref, w_hbm, b_ref, o_ref, stage_ref, wbf_ref, slot_ref,
                  sem, *, nc, nblk):
    first, slot = _stream_expert_weights(te_ref, nt_ref, w_hbm, stage_ref, slot_ref, sem)
    tm = a_ref.shape[0]

    def compute(rows, convert):
        a = a_ref[:rows, :]
        half = wbf_ref.shape[1] // 2

        def cols(n0):
            if convert:
                wbf_ref[:, n0:n0 + nc] = stage_ref[slot, :, n0:n0 + nc].astype(BF16)
            return (jnp.dot(a, wbf_ref[:, n0:n0 + nc], preferred_element_type=F32)
                    + b_ref[0, :, n0:n0 + nc])

        for n0 in range(0, half, nc):
            words = _pack_bf16_pair(cols(n0), cols(half + n0))
            for c in range(nc // V7X_LANES):
                blk = n0 // V7X_LANES + c
                o_ref[pl.ds(blk, rows, stride=nblk), :] = words[:, c * V7X_LANES:(c + 1) * V7X_LANES]
        if rows < tm:
            o_ref[rows * nblk:, :] = jnp.zeros(((tm - rows) * nblk, o_ref.shape[1]), o_ref.dtype)

    _for_real_rows(pl.program_id(1), nt_ref, tv_ref, o_ref, tm, compute,
                   first, lambda: _cast_block(stage_ref, slot, wbf_ref))


def _gemm2(act, w2, b2, tile_e, n_tiles_used, tile_rows, tm):
    n_rows, f = act.shape
    n_e, _, d = w2.shape
    tn = d
    nc = min(512, tn // 2)
    nblk = d // 2 // V7X_LANES
    assert f % CAST_ROWS == 0
    n_tiles = n_rows // tm

    def used(i, nt):
        return jnp.minimum(i, nt[0] - 1)

    return pl.pallas_call(
        functools.partial(_gemm2_kernel, nc=nc, nblk=nblk),
        grid_spec=pltpu.PrefetchScalarGridSpec(
            num_scalar_prefetch=3, grid=(d // tn, n_tiles),
            in_specs=[pl.BlockSpec((tm, f), lambda j, i, te, nt, tv: (used(i, nt), 0)),
                      pl.BlockSpec(memory_space=pl.ANY),
                      pl.BlockSpec((1, 1, tn), lambda j, i, te, nt, tv: (te[used(i, nt)], 0, j))],
            out_specs=pl.BlockSpec((tm * nblk, V7X_LANES), lambda j, i, te, nt, tv: (i, 0)),
            scratch_shapes=[pltpu.VMEM((2, f, tn), F32), pltpu.VMEM((f, tn), BF16),
                            pltpu.SMEM((1,), jnp.int32), pltpu.SemaphoreType.DMA((2,))]),
        out_shape=jax.ShapeDtypeStruct((n_rows * nblk, V7X_LANES), jnp.uint32),
        compiler_params=_cparams(("arbitrary", "arbitrary")),
        name="moe_gemm2",
    )(tile_e, n_tiles_used, tile_rows, act, w2, b2.reshape(n_e, 1, d))


COMBINE_ROW_GROUP = 64


def _combine_kernel(dest_ref, x1_ref, gate_ref, gt_ref, g_ref, b_ref, y_ref, o_ref, buf, sem,
                    *, alpha, tc, n_tok, nblk):
    i = pl.program_id(0)
    n_steps = pl.num_programs(0)

    def rows(tok):
        return pl.ds(pl.multiple_of(tok * nblk, nblk), nblk)

    def copy(step, slot, k, r):
        src = y_ref.at[rows(dest_ref[k * n_tok + step * tc + r])]
        return pltpu.make_async_copy(src, buf.at[slot, k, rows(r)], sem.at[slot])

    def issue(step, slot):
        def body(r, c):
            for k in range(TOP_K):
                copy(step, slot, k, r).start(priority=k % 2)
            return c
        lax.fori_loop(0, tc, body, 0, unroll=2)

    def drain(slot):
        pltpu.make_async_copy(buf.at[slot], buf.at[slot], sem.at[slot]).wait()

    @pl.when(i == 0)
    def _():
        issue(0, 0)

    def step(slot, prefetch):
        drain(slot)
        for rg in range(tc // COMBINE_ROW_GROUP):
            r0 = rg * COMBINE_ROW_GROUP
            if prefetch:
                for r in range(r0, r0 + COMBINE_ROW_GROUP):
                    for k in range(TOP_K):
                        copy(i + 1, 1 - slot, k, r).start(priority=k % 2)
            rows = slice(r0, r0 + COMBINE_ROW_GROUP)
            gates = gate_ref[rows, :]
            ff_lo = ff_hi = None
            for k in range(TOP_K):
                lo, hi = _unpack_bf16_pair(
                    _load_token_rows(buf.at[slot, k], COMBINE_ROW_GROUP, nblk, tok0=r0))
                gk = gates[:, k:k + 1]
                ff_lo = gk * lo if ff_lo is None else ff_lo + gk * lo
                ff_hi = gk * hi if ff_hi is None else ff_hi + gk * hi
            ff = jnp.concatenate([ff_lo, ff_hi], axis=1)
            y = alpha * x1_ref[rows, :] + (1.0 + gt_ref[0]) * ff
            o_ref[rows, :] = _layer_norm(y, g_ref[...], b_ref[...])

    for slot in range(2):
        for prefetch in (True, False):
            @pl.when((i % 2 == slot) & ((i + 1 < n_steps) == prefetch))
            def _(slot=slot, prefetch=prefetch):
                step(slot, prefetch)


def _combine(dest_flat, x1, gates_tk, gt, g, b, y, seq, alpha):
    t, d = x1.shape
    tc = _pick(seq, (512, 256, 128))
    per_b = seq // tc
    nblk = d // 2 // V7X_LANES
    return pl.pallas_call(
        functools.partial(_combine_kernel, alpha=alpha, tc=tc, n_tok=t, nblk=nblk),
        grid_spec=pltpu.PrefetchScalarGridSpec(
            num_scalar_prefetch=1, grid=(t // tc,),
            in_specs=[pl.BlockSpec((tc, d), lambda i, ds: (i, 0)),
                      pl.BlockSpec((tc, TOP_K), lambda i, ds: (i, 0)),
                      pl.BlockSpec((1, 1, d), lambda i, ds: (i // per_b, 0, 0)),
                      pl.BlockSpec((1, d), lambda i, ds: (0, 0)),
                      pl.BlockSpec((1, d), lambda i, ds: (0, 0)),
                      pl.BlockSpec(memory_space=pl.ANY)],
            out_specs=pl.BlockSpec((tc, d), lambda i, ds: (i, 0)),
            scratch_shapes=[pltpu.VMEM((2, TOP_K, tc * nblk, V7X_LANES), jnp.uint32),
                            pltpu.SemaphoreType.DMA((2,))]),
        out_shape=jax.ShapeDtypeStruct((t, d), F32),
        compiler_params=_cparams(("arbitrary",)),
        name="moe_combine_ln2",
    )(dest_flat, x1, gates_tk, gt[:, None, :], g.reshape(1, d), b.reshape(1, d), y)


MOE_ROW_TILE = 512


def _moe_ffn(h2p, logits_t, x1, gt_f, ln_g, ln_b, w1, b1, w2, b2, seq, alpha):
    tm = MOE_ROW_TILE
    n_e, t = logits_t.shape
    n_rows = -(-(TOP_K * t) // tm) * tm + n_e * tm
    n_tiles = n_rows // tm
    gate_t, dest, tiles, pads = _routing(logits_t, tm, n_tiles)
    dest = dest.reshape(-1)
    tile_e = tiles[0, :n_tiles]
    n_used_rows = tiles[1, :1]
    tile_rows = tiles[2, :n_tiles]
    n_used_tiles = n_used_rows // tm
    nblk = x1.shape[1] // 2 // V7X_LANES
    xin = _dispatch(h2p, dest, pads[0, :, 0], pads[1, :, 0], n_used_rows, n_rows, tm, nblk)
    act = _gemm1(xin, w1, b1, tile_e, n_used_tiles, tile_rows, tm, nblk)
    y = _gemm2(act, w2, b2, tile_e, n_used_tiles, tile_rows, tm)
    return _combine(dest, x1, gate_t.T, gt_f, ln_g, ln_b, y, seq, alpha)


def kernel(x, c, positions, w_ada, b_ada, w_in, hgrn_lb, gnorm_w, w_o, ln1_g, ln1_b,
           router_w, router_b, w1, b1, w2, b2, ln2_g, ln2_b):
    bsz, seq, d = x.shape
    depth = w_ada.shape[0]
    t = bsz * seq
    att_w = d // 2
    hg_w = d - att_w
    alpha = (2.0 * depth) ** 0.25

    lb_all = jnp.cumsum(jax.nn.softmax(hgrn_lb.astype(F32), axis=0), axis=0)
    inv = ROPE_THETA ** (-(jnp.arange(0, ROT_DIM, 2, dtype=F32) / ROT_DIM))
    lane = np.arange(V7X_LANES)
    inv_lane = inv[(lane % ATT_HEAD_DIM) % (ROT_DIM // 2)].reshape(1, V7X_LANES)
    pos_b = jnp.broadcast_to(positions.astype(F32).reshape(t, 1), (t, V7X_LANES))

    x2d = x.reshape(t, d)
    for l in range(depth):
        mod = _adaln(c, w_ada[l], b_ada[l])
        sh_a, sc_a, gt_a, sh_f, sc_f, gt_f = jnp.split(mod, 6, axis=-1)

        proj = _in_proj(x2d, sc_a, sh_a, w_in[l], seq)
        proj3 = proj.reshape(bsz, seq, proj.shape[1])
        q_hp, k_hp, v_hp = _qk_rope(proj, pos_b, inv_lane, bsz, seq, att_w)
        att = _attention(q_hp, k_hp, v_hp)
        rec = _hgrn2(proj3, lb_all[l], gnorm_w[l], att_w, hg_w).reshape(t, hg_w)

        x1, h2p, logits_t = _out_proj(att, rec, x2d, w_o[l], gt_a, sc_f, sh_f, ln1_g[l], ln1_b[l],
                                      router_w[l], router_b[l], seq, alpha)

        x2d = _moe_ffn(h2p, logits_t, x1, gt_f, ln2_g[l], ln2_b[l], w1[l], b1[l], w2[l], b2[l],
                       seq, alpha)
    return x2d.reshape(bsz, seq, d)
```

```python
import functools

import numpy as np
import jax
import jax.numpy as jnp
from jax import lax
from jax.experimental import pallas as pl
from jax.experimental.pallas import tpu as pltpu

F32 = jnp.float32
BF16 = jnp.bfloat16

V7X_LANES = 128
V7X_VMEM_LIMIT = 56 * 1024 * 1024

ATT_HEAD_DIM = 64
DILATED_PAIRS = ((128, 1), (512, 4), (2048, 16))
ATT_BLOCK = 128
ROT_DIM = ATT_HEAD_DIM // 4
ROPE_THETA = 500000.0
HG_EXPAND = 128
HG_CHUNK = 64
HG_SUB = 16
TOP_K = 4
SWIGLU_ALPHA = 1.702
SWIGLU_LIMIT = 7.0
LN_EPS = 1e-5
RMS_EPS = 1e-6
NEG_INF = -1e30

LN_2 = float(np.log(2.0))
Q_SCALE = ATT_HEAD_DIM ** -0.5 / LN_2

NT_DIMS = (((1,), (1,)), ((), ()))


def _pick(n, candidates):
    for c in candidates:
        if n % c == 0:
            return c
    raise ValueError(f"no tile in {candidates} divides {n}")


def _cparams(sem):
    return pltpu.CompilerParams(dimension_semantics=sem, vmem_limit_bytes=V7X_VMEM_LIMIT)


def _adaln_kernel(c_ref, w_ref, b_ref, o_ref):
    c = c_ref[...]
    s = c * jax.nn.sigmoid(c)
    o_ref[...] = jnp.dot(s.astype(BF16), w_ref[...].astype(BF16),
                         preferred_element_type=F32) + b_ref[...]


def _adaln(c, w, b):
    bsz, d = c.shape
    n = w.shape[1]
    rows = 8
    cp = jnp.zeros((rows, d), F32).at[:bsz].set(c)
    tn = _pick(n, (1024, 512, 256, 128))
    out = pl.pallas_call(
        _adaln_kernel,
        grid=(n // tn,),
        in_specs=[pl.BlockSpec((rows, d), lambda j: (0, 0)),
                  pl.BlockSpec((d, tn), lambda j: (0, j)),
                  pl.BlockSpec((1, tn), lambda j: (0, j))],
        out_specs=pl.BlockSpec((rows, tn), lambda j: (0, j)),
        out_shape=jax.ShapeDtypeStruct((rows, n), F32),
        compiler_params=_cparams(("arbitrary",)),
        name="adaln",
    )(cp, w, b.reshape(1, n))
    return out[:bsz]


def _in_proj_kernel(x_ref, sc_ref, sh_ref, w_hbm, o_ref, wbf_ref, h_ref, stage_ref, sem):
    m, n = pl.program_id(0), pl.program_id(1)
    n_col = pl.num_programs(1)
    tn = o_ref.shape[1]

    def fetch(nn):
        src = w_hbm.at[:, pl.ds(pl.multiple_of(nn * tn, tn), tn)]
        return pltpu.make_async_copy(src, stage_ref, sem)

    @pl.when((m == 0) & (n == 0))
    def _():
        fetch(0).start()

    @pl.when(m == 0)
    def _():
        fetch(n).wait()
        wbf_ref[n] = stage_ref[...].astype(BF16)

        @pl.when(n + 1 < n_col)
        def _():
            fetch(n + 1).start()

    @pl.when(n == 0)
    def _():
        h_ref[...] = (x_ref[...] * (1.0 + sc_ref[0]) + sh_ref[0]).astype(BF16)

    o_ref[...] = jnp.dot(h_ref[...], wbf_ref[n], preferred_element_type=F32).astype(o_ref.dtype)


def _in_proj(x2d, sc, sh, w, seq):
    t, d = x2d.shape
    nc = w.shape[1]
    tm = _pick(seq, (512, 256, 128))
    tn = _pick(nc, (1024, 512, 256, 128))
    per_b = seq // tm
    vec = pl.BlockSpec((1, 1, d), lambda i, j: (i // per_b, 0, 0))
    return pl.pallas_call(
        _in_proj_kernel,
        grid=(t // tm, nc // tn),
        in_specs=[pl.BlockSpec((tm, d), lambda i, j: (i, 0)), vec, vec,
                  pl.BlockSpec(memory_space=pl.ANY)],
        out_specs=pl.BlockSpec((tm, tn), lambda i, j: (i, j)),
        out_shape=jax.ShapeDtypeStruct((t, nc), BF16),
        scratch_shapes=[pltpu.VMEM((nc // tn, d, tn), BF16), pltpu.VMEM((tm, d), BF16),
                        pltpu.VMEM((d, tn), F32), pltpu.SemaphoreType.DMA(())],
        compiler_params=_cparams(("arbitrary", "arbitrary")),
        name="in_proj",
    )(x2d, sc[:, None, :], sh[:, None, :], w)


def _rope_kernel(q_ref, k_ref, v_ref, pos_ref, inv_ref, qo_ref, ko_ref, vo_ref):
    tm = pos_ref.shape[0]
    lane = lax.broadcasted_iota(jnp.int32, (tm, V7X_LANES), 1)
    lh = lane % ATT_HEAD_DIM
    half = ROT_DIM // 2
    ang = pos_ref[...] * inv_ref[...]
    cs = jnp.where(lh < ROT_DIM, jnp.cos(ang), 1.0)
    sn = jnp.sin(ang)
    sn = jnp.where(lh < half, -sn, jnp.where(lh < ROT_DIM, sn, 0.0))

    def rope(t):
        swapped = jnp.where(lh < half,
                            pltpu.roll(t, V7X_LANES - half, axis=1),
                            pltpu.roll(t, half, axis=1))
        return t * cs + swapped * sn

    for h in range(qo_ref.shape[1]):
        lanes = slice(h * V7X_LANES, (h + 1) * V7X_LANES)
        qo_ref[0, h] = rope(q_ref[:, lanes].astype(F32)) * Q_SCALE
        ko_ref[0, h] = rope(k_ref[:, lanes].astype(F32))
        vo_ref[0, h] = v_ref[:, lanes].astype(F32)


def _qk_rope(proj, pos_b, inv_lane, bsz, seq, att_w):
    t = proj.shape[0]
    hp = att_w // V7X_LANES
    tm = _pick(seq, (512, 256, 128))
    per_b = seq // tm
    out_spec = pl.BlockSpec((1, hp, tm, V7X_LANES), lambda i: (i // per_b, 0, i % per_b, 0))
    shp = jax.ShapeDtypeStruct((bsz, hp, seq, V7X_LANES), F32)
    return pl.pallas_call(
        _rope_kernel,
        grid=(t // tm,),
        in_specs=[pl.BlockSpec((tm, att_w), lambda i: (i, 0)),
                  pl.BlockSpec((tm, att_w), lambda i: (i, 1)),
                  pl.BlockSpec((tm, att_w), lambda i: (i, 2)),
                  pl.BlockSpec((tm, V7X_LANES), lambda i: (i, 0)),
                  pl.BlockSpec((1, V7X_LANES), lambda i: (0, 0))],
        out_specs=[out_spec, out_spec, out_spec],
        out_shape=[shp, shp, shp],
        compiler_params=_cparams(("arbitrary",)),
        name="qk_rope",
    )(proj, proj, proj, pos_b, inv_lane)


ATT_TASKS_PER_STEP = 8


def _attn_kernel(q_in, k_in, v_in, out_ref, qs, ks, vs, sm, no_lo, nl_lo, no_s, nl_s, nat, *, seq):
    qn = ATT_BLOCK
    n_task = seq // qn
    n_head = V7X_LANES // ATT_HEAD_DIM
    gc = min(ATT_TASKS_PER_STEP, n_task)
    (_, d_lo), (_, dm), (_, d_hi) = DILATED_PAIRS
    g_hi = d_hi // dm
    rows_m = seq // dm
    lane = lax.broadcasted_iota(jnp.int32, (qn, V7X_LANES), 1)
    qi = lax.broadcasted_iota(jnp.int32, (gc, qn, 2 * qn), 1)
    kj = lax.broadcasted_iota(jnp.int32, (gc, qn, 2 * qn), 2)
    dist = qi + qn - kj
    band = (dist >= 0) & (dist <= qn)
    zeros = jnp.zeros((qn, V7X_LANES), BF16)
    srcs = (q_in, k_in, v_in)

    for a, src in enumerate(srcs):
        for r in range(dm):
            sm[a, r] = src[0, 0, pl.ds(r, rows_m, stride=dm), :]

    def task_index(d, r, n):
        if d == d_lo:
            return None, slice(n * qn, (n + 1) * qn)
        if d == dm:
            return r, slice(n * qn, (n + 1) * qn)
        return r % dm, pl.ds(r // dm + n * qn * g_hi, qn, stride=g_hi)

    for bi, (window, d) in enumerate(DILATED_PAIRS):
        nb = seq // (d * qn)
        tasks = [(r, n) for r in range(d) for n in range(nb)]
        for g, (r, n) in enumerate(tasks):
            sel, rows = task_index(d, r, n)

            def operand(a, sel=sel, rows=rows):
                return srcs[a][0, 0, rows, :] if sel is None else sm[a, sel, rows, :]

            qv = operand(0)
            for h in range(n_head):
                qs[h, g] = jnp.where(lane // ATT_HEAD_DIM == h, qv, 0.0).astype(BF16)
            for a, dst in ((1, ks), (2, vs)):
                blk = operand(a).astype(BF16)
                dst[g, qn:, :] = blk
                if n + 1 < nb:
                    dst[g + 1, :qn, :] = blk
                if n == 0:
                    dst[g, :qn, :] = zeros

        for g0 in range(0, n_task, gc):
            gsl = slice(g0, g0 + gc)
            gidx = g0 + lax.broadcasted_iota(jnp.int32, (gc, qn, 2 * qn), 0)
            valid = band & ((gidx % nb != 0) | (kj >= qn))
            k = ks[gsl]
            v = vs[gsl]
            o = None
            for h in range(n_head):
                s = jnp.einsum("gqd,gkd->gqk", qs[h, gsl], k, preferred_element_type=F32)
                s = jnp.where(valid, s, NEG_INF)
                m = jnp.max(s, axis=-1, keepdims=True)
                p = jnp.exp2(s - m)
                den = jnp.sum(p, axis=-1, keepdims=True)
                oh = jnp.einsum("gqk,gkd->gqd", p.astype(BF16), v,
                                preferred_element_type=F32) / den
                lh = jnp.broadcast_to(m * LN_2 + jnp.log(den), oh.shape)
                if o is None:
                    o, lse = oh, lh
                else:
                    in_head = lane[None] // ATT_HEAD_DIM == h
                    o, lse = jnp.where(in_head, oh, o), jnp.where(in_head, lh, lse)
            for t in range(gc):
                sel, rows = task_index(d, *tasks[g0 + t])
                if sel is None:
                    no_lo[rows, :] = o[t]
                    nl_lo[rows, :] = lse[t]
                else:
                    no_s[bi - 1, sel, rows, :] = o[t]
                    nl_s[bi - 1, sel, rows, :] = lse[t]

    for r in range(dm):
        for c in range(rows_m // qn):
            rows = slice(c * qn, (c + 1) * qn)
            seq_rows = pl.ds(r + dm * c * qn, qn, stride=dm)
            ls = [nl_lo[seq_rows, :], nl_s[0, r, rows, :], nl_s[1, r, rows, :]]
            os_ = [no_lo[seq_rows, :], no_s[0, r, rows, :], no_s[1, r, rows, :]]
            mx = functools.reduce(jnp.maximum, ls)
            ws = [jnp.exp(l - mx) for l in ls]
            tot = functools.reduce(lambda x, y: x + y, ws)
            acc = functools.reduce(lambda x, y: x + y, [w * o for w, o in zip(ws, os_)])
            nat[seq_rows, :] = acc / tot
    out_ref[0, 0] = nat[...].astype(out_ref.dtype)


def _attention(q_hp, k_hp, v_hp):
    bsz, hp, seq, _ = q_hp.shape
    (_, d_lo), (_, dm), (_, d_hi) = DILATED_PAIRS
    assert d_lo == 1 and d_hi % dm == 0
    for window, d in DILATED_PAIRS:
        assert window // d == ATT_BLOCK and seq % (d * ATT_BLOCK) == 0
    n_task = seq // ATT_BLOCK
    assert n_task % min(ATT_TASKS_PER_STEP, n_task) == 0
    n_head = V7X_LANES // ATT_HEAD_DIM
    streams = (dm, seq // dm, V7X_LANES)
    spec = pl.BlockSpec((1, 1, seq, V7X_LANES), lambda b, h: (b, h, 0, 0))
    return pl.pallas_call(
        functools.partial(_attn_kernel, seq=seq),
        grid=(bsz, hp),
        in_specs=[spec, spec, spec],
        out_specs=spec,
        out_shape=jax.ShapeDtypeStruct((bsz, hp, seq, V7X_LANES), BF16),
        scratch_shapes=[pltpu.VMEM((n_head, n_task, ATT_BLOCK, V7X_LANES), BF16),
                        pltpu.VMEM((n_task, 2 * ATT_BLOCK, V7X_LANES), BF16),
                        pltpu.VMEM((n_task, 2 * ATT_BLOCK, V7X_LANES), BF16),
                        pltpu.VMEM((3,) + streams, F32),
                        pltpu.VMEM((seq, V7X_LANES), F32), pltpu.VMEM((seq, V7X_LANES), F32),
                        pltpu.VMEM((2,) + streams, F32), pltpu.VMEM((2,) + streams, F32),
                        pltpu.VMEM((seq, V7X_LANES), F32)],
        compiler_params=_cparams(("arbitrary", "arbitrary")),
        name="dilated_attn",
    )(q_hp, k_hp, v_hp)


def _hgrn_kernel(q_ref, f_ref, i_ref, g_ref, lb_ref, gw_ref, sums_ref, o_ref,
                 st_ref, hl_ref, kk_ref, cum_ref, *, heads, ts):
    @pl.when(pl.program_id(2) == 0)
    def _():
        st_ref[...] = jnp.zeros_like(st_ref)

    c_len = HG_CHUNK
    lanes = heads * HG_EXPAND
    lb = lb_ref[...]
    for c in range(ts // c_len):
        rows = slice(c * c_len, (c + 1) * c_len)
        f = lb + (1.0 - lb) * jax.nn.sigmoid(f_ref[0, rows, :].astype(F32))
        kk_ref[rows, :] = 1.0 - f
        hi, lo = _bf16_split(jnp.log(f))
        hl_ref[rows, :lanes] = hi
        hl_ref[rows, lanes:] = lo
    both = jnp.dot(sums_ref[...], hl_ref[...], preferred_element_type=F32)
    cum_ref[...] = both[:, :lanes] + both[:, lanes:]

    n_sub = c_len // HG_SUB

    def sub_block_anchor(b):
        mids = [b[i * HG_SUB + HG_SUB // 2:i * HG_SUB + HG_SUB // 2 + 1, :] for i in range(n_sub)]
        return jnp.concatenate([jnp.broadcast_to(m, (HG_SUB, m.shape[1])) for m in mids], axis=0)

    gi = lax.broadcasted_iota(jnp.int32, (heads * n_sub, HG_SUB, c_len), 0) % n_sub
    qi = lax.broadcasted_iota(jnp.int32, (heads * n_sub, HG_SUB, c_len), 1)
    si = lax.broadcasted_iota(jnp.int32, (heads * n_sub, HG_SUB, c_len), 2)
    causal = si <= gi * HG_SUB + qi
    key_row = lax.broadcasted_iota(jnp.int32, (c_len, HG_EXPAND), 0)

    for c in range(ts // c_len):
        rows = slice(c * c_len, (c + 1) * c_len)
        qts, kts, vbs = [], [], []
        for h in range(heads):
            lsl = slice(h * HG_EXPAND, (h + 1) * HG_EXPAND)
            b = cum_ref[rows, lsl]
            anchor = sub_block_anchor(b)
            kk = kk_ref[rows, lsl]
            qt = (q_ref[0, rows, lsl].astype(F32) * jnp.exp(b - anchor)).astype(BF16)
            qts.append(qt.reshape(n_sub, HG_SUB, HG_EXPAND))
            vb = i_ref[0, rows, lsl].astype(BF16)
            for i in range(n_sub):
                hi_r = (i + 1) * HG_SUB
                kt = kk * jnp.exp(anchor[i * HG_SUB:i * HG_SUB + 1, :] - b)
                kts.append(jnp.where(key_row < hi_r, kt, 0.0).astype(BF16))
                vbs.append(vb)
        a = jnp.einsum("gqk,gsk->gqs", jnp.concatenate(qts, axis=0), jnp.stack(kts),
                       preferred_element_type=F32)
        a = jnp.where(causal, a, 0.0).astype(BF16)
        o_intra = jnp.einsum("gqs,gsv->gqv", a, jnp.stack(vbs), preferred_element_type=F32)

        for h in range(heads):
            lsl = slice(h * HG_EXPAND, (h + 1) * HG_EXPAND)
            b = cum_ref[rows, lsl]
            b_last = jnp.broadcast_to(b[c_len - 1:c_len, :], b.shape)
            kk = kk_ref[rows, lsl]
            q = q_ref[0, rows, lsl].astype(F32)
            v = i_ref[0, rows, lsl].astype(F32)
            st = st_ref[h]
            o_inter = lax.dot_general((q * jnp.exp(b)).astype(BF16), st.astype(BF16), NT_DIMS,
                                      preferred_element_type=F32)
            o = o_inter + o_intra[h * n_sub:(h + 1) * n_sub].reshape(c_len, HG_EXPAND)
            kl = kk * jnp.exp(b_last - b)
            upd = jnp.dot(v.T.astype(BF16), kl.astype(BF16), preferred_element_type=F32)
            st_ref[h] = st * jnp.exp(b_last[0:1, :]) + upd
            o = o * lax.rsqrt(jnp.mean(o * o, axis=-1, keepdims=True) + RMS_EPS)
            g = g_ref[0, rows, lsl].astype(F32)
            o = o * gw_ref[:, lsl] * (g * jax.nn.sigmoid(g))
            o_ref[0, rows, lsl] = o.astype(o_ref.dtype)


def _hgrn_cumsum_matrix(ts):
    t = np.arange(ts)[:, None]
    s = np.arange(ts)[None, :]
    return (((t // HG_CHUNK) == (s // HG_CHUNK)) & (s <= t)).astype(np.float32)


def _hgrn2(proj3, lb, gw, att_w, hg_w):
    bsz, seq, _ = proj3.shape
    n_heads = hg_w // HG_EXPAND
    heads = _pick(n_heads, (8, 4, 2, 1))
    lanes = heads * HG_EXPAND
    ts = _pick(seq, (256, 128, 64))
    base = 3 * att_w
    assert base % lanes == 0 and hg_w % lanes == 0

    def seg(k):
        off = (base + k * hg_w) // lanes
        return pl.BlockSpec((1, ts, lanes), lambda b, h, s: (b, s, off + h))

    vec = pl.BlockSpec((1, lanes), lambda b, h, s: (0, h))
    sums = jnp.asarray(_hgrn_cumsum_matrix(ts), BF16)
    return pl.pallas_call(
        functools.partial(_hgrn_kernel, heads=heads, ts=ts),
        grid=(bsz, hg_w // lanes, seq // ts),
        in_specs=[seg(0), seg(1), seg(2), seg(3), vec, vec,
                  pl.BlockSpec((ts, ts), lambda b, h, s: (0, 0))],
        out_specs=pl.BlockSpec((1, ts, lanes), lambda b, h, s: (b, s, h)),
        out_shape=jax.ShapeDtypeStruct((bsz, seq, hg_w), BF16),
        scratch_shapes=[pltpu.VMEM((heads, HG_EXPAND, HG_EXPAND), F32),
                        pltpu.VMEM((ts, 2 * lanes), BF16),
                        pltpu.VMEM((ts, lanes), F32),
                        pltpu.VMEM((ts, lanes), F32)],
        compiler_params=_cparams(("arbitrary", "arbitrary", "arbitrary")),
        name="hgrn2",
    )(proj3, proj3, proj3, proj3, lb.reshape(1, hg_w), gw.reshape(1, hg_w), sums)


def _layer_norm(y, g, b):
    mu = jnp.mean(y, axis=-1, keepdims=True)
    yc = y - mu
    var = jnp.mean(yc * yc, axis=-1, keepdims=True)
    return yc * lax.rsqrt(var + LN_EPS) * g + b


def _bf16_split(x):
    hi = x.astype(BF16)
    lo = (x - hi.astype(F32)).astype(BF16)
    return hi, lo


def _pack_bf16_pair(lo, hi):
    lo_bits = lax.bitcast_convert_type(lo.astype(BF16).astype(F32), jnp.uint32)
    hi_bits = lax.bitcast_convert_type(hi.astype(BF16).astype(F32), jnp.uint32)
    return (lo_bits >> 16) | (hi_bits & jnp.uint32(0xFFFF0000))


def _store_token_rows(ref, tok0, words):
    n, width = words.shape
    nblk = width // V7X_LANES
    for c in range(nblk):
        ref[pl.ds(tok0 * nblk + c, n, stride=nblk), :] = words[:, c * V7X_LANES:(c + 1) * V7X_LANES]


def _load_token_rows(ref, n, nblk, tok0=0):
    return jnp.concatenate([ref[pl.ds(tok0 * nblk + c, n, stride=nblk), :] for c in range(nblk)],
                           axis=1)


def _unpack_bf16_pair(words):
    lo = lax.bitcast_convert_type(words << 16, F32)
    hi = lax.bitcast_convert_type(words & jnp.uint32(0xFFFF0000), F32)
    return lo, hi


def _load_weight_bf16(w_hbm, wbf_ref, stage_ref, sem):
    rows = stage_ref.shape[1]
    n_slab = w_hbm.shape[0] // rows

    def fetch(s, slot):
        return pltpu.make_async_copy(w_hbm.at[pl.ds(s * rows, rows)], stage_ref.at[slot],
                                     sem.at[slot])

    fetch(0, 0).start()
    for s in range(n_slab):
        if s + 1 < n_slab:
            fetch(s + 1, (s + 1) % 2).start()
        fetch(s, s % 2).wait()
        wbf_ref[s * rows:(s + 1) * rows, :] = stage_ref[s % 2].astype(BF16)


def _out_proj_kernel(att_ref, rec_ref, x_ref, wo_hbm, gt_ref, sc_ref, sh_ref, g_ref, b_ref,
                     rw_ref, rb_ref, x1_ref, h2_ref, lg_ref, wobf_ref, rwhi_ref, rwlo_ref,
                     stage_ref, hi_ref, lo_ref, mix_a, mix_b, sem, *, alpha, att_w):
    i = pl.program_id(0)

    @pl.when(i == 0)
    def _():
        _load_weight_bf16(wo_hbm, wobf_ref, stage_ref, sem)
        hi, lo = _bf16_split(rw_ref[...])
        rwhi_ref[...] = hi
        rwlo_ref[...] = lo
        mix_b[...] = jnp.zeros_like(mix_b)

    def step(mix_out, mix_in):
        att = jnp.concatenate([att_ref[0, h] for h in range(att_ref.shape[1])], axis=1)
        mix_out[...] = (jnp.dot(att, wobf_ref[:att_w, :], preferred_element_type=F32)
                        + jnp.dot(rec_ref[...], wobf_ref[att_w:, :], preferred_element_type=F32))
        half = x_ref.shape[1] // 2
        for r in range(x_ref.shape[0] // LN_ROW_GROUP):
            rows = slice(r * LN_ROW_GROUP, (r + 1) * LN_ROW_GROUP)
            y = alpha * x_ref[rows, :] + (1.0 + gt_ref[0]) * mix_in[rows, :]
            x1 = _layer_norm(y, g_ref[...], b_ref[...])
            x1_ref[rows, :] = x1
            h2 = x1 * (1.0 + sc_ref[0]) + sh_ref[0]
            hi, lo = _bf16_split(h2)
            hi_ref[rows, :] = hi
            lo_ref[rows, :] = lo
            _store_token_rows(h2_ref, r * LN_ROW_GROUP,
                              _pack_bf16_pair(h2[:, :half], h2[:, half:]))
        lg = (jnp.dot(hi_ref[...], rwhi_ref[...], preferred_element_type=F32)
              + jnp.dot(lo_ref[...], rwhi_ref[...], preferred_element_type=F32)
              + jnp.dot(hi_ref[...], rwlo_ref[...], preferred_element_type=F32))
        lg_ref[...] = lg.T[:lg_ref.shape[0], :] + rb_ref[...]

    @pl.when(i % 2 == 0)
    def _():
        step(mix_a, mix_b)

    @pl.when(i % 2 == 1)
    def _():
        step(mix_b, mix_a)


OUT_PROJ_STAGE_ROWS = 256
LN_ROW_GROUP = 16


def _out_proj(att, rec, x2d, wo, gt, sc, sh, g, b, rw, rb, seq, alpha):
    t, d = x2d.shape
    hp = att.shape[1]
    att_w = hp * V7X_LANES
    n_e = rw.shape[1]
    assert n_e <= V7X_LANES and d % OUT_PROJ_STAGE_ROWS == 0
    rw_pad = jnp.zeros((d, V7X_LANES), F32).at[:, :n_e].set(rw)
    nblk = d // 2 // V7X_LANES
    tm = _pick(seq, (512, 256, 128))
    per_b = seq // tm
    n_tile = t // tm
    mm = lambda i: jnp.minimum(i, n_tile - 1)
    ep = lambda i: jnp.maximum(i - 1, 0)
    vec3 = pl.BlockSpec((1, 1, d), lambda i: (ep(i) // per_b, 0, 0))
    full = lambda shape: pl.BlockSpec(shape, lambda i: (0,) * len(shape))
    return pl.pallas_call(
        functools.partial(_out_proj_kernel, alpha=alpha, att_w=att_w),
        grid=(n_tile + 1,),
        in_specs=[pl.BlockSpec((1, hp, tm, V7X_LANES),
                               lambda i: (mm(i) // per_b, 0, mm(i) % per_b, 0)),
                  pl.BlockSpec((tm, d - att_w), lambda i: (mm(i), 0)),
                  pl.BlockSpec((tm, d), lambda i: (ep(i), 0)),
                  pl.BlockSpec(memory_space=pl.ANY), vec3, vec3, vec3, full((1, d)), full((1, d)),
                  full((d, V7X_LANES)), full((n_e, 1))],
        out_specs=[pl.BlockSpec((tm, d), lambda i: (ep(i), 0)),
                   pl.BlockSpec((tm * nblk, V7X_LANES), lambda i: (ep(i), 0)),
                   pl.BlockSpec((n_e, tm), lambda i: (0, ep(i)))],
        out_shape=[jax.ShapeDtypeStruct((t, d), F32),
                   jax.ShapeDtypeStruct((t * nblk, V7X_LANES), jnp.uint32),
                   jax.ShapeDtypeStruct((n_e, t), F32)],
        scratch_shapes=[pltpu.VMEM((d, d), BF16), pltpu.VMEM((d, V7X_LANES), BF16),
                        pltpu.VMEM((d, V7X_LANES), BF16),
                        pltpu.VMEM((2, OUT_PROJ_STAGE_ROWS, d), F32),
                        pltpu.VMEM((tm, d), BF16), pltpu.VMEM((tm, d), BF16),
                        pltpu.VMEM((tm, d), F32), pltpu.VMEM((tm, d), F32),
                        pltpu.SemaphoreType.DMA((2,))],
        compiler_params=_cparams(("arbitrary",)),
        name="out_proj_ln1",
    )(att, rec, x2d, wo, gt[:, None, :], sc[:, None, :], sh[:, None, :],
      g.reshape(1, d), b.reshape(1, d), rw_pad, rb.reshape(n_e, 1))


def _routing_kernel(lg_ref, tri_ref, low_ref, gate_ref, dest_ref, tile_ref, pad_ref, sel_scr, rk_scr,
                    *, blk, tm):
    n_e, t = lg_ref.shape
    eidx = lax.broadcasted_iota(jnp.int32, (n_e, t), 0)
    cur = lg_ref[...]
    vals, idxs = [], []
    for _ in range(TOP_K):
        m = jnp.max(cur, axis=0, keepdims=True)
        ik = jnp.min(jnp.where(cur == m, eidx, n_e), axis=0, keepdims=True)
        cur = jnp.where(eidx == ik, -jnp.inf, cur)
        vals.append(m)
        idxs.append(ik)
    es = [jnp.exp(v - vals[0]) for v in vals]
    den = functools.reduce(lambda a, b: a + b, es)
    for k in range(TOP_K):
        gate_ref[k:k + 1, :] = es[k] / den
    sel = functools.reduce(lambda a, b: a | b, [eidx == ik for ik in idxs])
    sel_scr[...] = jnp.where(sel, 1.0, 0.0)

    tri = tri_ref[...]
    carry = jnp.zeros((n_e, 1), F32)
    for j in range(t // blk):
        sb = sel_scr[:, j * blk:(j + 1) * blk]
        pre = jnp.dot(sb.astype(BF16), tri, preferred_element_type=F32)
        rk_scr[:, j * blk:(j + 1) * blk] = pre + carry
        carry = carry + jnp.sum(sb, axis=1, keepdims=True)

    counts = jnp.broadcast_to(carry, (n_e, V7X_LANES))
    padded = jnp.floor((counts + (tm - 1)) * (1.0 / tm)) * tm
    pends = jnp.dot(low_ref[...], padded, precision=lax.Precision.HIGHEST,
                    preferred_element_type=F32)
    pstarts = pends - padded
    row0 = rk_scr[...] + pstarts[:, 0:1]
    for k in range(TOP_K):
        dest_ref[k:k + 1, :] = jnp.sum(jnp.where(eidx == idxs[k], row0, 0.0), axis=0,
                                       keepdims=True).astype(jnp.int32)
    starts = (lax.broadcasted_iota(jnp.int32, (n_e, tile_ref.shape[1]), 1) * tm).astype(F32)
    owner = jnp.sum(jnp.where(pends[:, 0:1] <= starts, 1.0, 0.0), axis=0, keepdims=True)
    tile_ref[0:1, :] = jnp.minimum(owner, n_e - 1.0).astype(jnp.int32)
    tile_ref[1:2, :] = jnp.broadcast_to(jnp.max(pends[:, 0:1], axis=0, keepdims=True),
                                        (1, tile_ref.shape[1])).astype(jnp.int32)
    e_tile = lax.broadcasted_iota(jnp.int32, starts.shape, 0).astype(F32)
    row_end = jnp.sum(jnp.where(e_tile == owner, (pstarts + counts)[:, 0:1], 0.0), axis=0,
                      keepdims=True)
    tile_ref[2:3, :] = jnp.clip(row_end - starts[0:1, :], 0.0, float(tm)).astype(jnp.int32)
    pad_ref[0] = (pstarts + counts).astype(jnp.int32)
    pad_ref[1] = (padded - counts).astype(jnp.int32)


def _routing(logits_t, tm, n_tiles):
    n_e, t = logits_t.shape
    assert tm & (tm - 1) == 0
    blk = _pick(t, (256, 128))
    tri = jnp.asarray(np.triu(np.ones((blk, blk), np.float32), 1), BF16)
    low = jnp.asarray(np.tril(np.ones((n_e, n_e), np.float32)))
    ntp = -(-n_tiles // V7X_LANES) * V7X_LANES
    full = lambda shape: pl.BlockSpec(shape, lambda: (0,) * len(shape))
    return pl.pallas_call(
        functools.partial(_routing_kernel, blk=blk, tm=tm),
        in_specs=[full((n_e, t)), full((blk, blk)), full((n_e, n_e))],
        out_specs=[full((TOP_K, t)), full((TOP_K, t)), full((3, ntp)), full((2, n_e, V7X_LANES))],
        out_shape=[jax.ShapeDtypeStruct((TOP_K, t), F32),
                   jax.ShapeDtypeStruct((TOP_K, t), jnp.int32),
                   jax.ShapeDtypeStruct((3, ntp), jnp.int32),
                   jax.ShapeDtypeStruct((2, n_e, V7X_LANES), jnp.int32)],
        scratch_shapes=[pltpu.VMEM((n_e, t), F32), pltpu.VMEM((n_e, t), F32)],
        compiler_params=pltpu.CompilerParams(vmem_limit_bytes=V7X_VMEM_LIMIT),
        name="routing",
    )(logits_t, tri, low)


def _dispatch_kernel(dest_ref, pad_off_ref, pad_n_ref, nrow_ref, h_ref, o_ref, zero_ref, sem, zsem,
                     *, n_tok, tq, tm, n_e, nblk):
    step = pl.program_id(0)
    base = step * tq

    def rows(tok, n=1):
        return pl.ds(pl.multiple_of(tok * nblk, nblk), n * nblk)

    def fill(off, n):
        return pltpu.make_async_copy(zero_ref.at[rows(0, n)], o_ref.at[rows(off, n)], zsem)

    pieces = [1 << s for s in range(tm.bit_length() - 2, -1, -1)]

    def pad_rows(wait):
        def per_expert(e, c):
            off = pad_off_ref[e]
            n = pad_n_ref[e]
            for p in pieces:
                hit = (n & p) != 0

                @pl.when(hit)
                def _(off=off, p=p):
                    cp = fill(off, p)
                    cp.wait() if wait else cp.start()

                off = off + jnp.where(hit, p, 0)
            return c
        lax.fori_loop(0, n_e, per_expert, 0)

        def per_tile(i, c):
            cp = fill(i * tm, tm)
            cp.wait() if wait else cp.start()
            return c
        lax.fori_loop(nrow_ref[0] // tm, o_ref.shape[0] // (tm * nblk), per_tile, 0)

    @pl.when(step == 0)
    def _():
        zero_ref[...] = jnp.zeros_like(zero_ref)
        pad_rows(False)

    def start(r, c):
        for k in range(TOP_K):
            dst = o_ref.at[rows(dest_ref[k * n_tok + base + r])]
            pltpu.make_async_copy(h_ref.at[rows(r)], dst, sem).start(priority=k % 2)
        return c

    lax.fori_loop(0, tq, start, 0, unroll=4)
    for _ in range(TOP_K):
        pltpu.make_async_copy(h_ref, h_ref, sem).wait()

    @pl.when(step == 0)
    def _():
        pad_rows(True)


def _dispatch(h2p, dest_flat, pad_off, pad_n, n_used_rows, n_rows, tm, nblk):
    t = h2p.shape[0] // nblk
    n_e = pad_off.shape[0]
    tq = _pick(t, (256, 128))
    return pl.pallas_call(
        functools.partial(_dispatch_kernel, n_tok=t, tq=tq, tm=tm, n_e=n_e, nblk=nblk),
        grid_spec=pltpu.PrefetchScalarGridSpec(
            num_scalar_prefetch=4, grid=(t // tq,),
            in_specs=[pl.BlockSpec((tq * nblk, V7X_LANES), lambda i, *_: (i, 0))],
            out_specs=pl.BlockSpec(memory_space=pl.ANY),
            scratch_shapes=[pltpu.VMEM((tm * nblk, V7X_LANES), h2p.dtype),
                            pltpu.SemaphoreType.DMA(()), pltpu.SemaphoreType.DMA(())]),
        out_shape=jax.ShapeDtypeStruct((n_rows * nblk, V7X_LANES), h2p.dtype),
        compiler_params=pltpu.CompilerParams(dimension_semantics=("arbitrary",),
                                             has_side_effects=True,
                                             vmem_limit_bytes=V7X_VMEM_LIMIT),
        name="moe_dispatch",
    )(dest_flat, pad_off, pad_n, n_used_rows, h2p)


EXPERT_COL_BLOCK = 2048
GEMM1_COL_CHUNK = 256
MOE_TILE_LEVELS = 3
MOE_MIN_ROWS = 16
CAST_ROWS = 256


def _stream_expert_weights(te_ref, nt_ref, w_hbm, stage_ref, slot_ref, sem):
    j, i = pl.program_id(0), pl.program_id(1)
    n_pass, n_tiles = pl.num_programs(0), pl.num_programs(1)
    tn = stage_ref.shape[2]
    n_used = nt_ref[0]
    cur = te_ref[i]
    first = (i < n_used) & ((i == 0) | (cur != te_ref[jnp.maximum(i - 1, 0)]))

    def fetch(e, jj, slot):
        src = w_hbm.at[e, :, pl.ds(pl.multiple_of(jj * tn, tn), tn)]
        return pltpu.make_async_copy(src, stage_ref.at[slot], sem.at[slot])

    @pl.when((i == 0) & (j == 0))
    def _():
        slot_ref[0] = 1
        fetch(te_ref[0], 0, 0).start()

    @pl.when(first)
    def _():
        slot = 1 - slot_ref[0]
        slot_ref[0] = slot
        fetch(cur, j, slot).wait()

        def same_group(k):
            return (k < n_used) & (te_ref[jnp.minimum(k, n_tiles - 1)] == cur)
        nxt = lax.while_loop(same_group, lambda k: k + 1, i + 1)

        @pl.when(nxt < n_used)
        def _():
            fetch(te_ref[jnp.minimum(nxt, n_tiles - 1)], j, 1 - slot).start()

        @pl.when((nxt >= n_used) & (j + 1 < n_pass))
        def _():
            fetch(te_ref[0], j + 1, 1 - slot).start()

    return first, slot_ref[0]


def _cast_block(stage_ref, slot, wbf_ref):
    def cast(r, c):
        rows = pl.ds(pl.multiple_of(r * CAST_ROWS, CAST_ROWS), CAST_ROWS)
        wbf_ref[rows, :] = stage_ref[slot, rows, :].astype(BF16)
        return c
    lax.fori_loop(0, wbf_ref.shape[0] // CAST_ROWS, cast, 0)


def _gemm1_kernel(te_ref, nt_ref, tv_ref, x_ref, w_hbm, b_ref, o_ref, stage_ref, wbf_ref,
                  slot_ref, sem, *, nc, nblk):
    first, slot = _stream_expert_weights(te_ref, nt_ref, w_hbm, stage_ref, slot_ref, sem)

    def compute(rows, convert):
        words = _load_token_rows(x_ref, rows, nblk)
        x_lo, x_hi = (v.astype(BF16) for v in _unpack_bf16_pair(words))
        half = x_lo.shape[1]
        lane128 = lax.broadcasted_iota(jnp.int32, (rows, V7X_LANES), 1)
        even_idx = (2 * lane128) % V7X_LANES
        for n0 in range(0, wbf_ref.shape[1], nc):
            if convert:
                wbf_ref[:, n0:n0 + nc] = stage_ref[slot, :, n0:n0 + nc].astype(BF16)
            hb = (jnp.dot(x_lo, wbf_ref[:half, n0:n0 + nc], preferred_element_type=F32)
                  + jnp.dot(x_hi, wbf_ref[half:, n0:n0 + nc], preferred_element_type=F32)
                  + b_ref[0, :, n0:n0 + nc])
            nxt = pltpu.roll(hb, nc - 1, axis=1)
            glu = jnp.minimum(hb, SWIGLU_LIMIT)
            lin = jnp.clip(nxt, -SWIGLU_LIMIT, SWIGLU_LIMIT)
            act = glu * jax.nn.sigmoid(SWIGLU_ALPHA * glu) * (lin + 1.0)
            for c in range(nc // (2 * V7X_LANES)):
                lo_blk = act[:, (2 * c) * V7X_LANES:(2 * c + 1) * V7X_LANES]
                hi_blk = act[:, (2 * c + 1) * V7X_LANES:(2 * c + 2) * V7X_LANES]
                out = jnp.where(lane128 < V7X_LANES // 2,
                                jnp.take_along_axis(lo_blk, even_idx, axis=1),
                                jnp.take_along_axis(hi_blk, even_idx, axis=1))
                o0 = n0 // 2 + c * V7X_LANES
                o_ref[:rows, o0:o0 + V7X_LANES] = out.astype(o_ref.dtype)
        if rows < o_ref.shape[0]:
            o_ref[rows:, :] = jnp.zeros((o_ref.shape[0] - rows, o_ref.shape[1]), o_ref.dtype)

    _for_real_rows(pl.program_id(1), nt_ref, tv_ref, o_ref, o_ref.shape[0], compute,
                   first, lambda: _cast_block(stage_ref, slot, wbf_ref))


def _for_real_rows(i, nt_ref, tv_ref, o_ref, tm, compute, first, convert_all):
    used = i < nt_ref[0]
    real = tv_ref[i]
    sizes = [tm >> s for s in range(MOE_TILE_LEVELS) if (tm >> s) % MOE_MIN_ROWS == 0]

    @pl.when(first & (real <= sizes[1]))
    def _():
        convert_all()

    for n, rows in enumerate(sizes):
        fits = real <= rows
        if n + 1 < len(sizes):
            fits = fits & (real > sizes[n + 1])
        if n == 0:
            @pl.when(used & fits & first)
            def _():
                compute(rows, True)
            fits = fits & jnp.logical_not(first)

        @pl.when(used & fits)
        def _(rows=rows):
            compute(rows, False)

    @pl.when(jnp.logical_not(used))
    def _():
        o_ref[...] = jnp.zeros_like(o_ref)


def _gemm1(xin, w1, b1, tile_e, n_tiles_used, tile_rows, tm, nblk):
    n_rows = xin.shape[0] // nblk
    n_e, d, f2 = w1.shape
    tn = min(EXPERT_COL_BLOCK, f2)
    nc = min(GEMM1_COL_CHUNK, tn)
    assert f2 % tn == 0 and d % CAST_ROWS == 0
    n_tiles = n_rows // tm
    assert nc % (2 * V7X_LANES) == 0

    def used(i, nt):
        return jnp.minimum(i, nt[0] - 1)

    return pl.pallas_call(
        functools.partial(_gemm1_kernel, nc=nc, nblk=nblk),
        grid_spec=pltpu.PrefetchScalarGridSpec(
            num_scalar_prefetch=3, grid=(f2 // tn, n_tiles),
            in_specs=[pl.BlockSpec((tm * nblk, V7X_LANES),
                                   lambda j, i, te, nt, tv: (used(i, nt), 0)),
                      pl.BlockSpec(memory_space=pl.ANY),
                      pl.BlockSpec((1, 1, tn), lambda j, i, te, nt, tv: (te[used(i, nt)], 0, j))],
            out_specs=pl.BlockSpec((tm, tn // 2), lambda j, i, te, nt, tv: (i, j)),
            scratch_shapes=[pltpu.VMEM((2, d, tn), F32), pltpu.VMEM((d, tn), BF16),
                            pltpu.SMEM((1,), jnp.int32), pltpu.SemaphoreType.DMA((2,))]),
        out_shape=jax.ShapeDtypeStruct((n_rows, f2 // 2), BF16),
        compiler_params=_cparams(("arbitrary", "arbitrary")),
        name="moe_gemm1",
    )(tile_e, n_tiles_used, tile_rows, xin, w1, b1.reshape(n_e, 1, f2))


def _gemm2_kernel(te_ref, nt_ref, tv_ref, a_ref, w_hbm, b_ref, o_ref, stage_ref, wbf_ref, slot_ref,
                  sem, *, nc, nblk):
    first, slot = _stream_expert_weights(te_ref, nt_ref, w_hbm, stage_ref, slot_ref, sem)
    tm = a_ref.shape[0]

    def compute(rows, convert):
        a = a_ref[:rows, :]
        half = wbf_ref.shape[1] // 2

        def cols(n0):
            if convert:
                wbf_ref[:, n0:n0 + nc] = stage_ref[slot, :, n0:n0 + nc].astype(BF16)
            return (jnp.dot(a, wbf_ref[:, n0:n0 + nc], preferred_element_type=F32)
                    + b_ref[0, :, n0:n0 + nc])

        for n0 in range(0, half, nc):
            words = _pack_bf16_pair(cols(n0), cols(half + n0))
            for c in range(nc // V7X_LANES):
                blk = n0 // V7X_LANES + c
                o_ref[pl.ds(blk, rows, stride=nblk), :] = words[:, c * V7X_LANES:(c + 1) * V7X_LANES]
        if rows < tm:
            o_ref[rows * nblk:, :] = jnp.zeros(((tm - rows) * nblk, o_ref.shape[1]), o_ref.dtype)

    _for_real_rows(pl.program_id(1), nt_ref, tv_ref, o_ref, tm, compute,
                   first, lambda: _cast_block(stage_ref, slot, wbf_ref))


def _gemm2(act, w2, b2, tile_e, n_tiles_used, tile_rows, tm):
    n_rows, f = act.shape
    n_e, _, d = w2.shape
    tn = d
    nc = min(512, tn // 2)
    nblk = d // 2 // V7X_LANES
    assert f % CAST_ROWS == 0
    n_tiles = n_rows // tm

    def used(i, nt):
        return jnp.minimum(i, nt[0] - 1)

    return pl.pallas_call(
        functools.partial(_gemm2_kernel, nc=nc, nblk=nblk),
        grid_spec=pltpu.PrefetchScalarGridSpec(
            num_scalar_prefetch=3, grid=(d // tn, n_tiles),
            in_specs=[pl.BlockSpec((tm, f), lambda j, i, te, nt, tv: (used(i, nt), 0)),
                      pl.BlockSpec(memory_space=pl.ANY),
                      pl.BlockSpec((1, 1, tn), lambda j, i, te, nt, tv: (te[used(i, nt)], 0, j))],
            out_specs=pl.BlockSpec((tm * nblk, V7X_LANES), lambda j, i, te, nt, tv: (i, 0)),
            scratch_shapes=[pltpu.VMEM((2, f, tn), F32), pltpu.VMEM((f, tn), BF16),
                            pltpu.SMEM((1,), jnp.int32), pltpu.SemaphoreType.DMA((2,))]),
        out_shape=jax.ShapeDtypeStruct((n_rows * nblk, V7X_LANES), jnp.uint32),
        compiler_params=_cparams(("arbitrary", "arbitrary")),
        name="moe_gemm2",
    )(tile_e, n_tiles_used, tile_rows, act, w2, b2.reshape(n_e, 1, d))


COMBINE_ROW_GROUP = 64


def _combine_kernel(dest_ref, x1_ref, gate_ref, gt_ref, g_ref, b_ref, y_ref, o_ref, buf, sem,
                    *, alpha, tc, n_tok, nblk):
    i = pl.program_id(0)
    n_steps = pl.num_programs(0)

    def rows(tok):
        return pl.ds(pl.multiple_of(tok * nblk, nblk), nblk)

    def copy(step, slot, k, r):
        src = y_ref.at[rows(dest_ref[k * n_tok + step * tc + r])]
        return pltpu.make_async_copy(src, buf.at[slot, k, rows(r)], sem.at[slot])

    def issue(step, slot):
        def body(r, c):
            for k in range(TOP_K):
                copy(step, slot, k, r).start(priority=k % 2)
            return c
        lax.fori_loop(0, tc, body, 0, unroll=2)

    def drain(slot):
        pltpu.make_async_copy(buf.at[slot], buf.at[slot], sem.at[slot]).wait()

    @pl.when(i == 0)
    def _():
        issue(0, 0)

    def step(slot, prefetch):
        drain(slot)
        for rg in range(tc // COMBINE_ROW_GROUP):
            r0 = rg * COMBINE_ROW_GROUP
            if prefetch:
                for r in range(r0, r0 + COMBINE_ROW_GROUP):
                    for k in range(TOP_K):
                        copy(i + 1, 1 - slot, k, r).start(priority=k % 2)
            rows = slice(r0, r0 + COMBINE_ROW_GROUP)
            gates = gate_ref[rows, :]
            ff_lo = ff_hi = None
            for k in range(TOP_K):
                lo, hi = _unpack_bf16_pair(
                    _load_token_rows(buf.at[slot, k], COMBINE_ROW_GROUP, nblk, tok0=r0))
                gk = gates[:, k:k + 1]
                ff_lo = gk * lo if ff_lo is None else ff_lo + gk * lo
                ff_hi = gk * hi if ff_hi is None else ff_hi + gk * hi
            ff = jnp.concatenate([ff_lo, ff_hi], axis=1)
            y = alpha * x1_ref[rows, :] + (1.0 + gt_ref[0]) * ff
            o_ref[rows, :] = _layer_norm(y, g_ref[...], b_ref[...])

    for slot in range(2):
        for prefetch in (True, False):
            @pl.when((i % 2 == slot) & ((i + 1 < n_steps) == prefetch))
            def _(slot=slot, prefetch=prefetch):
                step(slot, prefetch)


def _combine(dest_flat, x1, gates_tk, gt, g, b, y, seq, alpha):
    t, d = x1.shape
    tc = _pick(seq, (256, 128))
    per_b = seq // tc
    nblk = d // 2 // V7X_LANES
    return pl.pallas_call(
        functools.partial(_combine_kernel, alpha=alpha, tc=tc, n_tok=t, nblk=nblk),
        grid_spec=pltpu.PrefetchScalarGridSpec(
            num_scalar_prefetch=1, grid=(t // tc,),
            in_specs=[pl.BlockSpec((tc, d), lambda i, ds: (i, 0)),
                      pl.BlockSpec((tc, TOP_K), lambda i, ds: (i, 0)),
                      pl.BlockSpec((1, 1, d), lambda i, ds: (i // per_b, 0, 0)),
                      pl.BlockSpec((1, d), lambda i, ds: (0, 0)),
                      pl.BlockSpec((1, d), lambda i, ds: (0, 0)),
                      pl.BlockSpec(memory_space=pl.ANY)],
            out_specs=pl.BlockSpec((tc, d), lambda i, ds: (i, 0)),
            scratch_shapes=[pltpu.VMEM((2, TOP_K, tc * nblk, V7X_LANES), jnp.uint32),
                            pltpu.SemaphoreType.DMA((2,))]),
        out_shape=jax.ShapeDtypeStruct((t, d), F32),
        compiler_params=_cparams(("arbitrary",)),
        name="moe_combine_ln2",
    )(dest_flat, x1, gates_tk, gt[:, None, :], g.reshape(1, d), b.reshape(1, d), y)


MOE_ROW_TILE = 512


def _moe_ffn(h2p, logits_t, x1, gt_f, ln_g, ln_b, w1, b1, w2, b2, seq, alpha):
    tm = MOE_ROW_TILE
    n_e, t = logits_t.shape
    n_rows = -(-(TOP_K * t) // tm) * tm + n_e * tm
    n_tiles = n_rows // tm
    gate_t, dest, tiles, pads = _routing(logits_t, tm, n_tiles)
    dest = dest.reshape(-1)
    tile_e = tiles[0, :n_tiles]
    n_used_rows = tiles[1, :1]
    tile_rows = tiles[2, :n_tiles]
    n_used_tiles = n_used_rows // tm
    nblk = x1.shape[1] // 2 // V7X_LANES
    xin = _dispatch(h2p, dest, pads[0, :, 0], pads[1, :, 0], n_used_rows, n_rows, tm, nblk)
    act = _gemm1(xin, w1, b1, tile_e, n_used_tiles, tile_rows, tm, nblk)
    y = _gemm2(act, w2, b2, tile_e, n_used_tiles, tile_rows, tm)
    return _combine(dest, x1, gate_t.T, gt_f, ln_g, ln_b, y, seq, alpha)


def kernel(x, c, positions, w_ada, b_ada, w_in, hgrn_lb, gnorm_w, w_o, ln1_g, ln1_b,
           router_w, router_b, w1, b1, w2, b2, ln2_g, ln2_b):
    bsz, seq, d = x.shape
    depth = w_ada.shape[0]
    t = bsz * seq
    att_w = d // 2
    hg_w = d - att_w
    alpha = (2.0 * depth) ** 0.25

    lb_all = jnp.cumsum(jax.nn.softmax(hgrn_lb.astype(F32), axis=0), axis=0)
    inv = ROPE_THETA ** (-(jnp.arange(0, ROT_DIM, 2, dtype=F32) / ROT_DIM))
    lane = np.arange(V7X_LANES)
    inv_lane = inv[(lane % ATT_HEAD_DIM) % (ROT_DIM // 2)].reshape(1, V7X_LANES)
    pos_b = jnp.broadcast_to(positions.astype(F32).reshape(t, 1), (t, V7X_LANES))

    x2d = x.reshape(t, d)
    for l in range(depth):
        mod = _adaln(c, w_ada[l], b_ada[l])
        sh_a, sc_a, gt_a, sh_f, sc_f, gt_f = jnp.split(mod, 6, axis=-1)

        proj = _in_proj(x2d, sc_a, sh_a, w_in[l], seq)
        proj3 = proj.reshape(bsz, seq, proj.shape[1])
        q_hp, k_hp, v_hp = _qk_rope(proj, pos_b, inv_lane, bsz, seq, att_w)
        att = _attention(q_hp, k_hp, v_hp)
        rec = _hgrn2(proj3, lb_all[l], gnorm_w[l], att_w, hg_w).reshape(t, hg_w)

        x1, h2p, logits_t = _out_proj(att, rec, x2d, w_o[l], gt_a, sc_f, sh_f, ln1_g[l], ln1_b[l],
                                      router_w[l], router_b[l], seq, alpha)

        x2d = _moe_ffn(h2p, logits_t, x1, gt_f, ln2_g[l], ln2_b[l], w1[l], b1[l], w2[l], b2[l],
                       seq, alpha)
    return x2d.reshape(bsz, seq, d)
```

```python
import functools

import numpy as np
import jax
import jax.numpy as jnp
from jax import lax
from jax.experimental import pallas as pl
from jax.experimental.pallas import tpu as pltpu

F32 = jnp.float32
BF16 = jnp.bfloat16

V7X_LANES = 128
V7X_VMEM_LIMIT = 56 * 1024 * 1024

ATT_HEAD_DIM = 64
DILATED_PAIRS = ((128, 1), (512, 4), (2048, 16))
ATT_BLOCK = 128
ROT_DIM = ATT_HEAD_DIM // 4
ROPE_THETA = 500000.0
HG_EXPAND = 128
HG_CHUNK = 64
HG_SUB = 16
TOP_K = 4
SWIGLU_ALPHA = 1.702
SWIGLU_LIMIT = 7.0
LN_EPS = 1e-5
RMS_EPS = 1e-6
NEG_INF = -1e30

LN_2 = float(np.log(2.0))
Q_SCALE = ATT_HEAD_DIM ** -0.5 / LN_2

NT_DIMS = (((1,), (1,)), ((), ()))


def _pick(n, candidates):
    for c in candidates:
        if n % c == 0:
            return c
    raise ValueError(f"no tile in {candidates} divides {n}")


def _cparams(sem):
    return pltpu.CompilerParams(dimension_semantics=sem, vmem_limit_bytes=V7X_VMEM_LIMIT)


def _adaln_kernel(c_ref, w_ref, b_ref, o_ref):
    c = c_ref[...]
    s = c * jax.nn.sigmoid(c)
    o_ref[...] = jnp.dot(s.astype(BF16), w_ref[...].astype(BF16),
                         preferred_element_type=F32) + b_ref[...]


def _adaln(c, w, b):
    bsz, d = c.shape
    n = w.shape[1]
    rows = 8
    cp = jnp.zeros((rows, d), F32).at[:bsz].set(c)
    tn = _pick(n, (1024, 512, 256, 128))
    out = pl.pallas_call(
        _adaln_kernel,
        grid=(n // tn,),
        in_specs=[pl.BlockSpec((rows, d), lambda j: (0, 0)),
                  pl.BlockSpec((d, tn), lambda j: (0, j)),
                  pl.BlockSpec((1, tn), lambda j: (0, j))],
        out_specs=pl.BlockSpec((rows, tn), lambda j: (0, j)),
        out_shape=jax.ShapeDtypeStruct((rows, n), F32),
        compiler_params=_cparams(("arbitrary",)),
        name="adaln",
    )(cp, w, b.reshape(1, n))
    return out[:bsz]


def _in_proj_kernel(x_ref, sc_ref, sh_ref, w_hbm, o_ref, wbf_ref, h_ref, stage_ref, sem):
    m, n = pl.program_id(0), pl.program_id(1)
    n_col = pl.num_programs(1)
    tn = o_ref.shape[1]

    def fetch(nn):
        src = w_hbm.at[:, pl.ds(pl.multiple_of(nn * tn, tn), tn)]
        return pltpu.make_async_copy(src, stage_ref, sem)

    @pl.when((m == 0) & (n == 0))
    def _():
        fetch(0).start()

    @pl.when(m == 0)
    def _():
        fetch(n).wait()
        wbf_ref[n] = stage_ref[...].astype(BF16)

        @pl.when(n + 1 < n_col)
        def _():
            fetch(n + 1).start()

    @pl.when(n == 0)
    def _():
        h_ref[...] = (x_ref[...] * (1.0 + sc_ref[0]) + sh_ref[0]).astype(BF16)

    o_ref[...] = jnp.dot(h_ref[...], wbf_ref[n], preferred_element_type=F32).astype(o_ref.dtype)


def _in_proj(x2d, sc, sh, w, seq):
    t, d = x2d.shape
    nc = w.shape[1]
    tm = _pick(seq, (512, 256, 128))
    tn = _pick(nc, (1024, 512, 256, 128))
    per_b = seq // tm
    vec = pl.BlockSpec((1, 1, d), lambda i, j: (i // per_b, 0, 0))
    return pl.pallas_call(
        _in_proj_kernel,
        grid=(t // tm, nc // tn),
        in_specs=[pl.BlockSpec((tm, d), lambda i, j: (i, 0)), vec, vec,
                  pl.BlockSpec(memory_space=pl.ANY)],
        out_specs=pl.BlockSpec((tm, tn), lambda i, j: (i, j)),
        out_shape=jax.ShapeDtypeStruct((t, nc), BF16),
        scratch_shapes=[pltpu.VMEM((nc // tn, d, tn), BF16), pltpu.VMEM((tm, d), BF16),
                        pltpu.VMEM((d, tn), F32), pltpu.SemaphoreType.DMA(())],
        compiler_params=_cparams(("arbitrary", "arbitrary")),
        name="in_proj",
    )(x2d, sc[:, None, :], sh[:, None, :], w)


def _rope_kernel(q_ref, k_ref, v_ref, pos_ref, inv_ref, qo_ref, ko_ref, vo_ref):
    tm = pos_ref.shape[0]
    lane = lax.broadcasted_iota(jnp.int32, (tm, V7X_LANES), 1)
    lh = lane % ATT_HEAD_DIM
    half = ROT_DIM // 2
    ang = pos_ref[...] * inv_ref[...]
    cs = jnp.where(lh < ROT_DIM, jnp.cos(ang), 1.0)
    sn = jnp.sin(ang)
    sn = jnp.where(lh < half, -sn, jnp.where(lh < ROT_DIM, sn, 0.0))

    def rope(t):
        swapped = jnp.where(lh < half,
                            pltpu.roll(t, V7X_LANES - half, axis=1),
                            pltpu.roll(t, half, axis=1))
        return t * cs + swapped * sn

    for h in range(qo_ref.shape[1]):
        lanes = slice(h * V7X_LANES, (h + 1) * V7X_LANES)
        qo_ref[0, h] = rope(q_ref[:, lanes].astype(F32)) * Q_SCALE
        ko_ref[0, h] = rope(k_ref[:, lanes].astype(F32))
        vo_ref[0, h] = v_ref[:, lanes].astype(F32)


def _qk_rope(proj, pos_b, inv_lane, bsz, seq, att_w):
    t = proj.shape[0]
    hp = att_w // V7X_LANES
    tm = _pick(seq, (512, 256, 128))
    per_b = seq // tm
    out_spec = pl.BlockSpec((1, hp, tm, V7X_LANES), lambda i: (i // per_b, 0, i % per_b, 0))
    shp = jax.ShapeDtypeStruct((bsz, hp, seq, V7X_LANES), F32)
    return pl.pallas_call(
        _rope_kernel,
        grid=(t // tm,),
        in_specs=[pl.BlockSpec((tm, att_w), lambda i: (i, 0)),
                  pl.BlockSpec((tm, att_w), lambda i: (i, 1)),
                  pl.BlockSpec((tm, att_w), lambda i: (i, 2)),
                  pl.BlockSpec((tm, V7X_LANES), lambda i: (i, 0)),
                  pl.BlockSpec((1, V7X_LANES), lambda i: (0, 0))],
        out_specs=[out_spec, out_spec, out_spec],
        out_shape=[shp, shp, shp],
        compiler_params=_cparams(("arbitrary",)),
        name="qk_rope",
    )(proj, proj, proj, pos_b, inv_lane)


ATT_TASKS_PER_STEP = 8


def _attn_kernel(q_in, k_in, v_in, out_ref, qs, ks, vs, sm, no_lo, nl_lo, no_s, nl_s, nat, *, seq):
    qn = ATT_BLOCK
    n_task = seq // qn
    n_head = V7X_LANES // ATT_HEAD_DIM
    gc = min(ATT_TASKS_PER_STEP, n_task)
    (_, d_lo), (_, dm), (_, d_hi) = DILATED_PAIRS
    g_hi = d_hi // dm
    rows_m = seq // dm
    lane = lax.broadcasted_iota(jnp.int32, (qn, V7X_LANES), 1)
    qi = lax.broadcasted_iota(jnp.int32, (gc, qn, 2 * qn), 1)
    kj = lax.broadcasted_iota(jnp.int32, (gc, qn, 2 * qn), 2)
    dist = qi + qn - kj
    band = (dist >= 0) & (dist <= qn)
    zeros = jnp.zeros((qn, V7X_LANES), BF16)
    srcs = (q_in, k_in, v_in)

    for a, src in enumerate(srcs):
        for r in range(dm):
            sm[a, r] = src[0, 0, pl.ds(r, rows_m, stride=dm), :]

    def task_index(d, r, n):
        if d == d_lo:
            return None, slice(n * qn, (n + 1) * qn)
        if d == dm:
            return r, slice(n * qn, (n + 1) * qn)
        return r % dm, pl.ds(r // dm + n * qn * g_hi, qn, stride=g_hi)

    for bi, (window, d) in enumerate(DILATED_PAIRS):
        nb = seq // (d * qn)
        tasks = [(r, n) for r in range(d) for n in range(nb)]
        for g, (r, n) in enumerate(tasks):
            sel, rows = task_index(d, r, n)

            def operand(a, sel=sel, rows=rows):
                return srcs[a][0, 0, rows, :] if sel is None else sm[a, sel, rows, :]

            qv = operand(0)
            for h in range(n_head):
                qs[h, g] = jnp.where(lane // ATT_HEAD_DIM == h, qv, 0.0).astype(BF16)
            for a, dst in ((1, ks), (2, vs)):
                blk = operand(a).astype(BF16)
                dst[g, qn:, :] = blk
                if n + 1 < nb:
                    dst[g + 1, :qn, :] = blk
                if n == 0:
                    dst[g, :qn, :] = zeros

        for g0 in range(0, n_task, gc):
            gsl = slice(g0, g0 + gc)
            gidx = g0 + lax.broadcasted_iota(jnp.int32, (gc, qn, 2 * qn), 0)
            valid = band & ((gidx % nb != 0) | (kj >= qn))
            k = ks[gsl]
            v = vs[gsl]
            o = None
            for h in range(n_head):
                s = jnp.einsum("gqd,gkd->gqk", qs[h, gsl], k, preferred_element_type=F32)
                s = jnp.where(valid, s, NEG_INF)
                m = jnp.max(s, axis=-1, keepdims=True)
                p = jnp.exp2(s - m)
                den = jnp.sum(p, axis=-1, keepdims=True)
                oh = jnp.einsum("gqk,gkd->gqd", p.astype(BF16), v,
                                preferred_element_type=F32) / den
                lh = jnp.broadcast_to(m * LN_2 + jnp.log(den), oh.shape)
                if o is None:
                    o, lse = oh, lh
                else:
                    in_head = lane[None] // ATT_HEAD_DIM == h
                    o, lse = jnp.where(in_head, oh, o), jnp.where(in_head, lh, lse)
            for t in range(gc):
                sel, rows = task_index(d, *tasks[g0 + t])
                if sel is None:
                    no_lo[rows, :] = o[t]
                    nl_lo[rows, :] = lse[t]
                else:
                    no_s[bi - 1, sel, rows, :] = o[t]
                    nl_s[bi - 1, sel, rows, :] = lse[t]

    for r in range(dm):
        for c in range(rows_m // qn):
            rows = slice(c * qn, (c + 1) * qn)
            seq_rows = pl.ds(r + dm * c * qn, qn, stride=dm)
            ls = [nl_lo[seq_rows, :], nl_s[0, r, rows, :], nl_s[1, r, rows, :]]
            os_ = [no_lo[seq_rows, :], no_s[0, r, rows, :], no_s[1, r, rows, :]]
            mx = functools.reduce(jnp.maximum, ls)
            ws = [jnp.exp(l - mx) for l in ls]
            tot = functools.reduce(lambda x, y: x + y, ws)
            acc = functools.reduce(lambda x, y: x + y, [w * o for w, o in zip(ws, os_)])
            nat[seq_rows, :] = acc / tot
    out_ref[0, 0] = nat[...].astype(out_ref.dtype)


def _attention(q_hp, k_hp, v_hp):
    bsz, hp, seq, _ = q_hp.shape
    (_, d_lo), (_, dm), (_, d_hi) = DILATED_PAIRS
    assert d_lo == 1 and d_hi % dm == 0
    for window, d in DILATED_PAIRS:
        assert window // d == ATT_BLOCK and seq % (d * ATT_BLOCK) == 0
    n_task = seq // ATT_BLOCK
    assert n_task % min(ATT_TASKS_PER_STEP, n_task) == 0
    n_head = V7X_LANES // ATT_HEAD_DIM
    streams = (dm, seq // dm, V7X_LANES)
    spec = pl.BlockSpec((1, 1, seq, V7X_LANES), lambda b, h: (b, h, 0, 0))
    return pl.pallas_call(
        functools.partial(_attn_kernel, seq=seq),
        grid=(bsz, hp),
        in_specs=[spec, spec, spec],
        out_specs=spec,
        out_shape=jax.ShapeDtypeStruct((bsz, hp, seq, V7X_LANES), BF16),
        scratch_shapes=[pltpu.VMEM((n_head, n_task, ATT_BLOCK, V7X_LANES), BF16),
                        pltpu.VMEM((n_task, 2 * ATT_BLOCK, V7X_LANES), BF16),
                        pltpu.VMEM((n_task, 2 * ATT_BLOCK, V7X_LANES), BF16),
                        pltpu.VMEM((3,) + streams, F32),
                        pltpu.VMEM((seq, V7X_LANES), F32), pltpu.VMEM((seq, V7X_LANES), F32),
                        pltpu.VMEM((2,) + streams, F32), pltpu.VMEM((2,) + streams, F32),
                        pltpu.VMEM((seq, V7X_LANES), F32)],
        compiler_params=_cparams(("arbitrary", "arbitrary")),
        name="dilated_attn",
    )(q_hp, k_hp, v_hp)


def _hgrn_kernel(q_ref, f_ref, i_ref, g_ref, lb_ref, gw_ref, sums_ref, o_ref,
                 st_ref, hl_ref, kk_ref, cum_ref, *, heads, ts):
    @pl.when(pl.program_id(2) == 0)
    def _():
        st_ref[...] = jnp.zeros_like(st_ref)

    c_len = HG_CHUNK
    lanes = heads * HG_EXPAND
    lb = lb_ref[...]
    for c in range(ts // c_len):
        rows = slice(c * c_len, (c + 1) * c_len)
        f = lb + (1.0 - lb) * jax.nn.sigmoid(f_ref[0, rows, :].astype(F32))
        kk_ref[rows, :] = 1.0 - f
        hi, lo = _bf16_split(jnp.log(f))
        hl_ref[rows, :lanes] = hi
        hl_ref[rows, lanes:] = lo
    both = jnp.dot(sums_ref[...], hl_ref[...], preferred_element_type=F32)
    cum_ref[...] = both[:, :lanes] + both[:, lanes:]

    n_sub = c_len // HG_SUB

    def sub_block_anchor(b):
        mids = [b[i * HG_SUB + HG_SUB // 2:i * HG_SUB + HG_SUB // 2 + 1, :] for i in range(n_sub)]
        return jnp.concatenate([jnp.broadcast_to(m, (HG_SUB, m.shape[1])) for m in mids], axis=0)

    gi = lax.broadcasted_iota(jnp.int32, (heads * n_sub, HG_SUB, c_len), 0) % n_sub
    qi = lax.broadcasted_iota(jnp.int32, (heads * n_sub, HG_SUB, c_len), 1)
    si = lax.broadcasted_iota(jnp.int32, (heads * n_sub, HG_SUB, c_len), 2)
    causal = si <= gi * HG_SUB + qi
    key_row = lax.broadcasted_iota(jnp.int32, (c_len, HG_EXPAND), 0)

    for c in range(ts // c_len):
        rows = slice(c * c_len, (c + 1) * c_len)
        qts, kts, vbs = [], [], []
        for h in range(heads):
            lsl = slice(h * HG_EXPAND, (h + 1) * HG_EXPAND)
            b = cum_ref[rows, lsl]
            anchor = sub_block_anchor(b)
            kk = kk_ref[rows, lsl]
            qt = (q_ref[0, rows, lsl].astype(F32) * jnp.exp(b - anchor)).astype(BF16)
            qts.append(qt.reshape(n_sub, HG_SUB, HG_EXPAND))
            vb = i_ref[0, rows, lsl].astype(BF16)
            for i in range(n_sub):
                hi_r = (i + 1) * HG_SUB
                kt = kk * jnp.exp(anchor[i * HG_SUB:i * HG_SUB + 1, :] - b)
                kts.append(jnp.where(key_row < hi_r, kt, 0.0).astype(BF16))
                vbs.append(vb)
        a = jnp.einsum("gqk,gsk->gqs", jnp.concatenate(qts, axis=0), jnp.stack(kts),
                       preferred_element_type=F32)
        a = jnp.where(causal, a, 0.0).astype(BF16)
        o_intra = jnp.einsum("gqs,gsv->gqv", a, jnp.stack(vbs), preferred_element_type=F32)

        for h in range(heads):
            lsl = slice(h * HG_EXPAND, (h + 1) * HG_EXPAND)
            b = cum_ref[rows, lsl]
            b_last = jnp.broadcast_to(b[c_len - 1:c_len, :], b.shape)
            kk = kk_ref[rows, lsl]
            q = q_ref[0, rows, lsl].astype(F32)
            v = i_ref[0, rows, lsl].astype(F32)
            st = st_ref[h]
            o_inter = lax.dot_general((q * jnp.exp(b)).astype(BF16), st.astype(BF16), NT_DIMS,
                                      preferred_element_type=F32)
            o = o_inter + o_intra[h * n_sub:(h + 1) * n_sub].reshape(c_len, HG_EXPAND)
            kl = kk * jnp.exp(b_last - b)
            upd = jnp.dot(v.T.astype(BF16), kl.astype(BF16), preferred_element_type=F32)
            st_ref[h] = st * jnp.exp(b_last[0:1, :]) + upd
            o = o * lax.rsqrt(jnp.mean(o * o, axis=-1, keepdims=True) + RMS_EPS)
            g = g_ref[0, rows, lsl].astype(F32)
            o = o * gw_ref[:, lsl] * (g * jax.nn.sigmoid(g))
            o_ref[0, rows, lsl] = o.astype(o_ref.dtype)


def _hgrn_cumsum_matrix(ts):
    t = np.arange(ts)[:, None]
    s = np.arange(ts)[None, :]
    return (((t // HG_CHUNK) == (s // HG_CHUNK)) & (s <= t)).astype(np.float32)


def _hgrn2(proj3, lb, gw, att_w, hg_w):
    bsz, seq, _ = proj3.shape
    n_heads = hg_w // HG_EXPAND
    heads = _pick(n_heads, (8, 4, 2, 1))
    lanes = heads * HG_EXPAND
    ts = _pick(seq, (256, 128, 64))
    base = 3 * att_w
    assert base % lanes == 0 and hg_w % lanes == 0

    def seg(k):
        off = (base + k * hg_w) // lanes
        return pl.BlockSpec((1, ts, lanes), lambda b, h, s: (b, s, off + h))

    vec = pl.BlockSpec((1, lanes), lambda b, h, s: (0, h))
    sums = jnp.asarray(_hgrn_cumsum_matrix(ts), BF16)
    return pl.pallas_call(
        functools.partial(_hgrn_kernel, heads=heads, ts=ts),
        grid=(bsz, hg_w // lanes, seq // ts),
        in_specs=[seg(0), seg(1), seg(2), seg(3), vec, vec,
                  pl.BlockSpec((ts, ts), lambda b, h, s: (0, 0))],
        out_specs=pl.BlockSpec((1, ts, lanes), lambda b, h, s: (b, s, h)),
        out_shape=jax.ShapeDtypeStruct((bsz, seq, hg_w), BF16),
        scratch_shapes=[pltpu.VMEM((heads, HG_EXPAND, HG_EXPAND), F32),
                        pltpu.VMEM((ts, 2 * lanes), BF16),
                        pltpu.VMEM((ts, lanes), F32),
                        pltpu.VMEM((ts, lanes), F32)],
        compiler_params=_cparams(("arbitrary", "arbitrary", "arbitrary")),
        name="hgrn2",
    )(proj3, proj3, proj3, proj3, lb.reshape(1, hg_w), gw.reshape(1, hg_w), sums)


def _layer_norm(y, g, b):
    mu = jnp.mean(y, axis=-1, keepdims=True)
    yc = y - mu
    var = jnp.mean(yc * yc, axis=-1, keepdims=True)
    return yc * lax.rsqrt(var + LN_EPS) * g + b


def _bf16_split(x):
    hi = x.astype(BF16)
    lo = (x - hi.astype(F32)).astype(BF16)
    return hi, lo


def _pack_bf16_pair(lo, hi):
    lo_bits = lax.bitcast_convert_type(lo.astype(BF16).astype(F32), jnp.uint32)
    hi_bits = lax.bitcast_convert_type(hi.astype(BF16).astype(F32), jnp.uint32)
    return (lo_bits >> 16) | (hi_bits & jnp.uint32(0xFFFF0000))


def _store_token_rows(ref, tok0, words):
    n, width = words.shape
    nblk = width // V7X_LANES
    for c in range(nblk):
        ref[pl.ds(tok0 * nblk + c, n, stride=nblk), :] = words[:, c * V7X_LANES:(c + 1) * V7X_LANES]


def _load_token_rows(ref, n, nblk, tok0=0):
    return jnp.concatenate([ref[pl.ds(tok0 * nblk + c, n, stride=nblk), :] for c in range(nblk)],
                           axis=1)


def _unpack_bf16_pair(words):
    lo = lax.bitcast_convert_type(words << 16, F32)
    hi = lax.bitcast_convert_type(words & jnp.uint32(0xFFFF0000), F32)
    return lo, hi


def _load_weight_bf16(w_hbm, wbf_ref, stage_ref, sem):
    rows = stage_ref.shape[1]
    n_slab = w_hbm.shape[0] // rows

    def fetch(s, slot):
        return pltpu.make_async_copy(w_hbm.at[pl.ds(s * rows, rows)], stage_ref.at[slot],
                                     sem.at[slot])

    fetch(0, 0).start()
    for s in range(n_slab):
        if s + 1 < n_slab:
            fetch(s + 1, (s + 1) % 2).start()
        fetch(s, s % 2).wait()
        wbf_ref[s * rows:(s + 1) * rows, :] = stage_ref[s % 2].astype(BF16)


def _out_proj_kernel(att_ref, rec_ref, x_ref, wo_hbm, gt_ref, sc_ref, sh_ref, g_ref, b_ref,
                     rw_ref, rb_ref, x1_ref, h2_ref, lg_ref, wobf_ref, rwhi_ref, rwlo_ref,
                     stage_ref, hi_ref, lo_ref, mix_a, mix_b, sem, *, alpha, att_w):
    i = pl.program_id(0)

    @pl.when(i == 0)
    def _():
        _load_weight_bf16(wo_hbm, wobf_ref, stage_ref, sem)
        hi, lo = _bf16_split(rw_ref[...])
        rwhi_ref[...] = hi
        rwlo_ref[...] = lo
        mix_b[...] = jnp.zeros_like(mix_b)

    def step(mix_out, mix_in):
        att = jnp.concatenate([att_ref[0, h] for h in range(att_ref.shape[1])], axis=1)
        mix_out[...] = (jnp.dot(att, wobf_ref[:att_w, :], preferred_element_type=F32)
                        + jnp.dot(rec_ref[...], wobf_ref[att_w:, :], preferred_element_type=F32))
        half = x_ref.shape[1] // 2
        for r in range(x_ref.shape[0] // LN_ROW_GROUP):
            rows = slice(r * LN_ROW_GROUP, (r + 1) * LN_ROW_GROUP)
            y = alpha * x_ref[rows, :] + (1.0 + gt_ref[0]) * mix_in[rows, :]
            x1 = _layer_norm(y, g_ref[...], b_ref[...])
            x1_ref[rows, :] = x1
            h2 = x1 * (1.0 + sc_ref[0]) + sh_ref[0]
            hi, lo = _bf16_split(h2)
            hi_ref[rows, :] = hi
            lo_ref[rows, :] = lo
            _store_token_rows(h2_ref, r * LN_ROW_GROUP,
                              _pack_bf16_pair(h2[:, :half], h2[:, half:]))
        lg = (jnp.dot(hi_ref[...], rwhi_ref[...], preferred_element_type=F32)
              + jnp.dot(lo_ref[...], rwhi_ref[...], preferred_element_type=F32)
              + jnp.dot(hi_ref[...], rwlo_ref[...], preferred_element_type=F32))
        lg_ref[...] = lg.T[:lg_ref.shape[0], :] + rb_ref[...]

    @pl.when(i % 2 == 0)
    def _():
        step(mix_a, mix_b)

    @pl.when(i % 2 == 1)
    def _():
        step(mix_b, mix_a)


OUT_PROJ_STAGE_ROWS = 256
LN_ROW_GROUP = 16


def _out_proj(att, rec, x2d, wo, gt, sc, sh, g, b, rw, rb, seq, alpha):
    t, d = x2d.shape
    hp = att.shape[1]
    att_w = hp * V7X_LANES
    n_e = rw.shape[1]
    assert n_e <= V7X_LANES and d % OUT_PROJ_STAGE_ROWS == 0
    rw_pad = jnp.zeros((d, V7X_LANES), F32).at[:, :n_e].set(rw)
    nblk = d // 2 // V7X_LANES
    tm = _pick(seq, (512, 256, 128))
    per_b = seq // tm
    n_tile = t // tm
    mm = lambda i: jnp.minimum(i, n_tile - 1)
    ep = lambda i: jnp.maximum(i - 1, 0)
    vec3 = pl.BlockSpec((1, 1, d), lambda i: (ep(i) // per_b, 0, 0))
    full = lambda shape: pl.BlockSpec(shape, lambda i: (0,) * len(shape))
    return pl.pallas_call(
        functools.partial(_out_proj_kernel, alpha=alpha, att_w=att_w),
        grid=(n_tile + 1,),
        in_specs=[pl.BlockSpec((1, hp, tm, V7X_LANES),
                               lambda i: (mm(i) // per_b, 0, mm(i) % per_b, 0)),
                  pl.BlockSpec((tm, d - att_w), lambda i: (mm(i), 0)),
                  pl.BlockSpec((tm, d), lambda i: (ep(i), 0)),
                  pl.BlockSpec(memory_space=pl.ANY), vec3, vec3, vec3, full((1, d)), full((1, d)),
                  full((d, V7X_LANES)), full((n_e, 1))],
        out_specs=[pl.BlockSpec((tm, d), lambda i: (ep(i), 0)),
                   pl.BlockSpec((tm * nblk, V7X_LANES), lambda i: (ep(i), 0)),
                   pl.BlockSpec((n_e, tm), lambda i: (0, ep(i)))],
        out_shape=[jax.ShapeDtypeStruct((t, d), F32),
                   jax.ShapeDtypeStruct((t * nblk, V7X_LANES), jnp.uint32),
                   jax.ShapeDtypeStruct((n_e, t), F32)],
        scratch_shapes=[pltpu.VMEM((d, d), BF16), pltpu.VMEM((d, V7X_LANES), BF16),
                        pltpu.VMEM((d, V7X_LANES), BF16),
                        pltpu.VMEM((2, OUT_PROJ_STAGE_ROWS, d), F32),
                        pltpu.VMEM((tm, d), BF16), pltpu.VMEM((tm, d), BF16),
                        pltpu.VMEM((tm, d), F32), pltpu.VMEM((tm, d), F32),
                        pltpu.SemaphoreType.DMA((2,))],
        compiler_params=_cparams(("arbitrary",)),
        name="out_proj_ln1",
    )(att, rec, x2d, wo, gt[:, None, :], sc[:, None, :], sh[:, None, :],
      g.reshape(1, d), b.reshape(1, d), rw_pad, rb.reshape(n_e, 1))


def _routing_kernel(lg_ref, tri_ref, low_ref, gate_ref, dest_ref, tile_ref, pad_ref, sel_scr, rk_scr,
                    *, blk, tm):
    n_e, t = lg_ref.shape
    eidx = lax.broadcasted_iota(jnp.int32, (n_e, t), 0)
    cur = lg_ref[...]
    vals, idxs = [], []
    for _ in range(TOP_K):
        m = jnp.max(cur, axis=0, keepdims=True)
        ik = jnp.min(jnp.where(cur == m, eidx, n_e), axis=0, keepdims=True)
        cur = jnp.where(eidx == ik, -jnp.inf, cur)
        vals.append(m)
        idxs.append(ik)
    es = [jnp.exp(v - vals[0]) for v in vals]
    den = functools.reduce(lambda a, b: a + b, es)
    for k in range(TOP_K):
        gate_ref[k:k + 1, :] = es[k] / den
    sel = functools.reduce(lambda a, b: a | b, [eidx == ik for ik in idxs])
    sel_scr[...] = jnp.where(sel, 1.0, 0.0)

    tri = tri_ref[...]
    carry = jnp.zeros((n_e, 1), F32)
    for j in range(t // blk):
        sb = sel_scr[:, j * blk:(j + 1) * blk]
        pre = jnp.dot(sb.astype(BF16), tri, preferred_element_type=F32)
        rk_scr[:, j * blk:(j + 1) * blk] = pre + carry
        carry = carry + jnp.sum(sb, axis=1, keepdims=True)

    counts = jnp.broadcast_to(carry, (n_e, V7X_LANES))
    padded = jnp.floor((counts + (tm - 1)) * (1.0 / tm)) * tm
    pends = jnp.dot(low_ref[...], padded, precision=lax.Precision.HIGHEST,
                    preferred_element_type=F32)
    pstarts = pends - padded
    row0 = rk_scr[...] + pstarts[:, 0:1]
    for k in range(TOP_K):
        dest_ref[k:k + 1, :] = jnp.sum(jnp.where(eidx == idxs[k], row0, 0.0), axis=0,
                                       keepdims=True).astype(jnp.int32)
    starts = (lax.broadcasted_iota(jnp.int32, (n_e, tile_ref.shape[1]), 1) * tm).astype(F32)
    owner = jnp.sum(jnp.where(pends[:, 0:1] <= starts, 1.0, 0.0), axis=0, keepdims=True)
    tile_ref[0:1, :] = jnp.minimum(owner, n_e - 1.0).astype(jnp.int32)
    tile_ref[1:2, :] = jnp.broadcast_to(jnp.max(pends[:, 0:1], axis=0, keepdims=True),
                                        (1, tile_ref.shape[1])).astype(jnp.int32)
    e_tile = lax.broadcasted_iota(jnp.int32, starts.shape, 0).astype(F32)
    row_end = jnp.sum(jnp.where(e_tile == owner, (pstarts + counts)[:, 0:1], 0.0), axis=0,
                      keepdims=True)
    tile_ref[2:3, :] = jnp.clip(row_end - starts[0:1, :], 0.0, float(tm)).astype(jnp.int32)
    pad_ref[0] = (pstarts + counts).astype(jnp.int32)
    pad_ref[1] = (padded - counts).astype(jnp.int32)


def _routing(logits_t, tm, n_tiles):
    n_e, t = logits_t.shape
    assert tm & (tm - 1) == 0
    blk = _pick(t, (256, 128))
    tri = jnp.asarray(np.triu(np.ones((blk, blk), np.float32), 1), BF16)
    low = jnp.asarray(np.tril(np.ones((n_e, n_e), np.float32)))
    ntp = -(-n_tiles // V7X_LANES) * V7X_LANES
    full = lambda shape: pl.BlockSpec(shape, lambda: (0,) * len(shape))
    return pl.pallas_call(
        functools.partial(_routing_kernel, blk=blk, tm=tm),
        in_specs=[full((n_e, t)), full((blk, blk)), full((n_e, n_e))],
        out_specs=[full((TOP_K, t)), full((TOP_K, t)), full((3, ntp)), full((2, n_e, V7X_LANES))],
        out_shape=[jax.ShapeDtypeStruct((TOP_K, t), F32),
                   jax.ShapeDtypeStruct((TOP_K, t), jnp.int32),
                   jax.ShapeDtypeStruct((3, ntp), jnp.int32),
                   jax.ShapeDtypeStruct((2, n_e, V7X_LANES), jnp.int32)],
        scratch_shapes=[pltpu.VMEM((n_e, t), F32), pltpu.VMEM((n_e, t), F32)],
        compiler_params=pltpu.CompilerParams(vmem_limit_bytes=V7X_VMEM_LIMIT),
        name="routing",
    )(logits_t, tri, low)


def _dispatch_kernel(dest_ref, pad_off_ref, pad_n_ref, nrow_ref, h_ref, o_ref, zero_ref, sem, zsem,
                     *, n_tok, tq, tm, n_e, nblk):
    step = pl.program_id(0)
    base = step * tq

    def rows(tok, n=1):
        return pl.ds(pl.multiple_of(tok * nblk, nblk), n * nblk)

    def fill(off, n):
        return pltpu.make_async_copy(zero_ref.at[rows(0, n)], o_ref.at[rows(off, n)], zsem)

    pieces = [1 << s for s in range(tm.bit_length() - 2, -1, -1)]

    def pad_rows(wait):
        def per_expert(e, c):
            off = pad_off_ref[e]
            n = pad_n_ref[e]
            for p in pieces:
                hit = (n & p) != 0

                @pl.when(hit)
                def _(off=off, p=p):
                    cp = fill(off, p)
                    cp.wait() if wait else cp.start()

                off = off + jnp.where(hit, p, 0)
            return c
        lax.fori_loop(0, n_e, per_expert, 0)

        def per_tile(i, c):
            cp = fill(i * tm, tm)
            cp.wait() if wait else cp.start()
            return c
        lax.fori_loop(nrow_ref[0] // tm, o_ref.shape[0] // (tm * nblk), per_tile, 0)

    @pl.when(step == 0)
    def _():
        zero_ref[...] = jnp.zeros_like(zero_ref)
        pad_rows(False)

    def start(r, c):
        for k in range(TOP_K):
            dst = o_ref.at[rows(dest_ref[k * n_tok + base + r])]
            pltpu.make_async_copy(h_ref.at[rows(r)], dst, sem).start(priority=k % 2)
        return c

    lax.fori_loop(0, tq, start, 0, unroll=4)
    for _ in range(TOP_K):
        pltpu.make_async_copy(h_ref, h_ref, sem).wait()

    @pl.when(step == 0)
    def _():
        pad_rows(True)


def _dispatch(h2p, dest_flat, pad_off, pad_n, n_used_rows, n_rows, tm, nblk):
    t = h2p.shape[0] // nblk
    n_e = pad_off.shape[0]
    tq = _pick(t, (512, 256, 128))
    return pl.pallas_call(
        functools.partial(_dispatch_kernel, n_tok=t, tq=tq, tm=tm, n_e=n_e, nblk=nblk),
        grid_spec=pltpu.PrefetchScalarGridSpec(
            num_scalar_prefetch=4, grid=(t // tq,),
            in_specs=[pl.BlockSpec((tq * nblk, V7X_LANES), lambda i, *_: (i, 0))],
            out_specs=pl.BlockSpec(memory_space=pl.ANY),
            scratch_shapes=[pltpu.VMEM((tm * nblk, V7X_LANES), h2p.dtype),
                            pltpu.SemaphoreType.DMA(()), pltpu.SemaphoreType.DMA(())]),
        out_shape=jax.ShapeDtypeStruct((n_rows * nblk, V7X_LANES), h2p.dtype),
        compiler_params=pltpu.CompilerParams(dimension_semantics=("arbitrary",),
                                             has_side_effects=True,
                                             vmem_limit_bytes=V7X_VMEM_LIMIT),
        name="moe_dispatch",
    )(dest_flat, pad_off, pad_n, n_used_rows, h2p)


EXPERT_COL_BLOCK = 2048
GEMM1_COL_CHUNK = 256
MOE_TILE_LEVELS = 3
MOE_MIN_ROWS = 16
CAST_ROWS = 256


def _stream_expert_weights(te_ref, nt_ref, w_hbm, stage_ref, slot_ref, sem):
    j, i = pl.program_id(0), pl.program_id(1)
    n_pass, n_tiles = pl.num_programs(0), pl.num_programs(1)
    tn = stage_ref.shape[2]
    n_used = nt_ref[0]
    cur = te_ref[i]
    first = (i < n_used) & ((i == 0) | (cur != te_ref[jnp.maximum(i - 1, 0)]))

    def fetch(e, jj, slot):
        src = w_hbm.at[e, :, pl.ds(pl.multiple_of(jj * tn, tn), tn)]
        return pltpu.make_async_copy(src, stage_ref.at[slot], sem.at[slot])

    @pl.when((i == 0) & (j == 0))
    def _():
        slot_ref[0] = 1
        fetch(te_ref[0], 0, 0).start()

    @pl.when(first)
    def _():
        slot = 1 - slot_ref[0]
        slot_ref[0] = slot
        fetch(cur, j, slot).wait()

        def same_group(k):
            return (k < n_used) & (te_ref[jnp.minimum(k, n_tiles - 1)] == cur)
        nxt = lax.while_loop(same_group, lambda k: k + 1, i + 1)

        @pl.when(nxt < n_used)
        def _():
            fetch(te_ref[jnp.minimum(nxt, n_tiles - 1)], j, 1 - slot).start()

        @pl.when((nxt >= n_used) & (j + 1 < n_pass))
        def _():
            fetch(te_ref[0], j + 1, 1 - slot).start()

    return first, slot_ref[0]


def _cast_block(stage_ref, slot, wbf_ref):
    def cast(r, c):
        rows = pl.ds(pl.multiple_of(r * CAST_ROWS, CAST_ROWS), CAST_ROWS)
        wbf_ref[rows, :] = stage_ref[slot, rows, :].astype(BF16)
        return c
    lax.fori_loop(0, wbf_ref.shape[0] // CAST_ROWS, cast, 0)


def _gemm1_kernel(te_ref, nt_ref, tv_ref, x_ref, w_hbm, b_ref, o_ref, stage_ref, wbf_ref,
                  slot_ref, sem, *, nc, nblk):
    first, slot = _stream_expert_weights(te_ref, nt_ref, w_hbm, stage_ref, slot_ref, sem)

    def compute(rows, convert):
        words = _load_token_rows(x_ref, rows, nblk)
        x_lo, x_hi = (v.astype(BF16) for v in _unpack_bf16_pair(words))
        half = x_lo.shape[1]
        lane128 = lax.broadcasted_iota(jnp.int32, (rows, V7X_LANES), 1)
        even_idx = (2 * lane128) % V7X_LANES
        for n0 in range(0, wbf_ref.shape[1], nc):
            if convert:
                wbf_ref[:, n0:n0 + nc] = stage_ref[slot, :, n0:n0 + nc].astype(BF16)
            hb = (jnp.dot(x_lo, wbf_ref[:half, n0:n0 + nc], preferred_element_type=F32)
                  + jnp.dot(x_hi, wbf_ref[half:, n0:n0 + nc], preferred_element_type=F32)
                  + b_ref[0, :, n0:n0 + nc])
            nxt = pltpu.roll(hb, nc - 1, axis=1)
            glu = jnp.minimum(hb, SWIGLU_LIMIT)
            lin = jnp.clip(nxt, -SWIGLU_LIMIT, SWIGLU_LIMIT)
            act = glu * jax.nn.sigmoid(SWIGLU_ALPHA * glu) * (lin + 1.0)
            for c in range(nc // (2 * V7X_LANES)):
                lo_blk = act[:, (2 * c) * V7X_LANES:(2 * c + 1) * V7X_LANES]
                hi_blk = act[:, (2 * c + 1) * V7X_LANES:(2 * c + 2) * V7X_LANES]
                out = jnp.where(lane128 < V7X_LANES // 2,
                                jnp.take_along_axis(lo_blk, even_idx, axis=1),
                                jnp.take_along_axis(hi_blk, even_idx, axis=1))
                o0 = n0 // 2 + c * V7X_LANES
                o_ref[:rows, o0:o0 + V7X_LANES] = out.astype(o_ref.dtype)
        if rows < o_ref.shape[0]:
            o_ref[rows:, :] = jnp.zeros((o_ref.shape[0] - rows, o_ref.shape[1]), o_ref.dtype)

    _for_real_rows(pl.program_id(1), nt_ref, tv_ref, o_ref, o_ref.shape[0], compute,
                   first, lambda: _cast_block(stage_ref, slot, wbf_ref))


def _for_real_rows(i, nt_ref, tv_ref, o_ref, tm, compute, first, convert_all):
    used = i < nt_ref[0]
    real = tv_ref[i]
    sizes = [tm >> s for s in range(MOE_TILE_LEVELS) if (tm >> s) % MOE_MIN_ROWS == 0]

    @pl.when(first & (real <= sizes[1]))
    def _():
        convert_all()

    for n, rows in enumerate(sizes):
        fits = real <= rows
        if n + 1 < len(sizes):
            fits = fits & (real > sizes[n + 1])
        if n == 0:
            @pl.when(used & fits & first)
            def _():
                compute(rows, True)
            fits = fits & jnp.logical_not(first)

        @pl.when(used & fits)
        def _(rows=rows):
            compute(rows, False)

    @pl.when(jnp.logical_not(used))
    def _():
        o_ref[...] = jnp.zeros_like(o_ref)


def _gemm1(xin, w1, b1, tile_e, n_tiles_used, tile_rows, tm, nblk):
    n_rows = xin.shape[0] // nblk
    n_e, d, f2 = w1.shape
    tn = min(EXPERT_COL_BLOCK, f2)
    nc = min(GEMM1_COL_CHUNK, tn)
    assert f2 % tn == 0 and d % CAST_ROWS == 0
    n_tiles = n_rows // tm
    assert nc % (2 * V7X_LANES) == 0

    def used(i, nt):
        return jnp.minimum(i, nt[0] - 1)

    return pl.pallas_call(
        functools.partial(_gemm1_kernel, nc=nc, nblk=nblk),
        grid_spec=pltpu.PrefetchScalarGridSpec(
            num_scalar_prefetch=3, grid=(f2 // tn, n_tiles),
            in_specs=[pl.BlockSpec((tm * nblk, V7X_LANES),
                                   lambda j, i, te, nt, tv: (used(i, nt), 0)),
                      pl.BlockSpec(memory_space=pl.ANY),
                      pl.BlockSpec((1, 1, tn), lambda j, i, te, nt, tv: (te[used(i, nt)], 0, j))],
            out_specs=pl.BlockSpec((tm, tn // 2), lambda j, i, te, nt, tv: (i, j)),
            scratch_shapes=[pltpu.VMEM((2, d, tn), F32), pltpu.VMEM((d, tn), BF16),
                            pltpu.SMEM((1,), jnp.int32), pltpu.SemaphoreType.DMA((2,))]),
        out_shape=jax.ShapeDtypeStruct((n_rows, f2 // 2), BF16),
        compiler_params=_cparams(("arbitrary", "arbitrary")),
        name="moe_gemm1",
    )(tile_e, n_tiles_used, tile_rows, xin, w1, b1.reshape(n_e, 1, f2))


def _gemm2_kernel(te_ref, nt_ref, tv_ref, a_ref, w_hbm, b_ref, o_ref, stage_ref, wbf_ref, slot_ref,
                  sem, *, nc, nblk):
    first, slot = _stream_expert_weights(te_ref, nt_ref, w_hbm, stage_ref, slot_ref, sem)
    tm = a_ref.shape[0]

    def compute(rows, convert):
        a = a_ref[:rows, :]
        half = wbf_ref.shape[1] // 2

        def cols(n0):
            if convert:
                wbf_ref[:, n0:n0 + nc] = stage_ref[slot, :, n0:n0 + nc].astype(BF16)
            return (jnp.dot(a, wbf_ref[:, n0:n0 + nc], preferred_element_type=F32)
                    + b_ref[0, :, n0:n0 + nc])

        for n0 in range(0, half, nc):
            words = _pack_bf16_pair(cols(n0), cols(half + n0))
            for c in range(nc // V7X_LANES):
                blk = n0 // V7X_LANES + c
                o_ref[pl.ds(blk, rows, stride=nblk), :] = words[:, c * V7X_LANES:(c + 1) * V7X_LANES]
        if rows < tm:
            o_ref[rows * nblk:, :] = jnp.zeros(((tm - rows) * nblk, o_ref.shape[1]), o_ref.dtype)

    _for_real_rows(pl.program_id(1), nt_ref, tv_ref, o_ref, tm, compute,
                   first, lambda: _cast_block(stage_ref, slot, wbf_ref))


def _gemm2(act, w2, b2, tile_e, n_tiles_used, tile_rows, tm):
    n_rows, f = act.shape
    n_e, _, d = w2.shape
    tn = d
    nc = min(512, tn // 2)
    nblk = d // 2 // V7X_LANES
    assert f % CAST_ROWS == 0
    n_tiles = n_rows // tm

    def used(i, nt):
        return jnp.minimum(i, nt[0] - 1)

    return pl.pallas_call(
        functools.partial(_gemm2_kernel, nc=nc, nblk=nblk),
        grid_spec=pltpu.PrefetchScalarGridSpec(
            num_scalar_prefetch=3, grid=(d // tn, n_tiles),
            in_specs=[pl.BlockSpec((tm, f), lambda j, i, te, nt, tv: (used(i, nt), 0)),
                      pl.BlockSpec(memory_space=pl.ANY),
                      pl.BlockSpec((1, 1, tn), lambda j, i, te, nt, tv: (te[used(i, nt)], 0, j))],
            out_specs=pl.BlockSpec((tm * nblk, V7X_LANES), lambda j, i, te, nt, tv: (i, 0)),
            scratch_shapes=[pltpu.VMEM((2, f, tn), F32), pltpu.VMEM((f, tn), BF16),
                            pltpu.SMEM((1,), jnp.int32), pltpu.SemaphoreType.DMA((2,))]),
        out_shape=jax.ShapeDtypeStruct((n_rows * nblk, V7X_LANES), jnp.uint32),
        compiler_params=_cparams(("arbitrary", "arbitrary")),
        name="moe_gemm2",
    )(tile_e, n_tiles_used, tile_rows, act, w2, b2.reshape(n_e, 1, d))


COMBINE_ROW_GROUP = 64


def _combine_kernel(dest_ref, x1_ref, gate_ref, gt_ref, g_ref, b_ref, y_ref, o_ref, buf, sem,
                    *, alpha, tc, n_tok, nblk):
    i = pl.program_id(0)
    n_steps = pl.num_programs(0)

    def rows(tok):
        return pl.ds(pl.multiple_of(tok * nblk, nblk), nblk)

    def copy(step, slot, k, r):
        src = y_ref.at[rows(dest_ref[k * n_tok + step * tc + r])]
        return pltpu.make_async_copy(src, buf.at[slot, k, rows(r)], sem.at[slot])

    def issue(step, slot):
        def body(r, c):
            for k in range(TOP_K):
                copy(step, slot, k, r).start(priority=k % 2)
            return c
        lax.fori_loop(0, tc, body, 0, unroll=2)

    def drain(slot):
        pltpu.make_async_copy(buf.at[slot], buf.at[slot], sem.at[slot]).wait()

    @pl.when(i == 0)
    def _():
        issue(0, 0)

    def step(slot, prefetch):
        drain(slot)
        for rg in range(tc // COMBINE_ROW_GROUP):
            r0 = rg * COMBINE_ROW_GROUP
            if prefetch:
                for r in range(r0, r0 + COMBINE_ROW_GROUP):
                    for k in range(TOP_K):
                        copy(i + 1, 1 - slot, k, r).start(priority=k % 2)
            rows = slice(r0, r0 + COMBINE_ROW_GROUP)
            gates = gate_ref[rows, :]
            ff_lo = ff_hi = None
            for k in range(TOP_K):
                lo, hi = _unpack_bf16_pair(
                    _load_token_rows(buf.at[slot, k], COMBINE_ROW_GROUP, nblk, tok0=r0))
                gk = gates[:, k:k + 1]
                ff_lo = gk * lo if ff_lo is None else ff_lo + gk * lo
                ff_hi = gk * hi if ff_hi is None else ff_hi + gk * hi
            ff = jnp.concatenate([ff_lo, ff_hi], axis=1)
            y = alpha * x1_ref[rows, :] + (1.0 + gt_ref[0]) * ff
            o_ref[rows, :] = _layer_norm(y, g_ref[...], b_ref[...])

    for slot in range(2):
        for prefetch in (True, False):
            @pl.when((i % 2 == slot) & ((i + 1 < n_steps) == prefetch))
            def _(slot=slot, prefetch=prefetch):
                step(slot, prefetch)


def _combine(dest_flat, x1, gates_tk, gt, g, b, y, seq, alpha):
    t, d = x1.shape
    tc = _pick(seq, (256, 128))
    per_b = seq // tc
    nblk = d // 2 // V7X_LANES
    return pl.pallas_call(
        functools.partial(_combine_kernel, alpha=alpha, tc=tc, n_tok=t, nblk=nblk),
        grid_spec=pltpu.PrefetchScalarGridSpec(
            num_scalar_prefetch=1, grid=(t // tc,),
            in_specs=[pl.BlockSpec((tc, d), lambda i, ds: (i, 0)),
                      pl.BlockSpec((tc, TOP_K), lambda i, ds: (i, 0)),
                      pl.BlockSpec((1, 1, d), lambda i, ds: (i // per_b, 0, 0)),
                      pl.BlockSpec((1, d), lambda i, ds: (0, 0)),
                      pl.BlockSpec((1, d), lambda i, ds: (0, 0)),
                      pl.BlockSpec(memory_space=pl.ANY)],
            out_specs=pl.BlockSpec((tc, d), lambda i, ds: (i, 0)),
            scratch_shapes=[pltpu.VMEM((2, TOP_K, tc * nblk, V7X_LANES), jnp.uint32),
                            pltpu.SemaphoreType.DMA((2,))]),
        out_shape=jax.ShapeDtypeStruct((t, d), F32),
        compiler_params=_cparams(("arbitrary",)),
        name="moe_combine_ln2",
    )(dest_flat, x1, gates_tk, gt[:, None, :], g.reshape(1, d), b.reshape(1, d), y)


MOE_ROW_TILE = 512


def _moe_ffn(h2p, logits_t, x1, gt_f, ln_g, ln_b, w1, b1, w2, b2, seq, alpha):
    tm = MOE_ROW_TILE
    n_e, t = logits_t.shape
    n_rows = -(-(TOP_K * t) // tm) * tm + n_e * tm
    n_tiles = n_rows // tm
    gate_t, dest, tiles, pads = _routing(logits_t, tm, n_tiles)
    dest = dest.reshape(-1)
    tile_e = tiles[0, :n_tiles]
    n_used_rows = tiles[1, :1]
    tile_rows = tiles[2, :n_tiles]
    n_used_tiles = n_used_rows // tm
    nblk = x1.shape[1] // 2 // V7X_LANES
    xin = _dispatch(h2p, dest, pads[0, :, 0], pads[1, :, 0], n_used_rows, n_rows, tm, nblk)
    act = _gemm1(xin, w1, b1, tile_e, n_used_tiles, tile_rows, tm, nblk)
    y = _gemm2(act, w2, b2, tile_e, n_used_tiles, tile_rows, tm)
    return _combine(dest, x1, gate_t.T, gt_f, ln_g, ln_b, y, seq, alpha)


def kernel(x, c, positions, w_ada, b_ada, w_in, hgrn_lb, gnorm_w, w_o, ln1_g, ln1_b,
           router_w, router_b, w1, b1, w2, b2, ln2_g, ln2_b):
    bsz, seq, d = x.shape
    depth = w_ada.shape[0]
    t = bsz * seq
    att_w = d // 2
    hg_w = d - att_w
    alpha = (2.0 * depth) ** 0.25

    lb_all = jnp.cumsum(jax.nn.softmax(hgrn_lb.astype(F32), axis=0), axis=0)
    inv = ROPE_THETA ** (-(jnp.arange(0, ROT_DIM, 2, dtype=F32) / ROT_DIM))
    lane = np.arange(V7X_LANES)
    inv_lane = inv[(lane % ATT_HEAD_DIM) % (ROT_DIM // 2)].reshape(1, V7X_LANES)
    pos_b = jnp.broadcast_to(positions.astype(F32).reshape(t, 1), (t, V7X_LANES))

    x2d = x.reshape(t, d)
    for l in range(depth):
        mod = _adaln(c, w_ada[l], b_ada[l])
        sh_a, sc_a, gt_a, sh_f, sc_f, gt_f = jnp.split(mod, 6, axis=-1)

        proj = _in_proj(x2d, sc_a, sh_a, w_in[l], seq)
        proj3 = proj.reshape(bsz, seq, proj.shape[1])
        q_hp, k_hp, v_hp = _qk_rope(proj, pos_b, inv_lane, bsz, seq, att_w)
        att = _attention(q_hp, k_hp, v_hp)
        rec = _hgrn2(proj3, lb_all[l], gnorm_w[l], att_w, hg_w).reshape(t, hg_w)

        x1, h2p, logits_t = _out_proj(att, rec, x2d, w_o[l], gt_a, sc_f, sh_f, ln1_g[l], ln1_b[l],
                                      router_w[l], router_b[l], seq, alpha)

        x2d = _moe_ffn(h2p, logits_t, x1, gt_f, ln2_g[l], ln2_b[l], w1[l], b1[l], w2[l], b2[l],
                       seq, alpha)
    return x2d.reshape(bsz, seq, d)
```

```python
import functools

import numpy as np
import jax
import jax.numpy as jnp
from jax import lax
from jax.experimental import pallas as pl
from jax.experimental.pallas import tpu as pltpu

F32 = jnp.float32
BF16 = jnp.bfloat16

V7X_LANES = 128
V7X_VMEM_LIMIT = 56 * 1024 * 1024

ATT_HEAD_DIM = 64
DILATED_PAIRS = ((128, 1), (512, 4), (2048, 16))
ATT_BLOCK = 128
ROT_DIM = ATT_HEAD_DIM // 4
ROPE_THETA = 500000.0
HG_EXPAND = 128
HG_CHUNK = 64
HG_SUB = 16
TOP_K = 4
SWIGLU_ALPHA = 1.702
SWIGLU_LIMIT = 7.0
LN_EPS = 1e-5
RMS_EPS = 1e-6
NEG_INF = -1e30

LN_2 = float(np.log(2.0))
Q_SCALE = ATT_HEAD_DIM ** -0.5 / LN_2

NT_DIMS = (((1,), (1,)), ((), ()))


def _pick(n, candidates):
    for c in candidates:
        if n % c == 0:
            return c
    raise ValueError(f"no tile in {candidates} divides {n}")


def _cparams(sem):
    return pltpu.CompilerParams(dimension_semantics=sem, vmem_limit_bytes=V7X_VMEM_LIMIT)


def _adaln_kernel(c_ref, w_ref, b_ref, o_ref):
    c = c_ref[...]
    s = c * jax.nn.sigmoid(c)
    o_ref[...] = jnp.dot(s.astype(BF16), w_ref[...].astype(BF16),
                         preferred_element_type=F32) + b_ref[...]


def _adaln(c, w, b):
    bsz, d = c.shape
    n = w.shape[1]
    rows = 8
    cp = jnp.zeros((rows, d), F32).at[:bsz].set(c)
    tn = _pick(n, (1024, 512, 256, 128))
    out = pl.pallas_call(
        _adaln_kernel,
        grid=(n // tn,),
        in_specs=[pl.BlockSpec((rows, d), lambda j: (0, 0)),
                  pl.BlockSpec((d, tn), lambda j: (0, j)),
                  pl.BlockSpec((1, tn), lambda j: (0, j))],
        out_specs=pl.BlockSpec((rows, tn), lambda j: (0, j)),
        out_shape=jax.ShapeDtypeStruct((rows, n), F32),
        compiler_params=_cparams(("arbitrary",)),
        name="adaln",
    )(cp, w, b.reshape(1, n))
    return out[:bsz]


def _in_proj_kernel(x_ref, sc_ref, sh_ref, w_hbm, o_ref, wbf_ref, h_ref, stage_ref, sem):
    m, n = pl.program_id(0), pl.program_id(1)
    n_col = pl.num_programs(1)
    tn = o_ref.shape[1]

    def fetch(nn):
        src = w_hbm.at[:, pl.ds(pl.multiple_of(nn * tn, tn), tn)]
        return pltpu.make_async_copy(src, stage_ref, sem)

    @pl.when((m == 0) & (n == 0))
    def _():
        fetch(0).start()

    @pl.when(m == 0)
    def _():
        fetch(n).wait()
        wbf_ref[n] = stage_ref[...].astype(BF16)

        @pl.when(n + 1 < n_col)
        def _():
            fetch(n + 1).start()

    @pl.when(n == 0)
    def _():
        h_ref[...] = (x_ref[...] * (1.0 + sc_ref[0]) + sh_ref[0]).astype(BF16)

    o_ref[...] = jnp.dot(h_ref[...], wbf_ref[n], preferred_element_type=F32).astype(o_ref.dtype)


def _in_proj(x2d, sc, sh, w, seq):
    t, d = x2d.shape
    nc = w.shape[1]
    tm = _pick(seq, (512, 256, 128))
    tn = _pick(nc, (1024, 512, 256, 128))
    per_b = seq // tm
    vec = pl.BlockSpec((1, 1, d), lambda i, j: (i // per_b, 0, 0))
    return pl.pallas_call(
        _in_proj_kernel,
        grid=(t // tm, nc // tn),
        in_specs=[pl.BlockSpec((tm, d), lambda i, j: (i, 0)), vec, vec,
                  pl.BlockSpec(memory_space=pl.ANY)],
        out_specs=pl.BlockSpec((tm, tn), lambda i, j: (i, j)),
        out_shape=jax.ShapeDtypeStruct((t, nc), BF16),
        scratch_shapes=[pltpu.VMEM((nc // tn, d, tn), BF16), pltpu.VMEM((tm, d), BF16),
                        pltpu.VMEM((d, tn), F32), pltpu.SemaphoreType.DMA(())],
        compiler_params=_cparams(("arbitrary", "arbitrary")),
        name="in_proj",
    )(x2d, sc[:, None, :], sh[:, None, :], w)


def _rope_kernel(q_ref, k_ref, v_ref, pos_ref, inv_ref, qo_ref, ko_ref, vo_ref):
    tm = pos_ref.shape[0]
    lane = lax.broadcasted_iota(jnp.int32, (tm, V7X_LANES), 1)
    lh = lane % ATT_HEAD_DIM
    half = ROT_DIM // 2
    ang = pos_ref[...] * inv_ref[...]
    cs = jnp.where(lh < ROT_DIM, jnp.cos(ang), 1.0)
    sn = jnp.sin(ang)
    sn = jnp.where(lh < half, -sn, jnp.where(lh < ROT_DIM, sn, 0.0))

    def rope(t):
        swapped = jnp.where(lh < half,
                            pltpu.roll(t, V7X_LANES - half, axis=1),
                            pltpu.roll(t, half, axis=1))
        return t * cs + swapped * sn

    for h in range(qo_ref.shape[1]):
        lanes = slice(h * V7X_LANES, (h + 1) * V7X_LANES)
        qo_ref[0, h] = rope(q_ref[:, lanes].astype(F32)) * Q_SCALE
        ko_ref[0, h] = rope(k_ref[:, lanes].astype(F32))
        vo_ref[0, h] = v_ref[:, lanes].astype(F32)


def _qk_rope(proj, pos_b, inv_lane, bsz, seq, att_w):
    t = proj.shape[0]
    hp = att_w // V7X_LANES
    tm = _pick(seq, (512, 256, 128))
    per_b = seq // tm
    out_spec = pl.BlockSpec((1, hp, tm, V7X_LANES), lambda i: (i // per_b, 0, i % per_b, 0))
    shp = jax.ShapeDtypeStruct((bsz, hp, seq, V7X_LANES), F32)
    return pl.pallas_call(
        _rope_kernel,
        grid=(t // tm,),
        in_specs=[pl.BlockSpec((tm, att_w), lambda i: (i, 0)),
                  pl.BlockSpec((tm, att_w), lambda i: (i, 1)),
                  pl.BlockSpec((tm, att_w), lambda i: (i, 2)),
                  pl.BlockSpec((tm, V7X_LANES), lambda i: (i, 0)),
                  pl.BlockSpec((1, V7X_LANES), lambda i: (0, 0))],
        out_specs=[out_spec, out_spec, out_spec],
        out_shape=[shp, shp, shp],
        compiler_params=_cparams(("arbitrary",)),
        name="qk_rope",
    )(proj, proj, proj, pos_b, inv_lane)


ATT_TASKS_PER_STEP = 8


def _attn_kernel(q_in, k_in, v_in, out_ref, qs, ks, vs, sm, no_lo, nl_lo, no_s, nl_s, nat, *, seq):
    qn = ATT_BLOCK
    n_task = seq // qn
    n_head = V7X_LANES // ATT_HEAD_DIM
    gc = min(ATT_TASKS_PER_STEP, n_task)
    (_, d_lo), (_, dm), (_, d_hi) = DILATED_PAIRS
    g_hi = d_hi // dm
    rows_m = seq // dm
    lane = lax.broadcasted_iota(jnp.int32, (qn, V7X_LANES), 1)
    qi = lax.broadcasted_iota(jnp.int32, (gc, qn, 2 * qn), 1)
    kj = lax.broadcasted_iota(jnp.int32, (gc, qn, 2 * qn), 2)
    dist = qi + qn - kj
    band = (dist >= 0) & (dist <= qn)
    zeros = jnp.zeros((qn, V7X_LANES), BF16)
    srcs = (q_in, k_in, v_in)

    for a, src in enumerate(srcs):
        for r in range(dm):
            sm[a, r] = src[0, 0, pl.ds(r, rows_m, stride=dm), :]

    def task_index(d, r, n):
        if d == d_lo:
            return None, slice(n * qn, (n + 1) * qn)
        if d == dm:
            return r, slice(n * qn, (n + 1) * qn)
        return r % dm, pl.ds(r // dm + n * qn * g_hi, qn, stride=g_hi)

    for bi, (window, d) in enumerate(DILATED_PAIRS):
        nb = seq // (d * qn)
        tasks = [(r, n) for r in range(d) for n in range(nb)]
        for g, (r, n) in enumerate(tasks):
            sel, rows = task_index(d, r, n)

            def operand(a, sel=sel, rows=rows):
                return srcs[a][0, 0, rows, :] if sel is None else sm[a, sel, rows, :]

            qv = operand(0)
            for h in range(n_head):
                qs[h, g] = jnp.where(lane // ATT_HEAD_DIM == h, qv, 0.0).astype(BF16)
            for a, dst in ((1, ks), (2, vs)):
                blk = operand(a).astype(BF16)
                dst[g, qn:, :] = blk
                if n + 1 < nb:
                    dst[g + 1, :qn, :] = blk
                if n == 0:
                    dst[g, :qn, :] = zeros

        for g0 in range(0, n_task, gc):
            gsl = slice(g0, g0 + gc)
            gidx = g0 + lax.broadcasted_iota(jnp.int32, (gc, qn, 2 * qn), 0)
            valid = band & ((gidx % nb != 0) | (kj >= qn))
            k = ks[gsl]
            v = vs[gsl]
            o = None
            for h in range(n_head):
                s = jnp.einsum("gqd,gkd->gqk", qs[h, gsl], k, preferred_element_type=F32)
                s = jnp.where(valid, s, NEG_INF)
                m = jnp.max(s, axis=-1, keepdims=True)
                p = jnp.exp2(s - m)
                den = jnp.sum(p, axis=-1, keepdims=True)
                oh = jnp.einsum("gqk,gkd->gqd", p.astype(BF16), v,
                                preferred_element_type=F32) / den
                lh = jnp.broadcast_to(m * LN_2 + jnp.log(den), oh.shape)
                if o is None:
                    o, lse = oh, lh
                else:
                    in_head = lane[None] // ATT_HEAD_DIM == h
                    o, lse = jnp.where(in_head, oh, o), jnp.where(in_head, lh, lse)
            for t in range(gc):
                sel, rows = task_index(d, *tasks[g0 + t])
                if sel is None:
                    no_lo[rows, :] = o[t]
                    nl_lo[rows, :] = lse[t]
                else:
                    no_s[bi - 1, sel, rows, :] = o[t]
                    nl_s[bi - 1, sel, rows, :] = lse[t]

    for r in range(dm):
        for c in range(rows_m // qn):
            rows = slice(c * qn, (c + 1) * qn)
            seq_rows = pl.ds(r + dm * c * qn, qn, stride=dm)
            ls = [nl_lo[seq_rows, :], nl_s[0, r, rows, :], nl_s[1, r, rows, :]]
            os_ = [no_lo[seq_rows, :], no_s[0, r, rows, :], no_s[1, r, rows, :]]
            mx = functools.reduce(jnp.maximum, ls)
            ws = [jnp.exp(l - mx) for l in ls]
            tot = functools.reduce(lambda x, y: x + y, ws)
            acc = functools.reduce(lambda x, y: x + y, [w * o for w, o in zip(ws, os_)])
            nat[seq_rows, :] = acc / tot
    out_ref[0, 0] = nat[...].astype(out_ref.dtype)


def _attention(q_hp, k_hp, v_hp):
    bsz, hp, seq, _ = q_hp.shape
    (_, d_lo), (_, dm), (_, d_hi) = DILATED_PAIRS
    assert d_lo == 1 and d_hi % dm == 0
    for window, d in DILATED_PAIRS:
        assert window // d == ATT_BLOCK and seq % (d * ATT_BLOCK) == 0
    n_task = seq // ATT_BLOCK
    assert n_task % min(ATT_TASKS_PER_STEP, n_task) == 0
    n_head = V7X_LANES // ATT_HEAD_DIM
    streams = (dm, seq // dm, V7X_LANES)
    spec = pl.BlockSpec((1, 1, seq, V7X_LANES), lambda b, h: (b, h, 0, 0))
    return pl.pallas_call(
        functools.partial(_attn_kernel, seq=seq),
        grid=(bsz, hp),
        in_specs=[spec, spec, spec],
        out_specs=spec,
        out_shape=jax.ShapeDtypeStruct((bsz, hp, seq, V7X_LANES), BF16),
        scratch_shapes=[pltpu.VMEM((n_head, n_task, ATT_BLOCK, V7X_LANES), BF16),
                        pltpu.VMEM((n_task, 2 * ATT_BLOCK, V7X_LANES), BF16),
                        pltpu.VMEM((n_task, 2 * ATT_BLOCK, V7X_LANES), BF16),
                        pltpu.VMEM((3,) + streams, F32),
                        pltpu.VMEM((seq, V7X_LANES), F32), pltpu.VMEM((seq, V7X_LANES), F32),
                        pltpu.VMEM((2,) + streams, F32), pltpu.VMEM((2,) + streams, F32),
                        pltpu.VMEM((seq, V7X_LANES), F32)],
        compiler_params=_cparams(("arbitrary", "arbitrary")),
        name="dilated_attn",
    )(q_hp, k_hp, v_hp)


def _hgrn_kernel(q_ref, f_ref, i_ref, g_ref, lb_ref, gw_ref, sums_ref, o_ref,
                 st_ref, hl_ref, kk_ref, cum_ref, *, heads, ts):
    @pl.when(pl.program_id(2) == 0)
    def _():
        st_ref[...] = jnp.zeros_like(st_ref)

    c_len = HG_CHUNK
    lanes = heads * HG_EXPAND
    lb = lb_ref[...]
    for c in range(ts // c_len):
        rows = slice(c * c_len, (c + 1) * c_len)
        f = lb + (1.0 - lb) * jax.nn.sigmoid(f_ref[0, rows, :].astype(F32))
        kk_ref[rows, :] = 1.0 - f
        hi, lo = _bf16_split(jnp.log(f))
        hl_ref[rows, :lanes] = hi
        hl_ref[rows, lanes:] = lo
    both = jnp.dot(sums_ref[...], hl_ref[...], preferred_element_type=F32)
    cum_ref[...] = both[:, :lanes] + both[:, lanes:]

    n_sub = c_len // HG_SUB

    def sub_block_anchor(b):
        mids = [b[i * HG_SUB + HG_SUB // 2:i * HG_SUB + HG_SUB // 2 + 1, :] for i in range(n_sub)]
        return jnp.concatenate([jnp.broadcast_to(m, (HG_SUB, m.shape[1])) for m in mids], axis=0)

    gi = lax.broadcasted_iota(jnp.int32, (heads * n_sub, HG_SUB, c_len), 0) % n_sub
    qi = lax.broadcasted_iota(jnp.int32, (heads * n_sub, HG_SUB, c_len), 1)
    si = lax.broadcasted_iota(jnp.int32, (heads * n_sub, HG_SUB, c_len), 2)
    causal = si <= gi * HG_SUB + qi
    key_row = lax.broadcasted_iota(jnp.int32, (c_len, HG_EXPAND), 0)

    for c in range(ts // c_len):
        rows = slice(c * c_len, (c + 1) * c_len)
        qts, kts, vbs = [], [], []
        for h in range(heads):
            lsl = slice(h * HG_EXPAND, (h + 1) * HG_EXPAND)
            b = cum_ref[rows, lsl]
            anchor = sub_block_anchor(b)
            kk = kk_ref[rows, lsl]
            qt = (q_ref[0, rows, lsl].astype(F32) * jnp.exp(b - anchor)).astype(BF16)
            qts.append(qt.reshape(n_sub, HG_SUB, HG_EXPAND))
            vb = i_ref[0, rows, lsl].astype(BF16)
            for i in range(n_sub):
                hi_r = (i + 1) * HG_SUB
                kt = kk * jnp.exp(anchor[i * HG_SUB:i * HG_SUB + 1, :] - b)
                kts.append(jnp.where(key_row < hi_r, kt, 0.0).astype(BF16))
                vbs.append(vb)
        a = jnp.einsum("gqk,gsk->gqs", jnp.concatenate(qts, axis=0), jnp.stack(kts),
                       preferred_element_type=F32)
        a = jnp.where(causal, a, 0.0).astype(BF16)
        o_intra = jnp.einsum("gqs,gsv->gqv", a, jnp.stack(vbs), preferred_element_type=F32)

        for h in range(heads):
            lsl = slice(h * HG_EXPAND, (h + 1) * HG_EXPAND)
            b = cum_ref[rows, lsl]
            b_last = jnp.broadcast_to(b[c_len - 1:c_len, :], b.shape)
            kk = kk_ref[rows, lsl]
            q = q_ref[0, rows, lsl].astype(F32)
            v = i_ref[0, rows, lsl].astype(F32)
            st = st_ref[h]
            o_inter = lax.dot_general((q * jnp.exp(b)).astype(BF16), st.astype(BF16), NT_DIMS,
                                      preferred_element_type=F32)
            o = o_inter + o_intra[h * n_sub:(h + 1) * n_sub].reshape(c_len, HG_EXPAND)
            kl = kk * jnp.exp(b_last - b)
            upd = jnp.dot(v.T.astype(BF16), kl.astype(BF16), preferred_element_type=F32)
            st_ref[h] = st * jnp.exp(b_last[0:1, :]) + upd
            o = o * lax.rsqrt(jnp.mean(o * o, axis=-1, keepdims=True) + RMS_EPS)
            g = g_ref[0, rows, lsl].astype(F32)
            o = o * gw_ref[:, lsl] * (g * jax.nn.sigmoid(g))
            o_ref[0, rows, lsl] = o.astype(o_ref.dtype)


def _hgrn_cumsum_matrix(ts):
    t = np.arange(ts)[:, None]
    s = np.arange(ts)[None, :]
    return (((t // HG_CHUNK) == (s // HG_CHUNK)) & (s <= t)).astype(np.float32)


def _hgrn2(proj3, lb, gw, att_w, hg_w):
    bsz, seq, _ = proj3.shape
    n_heads = hg_w // HG_EXPAND
    heads = _pick(n_heads, (8, 4, 2, 1))
    lanes = heads * HG_EXPAND
    ts = _pick(seq, (256, 128, 64))
    base = 3 * att_w
    assert base % lanes == 0 and hg_w % lanes == 0

    def seg(k):
        off = (base + k * hg_w) // lanes
        return pl.BlockSpec((1, ts, lanes), lambda b, h, s: (b, s, off + h))

    vec = pl.BlockSpec((1, lanes), lambda b, h, s: (0, h))
    sums = jnp.asarray(_hgrn_cumsum_matrix(ts), BF16)
    return pl.pallas_call(
        functools.partial(_hgrn_kernel, heads=heads, ts=ts),
        grid=(bsz, hg_w // lanes, seq // ts),
        in_specs=[seg(0), seg(1), seg(2), seg(3), vec, vec,
                  pl.BlockSpec((ts, ts), lambda b, h, s: (0, 0))],
        out_specs=pl.BlockSpec((1, ts, lanes), lambda b, h, s: (b, s, h)),
        out_shape=jax.ShapeDtypeStruct((bsz, seq, hg_w), BF16),
        scratch_shapes=[pltpu.VMEM((heads, HG_EXPAND, HG_EXPAND), F32),
                        pltpu.VMEM((ts, 2 * lanes), BF16),
                        pltpu.VMEM((ts, lanes), F32),
                        pltpu.VMEM((ts, lanes), F32)],
        compiler_params=_cparams(("arbitrary", "arbitrary", "arbitrary")),
        name="hgrn2",
    )(proj3, proj3, proj3, proj3, lb.reshape(1, hg_w), gw.reshape(1, hg_w), sums)


def _layer_norm(y, g, b):
    mu = jnp.mean(y, axis=-1, keepdims=True)
    yc = y - mu
    var = jnp.mean(yc * yc, axis=-1, keepdims=True)
    return yc * lax.rsqrt(var + LN_EPS) * g + b


def _bf16_split(x):
    hi = x.astype(BF16)
    lo = (x - hi.astype(F32)).astype(BF16)
    return hi, lo


def _pack_bf16_pair(lo, hi):
    lo_bits = lax.bitcast_convert_type(lo.astype(BF16).astype(F32), jnp.uint32)
    hi_bits = lax.bitcast_convert_type(hi.astype(BF16).astype(F32), jnp.uint32)
    return (lo_bits >> 16) | (hi_bits & jnp.uint32(0xFFFF0000))


def _store_token_rows(ref, tok0, words):
    n, width = words.shape
    nblk = width // V7X_LANES
    for c in range(nblk):
        ref[pl.ds(tok0 * nblk + c, n, stride=nblk), :] = words[:, c * V7X_LANES:(c + 1) * V7X_LANES]


def _load_token_rows(ref, n, nblk, tok0=0):
    return jnp.concatenate([ref[pl.ds(tok0 * nblk + c, n, stride=nblk), :] for c in range(nblk)],
                           axis=1)


def _unpack_bf16_pair(words):
    lo = lax.bitcast_convert_type(words << 16, F32)
    hi = lax.bitcast_convert_type(words & jnp.uint32(0xFFFF0000), F32)
    return lo, hi


def _load_weight_bf16(w_hbm, wbf_ref, stage_ref, sem):
    rows = stage_ref.shape[1]
    n_slab = w_hbm.shape[0] // rows

    def fetch(s, slot):
        return pltpu.make_async_copy(w_hbm.at[pl.ds(s * rows, rows)], stage_ref.at[slot],
                                     sem.at[slot])

    fetch(0, 0).start()
    for s in range(n_slab):
        if s + 1 < n_slab:
            fetch(s + 1, (s + 1) % 2).start()
        fetch(s, s % 2).wait()
        wbf_ref[s * rows:(s + 1) * rows, :] = stage_ref[s % 2].astype(BF16)


def _out_proj_kernel(att_ref, rec_ref, x_ref, wo_hbm, gt_ref, sc_ref, sh_ref, g_ref, b_ref,
                     rw_ref, rb_ref, x1_ref, h2_ref, lg_ref, wobf_ref, rwhi_ref, rwlo_ref,
                     stage_ref, hi_ref, lo_ref, mix_a, mix_b, sem, *, alpha, att_w):
    i = pl.program_id(0)

    @pl.when(i == 0)
    def _():
        _load_weight_bf16(wo_hbm, wobf_ref, stage_ref, sem)
        hi, lo = _bf16_split(rw_ref[...])
        rwhi_ref[...] = hi
        rwlo_ref[...] = lo
        mix_b[...] = jnp.zeros_like(mix_b)

    def step(mix_out, mix_in):
        att = jnp.concatenate([att_ref[0, h] for h in range(att_ref.shape[1])], axis=1)
        mix_out[...] = (jnp.dot(att, wobf_ref[:att_w, :], preferred_element_type=F32)
                        + jnp.dot(rec_ref[...], wobf_ref[att_w:, :], preferred_element_type=F32))
        half = x_ref.shape[1] // 2
        for r in range(x_ref.shape[0] // LN_ROW_GROUP):
            rows = slice(r * LN_ROW_GROUP, (r + 1) * LN_ROW_GROUP)
            y = alpha * x_ref[rows, :] + (1.0 + gt_ref[0]) * mix_in[rows, :]
            x1 = _layer_norm(y, g_ref[...], b_ref[...])
            x1_ref[rows, :] = x1
            h2 = x1 * (1.0 + sc_ref[0]) + sh_ref[0]
            hi, lo = _bf16_split(h2)
            hi_ref[rows, :] = hi
            lo_ref[rows, :] = lo
            _store_token_rows(h2_ref, r * LN_ROW_GROUP,
                              _pack_bf16_pair(h2[:, :half], h2[:, half:]))
        both = jnp.dot(hi_ref[...], jnp.concatenate([rwhi_ref[...], rwlo_ref[...]], axis=1),
                       preferred_element_type=F32)
        lg = (both[:, :V7X_LANES] + both[:, V7X_LANES:]
              + jnp.dot(lo_ref[...], rwhi_ref[...], preferred_element_type=F32))
        lg_ref[...] = lg.T[:lg_ref.shape[0], :] + rb_ref[...]

    @pl.when(i % 2 == 0)
    def _():
        step(mix_a, mix_b)

    @pl.when(i % 2 == 1)
    def _():
        step(mix_b, mix_a)


OUT_PROJ_STAGE_ROWS = 256
LN_ROW_GROUP = 16


def _out_proj(att, rec, x2d, wo, gt, sc, sh, g, b, rw, rb, seq, alpha):
    t, d = x2d.shape
    hp = att.shape[1]
    att_w = hp * V7X_LANES
    n_e = rw.shape[1]
    assert n_e <= V7X_LANES and d % OUT_PROJ_STAGE_ROWS == 0
    rw_pad = jnp.zeros((d, V7X_LANES), F32).at[:, :n_e].set(rw)
    nblk = d // 2 // V7X_LANES
    tm = _pick(seq, (512, 256, 128))
    per_b = seq // tm
    n_tile = t // tm
    mm = lambda i: jnp.minimum(i, n_tile - 1)
    ep = lambda i: jnp.maximum(i - 1, 0)
    vec3 = pl.BlockSpec((1, 1, d), lambda i: (ep(i) // per_b, 0, 0))
    full = lambda shape: pl.BlockSpec(shape, lambda i: (0,) * len(shape))
    return pl.pallas_call(
        functools.partial(_out_proj_kernel, alpha=alpha, att_w=att_w),
        grid=(n_tile + 1,),
        in_specs=[pl.BlockSpec((1, hp, tm, V7X_LANES),
                               lambda i: (mm(i) // per_b, 0, mm(i) % per_b, 0)),
                  pl.BlockSpec((tm, d - att_w), lambda i: (mm(i), 0)),
                  pl.BlockSpec((tm, d), lambda i: (ep(i), 0)),
                  pl.BlockSpec(memory_space=pl.ANY), vec3, vec3, vec3, full((1, d)), full((1, d)),
                  full((d, V7X_LANES)), full((n_e, 1))],
        out_specs=[pl.BlockSpec((tm, d), lambda i: (ep(i), 0)),
                   pl.BlockSpec((tm * nblk, V7X_LANES), lambda i: (ep(i), 0)),
                   pl.BlockSpec((n_e, tm), lambda i: (0, ep(i)))],
        out_shape=[jax.ShapeDtypeStruct((t, d), F32),
                   jax.ShapeDtypeStruct((t * nblk, V7X_LANES), jnp.uint32),
                   jax.ShapeDtypeStruct((n_e, t), F32)],
        scratch_shapes=[pltpu.VMEM((d, d), BF16), pltpu.VMEM((d, V7X_LANES), BF16),
                        pltpu.VMEM((d, V7X_LANES), BF16),
                        pltpu.VMEM((2, OUT_PROJ_STAGE_ROWS, d), F32),
                        pltpu.VMEM((tm, d), BF16), pltpu.VMEM((tm, d), BF16),
                        pltpu.VMEM((tm, d), F32), pltpu.VMEM((tm, d), F32),
                        pltpu.SemaphoreType.DMA((2,))],
        compiler_params=_cparams(("arbitrary",)),
        name="out_proj_ln1",
    )(att, rec, x2d, wo, gt[:, None, :], sc[:, None, :], sh[:, None, :],
      g.reshape(1, d), b.reshape(1, d), rw_pad, rb.reshape(n_e, 1))


def _routing_kernel(lg_ref, tri_ref, low_ref, gate_ref, dest_ref, tile_ref, pad_ref, sel_scr, rk_scr,
                    *, blk, tm):
    n_e, t = lg_ref.shape
    eidx = lax.broadcasted_iota(jnp.int32, (n_e, t), 0)
    cur = lg_ref[...]
    vals, idxs = [], []
    for _ in range(TOP_K):
        m = jnp.max(cur, axis=0, keepdims=True)
        ik = jnp.min(jnp.where(cur == m, eidx, n_e), axis=0, keepdims=True)
        cur = jnp.where(eidx == ik, -jnp.inf, cur)
        vals.append(m)
        idxs.append(ik)
    es = [jnp.exp(v - vals[0]) for v in vals]
    den = functools.reduce(lambda a, b: a + b, es)
    for k in range(TOP_K):
        gate_ref[k:k + 1, :] = es[k] / den
    sel = functools.reduce(lambda a, b: a | b, [eidx == ik for ik in idxs])
    sel_scr[...] = jnp.where(sel, 1.0, 0.0)

    tri = tri_ref[...]
    carry = jnp.zeros((n_e, 1), F32)
    for j in range(t // blk):
        sb = sel_scr[:, j * blk:(j + 1) * blk]
        pre = jnp.dot(sb.astype(BF16), tri, preferred_element_type=F32)
        rk_scr[:, j * blk:(j + 1) * blk] = pre + carry
        carry = carry + jnp.sum(sb, axis=1, keepdims=True)

    counts = jnp.broadcast_to(carry, (n_e, V7X_LANES))
    padded = jnp.floor((counts + (tm - 1)) * (1.0 / tm)) * tm
    pends = jnp.dot(low_ref[...], padded, precision=lax.Precision.HIGHEST,
                    preferred_element_type=F32)
    pstarts = pends - padded
    row0 = rk_scr[...] + pstarts[:, 0:1]
    for k in range(TOP_K):
        dest_ref[k:k + 1, :] = jnp.sum(jnp.where(eidx == idxs[k], row0, 0.0), axis=0,
                                       keepdims=True).astype(jnp.int32)
    starts = (lax.broadcasted_iota(jnp.int32, (n_e, tile_ref.shape[1]), 1) * tm).astype(F32)
    owner = jnp.sum(jnp.where(pends[:, 0:1] <= starts, 1.0, 0.0), axis=0, keepdims=True)
    tile_ref[0:1, :] = jnp.minimum(owner, n_e - 1.0).astype(jnp.int32)
    tile_ref[1:2, :] = jnp.broadcast_to(jnp.max(pends[:, 0:1], axis=0, keepdims=True),
                                        (1, tile_ref.shape[1])).astype(jnp.int32)
    e_tile = lax.broadcasted_iota(jnp.int32, starts.shape, 0).astype(F32)
    row_end = jnp.sum(jnp.where(e_tile == owner, (pstarts + counts)[:, 0:1], 0.0), axis=0,
                      keepdims=True)
    tile_ref[2:3, :] = jnp.clip(row_end - starts[0:1, :], 0.0, float(tm)).astype(jnp.int32)
    pad_ref[0] = (pstarts + counts).astype(jnp.int32)
    pad_ref[1] = (padded - counts).astype(jnp.int32)


def _routing(logits_t, tm, n_tiles):
    n_e, t = logits_t.shape
    assert tm & (tm - 1) == 0
    blk = _pick(t, (256, 128))
    tri = jnp.asarray(np.triu(np.ones((blk, blk), np.float32), 1), BF16)
    low = jnp.asarray(np.tril(np.ones((n_e, n_e), np.float32)))
    ntp = -(-n_tiles // V7X_LANES) * V7X_LANES
    full = lambda shape: pl.BlockSpec(shape, lambda: (0,) * len(shape))
    return pl.pallas_call(
        functools.partial(_routing_kernel, blk=blk, tm=tm),
        in_specs=[full((n_e, t)), full((blk, blk)), full((n_e, n_e))],
        out_specs=[full((TOP_K, t)), full((TOP_K, t)), full((3, ntp)), full((2, n_e, V7X_LANES))],
        out_shape=[jax.ShapeDtypeStruct((TOP_K, t), F32),
                   jax.ShapeDtypeStruct((TOP_K, t), jnp.int32),
                   jax.ShapeDtypeStruct((3, ntp), jnp.int32),
                   jax.ShapeDtypeStruct((2, n_e, V7X_LANES), jnp.int32)],
        scratch_shapes=[pltpu.VMEM((n_e, t), F32), pltpu.VMEM((n_e, t), F32)],
        compiler_params=pltpu.CompilerParams(vmem_limit_bytes=V7X_VMEM_LIMIT),
        name="routing",
    )(logits_t, tri, low)


def _dispatch_kernel(dest_ref, pad_off_ref, pad_n_ref, nrow_ref, h_ref, o_ref, zero_ref, sem, zsem,
                     *, n_tok, tq, tm, n_e, nblk):
    step = pl.program_id(0)
    base = step * tq

    def rows(tok, n=1):
        return pl.ds(pl.multiple_of(tok * nblk, nblk), n * nblk)

    def fill(off, n):
        return pltpu.make_async_copy(zero_ref.at[rows(0, n)], o_ref.at[rows(off, n)], zsem)

    pieces = [1 << s for s in range(tm.bit_length() - 2, -1, -1)]

    def pad_rows(wait):
        def per_expert(e, c):
            off = pad_off_ref[e]
            n = pad_n_ref[e]
            for p in pieces:
                hit = (n & p) != 0

                @pl.when(hit)
                def _(off=off, p=p):
                    cp = fill(off, p)
                    cp.wait() if wait else cp.start()

                off = off + jnp.where(hit, p, 0)
            return c
        lax.fori_loop(0, n_e, per_expert, 0)

        def per_tile(i, c):
            cp = fill(i * tm, tm)
            cp.wait() if wait else cp.start()
            return c
        lax.fori_loop(nrow_ref[0] // tm, o_ref.shape[0] // (tm * nblk), per_tile, 0)

    @pl.when(step == 0)
    def _():
        zero_ref[...] = jnp.zeros_like(zero_ref)
        pad_rows(False)

    def start(r, c):
        for k in range(TOP_K):
            dst = o_ref.at[rows(dest_ref[k * n_tok + base + r])]
            pltpu.make_async_copy(h_ref.at[rows(r)], dst, sem).start(priority=k % 2)
        return c

    lax.fori_loop(0, tq, start, 0, unroll=4)
    for _ in range(TOP_K):
        pltpu.make_async_copy(h_ref, h_ref, sem).wait()

    @pl.when(step == 0)
    def _():
        pad_rows(True)


def _dispatch(h2p, dest_flat, pad_off, pad_n, n_used_rows, n_rows, tm, nblk):
    t = h2p.shape[0] // nblk
    n_e = pad_off.shape[0]
    tq = _pick(t, (256, 128))
    return pl.pallas_call(
        functools.partial(_dispatch_kernel, n_tok=t, tq=tq, tm=tm, n_e=n_e, nblk=nblk),
        grid_spec=pltpu.PrefetchScalarGridSpec(
            num_scalar_prefetch=4, grid=(t // tq,),
            in_specs=[pl.BlockSpec((tq * nblk, V7X_LANES), lambda i, *_: (i, 0))],
            out_specs=pl.BlockSpec(memory_space=pl.ANY),
            scratch_shapes=[pltpu.VMEM((tm * nblk, V7X_LANES), h2p.dtype),
                            pltpu.SemaphoreType.DMA(()), pltpu.SemaphoreType.DMA(())]),
        out_shape=jax.ShapeDtypeStruct((n_rows * nblk, V7X_LANES), h2p.dtype),
        compiler_params=pltpu.CompilerParams(dimension_semantics=("arbitrary",),
                                             has_side_effects=True,
                                             vmem_limit_bytes=V7X_VMEM_LIMIT),
        name="moe_dispatch",
    )(dest_flat, pad_off, pad_n, n_used_rows, h2p)


EXPERT_COL_BLOCK = 2048
GEMM1_COL_CHUNK = 256
MOE_TILE_LEVELS = 3
MOE_MIN_ROWS = 16
CAST_ROWS = 256


def _stream_expert_weights(te_ref, nt_ref, w_hbm, stage_ref, slot_ref, sem):
    j, i = pl.program_id(0), pl.program_id(1)
    n_pass, n_tiles = pl.num_programs(0), pl.num_programs(1)
    tn = stage_ref.shape[2]
    n_used = nt_ref[0]
    cur = te_ref[i]
    first = (i < n_used) & ((i == 0) | (cur != te_ref[jnp.maximum(i - 1, 0)]))

    def fetch(e, jj, slot):
        src = w_hbm.at[e, :, pl.ds(pl.multiple_of(jj * tn, tn), tn)]
        return pltpu.make_async_copy(src, stage_ref.at[slot], sem.at[slot])

    @pl.when((i == 0) & (j == 0))
    def _():
        slot_ref[0] = 1
        fetch(te_ref[0], 0, 0).start()

    @pl.when(first)
    def _():
        slot = 1 - slot_ref[0]
        slot_ref[0] = slot
        fetch(cur, j, slot).wait()

        def same_group(k):
            return (k < n_used) & (te_ref[jnp.minimum(k, n_tiles - 1)] == cur)
        nxt = lax.while_loop(same_group, lambda k: k + 1, i + 1)

        @pl.when(nxt < n_used)
        def _():
            fetch(te_ref[jnp.minimum(nxt, n_tiles - 1)], j, 1 - slot).start()

        @pl.when((nxt >= n_used) & (j + 1 < n_pass))
        def _():
            fetch(te_ref[0], j + 1, 1 - slot).start()

    return first, slot_ref[0]


def _cast_block(stage_ref, slot, wbf_ref):
    def cast(r, c):
        rows = pl.ds(pl.multiple_of(r * CAST_ROWS, CAST_ROWS), CAST_ROWS)
        wbf_ref[rows, :] = stage_ref[slot, rows, :].astype(BF16)
        return c
    lax.fori_loop(0, wbf_ref.shape[0] // CAST_ROWS, cast, 0)


def _gemm1_kernel(te_ref, nt_ref, tv_ref, x_ref, w_hbm, b_ref, o_ref, stage_ref, wbf_ref,
                  slot_ref, sem, *, nc, nblk):
    first, slot = _stream_expert_weights(te_ref, nt_ref, w_hbm, stage_ref, slot_ref, sem)

    def compute(rows, convert):
        words = _load_token_rows(x_ref, rows, nblk)
        x_lo, x_hi = (v.astype(BF16) for v in _unpack_bf16_pair(words))
        half = x_lo.shape[1]
        lane128 = lax.broadcasted_iota(jnp.int32, (rows, V7X_LANES), 1)
        even_idx = (2 * lane128) % V7X_LANES
        for n0 in range(0, wbf_ref.shape[1], nc):
            if convert:
                wbf_ref[:, n0:n0 + nc] = stage_ref[slot, :, n0:n0 + nc].astype(BF16)
            hb = (jnp.dot(x_lo, wbf_ref[:half, n0:n0 + nc], preferred_element_type=F32)
                  + jnp.dot(x_hi, wbf_ref[half:, n0:n0 + nc], preferred_element_type=F32)
                  + b_ref[0, :, n0:n0 + nc])
            nxt = pltpu.roll(hb, nc - 1, axis=1)
            glu = jnp.minimum(hb, SWIGLU_LIMIT)
            lin = jnp.clip(nxt, -SWIGLU_LIMIT, SWIGLU_LIMIT)
            act = glu * jax.nn.sigmoid(SWIGLU_ALPHA * glu) * (lin + 1.0)
            for c in range(nc // (2 * V7X_LANES)):
                lo_blk = act[:, (2 * c) * V7X_LANES:(2 * c + 1) * V7X_LANES]
                hi_blk = act[:, (2 * c + 1) * V7X_LANES:(2 * c + 2) * V7X_LANES]
                out = jnp.where(lane128 < V7X_LANES // 2,
                                jnp.take_along_axis(lo_blk, even_idx, axis=1),
                                jnp.take_along_axis(hi_blk, even_idx, axis=1))
                o0 = n0 // 2 + c * V7X_LANES
                o_ref[:rows, o0:o0 + V7X_LANES] = out.astype(o_ref.dtype)
        if rows < o_ref.shape[0]:
            o_ref[rows:, :] = jnp.zeros((o_ref.shape[0] - rows, o_ref.shape[1]), o_ref.dtype)

    _for_real_rows(pl.program_id(1), nt_ref, tv_ref, o_ref, o_ref.shape[0], compute,
                   first, lambda: _cast_block(stage_ref, slot, wbf_ref))


def _for_real_rows(i, nt_ref, tv_ref, o_ref, tm, compute, first, convert_all):
    used = i < nt_ref[0]
    real = tv_ref[i]
    sizes = [tm >> s for s in range(MOE_TILE_LEVELS) if (tm >> s) % MOE_MIN_ROWS == 0]

    @pl.when(first & (real <= sizes[1]))
    def _():
        convert_all()

    for n, rows in enumerate(sizes):
        fits = real <= rows
        if n + 1 < len(sizes):
            fits = fits & (real > sizes[n + 1])
        if n == 0:
            @pl.when(used & fits & first)
            def _():
                compute(rows, True)
            fits = fits & jnp.logical_not(first)

        @pl.when(used & fits)
        def _(rows=rows):
            compute(rows, False)

    @pl.when(jnp.logical_not(used))
    def _():
        o_ref[...] = jnp.zeros_like(o_ref)


def _gemm1(xin, w1, b1, tile_e, n_tiles_used, tile_rows, tm, nblk):
    n_rows = xin.shape[0] // nblk
    n_e, d, f2 = w1.shape
    tn = min(EXPERT_COL_BLOCK, f2)
    nc = min(GEMM1_COL_CHUNK, tn)
    assert f2 % tn == 0 and d % CAST_ROWS == 0
    n_tiles = n_rows // tm
    assert nc % (2 * V7X_LANES) == 0

    def used(i, nt):
        return jnp.minimum(i, nt[0] - 1)

    return pl.pallas_call(
        functools.partial(_gemm1_kernel, nc=nc, nblk=nblk),
        grid_spec=pltpu.PrefetchScalarGridSpec(
            num_scalar_prefetch=3, grid=(f2 // tn, n_tiles),
            in_specs=[pl.BlockSpec((tm * nblk, V7X_LANES),
                                   lambda j, i, te, nt, tv: (used(i, nt), 0)),
                      pl.BlockSpec(memory_space=pl.ANY),
                      pl.BlockSpec((1, 1, tn), lambda j, i, te, nt, tv: (te[used(i, nt)], 0, j))],
            out_specs=pl.BlockSpec((tm, tn // 2), lambda j, i, te, nt, tv: (i, j)),
            scratch_shapes=[pltpu.VMEM((2, d, tn), F32), pltpu.VMEM((d, tn), BF16),
                            pltpu.SMEM((1,), jnp.int32), pltpu.SemaphoreType.DMA((2,))]),
        out_shape=jax.ShapeDtypeStruct((n_rows, f2 // 2), BF16),
        compiler_params=_cparams(("arbitrary", "arbitrary")),
        name="moe_gemm1",
    )(tile_e, n_tiles_used, tile_rows, xin, w1, b1.reshape(n_e, 1, f2))


def _gemm2_kernel(te_ref, nt_ref, tv_ref, a_ref, w_hbm, b_ref, o_ref, stage_ref, wbf_ref, slot_ref,
                  sem, *, nc, nblk):
    first, slot = _stream_expert_weights(te_ref, nt_ref, w_hbm, stage_ref, slot_ref, sem)
    tm = a_ref.shape[0]

    def compute(rows, convert):
        a = a_ref[:rows, :]
        half = wbf_ref.shape[1] // 2

        def cols(n0):
            if convert:
                wbf_ref[:, n0:n0 + nc] = stage_ref[slot, :, n0:n0 + nc].astype(BF16)
            return (jnp.dot(a, wbf_ref[:, n0:n0 + nc], preferred_element_type=F32)
                    + b_ref[0, :, n0:n0 + nc])

        for n0 in range(0, half, nc):
            words = _pack_bf16_pair(cols(n0), cols(half + n0))
            for c in range(nc // V7X_LANES):
                blk = n0 // V7X_LANES + c
                o_ref[pl.ds(blk, rows, stride=nblk), :] = words[:, c * V7X_LANES:(c + 1) * V7X_LANES]
        if rows < tm:
            o_ref[rows * nblk:, :] = jnp.zeros(((tm - rows) * nblk, o_ref.shape[1]), o_ref.dtype)

    _for_real_rows(pl.program_id(1), nt_ref, tv_ref, o_ref, tm, compute,
                   first, lambda: _cast_block(stage_ref, slot, wbf_ref))


def _gemm2(act, w2, b2, tile_e, n_tiles_used, tile_rows, tm):
    n_rows, f = act.shape
    n_e, _, d = w2.shape
    tn = d
    nc = min(512, tn // 2)
    nblk = d // 2 // V7X_LANES
    assert f % CAST_ROWS == 0
    n_tiles = n_rows // tm

    def used(i, nt):
        return jnp.minimum(i, nt[0] - 1)

    return pl.pallas_call(
        functools.partial(_gemm2_kernel, nc=nc, nblk=nblk),
        grid_spec=pltpu.PrefetchScalarGridSpec(
            num_scalar_prefetch=3, grid=(d // tn, n_tiles),
            in_specs=[pl.BlockSpec((tm, f), lambda j, i, te, nt, tv: (used(i, nt), 0)),
                      pl.BlockSpec(memory_space=pl.ANY),
                      pl.BlockSpec((1, 1, tn), lambda j, i, te, nt, tv: (te[used(i, nt)], 0, j))],
            out_specs=pl.BlockSpec((tm * nblk, V7X_LANES), lambda j, i, te, nt, tv: (i, 0)),
            scratch_shapes=[pltpu.VMEM((2, f, tn), F32), pltpu.VMEM((f, tn), BF16),
                            pltpu.SMEM((1,), jnp.int32), pltpu.SemaphoreType.DMA((2,))]),
        out_shape=jax.ShapeDtypeStruct((n_rows * nblk, V7X_LANES), jnp.uint32),
        compiler_params=_cparams(("arbitrary", "arbitrary")),
        name="moe_gemm2",
    )(tile_e, n_tiles_used, tile_rows, act, w2, b2.reshape(n_e, 1, d))


COMBINE_ROW_GROUP = 64


def _combine_kernel(dest_ref, x1_ref, gate_ref, gt_ref, g_ref, b_ref, y_ref, o_ref, buf, sem,
                    *, alpha, tc, n_tok, nblk):
    i = pl.program_id(0)
    n_steps = pl.num_programs(0)

    def rows(tok):
        return pl.ds(pl.multiple_of(tok * nblk, nblk), nblk)

    def copy(step, slot, k, r):
        src = y_ref.at[rows(dest_ref[k * n_tok + step * tc + r])]
        return pltpu.make_async_copy(src, buf.at[slot, k, rows(r)], sem.at[slot])

    def issue(step, slot):
        def body(r, c):
            for k in range(TOP_K):
                copy(step, slot, k, r).start(priority=k % 2)
            return c
        lax.fori_loop(0, tc, body, 0, unroll=2)

    def drain(slot):
        pltpu.make_async_copy(buf.at[slot], buf.at[slot], sem.at[slot]).wait()

    @pl.when(i == 0)
    def _():
        issue(0, 0)

    def step(slot, prefetch):
        drain(slot)
        for rg in range(tc // COMBINE_ROW_GROUP):
            r0 = rg * COMBINE_ROW_GROUP
            if prefetch:
                for r in range(r0, r0 + COMBINE_ROW_GROUP):
                    for k in range(TOP_K):
                        copy(i + 1, 1 - slot, k, r).start(priority=k % 2)
            rows = slice(r0, r0 + COMBINE_ROW_GROUP)
            gates = gate_ref[rows, :]
            ff_lo = ff_hi = None
            for k in range(TOP_K):
                lo, hi = _unpack_bf16_pair(
                    _load_token_rows(buf.at[slot, k], COMBINE_ROW_GROUP, nblk, tok0=r0))
                gk = gates[:, k:k + 1]
                ff_lo = gk * lo if ff_lo is None else ff_lo + gk * lo
                ff_hi = gk * hi if ff_hi is None else ff_hi + gk * hi
            ff = jnp.concatenate([ff_lo, ff_hi], axis=1)
            y = alpha * x1_ref[rows, :] + (1.0 + gt_ref[0]) * ff
            o_ref[rows, :] = _layer_norm(y, g_ref[...], b_ref[...])

    for slot in range(2):
        for prefetch in (True, False):
            @pl.when((i % 2 == slot) & ((i + 1 < n_steps) == prefetch))
            def _(slot=slot, prefetch=prefetch):
                step(slot, prefetch)


def _combine(dest_flat, x1, gates_tk, gt, g, b, y, seq, alpha):
    t, d = x1.shape
    tc = _pick(seq, (256, 128))
    per_b = seq // tc
    nblk = d // 2 // V7X_LANES
    return pl.pallas_call(
        functools.partial(_combine_kernel, alpha=alpha, tc=tc, n_tok=t, nblk=nblk),
        grid_spec=pltpu.PrefetchScalarGridSpec(
            num_scalar_prefetch=1, grid=(t // tc,),
            in_specs=[pl.BlockSpec((tc, d), lambda i, ds: (i, 0)),
                      pl.BlockSpec((tc, TOP_K), lambda i, ds: (i, 0)),
                      pl.BlockSpec((1, 1, d), lambda i, ds: (i // per_b, 0, 0)),
                      pl.BlockSpec((1, d), lambda i, ds: (0, 0)),
                      pl.BlockSpec((1, d), lambda i, ds: (0, 0)),
                      pl.BlockSpec(memory_space=pl.ANY)],
            out_specs=pl.BlockSpec((tc, d), lambda i, ds: (i, 0)),
            scratch_shapes=[pltpu.VMEM((2, TOP_K, tc * nblk, V7X_LANES), jnp.uint32),
                            pltpu.SemaphoreType.DMA((2,))]),
        out_shape=jax.ShapeDtypeStruct((t, d), F32),
        compiler_params=_cparams(("arbitrary",)),
        name="moe_combine_ln2",
    )(dest_flat, x1, gates_tk, gt[:, None, :], g.reshape(1, d), b.reshape(1, d), y)


MOE_ROW_TILE = 512


def _moe_ffn(h2p, logits_t, x1, gt_f, ln_g, ln_b, w1, b1, w2, b2, seq, alpha):
    tm = MOE_ROW_TILE
    n_e, t = logits_t.shape
    n_rows = -(-(TOP_K * t) // tm) * tm + n_e * tm
    n_tiles = n_rows // tm
    gate_t, dest, tiles, pads = _routing(logits_t, tm, n_tiles)
    dest = dest.reshape(-1)
    tile_e = tiles[0, :n_tiles]
    n_used_rows = tiles[1, :1]
    tile_rows = tiles[2, :n_tiles]
    n_used_tiles = n_used_rows // tm
    nblk = x1.shape[1] // 2 // V7X_LANES
    xin = _dispatch(h2p, dest, pads[0, :, 0], pads[1, :, 0], n_used_rows, n_rows, tm, nblk)
    act = _gemm1(xin, w1, b1, tile_e, n_used_tiles, tile_rows, tm, nblk)
    y = _gemm2(act, w2, b2, tile_e, n_used_tiles, tile_rows, tm)
    return _combine(dest, x1, gate_t.T, gt_f, ln_g, ln_b, y, seq, alpha)


def kernel(x, c, positions, w_ada, b_ada, w_in, hgrn_lb, gnorm_w, w_o, ln1_g, ln1_b,
           router_w, router_b, w1, b1, w2, b2, ln2_g, ln2_b):
    bsz, seq, d = x.shape
    depth = w_ada.shape[0]
    t = bsz * seq
    att_w = d // 2
    hg_w = d - att_w
    alpha = (2.0 * depth) ** 0.25

    lb_all = jnp.cumsum(jax.nn.softmax(hgrn_lb.astype(F32), axis=0), axis=0)
    inv = ROPE_THETA ** (-(jnp.arange(0, ROT_DIM, 2, dtype=F32) / ROT_DIM))
    lane = np.arange(V7X_LANES)
    inv_lane = inv[(lane % ATT_HEAD_DIM) % (ROT_DIM // 2)].reshape(1, V7X_LANES)
    pos_b = jnp.broadcast_to(positions.astype(F32).reshape(t, 1), (t, V7X_LANES))

    x2d = x.reshape(t, d)
    for l in range(depth):
        mod = _adaln(c, w_ada[l], b_ada[l])
        sh_a, sc_a, gt_a, sh_f, sc_f, gt_f = jnp.split(mod, 6, axis=-1)

        proj = _in_proj(x2d, sc_a, sh_a, w_in[l], seq)
        proj3 = proj.reshape(bsz, seq, proj.shape[1])
        q_hp, k_hp, v_hp = _qk_rope(proj, pos_b, inv_lane, bsz, seq, att_w)
        att = _attention(q_hp, k_hp, v_hp)
        rec = _hgrn2(proj3, lb_all[l], gnorm_w[l], att_w, hg_w).reshape(t, hg_w)

        x1, h2p, logits_t = _out_proj(att, rec, x2d, w_o[l], gt_a, sc_f, sh_f, ln1_g[l], ln1_b[l],
                                      router_w[l], router_b[l], seq, alpha)

        x2d = _moe_ffn(h2p, logits_t, x1, gt_f, ln2_g[l], ln2_b[l], w1[l], b1[l], w2[l], b2[l],
                       seq, alpha)
    return x2d.reshape(bsz, seq, d)
```
